```python
import jax, jax.numpy as jnp
from jax import lax
import numpy as np

D_MODEL = 1024
BATCH = 8
SEQ = 2048
DEPTH = 1
DEC_BATCH = 128
DEC_SEQ = 4
PAST_LEN = 16384
PAGE_SIZE = 128

LRU_WIDTH = D_MODEL // 2
LRU_BLOCKS = 8
LRU_BLOCK = LRU_WIDTH // LRU_BLOCKS
CONV_WIDTH = 4
LRU_C = 8.0
HG_WIDTH = D_MODEL - LRU_WIDTH
HG_HEAD_DIM = 128
HG_HEADS = HG_WIDTH // HG_HEAD_DIM
HG_CHUNK = 64
IN_COLS = 2 * LRU_WIDTH + 4 * HG_WIDTH
SPLITS = (LRU_WIDTH, 2 * LRU_WIDTH, 2 * LRU_WIDTH + HG_WIDTH, 2 * LRU_WIDTH + 2 * HG_WIDTH, 2 * LRU_WIDTH + 3 * HG_WIDTH)
N_EXPERTS = 32
TOP_K = 4
D_EXPERT = D_MODEL
SWIGLU_LIMIT = 7.0
SWIGLU_ALPHA = 1.702
MOE_BLOCK = 256
RMS_EPS = 1e-6

kernel_name = 'hymba_rglru_hgrn2_moe_step'

F32 = jnp.float32


def _rmsnorm(x, g):
    xf = x.astype(F32)
    y = xf * lax.rsqrt(jnp.mean(xf * xf, axis=-1, keepdims=True) + RMS_EPS)
    return (y * g.astype(F32)).astype(x.dtype)


def _causal_conv(xb, buf, w, b):
    L = xb.shape[1]
    full = jnp.concatenate([buf.astype(xb.dtype), xb], axis=1)
    y = b
    for k in range(CONV_WIDTH):
        y = y + full[:, k:k + L] * w[k]
    return y, full[:, L:]


def _lin_combine(left, right):
    a1, b1 = left
    a2, b2 = right
    return a1 * a2, a2 * b1 + b2


def _rglru(xc, lam, w_a, b_a, w_i, b_i, h0, reset_first):
    B, L, W = xc.shape
    xf = xc.astype(F32)
    xr = xf.reshape(B, L, LRU_BLOCKS, LRU_BLOCK)
    r = jax.nn.sigmoid(jnp.einsum('blnd,nde->blne', xr, w_a.astype(F32)).reshape(B, L, W) + b_a.astype(F32))
    i = jax.nn.sigmoid(jnp.einsum('blnd,nde->blne', xr, w_i.astype(F32)).reshape(B, L, W) + b_i.astype(F32))
    log_a = -LRU_C * r * jax.nn.softplus(-lam.astype(F32))
    a = jnp.exp(log_a)
    mult = jnp.sqrt(-jnp.expm1(2.0 * log_a))
    if reset_first:
        mult = mult.at[:, 0].set(1.0)
    u = mult * i * xf
    a_cum, h = lax.associative_scan(_lin_combine, (a, u), axis=1)
    h = h + a_cum * h0.astype(F32)[:, None]
    return h, h[:, -1]


def _hgrn2(q, f_logit, v, lb, S0):
    B, L, H, DK = q.shape
    q = q.astype(F32)
    v = v.astype(F32)
    f = lb + (1.0 - lb) * jax.nn.sigmoid(f_logit.astype(F32))
    g = jnp.log(f)
    k = 1.0 - f
    C = HG_CHUNK if L % HG_CHUNK == 0 else L
    n = L // C

    def to_chunks(t):
        return t.reshape(B, n, C, H, t.shape[-1]).transpose(1, 0, 3, 2, 4)

    mask = jnp.tril(jnp.ones((C, C), dtype=bool))[:, :, None]

    def step(S, inp):
        qc, kc, vc, gc = inp
        b = jnp.cumsum(gc, axis=2)
        decay = jnp.exp(jnp.where(mask, b[:, :, :, None, :] - b[:, :, None, :, :], -jnp.inf))
        att = jnp.einsum('bhtd,bhsd,bhtsd->bhts', qc, kc, decay)
        b_last = b[:, :, -1]
        o = jnp.einsum('bhts,bhsv->bhtv', att, vc) + jnp.einsum('bhtd,bhdv->bhtv', qc * jnp.exp(b), S)
        S_new = jnp.exp(b_last)[..., None] * S + jnp.einsum('bhsd,bhsv->bhdv', kc * jnp.exp(b_last[:, :, None] - b), vc)
        return S_new, o

    S_fin, o = lax.scan(step, S0.astype(F32), (to_chunks(q), to_chunks(k), to_chunks(v), to_chunks(g)))
    o = o.transpose(1, 0, 3, 2, 4).reshape(B, L, H, v.shape[-1])
    return o, S_fin


def _moe(h, router_w, router_b, w_gu, b_gu, w_down, b_down):
    B, L, D = h.shape
    T = B * L
    xt = h.reshape(T, D)
    logits = xt.astype(F32) @ router_w.astype(F32) + router_b.astype(F32)
    top_val, top_idx = lax.top_k(logits, TOP_K)
    gates = jax.nn.softmax(top_val, axis=-1)
    N = T * TOP_K
    flat_e = top_idx.reshape(N).astype(jnp.int32)
    flat_tok = jnp.repeat(jnp.arange(T, dtype=jnp.int32), TOP_K)
    flat_g = gates.reshape(N)
    order = jnp.argsort(flat_e)
    se = flat_e[order]
    counts = jnp.bincount(flat_e, length=N_EXPERTS).astype(jnp.int32)
    starts = jnp.cumsum(counts) - counts
    padded = (counts + MOE_BLOCK - 1) // MOE_BLOCK * MOE_BLOCK
    pad_ends = jnp.cumsum(padded)
    pad_starts = pad_ends - padded
    dest = pad_starts[se] + jnp.arange(N, dtype=jnp.int32) - starts[se]
    n_blocks = -(-N // MOE_BLOCK) + N_EXPERTS
    P = n_blocks * MOE_BLOCK
    row_tok = jnp.zeros((P,), jnp.int32).at[dest].set(flat_tok[order])
    row_gate = jnp.zeros((P,), F32).at[dest].set(flat_g[order])
    block_e = jnp.minimum(jnp.searchsorted(pad_ends, jnp.arange(n_blocks, dtype=jnp.int32) * MOE_BLOCK, side='right'), N_EXPERTS - 1)
    xs = xt[row_tok].reshape(n_blocks, MOE_BLOCK, D)

    def expert_block(args):
        xb, e = args
        hu = xb @ w_gu[e] + b_gu[e]
        glu = jnp.minimum(hu[:, :D_EXPERT], SWIGLU_LIMIT)
        lin = jnp.clip(hu[:, D_EXPERT:], -SWIGLU_LIMIT, SWIGLU_LIMIT)
        act = glu * jax.nn.sigmoid(SWIGLU_ALPHA * glu) * (lin + 1.0)
        return act @ w_down[e] + b_down[e]

    ys = lax.map(expert_block, (xs, block_e)).reshape(P, D)
    y = jax.ops.segment_sum(ys * row_gate[:, None].astype(ys.dtype), row_tok, num_segments=T)
    return y.reshape(B, L, D)


def _layer(x, c, conv_st, h_st, S_st, lb, reset_first, ada_w, ada_b, n_pre_m, n_post_m, n_pre_f, n_post_f,
           w_in, conv_w, conv_b, lru_wa, lru_ba, lru_wi, lru_bi, lru_lambda, hg_norm, w_out,
           router_w, router_b, w_gu, b_gu, w_down, b_down):
    B, L, _ = x.shape
    mod = jax.nn.silu(c) @ ada_w + ada_b
    sh_m, sc_m, gt_m, sh_f, sc_f, gt_f = [m[:, None, :] for m in jnp.split(mod, 6, axis=-1)]
    hn = _rmsnorm(x, n_pre_m) * (1.0 + sc_m) + sh_m
    proj = hn @ w_in
    xl, yl, qh, fh, ih, gh = jnp.split(proj, SPLITS, axis=-1)
    xc, conv_new = _causal_conv(xl, conv_st, conv_w, conv_b)
    hl, h_new = _rglru(xc, lru_lambda, lru_wa, lru_ba, lru_wi, lru_bi, h_st, reset_first)
    out_l = hl.astype(x.dtype) * jax.nn.gelu(yl)

    def heads(t):
        return t.reshape(B, L, HG_HEADS, HG_HEAD_DIM)

    oh, S_new = _hgrn2(heads(qh), heads(fh), heads(ih), lb.reshape(HG_HEADS, HG_HEAD_DIM), S_st)
    out_h = (_rmsnorm(oh, hg_norm) * jax.nn.sigmoid(heads(gh).astype(F32))).astype(x.dtype).reshape(B, L, HG_WIDTH)
    mix = jnp.concatenate([out_l, out_h], axis=-1) @ w_out
    x = x + gt_m * _rmsnorm(mix, n_post_m)
    hf = _rmsnorm(x, n_pre_f) * (1.0 + sc_f) + sh_f
    x = x + gt_f * _rmsnorm(_moe(hf, router_w, router_b, w_gu, b_gu, w_down, b_down), n_post_f)
    return x, conv_new.astype(x.dtype), h_new.astype(x.dtype), S_new.astype(x.dtype)


def _run_group(x, c, conv_st, h_st, S_st, reset_first, lbs, weights):
    convs, hs, Ss = [], [], []
    for l in range(DEPTH):
        x, cn, hn, Sn = _layer(x, c, conv_st[l], h_st[l], S_st[l], lbs[l], reset_first, *[w[l] for w in weights])
        convs.append(cn)
        hs.append(hn)
        Ss.append(Sn)
    return x, jnp.stack(convs), jnp.stack(hs), jnp.stack(Ss)


def setup_inputs(seed: int = 0) -> dict:
    key = jax.random.key(seed)
    ks = jax.random.split(key, 32)

    def nrm(k, shape, scale):
        return jax.random.normal(k, shape, F32) * scale

    a0 = jax.random.uniform(ks[16], (DEPTH, LRU_WIDTH), F32, 0.9, 0.999)
    return {
        'x_prompt': nrm(ks[0], (BATCH, SEQ, D_MODEL), 1.0),
        'x_sample': nrm(ks[1], (DEC_BATCH, DEC_SEQ, D_MODEL), 1.0),
        'state_rglru_conv': nrm(ks[2], (DEPTH, DEC_BATCH, CONV_WIDTH - 1, LRU_WIDTH), 1.0),
        'state_rglru_h': nrm(ks[3], (DEPTH, DEC_BATCH, LRU_WIDTH), 0.5),
        'state_hgrn_S': nrm(ks[4], (DEPTH, DEC_BATCH, HG_HEADS, HG_HEAD_DIM, HG_HEAD_DIM), 0.5),
        'c_prompt': nrm(ks[5], (BATCH, D_MODEL), 1.0),
        'c_sample': nrm(ks[6], (DEC_BATCH, D_MODEL), 1.0),
        'ada_w': nrm(ks[7], (DEPTH, D_MODEL, 6 * D_MODEL), 0.5 * D_MODEL ** -0.5),
        'ada_b': nrm(ks[8], (DEPTH, 6 * D_MODEL), 0.02),
        'norm_pre_mix': 1.0 + nrm(ks[9], (DEPTH, D_MODEL), 0.05),
        'norm_post_mix': 1.0 + nrm(ks[10], (DEPTH, D_MODEL), 0.05),
        'norm_pre_ffn': 1.0 + nrm(ks[11], (DEPTH, D_MODEL), 0.05),
        'norm_post_ffn': 1.0 + nrm(ks[12], (DEPTH, D_MODEL), 0.05),
        'w_in': nrm(ks[13], (DEPTH, D_MODEL, IN_COLS), D_MODEL ** -0.5),
        'conv_w': nrm(ks[14], (DEPTH, CONV_WIDTH, LRU_WIDTH), 0.5),
        'conv_b': nrm(ks[15], (DEPTH, LRU_WIDTH), 0.02),
        'lru_wa': nrm(ks[17], (DEPTH, LRU_BLOCKS, LRU_BLOCK, LRU_BLOCK), LRU_BLOCK ** -0.5),
        'lru_ba': nrm(ks[18], (DEPTH, LRU_WIDTH), 0.02),
        'lru_wi': nrm(ks[19], (DEPTH, LRU_BLOCKS, LRU_BLOCK, LRU_BLOCK), LRU_BLOCK ** -0.5),
        'lru_bi': nrm(ks[20], (DEPTH, LRU_WIDTH), 0.02),
        'lru_lambda': jnp.log(a0) - jnp.log1p(-a0),
        'hg_lb': nrm(ks[21], (DEPTH + 1, HG_WIDTH), 0.5),
        'hg_norm': 1.0 + nrm(ks[22], (DEPTH, HG_HEAD_DIM), 0.05),
        'w_out': nrm(ks[23], (DEPTH, D_MODEL, D_MODEL), D_MODEL ** -0.5),
        'router_w': nrm(ks[24], (DEPTH, D_MODEL, N_EXPERTS), D_MODEL ** -0.5),
        'router_b': nrm(ks[25], (DEPTH, N_EXPERTS), 0.01),
        'w_gu': nrm(ks[26], (DEPTH, N_EXPERTS, D_MODEL, 2 * D_EXPERT), D_MODEL ** -0.5),
        'b_gu': nrm(ks[27], (DEPTH, N_EXPERTS, 2 * D_EXPERT), 0.01),
        'w_down': nrm(ks[28], (DEPTH, N_EXPERTS, D_EXPERT, D_MODEL), D_EXPERT ** -0.5),
        'b_down': nrm(ks[29], (DEPTH, N_EXPERTS, D_MODEL), 0.01),
    }


def reference(x_prompt, x_sample, state_rglru_conv, state_rglru_h, state_hgrn_S, c_prompt, c_sample,
              ada_w, ada_b, norm_pre_mix, norm_post_mix, norm_pre_ffn, norm_post_ffn,
              w_in, conv_w, conv_b, lru_wa, lru_ba, lru_wi, lru_bi, lru_lambda, hg_lb, hg_norm, w_out,
              router_w, router_b, w_gu, b_gu, w_down, b_down):
    weights = (ada_w, ada_b, norm_pre_mix, norm_post_mix, norm_pre_ffn, norm_post_ffn,
               w_in, conv_w, conv_b, lru_wa, lru_ba, lru_wi, lru_bi, lru_lambda, hg_norm, w_out,
               router_w, router_b, w_gu, b_gu, w_down, b_down)
    lbs = jnp.cumsum(jax.nn.softmax(hg_lb.astype(F32), axis=0), axis=0)[:DEPTH]
    dt = x_prompt.dtype
    conv0 = jnp.zeros((DEPTH, BATCH, CONV_WIDTH - 1, LRU_WIDTH), dt)
    h0 = jnp.zeros((DEPTH, BATCH, LRU_WIDTH), dt)
    S0 = jnp.zeros((DEPTH, BATCH, HG_HEADS, HG_HEAD_DIM, HG_HEAD_DIM), dt)
    y_prompt, conv_p, h_p, S_p = _run_group(x_prompt, c_prompt, conv0, h0, S0, True, lbs, weights)
    y_sample, conv_s, h_s, S_s = _run_group(x_sample, c_sample, state_rglru_conv, state_rglru_h, state_hgrn_S, False, lbs, weights)
    return (y_prompt, y_sample, conv_p, h_p, S_p, conv_s, h_s, S_s)
```

```python
import functools
import math

import jax
import jax.numpy as jnp
from jax import lax
from jax.experimental import pallas as pl
from jax.experimental.pallas import tpu as pltpu

F32 = jnp.float32
BF16 = jnp.bfloat16
I32 = jnp.int32

D_MODEL = 1024
LRU_WIDTH = 512
LRU_BLOCKS = 8
LRU_BLOCK = LRU_WIDTH // LRU_BLOCKS
CONV_WIDTH = 4
LRU_C = 8.0
HG_WIDTH = 512
HG_HEAD_DIM = 128
HG_HEADS = 4
IN_COLS = 2 * LRU_WIDTH + 4 * HG_WIDTH
N_EXPERTS = 32
TOP_K = 4
D_EXPERT = 1024
SWIGLU_LIMIT = 7.0
SWIGLU_ALPHA = 1.702
RMS_EPS = 1e-6

C_XL, C_YL, C_Q, C_F, C_V, C_G = 0, 512, 1024, 1536, 2048, 2560

SUBLANES = 8
CONV_HDR = SUBLANES
HG_CHUNK = 64
MOE_BM = 256
VMEM_LIMIT = 56 * 1024 * 1024

_NT = (((1,), (1,)), ((), ()))
_TN = (((0,), (0,)), ((), ()))


def _cparams(sem):
    return pltpu.CompilerParams(dimension_semantics=sem, vmem_limit_bytes=VMEM_LIMIT)


def _rms(x, g):
    return x * lax.rsqrt(jnp.mean(x * x, axis=-1, keepdims=True) + RMS_EPS) * g


def _gelu_tanh(x):
    c = math.sqrt(2.0 / math.pi)
    return 0.5 * x * (1.0 + jnp.tanh(c * (x + 0.044715 * (x * x * x))))


def _ada_kernel(c_ref, w_ref, b_ref, o_ref):
    c = c_ref[...]
    s = c * jax.nn.sigmoid(c)
    o_ref[...] = jnp.dot(s.astype(BF16), w_ref[...].astype(BF16), preferred_element_type=F32) + b_ref[...]


def _ada(c_all, ada_w, ada_b):
    n = c_all.shape[0]
    tn = 1024
    return pl.pallas_call(
        _ada_kernel,
        out_shape=jax.ShapeDtypeStruct((n, 6 * D_MODEL), F32),
        grid=(6 * D_MODEL // tn,),
        in_specs=[pl.BlockSpec((n, D_MODEL), lambda j: (0, 0)),
                  pl.BlockSpec((D_MODEL, tn), lambda j: (0, j)),
                  pl.BlockSpec((1, tn), lambda j: (0, j))],
        out_specs=pl.BlockSpec((n, tn), lambda j: (0, j)),
        compiler_params=_cparams(("arbitrary",)),
    )(c_all, ada_w, ada_b)


def _inproj_kernel(x_ref, sc_ref, sh_ref, g_ref, w_ref, o_ref):
    h = _rms(x_ref[...], g_ref[...]) * (1.0 + sc_ref[...]) + sh_ref[...]
    o_ref[...] = jnp.dot(h.astype(BF16), w_ref[...], preferred_element_type=F32)


def _mod_spec(mod, tm, tiles_per_seq):
    if mod.ndim == 3:
        return pl.BlockSpec((None, 1, D_MODEL), lambda i: (i // tiles_per_seq, 0, 0))
    return pl.BlockSpec((tm, D_MODEL), lambda i: (i, 0))


def _inproj(x, sc, sh, g, w_bf, tm, tiles_per_seq):
    t = x.shape[0]
    return pl.pallas_call(
        _inproj_kernel,
        out_shape=jax.ShapeDtypeStruct((t, IN_COLS), F32),
        grid=(t // tm,),
        in_specs=[pl.BlockSpec((tm, D_MODEL), lambda i: (i, 0)),
                  _mod_spec(sc, tm, tiles_per_seq), _mod_spec(sh, tm, tiles_per_seq),
                  pl.BlockSpec((1, D_MODEL), lambda i: (0, 0)),
                  pl.BlockSpec((D_MODEL, IN_COLS), lambda i: (0, 0))],
        out_specs=pl.BlockSpec((tm, IN_COLS), lambda i: (i, 0)),
        compiler_params=_cparams(("arbitrary",)),
    )(x, sc, sh, g, w_bf)


def _lru_gates(xc, wg_ref, bg_ref, lam_ref):
    gates = jnp.dot(xc.astype(BF16), wg_ref[...], preferred_element_type=F32) + bg_ref[...]
    r = jax.nn.sigmoid(gates[:, :LRU_WIDTH])
    ig = jax.nn.sigmoid(gates[:, LRU_WIDTH:])
    z = -lam_ref[...]
    softplus = jnp.maximum(z, 0.0) + jnp.log1p(jnp.exp(-jnp.abs(z)))
    log_a = -LRU_C * r * softplus
    a = jnp.exp(log_a)
    th = jnp.tanh(log_a)
    mult2 = -2.0 * th / (1.0 - th)
    return a, jnp.sqrt(mult2), ig


def _conv4(xbuf, xl, cw_ref, cb_ref, rows):
    cw = cw_ref[...]
    h = CONV_HDR
    return (cb_ref[...] + cw[0:1] * xbuf[h - 3:h - 3 + rows, :] + cw[1:2] * xbuf[h - 2:h - 2 + rows, :]
            + cw[2:3] * xbuf[h - 1:h - 1 + rows, :] + cw[3:4] * xl)


def _head_norm_gate(o, gate, hgn):
    return _rms(o, hgn) * jax.nn.sigmoid(gate)


def _mixer_prompt_kernel(proj_ref, cw_ref, cb_ref, wg_ref, bg_ref, lam_ref, lb_ref, hgn_ref,
                         m_ref, conv_ref, h_ref, s_ref,
                         xbuf, hcar, st, obuf, *, tc):
    j = pl.program_id(1)
    w = LRU_WIDTH

    @pl.when(j == 0)
    def _():
        xbuf[0:CONV_HDR, :] = jnp.zeros((CONV_HDR, w), F32)
        hcar[...] = jnp.zeros_like(hcar)
        st[...] = jnp.zeros_like(st)

    xl = proj_ref[:, C_XL:C_XL + w]
    xbuf[CONV_HDR:CONV_HDR + tc, :] = xl
    xc = _conv4(xbuf, xl, cw_ref, cb_ref, tc)
    xbuf[0:CONV_HDR, :] = xbuf[tc:tc + CONV_HDR, :]
    a, mult, ig = _lru_gates(xc, wg_ref, bg_ref, lam_ref)
    row = lax.broadcasted_iota(I32, (tc, w), 0)
    mult = jnp.where((row == 0) & (j == 0), 1.0, mult)
    u = mult * ig * xc
    s = 1
    while s < tc:
        keep = row >= s
        a_sh = jnp.where(keep, pltpu.roll(a, s, 0), 1.0)
        u_sh = jnp.where(keep, pltpu.roll(u, s, 0), 0.0)
        u = a * u_sh + u
        a = a * a_sh
        s *= 2
    h = u + a * hcar[0:1, :]
    hcar[0:1, :] = h[tc - 1:tc, :]
    m_ref[:, 0:w] = (h * _gelu_tanh(proj_ref[:, C_YL:C_YL + w])).astype(BF16)

    lb = lb_ref[...]
    f = lb + (1.0 - lb) * jax.nn.sigmoid(proj_ref[:, C_F:C_F + w])
    k = 1.0 - f
    b = jnp.log(f)
    rowc = row & (HG_CHUNK - 1)
    s = 1
    while s < HG_CHUNK:
        b = b + jnp.where(rowc >= s, pltpu.roll(b, s, 0), 0.0)
        s *= 2
    q = proj_ref[:, C_Q:C_Q + w]
    v = proj_ref[:, C_V:C_V + w]
    tri = (lax.broadcasted_iota(I32, (HG_CHUNK, HG_CHUNK), 0)
           >= lax.broadcasted_iota(I32, (HG_CHUNK, HG_CHUNK), 1))
    mid = HG_CHUNK // 2
    for hd in range(HG_HEADS):
        cs = slice(hd * HG_HEAD_DIM, (hd + 1) * HG_HEAD_DIM)
        s_t = st[hd]
        for c in range(tc // HG_CHUNK):
            rs = slice(c * HG_CHUNK, (c + 1) * HG_CHUNK)
            bc, qc, kc = b[rs, cs], q[rs, cs], k[rs, cs]
            vb = v[rs, cs].astype(BF16)
            b_mid = bc[mid - 1:mid, :]
            b_end = bc[HG_CHUNK - 1:HG_CHUNK, :]
            qp = (qc * jnp.exp(bc - b_mid)).astype(BF16)
            kp = (kc * jnp.exp(b_mid - bc)).astype(BF16)
            att = lax.dot_general(qp, kp, _NT, preferred_element_type=F32)
            att = jnp.where(tri, att, 0.0)
            q_in = (qc * jnp.exp(bc)).astype(BF16)
            o = (jnp.dot(att.astype(BF16), vb, preferred_element_type=F32)
                 + lax.dot_general(q_in, s_t.astype(BF16), _NT, preferred_element_type=F32))
            k_out = (kc * jnp.exp(b_end - bc)).astype(BF16)
            s_t = s_t * jnp.exp(b_end) + lax.dot_general(vb, k_out, _TN, preferred_element_type=F32)
            obuf[rs, cs] = o
        st[hd] = s_t
    hgn = hgn_ref[...]
    for hd in range(HG_HEADS):
        cs = slice(hd * HG_HEAD_DIM, (hd + 1) * HG_HEAD_DIM)
        gate = proj_ref[:, C_G + hd * HG_HEAD_DIM:C_G + (hd + 1) * HG_HEAD_DIM]
        m_ref[:, w + hd * HG_HEAD_DIM:w + (hd + 1) * HG_HEAD_DIM] = (
            _head_norm_gate(obuf[:, cs], gate, hgn).astype(BF16))

    @pl.when(j == pl.num_programs(1) - 1)
    def _():
        conv_ref[...] = xbuf[CONV_HDR - 3:CONV_HDR, :]
        h_ref[...] = hcar[0:1, :]
        for hd in range(HG_HEADS):
            s_ref[hd] = st[hd].T


def _mixer_prompt(proj, nb, seq, mw, tc):
    cw, cb, wg, bg, lam, lb, hgn = mw
    nj = seq // tc
    const = lambda shape: pl.BlockSpec(shape, lambda b, j: tuple(0 for _ in shape))
    return pl.pallas_call(
        functools.partial(_mixer_prompt_kernel, tc=tc),
        out_shape=(jax.ShapeDtypeStruct((nb * seq, D_MODEL), BF16),
                   jax.ShapeDtypeStruct((nb, CONV_WIDTH - 1, LRU_WIDTH), F32),
                   jax.ShapeDtypeStruct((nb, 1, LRU_WIDTH), F32),
                   jax.ShapeDtypeStruct((nb, HG_HEADS, HG_HEAD_DIM, HG_HEAD_DIM), F32)),
        grid=(nb, nj),
        in_specs=[pl.BlockSpec((tc, IN_COLS), lambda b, j: (b * nj + j, 0)),
                  const((CONV_WIDTH, LRU_WIDTH)), const((1, LRU_WIDTH)),
                  const((LRU_WIDTH, 2 * LRU_WIDTH)), const((1, 2 * LRU_WIDTH)),
                  const((1, LRU_WIDTH)), const((1, HG_WIDTH)), const((1, HG_HEAD_DIM))],
        out_specs=(pl.BlockSpec((tc, D_MODEL), lambda b, j: (b * nj + j, 0)),
                   pl.BlockSpec((None, CONV_WIDTH - 1, LRU_WIDTH), lambda b, j: (b, 0, 0)),
                   pl.BlockSpec((None, 1, LRU_WIDTH), lambda b, j: (b, 0, 0)),
                   pl.BlockSpec((None, HG_HEADS, HG_HEAD_DIM, HG_HEAD_DIM), lambda b, j: (b, 0, 0, 0))),
        scratch_shapes=[pltpu.VMEM((CONV_HDR + tc, LRU_WIDTH), F32),
                        pltpu.VMEM((SUBLANES, LRU_WIDTH), F32),
                        pltpu.VMEM((HG_HEADS, HG_HEAD_DIM, HG_HEAD_DIM), F32),
                        pltpu.VMEM((tc, HG_WIDTH), F32)],
        compiler_params=_cparams(("arbitrary", "arbitrary")),
    )(proj, cw, cb, wg, bg, lam, lb, hgn)


def _mixer_sample_kernel(proj_ref, cst_ref, h0_ref, s0_ref, cw_ref, cb_ref, wg_ref, bg_ref, lam_ref, lb_ref,
                         hgn_ref, m_ref, conv_ref, h_ref, s_ref, xbuf, *, steps):
    w = LRU_WIDTH
    rows = SUBLANES
    row = lax.broadcasted_iota(I32, (rows, w), 0)
    valid = row < steps

    xl = proj_ref[:, C_XL:C_XL + w]
    xbuf[0:CONV_HDR, :] = cst_ref[...]
    xbuf[CONV_HDR:CONV_HDR + rows, :] = xl
    xc = _conv4(xbuf, xl, cw_ref, cb_ref, rows)
    conv_ref[...] = xbuf[CONV_HDR + steps - 3:CONV_HDR + steps, :]
    a, mult, ig = _lru_gates(xc, wg_ref, bg_ref, lam_ref)
    u = mult * ig * xc
    h = h0_ref[...]
    hs = jnp.zeros((rows, w), F32)
    for t in range(steps):
        h = a[t:t + 1, :] * h + u[t:t + 1, :]
        hs = jnp.where(row == t, h, hs)
    h_ref[...] = h
    m_ref[:, 0:w] = (hs * _gelu_tanh(proj_ref[:, C_YL:C_YL + w])).astype(BF16)

    lb = lb_ref[...]
    f = lb + (1.0 - lb) * jax.nn.sigmoid(proj_ref[:, C_F:C_F + w])
    k = 1.0 - f
    b = jnp.log(f)
    s = 1
    while s < rows:
        b = b + jnp.where(row >= s, pltpu.roll(b, s, 0), 0.0)
        s *= 2
    q = proj_ref[:, C_Q:C_Q + w]
    v = proj_ref[:, C_V:C_V + w]
    b_end = b[steps - 1:steps, :]
    q_in = jnp.where(valid, q * jnp.exp(b), 0.0)
    k_out = jnp.where(valid, k * jnp.exp(jnp.where(valid, b_end - b, 0.0)), 0.0)
    v_ok = jnp.where(valid, v, 0.0)
    hgn = hgn_ref[...]
    row_h = lax.broadcasted_iota(I32, (rows, HG_HEAD_DIM), 0)
    for hd in range(HG_HEADS):
        cs = slice(hd * HG_HEAD_DIM, (hd + 1) * HG_HEAD_DIM)
        s_t = s0_ref[hd].T
        o = lax.dot_general(q_in[:, cs].astype(BF16), s_t.astype(BF16), _NT, preferred_element_type=F32)
        for d in range(steps):
            ok = (row_h < steps) & (row_h >= d)
            k_d = k[:, cs] if d == 0 else pltpu.roll(k[:, cs], d, 0)
            v_d = v[:, cs] if d == 0 else pltpu.roll(v[:, cs], d, 0)
            b_d = b[:, cs] if d == 0 else pltpu.roll(b[:, cs], d, 0)
            decay = jnp.exp(jnp.where(ok, b[:, cs] - b_d, 0.0))
            wgt = jnp.sum(jnp.where(ok, q[:, cs] * k_d * decay, 0.0), axis=-1, keepdims=True)
            o = o + wgt * v_d
        s_new = s_t * jnp.exp(b_end[:, cs]) + lax.dot_general(
            v_ok[:, cs].astype(BF16), k_out[:, cs].astype(BF16), _TN, preferred_element_type=F32)
        s_ref[hd] = s_new.T
        gate = proj_ref[:, C_G + hd * HG_HEAD_DIM:C_G + (hd + 1) * HG_HEAD_DIM]
        m_ref[:, w + hd * HG_HEAD_DIM:w + (hd + 1) * HG_HEAD_DIM] = _head_norm_gate(o, gate, hgn).astype(BF16)


def _mixer_sample(proj_pad, cst_hdr, h0, s0, mw, steps):
    cw, cb, wg, bg, lam, lb, hgn = mw
    nb = proj_pad.shape[0]
    const = lambda shape: pl.BlockSpec(shape, lambda b: tuple(0 for _ in shape))
    per_b = lambda shape: pl.BlockSpec((None,) + shape, lambda b: (b,) + tuple(0 for _ in shape))
    return pl.pallas_call(
        functools.partial(_mixer_sample_kernel, steps=steps),
        out_shape=(jax.ShapeDtypeStruct((nb, SUBLANES, D_MODEL), BF16),
                   jax.ShapeDtypeStruct((nb, CONV_WIDTH - 1, LRU_WIDTH), F32),
                   jax.ShapeDtypeStruct((nb, 1, LRU_WIDTH), F32),
                   jax.ShapeDtypeStruct((nb, HG_HEADS, HG_HEAD_DIM, HG_HEAD_DIM), F32)),
        grid=(nb,),
        in_specs=[per_b((SUBLANES, IN_COLS)), per_b((CONV_HDR, LRU_WIDTH)), per_b((1, LRU_WIDTH)),
                  per_b((HG_HEADS, HG_HEAD_DIM, HG_HEAD_DIM)),
                  const((CONV_WIDTH, LRU_WIDTH)), const((1, LRU_WIDTH)),
                  const((LRU_WIDTH, 2 * LRU_WIDTH)), const((1, 2 * LRU_WIDTH)),
                  const((1, LRU_WIDTH)), const((1, HG_WIDTH)), const((1, HG_HEAD_DIM))],
        out_specs=(per_b((SUBLANES, D_MODEL)), per_b((CONV_WIDTH - 1, LRU_WIDTH)), per_b((1, LRU_WIDTH)),
                   per_b((HG_HEADS, HG_HEAD_DIM, HG_HEAD_DIM))),
        scratch_shapes=[pltpu.VMEM((CONV_HDR + SUBLANES, LRU_WIDTH), F32)],
        compiler_params=_cparams(("arbitrary",)),
    )(proj_pad, cst_hdr, h0, s0, cw, cb, wg, bg, lam, lb, hgn)


def _post_kernel(m_ref, x_ref, gtm_ref, scf_ref, shf_ref, npost_ref, npre_ref, wout_ref, rwt_ref, rb_ref,
                 x1_ref, hf_ref, idx_ref, gate_ref):
    mix = jnp.dot(m_ref[...], wout_ref[...], preferred_element_type=F32)
    x1 = x_ref[...] + gtm_ref[...] * _rms(mix, npost_ref[...])
    x1_ref[...] = x1
    hf = _rms(x1, npre_ref[...]) * (1.0 + scf_ref[...]) + shf_ref[...]
    hf_ref[...] = hf
    logits = lax.dot_general(rwt_ref[...], hf, _NT, precision=lax.Precision.HIGHEST,
                             preferred_element_type=F32) + rb_ref[...]
    e_iota = lax.broadcasted_iota(I32, logits.shape, 0)
    vals, idxs = [], []
    for _ in range(TOP_K):
        mx = jnp.max(logits, axis=0, keepdims=True)
        ix = jnp.min(jnp.where(logits == mx, e_iota, N_EXPERTS), axis=0, keepdims=True)
        vals.append(mx)
        idxs.append(ix)
        logits = jnp.where(e_iota == ix, -jnp.inf, logits)
    exps = [jnp.exp(vv - vals[0]) for vv in vals]
    den = exps[0] + exps[1] + exps[2] + exps[3]
    for kk in range(TOP_K):
        idx_ref[kk:kk + 1, :] = idxs[kk]
        gate_ref[kk:kk + 1, :] = exps[kk] / den


def _post(m, x, gtm, scf, shf, npost, npre, wout_bf, rwt, rb, tm, tiles_per_seq):
    t = x.shape[0]
    row = lambda: pl.BlockSpec((1, D_MODEL), lambda i: (0, 0))
    tok = lambda: pl.BlockSpec((tm, D_MODEL), lambda i: (i, 0))
    return pl.pallas_call(
        _post_kernel,
        out_shape=(jax.ShapeDtypeStruct((t, D_MODEL), F32), jax.ShapeDtypeStruct((t, D_MODEL), F32),
                   jax.ShapeDtypeStruct((TOP_K, t), I32), jax.ShapeDtypeStruct((TOP_K, t), F32)),
        grid=(t // tm,),
        in_specs=[tok(), tok(),
                  _mod_spec(gtm, tm, tiles_per_seq), _mod_spec(scf, tm, tiles_per_seq),
                  _mod_spec(shf, tm, tiles_per_seq), row(), row(),
                  pl.BlockSpec((D_MODEL, D_MODEL), lambda i: (0, 0)),
                  pl.BlockSpec((N_EXPERTS, D_MODEL), lambda i: (0, 0)),
                  pl.BlockSpec((N_EXPERTS, 1), lambda i: (0, 0))],
        out_specs=(tok(), tok(), pl.BlockSpec((TOP_K, tm), lambda i: (0, i)),
                   pl.BlockSpec((TOP_K, tm), lambda i: (0, i))),
        compiler_params=_cparams(("arbitrary",)),
    )(m, x, gtm, scf, shf, npost, npre, wout_bf, rwt, rb)


def _row_gather_pipeline(i, n_live, idx_hbm, src_hbm, buf, idx_smem, row_sem, idx_sem, rows):
    slot = i % 2

    def idx_copy(step, s):
        return pltpu.make_async_copy(idx_hbm.at[step], idx_smem.at[s], idx_sem.at[s])

    def issue_rows(s):
        def body(r, carry):
            src_row = idx_smem[s, 0, r]
            pltpu.make_async_copy(src_hbm.at[pl.ds(src_row, 1), :], buf.at[s, pl.ds(r, 1), :],
                                  row_sem.at[s]).start()
            return carry
        lax.fori_loop(0, rows, body, 0)

    @pl.when(i == 0)
    def _():
        idx_copy(0, 0).start()
        idx_copy(0, 0).wait()
        issue_rows(0)

        @pl.when(n_live > 1)
        def _():
            idx_copy(1, 1).start()

    @pl.when(i + 1 < n_live)
    def _():
        idx_copy(i + 1, 1 - slot).wait()
        issue_rows(1 - slot)

    @pl.when(i + 2 < n_live)
    def _():
        idx_copy(i + 2, slot).start()

    @pl.when(i < n_live)
    def _():
        pltpu.make_async_copy(buf.at[slot], buf.at[slot], row_sem.at[slot]).wait()


def _expert_kernel(be_ref, nu_ref, rt_hbm, hf_hbm, wgu_ref, bgu_ref, wd_ref, bd_ref, ys_ref,
                   xbuf, idx_smem, wgu_bf, wd_bf, row_sem, idx_sem):
    i = pl.program_id(0)
    n_used = nu_ref[0]
    _row_gather_pipeline(i, n_used, rt_hbm, hf_hbm, xbuf, idx_smem, row_sem, idx_sem, MOE_BM)

    @pl.when(i < n_used)
    def _():
        prev = be_ref[jnp.maximum(i - 1, 0)]

        @pl.when((i == 0) | (be_ref[i] != prev))
        def _():
            wgu_bf[...] = wgu_ref[...].astype(BF16)
            wd_bf[...] = wd_ref[...].astype(BF16)

        x = xbuf[i % 2].astype(BF16)
        hu = jnp.dot(x, wgu_bf[...], preferred_element_type=F32) + bgu_ref[...]
        glu = jnp.minimum(hu[:, :D_EXPERT], SWIGLU_LIMIT)
        lin = jnp.clip(hu[:, D_EXPERT:], -SWIGLU_LIMIT, SWIGLU_LIMIT)
        act = glu * jax.nn.sigmoid(SWIGLU_ALPHA * glu) * (lin + 1.0)
        ys_ref[...] = jnp.dot(act.astype(BF16), wd_bf[...], preferred_element_type=F32) + bd_ref[...]

    @pl.when(i >= n_used)
    def _():
        ys_ref[...] = jnp.zeros_like(ys_ref)


def _experts(block_e, n_used, row_tok, hf, w_gu, b_gu, w_down, b_down):
    n_blocks = row_tok.shape[0]
    return pl.pallas_call(
        _expert_kernel,
        out_shape=jax.ShapeDtypeStruct((n_blocks * MOE_BM, D_MODEL), F32),
        grid_spec=pltpu.PrefetchScalarGridSpec(
            num_scalar_prefetch=2,
            grid=(n_blocks,),
            in_specs=[pl.BlockSpec(memory_space=pl.ANY), pl.BlockSpec(memory_space=pl.ANY),
                      pl.BlockSpec((None, D_MODEL, 2 * D_EXPERT), lambda i, be, nu: (be[i], 0, 0)),
                      pl.BlockSpec((None, 1, 2 * D_EXPERT), lambda i, be, nu: (be[i], 0, 0)),
                      pl.BlockSpec((None, D_EXPERT, D_MODEL), lambda i, be, nu: (be[i], 0, 0)),
                      pl.BlockSpec((None, 1, D_MODEL), lambda i, be, nu: (be[i], 0, 0))],
            out_specs=pl.BlockSpec((MOE_BM, D_MODEL), lambda i, be, nu: (i, 0)),
            scratch_shapes=[pltpu.VMEM((2, MOE_BM, D_MODEL), F32),
                            pltpu.SMEM((2, 1, MOE_BM), I32),
                            pltpu.VMEM((D_MODEL, 2 * D_EXPERT), BF16),
                            pltpu.VMEM((D_EXPERT, D_MODEL), BF16),
                            pltpu.SemaphoreType.DMA((2,)),
                            pltpu.SemaphoreType.DMA((2,))]),
        compiler_params=_cparams(("arbitrary",)),
    )(block_e, n_used, row_tok, hf, w_gu, b_gu, w_down, b_down)


def _combine_kernel(dst_hbm, ys_hbm, g_ref, x1_ref, gt_ref, np_ref, o_ref, ybuf, idx_smem, row_sem, idx_sem,
                    *, tm):
    i = pl.program_id(0)
    _row_gather_pipeline(i, pl.num_programs(0), dst_hbm, ys_hbm, ybuf, idx_smem, row_sem, idx_sem, TOP_K * tm)
    slot = i % 2
    g = g_ref[...]
    y = g[:, 0:1] * ybuf[slot, 0:tm, :]
    for kk in range(1, TOP_K):
        y = y + g[:, kk:kk + 1] * ybuf[slot, kk * tm:(kk + 1) * tm, :]
    o_ref[...] = x1_ref[...] + gt_ref[...] * _rms(y, np_ref[...])


def _combine(dst, ys, gates_t, x1, gt, npost, tm, tiles_per_seq):
    t = x1.shape[0]
    return pl.pallas_call(
        functools.partial(_combine_kernel, tm=tm),
        out_shape=jax.ShapeDtypeStruct((t, D_MODEL), F32),
        grid=(t // tm,),
        in_specs=[pl.BlockSpec(memory_space=pl.ANY), pl.BlockSpec(memory_space=pl.ANY),
                  pl.BlockSpec((tm, TOP_K), lambda i: (i, 0)),
                  pl.BlockSpec((tm, D_MODEL), lambda i: (i, 0)),
                  _mod_spec(gt, tm, tiles_per_seq),
                  pl.BlockSpec((1, D_MODEL), lambda i: (0, 0))],
        out_specs=pl.BlockSpec((tm, D_MODEL), lambda i: (i, 0)),
        scratch_shapes=[pltpu.VMEM((2, TOP_K * tm, D_MODEL), F32),
                        pltpu.SMEM((2, 1, TOP_K * tm), I32),
                        pltpu.SemaphoreType.DMA((2,)),
                        pltpu.SemaphoreType.DMA((2,))],
        compiler_params=_cparams(("arbitrary",)),
    )(dst, ys, gates_t, x1, gt, npost)


def _dispatch_tables(idx):
    t = idx.shape[1]
    n = TOP_K * t
    e_flat = idx.reshape(n)
    tok_flat = jnp.tile(jnp.arange(t, dtype=I32), TOP_K)
    onehot = (e_flat[:, None] == jnp.arange(N_EXPERTS, dtype=I32)[None, :]).astype(I32)
    cum = jnp.cumsum(onehot, axis=0)
    counts = cum[-1]
    rank = jnp.sum(onehot * cum, axis=1) - 1
    padded = (counts + MOE_BM - 1) // MOE_BM * MOE_BM
    pad_ends = jnp.cumsum(padded)
    pad_starts = pad_ends - padded
    dest = pad_starts[e_flat] + rank
    n_blocks = -(-n // MOE_BM) + N_EXPERTS
    row_tok = jnp.zeros((n_blocks * MOE_BM,), I32).at[dest].set(tok_flat)
    block_e = jnp.minimum(
        jnp.searchsorted(pad_ends, jnp.arange(n_blocks, dtype=I32) * MOE_BM, side='right'),
        N_EXPERTS - 1).astype(I32)
    n_used = (pad_ends[-1:] // MOE_BM).astype(I32)
    return dest.reshape(TOP_K, t), row_tok.reshape(n_blocks, 1, MOE_BM), block_e, n_used


def _tile_major(a, tm):
    t = a.shape[1]
    return a.reshape(TOP_K, t // tm, tm).transpose(1, 0, 2).reshape(t // tm, 1, TOP_K * tm)


def _block_diag(wb):
    n, d, _ = wb.shape
    eye = jnp.eye(n, dtype=wb.dtype)
    return (wb[:, :, None, :] * eye[:, None, :, None]).reshape(n * d, n * d)


def _pick_tile(n, prefs):
    for p in prefs:
        if n % p == 0:
            return p
    raise ValueError(f"no tile for {n}")


def kernel(x_prompt, x_sample, state_rglru_conv, state_rglru_h, state_hgrn_S, c_prompt, c_sample, ada_w, ada_b, norm_pre_mix, norm_post_mix, norm_pre_ffn, norm_post_ffn, w_in, conv_w, conv_b, lru_wa, lru_ba, lru_wi, lru_bi, lru_lambda, hg_lb, hg_norm, w_out, router_w, router_b, w_gu, b_gu, w_down, b_down):
    assert ada_w.shape[0] == 1, "single-layer trunk"
    bp, lp, _ = x_prompt.shape
    bs, ls, _ = x_sample.shape
    tp, ts = bp * lp, bs * ls
    assert CONV_WIDTH - 1 <= ls <= SUBLANES

    w_in_bf = w_in[0].astype(BF16)
    w_out_bf = w_out[0].astype(BF16)
    wg = jnp.concatenate([_block_diag(lru_wa[0]), _block_diag(lru_wi[0])], axis=1).astype(BF16)
    bg = jnp.concatenate([lru_ba[0], lru_bi[0]])[None, :]
    lbs = jnp.cumsum(jax.nn.softmax(hg_lb.astype(F32), axis=0), axis=0)[0][None, :]
    mw = (conv_w[0], conv_b[0][None, :], wg, bg, lru_lambda[0][None, :], lbs, hg_norm[0][None, :])
    row = lambda p: p[0][None, :]

    n_c = bp + bs
    n_c_pad = -(-n_c // SUBLANES) * SUBLANES
    c_all = jnp.concatenate([c_prompt, c_sample, jnp.zeros((n_c_pad - n_c, D_MODEL), F32)], axis=0)
    mod = _ada(c_all, ada_w[0], ada_b[0][None, :])
    mod_p = [m[:, None, :] for m in jnp.split(mod[:bp], 6, axis=-1)]
    mod_s = [jnp.repeat(m, ls, axis=0) for m in jnp.split(mod[bp:n_c], 6, axis=-1)]

    xp = x_prompt.reshape(tp, D_MODEL)
    xs = x_sample.reshape(ts, D_MODEL)
    tm_p = _pick_tile(lp, (256, 128, 64))
    tm_s = _pick_tile(ts, (256, 128, 64, 32, 16, 8))

    proj_p = _inproj(xp, mod_p[1], mod_p[0], row(norm_pre_mix), w_in_bf, tm_p, lp // tm_p)
    proj_s = _inproj(xs, mod_s[1], mod_s[0], row(norm_pre_mix), w_in_bf, tm_s, 1)
    tc = _pick_tile(lp, (256, 128, 64))
    m_p, conv_p, h_p, s_p = _mixer_prompt(proj_p, bp, lp, mw, tc)

    proj_s_pad = jnp.pad(proj_s.reshape(bs, ls, IN_COLS), ((0, 0), (0, SUBLANES - ls), (0, 0)))
    cst_hdr = jnp.pad(state_rglru_conv[0], ((0, 0), (CONV_HDR - (CONV_WIDTH - 1), 0), (0, 0)))
    m_s_pad, conv_s, h_s, s_s = _mixer_sample(proj_s_pad, cst_hdr, state_rglru_h[0][:, None, :],
                                              state_hgrn_S[0], mw, ls)
    m_s = m_s_pad[:, :ls, :].reshape(ts, D_MODEL)

    rwt = router_w[0].T
    rb = router_b[0][:, None]
    post = lambda m, x, md, tm, tps: _post(m, x, md[2], md[4], md[3], row(norm_post_mix), row(norm_pre_ffn),
                                           w_out_bf, rwt, rb, tm, tps)
    x1_p, hf_p, idx_p, gate_p = post(m_p, xp, mod_p, tm_p, lp // tm_p)
    x1_s, hf_s, idx_s, gate_s = post(m_s, xs, mod_s, tm_s, 1)

    hf = jnp.concatenate([hf_p, hf_s], axis=0)
    idx = jnp.concatenate([idx_p, idx_s], axis=1)
    dest, row_tok, block_e, n_used = _dispatch_tables(idx)
    ys = _experts(block_e, n_used, row_tok, hf, w_gu[0], b_gu[0][:, None, :], w_down[0], b_down[0][:, None, :])

    tmc_p = _pick_tile(lp, (128, 64))
    tmc_s = _pick_tile(ts, (128, 64, 32, 16, 8))
    y_p = _combine(_tile_major(dest[:, :tp], tmc_p), ys, gate_p.T, x1_p, mod_p[5], row(norm_post_ffn),
                   tmc_p, lp // tmc_p)
    y_s = _combine(_tile_major(dest[:, tp:], tmc_s), ys, gate_s.T, x1_s, mod_s[5], row(norm_post_ffn),
                   tmc_s, 1)

    return (y_p.reshape(bp, lp, D_MODEL), y_s.reshape(bs, ls, D_MODEL),
            conv_p[None], h_p.reshape(1, bp, LRU_WIDTH), s_p[None],
            conv_s[None], h_s.reshape(1, bs, LRU_WIDTH), s_s[None])
```

```python
import functools
import math

import jax
import jax.numpy as jnp
from jax import lax
from jax.experimental import pallas as pl
from jax.experimental.pallas import tpu as pltpu

F32 = jnp.float32
BF16 = jnp.bfloat16
I32 = jnp.int32

D_MODEL = 1024
LRU_WIDTH = 512
LRU_BLOCKS = 8
LRU_BLOCK = LRU_WIDTH // LRU_BLOCKS
CONV_WIDTH = 4
LRU_C = 8.0
HG_WIDTH = 512
HG_HEAD_DIM = 128
HG_HEADS = 4
IN_COLS = 2 * LRU_WIDTH + 4 * HG_WIDTH
N_EXPERTS = 32
TOP_K = 4
D_EXPERT = 1024
SWIGLU_LIMIT = 7.0
SWIGLU_ALPHA = 1.702
RMS_EPS = 1e-6

C_XL, C_YL, C_Q, C_F, C_V, C_G = 0, 512, 1024, 1536, 2048, 2560

SUBLANES = 8
LANES = 128
LANE_CHUNKS = D_MODEL // LANES
CONV_HDR = SUBLANES
HG_CHUNK = 64
MOE_BM = 256
ZERO_ROWS = 64
VMEM_LIMIT = 56 * 1024 * 1024

_NT = (((1,), (1,)), ((), ()))
_TN = (((0,), (0,)), ((), ()))


def _cparams(sem):
    return pltpu.CompilerParams(dimension_semantics=sem, vmem_limit_bytes=VMEM_LIMIT)


def _rms(x, g):
    return x * lax.rsqrt(jnp.mean(x * x, axis=-1, keepdims=True) + RMS_EPS) * g


def _gelu_tanh(x):
    c = math.sqrt(2.0 / math.pi)
    return 0.5 * x * (1.0 + jnp.tanh(c * (x + 0.044715 * (x * x * x))))


def _ada_kernel(c_ref, w_ref, b_ref, o_ref):
    c = c_ref[...]
    s = c * jax.nn.sigmoid(c)
    o_ref[...] = jnp.dot(s.astype(BF16), w_ref[...].astype(BF16), preferred_element_type=F32) + b_ref[...]


def _ada(c_all, ada_w, ada_b):
    n = c_all.shape[0]
    tn = 1024
    return pl.pallas_call(
        _ada_kernel,
        out_shape=jax.ShapeDtypeStruct((n, 6 * D_MODEL), F32),
        grid=(6 * D_MODEL // tn,),
        in_specs=[pl.BlockSpec((n, D_MODEL), lambda j: (0, 0)),
                  pl.BlockSpec((D_MODEL, tn), lambda j: (0, j)),
                  pl.BlockSpec((1, tn), lambda j: (0, j))],
        out_specs=pl.BlockSpec((n, tn), lambda j: (0, j)),
        compiler_params=_cparams(("arbitrary",)),
    )(c_all, ada_w, ada_b)


def _inproj_kernel(x_ref, sc_ref, sh_ref, g_ref, w_ref, o_ref):
    h = _rms(x_ref[...], g_ref[...]) * (1.0 + sc_ref[...]) + sh_ref[...]
    o_ref[...] = jnp.dot(h.astype(BF16), w_ref[...], preferred_element_type=F32)


def _mod_spec(mod, tm, tiles_per_seq):
    if mod.ndim == 3:
        return pl.BlockSpec((None, 1, D_MODEL), lambda i: (i // tiles_per_seq, 0, 0))
    return pl.BlockSpec((tm, D_MODEL), lambda i: (i, 0))


def _inproj(x, sc, sh, g, w_bf, tm, tiles_per_seq):
    t = x.shape[0]
    return pl.pallas_call(
        _inproj_kernel,
        out_shape=jax.ShapeDtypeStruct((t, IN_COLS), F32),
        grid=(t // tm,),
        in_specs=[pl.BlockSpec((tm, D_MODEL), lambda i: (i, 0)),
                  _mod_spec(sc, tm, tiles_per_seq), _mod_spec(sh, tm, tiles_per_seq),
                  pl.BlockSpec((1, D_MODEL), lambda i: (0, 0)),
                  pl.BlockSpec((D_MODEL, IN_COLS), lambda i: (0, 0))],
        out_specs=pl.BlockSpec((tm, IN_COLS), lambda i: (i, 0)),
        compiler_params=_cparams(("arbitrary",)),
    )(x, sc, sh, g, w_bf)


def _lru_gates(xc, wg_ref, bg_ref, lam_ref):
    gates = jnp.dot(xc.astype(BF16), wg_ref[...], preferred_element_type=F32) + bg_ref[...]
    r = jax.nn.sigmoid(gates[:, :LRU_WIDTH])
    ig = jax.nn.sigmoid(gates[:, LRU_WIDTH:])
    z = -lam_ref[...]
    softplus = jnp.maximum(z, 0.0) + jnp.log1p(jnp.exp(-jnp.abs(z)))
    log_a = -LRU_C * r * softplus
    a = jnp.exp(log_a)
    th = jnp.tanh(log_a)
    mult2 = -2.0 * th / (1.0 - th)
    return a, jnp.sqrt(mult2), ig


def _conv4(xbuf, xl, cw_ref, cb_ref, rows):
    cw = cw_ref[...]
    h = CONV_HDR
    return (cb_ref[...] + cw[0:1] * xbuf[h - 3:h - 3 + rows, :] + cw[1:2] * xbuf[h - 2:h - 2 + rows, :]
            + cw[2:3] * xbuf[h - 1:h - 1 + rows, :] + cw[3:4] * xl)


def _head_norm_gate(o, gate, hgn):
    return _rms(o, hgn) * jax.nn.sigmoid(gate)


def _mixer_prompt_kernel(proj_ref, cw_ref, cb_ref, wg_ref, bg_ref, lam_ref, lb_ref, hgn_ref,
                         m_ref, conv_ref, h_ref, s_ref,
                         xbuf, hcar, st, obuf, *, tc):
    j = pl.program_id(1)
    w = LRU_WIDTH

    @pl.when(j == 0)
    def _():
        xbuf[0:CONV_HDR, :] = jnp.zeros((CONV_HDR, w), F32)
        hcar[...] = jnp.zeros_like(hcar)
        st[...] = jnp.zeros_like(st)

    xl = proj_ref[:, C_XL:C_XL + w]
    xbuf[CONV_HDR:CONV_HDR + tc, :] = xl
    xc = _conv4(xbuf, xl, cw_ref, cb_ref, tc)
    xbuf[0:CONV_HDR, :] = xbuf[tc:tc + CONV_HDR, :]
    a, mult, ig = _lru_gates(xc, wg_ref, bg_ref, lam_ref)
    row = lax.broadcasted_iota(I32, (tc, w), 0)
    mult = jnp.where((row == 0) & (j == 0), 1.0, mult)
    u = mult * ig * xc
    s = 1
    while s < tc:
        keep = row >= s
        a_sh = jnp.where(keep, pltpu.roll(a, s, 0), 1.0)
        u_sh = jnp.where(keep, pltpu.roll(u, s, 0), 0.0)
        u = a * u_sh + u
        a = a * a_sh
        s *= 2
    h = u + a * hcar[0:1, :]
    hcar[0:1, :] = h[tc - 1:tc, :]
    m_ref[:, 0:w] = (h * _gelu_tanh(proj_ref[:, C_YL:C_YL + w])).astype(BF16)

    lb = lb_ref[...]
    f = lb + (1.0 - lb) * jax.nn.sigmoid(proj_ref[:, C_F:C_F + w])
    k = 1.0 - f
    b = jnp.log(f)
    rowc = row & (HG_CHUNK - 1)
    s = 1
    while s < HG_CHUNK:
        b = b + jnp.where(rowc >= s, pltpu.roll(b, s, 0), 0.0)
        s *= 2
    q = proj_ref[:, C_Q:C_Q + w]
    v = proj_ref[:, C_V:C_V + w]
    tri = (lax.broadcasted_iota(I32, (HG_CHUNK, HG_CHUNK), 0)
           >= lax.broadcasted_iota(I32, (HG_CHUNK, HG_CHUNK), 1))
    mid = HG_CHUNK // 2
    for hd in range(HG_HEADS):
        cs = slice(hd * HG_HEAD_DIM, (hd + 1) * HG_HEAD_DIM)
        s_t = st[hd]
        for c in range(tc // HG_CHUNK):
            rs = slice(c * HG_CHUNK, (c + 1) * HG_CHUNK)
            bc, qc, kc = b[rs, cs], q[rs, cs], k[rs, cs]
            vb = v[rs, cs].astype(BF16)
            b_mid = bc[mid - 1:mid, :]
            b_end = bc[HG_CHUNK - 1:HG_CHUNK, :]
            qp = (qc * jnp.exp(bc - b_mid)).astype(BF16)
            kp = (kc * jnp.exp(b_mid - bc)).astype(BF16)
            att = lax.dot_general(qp, kp, _NT, preferred_element_type=F32)
            att = jnp.where(tri, att, 0.0)
            q_in = (qc * jnp.exp(bc)).astype(BF16)
            o = (jnp.dot(att.astype(BF16), vb, preferred_element_type=F32)
                 + lax.dot_general(q_in, s_t.astype(BF16), _NT, preferred_element_type=F32))
            k_out = (kc * jnp.exp(b_end - bc)).astype(BF16)
            s_t = s_t * jnp.exp(b_end) + lax.dot_general(vb, k_out, _TN, preferred_element_type=F32)
            obuf[rs, cs] = o
        st[hd] = s_t
    hgn = hgn_ref[...]
    for hd in range(HG_HEADS):
        cs = slice(hd * HG_HEAD_DIM, (hd + 1) * HG_HEAD_DIM)
        gate = proj_ref[:, C_G + hd * HG_HEAD_DIM:C_G + (hd + 1) * HG_HEAD_DIM]
        m_ref[:, w + hd * HG_HEAD_DIM:w + (hd + 1) * HG_HEAD_DIM] = (
            _head_norm_gate(obuf[:, cs], gate, hgn).astype(BF16))

    @pl.when(j == pl.num_programs(1) - 1)
    def _():
        conv_ref[...] = xbuf[CONV_HDR - 3:CONV_HDR, :]
        h_ref[...] = hcar[0:1, :]
        for hd in range(HG_HEADS):
            s_ref[hd] = st[hd].T


def _mixer_prompt(proj, nb, seq, mw, tc):
    cw, cb, wg, bg, lam, lb, hgn = mw
    nj = seq // tc
    const = lambda shape: pl.BlockSpec(shape, lambda b, j: tuple(0 for _ in shape))
    return pl.pallas_call(
        functools.partial(_mixer_prompt_kernel, tc=tc),
        out_shape=(jax.ShapeDtypeStruct((nb * seq, D_MODEL), BF16),
                   jax.ShapeDtypeStruct((nb, CONV_WIDTH - 1, LRU_WIDTH), F32),
                   jax.ShapeDtypeStruct((nb, 1, LRU_WIDTH), F32),
                   jax.ShapeDtypeStruct((nb, HG_HEADS, HG_HEAD_DIM, HG_HEAD_DIM), F32)),
        grid=(nb, nj),
        in_specs=[pl.BlockSpec((tc, IN_COLS), lambda b, j: (b * nj + j, 0)),
                  const((CONV_WIDTH, LRU_WIDTH)), const((1, LRU_WIDTH)),
                  const((LRU_WIDTH, 2 * LRU_WIDTH)), const((1, 2 * LRU_WIDTH)),
                  const((1, LRU_WIDTH)), const((1, HG_WIDTH)), const((1, HG_HEAD_DIM))],
        out_specs=(pl.BlockSpec((tc, D_MODEL), lambda b, j: (b * nj + j, 0)),
                   pl.BlockSpec((None, CONV_WIDTH - 1, LRU_WIDTH), lambda b, j: (b, 0, 0)),
                   pl.BlockSpec((None, 1, LRU_WIDTH), lambda b, j: (b, 0, 0)),
                   pl.BlockSpec((None, HG_HEADS, HG_HEAD_DIM, HG_HEAD_DIM), lambda b, j: (b, 0, 0, 0))),
        scratch_shapes=[pltpu.VMEM((CONV_HDR + tc, LRU_WIDTH), F32),
                        pltpu.VMEM((SUBLANES, LRU_WIDTH), F32),
                        pltpu.VMEM((HG_HEADS, HG_HEAD_DIM, HG_HEAD_DIM), F32),
                        pltpu.VMEM((tc, HG_WIDTH), F32)],
        compiler_params=_cparams(("arbitrary", "arbitrary")),
    )(proj, cw, cb, wg, bg, lam, lb, hgn)


def _mixer_sample_kernel(proj_ref, cst_ref, h0_ref, s0_ref, cw_ref, cb_ref, wg_ref, bg_ref, lam_ref, lb_ref,
                         hgn_ref, m_ref, conv_ref, h_ref, s_ref, xbuf, *, steps):
    w = LRU_WIDTH
    rows = SUBLANES
    row = lax.broadcasted_iota(I32, (rows, w), 0)
    valid = row < steps

    xl = proj_ref[:, C_XL:C_XL + w]
    xbuf[0:CONV_HDR, :] = cst_ref[...]
    xbuf[CONV_HDR:CONV_HDR + rows, :] = xl
    xc = _conv4(xbuf, xl, cw_ref, cb_ref, rows)
    conv_ref[...] = xbuf[CONV_HDR + steps - 3:CONV_HDR + steps, :]
    a, mult, ig = _lru_gates(xc, wg_ref, bg_ref, lam_ref)
    u = mult * ig * xc
    h = h0_ref[...]
    hs = jnp.zeros((rows, w), F32)
    for t in range(steps):
        h = a[t:t + 1, :] * h + u[t:t + 1, :]
        hs = jnp.where(row == t, h, hs)
    h_ref[...] = h
    m_ref[:, 0:w] = (hs * _gelu_tanh(proj_ref[:, C_YL:C_YL + w])).astype(BF16)

    lb = lb_ref[...]
    f = lb + (1.0 - lb) * jax.nn.sigmoid(proj_ref[:, C_F:C_F + w])
    k = 1.0 - f
    b = jnp.log(f)
    s = 1
    while s < rows:
        b = b + jnp.where(row >= s, pltpu.roll(b, s, 0), 0.0)
        s *= 2
    q = proj_ref[:, C_Q:C_Q + w]
    v = proj_ref[:, C_V:C_V + w]
    b_end = b[steps - 1:steps, :]
    q_in = jnp.where(valid, q * jnp.exp(b), 0.0)
    k_out = jnp.where(valid, k * jnp.exp(jnp.where(valid, b_end - b, 0.0)), 0.0)
    v_ok = jnp.where(valid, v, 0.0)
    hgn = hgn_ref[...]
    row_h = lax.broadcasted_iota(I32, (rows, HG_HEAD_DIM), 0)
    for hd in range(HG_HEADS):
        cs = slice(hd * HG_HEAD_DIM, (hd + 1) * HG_HEAD_DIM)
        s_t = s0_ref[hd].T
        o = lax.dot_general(q_in[:, cs].astype(BF16), s_t.astype(BF16), _NT, preferred_element_type=F32)
        for d in range(steps):
            ok = (row_h < steps) & (row_h >= d)
            k_d = k[:, cs] if d == 0 else pltpu.roll(k[:, cs], d, 0)
            v_d = v[:, cs] if d == 0 else pltpu.roll(v[:, cs], d, 0)
            b_d = b[:, cs] if d == 0 else pltpu.roll(b[:, cs], d, 0)
            decay = jnp.exp(jnp.where(ok, b[:, cs] - b_d, 0.0))
            wgt = jnp.sum(jnp.where(ok, q[:, cs] * k_d * decay, 0.0), axis=-1, keepdims=True)
            o = o + wgt * v_d
        s_new = s_t * jnp.exp(b_end[:, cs]) + lax.dot_general(
            v_ok[:, cs].astype(BF16), k_out[:, cs].astype(BF16), _TN, preferred_element_type=F32)
        s_ref[hd] = s_new.T
        gate = proj_ref[:, C_G + hd * HG_HEAD_DIM:C_G + (hd + 1) * HG_HEAD_DIM]
        m_ref[:, w + hd * HG_HEAD_DIM:w + (hd + 1) * HG_HEAD_DIM] = _head_norm_gate(o, gate, hgn).astype(BF16)


def _mixer_sample(proj_pad, cst_hdr, h0, s0, mw, steps):
    cw, cb, wg, bg, lam, lb, hgn = mw
    nb = proj_pad.shape[0]
    const = lambda shape: pl.BlockSpec(shape, lambda b: tuple(0 for _ in shape))
    per_b = lambda shape: pl.BlockSpec((None,) + shape, lambda b: (b,) + tuple(0 for _ in shape))
    return pl.pallas_call(
        functools.partial(_mixer_sample_kernel, steps=steps),
        out_shape=(jax.ShapeDtypeStruct((nb, SUBLANES, D_MODEL), BF16),
                   jax.ShapeDtypeStruct((nb, CONV_WIDTH - 1, LRU_WIDTH), F32),
                   jax.ShapeDtypeStruct((nb, 1, LRU_WIDTH), F32),
                   jax.ShapeDtypeStruct((nb, HG_HEADS, HG_HEAD_DIM, HG_HEAD_DIM), F32)),
        grid=(nb,),
        in_specs=[per_b((SUBLANES, IN_COLS)), per_b((CONV_HDR, LRU_WIDTH)), per_b((1, LRU_WIDTH)),
                  per_b((HG_HEADS, HG_HEAD_DIM, HG_HEAD_DIM)),
                  const((CONV_WIDTH, LRU_WIDTH)), const((1, LRU_WIDTH)),
                  const((LRU_WIDTH, 2 * LRU_WIDTH)), const((1, 2 * LRU_WIDTH)),
                  const((1, LRU_WIDTH)), const((1, HG_WIDTH)), const((1, HG_HEAD_DIM))],
        out_specs=(per_b((SUBLANES, D_MODEL)), per_b((CONV_WIDTH - 1, LRU_WIDTH)), per_b((1, LRU_WIDTH)),
                   per_b((HG_HEADS, HG_HEAD_DIM, HG_HEAD_DIM))),
        scratch_shapes=[pltpu.VMEM((CONV_HDR + SUBLANES, LRU_WIDTH), F32)],
        compiler_params=_cparams(("arbitrary",)),
    )(proj_pad, cst_hdr, h0, s0, cw, cb, wg, bg, lam, lb, hgn)


def _to_row_tiles(ref, val, rows):
    for c in range(LANE_CHUNKS):
        ref[pl.ds(c, rows, stride=LANE_CHUNKS), :] = val[:, c * LANES:(c + 1) * LANES]


def _from_row_tiles(ref, base, rows):
    return jnp.concatenate(
        [ref[pl.ds(base + c, rows, stride=LANE_CHUNKS), :] for c in range(LANE_CHUNKS)], axis=-1)


def _post_kernel(m_ref, x_ref, gtm_ref, scf_ref, shf_ref, npost_ref, npre_ref, wout_ref, rwt_ref, rb_ref,
                 cnt_in_ref, x1_ref, hf_ref, idx_ref, gate_ref, rank_ref, cnt_out_ref, cnt):
    i = pl.program_id(0)
    tm = x_ref.shape[0]

    @pl.when(i == 0)
    def _():
        cnt[...] = cnt_in_ref[...]

    mix = jnp.dot(m_ref[...], wout_ref[...], preferred_element_type=F32)
    x1 = x_ref[...] + gtm_ref[...] * _rms(mix, npost_ref[...])
    x1_ref[...] = x1
    hf = _rms(x1, npre_ref[...]) * (1.0 + scf_ref[...]) + shf_ref[...]
    _to_row_tiles(hf_ref, hf, tm)
    logits = lax.dot_general(rwt_ref[...], hf, _NT, precision=lax.Precision.HIGHEST,
                             preferred_element_type=F32) + rb_ref[...]
    e_iota = lax.broadcasted_iota(I32, logits.shape, 0)
    vals, idxs = [], []
    for _ in range(TOP_K):
        mx = jnp.max(logits, axis=0, keepdims=True)
        ix = jnp.min(jnp.where(logits == mx, e_iota, N_EXPERTS), axis=0, keepdims=True)
        vals.append(mx)
        idxs.append(ix)
        logits = jnp.where(e_iota == ix, -jnp.inf, logits)
    exps = [jnp.exp(vv - vals[0]) for vv in vals]
    den = exps[0] + exps[1] + exps[2] + exps[3]
    earlier = (lax.broadcasted_iota(I32, (tm, tm), 0) < lax.broadcasted_iota(I32, (tm, tm), 1))
    earlier = jnp.where(earlier, 1.0, 0.0).astype(BF16)
    base = cnt[...]
    for kk in range(TOP_K):
        idx_ref[kk:kk + 1, :] = idxs[kk]
        gate_ref[kk:kk + 1, :] = exps[kk] / den
        hit = e_iota == idxs[kk]
        onehot = jnp.where(hit, 1.0, 0.0)
        before = jnp.dot(onehot.astype(BF16), earlier, preferred_element_type=F32)
        rank = jnp.sum(jnp.where(hit, before + base, 0.0), axis=0, keepdims=True)
        rank_ref[kk:kk + 1, :] = rank.astype(I32)
        base = base + jnp.sum(onehot, axis=1, keepdims=True)
    cnt[...] = base

    @pl.when(i == pl.num_programs(0) - 1)
    def _():
        cnt_out_ref[...] = base


def _post(m, x, gtm, scf, shf, npost, npre, wout_bf, rwt, rb, cnt_in, tm, tiles_per_seq):
    t = x.shape[0]
    row = lambda: pl.BlockSpec((1, D_MODEL), lambda i: (0, 0))
    tok = lambda: pl.BlockSpec((tm, D_MODEL), lambda i: (i, 0))
    per_k = lambda: pl.BlockSpec((TOP_K, tm), lambda i: (0, i))
    cnt_spec = lambda: pl.BlockSpec((N_EXPERTS, 1), lambda i: (0, 0))
    return pl.pallas_call(
        _post_kernel,
        out_shape=(jax.ShapeDtypeStruct((t, D_MODEL), F32),
                   jax.ShapeDtypeStruct((t * LANE_CHUNKS, LANES), F32),
                   jax.ShapeDtypeStruct((TOP_K, t), I32), jax.ShapeDtypeStruct((TOP_K, t), F32),
                   jax.ShapeDtypeStruct((TOP_K, t), I32), jax.ShapeDtypeStruct((N_EXPERTS, 1), F32)),
        grid=(t // tm,),
        in_specs=[tok(), tok(),
                  _mod_spec(gtm, tm, tiles_per_seq), _mod_spec(scf, tm, tiles_per_seq),
                  _mod_spec(shf, tm, tiles_per_seq), row(), row(),
                  pl.BlockSpec((D_MODEL, D_MODEL), lambda i: (0, 0)),
                  pl.BlockSpec((N_EXPERTS, D_MODEL), lambda i: (0, 0)),
                  cnt_spec(), cnt_spec()],
        out_specs=(tok(), pl.BlockSpec((tm * LANE_CHUNKS, LANES), lambda i: (i, 0)),
                   per_k(), per_k(), per_k(), cnt_spec()),
        scratch_shapes=[pltpu.VMEM((N_EXPERTS, 1), F32)],
        compiler_params=_cparams(("arbitrary",)),
    )(m, x, gtm, scf, shf, npost, npre, wout_bf, rwt, rb, cnt_in)


def _dispatch_kernel(fill_lo_ref, fill_hi_ref, dest_ref, hfp_hbm, hfs_hbm, xs_hbm, zeros, row_sem, pad_sem,
                     *, tm, n_tiles_p, n_slots):
    i = pl.program_id(0)
    n_rows = TOP_K * tm

    def wait_rows():
        pltpu.make_async_copy(xs_hbm.at[pl.ds(0, n_rows)], xs_hbm.at[pl.ds(0, n_rows)], row_sem).wait()

    @pl.when(i == 0)
    def _():
        zeros[...] = jnp.zeros_like(zeros)

        def per_expert(e, total):
            lo, hi = fill_lo_ref[e], fill_hi_ref[e]

            def per_slot(s, carry):
                pltpu.make_async_copy(zeros.at[0], xs_hbm.at[s], pad_sem).start()
                return carry
            lax.fori_loop(lo, hi, per_slot, 0)
            return total + (hi - lo)
        n_fill = lax.fori_loop(0, N_EXPERTS, per_expert, 0)
        tail_lo = fill_hi_ref[N_EXPERTS - 1] // ZERO_ROWS
        n_tail = n_slots // ZERO_ROWS - tail_lo

        def per_tail(c, carry):
            pltpu.make_async_copy(zeros, xs_hbm.at[pl.ds((tail_lo + c) * ZERO_ROWS, ZERO_ROWS)], pad_sem).start()
            return carry
        lax.fori_loop(0, n_tail, per_tail, 0)

        def drain_slot(s, carry):
            pltpu.make_async_copy(zeros.at[0], xs_hbm.at[0], pad_sem).wait()
            return carry
        lax.fori_loop(0, n_fill, drain_slot, 0)

        def drain_tail(c, carry):
            pltpu.make_async_copy(zeros, xs_hbm.at[pl.ds(0, ZERO_ROWS)], pad_sem).wait()
            return carry
        lax.fori_loop(0, n_tail, drain_tail, 0)

    def issue(hf_hbm, first_token):
        for n in range(n_rows):
            pltpu.make_async_copy(hf_hbm.at[first_token + n % tm], xs_hbm.at[dest_ref[0, n]],
                                  row_sem).start(priority=n % 2)

    @pl.when(i < n_tiles_p)
    def _():
        issue(hfp_hbm, i * tm)

    @pl.when(i >= n_tiles_p)
    def _():
        issue(hfs_hbm, (i - n_tiles_p) * tm)

    @pl.when(i > 0)
    def _():
        wait_rows()

    @pl.when(i == pl.num_programs(0) - 1)
    def _():
        wait_rows()


def _dispatch(fill_lo, fill_hi, dest_tiles, hfp_tiles, hfs_tiles, n_slots, tm):
    n_tiles = dest_tiles.shape[0]
    any_spec = pl.BlockSpec(memory_space=pl.ANY)
    return pl.pallas_call(
        functools.partial(_dispatch_kernel, tm=tm, n_tiles_p=hfp_tiles.shape[0] // tm, n_slots=n_slots),
        out_shape=jax.ShapeDtypeStruct((n_slots, LANE_CHUNKS, LANES), F32),
        grid_spec=pltpu.PrefetchScalarGridSpec(
            num_scalar_prefetch=2, grid=(n_tiles,),
            in_specs=[pl.BlockSpec((None, 1, TOP_K * tm), lambda i, lo, hi: (i, 0, 0), memory_space=pltpu.SMEM),
                      any_spec, any_spec],
            out_specs=any_spec,
            scratch_shapes=[pltpu.VMEM((ZERO_ROWS, LANE_CHUNKS, LANES), F32),
                            pltpu.SemaphoreType.DMA, pltpu.SemaphoreType.DMA]),
        compiler_params=_cparams(("arbitrary",)),
    )(fill_lo, fill_hi, dest_tiles, hfp_tiles, hfs_tiles)


def _expert_kernel(be_ref, nu_ref, xs_ref, wgu_ref, bgu_ref, wd_ref, bd_ref, ys_ref, wgu_bf, wd_bf):
    i = pl.program_id(0)

    @pl.when(i < nu_ref[0])
    def _():
        prev = be_ref[jnp.maximum(i - 1, 0)]

        @pl.when((i == 0) | (be_ref[i] != prev))
        def _():
            wgu_bf[...] = wgu_ref[...].astype(BF16)
            wd_bf[...] = wd_ref[...].astype(BF16)

        x = _from_row_tiles(xs_ref, 0, MOE_BM).astype(BF16)
        hu = jnp.dot(x, wgu_bf[...], preferred_element_type=F32) + bgu_ref[...]
        glu = jnp.minimum(hu[:, :D_EXPERT], SWIGLU_LIMIT)
        lin = jnp.clip(hu[:, D_EXPERT:], -SWIGLU_LIMIT, SWIGLU_LIMIT)
        act = glu * jax.nn.sigmoid(SWIGLU_ALPHA * glu) * (lin + 1.0)
        y = jnp.dot(act.astype(BF16), wd_bf[...], preferred_element_type=F32) + bd_ref[...]
        _to_row_tiles(ys_ref, y, MOE_BM)

    @pl.when(i >= nu_ref[0])
    def _():
        ys_ref[...] = jnp.zeros_like(ys_ref)


def _experts(block_e, n_used, xs2d, w_gu, b_gu, w_down, b_down):
    n_blocks = block_e.shape[0]
    rows = MOE_BM * LANE_CHUNKS
    live = lambda i, be, nu: (jnp.minimum(i, nu[0] - 1), 0)
    return pl.pallas_call(
        _expert_kernel,
        out_shape=jax.ShapeDtypeStruct(xs2d.shape, F32),
        grid_spec=pltpu.PrefetchScalarGridSpec(
            num_scalar_prefetch=2,
            grid=(n_blocks,),
            in_specs=[pl.BlockSpec((rows, LANES), live),
                      pl.BlockSpec((None, D_MODEL, 2 * D_EXPERT), lambda i, be, nu: (be[i], 0, 0)),
                      pl.BlockSpec((None, 1, 2 * D_EXPERT), lambda i, be, nu: (be[i], 0, 0)),
                      pl.BlockSpec((None, D_EXPERT, D_MODEL), lambda i, be, nu: (be[i], 0, 0)),
                      pl.BlockSpec((None, 1, D_MODEL), lambda i, be, nu: (be[i], 0, 0))],
            out_specs=pl.BlockSpec((rows, LANES), lambda i, be, nu: (i, 0)),
            scratch_shapes=[pltpu.VMEM((D_MODEL, 2 * D_EXPERT), BF16),
                            pltpu.VMEM((D_EXPERT, D_MODEL), BF16)]),
        compiler_params=_cparams(("arbitrary",)),
    )(block_e, n_used, xs2d, w_gu, b_gu, w_down, b_down)


def _combine_kernel(dcur_ref, dnext_ref, ys_hbm, g_ref, x1_ref, gt_ref, np_ref, o_ref, ybuf, row_sem, *, tm):
    i = pl.program_id(0)
    n_rows = TOP_K * tm
    slot = i % 2

    def slot_rows(s):
        return ybuf.at[pl.ds(pl.multiple_of(s * n_rows * LANE_CHUNKS, n_rows * LANE_CHUNKS), n_rows * LANE_CHUNKS), :]

    def issue(d_ref, s):
        dst = slot_rows(s)
        for n in range(n_rows):
            pltpu.make_async_copy(ys_hbm.at[d_ref[0, n]], dst.at[pl.ds(n * LANE_CHUNKS, LANE_CHUNKS), :],
                                  row_sem.at[s]).start(priority=n % 2)

    @pl.when(i == 0)
    def _():
        issue(dcur_ref, 0)

    @pl.when(i + 1 < pl.num_programs(0))
    def _():
        issue(dnext_ref, 1 - slot)

    pltpu.make_async_copy(slot_rows(slot), slot_rows(slot), row_sem.at[slot]).wait()
    g = g_ref[...]
    y = None
    for kk in range(TOP_K):
        rows = _from_row_tiles(ybuf, (slot * n_rows + kk * tm) * LANE_CHUNKS, tm)
        y = g[:, kk:kk + 1] * rows if y is None else y + g[:, kk:kk + 1] * rows
    o_ref[...] = x1_ref[...] + gt_ref[...] * _rms(y, np_ref[...])


def _combine(dest_tiles, ys_tiles, gates_t, x1, gt, npost, tm, tiles_per_seq):
    t = x1.shape[0]
    n_tiles = t // tm
    smem_tile = lambda f: pl.BlockSpec((None, 1, TOP_K * tm), f, memory_space=pltpu.SMEM)
    return pl.pallas_call(
        functools.partial(_combine_kernel, tm=tm),
        out_shape=jax.ShapeDtypeStruct((t, D_MODEL), F32),
        grid=(n_tiles,),
        in_specs=[smem_tile(lambda i: (i, 0, 0)),
                  smem_tile(lambda i: (jnp.minimum(i + 1, n_tiles - 1), 0, 0)),
                  pl.BlockSpec(memory_space=pl.ANY),
                  pl.BlockSpec((tm, TOP_K), lambda i: (i, 0)),
                  pl.BlockSpec((tm, D_MODEL), lambda i: (i, 0)),
                  _mod_spec(gt, tm, tiles_per_seq),
                  pl.BlockSpec((1, D_MODEL), lambda i: (0, 0))],
        out_specs=pl.BlockSpec((tm, D_MODEL), lambda i: (i, 0)),
        scratch_shapes=[pltpu.VMEM((2 * TOP_K * tm * LANE_CHUNKS, LANES), F32),
                        pltpu.SemaphoreType.DMA((2,))],
        compiler_params=_cparams(("arbitrary",)),
    )(dest_tiles, dest_tiles, ys_tiles, gates_t, x1, gt, npost)


def _expert_layout(counts, n_blocks):
    padded = (counts + MOE_BM - 1) // MOE_BM * MOE_BM
    pad_ends = jnp.cumsum(padded)
    pad_starts = pad_ends - padded
    block_e = jnp.minimum(
        jnp.searchsorted(pad_ends, jnp.arange(n_blocks, dtype=I32) * MOE_BM, side='right'),
        N_EXPERTS - 1).astype(I32)
    n_used = (pad_ends[-1:] // MOE_BM).astype(I32)
    return pad_starts, pad_starts + counts, pad_ends, block_e, n_used


def _slots(idx, rank, pad_starts):
    onehot = idx[:, :, None] == jnp.arange(N_EXPERTS, dtype=I32)[None, None, :]
    return rank + jnp.sum(jnp.where(onehot, pad_starts[None, None, :], 0), axis=-1)


def _tile_major(a, tm):
    t = a.shape[1]
    return a.reshape(TOP_K, t // tm, tm).transpose(1, 0, 2).reshape(t // tm, 1, TOP_K * tm)


def _block_diag(wb):
    n, d, _ = wb.shape
    eye = jnp.eye(n, dtype=wb.dtype)
    return (wb[:, :, None, :] * eye[:, None, :, None]).reshape(n * d, n * d)


def _pick_tile(n, prefs):
    for p in prefs:
        if n % p == 0:
            return p
    raise ValueError(f"no tile for {n}")


def kernel(x_prompt, x_sample, state_rglru_conv, state_rglru_h, state_hgrn_S, c_prompt, c_sample, ada_w, ada_b, norm_pre_mix, norm_post_mix, norm_pre_ffn, norm_post_ffn, w_in, conv_w, conv_b, lru_wa, lru_ba, lru_wi, lru_bi, lru_lambda, hg_lb, hg_norm, w_out, router_w, router_b, w_gu, b_gu, w_down, b_down):
    assert ada_w.shape[0] == 1, "single-layer trunk"
    bp, lp, _ = x_prompt.shape
    bs, ls, _ = x_sample.shape
    tp, ts = bp * lp, bs * ls
    assert CONV_WIDTH - 1 <= ls <= SUBLANES

    w_in_bf = w_in[0].astype(BF16)
    w_out_bf = w_out[0].astype(BF16)
    wg = jnp.concatenate([_block_diag(lru_wa[0]), _block_diag(lru_wi[0])], axis=1).astype(BF16)
    bg = jnp.concatenate([lru_ba[0], lru_bi[0]])[None, :]
    lbs = jnp.cumsum(jax.nn.softmax(hg_lb.astype(F32), axis=0), axis=0)[0][None, :]
    mw = (conv_w[0], conv_b[0][None, :], wg, bg, lru_lambda[0][None, :], lbs, hg_norm[0][None, :])
    row = lambda p: p[0][None, :]

    n_c = bp + bs
    n_c_pad = -(-n_c // SUBLANES) * SUBLANES
    c_all = jnp.concatenate([c_prompt, c_sample, jnp.zeros((n_c_pad - n_c, D_MODEL), F32)], axis=0)
    mod = _ada(c_all, ada_w[0], ada_b[0][None, :])
    mod_p = [m[:, None, :] for m in jnp.split(mod[:bp], 6, axis=-1)]
    mod_s = [jnp.repeat(m, ls, axis=0) for m in jnp.split(mod[bp:n_c], 6, axis=-1)]

    xp = x_prompt.reshape(tp, D_MODEL)
    xs = x_sample.reshape(ts, D_MODEL)
    tm_p = _pick_tile(lp, (256, 128, 64))
    tm_s = _pick_tile(ts, (256, 128, 64, 32, 16, 8))

    proj_p = _inproj(xp, mod_p[1], mod_p[0], row(norm_pre_mix), w_in_bf, tm_p, lp // tm_p)
    proj_s = _inproj(xs, mod_s[1], mod_s[0], row(norm_pre_mix), w_in_bf, tm_s, 1)
    tc = _pick_tile(lp, (256, 128, 64))
    m_p, conv_p, h_p, s_p = _mixer_prompt(proj_p, bp, lp, mw, tc)

    proj_s_pad = jnp.pad(proj_s.reshape(bs, ls, IN_COLS), ((0, 0), (0, SUBLANES - ls), (0, 0)))
    cst_hdr = jnp.pad(state_rglru_conv[0], ((0, 0), (CONV_HDR - (CONV_WIDTH - 1), 0), (0, 0)))
    m_s_pad, conv_s, h_s, s_s = _mixer_sample(proj_s_pad, cst_hdr, state_rglru_h[0][:, None, :],
                                              state_hgrn_S[0], mw, ls)
    m_s = m_s_pad[:, :ls, :].reshape(ts, D_MODEL)

    rwt = router_w[0].T
    rb = router_b[0][:, None]
    post = lambda m, x, md, cnt, tm, tps: _post(m, x, md[2], md[4], md[3], row(norm_post_mix), row(norm_pre_ffn),
                                                w_out_bf, rwt, rb, cnt, tm, tps)
    x1_p, hf_p, idx_p, gate_p, rank_p, cnt_p = post(m_p, xp, mod_p, jnp.zeros((N_EXPERTS, 1), F32),
                                                    tm_p, lp // tm_p)
    x1_s, hf_s, idx_s, gate_s, rank_s, cnt_all = post(m_s, xs, mod_s, cnt_p, tm_s, 1)

    n_blocks = -(-(TOP_K * (tp + ts)) // MOE_BM) + N_EXPERTS
    n_slots = n_blocks * MOE_BM
    pad_starts, fill_lo, fill_hi, block_e, n_used = _expert_layout(cnt_all[:, 0].astype(I32), n_blocks)
    dest_p = _slots(idx_p, rank_p, pad_starts)
    dest_s = _slots(idx_s, rank_s, pad_starts)
    tmd = _pick_tile(math.gcd(lp, ts), (128, 64, 32, 16, 8))
    dtile_p = _tile_major(dest_p, tmd)
    dtile_s = _tile_major(dest_s, tmd)
    tiles = lambda a: a.reshape(-1, LANE_CHUNKS, LANES)
    xs_rows = _dispatch(fill_lo, fill_hi, jnp.concatenate([dtile_p, dtile_s], axis=0), tiles(hf_p), tiles(hf_s),
                        n_slots, tmd)
    ys = _experts(block_e, n_used, xs_rows.reshape(n_slots * LANE_CHUNKS, LANES),
                  w_gu[0], b_gu[0][:, None, :], w_down[0], b_down[0][:, None, :])
    ys_tiles = tiles(ys)

    y_p = _combine(dtile_p, ys_tiles, gate_p.T, x1_p, mod_p[5], row(norm_post_ffn), tmd, lp // tmd)
    y_s = _combine(dtile_s, ys_tiles, gate_s.T, x1_s, mod_s[5], row(norm_post_ffn), tmd, 1)

    return (y_p.reshape(bp, lp, D_MODEL), y_s.reshape(bs, ls, D_MODEL),
            conv_p[None], h_p.reshape(1, bp, LRU_WIDTH), s_p[None],
            conv_s[None], h_s.reshape(1, bs, LRU_WIDTH), s_s[None])
```

```python
import functools
import math

import jax
import jax.numpy as jnp
from jax import lax
from jax.experimental import pallas as pl
from jax.experimental.pallas import tpu as pltpu

F32 = jnp.float32
BF16 = jnp.bfloat16
I32 = jnp.int32

D_MODEL = 1024
LRU_WIDTH = 512
LRU_BLOCKS = 8
LRU_BLOCK = LRU_WIDTH // LRU_BLOCKS
CONV_WIDTH = 4
LRU_C = 8.0
HG_WIDTH = 512
HG_HEAD_DIM = 128
HG_HEADS = 4
IN_COLS = 2 * LRU_WIDTH + 4 * HG_WIDTH
N_EXPERTS = 32
TOP_K = 4
D_EXPERT = 1024
SWIGLU_LIMIT = 7.0
SWIGLU_ALPHA = 1.702
RMS_EPS = 1e-6

C_XL, C_YL, C_Q, C_F, C_V, C_G = 0, 512, 1024, 1536, 2048, 2560

SUBLANES = 8
LANES = 128
LANE_CHUNKS = D_MODEL // LANES
CONV_HDR = SUBLANES
HG_CHUNK = 64
MOE_BM = 256
ZERO_ROWS = 64
VMEM_LIMIT = 56 * 1024 * 1024

_NT = (((1,), (1,)), ((), ()))
_TN = (((0,), (0,)), ((), ()))


def _cparams(sem):
    return pltpu.CompilerParams(dimension_semantics=sem, vmem_limit_bytes=VMEM_LIMIT)


def _rms(x, g):
    return x * lax.rsqrt(jnp.mean(x * x, axis=-1, keepdims=True) + RMS_EPS) * g


def _gelu_tanh(x):
    c = math.sqrt(2.0 / math.pi)
    return 0.5 * x * (1.0 + jnp.tanh(c * (x + 0.044715 * (x * x * x))))


def _ada_kernel(c_ref, w_ref, b_ref, o_ref):
    c = c_ref[...]
    s = c * jax.nn.sigmoid(c)
    o_ref[...] = jnp.dot(s.astype(BF16), w_ref[...].astype(BF16), preferred_element_type=F32) + b_ref[...]


def _ada(c_all, ada_w, ada_b):
    n = c_all.shape[0]
    tn = 1024
    return pl.pallas_call(
        _ada_kernel,
        out_shape=jax.ShapeDtypeStruct((n, 6 * D_MODEL), F32),
        grid=(6 * D_MODEL // tn,),
        in_specs=[pl.BlockSpec((n, D_MODEL), lambda j: (0, 0)),
                  pl.BlockSpec((D_MODEL, tn), lambda j: (0, j)),
                  pl.BlockSpec((1, tn), lambda j: (0, j))],
        out_specs=pl.BlockSpec((n, tn), lambda j: (0, j)),
        compiler_params=_cparams(("arbitrary",)),
    )(c_all, ada_w, ada_b)


def _inproj_kernel(x_ref, sc_ref, sh_ref, g_ref, w_ref, o_ref):
    h = _rms(x_ref[...], g_ref[...]) * (1.0 + sc_ref[...]) + sh_ref[...]
    o_ref[...] = jnp.dot(h.astype(BF16), w_ref[...], preferred_element_type=F32)


def _mod_spec(mod, tm, tiles_per_seq):
    if mod.ndim == 3:
        return pl.BlockSpec((None, 1, D_MODEL), lambda i: (i // tiles_per_seq, 0, 0))
    return pl.BlockSpec((tm, D_MODEL), lambda i: (i, 0))


def _inproj(x, sc, sh, g, w_bf, tm, tiles_per_seq):
    t = x.shape[0]
    return pl.pallas_call(
        _inproj_kernel,
        out_shape=jax.ShapeDtypeStruct((t, IN_COLS), F32),
        grid=(t // tm,),
        in_specs=[pl.BlockSpec((tm, D_MODEL), lambda i: (i, 0)),
                  _mod_spec(sc, tm, tiles_per_seq), _mod_spec(sh, tm, tiles_per_seq),
                  pl.BlockSpec((1, D_MODEL), lambda i: (0, 0)),
                  pl.BlockSpec((D_MODEL, IN_COLS), lambda i: (0, 0))],
        out_specs=pl.BlockSpec((tm, IN_COLS), lambda i: (i, 0)),
        compiler_params=_cparams(("arbitrary",)),
    )(x, sc, sh, g, w_bf)


def _lru_gates(xc, wg_ref, bg_ref, lam_ref):
    gates = jnp.dot(xc.astype(BF16), wg_ref[...], preferred_element_type=F32) + bg_ref[...]
    r = jax.nn.sigmoid(gates[:, :LRU_WIDTH])
    ig = jax.nn.sigmoid(gates[:, LRU_WIDTH:])
    z = -lam_ref[...]
    softplus = jnp.maximum(z, 0.0) + jnp.log1p(jnp.exp(-jnp.abs(z)))
    log_a = -LRU_C * r * softplus
    a = jnp.exp(log_a)
    th = jnp.tanh(log_a)
    mult2 = -2.0 * th / (1.0 - th)
    return a, jnp.sqrt(mult2), ig


def _conv4(xbuf, xl, cw_ref, cb_ref, rows):
    cw = cw_ref[...]
    h = CONV_HDR
    return (cb_ref[...] + cw[0:1] * xbuf[h - 3:h - 3 + rows, :] + cw[1:2] * xbuf[h - 2:h - 2 + rows, :]
            + cw[2:3] * xbuf[h - 1:h - 1 + rows, :] + cw[3:4] * xl)


def _head_norm_gate(o, gate, hgn):
    return _rms(o, hgn) * jax.nn.sigmoid(gate)


def _mixer_prompt_kernel(proj_ref, cw_ref, cb_ref, wg_ref, bg_ref, lam_ref, lb_ref, hgn_ref,
                         m_ref, conv_ref, h_ref, s_ref,
                         xbuf, hcar, st, obuf, *, tc):
    j = pl.program_id(1)
    w = LRU_WIDTH

    @pl.when(j == 0)
    def _():
        xbuf[0:CONV_HDR, :] = jnp.zeros((CONV_HDR, w), F32)
        hcar[...] = jnp.zeros_like(hcar)
        st[...] = jnp.zeros_like(st)

    xl = proj_ref[:, C_XL:C_XL + w]
    xbuf[CONV_HDR:CONV_HDR + tc, :] = xl
    xc = _conv4(xbuf, xl, cw_ref, cb_ref, tc)
    xbuf[0:CONV_HDR, :] = xbuf[tc:tc + CONV_HDR, :]
    a, mult, ig = _lru_gates(xc, wg_ref, bg_ref, lam_ref)
    row = lax.broadcasted_iota(I32, (tc, w), 0)
    mult = jnp.where((row == 0) & (j == 0), 1.0, mult)
    u = mult * ig * xc
    s = 1
    while s < tc:
        keep = row >= s
        a_sh = jnp.where(keep, pltpu.roll(a, s, 0), 1.0)
        u_sh = jnp.where(keep, pltpu.roll(u, s, 0), 0.0)
        u = a * u_sh + u
        a = a * a_sh
        s *= 2
    h = u + a * hcar[0:1, :]
    hcar[0:1, :] = h[tc - 1:tc, :]
    m_ref[:, 0:w] = (h * _gelu_tanh(proj_ref[:, C_YL:C_YL + w])).astype(BF16)

    lb = lb_ref[...]
    f = lb + (1.0 - lb) * jax.nn.sigmoid(proj_ref[:, C_F:C_F + w])
    k = 1.0 - f
    b = jnp.log(f)
    rowc = row & (HG_CHUNK - 1)
    s = 1
    while s < HG_CHUNK:
        b = b + jnp.where(rowc >= s, pltpu.roll(b, s, 0), 0.0)
        s *= 2
    q = proj_ref[:, C_Q:C_Q + w]
    v = proj_ref[:, C_V:C_V + w]
    tri = (lax.broadcasted_iota(I32, (HG_CHUNK, HG_CHUNK), 0)
           >= lax.broadcasted_iota(I32, (HG_CHUNK, HG_CHUNK), 1))
    mid = HG_CHUNK // 2
    for hd in range(HG_HEADS):
        cs = slice(hd * HG_HEAD_DIM, (hd + 1) * HG_HEAD_DIM)
        s_t = st[hd]
        for c in range(tc // HG_CHUNK):
            rs = slice(c * HG_CHUNK, (c + 1) * HG_CHUNK)
            bc, qc, kc = b[rs, cs], q[rs, cs], k[rs, cs]
            vb = v[rs, cs].astype(BF16)
            b_mid = bc[mid - 1:mid, :]
            b_end = bc[HG_CHUNK - 1:HG_CHUNK, :]
            qp = (qc * jnp.exp(bc - b_mid)).astype(BF16)
            kp = (kc * jnp.exp(b_mid - bc)).astype(BF16)
            att = lax.dot_general(qp, kp, _NT, preferred_element_type=F32)
            att = jnp.where(tri, att, 0.0)
            q_in = (qc * jnp.exp(bc)).astype(BF16)
            o = (jnp.dot(att.astype(BF16), vb, preferred_element_type=F32)
                 + lax.dot_general(q_in, s_t.astype(BF16), _NT, preferred_element_type=F32))
            k_out = (kc * jnp.exp(b_end - bc)).astype(BF16)
            s_t = s_t * jnp.exp(b_end) + lax.dot_general(vb, k_out, _TN, preferred_element_type=F32)
            obuf[rs, cs] = o
        st[hd] = s_t
    hgn = hgn_ref[...]
    for hd in range(HG_HEADS):
        cs = slice(hd * HG_HEAD_DIM, (hd + 1) * HG_HEAD_DIM)
        gate = proj_ref[:, C_G + hd * HG_HEAD_DIM:C_G + (hd + 1) * HG_HEAD_DIM]
        m_ref[:, w + hd * HG_HEAD_DIM:w + (hd + 1) * HG_HEAD_DIM] = (
            _head_norm_gate(obuf[:, cs], gate, hgn).astype(BF16))

    @pl.when(j == pl.num_programs(1) - 1)
    def _():
        conv_ref[...] = xbuf[CONV_HDR - 3:CONV_HDR, :]
        h_ref[...] = hcar[0:1, :]
        for hd in range(HG_HEADS):
            s_ref[hd] = st[hd].T


def _mixer_prompt(proj, nb, seq, mw, tc):
    cw, cb, wg, bg, lam, lb, hgn = mw
    nj = seq // tc
    const = lambda shape: pl.BlockSpec(shape, lambda b, j: tuple(0 for _ in shape))
    return pl.pallas_call(
        functools.partial(_mixer_prompt_kernel, tc=tc),
        out_shape=(jax.ShapeDtypeStruct((nb * seq, D_MODEL), BF16),
                   jax.ShapeDtypeStruct((nb, CONV_WIDTH - 1, LRU_WIDTH), F32),
                   jax.ShapeDtypeStruct((nb, 1, LRU_WIDTH), F32),
                   jax.ShapeDtypeStruct((nb, HG_HEADS, HG_HEAD_DIM, HG_HEAD_DIM), F32)),
        grid=(nb, nj),
        in_specs=[pl.BlockSpec((tc, IN_COLS), lambda b, j: (b * nj + j, 0)),
                  const((CONV_WIDTH, LRU_WIDTH)), const((1, LRU_WIDTH)),
                  const((LRU_WIDTH, 2 * LRU_WIDTH)), const((1, 2 * LRU_WIDTH)),
                  const((1, LRU_WIDTH)), const((1, HG_WIDTH)), const((1, HG_HEAD_DIM))],
        out_specs=(pl.BlockSpec((tc, D_MODEL), lambda b, j: (b * nj + j, 0)),
                   pl.BlockSpec((None, CONV_WIDTH - 1, LRU_WIDTH), lambda b, j: (b, 0, 0)),
                   pl.BlockSpec((None, 1, LRU_WIDTH), lambda b, j: (b, 0, 0)),
                   pl.BlockSpec((None, HG_HEADS, HG_HEAD_DIM, HG_HEAD_DIM), lambda b, j: (b, 0, 0, 0))),
        scratch_shapes=[pltpu.VMEM((CONV_HDR + tc, LRU_WIDTH), F32),
                        pltpu.VMEM((SUBLANES, LRU_WIDTH), F32),
                        pltpu.VMEM((HG_HEADS, HG_HEAD_DIM, HG_HEAD_DIM), F32),
                        pltpu.VMEM((tc, HG_WIDTH), F32)],
        compiler_params=_cparams(("arbitrary", "arbitrary")),
    )(proj, cw, cb, wg, bg, lam, lb, hgn)


def _mixer_sample_kernel(proj_ref, cst_ref, h0_ref, s0_ref, cw_ref, cb_ref, wg_ref, bg_ref, lam_ref, lb_ref,
                         hgn_ref, m_ref, conv_ref, h_ref, s_ref, xbuf, *, steps):
    w = LRU_WIDTH
    rows = SUBLANES
    row = lax.broadcasted_iota(I32, (rows, w), 0)
    valid = row < steps

    xl = proj_ref[:, C_XL:C_XL + w]
    xbuf[0:CONV_HDR, :] = cst_ref[...]
    xbuf[CONV_HDR:CONV_HDR + rows, :] = xl
    xc = _conv4(xbuf, xl, cw_ref, cb_ref, rows)
    conv_ref[...] = xbuf[CONV_HDR + steps - 3:CONV_HDR + steps, :]
    a, mult, ig = _lru_gates(xc, wg_ref, bg_ref, lam_ref)
    u = mult * ig * xc
    h = h0_ref[...]
    hs = jnp.zeros((rows, w), F32)
    for t in range(steps):
        h = a[t:t + 1, :] * h + u[t:t + 1, :]
        hs = jnp.where(row == t, h, hs)
    h_ref[...] = h
    m_ref[:, 0:w] = (hs * _gelu_tanh(proj_ref[:, C_YL:C_YL + w])).astype(BF16)

    lb = lb_ref[...]
    f = lb + (1.0 - lb) * jax.nn.sigmoid(proj_ref[:, C_F:C_F + w])
    k = 1.0 - f
    b = jnp.log(f)
    s = 1
    while s < rows:
        b = b + jnp.where(row >= s, pltpu.roll(b, s, 0), 0.0)
        s *= 2
    q = proj_ref[:, C_Q:C_Q + w]
    v = proj_ref[:, C_V:C_V + w]
    b_end = b[steps - 1:steps, :]
    q_in = jnp.where(valid, q * jnp.exp(b), 0.0)
    k_out = jnp.where(valid, k * jnp.exp(jnp.where(valid, b_end - b, 0.0)), 0.0)
    v_ok = jnp.where(valid, v, 0.0)
    hgn = hgn_ref[...]
    row_h = lax.broadcasted_iota(I32, (rows, HG_HEAD_DIM), 0)
    for hd in range(HG_HEADS):
        cs = slice(hd * HG_HEAD_DIM, (hd + 1) * HG_HEAD_DIM)
        s_t = s0_ref[hd].T
        o = lax.dot_general(q_in[:, cs].astype(BF16), s_t.astype(BF16), _NT, preferred_element_type=F32)
        for d in range(steps):
            ok = (row_h < steps) & (row_h >= d)
            k_d = k[:, cs] if d == 0 else pltpu.roll(k[:, cs], d, 0)
            v_d = v[:, cs] if d == 0 else pltpu.roll(v[:, cs], d, 0)
            b_d = b[:, cs] if d == 0 else pltpu.roll(b[:, cs], d, 0)
            decay = jnp.exp(jnp.where(ok, b[:, cs] - b_d, 0.0))
            wgt = jnp.sum(jnp.where(ok, q[:, cs] * k_d * decay, 0.0), axis=-1, keepdims=True)
            o = o + wgt * v_d
        s_new = s_t * jnp.exp(b_end[:, cs]) + lax.dot_general(
            v_ok[:, cs].astype(BF16), k_out[:, cs].astype(BF16), _TN, preferred_element_type=F32)
        s_ref[hd] = s_new.T
        gate = proj_ref[:, C_G + hd * HG_HEAD_DIM:C_G + (hd + 1) * HG_HEAD_DIM]
        m_ref[:, w + hd * HG_HEAD_DIM:w + (hd + 1) * HG_HEAD_DIM] = _head_norm_gate(o, gate, hgn).astype(BF16)


def _mixer_sample(proj_pad, cst_hdr, h0, s0, mw, steps):
    cw, cb, wg, bg, lam, lb, hgn = mw
    nb = proj_pad.shape[0]
    const = lambda shape: pl.BlockSpec(shape, lambda b: tuple(0 for _ in shape))
    per_b = lambda shape: pl.BlockSpec((None,) + shape, lambda b: (b,) + tuple(0 for _ in shape))
    return pl.pallas_call(
        functools.partial(_mixer_sample_kernel, steps=steps),
        out_shape=(jax.ShapeDtypeStruct((nb, SUBLANES, D_MODEL), BF16),
                   jax.ShapeDtypeStruct((nb, CONV_WIDTH - 1, LRU_WIDTH), F32),
                   jax.ShapeDtypeStruct((nb, 1, LRU_WIDTH), F32),
                   jax.ShapeDtypeStruct((nb, HG_HEADS, HG_HEAD_DIM, HG_HEAD_DIM), F32)),
        grid=(nb,),
        in_specs=[per_b((SUBLANES, IN_COLS)), per_b((CONV_HDR, LRU_WIDTH)), per_b((1, LRU_WIDTH)),
                  per_b((HG_HEADS, HG_HEAD_DIM, HG_HEAD_DIM)),
                  const((CONV_WIDTH, LRU_WIDTH)), const((1, LRU_WIDTH)),
                  const((LRU_WIDTH, 2 * LRU_WIDTH)), const((1, 2 * LRU_WIDTH)),
                  const((1, LRU_WIDTH)), const((1, HG_WIDTH)), const((1, HG_HEAD_DIM))],
        out_specs=(per_b((SUBLANES, D_MODEL)), per_b((CONV_WIDTH - 1, LRU_WIDTH)), per_b((1, LRU_WIDTH)),
                   per_b((HG_HEADS, HG_HEAD_DIM, HG_HEAD_DIM))),
        scratch_shapes=[pltpu.VMEM((CONV_HDR + SUBLANES, LRU_WIDTH), F32)],
        compiler_params=_cparams(("arbitrary",)),
    )(proj_pad, cst_hdr, h0, s0, cw, cb, wg, bg, lam, lb, hgn)


def _to_row_tiles(ref, val, rows):
    for c in range(LANE_CHUNKS):
        ref[pl.ds(c, rows, stride=LANE_CHUNKS), :] = val[:, c * LANES:(c + 1) * LANES]


def _from_row_tiles(ref, base, rows):
    return jnp.concatenate(
        [ref[pl.ds(base + c, rows, stride=LANE_CHUNKS), :] for c in range(LANE_CHUNKS)], axis=-1)


def _post_kernel(m_ref, x_ref, gtm_ref, scf_ref, shf_ref, npost_ref, npre_ref, wout_ref, rwt_ref, rb_ref,
                 cnt_in_ref, x1_ref, hf_ref, idx_ref, gate_ref, rank_ref, cnt_out_ref, cnt):
    i = pl.program_id(0)
    tm = x_ref.shape[0]

    @pl.when(i == 0)
    def _():
        cnt[...] = cnt_in_ref[...]

    mix = jnp.dot(m_ref[...], wout_ref[...], preferred_element_type=F32)
    x1 = x_ref[...] + gtm_ref[...] * _rms(mix, npost_ref[...])
    x1_ref[...] = x1
    hf = _rms(x1, npre_ref[...]) * (1.0 + scf_ref[...]) + shf_ref[...]
    _to_row_tiles(hf_ref, hf, tm)
    logits = lax.dot_general(rwt_ref[...], hf, _NT, precision=lax.Precision.HIGHEST,
                             preferred_element_type=F32) + rb_ref[...]
    e_iota = lax.broadcasted_iota(I32, logits.shape, 0)
    vals, idxs = [], []
    for _ in range(TOP_K):
        mx = jnp.max(logits, axis=0, keepdims=True)
        ix = jnp.min(jnp.where(logits == mx, e_iota, N_EXPERTS), axis=0, keepdims=True)
        vals.append(mx)
        idxs.append(ix)
        logits = jnp.where(e_iota == ix, -jnp.inf, logits)
    exps = [jnp.exp(vv - vals[0]) for vv in vals]
    den = exps[0] + exps[1] + exps[2] + exps[3]
    earlier = (lax.broadcasted_iota(I32, (tm, tm), 0) < lax.broadcasted_iota(I32, (tm, tm), 1))
    earlier = jnp.where(earlier, 1.0, 0.0).astype(BF16)
    base = cnt[...]
    for kk in range(TOP_K):
        idx_ref[kk:kk + 1, :] = idxs[kk]
        gate_ref[kk:kk + 1, :] = exps[kk] / den
        hit = e_iota == idxs[kk]
        onehot = jnp.where(hit, 1.0, 0.0)
        before = jnp.dot(onehot.astype(BF16), earlier, preferred_element_type=F32)
        rank = jnp.sum(jnp.where(hit, before + base, 0.0), axis=0, keepdims=True)
        rank_ref[kk:kk + 1, :] = rank.astype(I32)
        base = base + jnp.sum(onehot, axis=1, keepdims=True)
    cnt[...] = base

    @pl.when(i == pl.num_programs(0) - 1)
    def _():
        cnt_out_ref[...] = base


def _post(m, x, gtm, scf, shf, npost, npre, wout_bf, rwt, rb, cnt_in, tm, tiles_per_seq):
    t = x.shape[0]
    row = lambda: pl.BlockSpec((1, D_MODEL), lambda i: (0, 0))
    tok = lambda: pl.BlockSpec((tm, D_MODEL), lambda i: (i, 0))
    per_k = lambda: pl.BlockSpec((TOP_K, tm), lambda i: (0, i))
    cnt_spec = lambda: pl.BlockSpec((N_EXPERTS, 1), lambda i: (0, 0))
    return pl.pallas_call(
        _post_kernel,
        out_shape=(jax.ShapeDtypeStruct((t, D_MODEL), F32),
                   jax.ShapeDtypeStruct((t * LANE_CHUNKS, LANES), F32),
                   jax.ShapeDtypeStruct((TOP_K, t), I32), jax.ShapeDtypeStruct((TOP_K, t), F32),
                   jax.ShapeDtypeStruct((TOP_K, t), I32), jax.ShapeDtypeStruct((N_EXPERTS, 1), F32)),
        grid=(t // tm,),
        in_specs=[tok(), tok(),
                  _mod_spec(gtm, tm, tiles_per_seq), _mod_spec(scf, tm, tiles_per_seq),
                  _mod_spec(shf, tm, tiles_per_seq), row(), row(),
                  pl.BlockSpec((D_MODEL, D_MODEL), lambda i: (0, 0)),
                  pl.BlockSpec((N_EXPERTS, D_MODEL), lambda i: (0, 0)),
                  cnt_spec(), cnt_spec()],
        out_specs=(tok(), pl.BlockSpec((tm * LANE_CHUNKS, LANES), lambda i: (i, 0)),
                   per_k(), per_k(), per_k(), cnt_spec()),
        scratch_shapes=[pltpu.VMEM((N_EXPERTS, 1), F32)],
        compiler_params=_cparams(("arbitrary",)),
    )(m, x, gtm, scf, shf, npost, npre, wout_bf, rwt, rb, cnt_in)


def _dispatch_kernel(fill_lo_ref, fill_hi_ref, dest_ref, hfp_ref, hfs_ref, xs_hbm, zeros, row_sem, pad_sem,
                     *, tm, n_tiles_p, n_slots):
    i = pl.program_id(0)
    n_rows = TOP_K * tm

    def wait_rows():
        pltpu.make_async_copy(xs_hbm.at[pl.ds(0, n_rows)], xs_hbm.at[pl.ds(0, n_rows)], row_sem).wait()

    @pl.when(i == 0)
    def _():
        zeros[...] = jnp.zeros_like(zeros)

        def per_expert(e, total):
            lo, hi = fill_lo_ref[e], fill_hi_ref[e]

            def per_slot(s, carry):
                pltpu.make_async_copy(zeros.at[0], xs_hbm.at[s], pad_sem).start()
                return carry
            lax.fori_loop(lo, hi, per_slot, 0)
            return total + (hi - lo)
        n_fill = lax.fori_loop(0, N_EXPERTS, per_expert, 0)
        tail_lo = fill_hi_ref[N_EXPERTS - 1] // ZERO_ROWS
        n_tail = n_slots // ZERO_ROWS - tail_lo

        def per_tail(c, carry):
            pltpu.make_async_copy(zeros, xs_hbm.at[pl.ds((tail_lo + c) * ZERO_ROWS, ZERO_ROWS)], pad_sem).start()
            return carry
        lax.fori_loop(0, n_tail, per_tail, 0)

        def drain_slot(s, carry):
            pltpu.make_async_copy(zeros.at[0], xs_hbm.at[0], pad_sem).wait()
            return carry
        lax.fori_loop(0, n_fill, drain_slot, 0)

        def drain_tail(c, carry):
            pltpu.make_async_copy(zeros, xs_hbm.at[pl.ds(0, ZERO_ROWS)], pad_sem).wait()
            return carry
        lax.fori_loop(0, n_tail, drain_tail, 0)

    def issue(hf_ref):
        for n in range(n_rows):
            pltpu.make_async_copy(hf_ref.at[pl.ds((n % tm) * LANE_CHUNKS, LANE_CHUNKS), :],
                                  xs_hbm.at[dest_ref[0, n]], row_sem).start(priority=n % 2)

    @pl.when(i < n_tiles_p)
    def _():
        issue(hfp_ref)

    @pl.when(i >= n_tiles_p)
    def _():
        issue(hfs_ref)

    wait_rows()


def _dispatch(fill_lo, fill_hi, dest_tiles, hfp_rows, hfs_rows, n_slots, tm):
    n_tiles = dest_tiles.shape[0]
    n_tiles_p = hfp_rows.shape[0] // (tm * LANE_CHUNKS)
    return pl.pallas_call(
        functools.partial(_dispatch_kernel, tm=tm, n_tiles_p=n_tiles_p, n_slots=n_slots),
        out_shape=jax.ShapeDtypeStruct((n_slots, LANE_CHUNKS, LANES), F32),
        grid_spec=pltpu.PrefetchScalarGridSpec(
            num_scalar_prefetch=2, grid=(n_tiles,),
            in_specs=[pl.BlockSpec((None, 1, TOP_K * tm), lambda i, lo, hi: (i, 0, 0), memory_space=pltpu.SMEM),
                      pl.BlockSpec((tm * LANE_CHUNKS, LANES), lambda i, lo, hi: (jnp.minimum(i, n_tiles_p - 1), 0)),
                      pl.BlockSpec((tm * LANE_CHUNKS, LANES), lambda i, lo, hi: (jnp.maximum(i - n_tiles_p, 0), 0))],
            out_specs=pl.BlockSpec(memory_space=pl.ANY),
            scratch_shapes=[pltpu.VMEM((ZERO_ROWS, LANE_CHUNKS, LANES), F32),
                            pltpu.SemaphoreType.DMA, pltpu.SemaphoreType.DMA]),
        compiler_params=_cparams(("arbitrary",)),
    )(fill_lo, fill_hi, dest_tiles, hfp_rows, hfs_rows)


def _expert_kernel(be_ref, nu_ref, xs_ref, wgu_ref, bgu_ref, wd_ref, bd_ref, ys_ref, wgu_bf, wd_bf):
    i = pl.program_id(0)

    @pl.when(i < nu_ref[0])
    def _():
        prev = be_ref[jnp.maximum(i - 1, 0)]

        @pl.when((i == 0) | (be_ref[i] != prev))
        def _():
            wgu_bf[...] = wgu_ref[...].astype(BF16)
            wd_bf[...] = wd_ref[...].astype(BF16)

        x = _from_row_tiles(xs_ref, 0, MOE_BM).astype(BF16)
        hu = jnp.dot(x, wgu_bf[...], preferred_element_type=F32) + bgu_ref[...]
        glu = jnp.minimum(hu[:, :D_EXPERT], SWIGLU_LIMIT)
        lin = jnp.clip(hu[:, D_EXPERT:], -SWIGLU_LIMIT, SWIGLU_LIMIT)
        act = glu * jax.nn.sigmoid(SWIGLU_ALPHA * glu) * (lin + 1.0)
        y = jnp.dot(act.astype(BF16), wd_bf[...], preferred_element_type=F32) + bd_ref[...]
        _to_row_tiles(ys_ref, y, MOE_BM)

    @pl.when(i >= nu_ref[0])
    def _():
        ys_ref[...] = jnp.zeros_like(ys_ref)


def _experts(block_e, n_used, xs2d, w_gu, b_gu, w_down, b_down):
    n_blocks = block_e.shape[0]
    rows = MOE_BM * LANE_CHUNKS
    live = lambda i, be, nu: (jnp.minimum(i, nu[0] - 1), 0)
    return pl.pallas_call(
        _expert_kernel,
        out_shape=jax.ShapeDtypeStruct(xs2d.shape, F32),
        grid_spec=pltpu.PrefetchScalarGridSpec(
            num_scalar_prefetch=2,
            grid=(n_blocks,),
            in_specs=[pl.BlockSpec((rows, LANES), live),
                      pl.BlockSpec((None, D_MODEL, 2 * D_EXPERT), lambda i, be, nu: (be[i], 0, 0)),
                      pl.BlockSpec((None, 1, 2 * D_EXPERT), lambda i, be, nu: (be[i], 0, 0)),
                      pl.BlockSpec((None, D_EXPERT, D_MODEL), lambda i, be, nu: (be[i], 0, 0)),
                      pl.BlockSpec((None, 1, D_MODEL), lambda i, be, nu: (be[i], 0, 0))],
            out_specs=pl.BlockSpec((rows, LANES), lambda i, be, nu: (i, 0)),
            scratch_shapes=[pltpu.VMEM((D_MODEL, 2 * D_EXPERT), BF16),
                            pltpu.VMEM((D_EXPERT, D_MODEL), BF16)]),
        compiler_params=_cparams(("arbitrary",)),
    )(block_e, n_used, xs2d, w_gu, b_gu, w_down, b_down)


def _combine_kernel(dcur_ref, dnext_ref, ys_hbm, g_ref, x1_ref, gt_ref, np_ref, o_ref, ybuf, row_sem, *, tm):
    i = pl.program_id(0)
    n_rows = TOP_K * tm
    slot = i % 2

    def slot_rows(s):
        return ybuf.at[pl.ds(pl.multiple_of(s * n_rows * LANE_CHUNKS, n_rows * LANE_CHUNKS), n_rows * LANE_CHUNKS), :]

    def issue(d_ref, s):
        dst = slot_rows(s)
        for n in range(n_rows):
            pltpu.make_async_copy(ys_hbm.at[d_ref[0, n]], dst.at[pl.ds(n * LANE_CHUNKS, LANE_CHUNKS), :],
                                  row_sem.at[s]).start(priority=n % 2)

    @pl.when(i == 0)
    def _():
        issue(dcur_ref, 0)

    @pl.when(i + 1 < pl.num_programs(0))
    def _():
        issue(dnext_ref, 1 - slot)

    pltpu.make_async_copy(slot_rows(slot), slot_rows(slot), row_sem.at[slot]).wait()
    g = g_ref[...]
    y = None
    for kk in range(TOP_K):
        rows = _from_row_tiles(ybuf, (slot * n_rows + kk * tm) * LANE_CHUNKS, tm)
        y = g[:, kk:kk + 1] * rows if y is None else y + g[:, kk:kk + 1] * rows
    o_ref[...] = x1_ref[...] + gt_ref[...] * _rms(y, np_ref[...])


def _combine(dest_tiles, ys_tiles, gates_t, x1, gt, npost, tm, tiles_per_seq):
    t = x1.shape[0]
    n_tiles = t // tm
    smem_tile = lambda f: pl.BlockSpec((None, 1, TOP_K * tm), f, memory_space=pltpu.SMEM)
    return pl.pallas_call(
        functools.partial(_combine_kernel, tm=tm),
        out_shape=jax.ShapeDtypeStruct((t, D_MODEL), F32),
        grid=(n_tiles,),
        in_specs=[smem_tile(lambda i: (i, 0, 0)),
                  smem_tile(lambda i: (jnp.minimum(i + 1, n_tiles - 1), 0, 0)),
                  pl.BlockSpec(memory_space=pl.ANY),
                  pl.BlockSpec((tm, TOP_K), lambda i: (i, 0)),
                  pl.BlockSpec((tm, D_MODEL), lambda i: (i, 0)),
                  _mod_spec(gt, tm, tiles_per_seq),
                  pl.BlockSpec((1, D_MODEL), lambda i: (0, 0))],
        out_specs=pl.BlockSpec((tm, D_MODEL), lambda i: (i, 0)),
        scratch_shapes=[pltpu.VMEM((2 * TOP_K * tm * LANE_CHUNKS, LANES), F32),
                        pltpu.SemaphoreType.DMA((2,))],
        compiler_params=_cparams(("arbitrary",)),
    )(dest_tiles, dest_tiles, ys_tiles, gates_t, x1, gt, npost)


def _expert_layout(counts, n_blocks):
    padded = (counts + MOE_BM - 1) // MOE_BM * MOE_BM
    pad_ends = jnp.cumsum(padded)
    pad_starts = pad_ends - padded
    block_start = jnp.arange(n_blocks, dtype=I32) * MOE_BM
    block_e = jnp.minimum(jnp.sum((pad_ends[None, :] <= block_start[:, None]).astype(I32), axis=1), N_EXPERTS - 1)
    n_used = (pad_ends[-1:] // MOE_BM).astype(I32)
    return pad_starts, pad_starts + counts, pad_ends, block_e, n_used


def _slots(idx, rank, pad_starts):
    onehot = idx[:, :, None] == jnp.arange(N_EXPERTS, dtype=I32)[None, None, :]
    return rank + jnp.sum(jnp.where(onehot, pad_starts[None, None, :], 0), axis=-1)


def _tile_major(a, tm):
    t = a.shape[1]
    return a.reshape(TOP_K, t // tm, tm).transpose(1, 0, 2).reshape(t // tm, 1, TOP_K * tm)


def _block_diag(wb):
    n, d, _ = wb.shape
    eye = jnp.eye(n, dtype=wb.dtype)
    return (wb[:, :, None, :] * eye[:, None, :, None]).reshape(n * d, n * d)


def _pick_tile(n, prefs):
    for p in prefs:
        if n % p == 0:
            return p
    raise ValueError(f"no tile for {n}")


def kernel(x_prompt, x_sample, state_rglru_conv, state_rglru_h, state_hgrn_S, c_prompt, c_sample, ada_w, ada_b, norm_pre_mix, norm_post_mix, norm_pre_ffn, norm_post_ffn, w_in, conv_w, conv_b, lru_wa, lru_ba, lru_wi, lru_bi, lru_lambda, hg_lb, hg_norm, w_out, router_w, router_b, w_gu, b_gu, w_down, b_down):
    assert ada_w.shape[0] == 1, "single-layer trunk"
    bp, lp, _ = x_prompt.shape
    bs, ls, _ = x_sample.shape
    tp, ts = bp * lp, bs * ls
    assert CONV_WIDTH - 1 <= ls <= SUBLANES

    w_in_bf = w_in[0].astype(BF16)
    w_out_bf = w_out[0].astype(BF16)
    wg = jnp.concatenate([_block_diag(lru_wa[0]), _block_diag(lru_wi[0])], axis=1).astype(BF16)
    bg = jnp.concatenate([lru_ba[0], lru_bi[0]])[None, :]
    lbs = jnp.cumsum(jax.nn.softmax(hg_lb.astype(F32), axis=0), axis=0)[0][None, :]
    mw = (conv_w[0], conv_b[0][None, :], wg, bg, lru_lambda[0][None, :], lbs, hg_norm[0][None, :])
    row = lambda p: p[0][None, :]

    n_c = bp + bs
    n_c_pad = -(-n_c // SUBLANES) * SUBLANES
    c_all = jnp.concatenate([c_prompt, c_sample, jnp.zeros((n_c_pad - n_c, D_MODEL), F32)], axis=0)
    mod = _ada(c_all, ada_w[0], ada_b[0][None, :])
    mod_p = [m[:, None, :] for m in jnp.split(mod[:bp], 6, axis=-1)]
    mod_s = [jnp.repeat(m, ls, axis=0) for m in jnp.split(mod[bp:n_c], 6, axis=-1)]

    xp = x_prompt.reshape(tp, D_MODEL)
    xs = x_sample.reshape(ts, D_MODEL)
    tm_p = _pick_tile(lp, (256, 128, 64))
    tm_s = _pick_tile(ts, (256, 128, 64, 32, 16, 8))

    proj_p = _inproj(xp, mod_p[1], mod_p[0], row(norm_pre_mix), w_in_bf, tm_p, lp // tm_p)
    proj_s = _inproj(xs, mod_s[1], mod_s[0], row(norm_pre_mix), w_in_bf, tm_s, 1)
    tc = _pick_tile(lp, (256, 128, 64))
    m_p, conv_p, h_p, s_p = _mixer_prompt(proj_p, bp, lp, mw, tc)

    proj_s_pad = jnp.pad(proj_s.reshape(bs, ls, IN_COLS), ((0, 0), (0, SUBLANES - ls), (0, 0)))
    cst_hdr = jnp.pad(state_rglru_conv[0], ((0, 0), (CONV_HDR - (CONV_WIDTH - 1), 0), (0, 0)))
    m_s_pad, conv_s, h_s, s_s = _mixer_sample(proj_s_pad, cst_hdr, state_rglru_h[0][:, None, :],
                                              state_hgrn_S[0], mw, ls)
    m_s = m_s_pad[:, :ls, :].reshape(ts, D_MODEL)

    rwt = router_w[0].T
    rb = router_b[0][:, None]
    post = lambda m, x, md, cnt, tm, tps: _post(m, x, md[2], md[4], md[3], row(norm_post_mix), row(norm_pre_ffn),
                                                w_out_bf, rwt, rb, cnt, tm, tps)
    x1_p, hf_p, idx_p, gate_p, rank_p, cnt_p = post(m_p, xp, mod_p, jnp.zeros((N_EXPERTS, 1), F32),
                                                    tm_p, lp // tm_p)
    x1_s, hf_s, idx_s, gate_s, rank_s, cnt_all = post(m_s, xs, mod_s, cnt_p, tm_s, 1)

    n_blocks = -(-(TOP_K * (tp + ts)) // MOE_BM) + N_EXPERTS
    n_slots = n_blocks * MOE_BM
    pad_starts, fill_lo, fill_hi, block_e, n_used = _expert_layout(cnt_all[:, 0].astype(I32), n_blocks)
    dest_p = _slots(idx_p, rank_p, pad_starts)
    dest_s = _slots(idx_s, rank_s, pad_starts)
    tmd = _pick_tile(math.gcd(lp, ts), (128, 64, 32, 16, 8))
    dtile_p = _tile_major(dest_p, tmd)
    dtile_s = _tile_major(dest_s, tmd)
    tiles = lambda a: a.reshape(-1, LANE_CHUNKS, LANES)
    xs_rows = _dispatch(fill_lo, fill_hi, jnp.concatenate([dtile_p, dtile_s], axis=0), hf_p, hf_s, n_slots, tmd)
    ys = _experts(block_e, n_used, xs_rows.reshape(n_slots * LANE_CHUNKS, LANES),
                  w_gu[0], b_gu[0][:, None, :], w_down[0], b_down[0][:, None, :])
    ys_tiles = tiles(ys)

    y_p = _combine(dtile_p, ys_tiles, gate_p.T, x1_p, mod_p[5], row(norm_post_ffn), tmd, lp // tmd)
    y_s = _combine(dtile_s, ys_tiles, gate_s.T, x1_s, mod_s[5], row(norm_post_ffn), tmd, 1)

    return (y_p.reshape(bp, lp, D_MODEL), y_s.reshape(bs, ls, D_MODEL),
            conv_p[None], h_p.reshape(1, bp, LRU_WIDTH), s_p[None],
            conv_s[None], h_s.reshape(1, bs, LRU_WIDTH), s_s[None])
```

```python
import functools
import math

import jax
import jax.numpy as jnp
from jax import lax
from jax.experimental import pallas as pl
from jax.experimental.pallas import tpu as pltpu

F32 = jnp.float32
BF16 = jnp.bfloat16
I32 = jnp.int32

D_MODEL = 1024
LRU_WIDTH = 512
LRU_BLOCKS = 8
LRU_BLOCK = LRU_WIDTH // LRU_BLOCKS
CONV_WIDTH = 4
LRU_C = 8.0
HG_WIDTH = 512
HG_HEAD_DIM = 128
HG_HEADS = 4
IN_COLS = 2 * LRU_WIDTH + 4 * HG_WIDTH
N_EXPERTS = 32
TOP_K = 4
D_EXPERT = 1024
SWIGLU_LIMIT = 7.0
SWIGLU_ALPHA = 1.702
RMS_EPS = 1e-6

C_XL, C_YL, C_Q, C_F, C_V, C_G = 0, 512, 1024, 1536, 2048, 2560

SUBLANES = 8
LANES = 128
LANE_CHUNKS = D_MODEL // LANES
CONV_HDR = SUBLANES
HG_CHUNK = 64
MOE_BM = 256
ZERO_ROWS = 64
VMEM_LIMIT = 56 * 1024 * 1024

_NT = (((1,), (1,)), ((), ()))
_TN = (((0,), (0,)), ((), ()))


def _cparams(sem):
    return pltpu.CompilerParams(dimension_semantics=sem, vmem_limit_bytes=VMEM_LIMIT)


def _rms(x, g):
    return x * lax.rsqrt(jnp.mean(x * x, axis=-1, keepdims=True) + RMS_EPS) * g


def _gelu_tanh(x):
    c = math.sqrt(2.0 / math.pi)
    return 0.5 * x * (1.0 + jnp.tanh(c * (x + 0.044715 * (x * x * x))))


def _ada_kernel(c_ref, w_ref, b_ref, o_ref):
    c = c_ref[...]
    s = c * jax.nn.sigmoid(c)
    o_ref[...] = jnp.dot(s.astype(BF16), w_ref[...].astype(BF16), preferred_element_type=F32) + b_ref[...]


def _ada(c_all, ada_w, ada_b):
    n = c_all.shape[0]
    tn = 1024
    return pl.pallas_call(
        _ada_kernel,
        out_shape=jax.ShapeDtypeStruct((n, 6 * D_MODEL), F32),
        grid=(6 * D_MODEL // tn,),
        in_specs=[pl.BlockSpec((n, D_MODEL), lambda j: (0, 0)),
                  pl.BlockSpec((D_MODEL, tn), lambda j: (0, j)),
                  pl.BlockSpec((1, tn), lambda j: (0, j))],
        out_specs=pl.BlockSpec((n, tn), lambda j: (0, j)),
        compiler_params=_cparams(("arbitrary",)),
    )(c_all, ada_w, ada_b)


def _inproj_kernel(x_ref, sc_ref, sh_ref, g_ref, w_ref, o_ref):
    h = _rms(x_ref[...], g_ref[...]) * (1.0 + sc_ref[...]) + sh_ref[...]
    o_ref[...] = jnp.dot(h.astype(BF16), w_ref[...], preferred_element_type=F32)


def _mod_spec(mod, tm, tiles_per_seq):
    if mod.ndim == 3:
        return pl.BlockSpec((None, 1, D_MODEL), lambda i: (i // tiles_per_seq, 0, 0))
    return pl.BlockSpec((tm, D_MODEL), lambda i: (i, 0))


def _inproj(x, sc, sh, g, w_bf, tm, tiles_per_seq):
    t = x.shape[0]
    return pl.pallas_call(
        _inproj_kernel,
        out_shape=jax.ShapeDtypeStruct((t, IN_COLS), F32),
        grid=(t // tm,),
        in_specs=[pl.BlockSpec((tm, D_MODEL), lambda i: (i, 0)),
                  _mod_spec(sc, tm, tiles_per_seq), _mod_spec(sh, tm, tiles_per_seq),
                  pl.BlockSpec((1, D_MODEL), lambda i: (0, 0)),
                  pl.BlockSpec((D_MODEL, IN_COLS), lambda i: (0, 0))],
        out_specs=pl.BlockSpec((tm, IN_COLS), lambda i: (i, 0)),
        compiler_params=_cparams(("arbitrary",)),
    )(x, sc, sh, g, w_bf)


def _sigmoid(x):
    return 0.5 * jnp.tanh(0.5 * x) + 0.5


def _group_rows(x):
    rows, w = x.shape
    return x.reshape(rows // SUBLANES, SUBLANES, w)


def _scan_rows(a, u, h0):
    a3, u3 = _group_rows(a), _group_rows(u)
    r3 = lax.broadcasted_iota(I32, a3.shape, 1)
    s = 1
    while s < SUBLANES:
        keep = r3 >= s
        a_sh = jnp.where(keep, pltpu.roll(a3, s, 1), 1.0)
        u_sh = jnp.where(keep, pltpu.roll(u3, s, 1), 0.0)
        u3 = a3 * u_sh + u3
        a3 = a3 * a_sh
        s *= 2
    out, h = [], h0
    for i in range(a3.shape[0]):
        hg = u3[i] + a3[i] * h
        h = hg[SUBLANES - 1:SUBLANES, :]
        out.append(hg)
    return jnp.concatenate(out, axis=0), h


def _chunk_cumsum(x, chunk):
    x3 = _group_rows(x)
    r3 = lax.broadcasted_iota(I32, x3.shape, 1)
    s = 1
    while s < SUBLANES:
        x3 = x3 + jnp.where(r3 >= s, pltpu.roll(x3, s, 1), 0.0)
        s *= 2
    out, carry = [], None
    for i in range(x3.shape[0]):
        cur = x3[i] if i % (chunk // SUBLANES) == 0 else x3[i] + carry
        carry = cur[SUBLANES - 1:SUBLANES, :]
        out.append(cur)
    return jnp.concatenate(out, axis=0)


def _lru_gates(xc, wg_ref, bg_ref, lam_ref):
    gates = jnp.dot(xc.astype(BF16), wg_ref[...], preferred_element_type=F32) + bg_ref[...]
    r = _sigmoid(gates[:, :LRU_WIDTH])
    ig = _sigmoid(gates[:, LRU_WIDTH:])
    z = -lam_ref[...]
    softplus = jnp.maximum(z, 0.0) + jnp.log1p(jnp.exp(-jnp.abs(z)))
    log_a = -LRU_C * r * softplus
    a = jnp.exp(log_a)
    th = jnp.tanh(log_a)
    mult2 = -2.0 * th / (1.0 - th)
    return a, jnp.sqrt(mult2), ig


def _conv4(xbuf, xl, cw_ref, cb_ref, rows):
    cw = cw_ref[...]
    h = CONV_HDR
    return (cb_ref[...] + cw[0:1] * xbuf[h - 3:h - 3 + rows, :] + cw[1:2] * xbuf[h - 2:h - 2 + rows, :]
            + cw[2:3] * xbuf[h - 1:h - 1 + rows, :] + cw[3:4] * xl)


def _conv4_rolled(xl, prev, cw_ref, cb_ref):
    cw = cw_ref[...]
    row8 = lax.broadcasted_iota(I32, prev.shape, 0)
    acc = cb_ref[...] + cw[CONV_WIDTH - 1:CONV_WIDTH] * xl
    for s in range(1, CONV_WIDTH):
        sh = pltpu.roll(xl, s, 0)
        top = jnp.where(row8 < s, pltpu.roll(prev, s, 0), sh[0:SUBLANES, :])
        sh = jnp.concatenate([top, sh[SUBLANES:, :]], axis=0)
        acc = acc + cw[CONV_WIDTH - 1 - s:CONV_WIDTH - s] * sh
    return acc


def _head_norm_gate(o, gate, hgn):
    return _rms(o, hgn) * _sigmoid(gate)


def _mixer_prompt_kernel(proj_ref, cw_ref, cb_ref, wg_ref, bg_ref, lam_ref, lb_ref, hgn_ref,
                         m_ref, conv_ref, h_ref, s_ref,
                         xprev, hcar, st, obuf, *, tc):
    j = pl.program_id(1)
    w = LRU_WIDTH

    @pl.when(j == 0)
    def _():
        xprev[...] = jnp.zeros_like(xprev)
        hcar[...] = jnp.zeros_like(hcar)
        st[...] = jnp.zeros_like(st)

    xl = proj_ref[:, C_XL:C_XL + w]
    xc = _conv4_rolled(xl, xprev[...], cw_ref, cb_ref)
    xprev[...] = xl[tc - SUBLANES:tc, :]
    a, mult, ig = _lru_gates(xc, wg_ref, bg_ref, lam_ref)
    row = lax.broadcasted_iota(I32, (tc, w), 0)
    mult = jnp.where((row == 0) & (j == 0), 1.0, mult)
    h, h_last = _scan_rows(a, mult * ig * xc, hcar[0:1, :])
    hcar[0:1, :] = h_last
    m_ref[:, 0:w] = (h * _gelu_tanh(proj_ref[:, C_YL:C_YL + w])).astype(BF16)

    lb = lb_ref[...]
    f = lb + (1.0 - lb) * _sigmoid(proj_ref[:, C_F:C_F + w])
    k = 1.0 - f
    b = _chunk_cumsum(jnp.log(f), HG_CHUNK)
    q = proj_ref[:, C_Q:C_Q + w]
    v = proj_ref[:, C_V:C_V + w]
    tri = (lax.broadcasted_iota(I32, (HG_CHUNK, HG_CHUNK), 0)
           >= lax.broadcasted_iota(I32, (HG_CHUNK, HG_CHUNK), 1))
    mid = HG_CHUNK // 2
    for hd in range(HG_HEADS):
        cs = slice(hd * HG_HEAD_DIM, (hd + 1) * HG_HEAD_DIM)
        s_t = st[hd]
        for c in range(tc // HG_CHUNK):
            rs = slice(c * HG_CHUNK, (c + 1) * HG_CHUNK)
            bc, qc, kc = b[rs, cs], q[rs, cs], k[rs, cs]
            vb = v[rs, cs].astype(BF16)
            b_mid = bc[mid - 1:mid, :]
            b_end = bc[HG_CHUNK - 1:HG_CHUNK, :]
            qp = (qc * jnp.exp(bc - b_mid)).astype(BF16)
            kp = (kc * jnp.exp(b_mid - bc)).astype(BF16)
            att = lax.dot_general(qp, kp, _NT, preferred_element_type=F32)
            att = jnp.where(tri, att, 0.0)
            q_in = (qc * jnp.exp(bc)).astype(BF16)
            o = (jnp.dot(att.astype(BF16), vb, preferred_element_type=F32)
                 + lax.dot_general(q_in, s_t.astype(BF16), _NT, preferred_element_type=F32))
            k_out = (kc * jnp.exp(b_end - bc)).astype(BF16)
            s_t = s_t * jnp.exp(b_end) + lax.dot_general(vb, k_out, _TN, preferred_element_type=F32)
            obuf[rs, cs] = o
        st[hd] = s_t
    hgn = hgn_ref[...]
    for hd in range(HG_HEADS):
        cs = slice(hd * HG_HEAD_DIM, (hd + 1) * HG_HEAD_DIM)
        gate = proj_ref[:, C_G + hd * HG_HEAD_DIM:C_G + (hd + 1) * HG_HEAD_DIM]
        m_ref[:, w + hd * HG_HEAD_DIM:w + (hd + 1) * HG_HEAD_DIM] = (
            _head_norm_gate(obuf[:, cs], gate, hgn).astype(BF16))

    @pl.when(j == pl.num_programs(1) - 1)
    def _():
        conv_ref[...] = xprev[SUBLANES - (CONV_WIDTH - 1):SUBLANES, :]
        h_ref[...] = hcar[0:1, :]
        for hd in range(HG_HEADS):
            s_ref[hd] = st[hd].T


def _mixer_prompt(proj, nb, seq, mw, tc):
    cw, cb, wg, bg, lam, lb, hgn = mw
    nj = seq // tc
    const = lambda shape: pl.BlockSpec(shape, lambda b, j: tuple(0 for _ in shape))
    return pl.pallas_call(
        functools.partial(_mixer_prompt_kernel, tc=tc),
        out_shape=(jax.ShapeDtypeStruct((nb * seq, D_MODEL), BF16),
                   jax.ShapeDtypeStruct((nb, CONV_WIDTH - 1, LRU_WIDTH), F32),
                   jax.ShapeDtypeStruct((nb, 1, LRU_WIDTH), F32),
                   jax.ShapeDtypeStruct((nb, HG_HEADS, HG_HEAD_DIM, HG_HEAD_DIM), F32)),
        grid=(nb, nj),
        in_specs=[pl.BlockSpec((tc, IN_COLS), lambda b, j: (b * nj + j, 0)),
                  const((CONV_WIDTH, LRU_WIDTH)), const((1, LRU_WIDTH)),
                  const((LRU_WIDTH, 2 * LRU_WIDTH)), const((1, 2 * LRU_WIDTH)),
                  const((1, LRU_WIDTH)), const((1, HG_WIDTH)), const((1, HG_HEAD_DIM))],
        out_specs=(pl.BlockSpec((tc, D_MODEL), lambda b, j: (b * nj + j, 0)),
                   pl.BlockSpec((None, CONV_WIDTH - 1, LRU_WIDTH), lambda b, j: (b, 0, 0)),
                   pl.BlockSpec((None, 1, LRU_WIDTH), lambda b, j: (b, 0, 0)),
                   pl.BlockSpec((None, HG_HEADS, HG_HEAD_DIM, HG_HEAD_DIM), lambda b, j: (b, 0, 0, 0))),
        scratch_shapes=[pltpu.VMEM((SUBLANES, LRU_WIDTH), F32),
                        pltpu.VMEM((SUBLANES, LRU_WIDTH), F32),
                        pltpu.VMEM((HG_HEADS, HG_HEAD_DIM, HG_HEAD_DIM), F32),
                        pltpu.VMEM((tc, HG_WIDTH), F32)],
        compiler_params=_cparams(("arbitrary", "arbitrary")),
    )(proj, cw, cb, wg, bg, lam, lb, hgn)


def _mixer_sample_kernel(proj_ref, cst_ref, h0_ref, s0_ref, cw_ref, cb_ref, wg_ref, bg_ref, lam_ref, lb_ref,
                         hgn_ref, m_ref, conv_ref, h_ref, s_ref, xbuf, *, steps):
    for bi in range(proj_ref.shape[0]):
        _mixer_sample_one(proj_ref.at[bi], cst_ref.at[bi], h0_ref.at[bi], s0_ref.at[bi], cw_ref, cb_ref, wg_ref,
                          bg_ref, lam_ref, lb_ref, hgn_ref, m_ref.at[bi], conv_ref.at[bi], h_ref.at[bi],
                          s_ref.at[bi], xbuf.at[bi], steps)


def _mixer_sample_one(proj_ref, cst_ref, h0_ref, s0_ref, cw_ref, cb_ref, wg_ref, bg_ref, lam_ref, lb_ref,
                      hgn_ref, m_ref, conv_ref, h_ref, s_ref, xbuf, steps):
    w = LRU_WIDTH
    rows = SUBLANES
    row = lax.broadcasted_iota(I32, (rows, w), 0)
    valid = row < steps

    xl = proj_ref[:, C_XL:C_XL + w]
    xbuf[0:CONV_HDR, :] = cst_ref[...]
    xbuf[CONV_HDR:CONV_HDR + rows, :] = xl
    xc = _conv4(xbuf, xl, cw_ref, cb_ref, rows)
    conv_ref[...] = xbuf[CONV_HDR + steps - 3:CONV_HDR + steps, :]
    a, mult, ig = _lru_gates(xc, wg_ref, bg_ref, lam_ref)
    u = mult * ig * xc
    h = h0_ref[...]
    hs = jnp.zeros((rows, w), F32)
    for t in range(steps):
        h = a[t:t + 1, :] * h + u[t:t + 1, :]
        hs = jnp.where(row == t, h, hs)
    h_ref[...] = h
    m_ref[:, 0:w] = (hs * _gelu_tanh(proj_ref[:, C_YL:C_YL + w])).astype(BF16)

    lb = lb_ref[...]
    f = lb + (1.0 - lb) * _sigmoid(proj_ref[:, C_F:C_F + w])
    k = 1.0 - f
    b = jnp.log(f)
    s = 1
    while s < rows:
        b = b + jnp.where(row >= s, pltpu.roll(b, s, 0), 0.0)
        s *= 2
    q = proj_ref[:, C_Q:C_Q + w]
    v = proj_ref[:, C_V:C_V + w]
    b_end = b[steps - 1:steps, :]
    q_in = jnp.where(valid, q * jnp.exp(b), 0.0)
    k_out = jnp.where(valid, k * jnp.exp(jnp.where(valid, b_end - b, 0.0)), 0.0)
    v_ok = jnp.where(valid, v, 0.0)
    hgn = hgn_ref[...]
    row_h = lax.broadcasted_iota(I32, (rows, HG_HEAD_DIM), 0)
    for hd in range(HG_HEADS):
        cs = slice(hd * HG_HEAD_DIM, (hd + 1) * HG_HEAD_DIM)
        s_t = s0_ref[hd].T
        o = lax.dot_general(q_in[:, cs].astype(BF16), s_t.astype(BF16), _NT, preferred_element_type=F32)
        for d in range(steps):
            ok = (row_h < steps) & (row_h >= d)
            k_d = k[:, cs] if d == 0 else pltpu.roll(k[:, cs], d, 0)
            v_d = v[:, cs] if d == 0 else pltpu.roll(v[:, cs], d, 0)
            b_d = b[:, cs] if d == 0 else pltpu.roll(b[:, cs], d, 0)
            decay = jnp.exp(jnp.where(ok, b[:, cs] - b_d, 0.0))
            wgt = jnp.sum(jnp.where(ok, q[:, cs] * k_d * decay, 0.0), axis=-1, keepdims=True)
            o = o + wgt * v_d
        s_new = s_t * jnp.exp(b_end[:, cs]) + lax.dot_general(
            v_ok[:, cs].astype(BF16), k_out[:, cs].astype(BF16), _TN, preferred_element_type=F32)
        s_ref[hd] = s_new.T
        gate = proj_ref[:, C_G + hd * HG_HEAD_DIM:C_G + (hd + 1) * HG_HEAD_DIM]
        m_ref[:, w + hd * HG_HEAD_DIM:w + (hd + 1) * HG_HEAD_DIM] = _head_norm_gate(o, gate, hgn).astype(BF16)


def _mixer_sample(proj_pad, cst_hdr, h0, s0, mw, steps):
    cw, cb, wg, bg, lam, lb, hgn = mw
    nb = proj_pad.shape[0]
    per_step = _pick_tile(nb, (4, 2, 1))
    const = lambda shape: pl.BlockSpec(shape, lambda b: tuple(0 for _ in shape))
    per_b = lambda shape: pl.BlockSpec((per_step,) + shape, lambda b: (b,) + tuple(0 for _ in shape))
    return pl.pallas_call(
        functools.partial(_mixer_sample_kernel, steps=steps),
        out_shape=(jax.ShapeDtypeStruct((nb, SUBLANES, D_MODEL), BF16),
                   jax.ShapeDtypeStruct((nb, CONV_WIDTH - 1, LRU_WIDTH), F32),
                   jax.ShapeDtypeStruct((nb, 1, LRU_WIDTH), F32),
                   jax.ShapeDtypeStruct((nb, HG_HEADS, HG_HEAD_DIM, HG_HEAD_DIM), F32)),
        grid=(nb // per_step,),
        in_specs=[per_b((SUBLANES, IN_COLS)), per_b((CONV_HDR, LRU_WIDTH)), per_b((1, LRU_WIDTH)),
                  per_b((HG_HEADS, HG_HEAD_DIM, HG_HEAD_DIM)),
                  const((CONV_WIDTH, LRU_WIDTH)), const((1, LRU_WIDTH)),
                  const((LRU_WIDTH, 2 * LRU_WIDTH)), const((1, 2 * LRU_WIDTH)),
                  const((1, LRU_WIDTH)), const((1, HG_WIDTH)), const((1, HG_HEAD_DIM))],
        out_specs=(per_b((SUBLANES, D_MODEL)), per_b((CONV_WIDTH - 1, LRU_WIDTH)), per_b((1, LRU_WIDTH)),
                   per_b((HG_HEADS, HG_HEAD_DIM, HG_HEAD_DIM))),
        scratch_shapes=[pltpu.VMEM((per_step, CONV_HDR + SUBLANES, LRU_WIDTH), F32)],
        compiler_params=_cparams(("arbitrary",)),
    )(proj_pad, cst_hdr, h0, s0, cw, cb, wg, bg, lam, lb, hgn)


def _to_row_tiles(ref, val, rows):
    for c in range(LANE_CHUNKS):
        ref[pl.ds(c, rows, stride=LANE_CHUNKS), :] = val[:, c * LANES:(c + 1) * LANES]


def _from_row_tiles(ref, base, rows):
    return jnp.concatenate(
        [ref[pl.ds(base + c, rows, stride=LANE_CHUNKS), :] for c in range(LANE_CHUNKS)], axis=-1)


def _post_kernel(m_ref, x_ref, gtm_ref, scf_ref, shf_ref, npost_ref, npre_ref, wout_ref, rwt_ref, rb_ref,
                 cnt_in_ref, x1_ref, hf_ref, idx_ref, gate_ref, rank_ref, cnt_out_ref, cnt):
    i = pl.program_id(0)
    tm = x_ref.shape[0]

    @pl.when(i == 0)
    def _():
        cnt[...] = cnt_in_ref[...]

    mix = jnp.dot(m_ref[...], wout_ref[...], preferred_element_type=F32)
    x1 = x_ref[...] + gtm_ref[...] * _rms(mix, npost_ref[...])
    x1_ref[...] = x1
    hf = _rms(x1, npre_ref[...]) * (1.0 + scf_ref[...]) + shf_ref[...]
    _to_row_tiles(hf_ref, hf, tm)
    logits = lax.dot_general(rwt_ref[...], hf, _NT, precision=lax.Precision.HIGHEST,
                             preferred_element_type=F32) + rb_ref[...]
    e_iota = lax.broadcasted_iota(I32, logits.shape, 0)
    vals, idxs = [], []
    for _ in range(TOP_K):
        mx = jnp.max(logits, axis=0, keepdims=True)
        ix = jnp.min(jnp.where(logits == mx, e_iota, N_EXPERTS), axis=0, keepdims=True)
        vals.append(mx)
        idxs.append(ix)
        logits = jnp.where(e_iota == ix, -jnp.inf, logits)
    exps = [jnp.exp(vv - vals[0]) for vv in vals]
    den = exps[0] + exps[1] + exps[2] + exps[3]
    earlier = (lax.broadcasted_iota(I32, (tm, tm), 0) < lax.broadcasted_iota(I32, (tm, tm), 1))
    earlier = jnp.where(earlier, 1.0, 0.0).astype(BF16)
    base = cnt[...]
    for kk in range(TOP_K):
        idx_ref[kk:kk + 1, :] = idxs[kk]
        gate_ref[kk:kk + 1, :] = exps[kk] / den
        hit = e_iota == idxs[kk]
        onehot = jnp.where(hit, 1.0, 0.0)
        before = jnp.dot(onehot.astype(BF16), earlier, preferred_element_type=F32)
        rank = jnp.sum(jnp.where(hit, before + base, 0.0), axis=0, keepdims=True)
        rank_ref[kk:kk + 1, :] = rank.astype(I32)
        base = base + jnp.sum(onehot, axis=1, keepdims=True)
    cnt[...] = base

    @pl.when(i == pl.num_programs(0) - 1)
    def _():
        cnt_out_ref[...] = base


def _post(m, x, gtm, scf, shf, npost, npre, wout_bf, rwt, rb, cnt_in, tm, tiles_per_seq):
    t = x.shape[0]
    row = lambda: pl.BlockSpec((1, D_MODEL), lambda i: (0, 0))
    tok = lambda: pl.BlockSpec((tm, D_MODEL), lambda i: (i, 0))
    per_k = lambda: pl.BlockSpec((TOP_K, tm), lambda i: (0, i))
    cnt_spec = lambda: pl.BlockSpec((N_EXPERTS, 1), lambda i: (0, 0))
    return pl.pallas_call(
        _post_kernel,
        out_shape=(jax.ShapeDtypeStruct((t, D_MODEL), F32),
                   jax.ShapeDtypeStruct((t * LANE_CHUNKS, LANES), F32),
                   jax.ShapeDtypeStruct((TOP_K, t), I32), jax.ShapeDtypeStruct((TOP_K, t), F32),
                   jax.ShapeDtypeStruct((TOP_K, t), I32), jax.ShapeDtypeStruct((N_EXPERTS, 1), F32)),
        grid=(t // tm,),
        in_specs=[tok(), tok(),
                  _mod_spec(gtm, tm, tiles_per_seq), _mod_spec(scf, tm, tiles_per_seq),
                  _mod_spec(shf, tm, tiles_per_seq), row(), row(),
                  pl.BlockSpec((D_MODEL, D_MODEL), lambda i: (0, 0)),
                  pl.BlockSpec((N_EXPERTS, D_MODEL), lambda i: (0, 0)),
                  cnt_spec(), cnt_spec()],
        out_specs=(tok(), pl.BlockSpec((tm * LANE_CHUNKS, LANES), lambda i: (i, 0)),
                   per_k(), per_k(), per_k(), cnt_spec()),
        scratch_shapes=[pltpu.VMEM((N_EXPERTS, 1), F32)],
        compiler_params=_cparams(("arbitrary",)),
    )(m, x, gtm, scf, shf, npost, npre, wout_bf, rwt, rb, cnt_in)


def _dispatch_kernel(fill_lo_ref, fill_hi_ref, dest_ref, hfp_ref, hfs_ref, xs_hbm, zeros, row_sem, pad_sem,
                     *, tm, n_tiles_p, n_slots):
    i = pl.program_id(0)
    n_rows = TOP_K * tm

    def wait_rows():
        pltpu.make_async_copy(xs_hbm.at[pl.ds(0, n_rows)], xs_hbm.at[pl.ds(0, n_rows)], row_sem).wait()

    @pl.when(i == 0)
    def _():
        zeros[...] = jnp.zeros_like(zeros)

        def per_expert(e, total):
            lo, hi = fill_lo_ref[e], fill_hi_ref[e]

            def per_slot(s, carry):
                pltpu.make_async_copy(zeros.at[0], xs_hbm.at[s], pad_sem).start()
                return carry
            lax.fori_loop(lo, hi, per_slot, 0)
            return total + (hi - lo)
        n_fill = lax.fori_loop(0, N_EXPERTS, per_expert, 0)
        tail_lo = fill_hi_ref[N_EXPERTS - 1] // ZERO_ROWS
        n_tail = n_slots // ZERO_ROWS - tail_lo

        def per_tail(c, carry):
            pltpu.make_async_copy(zeros, xs_hbm.at[pl.ds((tail_lo + c) * ZERO_ROWS, ZERO_ROWS)], pad_sem).start()
            return carry
        lax.fori_loop(0, n_tail, per_tail, 0)

        def drain_slot(s, carry):
            pltpu.make_async_copy(zeros.at[0], xs_hbm.at[0], pad_sem).wait()
            return carry
        lax.fori_loop(0, n_fill, drain_slot, 0)

        def drain_tail(c, carry):
            pltpu.make_async_copy(zeros, xs_hbm.at[pl.ds(0, ZERO_ROWS)], pad_sem).wait()
            return carry
        lax.fori_loop(0, n_tail, drain_tail, 0)

    def issue(hf_ref):
        for n in range(n_rows):
            pltpu.make_async_copy(hf_ref.at[pl.ds((n % tm) * LANE_CHUNKS, LANE_CHUNKS), :],
                                  xs_hbm.at[dest_ref[0, n]], row_sem).start(priority=n % 2)

    @pl.when(i < n_tiles_p)
    def _():
        issue(hfp_ref)

    @pl.when(i >= n_tiles_p)
    def _():
        issue(hfs_ref)

    wait_rows()


def _dispatch(fill_lo, fill_hi, dest_tiles, hfp_rows, hfs_rows, n_slots, tm):
    n_tiles = dest_tiles.shape[0]
    n_tiles_p = hfp_rows.shape[0] // (tm * LANE_CHUNKS)
    return pl.pallas_call(
        functools.partial(_dispatch_kernel, tm=tm, n_tiles_p=n_tiles_p, n_slots=n_slots),
        out_shape=jax.ShapeDtypeStruct((n_slots, LANE_CHUNKS, LANES), F32),
        grid_spec=pltpu.PrefetchScalarGridSpec(
            num_scalar_prefetch=2, grid=(n_tiles,),
            in_specs=[pl.BlockSpec((None, 1, TOP_K * tm), lambda i, lo, hi: (i, 0, 0), memory_space=pltpu.SMEM),
                      pl.BlockSpec((tm * LANE_CHUNKS, LANES), lambda i, lo, hi: (jnp.minimum(i, n_tiles_p - 1), 0)),
                      pl.BlockSpec((tm * LANE_CHUNKS, LANES), lambda i, lo, hi: (jnp.maximum(i - n_tiles_p, 0), 0))],
            out_specs=pl.BlockSpec(memory_space=pl.ANY),
            scratch_shapes=[pltpu.VMEM((ZERO_ROWS, LANE_CHUNKS, LANES), F32),
                            pltpu.SemaphoreType.DMA, pltpu.SemaphoreType.DMA]),
        compiler_params=_cparams(("arbitrary",)),
    )(fill_lo, fill_hi, dest_tiles, hfp_rows, hfs_rows)


def _expert_kernel(be_ref, nxt_ref, nu_ref, xs_ref, wgu_hbm, bgu_ref, wd_hbm, bd_ref, ys_ref,
                   wgu_f32, wd_f32, wgu_bf, wd_bf, w_sem):
    i = pl.program_id(0)

    def weight_copies(e):
        return (pltpu.make_async_copy(wgu_hbm.at[e], wgu_f32, w_sem.at[0]),
                pltpu.make_async_copy(wd_hbm.at[e], wd_f32, w_sem.at[1]))

    @pl.when(i == 0)
    def _():
        for cp in weight_copies(be_ref[0]):
            cp.start()

    @pl.when(i < nu_ref[0])
    def _():
        prev = be_ref[jnp.maximum(i - 1, 0)]

        @pl.when((i == 0) | (be_ref[i] != prev))
        def _():
            for cp in weight_copies(be_ref[i]):
                cp.wait()
            wgu_bf[...] = wgu_f32[...].astype(BF16)
            wd_bf[...] = wd_f32[...].astype(BF16)
            nxt = nxt_ref[be_ref[i]]

            @pl.when(nxt >= 0)
            def _():
                for cp in weight_copies(nxt):
                    cp.start()

        x = _from_row_tiles(xs_ref, 0, MOE_BM).astype(BF16)
        hu = jnp.dot(x, wgu_bf[...], preferred_element_type=F32) + bgu_ref[...]
        glu = jnp.minimum(hu[:, :D_EXPERT], SWIGLU_LIMIT)
        lin = jnp.clip(hu[:, D_EXPERT:], -SWIGLU_LIMIT, SWIGLU_LIMIT)
        act = glu * jax.nn.sigmoid(SWIGLU_ALPHA * glu) * (lin + 1.0)
        y = jnp.dot(act.astype(BF16), wd_bf[...], preferred_element_type=F32) + bd_ref[...]
        _to_row_tiles(ys_ref, y, MOE_BM)

    @pl.when(i >= nu_ref[0])
    def _():
        ys_ref[...] = jnp.zeros_like(ys_ref)


def _experts(block_e, next_e, n_used, xs2d, w_gu, b_gu, w_down, b_down):
    n_blocks = block_e.shape[0]
    rows = MOE_BM * LANE_CHUNKS
    live = lambda i, be, nx, nu: (jnp.minimum(i, nu[0] - 1), 0)
    return pl.pallas_call(
        _expert_kernel,
        out_shape=jax.ShapeDtypeStruct(xs2d.shape, F32),
        grid_spec=pltpu.PrefetchScalarGridSpec(
            num_scalar_prefetch=3,
            grid=(n_blocks,),
            in_specs=[pl.BlockSpec((rows, LANES), live),
                      pl.BlockSpec(memory_space=pl.ANY),
                      pl.BlockSpec((None, 1, 2 * D_EXPERT), lambda i, be, nx, nu: (be[i], 0, 0)),
                      pl.BlockSpec(memory_space=pl.ANY),
                      pl.BlockSpec((None, 1, D_MODEL), lambda i, be, nx, nu: (be[i], 0, 0))],
            out_specs=pl.BlockSpec((rows, LANES), lambda i, be, nx, nu: (i, 0)),
            scratch_shapes=[pltpu.VMEM((D_MODEL, 2 * D_EXPERT), F32),
                            pltpu.VMEM((D_EXPERT, D_MODEL), F32),
                            pltpu.VMEM((D_MODEL, 2 * D_EXPERT), BF16),
                            pltpu.VMEM((D_EXPERT, D_MODEL), BF16),
                            pltpu.SemaphoreType.DMA((2,))]),
        compiler_params=_cparams(("arbitrary",)),
    )(block_e, next_e, n_used, xs2d, w_gu, b_gu, w_down, b_down)


def _combine_kernel(dcur_ref, dnext_ref, ys_hbm, g_ref, x1_ref, gt_ref, np_ref, o_ref, ybuf, row_sem, *, tm):
    i = pl.program_id(0)
    n_rows = TOP_K * tm
    slot = i % 2

    def slot_rows(s):
        return ybuf.at[pl.ds(pl.multiple_of(s * n_rows * LANE_CHUNKS, n_rows * LANE_CHUNKS), n_rows * LANE_CHUNKS), :]

    def issue(d_ref, s):
        dst = slot_rows(s)
        for n in range(n_rows):
            pltpu.make_async_copy(ys_hbm.at[d_ref[0, n]], dst.at[pl.ds(n * LANE_CHUNKS, LANE_CHUNKS), :],
                                  row_sem.at[s]).start(priority=n % 2)

    @pl.when(i == 0)
    def _():
        issue(dcur_ref, 0)

    @pl.when(i + 1 < pl.num_programs(0))
    def _():
        issue(dnext_ref, 1 - slot)

    pltpu.make_async_copy(slot_rows(slot), slot_rows(slot), row_sem.at[slot]).wait()
    g = g_ref[...]
    y = None
    for kk in range(TOP_K):
        rows = _from_row_tiles(ybuf, (slot * n_rows + kk * tm) * LANE_CHUNKS, tm)
        y = g[:, kk:kk + 1] * rows if y is None else y + g[:, kk:kk + 1] * rows
    o_ref[...] = x1_ref[...] + gt_ref[...] * _rms(y, np_ref[...])


def _combine(dest_tiles, ys_tiles, gates_t, x1, gt, npost, tm, tiles_per_seq):
    t = x1.shape[0]
    n_tiles = t // tm
    smem_tile = lambda f: pl.BlockSpec((None, 1, TOP_K * tm), f, memory_space=pltpu.SMEM)
    return pl.pallas_call(
        functools.partial(_combine_kernel, tm=tm),
        out_shape=jax.ShapeDtypeStruct((t, D_MODEL), F32),
        grid=(n_tiles,),
        in_specs=[smem_tile(lambda i: (i, 0, 0)),
                  smem_tile(lambda i: (jnp.minimum(i + 1, n_tiles - 1), 0, 0)),
                  pl.BlockSpec(memory_space=pl.ANY),
                  pl.BlockSpec((tm, TOP_K), lambda i: (i, 0)),
                  pl.BlockSpec((tm, D_MODEL), lambda i: (i, 0)),
                  _mod_spec(gt, tm, tiles_per_seq),
                  pl.BlockSpec((1, D_MODEL), lambda i: (0, 0))],
        out_specs=pl.BlockSpec((tm, D_MODEL), lambda i: (i, 0)),
        scratch_shapes=[pltpu.VMEM((2 * TOP_K * tm * LANE_CHUNKS, LANES), F32),
                        pltpu.SemaphoreType.DMA((2,))],
        compiler_params=_cparams(("arbitrary",)),
    )(dest_tiles, dest_tiles, ys_tiles, gates_t, x1, gt, npost)


def _expert_layout(counts, n_blocks):
    padded = (counts + MOE_BM - 1) // MOE_BM * MOE_BM
    pad_ends = jnp.cumsum(padded)
    pad_starts = pad_ends - padded
    block_start = jnp.arange(n_blocks, dtype=I32) * MOE_BM
    block_e = jnp.minimum(jnp.sum((pad_ends[None, :] <= block_start[:, None]).astype(I32), axis=1), N_EXPERTS - 1)
    n_used = (pad_ends[-1:] // MOE_BM).astype(I32)
    e_ids = jnp.arange(N_EXPERTS, dtype=I32)
    later = (counts > 0)[None, :] & (e_ids[None, :] > e_ids[:, None])
    next_e = jnp.min(jnp.where(later, e_ids[None, :], N_EXPERTS), axis=1)
    next_e = jnp.where(next_e < N_EXPERTS, next_e, -1).astype(I32)
    return pad_starts, pad_starts + counts, pad_ends, block_e, next_e, n_used


def _slots(idx, rank, pad_starts):
    onehot = idx[:, :, None] == jnp.arange(N_EXPERTS, dtype=I32)[None, None, :]
    return rank + jnp.sum(jnp.where(onehot, pad_starts[None, None, :], 0), axis=-1)


def _tile_major(a, tm):
    t = a.shape[1]
    return a.reshape(TOP_K, t // tm, tm).transpose(1, 0, 2).reshape(t // tm, 1, TOP_K * tm)


def _block_diag(wb):
    n, d, _ = wb.shape
    eye = jnp.eye(n, dtype=wb.dtype)
    return (wb[:, :, None, :] * eye[:, None, :, None]).reshape(n * d, n * d)


def _pick_tile(n, prefs):
    for p in prefs:
        if n % p == 0:
            return p
    raise ValueError(f"no tile for {n}")


def kernel(x_prompt, x_sample, state_rglru_conv, state_rglru_h, state_hgrn_S, c_prompt, c_sample, ada_w, ada_b, norm_pre_mix, norm_post_mix, norm_pre_ffn, norm_post_ffn, w_in, conv_w, conv_b, lru_wa, lru_ba, lru_wi, lru_bi, lru_lambda, hg_lb, hg_norm, w_out, router_w, router_b, w_gu, b_gu, w_down, b_down):
    assert ada_w.shape[0] == 1, "single-layer trunk"
    bp, lp, _ = x_prompt.shape
    bs, ls, _ = x_sample.shape
    tp, ts = bp * lp, bs * ls
    assert CONV_WIDTH - 1 <= ls <= SUBLANES

    w_in_bf = w_in[0].astype(BF16)
    w_out_bf = w_out[0].astype(BF16)
    wg = jnp.concatenate([_block_diag(lru_wa[0]), _block_diag(lru_wi[0])], axis=1).astype(BF16)
    bg = jnp.concatenate([lru_ba[0], lru_bi[0]])[None, :]
    lbs = jnp.cumsum(jax.nn.softmax(hg_lb.astype(F32), axis=0), axis=0)[0][None, :]
    mw = (conv_w[0], conv_b[0][None, :], wg, bg, lru_lambda[0][None, :], lbs, hg_norm[0][None, :])
    row = lambda p: p[0][None, :]

    n_c = bp + bs
    n_c_pad = -(-n_c // SUBLANES) * SUBLANES
    c_all = jnp.concatenate([c_prompt, c_sample, jnp.zeros((n_c_pad - n_c, D_MODEL), F32)], axis=0)
    mod = _ada(c_all, ada_w[0], ada_b[0][None, :])
    mod_p = [m[:, None, :] for m in jnp.split(mod[:bp], 6, axis=-1)]
    mod_s = [jnp.repeat(m, ls, axis=0) for m in jnp.split(mod[bp:n_c], 6, axis=-1)]

    xp = x_prompt.reshape(tp, D_MODEL)
    xs = x_sample.reshape(ts, D_MODEL)
    tm_p = _pick_tile(lp, (256, 128, 64))
    tm_s = _pick_tile(ts, (256, 128, 64, 32, 16, 8))

    proj_p = _inproj(xp, mod_p[1], mod_p[0], row(norm_pre_mix), w_in_bf, tm_p, lp // tm_p)
    proj_s = _inproj(xs, mod_s[1], mod_s[0], row(norm_pre_mix), w_in_bf, tm_s, 1)
    tc = _pick_tile(lp, (256, 128, 64))
    m_p, conv_p, h_p, s_p = _mixer_prompt(proj_p, bp, lp, mw, tc)

    proj_s_pad = jnp.pad(proj_s.reshape(bs, ls, IN_COLS), ((0, 0), (0, SUBLANES - ls), (0, 0)))
    cst_hdr = jnp.pad(state_rglru_conv[0], ((0, 0), (CONV_HDR - (CONV_WIDTH - 1), 0), (0, 0)))
    m_s_pad, conv_s, h_s, s_s = _mixer_sample(proj_s_pad, cst_hdr, state_rglru_h[0][:, None, :],
                                              state_hgrn_S[0], mw, ls)
    m_s = m_s_pad[:, :ls, :].reshape(ts, D_MODEL)

    rwt = router_w[0].T
    rb = router_b[0][:, None]
    post = lambda m, x, md, cnt, tm, tps: _post(m, x, md[2], md[4], md[3], row(norm_post_mix), row(norm_pre_ffn),
                                                w_out_bf, rwt, rb, cnt, tm, tps)
    tmq_p = _pick_tile(lp, (512, 256, 128, 64))
    x1_p, hf_p, idx_p, gate_p, rank_p, cnt_p = post(m_p, xp, mod_p, jnp.zeros((N_EXPERTS, 1), F32),
                                                    tmq_p, lp // tmq_p)
    x1_s, hf_s, idx_s, gate_s, rank_s, cnt_all = post(m_s, xs, mod_s, cnt_p, tm_s, 1)

    n_blocks = -(-(TOP_K * (tp + ts)) // MOE_BM) + N_EXPERTS
    n_slots = n_blocks * MOE_BM
    pad_starts, fill_lo, fill_hi, block_e, next_e, n_used = _expert_layout(cnt_all[:, 0].astype(I32), n_blocks)
    dest_p = _slots(idx_p, rank_p, pad_starts)
    dest_s = _slots(idx_s, rank_s, pad_starts)
    tmd = _pick_tile(math.gcd(lp, ts), (128, 64, 32, 16, 8))
    dtile_p = _tile_major(dest_p, tmd)
    dtile_s = _tile_major(dest_s, tmd)
    tiles = lambda a: a.reshape(-1, LANE_CHUNKS, LANES)
    xs_rows = _dispatch(fill_lo, fill_hi, jnp.concatenate([dtile_p, dtile_s], axis=0), hf_p, hf_s, n_slots, tmd)
    ys = _experts(block_e, next_e, n_used, xs_rows.reshape(n_slots * LANE_CHUNKS, LANES),
                  w_gu[0], b_gu[0][:, None, :], w_down[0], b_down[0][:, None, :])
    ys_tiles = tiles(ys)

    y_p = _combine(dtile_p, ys_tiles, gate_p.T, x1_p, mod_p[5], row(norm_post_ffn), tmd, lp // tmd)
    y_s = _combine(dtile_s, ys_tiles, gate_s.T, x1_s, mod_s[5], row(norm_post_ffn), tmd, 1)

    return (y_p.reshape(bp, lp, D_MODEL), y_s.reshape(bs, ls, D_MODEL),
            conv_p[None], h_p.reshape(1, bp, LRU_WIDTH), s_p[None],
            conv_s[None], h_s.reshape(1, bs, LRU_WIDTH), s_s[None])
```

```python
import functools
import math

import jax
import jax.numpy as jnp
from jax import lax
from jax.experimental import pallas as pl
from jax.experimental.pallas import tpu as pltpu

F32 = jnp.float32
BF16 = jnp.bfloat16
I32 = jnp.int32

D_MODEL = 1024
LRU_WIDTH = 512
LRU_BLOCKS = 8
LRU_BLOCK = LRU_WIDTH // LRU_BLOCKS
CONV_WIDTH = 4
LRU_C = 8.0
HG_WIDTH = 512
HG_HEAD_DIM = 128
HG_HEADS = 4
IN_COLS = 2 * LRU_WIDTH + 4 * HG_WIDTH
N_EXPERTS = 32
TOP_K = 4
D_EXPERT = 1024
SWIGLU_LIMIT = 7.0
SWIGLU_ALPHA = 1.702
RMS_EPS = 1e-6

C_XL, C_YL, C_Q, C_F, C_V, C_G = 0, 512, 1024, 1536, 2048, 2560

SUBLANES = 8
LANES = 128
LANE_CHUNKS = D_MODEL // LANES
CONV_HDR = SUBLANES
HG_CHUNK = 64
MOE_BM = 256
VMEM_LIMIT = 56 * 1024 * 1024

_NT = (((1,), (1,)), ((), ()))
_TN = (((0,), (0,)), ((), ()))


def _cparams(sem):
    return pltpu.CompilerParams(dimension_semantics=sem, vmem_limit_bytes=VMEM_LIMIT)


def _rms(x, g):
    return x * lax.rsqrt(jnp.mean(x * x, axis=-1, keepdims=True) + RMS_EPS) * g


def _gelu_tanh(x):
    c = math.sqrt(2.0 / math.pi)
    return 0.5 * x * (1.0 + jnp.tanh(c * (x + 0.044715 * (x * x * x))))


def _ada_kernel(c_ref, w_ref, b_ref, o_ref):
    c = c_ref[...]
    s = c * jax.nn.sigmoid(c)
    o_ref[...] = jnp.dot(s.astype(BF16), w_ref[...].astype(BF16), preferred_element_type=F32) + b_ref[...]


def _ada(c_all, ada_w, ada_b):
    n = c_all.shape[0]
    tn = 1024
    return pl.pallas_call(
        _ada_kernel,
        out_shape=jax.ShapeDtypeStruct((n, 6 * D_MODEL), F32),
        grid=(6 * D_MODEL // tn,),
        in_specs=[pl.BlockSpec((n, D_MODEL), lambda j: (0, 0)),
                  pl.BlockSpec((D_MODEL, tn), lambda j: (0, j)),
                  pl.BlockSpec((1, tn), lambda j: (0, j))],
        out_specs=pl.BlockSpec((n, tn), lambda j: (0, j)),
        compiler_params=_cparams(("arbitrary",)),
    )(c_all, ada_w, ada_b)


def _inproj_kernel(x_ref, sc_ref, sh_ref, g_ref, w_ref, o_ref):
    h = _rms(x_ref[...], g_ref[...]) * (1.0 + sc_ref[...]) + sh_ref[...]
    o_ref[...] = jnp.dot(h.astype(BF16), w_ref[...], preferred_element_type=F32)


def _mod_spec(mod, tm, tiles_per_seq):
    if mod.ndim == 3:
        return pl.BlockSpec((None, 1, D_MODEL), lambda i: (i // tiles_per_seq, 0, 0))
    return pl.BlockSpec((tm, D_MODEL), lambda i: (i, 0))


def _inproj(x, sc, sh, g, w_bf, tm, tiles_per_seq):
    t = x.shape[0]
    return pl.pallas_call(
        _inproj_kernel,
        out_shape=jax.ShapeDtypeStruct((t, IN_COLS), F32),
        grid=(t // tm,),
        in_specs=[pl.BlockSpec((tm, D_MODEL), lambda i: (i, 0)),
                  _mod_spec(sc, tm, tiles_per_seq), _mod_spec(sh, tm, tiles_per_seq),
                  pl.BlockSpec((1, D_MODEL), lambda i: (0, 0)),
                  pl.BlockSpec((D_MODEL, IN_COLS), lambda i: (0, 0))],
        out_specs=pl.BlockSpec((tm, IN_COLS), lambda i: (i, 0)),
        compiler_params=_cparams(("arbitrary",)),
    )(x, sc, sh, g, w_bf)


def _sigmoid(x):
    return 0.5 * jnp.tanh(0.5 * x) + 0.5


def _group_rows(x):
    rows, w = x.shape
    return x.reshape(rows // SUBLANES, SUBLANES, w)


def _scan_rows(a, u, h0):
    a3, u3 = _group_rows(a), _group_rows(u)
    r3 = lax.broadcasted_iota(I32, a3.shape, 1)
    s = 1
    while s < SUBLANES:
        keep = r3 >= s
        a_sh = jnp.where(keep, pltpu.roll(a3, s, 1), 1.0)
        u_sh = jnp.where(keep, pltpu.roll(u3, s, 1), 0.0)
        u3 = a3 * u_sh + u3
        a3 = a3 * a_sh
        s *= 2
    out, h = [], h0
    for i in range(a3.shape[0]):
        hg = u3[i] + a3[i] * h
        h = hg[SUBLANES - 1:SUBLANES, :]
        out.append(hg)
    return jnp.concatenate(out, axis=0), h


def _chunk_cumsum(x, chunk):
    x3 = _group_rows(x)
    r3 = lax.broadcasted_iota(I32, x3.shape, 1)
    s = 1
    while s < SUBLANES:
        x3 = x3 + jnp.where(r3 >= s, pltpu.roll(x3, s, 1), 0.0)
        s *= 2
    out, carry = [], None
    for i in range(x3.shape[0]):
        cur = x3[i] if i % (chunk // SUBLANES) == 0 else x3[i] + carry
        carry = cur[SUBLANES - 1:SUBLANES, :]
        out.append(cur)
    return jnp.concatenate(out, axis=0)


def _lru_gates(xc, wg_ref, bg_ref, lam_ref):
    gates = jnp.dot(xc.astype(BF16), wg_ref[...], preferred_element_type=F32) + bg_ref[...]
    r = _sigmoid(gates[:, :LRU_WIDTH])
    ig = _sigmoid(gates[:, LRU_WIDTH:])
    z = -lam_ref[...]
    softplus = jnp.maximum(z, 0.0) + jnp.log1p(jnp.exp(-jnp.abs(z)))
    log_a = -LRU_C * r * softplus
    a = jnp.exp(log_a)
    th = jnp.tanh(log_a)
    mult2 = -2.0 * th / (1.0 - th)
    return a, jnp.sqrt(mult2), ig


def _conv4(xbuf, xl, cw_ref, cb_ref, rows):
    cw = cw_ref[...]
    h = CONV_HDR
    return (cb_ref[...] + cw[0:1] * xbuf[h - 3:h - 3 + rows, :] + cw[1:2] * xbuf[h - 2:h - 2 + rows, :]
            + cw[2:3] * xbuf[h - 1:h - 1 + rows, :] + cw[3:4] * xl)


def _conv4_rolled(xl, prev, cw_ref, cb_ref):
    cw = cw_ref[...]
    row8 = lax.broadcasted_iota(I32, prev.shape, 0)
    acc = cb_ref[...] + cw[CONV_WIDTH - 1:CONV_WIDTH] * xl
    for s in range(1, CONV_WIDTH):
        sh = pltpu.roll(xl, s, 0)
        top = jnp.where(row8 < s, pltpu.roll(prev, s, 0), sh[0:SUBLANES, :])
        sh = jnp.concatenate([top, sh[SUBLANES:, :]], axis=0)
        acc = acc + cw[CONV_WIDTH - 1 - s:CONV_WIDTH - s] * sh
    return acc


def _head_norm_gate(o, gate, hgn):
    return _rms(o, hgn) * _sigmoid(gate)


def _mixer_prompt_kernel(proj_ref, cw_ref, cb_ref, wg_ref, bg_ref, lam_ref, lb_ref, hgn_ref,
                         m_ref, conv_ref, h_ref, s_ref,
                         xprev, hcar, st, obuf, *, tc):
    j = pl.program_id(1)
    w = LRU_WIDTH

    @pl.when(j == 0)
    def _():
        xprev[...] = jnp.zeros_like(xprev)
        hcar[...] = jnp.zeros_like(hcar)
        st[...] = jnp.zeros_like(st)

    xl = proj_ref[:, C_XL:C_XL + w]
    xc = _conv4_rolled(xl, xprev[...], cw_ref, cb_ref)
    xprev[...] = xl[tc - SUBLANES:tc, :]
    a, mult, ig = _lru_gates(xc, wg_ref, bg_ref, lam_ref)
    row = lax.broadcasted_iota(I32, (tc, w), 0)
    mult = jnp.where((row == 0) & (j == 0), 1.0, mult)
    h, h_last = _scan_rows(a, mult * ig * xc, hcar[0:1, :])
    hcar[0:1, :] = h_last
    m_ref[:, 0:w] = (h * _gelu_tanh(proj_ref[:, C_YL:C_YL + w])).astype(BF16)

    lb = lb_ref[...]
    f = lb + (1.0 - lb) * _sigmoid(proj_ref[:, C_F:C_F + w])
    k = 1.0 - f
    b = _chunk_cumsum(jnp.log(f), HG_CHUNK)
    q = proj_ref[:, C_Q:C_Q + w]
    v = proj_ref[:, C_V:C_V + w]
    tri = (lax.broadcasted_iota(I32, (HG_CHUNK, HG_CHUNK), 0)
           >= lax.broadcasted_iota(I32, (HG_CHUNK, HG_CHUNK), 1))
    mid = HG_CHUNK // 2
    for hd in range(HG_HEADS):
        cs = slice(hd * HG_HEAD_DIM, (hd + 1) * HG_HEAD_DIM)
        s_t = st[hd]
        for c in range(tc // HG_CHUNK):
            rs = slice(c * HG_CHUNK, (c + 1) * HG_CHUNK)
            bc, qc, kc = b[rs, cs], q[rs, cs], k[rs, cs]
            vb = v[rs, cs].astype(BF16)
            b_mid = bc[mid - 1:mid, :]
            b_end = bc[HG_CHUNK - 1:HG_CHUNK, :]
            qp = (qc * jnp.exp(bc - b_mid)).astype(BF16)
            kp = (kc * jnp.exp(b_mid - bc)).astype(BF16)
            att = lax.dot_general(qp, kp, _NT, preferred_element_type=F32)
            att = jnp.where(tri, att, 0.0)
            q_in = (qc * jnp.exp(bc)).astype(BF16)
            o = (jnp.dot(att.astype(BF16), vb, preferred_element_type=F32)
                 + lax.dot_general(q_in, s_t.astype(BF16), _NT, preferred_element_type=F32))
            k_out = (kc * jnp.exp(b_end - bc)).astype(BF16)
            s_t = s_t * jnp.exp(b_end) + lax.dot_general(vb, k_out, _TN, preferred_element_type=F32)
            obuf[rs, cs] = o
        st[hd] = s_t
    hgn = hgn_ref[...]
    for hd in range(HG_HEADS):
        cs = slice(hd * HG_HEAD_DIM, (hd + 1) * HG_HEAD_DIM)
        gate = proj_ref[:, C_G + hd * HG_HEAD_DIM:C_G + (hd + 1) * HG_HEAD_DIM]
        m_ref[:, w + hd * HG_HEAD_DIM:w + (hd + 1) * HG_HEAD_DIM] = (
            _head_norm_gate(obuf[:, cs], gate, hgn).astype(BF16))

    @pl.when(j == pl.num_programs(1) - 1)
    def _():
        conv_ref[...] = xprev[SUBLANES - (CONV_WIDTH - 1):SUBLANES, :]
        h_ref[...] = hcar[0:1, :]
        for hd in range(HG_HEADS):
            s_ref[hd] = st[hd].T


def _mixer_prompt(proj, nb, seq, mw, tc):
    cw, cb, wg, bg, lam, lb, hgn = mw
    nj = seq // tc
    const = lambda shape: pl.BlockSpec(shape, lambda b, j: tuple(0 for _ in shape))
    return pl.pallas_call(
        functools.partial(_mixer_prompt_kernel, tc=tc),
        out_shape=(jax.ShapeDtypeStruct((nb * seq, D_MODEL), BF16),
                   jax.ShapeDtypeStruct((nb, CONV_WIDTH - 1, LRU_WIDTH), F32),
                   jax.ShapeDtypeStruct((nb, 1, LRU_WIDTH), F32),
                   jax.ShapeDtypeStruct((nb, HG_HEADS, HG_HEAD_DIM, HG_HEAD_DIM), F32)),
        grid=(nb, nj),
        in_specs=[pl.BlockSpec((tc, IN_COLS), lambda b, j: (b * nj + j, 0)),
                  const((CONV_WIDTH, LRU_WIDTH)), const((1, LRU_WIDTH)),
                  const((LRU_WIDTH, 2 * LRU_WIDTH)), const((1, 2 * LRU_WIDTH)),
                  const((1, LRU_WIDTH)), const((1, HG_WIDTH)), const((1, HG_HEAD_DIM))],
        out_specs=(pl.BlockSpec((tc, D_MODEL), lambda b, j: (b * nj + j, 0)),
                   pl.BlockSpec((None, CONV_WIDTH - 1, LRU_WIDTH), lambda b, j: (b, 0, 0)),
                   pl.BlockSpec((None, 1, LRU_WIDTH), lambda b, j: (b, 0, 0)),
                   pl.BlockSpec((None, HG_HEADS, HG_HEAD_DIM, HG_HEAD_DIM), lambda b, j: (b, 0, 0, 0))),
        scratch_shapes=[pltpu.VMEM((SUBLANES, LRU_WIDTH), F32),
                        pltpu.VMEM((SUBLANES, LRU_WIDTH), F32),
                        pltpu.VMEM((HG_HEADS, HG_HEAD_DIM, HG_HEAD_DIM), F32),
                        pltpu.VMEM((tc, HG_WIDTH), F32)],
        compiler_params=_cparams(("arbitrary", "arbitrary")),
    )(proj, cw, cb, wg, bg, lam, lb, hgn)


def _mixer_sample_kernel(proj_ref, cst_ref, h0_ref, s0_ref, cw_ref, cb_ref, wg_ref, bg_ref, lam_ref, lb_ref,
                         hgn_ref, m_ref, conv_ref, h_ref, s_ref, xbuf, *, steps):
    for bi in range(proj_ref.shape[0]):
        _mixer_sample_one(proj_ref.at[bi], cst_ref.at[bi], h0_ref.at[bi], s0_ref.at[bi], cw_ref, cb_ref, wg_ref,
                          bg_ref, lam_ref, lb_ref, hgn_ref, m_ref.at[bi], conv_ref.at[bi], h_ref.at[bi],
                          s_ref.at[bi], xbuf.at[bi], steps)


def _mixer_sample_one(proj_ref, cst_ref, h0_ref, s0_ref, cw_ref, cb_ref, wg_ref, bg_ref, lam_ref, lb_ref,
                      hgn_ref, m_ref, conv_ref, h_ref, s_ref, xbuf, steps):
    w = LRU_WIDTH
    rows = SUBLANES
    row = lax.broadcasted_iota(I32, (rows, w), 0)
    valid = row < steps

    xl = proj_ref[:, C_XL:C_XL + w]
    xbuf[0:CONV_HDR, :] = cst_ref[...]
    xbuf[CONV_HDR:CONV_HDR + rows, :] = xl
    xc = _conv4(xbuf, xl, cw_ref, cb_ref, rows)
    conv_ref[...] = xbuf[CONV_HDR + steps - 3:CONV_HDR + steps, :]
    a, mult, ig = _lru_gates(xc, wg_ref, bg_ref, lam_ref)
    u = mult * ig * xc
    h = h0_ref[...]
    hs = jnp.zeros((rows, w), F32)
    for t in range(steps):
        h = a[t:t + 1, :] * h + u[t:t + 1, :]
        hs = jnp.where(row == t, h, hs)
    h_ref[...] = h
    m_ref[:, 0:w] = (hs * _gelu_tanh(proj_ref[:, C_YL:C_YL + w])).astype(BF16)

    lb = lb_ref[...]
    f = lb + (1.0 - lb) * _sigmoid(proj_ref[:, C_F:C_F + w])
    k = 1.0 - f
    b = jnp.log(f)
    s = 1
    while s < rows:
        b = b + jnp.where(row >= s, pltpu.roll(b, s, 0), 0.0)
        s *= 2
    q = proj_ref[:, C_Q:C_Q + w]
    v = proj_ref[:, C_V:C_V + w]
    b_end = b[steps - 1:steps, :]
    q_in = jnp.where(valid, q * jnp.exp(b), 0.0)
    k_out = jnp.where(valid, k * jnp.exp(jnp.where(valid, b_end - b, 0.0)), 0.0)
    v_ok = jnp.where(valid, v, 0.0)
    hgn = hgn_ref[...]
    row_h = lax.broadcasted_iota(I32, (rows, HG_HEAD_DIM), 0)
    for hd in range(HG_HEADS):
        cs = slice(hd * HG_HEAD_DIM, (hd + 1) * HG_HEAD_DIM)
        s_t = s0_ref[hd].T
        o = lax.dot_general(q_in[:, cs].astype(BF16), s_t.astype(BF16), _NT, preferred_element_type=F32)
        for d in range(steps):
            ok = (row_h < steps) & (row_h >= d)
            k_d = k[:, cs] if d == 0 else pltpu.roll(k[:, cs], d, 0)
            v_d = v[:, cs] if d == 0 else pltpu.roll(v[:, cs], d, 0)
            b_d = b[:, cs] if d == 0 else pltpu.roll(b[:, cs], d, 0)
            decay = jnp.exp(jnp.where(ok, b[:, cs] - b_d, 0.0))
            wgt = jnp.sum(jnp.where(ok, q[:, cs] * k_d * decay, 0.0), axis=-1, keepdims=True)
            o = o + wgt * v_d
        s_new = s_t * jnp.exp(b_end[:, cs]) + lax.dot_general(
            v_ok[:, cs].astype(BF16), k_out[:, cs].astype(BF16), _TN, preferred_element_type=F32)
        s_ref[hd] = s_new.T
        gate = proj_ref[:, C_G + hd * HG_HEAD_DIM:C_G + (hd + 1) * HG_HEAD_DIM]
        m_ref[:, w + hd * HG_HEAD_DIM:w + (hd + 1) * HG_HEAD_DIM] = _head_norm_gate(o, gate, hgn).astype(BF16)


def _mixer_sample(proj_pad, cst_hdr, h0, s0, mw, steps):
    cw, cb, wg, bg, lam, lb, hgn = mw
    nb = proj_pad.shape[0]
    per_step = _pick_tile(nb, (4, 2, 1))
    const = lambda shape: pl.BlockSpec(shape, lambda b: tuple(0 for _ in shape))
    per_b = lambda shape: pl.BlockSpec((per_step,) + shape, lambda b: (b,) + tuple(0 for _ in shape))
    return pl.pallas_call(
        functools.partial(_mixer_sample_kernel, steps=steps),
        out_shape=(jax.ShapeDtypeStruct((nb, SUBLANES, D_MODEL), BF16),
                   jax.ShapeDtypeStruct((nb, CONV_WIDTH - 1, LRU_WIDTH), F32),
                   jax.ShapeDtypeStruct((nb, 1, LRU_WIDTH), F32),
                   jax.ShapeDtypeStruct((nb, HG_HEADS, HG_HEAD_DIM, HG_HEAD_DIM), F32)),
        grid=(nb // per_step,),
        in_specs=[per_b((SUBLANES, IN_COLS)), per_b((CONV_HDR, LRU_WIDTH)), per_b((1, LRU_WIDTH)),
                  per_b((HG_HEADS, HG_HEAD_DIM, HG_HEAD_DIM)),
                  const((CONV_WIDTH, LRU_WIDTH)), const((1, LRU_WIDTH)),
                  const((LRU_WIDTH, 2 * LRU_WIDTH)), const((1, 2 * LRU_WIDTH)),
                  const((1, LRU_WIDTH)), const((1, HG_WIDTH)), const((1, HG_HEAD_DIM))],
        out_specs=(per_b((SUBLANES, D_MODEL)), per_b((CONV_WIDTH - 1, LRU_WIDTH)), per_b((1, LRU_WIDTH)),
                   per_b((HG_HEADS, HG_HEAD_DIM, HG_HEAD_DIM))),
        scratch_shapes=[pltpu.VMEM((per_step, CONV_HDR + SUBLANES, LRU_WIDTH), F32)],
        compiler_params=_cparams(("arbitrary",)),
    )(proj_pad, cst_hdr, h0, s0, cw, cb, wg, bg, lam, lb, hgn)


def _to_row_tiles(ref, val, rows):
    for c in range(LANE_CHUNKS):
        ref[pl.ds(c, rows, stride=LANE_CHUNKS), :] = val[:, c * LANES:(c + 1) * LANES]


def _from_row_tiles(ref, base, rows):
    return jnp.concatenate(
        [ref[pl.ds(base + c, rows, stride=LANE_CHUNKS), :] for c in range(LANE_CHUNKS)], axis=-1)


def _post_kernel(mp_ref, ms_ref, xp_ref, xs_ref, gtmp_ref, scfp_ref, shfp_ref, gtms_ref, scfs_ref, shfs_ref,
                 npost_ref, npre_ref, wout_ref, rwt_ref, rb_ref,
                 x1_ref, hf_ref, idx_ref, gate_ref, rank_ref, cnt_out_ref, cnt, *, n_p):
    i = pl.program_id(0)
    tm = xp_ref.shape[0]

    @pl.when(i == 0)
    def _():
        cnt[...] = jnp.zeros_like(cnt)

    def run(m_ref, x_ref, gtm_ref, scf_ref, shf_ref):
        mix = jnp.dot(m_ref[...], wout_ref[...], preferred_element_type=F32)
        x1 = x_ref[...] + gtm_ref[...] * _rms(mix, npost_ref[...])
        x1_ref[...] = x1
        hf = _rms(x1, npre_ref[...]) * (1.0 + scf_ref[...]) + shf_ref[...]
        _to_row_tiles(hf_ref, hf, tm)
        logits = lax.dot_general(rwt_ref[...], hf, _NT, precision=lax.Precision.HIGHEST,
                                 preferred_element_type=F32) + rb_ref[...]
        e_iota = lax.broadcasted_iota(I32, logits.shape, 0)
        vals, idxs = [], []
        for _ in range(TOP_K):
            mx = jnp.max(logits, axis=0, keepdims=True)
            ix = jnp.min(jnp.where(logits == mx, e_iota, N_EXPERTS), axis=0, keepdims=True)
            vals.append(mx)
            idxs.append(ix)
            logits = jnp.where(e_iota == ix, -jnp.inf, logits)
        exps = [jnp.exp(vv - vals[0]) for vv in vals]
        den = exps[0] + exps[1] + exps[2] + exps[3]
        earlier = (lax.broadcasted_iota(I32, (tm, tm), 0) < lax.broadcasted_iota(I32, (tm, tm), 1))
        earlier = jnp.where(earlier, 1.0, 0.0).astype(BF16)
        base = cnt[...]
        for kk in range(TOP_K):
            idx_ref[kk:kk + 1, :] = idxs[kk]
            gate_ref[kk:kk + 1, :] = exps[kk] / den
            hit = e_iota == idxs[kk]
            onehot = jnp.where(hit, 1.0, 0.0)
            before = jnp.dot(onehot.astype(BF16), earlier, preferred_element_type=F32)
            rank = jnp.sum(jnp.where(hit, before + base, 0.0), axis=0, keepdims=True)
            rank_ref[kk:kk + 1, :] = rank.astype(I32)
            base = base + jnp.sum(onehot, axis=1, keepdims=True)
        cnt[...] = base

    @pl.when(i < n_p)
    def _():
        run(mp_ref, xp_ref, gtmp_ref, scfp_ref, shfp_ref)

    @pl.when(i >= n_p)
    def _():
        run(ms_ref, xs_ref, gtms_ref, scfs_ref, shfs_ref)

    @pl.when(i == pl.num_programs(0) - 1)
    def _():
        cnt_out_ref[...] = cnt[...]


def _post(m_p, m_s, x_p, x_s, mod_p, mod_s, npost, npre, wout_bf, rwt, rb, tm, tiles_per_seq):
    n_p, n_s = x_p.shape[0] // tm, x_s.shape[0] // tm
    t = (n_p + n_s) * tm
    first = lambda i: jnp.minimum(i, n_p - 1)
    second = lambda i: jnp.maximum(i - n_p, 0)
    row = lambda: pl.BlockSpec((1, D_MODEL), lambda i: (0, 0))
    tok_p = lambda: pl.BlockSpec((tm, D_MODEL), lambda i: (first(i), 0))
    tok_s = lambda: pl.BlockSpec((tm, D_MODEL), lambda i: (second(i), 0))
    seq_p = lambda: pl.BlockSpec((None, 1, D_MODEL), lambda i: (first(i) // tiles_per_seq, 0, 0))
    per_k = lambda: pl.BlockSpec((TOP_K, tm), lambda i: (0, i))
    return pl.pallas_call(
        functools.partial(_post_kernel, n_p=n_p),
        out_shape=(jax.ShapeDtypeStruct((t, D_MODEL), F32),
                   jax.ShapeDtypeStruct((t * LANE_CHUNKS, LANES), F32),
                   jax.ShapeDtypeStruct((TOP_K, t), I32), jax.ShapeDtypeStruct((TOP_K, t), F32),
                   jax.ShapeDtypeStruct((TOP_K, t), I32), jax.ShapeDtypeStruct((N_EXPERTS, 1), F32)),
        grid=(n_p + n_s,),
        in_specs=[tok_p(), tok_s(), tok_p(), tok_s(), seq_p(), seq_p(), seq_p(), tok_s(), tok_s(), tok_s(),
                  row(), row(),
                  pl.BlockSpec((D_MODEL, D_MODEL), lambda i: (0, 0)),
                  pl.BlockSpec((N_EXPERTS, D_MODEL), lambda i: (0, 0)),
                  pl.BlockSpec((N_EXPERTS, 1), lambda i: (0, 0))],
        out_specs=(pl.BlockSpec((tm, D_MODEL), lambda i: (i, 0)),
                   pl.BlockSpec((tm * LANE_CHUNKS, LANES), lambda i: (i, 0)),
                   per_k(), per_k(), per_k(), pl.BlockSpec((N_EXPERTS, 1), lambda i: (0, 0))),
        scratch_shapes=[pltpu.VMEM((N_EXPERTS, 1), F32)],
        compiler_params=_cparams(("arbitrary",)),
    )(m_p, m_s, x_p, x_s, *mod_p, *mod_s, npost, npre, wout_bf, rwt, rb)


def _invert_kernel(fill_lo_ref, fill_hi_ref, dest_ref, src_ref, *, tm):
    i = pl.program_id(0)

    @pl.when(i == 0)
    def _():
        def per_expert(e, carry):
            def per_slot(s, c):
                src_ref[s] = 0
                return c
            lax.fori_loop(fill_lo_ref[e], fill_hi_ref[e], per_slot, 0)
            return carry
        lax.fori_loop(0, N_EXPERTS, per_expert, 0)

    for n in range(TOP_K * tm):
        src_ref[dest_ref[0, n]] = (i * tm + n % tm) * TOP_K + n // tm


def _invert(fill_lo, fill_hi, dest_tiles, n_slots, tm):
    return pl.pallas_call(
        functools.partial(_invert_kernel, tm=tm),
        out_shape=jax.ShapeDtypeStruct((n_slots,), I32),
        grid_spec=pltpu.PrefetchScalarGridSpec(
            num_scalar_prefetch=2, grid=(dest_tiles.shape[0],),
            in_specs=[pl.BlockSpec((None, 1, TOP_K * tm), lambda i, lo, hi: (i, 0, 0), memory_space=pltpu.SMEM)],
            out_specs=pl.BlockSpec(memory_space=pltpu.SMEM)),
        compiler_params=_cparams(("arbitrary",)),
    )(fill_lo, fill_hi, dest_tiles)


def _expert_kernel(be_ref, nxt_ref, nu_ref, src_cur_ref, src_next_ref, hf_hbm, wgu_hbm, bgu_ref, wd_hbm, bd_ref,
                   ys_ref, xbuf, wgu_f32, wd_f32, wgu_bf, wd_bf, w_sem, row_sem):
    i = pl.program_id(0)
    n_used = nu_ref[0]
    slot = i % 2
    slot_len = MOE_BM * LANE_CHUNKS

    def slot_rows(s):
        return xbuf.at[pl.ds(pl.multiple_of(s * slot_len, slot_len), slot_len), :]

    def gather(src_ref, s):
        dst = slot_rows(s)
        for r in range(MOE_BM):
            token = lax.shift_right_logical(src_ref[0, r], 2)
            pltpu.make_async_copy(hf_hbm.at[token], dst.at[pl.ds(r * LANE_CHUNKS, LANE_CHUNKS), :],
                                  row_sem.at[s]).start(priority=r % 2)

    def wait_rows(s):
        pltpu.make_async_copy(slot_rows(s), slot_rows(s), row_sem.at[s]).wait()

    def weight_copies(e):
        return (pltpu.make_async_copy(wgu_hbm.at[e], wgu_f32, w_sem.at[0]),
                pltpu.make_async_copy(wd_hbm.at[e], wd_f32, w_sem.at[1]))

    @pl.when(i == 0)
    def _():
        for cp in weight_copies(be_ref[0]):
            cp.start()
        gather(src_cur_ref, 0)

    @pl.when(i < n_used)
    def _():
        prev = be_ref[jnp.maximum(i - 1, 0)]

        @pl.when((i == 0) | (be_ref[i] != prev))
        def _():
            for cp in weight_copies(be_ref[i]):
                cp.wait()
            wgu_bf[...] = wgu_f32[...].astype(BF16)
            wd_bf[...] = wd_f32[...].astype(BF16)
            nxt = nxt_ref[be_ref[i]]

            @pl.when(nxt >= 0)
            def _():
                for cp in weight_copies(nxt):
                    cp.start()

        wait_rows(slot)
        x = _from_row_tiles(xbuf, slot * slot_len, MOE_BM).astype(BF16)
        gather(src_next_ref, 1 - slot)
        hu = jnp.dot(x, wgu_bf[...], preferred_element_type=F32) + bgu_ref[...]
        glu = jnp.minimum(hu[:, :D_EXPERT], SWIGLU_LIMIT)
        lin = jnp.clip(hu[:, D_EXPERT:], -SWIGLU_LIMIT, SWIGLU_LIMIT)
        act = glu * jax.nn.sigmoid(SWIGLU_ALPHA * glu) * (lin + 1.0)
        y = jnp.dot(act.astype(BF16), wd_bf[...], preferred_element_type=F32) + bd_ref[...]
        _to_row_tiles(ys_ref, y, MOE_BM)

    @pl.when(i == n_used)
    def _():
        wait_rows(slot)

    @pl.when(i >= n_used)
    def _():
        ys_ref[...] = jnp.zeros_like(ys_ref)


def _experts(block_e, next_e, n_used, src_tiles, hf_tiles, w_gu, b_gu, w_down, b_down):
    n_blocks = block_e.shape[0]
    rows = MOE_BM * LANE_CHUNKS
    smem_tile = lambda f: pl.BlockSpec((None, 1, MOE_BM), f, memory_space=pltpu.SMEM)
    return pl.pallas_call(
        _expert_kernel,
        out_shape=jax.ShapeDtypeStruct((n_blocks * rows, LANES), F32),
        grid_spec=pltpu.PrefetchScalarGridSpec(
            num_scalar_prefetch=3,
            grid=(n_blocks,),
            in_specs=[smem_tile(lambda i, be, nx, nu: (i, 0, 0)),
                      smem_tile(lambda i, be, nx, nu: (jnp.minimum(i + 1, n_blocks - 1), 0, 0)),
                      pl.BlockSpec(memory_space=pl.ANY),
                      pl.BlockSpec(memory_space=pl.ANY),
                      pl.BlockSpec((None, 1, 2 * D_EXPERT), lambda i, be, nx, nu: (be[i], 0, 0)),
                      pl.BlockSpec(memory_space=pl.ANY),
                      pl.BlockSpec((None, 1, D_MODEL), lambda i, be, nx, nu: (be[i], 0, 0))],
            out_specs=pl.BlockSpec((rows, LANES), lambda i, be, nx, nu: (i, 0)),
            scratch_shapes=[pltpu.VMEM((2 * rows, LANES), F32),
                            pltpu.VMEM((D_MODEL, 2 * D_EXPERT), F32),
                            pltpu.VMEM((D_EXPERT, D_MODEL), F32),
                            pltpu.VMEM((D_MODEL, 2 * D_EXPERT), BF16),
                            pltpu.VMEM((D_EXPERT, D_MODEL), BF16),
                            pltpu.SemaphoreType.DMA((2,)),
                            pltpu.SemaphoreType.DMA((2,))]),
        compiler_params=_cparams(("arbitrary",)),
    )(block_e, next_e, n_used, src_tiles, src_tiles, hf_tiles, w_gu, b_gu, w_down, b_down)


def _combine_kernel(dcur_ref, dnext_ref, ys_hbm, g_ref, x1_ref, gt_ref, np_ref, o_ref, ybuf, row_sem, *, tm):
    i = pl.program_id(0)
    n_rows = TOP_K * tm
    slot = i % 2

    def slot_rows(s):
        return ybuf.at[pl.ds(pl.multiple_of(s * n_rows * LANE_CHUNKS, n_rows * LANE_CHUNKS), n_rows * LANE_CHUNKS), :]

    def issue(d_ref, s):
        dst = slot_rows(s)
        for n in range(n_rows):
            pltpu.make_async_copy(ys_hbm.at[d_ref[0, n]], dst.at[pl.ds(n * LANE_CHUNKS, LANE_CHUNKS), :],
                                  row_sem.at[s]).start(priority=n % 2)

    @pl.when(i == 0)
    def _():
        issue(dcur_ref, 0)

    @pl.when(i + 1 < pl.num_programs(0))
    def _():
        issue(dnext_ref, 1 - slot)

    pltpu.make_async_copy(slot_rows(slot), slot_rows(slot), row_sem.at[slot]).wait()
    g = g_ref[...]
    y = None
    for kk in range(TOP_K):
        rows = _from_row_tiles(ybuf, (slot * n_rows + kk * tm) * LANE_CHUNKS, tm)
        y = g[:, kk:kk + 1] * rows if y is None else y + g[:, kk:kk + 1] * rows
    o_ref[...] = x1_ref[...] + gt_ref[...] * _rms(y, np_ref[...])


def _combine(dest_tiles, ys_tiles, gates_t, x1, gt, npost, tm, tiles_per_seq, tile0):
    n_tiles = dest_tiles.shape[0]
    t = n_tiles * tm
    smem_tile = lambda f: pl.BlockSpec((None, 1, TOP_K * tm), f, memory_space=pltpu.SMEM)
    return pl.pallas_call(
        functools.partial(_combine_kernel, tm=tm),
        out_shape=jax.ShapeDtypeStruct((t, D_MODEL), F32),
        grid=(n_tiles,),
        in_specs=[smem_tile(lambda i: (i, 0, 0)),
                  smem_tile(lambda i: (jnp.minimum(i + 1, n_tiles - 1), 0, 0)),
                  pl.BlockSpec(memory_space=pl.ANY),
                  pl.BlockSpec((tm, TOP_K), lambda i: (i + tile0, 0)),
                  pl.BlockSpec((tm, D_MODEL), lambda i: (i + tile0, 0)),
                  _mod_spec(gt, tm, tiles_per_seq),
                  pl.BlockSpec((1, D_MODEL), lambda i: (0, 0))],
        out_specs=pl.BlockSpec((tm, D_MODEL), lambda i: (i, 0)),
        scratch_shapes=[pltpu.VMEM((2 * TOP_K * tm * LANE_CHUNKS, LANES), F32),
                        pltpu.SemaphoreType.DMA((2,))],
        compiler_params=_cparams(("arbitrary",)),
    )(dest_tiles, dest_tiles, ys_tiles, gates_t, x1, gt, npost)


def _expert_layout(counts, n_blocks):
    padded = (counts + MOE_BM - 1) // MOE_BM * MOE_BM
    pad_ends = jnp.cumsum(padded)
    pad_starts = pad_ends - padded
    block_start = jnp.arange(n_blocks, dtype=I32) * MOE_BM
    block_e = jnp.minimum(jnp.sum((pad_ends[None, :] <= block_start[:, None]).astype(I32), axis=1), N_EXPERTS - 1)
    n_used = (pad_ends[-1:] // MOE_BM).astype(I32)
    e_ids = jnp.arange(N_EXPERTS, dtype=I32)
    later = (counts > 0)[None, :] & (e_ids[None, :] > e_ids[:, None])
    next_e = jnp.min(jnp.where(later, e_ids[None, :], N_EXPERTS), axis=1)
    next_e = jnp.where(next_e < N_EXPERTS, next_e, -1).astype(I32)
    return pad_starts, pad_starts + counts, pad_ends, block_e, next_e, n_used


def _slots(idx, rank, pad_starts):
    onehot = idx[:, :, None] == jnp.arange(N_EXPERTS, dtype=I32)[None, None, :]
    return rank + jnp.sum(jnp.where(onehot, pad_starts[None, None, :], 0), axis=-1)


def _tile_major(a, tm):
    t = a.shape[1]
    return a.reshape(TOP_K, t // tm, tm).transpose(1, 0, 2).reshape(t // tm, 1, TOP_K * tm)


def _block_diag(wb):
    n, d, _ = wb.shape
    eye = jnp.eye(n, dtype=wb.dtype)
    return (wb[:, :, None, :] * eye[:, None, :, None]).reshape(n * d, n * d)


def _pick_tile(n, prefs):
    for p in prefs:
        if n % p == 0:
            return p
    raise ValueError(f"no tile for {n}")


def kernel(x_prompt, x_sample, state_rglru_conv, state_rglru_h, state_hgrn_S, c_prompt, c_sample, ada_w, ada_b, norm_pre_mix, norm_post_mix, norm_pre_ffn, norm_post_ffn, w_in, conv_w, conv_b, lru_wa, lru_ba, lru_wi, lru_bi, lru_lambda, hg_lb, hg_norm, w_out, router_w, router_b, w_gu, b_gu, w_down, b_down):
    assert ada_w.shape[0] == 1, "single-layer trunk"
    bp, lp, _ = x_prompt.shape
    bs, ls, _ = x_sample.shape
    tp, ts = bp * lp, bs * ls
    assert CONV_WIDTH - 1 <= ls <= SUBLANES

    w_in_bf = w_in[0].astype(BF16)
    w_out_bf = w_out[0].astype(BF16)
    wg = jnp.concatenate([_block_diag(lru_wa[0]), _block_diag(lru_wi[0])], axis=1).astype(BF16)
    bg = jnp.concatenate([lru_ba[0], lru_bi[0]])[None, :]
    lbs = jnp.cumsum(jax.nn.softmax(hg_lb.astype(F32), axis=0), axis=0)[0][None, :]
    mw = (conv_w[0], conv_b[0][None, :], wg, bg, lru_lambda[0][None, :], lbs, hg_norm[0][None, :])
    row = lambda p: p[0][None, :]

    n_c = bp + bs
    n_c_pad = -(-n_c // SUBLANES) * SUBLANES
    c_all = jnp.concatenate([c_prompt, c_sample, jnp.zeros((n_c_pad - n_c, D_MODEL), F32)], axis=0)
    mod = _ada(c_all, ada_w[0], ada_b[0][None, :])
    mod_p = [m[:, None, :] for m in jnp.split(mod[:bp], 6, axis=-1)]
    mod_s = [jnp.repeat(m, ls, axis=0) for m in jnp.split(mod[bp:n_c], 6, axis=-1)]

    xp = x_prompt.reshape(tp, D_MODEL)
    xs = x_sample.reshape(ts, D_MODEL)
    tm_p = _pick_tile(lp, (256, 128, 64))
    tm_s = _pick_tile(ts, (256, 128, 64, 32, 16, 8))

    proj_p = _inproj(xp, mod_p[1], mod_p[0], row(norm_pre_mix), w_in_bf, tm_p, lp // tm_p)
    proj_s = _inproj(xs, mod_s[1], mod_s[0], row(norm_pre_mix), w_in_bf, tm_s, 1)
    tc = _pick_tile(lp, (256, 128, 64))
    m_p, conv_p, h_p, s_p = _mixer_prompt(proj_p, bp, lp, mw, tc)

    proj_s_pad = jnp.pad(proj_s.reshape(bs, ls, IN_COLS), ((0, 0), (0, SUBLANES - ls), (0, 0)))
    cst_hdr = jnp.pad(state_rglru_conv[0], ((0, 0), (CONV_HDR - (CONV_WIDTH - 1), 0), (0, 0)))
    m_s_pad, conv_s, h_s, s_s = _mixer_sample(proj_s_pad, cst_hdr, state_rglru_h[0][:, None, :],
                                              state_hgrn_S[0], mw, ls)
    m_s = m_s_pad[:, :ls, :].reshape(ts, D_MODEL)

    tmq = _pick_tile(math.gcd(lp, ts), (512, 256, 128, 64, 32, 16, 8))
    x1, hf, idx, gate, rank, counts = _post(
        m_p, m_s, xp, xs, (mod_p[2], mod_p[4], mod_p[3]), (mod_s[2], mod_s[4], mod_s[3]),
        row(norm_post_mix), row(norm_pre_ffn), w_out_bf, router_w[0].T, router_b[0][:, None], tmq, lp // tmq)

    n_blocks = -(-(TOP_K * (tp + ts)) // MOE_BM) + N_EXPERTS + 1
    n_slots = n_blocks * MOE_BM
    pad_starts, fill_lo, pad_ends, block_e, next_e, n_used = _expert_layout(counts[:, 0].astype(I32), n_blocks)
    fill_hi = pad_ends.at[N_EXPERTS - 1].set(n_slots)
    dest = _slots(idx, rank, pad_starts)
    tmd = _pick_tile(math.gcd(lp, ts), (128, 64, 32, 16, 8))
    dtiles = _tile_major(dest, tmd)
    src = _invert(fill_lo, fill_hi, dtiles, n_slots, tmd)
    hf_tiles = hf.reshape(-1, LANE_CHUNKS, LANES)
    ys = _experts(block_e, next_e, n_used, src.reshape(n_blocks, 1, MOE_BM), hf_tiles,
                  w_gu[0], b_gu[0][:, None, :], w_down[0], b_down[0][:, None, :])
    ys_tiles = ys.reshape(-1, LANE_CHUNKS, LANES)

    gates_t = gate.T
    y_p = _combine(dtiles[:tp // tmd], ys_tiles, gates_t, x1, mod_p[5], row(norm_post_ffn), tmd, lp // tmd, 0)
    y_s = _combine(dtiles[tp // tmd:], ys_tiles, gates_t, x1, mod_s[5], row(norm_post_ffn), tmd, 1, tp // tmd)

    return (y_p.reshape(bp, lp, D_MODEL), y_s.reshape(bs, ls, D_MODEL),
            conv_p[None], h_p.reshape(1, bp, LRU_WIDTH), s_p[None],
            conv_s[None], h_s.reshape(1, bs, LRU_WIDTH), s_s[None])
```

```python
import functools
import math

import jax
import jax.numpy as jnp
from jax import lax
from jax.experimental import pallas as pl
from jax.experimental.pallas import tpu as pltpu

F32 = jnp.float32
BF16 = jnp.bfloat16
I32 = jnp.int32

D_MODEL = 1024
LRU_WIDTH = 512
LRU_BLOCKS = 8
LRU_BLOCK = LRU_WIDTH // LRU_BLOCKS
CONV_WIDTH = 4
LRU_C = 8.0
HG_WIDTH = 512
HG_HEAD_DIM = 128
HG_HEADS = 4
IN_COLS = 2 * LRU_WIDTH + 4 * HG_WIDTH
N_EXPERTS = 32
TOP_K = 4
D_EXPERT = 1024
SWIGLU_LIMIT = 7.0
SWIGLU_ALPHA = 1.702
RMS_EPS = 1e-6

C_XL, C_YL, C_Q, C_F, C_V, C_G = 0, 512, 1024, 1536, 2048, 2560

SUBLANES = 8
LANES = 128
LANE_CHUNKS = D_MODEL // LANES
CONV_HDR = SUBLANES
HG_CHUNK = 64
MOE_BM = 256
GATHER_SLOTS = 3
VMEM_LIMIT = 56 * 1024 * 1024

_NT = (((1,), (1,)), ((), ()))
_TN = (((0,), (0,)), ((), ()))


def _cparams(sem):
    return pltpu.CompilerParams(dimension_semantics=sem, vmem_limit_bytes=VMEM_LIMIT)


def _rms(x, g):
    return x * lax.rsqrt(jnp.mean(x * x, axis=-1, keepdims=True) + RMS_EPS) * g


def _gelu_tanh(x):
    c = math.sqrt(2.0 / math.pi)
    return 0.5 * x * (1.0 + jnp.tanh(c * (x + 0.044715 * (x * x * x))))


def _ada_kernel(c_ref, w_ref, b_ref, o_ref):
    c = c_ref[...]
    s = c * jax.nn.sigmoid(c)
    o_ref[...] = jnp.dot(s.astype(BF16), w_ref[...].astype(BF16), preferred_element_type=F32) + b_ref[...]


def _ada(c_all, ada_w, ada_b):
    n = c_all.shape[0]
    tn = 1024
    return pl.pallas_call(
        _ada_kernel,
        out_shape=jax.ShapeDtypeStruct((n, 6 * D_MODEL), F32),
        grid=(6 * D_MODEL // tn,),
        in_specs=[pl.BlockSpec((n, D_MODEL), lambda j: (0, 0)),
                  pl.BlockSpec((D_MODEL, tn), lambda j: (0, j)),
                  pl.BlockSpec((1, tn), lambda j: (0, j))],
        out_specs=pl.BlockSpec((n, tn), lambda j: (0, j)),
        compiler_params=_cparams(("arbitrary",)),
    )(c_all, ada_w, ada_b)


def _inproj_kernel(x_ref, sc_ref, sh_ref, g_ref, w_ref, o_ref):
    h = _rms(x_ref[...], g_ref[...]) * (1.0 + sc_ref[...]) + sh_ref[...]
    o_ref[...] = jnp.dot(h.astype(BF16), w_ref[...], preferred_element_type=F32)


def _mod_spec(mod, tm, tiles_per_seq):
    if mod.ndim == 3:
        return pl.BlockSpec((None, 1, D_MODEL), lambda i: (i // tiles_per_seq, 0, 0))
    return pl.BlockSpec((tm, D_MODEL), lambda i: (i, 0))


def _inproj(x, sc, sh, g, w_bf, tm, tiles_per_seq):
    t = x.shape[0]
    return pl.pallas_call(
        _inproj_kernel,
        out_shape=jax.ShapeDtypeStruct((t, IN_COLS), F32),
        grid=(t // tm,),
        in_specs=[pl.BlockSpec((tm, D_MODEL), lambda i: (i, 0)),
                  _mod_spec(sc, tm, tiles_per_seq), _mod_spec(sh, tm, tiles_per_seq),
                  pl.BlockSpec((1, D_MODEL), lambda i: (0, 0)),
                  pl.BlockSpec((D_MODEL, IN_COLS), lambda i: (0, 0))],
        out_specs=pl.BlockSpec((tm, IN_COLS), lambda i: (i, 0)),
        compiler_params=_cparams(("arbitrary",)),
    )(x, sc, sh, g, w_bf)


def _sigmoid(x):
    return 0.5 * jnp.tanh(0.5 * x) + 0.5


def _group_rows(x):
    rows, w = x.shape
    return x.reshape(rows // SUBLANES, SUBLANES, w)


def _scan_rows(a, u, h0):
    a3, u3 = _group_rows(a), _group_rows(u)
    r3 = lax.broadcasted_iota(I32, a3.shape, 1)
    s = 1
    while s < SUBLANES:
        keep = r3 >= s
        a_sh = jnp.where(keep, pltpu.roll(a3, s, 1), 1.0)
        u_sh = jnp.where(keep, pltpu.roll(u3, s, 1), 0.0)
        u3 = a3 * u_sh + u3
        a3 = a3 * a_sh
        s *= 2
    out, h = [], h0
    for i in range(a3.shape[0]):
        hg = u3[i] + a3[i] * h
        h = hg[SUBLANES - 1:SUBLANES, :]
        out.append(hg)
    return jnp.concatenate(out, axis=0), h


def _chunk_cumsum(x, chunk):
    x3 = _group_rows(x)
    r3 = lax.broadcasted_iota(I32, x3.shape, 1)
    s = 1
    while s < SUBLANES:
        x3 = x3 + jnp.where(r3 >= s, pltpu.roll(x3, s, 1), 0.0)
        s *= 2
    out, carry = [], None
    for i in range(x3.shape[0]):
        cur = x3[i] if i % (chunk // SUBLANES) == 0 else x3[i] + carry
        carry = cur[SUBLANES - 1:SUBLANES, :]
        out.append(cur)
    return jnp.concatenate(out, axis=0)


def _lru_gates(xc, wg_ref, bg_ref, lam_ref):
    gates = jnp.dot(xc.astype(BF16), wg_ref[...], preferred_element_type=F32) + bg_ref[...]
    r = _sigmoid(gates[:, :LRU_WIDTH])
    ig = _sigmoid(gates[:, LRU_WIDTH:])
    z = -lam_ref[...]
    softplus = jnp.maximum(z, 0.0) + jnp.log1p(jnp.exp(-jnp.abs(z)))
    log_a = -LRU_C * r * softplus
    a = jnp.exp(log_a)
    th = jnp.tanh(log_a)
    mult2 = -2.0 * th / (1.0 - th)
    return a, jnp.sqrt(mult2), ig


def _conv4(xbuf, xl, cw_ref, cb_ref, rows):
    cw = cw_ref[...]
    h = CONV_HDR
    return (cb_ref[...] + cw[0:1] * xbuf[h - 3:h - 3 + rows, :] + cw[1:2] * xbuf[h - 2:h - 2 + rows, :]
            + cw[2:3] * xbuf[h - 1:h - 1 + rows, :] + cw[3:4] * xl)


def _conv4_rolled(xl, prev, cw_ref, cb_ref):
    cw = cw_ref[...]
    row8 = lax.broadcasted_iota(I32, prev.shape, 0)
    acc = cb_ref[...] + cw[CONV_WIDTH - 1:CONV_WIDTH] * xl
    for s in range(1, CONV_WIDTH):
        sh = pltpu.roll(xl, s, 0)
        top = jnp.where(row8 < s, pltpu.roll(prev, s, 0), sh[0:SUBLANES, :])
        sh = jnp.concatenate([top, sh[SUBLANES:, :]], axis=0)
        acc = acc + cw[CONV_WIDTH - 1 - s:CONV_WIDTH - s] * sh
    return acc


def _head_norm_gate(o, gate, hgn):
    return _rms(o, hgn) * _sigmoid(gate)


def _mixer_prompt_kernel(proj_ref, cw_ref, cb_ref, wg_ref, bg_ref, lam_ref, lb_ref, hgn_ref,
                         m_ref, conv_ref, h_ref, s_ref,
                         xprev, hcar, st, obuf, *, tc):
    j = pl.program_id(1)
    w = LRU_WIDTH

    @pl.when(j == 0)
    def _():
        xprev[...] = jnp.zeros_like(xprev)
        hcar[...] = jnp.zeros_like(hcar)
        st[...] = jnp.zeros_like(st)

    xl = proj_ref[:, C_XL:C_XL + w]
    xc = _conv4_rolled(xl, xprev[...], cw_ref, cb_ref)
    xprev[...] = xl[tc - SUBLANES:tc, :]
    a, mult, ig = _lru_gates(xc, wg_ref, bg_ref, lam_ref)
    row = lax.broadcasted_iota(I32, (tc, w), 0)
    mult = jnp.where((row == 0) & (j == 0), 1.0, mult)
    h, h_last = _scan_rows(a, mult * ig * xc, hcar[0:1, :])
    hcar[0:1, :] = h_last
    m_ref[:, 0:w] = (h * _gelu_tanh(proj_ref[:, C_YL:C_YL + w])).astype(BF16)

    lb = lb_ref[...]
    f = lb + (1.0 - lb) * _sigmoid(proj_ref[:, C_F:C_F + w])
    k = 1.0 - f
    b = _chunk_cumsum(jnp.log(f), HG_CHUNK)
    q = proj_ref[:, C_Q:C_Q + w]
    v = proj_ref[:, C_V:C_V + w]
    tri = (lax.broadcasted_iota(I32, (HG_CHUNK, HG_CHUNK), 0)
           >= lax.broadcasted_iota(I32, (HG_CHUNK, HG_CHUNK), 1))
    mid = HG_CHUNK // 2
    for hd in range(HG_HEADS):
        cs = slice(hd * HG_HEAD_DIM, (hd + 1) * HG_HEAD_DIM)
        s_t = st[hd]
        for c in range(tc // HG_CHUNK):
            rs = slice(c * HG_CHUNK, (c + 1) * HG_CHUNK)
            bc, qc, kc = b[rs, cs], q[rs, cs], k[rs, cs]
            vb = v[rs, cs].astype(BF16)
            b_mid = bc[mid - 1:mid, :]
            b_end = bc[HG_CHUNK - 1:HG_CHUNK, :]
            qp = (qc * jnp.exp(bc - b_mid)).astype(BF16)
            kp = (kc * jnp.exp(b_mid - bc)).astype(BF16)
            att = lax.dot_general(qp, kp, _NT, preferred_element_type=F32)
            att = jnp.where(tri, att, 0.0)
            q_in = (qc * jnp.exp(bc)).astype(BF16)
            o = (jnp.dot(att.astype(BF16), vb, preferred_element_type=F32)
                 + lax.dot_general(q_in, s_t.astype(BF16), _NT, preferred_element_type=F32))
            k_out = (kc * jnp.exp(b_end - bc)).astype(BF16)
            s_t = s_t * jnp.exp(b_end) + lax.dot_general(vb, k_out, _TN, preferred_element_type=F32)
            obuf[rs, cs] = o
        st[hd] = s_t
    hgn = hgn_ref[...]
    for hd in range(HG_HEADS):
        cs = slice(hd * HG_HEAD_DIM, (hd + 1) * HG_HEAD_DIM)
        gate = proj_ref[:, C_G + hd * HG_HEAD_DIM:C_G + (hd + 1) * HG_HEAD_DIM]
        m_ref[:, w + hd * HG_HEAD_DIM:w + (hd + 1) * HG_HEAD_DIM] = (
            _head_norm_gate(obuf[:, cs], gate, hgn).astype(BF16))

    @pl.when(j == pl.num_programs(1) - 1)
    def _():
        conv_ref[...] = xprev[SUBLANES - (CONV_WIDTH - 1):SUBLANES, :]
        h_ref[...] = hcar[0:1, :]
        for hd in range(HG_HEADS):
            s_ref[hd] = st[hd].T


def _mixer_prompt(proj, nb, seq, mw, tc):
    cw, cb, wg, bg, lam, lb, hgn = mw
    nj = seq // tc
    const = lambda shape: pl.BlockSpec(shape, lambda b, j: tuple(0 for _ in shape))
    return pl.pallas_call(
        functools.partial(_mixer_prompt_kernel, tc=tc),
        out_shape=(jax.ShapeDtypeStruct((nb * seq, D_MODEL), BF16),
                   jax.ShapeDtypeStruct((nb, CONV_WIDTH - 1, LRU_WIDTH), F32),
                   jax.ShapeDtypeStruct((nb, 1, LRU_WIDTH), F32),
                   jax.ShapeDtypeStruct((nb, HG_HEADS, HG_HEAD_DIM, HG_HEAD_DIM), F32)),
        grid=(nb, nj),
        in_specs=[pl.BlockSpec((tc, IN_COLS), lambda b, j: (b * nj + j, 0)),
                  const((CONV_WIDTH, LRU_WIDTH)), const((1, LRU_WIDTH)),
                  const((LRU_WIDTH, 2 * LRU_WIDTH)), const((1, 2 * LRU_WIDTH)),
                  const((1, LRU_WIDTH)), const((1, HG_WIDTH)), const((1, HG_HEAD_DIM))],
        out_specs=(pl.BlockSpec((tc, D_MODEL), lambda b, j: (b * nj + j, 0)),
                   pl.BlockSpec((None, CONV_WIDTH - 1, LRU_WIDTH), lambda b, j: (b, 0, 0)),
                   pl.BlockSpec((None, 1, LRU_WIDTH), lambda b, j: (b, 0, 0)),
                   pl.BlockSpec((None, HG_HEADS, HG_HEAD_DIM, HG_HEAD_DIM), lambda b, j: (b, 0, 0, 0))),
        scratch_shapes=[pltpu.VMEM((SUBLANES, LRU_WIDTH), F32),
                        pltpu.VMEM((SUBLANES, LRU_WIDTH), F32),
                        pltpu.VMEM((HG_HEADS, HG_HEAD_DIM, HG_HEAD_DIM), F32),
                        pltpu.VMEM((tc, HG_WIDTH), F32)],
        compiler_params=_cparams(("arbitrary", "arbitrary")),
    )(proj, cw, cb, wg, bg, lam, lb, hgn)


def _mixer_sample_kernel(proj_ref, cst_ref, h0_ref, s0_ref, cw_ref, cb_ref, wg_ref, bg_ref, lam_ref, lb_ref,
                         hgn_ref, m_ref, conv_ref, h_ref, s_ref, xbuf, *, steps):
    for bi in range(proj_ref.shape[0]):
        _mixer_sample_one(proj_ref.at[bi], cst_ref.at[bi], h0_ref.at[bi], s0_ref.at[bi], cw_ref, cb_ref, wg_ref,
                          bg_ref, lam_ref, lb_ref, hgn_ref, m_ref.at[bi], conv_ref.at[bi], h_ref.at[bi],
                          s_ref.at[bi], xbuf.at[bi], steps)


def _mixer_sample_one(proj_ref, cst_ref, h0_ref, s0_ref, cw_ref, cb_ref, wg_ref, bg_ref, lam_ref, lb_ref,
                      hgn_ref, m_ref, conv_ref, h_ref, s_ref, xbuf, steps):
    w = LRU_WIDTH
    rows = SUBLANES
    row = lax.broadcasted_iota(I32, (rows, w), 0)
    valid = row < steps

    xl = proj_ref[:, C_XL:C_XL + w]
    xbuf[0:CONV_HDR, :] = cst_ref[...]
    xbuf[CONV_HDR:CONV_HDR + rows, :] = xl
    xc = _conv4(xbuf, xl, cw_ref, cb_ref, rows)
    conv_ref[...] = xbuf[CONV_HDR + steps - 3:CONV_HDR + steps, :]
    a, mult, ig = _lru_gates(xc, wg_ref, bg_ref, lam_ref)
    u = mult * ig * xc
    h = h0_ref[...]
    hs = jnp.zeros((rows, w), F32)
    for t in range(steps):
        h = a[t:t + 1, :] * h + u[t:t + 1, :]
        hs = jnp.where(row == t, h, hs)
    h_ref[...] = h
    m_ref[:, 0:w] = (hs * _gelu_tanh(proj_ref[:, C_YL:C_YL + w])).astype(BF16)

    lb = lb_ref[...]
    f = lb + (1.0 - lb) * _sigmoid(proj_ref[:, C_F:C_F + w])
    k = 1.0 - f
    b = jnp.log(f)
    s = 1
    while s < rows:
        b = b + jnp.where(row >= s, pltpu.roll(b, s, 0), 0.0)
        s *= 2
    q = proj_ref[:, C_Q:C_Q + w]
    v = proj_ref[:, C_V:C_V + w]
    b_end = b[steps - 1:steps, :]
    q_in = jnp.where(valid, q * jnp.exp(b), 0.0)
    k_out = jnp.where(valid, k * jnp.exp(jnp.where(valid, b_end - b, 0.0)), 0.0)
    v_ok = jnp.where(valid, v, 0.0)
    hgn = hgn_ref[...]
    row_h = lax.broadcasted_iota(I32, (rows, HG_HEAD_DIM), 0)
    for hd in range(HG_HEADS):
        cs = slice(hd * HG_HEAD_DIM, (hd + 1) * HG_HEAD_DIM)
        s_t = s0_ref[hd].T
        o = lax.dot_general(q_in[:, cs].astype(BF16), s_t.astype(BF16), _NT, preferred_element_type=F32)
        for d in range(steps):
            ok = (row_h < steps) & (row_h >= d)
            k_d = k[:, cs] if d == 0 else pltpu.roll(k[:, cs], d, 0)
            v_d = v[:, cs] if d == 0 else pltpu.roll(v[:, cs], d, 0)
            b_d = b[:, cs] if d == 0 else pltpu.roll(b[:, cs], d, 0)
            decay = jnp.exp(jnp.where(ok, b[:, cs] - b_d, 0.0))
            wgt = jnp.sum(jnp.where(ok, q[:, cs] * k_d * decay, 0.0), axis=-1, keepdims=True)
            o = o + wgt * v_d
        s_new = s_t * jnp.exp(b_end[:, cs]) + lax.dot_general(
            v_ok[:, cs].astype(BF16), k_out[:, cs].astype(BF16), _TN, preferred_element_type=F32)
        s_ref[hd] = s_new.T
        gate = proj_ref[:, C_G + hd * HG_HEAD_DIM:C_G + (hd + 1) * HG_HEAD_DIM]
        m_ref[:, w + hd * HG_HEAD_DIM:w + (hd + 1) * HG_HEAD_DIM] = _head_norm_gate(o, gate, hgn).astype(BF16)


def _mixer_sample(proj_pad, cst_hdr, h0, s0, mw, steps):
    cw, cb, wg, bg, lam, lb, hgn = mw
    nb = proj_pad.shape[0]
    per_step = _pick_tile(nb, (4, 2, 1))
    const = lambda shape: pl.BlockSpec(shape, lambda b: tuple(0 for _ in shape))
    per_b = lambda shape: pl.BlockSpec((per_step,) + shape, lambda b: (b,) + tuple(0 for _ in shape))
    return pl.pallas_call(
        functools.partial(_mixer_sample_kernel, steps=steps),
        out_shape=(jax.ShapeDtypeStruct((nb, SUBLANES, D_MODEL), BF16),
                   jax.ShapeDtypeStruct((nb, CONV_WIDTH - 1, LRU_WIDTH), F32),
                   jax.ShapeDtypeStruct((nb, 1, LRU_WIDTH), F32),
                   jax.ShapeDtypeStruct((nb, HG_HEADS, HG_HEAD_DIM, HG_HEAD_DIM), F32)),
        grid=(nb // per_step,),
        in_specs=[per_b((SUBLANES, IN_COLS)), per_b((CONV_HDR, LRU_WIDTH)), per_b((1, LRU_WIDTH)),
                  per_b((HG_HEADS, HG_HEAD_DIM, HG_HEAD_DIM)),
                  const((CONV_WIDTH, LRU_WIDTH)), const((1, LRU_WIDTH)),
                  const((LRU_WIDTH, 2 * LRU_WIDTH)), const((1, 2 * LRU_WIDTH)),
                  const((1, LRU_WIDTH)), const((1, HG_WIDTH)), const((1, HG_HEAD_DIM))],
        out_specs=(per_b((SUBLANES, D_MODEL)), per_b((CONV_WIDTH - 1, LRU_WIDTH)), per_b((1, LRU_WIDTH)),
                   per_b((HG_HEADS, HG_HEAD_DIM, HG_HEAD_DIM))),
        scratch_shapes=[pltpu.VMEM((per_step, CONV_HDR + SUBLANES, LRU_WIDTH), F32)],
        compiler_params=_cparams(("arbitrary",)),
    )(proj_pad, cst_hdr, h0, s0, cw, cb, wg, bg, lam, lb, hgn)


def _to_row_tiles(ref, val, rows):
    for c in range(LANE_CHUNKS):
        ref[pl.ds(c, rows, stride=LANE_CHUNKS), :] = val[:, c * LANES:(c + 1) * LANES]


def _from_row_tiles(ref, base, rows):
    return jnp.concatenate(
        [ref[pl.ds(base + c, rows, stride=LANE_CHUNKS), :] for c in range(LANE_CHUNKS)], axis=-1)


def _post_kernel(mp_ref, ms_ref, xp_ref, xs_ref, gtmp_ref, scfp_ref, shfp_ref, gtms_ref, scfs_ref, shfs_ref,
                 npost_ref, npre_ref, wout_ref, rwt_ref, rb_ref,
                 x1_ref, hf_ref, idx_ref, gate_ref, rank_ref, cnt_out_ref, cnt, *, n_p):
    i = pl.program_id(0)
    tm = xp_ref.shape[0]

    @pl.when(i == 0)
    def _():
        cnt[...] = jnp.zeros_like(cnt)

    def run(m_ref, x_ref, gtm_ref, scf_ref, shf_ref):
        mix = jnp.dot(m_ref[...], wout_ref[...], preferred_element_type=F32)
        x1 = x_ref[...] + gtm_ref[...] * _rms(mix, npost_ref[...])
        x1_ref[...] = x1
        hf = _rms(x1, npre_ref[...]) * (1.0 + scf_ref[...]) + shf_ref[...]
        _to_row_tiles(hf_ref, hf, tm)
        logits = lax.dot_general(rwt_ref[...], hf, _NT, precision=lax.Precision.HIGHEST,
                                 preferred_element_type=F32) + rb_ref[...]
        e_iota = lax.broadcasted_iota(I32, logits.shape, 0)
        vals, idxs = [], []
        for _ in range(TOP_K):
            mx = jnp.max(logits, axis=0, keepdims=True)
            ix = jnp.min(jnp.where(logits == mx, e_iota, N_EXPERTS), axis=0, keepdims=True)
            vals.append(mx)
            idxs.append(ix)
            logits = jnp.where(e_iota == ix, -jnp.inf, logits)
        exps = [jnp.exp(vv - vals[0]) for vv in vals]
        den = exps[0] + exps[1] + exps[2] + exps[3]
        earlier = (lax.broadcasted_iota(I32, (tm, tm), 0) < lax.broadcasted_iota(I32, (tm, tm), 1))
        earlier = jnp.where(earlier, 1.0, 0.0).astype(BF16)
        base = cnt[...]
        for kk in range(TOP_K):
            idx_ref[kk:kk + 1, :] = idxs[kk]
            gate_ref[kk:kk + 1, :] = exps[kk] / den
            hit = e_iota == idxs[kk]
            onehot = jnp.where(hit, 1.0, 0.0)
            before = jnp.dot(onehot.astype(BF16), earlier, preferred_element_type=F32)
            rank = jnp.sum(jnp.where(hit, before + base, 0.0), axis=0, keepdims=True)
            rank_ref[kk:kk + 1, :] = rank.astype(I32)
            base = base + jnp.sum(onehot, axis=1, keepdims=True)
        cnt[...] = base

    @pl.when(i < n_p)
    def _():
        run(mp_ref, xp_ref, gtmp_ref, scfp_ref, shfp_ref)

    @pl.when(i >= n_p)
    def _():
        run(ms_ref, xs_ref, gtms_ref, scfs_ref, shfs_ref)

    @pl.when(i == pl.num_programs(0) - 1)
    def _():
        cnt_out_ref[...] = cnt[...]


def _post(m_p, m_s, x_p, x_s, mod_p, mod_s, npost, npre, wout_bf, rwt, rb, tm, tiles_per_seq):
    n_p, n_s = x_p.shape[0] // tm, x_s.shape[0] // tm
    t = (n_p + n_s) * tm
    first = lambda i: jnp.minimum(i, n_p - 1)
    second = lambda i: jnp.maximum(i - n_p, 0)
    row = lambda: pl.BlockSpec((1, D_MODEL), lambda i: (0, 0))
    tok_p = lambda: pl.BlockSpec((tm, D_MODEL), lambda i: (first(i), 0))
    tok_s = lambda: pl.BlockSpec((tm, D_MODEL), lambda i: (second(i), 0))
    seq_p = lambda: pl.BlockSpec((None, 1, D_MODEL), lambda i: (first(i) // tiles_per_seq, 0, 0))
    per_k = lambda: pl.BlockSpec((TOP_K, tm), lambda i: (0, i))
    return pl.pallas_call(
        functools.partial(_post_kernel, n_p=n_p),
        out_shape=(jax.ShapeDtypeStruct((t, D_MODEL), F32),
                   jax.ShapeDtypeStruct((t * LANE_CHUNKS, LANES), F32),
                   jax.ShapeDtypeStruct((TOP_K, t), I32), jax.ShapeDtypeStruct((TOP_K, t), F32),
                   jax.ShapeDtypeStruct((TOP_K, t), I32), jax.ShapeDtypeStruct((N_EXPERTS, 1), F32)),
        grid=(n_p + n_s,),
        in_specs=[tok_p(), tok_s(), tok_p(), tok_s(), seq_p(), seq_p(), seq_p(), tok_s(), tok_s(), tok_s(),
                  row(), row(),
                  pl.BlockSpec((D_MODEL, D_MODEL), lambda i: (0, 0)),
                  pl.BlockSpec((N_EXPERTS, D_MODEL), lambda i: (0, 0)),
                  pl.BlockSpec((N_EXPERTS, 1), lambda i: (0, 0))],
        out_specs=(pl.BlockSpec((tm, D_MODEL), lambda i: (i, 0)),
                   pl.BlockSpec((tm * LANE_CHUNKS, LANES), lambda i: (i, 0)),
                   per_k(), per_k(), per_k(), pl.BlockSpec((N_EXPERTS, 1), lambda i: (0, 0))),
        scratch_shapes=[pltpu.VMEM((N_EXPERTS, 1), F32)],
        compiler_params=_cparams(("arbitrary",)),
    )(m_p, m_s, x_p, x_s, *mod_p, *mod_s, npost, npre, wout_bf, rwt, rb)


def _invert_kernel(fill_lo_ref, fill_hi_ref, dest_ref, src_ref, *, tm):
    i = pl.program_id(0)

    @pl.when(i == 0)
    def _():
        def per_expert(e, carry):
            def per_slot(s, c):
                src_ref[s] = 0
                return c
            lax.fori_loop(fill_lo_ref[e], fill_hi_ref[e], per_slot, 0)
            return carry
        lax.fori_loop(0, N_EXPERTS, per_expert, 0)

    for n in range(TOP_K * tm):
        src_ref[dest_ref[0, n]] = (i * tm + n % tm) * TOP_K + n // tm


def _invert(fill_lo, fill_hi, dest_tiles, n_slots, tm):
    return pl.pallas_call(
        functools.partial(_invert_kernel, tm=tm),
        out_shape=jax.ShapeDtypeStruct((n_slots,), I32),
        grid_spec=pltpu.PrefetchScalarGridSpec(
            num_scalar_prefetch=2, grid=(dest_tiles.shape[0],),
            in_specs=[pl.BlockSpec((None, 1, TOP_K * tm), lambda i, lo, hi: (i, 0, 0), memory_space=pltpu.SMEM)],
            out_specs=pl.BlockSpec(memory_space=pltpu.SMEM)),
        compiler_params=_cparams(("arbitrary",)),
    )(fill_lo, fill_hi, dest_tiles)


def _expert_kernel(be_ref, nxt_ref, nu_ref, src0_ref, src1_ref, src2_ref, hf_hbm, wgu_hbm, bgu_ref, wd_hbm, bd_ref,
                   ys_ref, xbuf, wgu_f32, wd_f32, wgu_bf, wd_bf, w_sem, row_sem):
    i = pl.program_id(0)
    n_used = nu_ref[0]
    slot = lax.rem(i, GATHER_SLOTS)
    slot_len = MOE_BM * LANE_CHUNKS

    def slot_rows(s):
        return xbuf.at[pl.ds(pl.multiple_of(s * slot_len, slot_len), slot_len), :]

    def gather(src_ref, s):
        dst = slot_rows(s)
        for r in range(MOE_BM):
            token = lax.shift_right_logical(src_ref[0, r], 2)
            pltpu.make_async_copy(hf_hbm.at[token], dst.at[pl.ds(r * LANE_CHUNKS, LANE_CHUNKS), :],
                                  row_sem.at[s]).start(priority=0)

    def wait_rows(s):
        pltpu.make_async_copy(slot_rows(s), slot_rows(s), row_sem.at[s]).wait()

    def weight_copies(e):
        return (pltpu.make_async_copy(wgu_hbm.at[e], wgu_f32, w_sem.at[0]),
                pltpu.make_async_copy(wd_hbm.at[e], wd_f32, w_sem.at[1]))

    @pl.when(i == 0)
    def _():
        for cp in weight_copies(be_ref[0]):
            cp.start(priority=1)
        gather(src0_ref, 0)
        gather(src1_ref, 1)

    @pl.when(i < n_used)
    def _():
        prev = be_ref[jnp.maximum(i - 1, 0)]

        @pl.when((i == 0) | (be_ref[i] != prev))
        def _():
            for cp in weight_copies(be_ref[i]):
                cp.wait()
            wgu_bf[...] = wgu_f32[...].astype(BF16)
            wd_bf[...] = wd_f32[...].astype(BF16)
            nxt = nxt_ref[be_ref[i]]

            @pl.when(nxt >= 0)
            def _():
                for cp in weight_copies(nxt):
                    cp.start(priority=1)

        wait_rows(slot)
        x = _from_row_tiles(xbuf, slot * slot_len, MOE_BM).astype(BF16)
        gather(src2_ref, lax.rem(i + 2, GATHER_SLOTS))
        hu = jnp.dot(x, wgu_bf[...], preferred_element_type=F32) + bgu_ref[...]
        glu = jnp.minimum(hu[:, :D_EXPERT], SWIGLU_LIMIT)
        lin = jnp.clip(hu[:, D_EXPERT:], -SWIGLU_LIMIT, SWIGLU_LIMIT)
        act = glu * jax.nn.sigmoid(SWIGLU_ALPHA * glu) * (lin + 1.0)
        y = jnp.dot(act.astype(BF16), wd_bf[...], preferred_element_type=F32) + bd_ref[...]
        _to_row_tiles(ys_ref, y, MOE_BM)

    @pl.when((i == n_used) | (i == n_used + 1))
    def _():
        wait_rows(slot)

    @pl.when(i >= n_used)
    def _():
        ys_ref[...] = jnp.zeros_like(ys_ref)


def _experts(block_e, next_e, n_used, src_tiles, hf_tiles, w_gu, b_gu, w_down, b_down):
    n_blocks = block_e.shape[0]
    rows = MOE_BM * LANE_CHUNKS
    smem_tile = lambda f: pl.BlockSpec((None, 1, MOE_BM), f, memory_space=pltpu.SMEM)
    return pl.pallas_call(
        _expert_kernel,
        out_shape=jax.ShapeDtypeStruct((n_blocks * rows, LANES), F32),
        grid_spec=pltpu.PrefetchScalarGridSpec(
            num_scalar_prefetch=3,
            grid=(n_blocks,),
            in_specs=[smem_tile(lambda i, be, nx, nu: (i, 0, 0)),
                      smem_tile(lambda i, be, nx, nu: (jnp.minimum(i + 1, n_blocks - 1), 0, 0)),
                      smem_tile(lambda i, be, nx, nu: (jnp.minimum(i + 2, n_blocks - 1), 0, 0)),
                      pl.BlockSpec(memory_space=pl.ANY),
                      pl.BlockSpec(memory_space=pl.ANY),
                      pl.BlockSpec((None, 1, 2 * D_EXPERT), lambda i, be, nx, nu: (be[i], 0, 0)),
                      pl.BlockSpec(memory_space=pl.ANY),
                      pl.BlockSpec((None, 1, D_MODEL), lambda i, be, nx, nu: (be[i], 0, 0))],
            out_specs=pl.BlockSpec((rows, LANES), lambda i, be, nx, nu: (i, 0)),
            scratch_shapes=[pltpu.VMEM((GATHER_SLOTS * rows, LANES), F32),
                            pltpu.VMEM((D_MODEL, 2 * D_EXPERT), F32),
                            pltpu.VMEM((D_EXPERT, D_MODEL), F32),
                            pltpu.VMEM((D_MODEL, 2 * D_EXPERT), BF16),
                            pltpu.VMEM((D_EXPERT, D_MODEL), BF16),
                            pltpu.SemaphoreType.DMA((2,)),
                            pltpu.SemaphoreType.DMA((GATHER_SLOTS,))]),
        compiler_params=_cparams(("arbitrary",)),
    )(block_e, next_e, n_used, src_tiles, src_tiles, src_tiles, hf_tiles, w_gu, b_gu, w_down, b_down)


def _combine_kernel(dcur_ref, dnext_ref, ys_hbm, g_ref, x1_ref, gt_ref, np_ref, o_ref, ybuf, row_sem, *, tm):
    i = pl.program_id(0)
    n_rows = TOP_K * tm
    slot = i % 2

    def slot_rows(s):
        return ybuf.at[pl.ds(pl.multiple_of(s * n_rows * LANE_CHUNKS, n_rows * LANE_CHUNKS), n_rows * LANE_CHUNKS), :]

    def issue(d_ref, s):
        dst = slot_rows(s)
        for n in range(n_rows):
            pltpu.make_async_copy(ys_hbm.at[d_ref[0, n]], dst.at[pl.ds(n * LANE_CHUNKS, LANE_CHUNKS), :],
                                  row_sem.at[s]).start(priority=n % 2)

    @pl.when(i == 0)
    def _():
        issue(dcur_ref, 0)

    @pl.when(i + 1 < pl.num_programs(0))
    def _():
        issue(dnext_ref, 1 - slot)

    pltpu.make_async_copy(slot_rows(slot), slot_rows(slot), row_sem.at[slot]).wait()
    g = g_ref[...]
    y = None
    for kk in range(TOP_K):
        rows = _from_row_tiles(ybuf, (slot * n_rows + kk * tm) * LANE_CHUNKS, tm)
        y = g[:, kk:kk + 1] * rows if y is None else y + g[:, kk:kk + 1] * rows
    o_ref[...] = x1_ref[...] + gt_ref[...] * _rms(y, np_ref[...])


def _combine(dest_tiles, ys_tiles, gates_t, x1, gt, npost, tm, tiles_per_seq, tile0):
    n_tiles = dest_tiles.shape[0]
    t = n_tiles * tm
    smem_tile = lambda f: pl.BlockSpec((None, 1, TOP_K * tm), f, memory_space=pltpu.SMEM)
    return pl.pallas_call(
        functools.partial(_combine_kernel, tm=tm),
        out_shape=jax.ShapeDtypeStruct((t, D_MODEL), F32),
        grid=(n_tiles,),
        in_specs=[smem_tile(lambda i: (i, 0, 0)),
                  smem_tile(lambda i: (jnp.minimum(i + 1, n_tiles - 1), 0, 0)),
                  pl.BlockSpec(memory_space=pl.ANY),
                  pl.BlockSpec((tm, TOP_K), lambda i: (i + tile0, 0)),
                  pl.BlockSpec((tm, D_MODEL), lambda i: (i + tile0, 0)),
                  _mod_spec(gt, tm, tiles_per_seq),
                  pl.BlockSpec((1, D_MODEL), lambda i: (0, 0))],
        out_specs=pl.BlockSpec((tm, D_MODEL), lambda i: (i, 0)),
        scratch_shapes=[pltpu.VMEM((2 * TOP_K * tm * LANE_CHUNKS, LANES), F32),
                        pltpu.SemaphoreType.DMA((2,))],
        compiler_params=_cparams(("arbitrary",)),
    )(dest_tiles, dest_tiles, ys_tiles, gates_t, x1, gt, npost)


def _expert_layout(counts, n_blocks):
    padded = (counts + MOE_BM - 1) // MOE_BM * MOE_BM
    pad_ends = jnp.cumsum(padded)
    pad_starts = pad_ends - padded
    block_start = jnp.arange(n_blocks, dtype=I32) * MOE_BM
    block_e = jnp.minimum(jnp.sum((pad_ends[None, :] <= block_start[:, None]).astype(I32), axis=1), N_EXPERTS - 1)
    n_used = (pad_ends[-1:] // MOE_BM).astype(I32)
    e_ids = jnp.arange(N_EXPERTS, dtype=I32)
    later = (counts > 0)[None, :] & (e_ids[None, :] > e_ids[:, None])
    next_e = jnp.min(jnp.where(later, e_ids[None, :], N_EXPERTS), axis=1)
    next_e = jnp.where(next_e < N_EXPERTS, next_e, -1).astype(I32)
    return pad_starts, pad_starts + counts, pad_ends, block_e, next_e, n_used


def _slots(idx, rank, pad_starts):
    onehot = idx[:, :, None] == jnp.arange(N_EXPERTS, dtype=I32)[None, None, :]
    return rank + jnp.sum(jnp.where(onehot, pad_starts[None, None, :], 0), axis=-1)


def _tile_major(a, tm):
    t = a.shape[1]
    return a.reshape(TOP_K, t // tm, tm).transpose(1, 0, 2).reshape(t // tm, 1, TOP_K * tm)


def _block_diag(wb):
    n, d, _ = wb.shape
    eye = jnp.eye(n, dtype=wb.dtype)
    return (wb[:, :, None, :] * eye[:, None, :, None]).reshape(n * d, n * d)


def _pick_tile(n, prefs):
    for p in prefs:
        if n % p == 0:
            return p
    raise ValueError(f"no tile for {n}")


def kernel(x_prompt, x_sample, state_rglru_conv, state_rglru_h, state_hgrn_S, c_prompt, c_sample, ada_w, ada_b, norm_pre_mix, norm_post_mix, norm_pre_ffn, norm_post_ffn, w_in, conv_w, conv_b, lru_wa, lru_ba, lru_wi, lru_bi, lru_lambda, hg_lb, hg_norm, w_out, router_w, router_b, w_gu, b_gu, w_down, b_down):
    assert ada_w.shape[0] == 1, "single-layer trunk"
    bp, lp, _ = x_prompt.shape
    bs, ls, _ = x_sample.shape
    tp, ts = bp * lp, bs * ls
    assert CONV_WIDTH - 1 <= ls <= SUBLANES

    w_in_bf = w_in[0].astype(BF16)
    w_out_bf = w_out[0].astype(BF16)
    wg = jnp.concatenate([_block_diag(lru_wa[0]), _block_diag(lru_wi[0])], axis=1).astype(BF16)
    bg = jnp.concatenate([lru_ba[0], lru_bi[0]])[None, :]
    lbs = jnp.cumsum(jax.nn.softmax(hg_lb.astype(F32), axis=0), axis=0)[0][None, :]
    mw = (conv_w[0], conv_b[0][None, :], wg, bg, lru_lambda[0][None, :], lbs, hg_norm[0][None, :])
    row = lambda p: p[0][None, :]

    n_c = bp + bs
    n_c_pad = -(-n_c // SUBLANES) * SUBLANES
    c_all = jnp.concatenate([c_prompt, c_sample, jnp.zeros((n_c_pad - n_c, D_MODEL), F32)], axis=0)
    mod = _ada(c_all, ada_w[0], ada_b[0][None, :])
    mod_p = [m[:, None, :] for m in jnp.split(mod[:bp], 6, axis=-1)]
    mod_s = [jnp.repeat(m, ls, axis=0) for m in jnp.split(mod[bp:n_c], 6, axis=-1)]

    xp = x_prompt.reshape(tp, D_MODEL)
    xs = x_sample.reshape(ts, D_MODEL)
    tm_p = _pick_tile(lp, (256, 128, 64))
    tm_s = _pick_tile(ts, (256, 128, 64, 32, 16, 8))

    proj_p = _inproj(xp, mod_p[1], mod_p[0], row(norm_pre_mix), w_in_bf, tm_p, lp // tm_p)
    proj_s = _inproj(xs, mod_s[1], mod_s[0], row(norm_pre_mix), w_in_bf, tm_s, 1)
    tc = _pick_tile(lp, (256, 128, 64))
    m_p, conv_p, h_p, s_p = _mixer_prompt(proj_p, bp, lp, mw, tc)

    proj_s_pad = jnp.pad(proj_s.reshape(bs, ls, IN_COLS), ((0, 0), (0, SUBLANES - ls), (0, 0)))
    cst_hdr = jnp.pad(state_rglru_conv[0], ((0, 0), (CONV_HDR - (CONV_WIDTH - 1), 0), (0, 0)))
    m_s_pad, conv_s, h_s, s_s = _mixer_sample(proj_s_pad, cst_hdr, state_rglru_h[0][:, None, :],
                                              state_hgrn_S[0], mw, ls)
    m_s = m_s_pad[:, :ls, :].reshape(ts, D_MODEL)

    tmq = _pick_tile(math.gcd(lp, ts), (512, 256, 128, 64, 32, 16, 8))
    x1, hf, idx, gate, rank, counts = _post(
        m_p, m_s, xp, xs, (mod_p[2], mod_p[4], mod_p[3]), (mod_s[2], mod_s[4], mod_s[3]),
        row(norm_post_mix), row(norm_pre_ffn), w_out_bf, router_w[0].T, router_b[0][:, None], tmq, lp // tmq)

    n_blocks = -(-(TOP_K * (tp + ts)) // MOE_BM) + N_EXPERTS + GATHER_SLOTS - 1
    n_slots = n_blocks * MOE_BM
    pad_starts, fill_lo, pad_ends, block_e, next_e, n_used = _expert_layout(counts[:, 0].astype(I32), n_blocks)
    fill_hi = pad_ends.at[N_EXPERTS - 1].set(n_slots)
    dest = _slots(idx, rank, pad_starts)
    tmd = _pick_tile(math.gcd(lp, ts), (128, 64, 32, 16, 8))
    dtiles = _tile_major(dest, tmd)
    src = _invert(fill_lo, fill_hi, dtiles, n_slots, tmd)
    hf_tiles = hf.reshape(-1, LANE_CHUNKS, LANES)
    ys = _experts(block_e, next_e, n_used, src.reshape(n_blocks, 1, MOE_BM), hf_tiles,
                  w_gu[0], b_gu[0][:, None, :], w_down[0], b_down[0][:, None, :])
    ys_tiles = ys.reshape(-1, LANE_CHUNKS, LANES)

    gates_t = gate.T
    y_p = _combine(dtiles[:tp // tmd], ys_tiles, gates_t, x1, mod_p[5], row(norm_post_ffn), tmd, lp // tmd, 0)
    y_s = _combine(dtiles[tp // tmd:], ys_tiles, gates_t, x1, mod_s[5], row(norm_post_ffn), tmd, 1, tp // tmd)

    return (y_p.reshape(bp, lp, D_MODEL), y_s.reshape(bs, ls, D_MODEL),
            conv_p[None], h_p.reshape(1, bp, LRU_WIDTH), s_p[None],
            conv_s[None], h_s.reshape(1, bs, LRU_WIDTH), s_s[None])
```

```python
import functools
import math

import jax
import jax.numpy as jnp
from jax import lax
from jax.experimental import pallas as pl
from jax.experimental.pallas import tpu as pltpu

F32 = jnp.float32
BF16 = jnp.bfloat16
I32 = jnp.int32

D_MODEL = 1024
LRU_WIDTH = 512
LRU_BLOCKS = 8
LRU_BLOCK = LRU_WIDTH // LRU_BLOCKS
CONV_WIDTH = 4
LRU_C = 8.0
HG_WIDTH = 512
HG_HEAD_DIM = 128
HG_HEADS = 4
IN_COLS = 2 * LRU_WIDTH + 4 * HG_WIDTH
N_EXPERTS = 32
TOP_K = 4
D_EXPERT = 1024
SWIGLU_LIMIT = 7.0
SWIGLU_ALPHA = 1.702
RMS_EPS = 1e-6

C_XL, C_YL, C_Q, C_F, C_V, C_G = 0, 512, 1024, 1536, 2048, 2560

SUBLANES = 8
LANES = 128
LANE_CHUNKS = D_MODEL // LANES
CONV_HDR = SUBLANES
HG_CHUNK = 64
MOE_BM = 256
GATHER_SLOTS = 3
VMEM_LIMIT = 56 * 1024 * 1024

_NT = (((1,), (1,)), ((), ()))
_TN = (((0,), (0,)), ((), ()))


def _cparams(sem):
    return pltpu.CompilerParams(dimension_semantics=sem, vmem_limit_bytes=VMEM_LIMIT)


def _rms(x, g):
    return x * lax.rsqrt(jnp.mean(x * x, axis=-1, keepdims=True) + RMS_EPS) * g


def _gelu_tanh(x):
    c = math.sqrt(2.0 / math.pi)
    return 0.5 * x * (1.0 + jnp.tanh(c * (x + 0.044715 * (x * x * x))))


def _ada_kernel(c_ref, w_ref, b_ref, o_ref):
    c = c_ref[...]
    s = c * jax.nn.sigmoid(c)
    o_ref[...] = jnp.dot(s.astype(BF16), w_ref[...].astype(BF16), preferred_element_type=F32) + b_ref[...]


def _ada(c_all, ada_w, ada_b):
    n = c_all.shape[0]
    tn = 1024
    return pl.pallas_call(
        _ada_kernel,
        out_shape=jax.ShapeDtypeStruct((n, 6 * D_MODEL), F32),
        grid=(6 * D_MODEL // tn,),
        in_specs=[pl.BlockSpec((n, D_MODEL), lambda j: (0, 0)),
                  pl.BlockSpec((D_MODEL, tn), lambda j: (0, j)),
                  pl.BlockSpec((1, tn), lambda j: (0, j))],
        out_specs=pl.BlockSpec((n, tn), lambda j: (0, j)),
        compiler_params=_cparams(("arbitrary",)),
    )(c_all, ada_w, ada_b)


def _inproj_kernel(x_ref, sc_ref, sh_ref, g_ref, w_ref, o_ref):
    h = _rms(x_ref[...], g_ref[...]) * (1.0 + sc_ref[...]) + sh_ref[...]
    o_ref[...] = jnp.dot(h.astype(BF16), w_ref[...], preferred_element_type=F32)


def _mod_spec(mod, tm, tiles_per_seq):
    if mod.ndim == 3:
        return pl.BlockSpec((None, 1, D_MODEL), lambda i: (i // tiles_per_seq, 0, 0))
    return pl.BlockSpec((tm, D_MODEL), lambda i: (i, 0))


def _inproj(x, sc, sh, g, w_bf, tm, tiles_per_seq):
    t = x.shape[0]
    return pl.pallas_call(
        _inproj_kernel,
        out_shape=jax.ShapeDtypeStruct((t, IN_COLS), F32),
        grid=(t // tm,),
        in_specs=[pl.BlockSpec((tm, D_MODEL), lambda i: (i, 0)),
                  _mod_spec(sc, tm, tiles_per_seq), _mod_spec(sh, tm, tiles_per_seq),
                  pl.BlockSpec((1, D_MODEL), lambda i: (0, 0)),
                  pl.BlockSpec((D_MODEL, IN_COLS), lambda i: (0, 0))],
        out_specs=pl.BlockSpec((tm, IN_COLS), lambda i: (i, 0)),
        compiler_params=_cparams(("arbitrary",)),
    )(x, sc, sh, g, w_bf)


def _sigmoid(x):
    return 0.5 * jnp.tanh(0.5 * x) + 0.5


def _group_rows(x):
    rows, w = x.shape
    return x.reshape(rows // SUBLANES, SUBLANES, w)


def _scan_rows(a, u, h0):
    a3, u3 = _group_rows(a), _group_rows(u)
    r3 = lax.broadcasted_iota(I32, a3.shape, 1)
    s = 1
    while s < SUBLANES:
        keep = r3 >= s
        a_sh = jnp.where(keep, pltpu.roll(a3, s, 1), 1.0)
        u_sh = jnp.where(keep, pltpu.roll(u3, s, 1), 0.0)
        u3 = a3 * u_sh + u3
        a3 = a3 * a_sh
        s *= 2
    out, h = [], h0
    for i in range(a3.shape[0]):
        hg = u3[i] + a3[i] * h
        h = hg[SUBLANES - 1:SUBLANES, :]
        out.append(hg)
    return jnp.concatenate(out, axis=0), h


def _chunk_cumsum(x, chunk):
    x3 = _group_rows(x)
    r3 = lax.broadcasted_iota(I32, x3.shape, 1)
    s = 1
    while s < SUBLANES:
        x3 = x3 + jnp.where(r3 >= s, pltpu.roll(x3, s, 1), 0.0)
        s *= 2
    out, carry = [], None
    for i in range(x3.shape[0]):
        cur = x3[i] if i % (chunk // SUBLANES) == 0 else x3[i] + carry
        carry = cur[SUBLANES - 1:SUBLANES, :]
        out.append(cur)
    return jnp.concatenate(out, axis=0)


def _lru_gates(xc, wg_ref, bg_ref, lam_ref):
    gates = jnp.dot(xc.astype(BF16), wg_ref[...], preferred_element_type=F32) + bg_ref[...]
    r = _sigmoid(gates[:, :LRU_WIDTH])
    ig = _sigmoid(gates[:, LRU_WIDTH:])
    z = -lam_ref[...]
    softplus = jnp.maximum(z, 0.0) + jnp.log1p(jnp.exp(-jnp.abs(z)))
    log_a = -LRU_C * r * softplus
    a = jnp.exp(log_a)
    th = jnp.tanh(log_a)
    mult2 = -2.0 * th / (1.0 - th)
    return a, jnp.sqrt(mult2), ig


def _conv4(xbuf, xl, cw_ref, cb_ref, rows):
    cw = cw_ref[...]
    h = CONV_HDR
    return (cb_ref[...] + cw[0:1] * xbuf[h - 3:h - 3 + rows, :] + cw[1:2] * xbuf[h - 2:h - 2 + rows, :]
            + cw[2:3] * xbuf[h - 1:h - 1 + rows, :] + cw[3:4] * xl)


def _conv4_rolled(xl, prev, cw_ref, cb_ref):
    cw = cw_ref[...]
    row8 = lax.broadcasted_iota(I32, prev.shape, 0)
    acc = cb_ref[...] + cw[CONV_WIDTH - 1:CONV_WIDTH] * xl
    for s in range(1, CONV_WIDTH):
        sh = pltpu.roll(xl, s, 0)
        top = jnp.where(row8 < s, pltpu.roll(prev, s, 0), sh[0:SUBLANES, :])
        sh = jnp.concatenate([top, sh[SUBLANES:, :]], axis=0)
        acc = acc + cw[CONV_WIDTH - 1 - s:CONV_WIDTH - s] * sh
    return acc


def _head_norm_gate(o, gate, hgn):
    return _rms(o, hgn) * _sigmoid(gate)


def _mixer_prompt_kernel(proj_ref, cw_ref, cb_ref, wg_ref, bg_ref, lam_ref, lb_ref, hgn_ref,
                         m_ref, conv_ref, h_ref, s_ref,
                         xprev, hcar, st, obuf, *, tc):
    j = pl.program_id(1)
    w = LRU_WIDTH

    @pl.when(j == 0)
    def _():
        xprev[...] = jnp.zeros_like(xprev)
        hcar[...] = jnp.zeros_like(hcar)
        st[...] = jnp.zeros_like(st)

    xl = proj_ref[:, C_XL:C_XL + w]
    xc = _conv4_rolled(xl, xprev[...], cw_ref, cb_ref)
    xprev[...] = xl[tc - SUBLANES:tc, :]
    a, mult, ig = _lru_gates(xc, wg_ref, bg_ref, lam_ref)
    row = lax.broadcasted_iota(I32, (tc, w), 0)
    mult = jnp.where((row == 0) & (j == 0), 1.0, mult)
    h, h_last = _scan_rows(a, mult * ig * xc, hcar[0:1, :])
    hcar[0:1, :] = h_last
    m_ref[:, 0:w] = (h * _gelu_tanh(proj_ref[:, C_YL:C_YL + w])).astype(BF16)

    lb = lb_ref[...]
    f = lb + (1.0 - lb) * _sigmoid(proj_ref[:, C_F:C_F + w])
    k = 1.0 - f
    b = _chunk_cumsum(jnp.log(f), HG_CHUNK)
    q = proj_ref[:, C_Q:C_Q + w]
    v = proj_ref[:, C_V:C_V + w]
    tri = (lax.broadcasted_iota(I32, (HG_CHUNK, HG_CHUNK), 0)
           >= lax.broadcasted_iota(I32, (HG_CHUNK, HG_CHUNK), 1))
    mid = HG_CHUNK // 2
    for hd in range(HG_HEADS):
        cs = slice(hd * HG_HEAD_DIM, (hd + 1) * HG_HEAD_DIM)
        s_t = st[hd]
        for c in range(tc // HG_CHUNK):
            rs = slice(c * HG_CHUNK, (c + 1) * HG_CHUNK)
            bc, qc, kc = b[rs, cs], q[rs, cs], k[rs, cs]
            vb = v[rs, cs].astype(BF16)
            b_mid = bc[mid - 1:mid, :]
            b_end = bc[HG_CHUNK - 1:HG_CHUNK, :]
            qp = (qc * jnp.exp(bc - b_mid)).astype(BF16)
            kp = (kc * jnp.exp(b_mid - bc)).astype(BF16)
            att = lax.dot_general(qp, kp, _NT, preferred_element_type=F32)
            att = jnp.where(tri, att, 0.0)
            q_in = (qc * jnp.exp(bc)).astype(BF16)
            o = (jnp.dot(att.astype(BF16), vb, preferred_element_type=F32)
                 + lax.dot_general(q_in, s_t.astype(BF16), _NT, preferred_element_type=F32))
            k_out = (kc * jnp.exp(b_end - bc)).astype(BF16)
            s_t = s_t * jnp.exp(b_end) + lax.dot_general(vb, k_out, _TN, preferred_element_type=F32)
            obuf[rs, cs] = o
        st[hd] = s_t
    hgn = hgn_ref[...]
    for hd in range(HG_HEADS):
        cs = slice(hd * HG_HEAD_DIM, (hd + 1) * HG_HEAD_DIM)
        gate = proj_ref[:, C_G + hd * HG_HEAD_DIM:C_G + (hd + 1) * HG_HEAD_DIM]
        m_ref[:, w + hd * HG_HEAD_DIM:w + (hd + 1) * HG_HEAD_DIM] = (
            _head_norm_gate(obuf[:, cs], gate, hgn).astype(BF16))

    @pl.when(j == pl.num_programs(1) - 1)
    def _():
        conv_ref[...] = xprev[SUBLANES - (CONV_WIDTH - 1):SUBLANES, :]
        h_ref[...] = hcar[0:1, :]
        for hd in range(HG_HEADS):
            s_ref[hd] = st[hd].T


def _mixer_prompt(proj, nb, seq, mw, tc):
    cw, cb, wg, bg, lam, lb, hgn = mw
    nj = seq // tc
    const = lambda shape: pl.BlockSpec(shape, lambda b, j: tuple(0 for _ in shape))
    return pl.pallas_call(
        functools.partial(_mixer_prompt_kernel, tc=tc),
        out_shape=(jax.ShapeDtypeStruct((nb * seq, D_MODEL), BF16),
                   jax.ShapeDtypeStruct((nb, CONV_WIDTH - 1, LRU_WIDTH), F32),
                   jax.ShapeDtypeStruct((nb, 1, LRU_WIDTH), F32),
                   jax.ShapeDtypeStruct((nb, HG_HEADS, HG_HEAD_DIM, HG_HEAD_DIM), F32)),
        grid=(nb, nj),
        in_specs=[pl.BlockSpec((tc, IN_COLS), lambda b, j: (b * nj + j, 0)),
                  const((CONV_WIDTH, LRU_WIDTH)), const((1, LRU_WIDTH)),
                  const((LRU_WIDTH, 2 * LRU_WIDTH)), const((1, 2 * LRU_WIDTH)),
                  const((1, LRU_WIDTH)), const((1, HG_WIDTH)), const((1, HG_HEAD_DIM))],
        out_specs=(pl.BlockSpec((tc, D_MODEL), lambda b, j: (b * nj + j, 0)),
                   pl.BlockSpec((None, CONV_WIDTH - 1, LRU_WIDTH), lambda b, j: (b, 0, 0)),
                   pl.BlockSpec((None, 1, LRU_WIDTH), lambda b, j: (b, 0, 0)),
                   pl.BlockSpec((None, HG_HEADS, HG_HEAD_DIM, HG_HEAD_DIM), lambda b, j: (b, 0, 0, 0))),
        scratch_shapes=[pltpu.VMEM((SUBLANES, LRU_WIDTH), F32),
                        pltpu.VMEM((SUBLANES, LRU_WIDTH), F32),
                        pltpu.VMEM((HG_HEADS, HG_HEAD_DIM, HG_HEAD_DIM), F32),
                        pltpu.VMEM((tc, HG_WIDTH), F32)],
        compiler_params=_cparams(("arbitrary", "arbitrary")),
    )(proj, cw, cb, wg, bg, lam, lb, hgn)


def _mixer_sample_kernel(proj_ref, cst_ref, h0_ref, s0_ref, cw_ref, cb_ref, wg_ref, bg_ref, lam_ref, lb_ref,
                         hgn_ref, m_ref, conv_ref, h_ref, s_ref, xbuf, *, steps):
    for bi in range(proj_ref.shape[0]):
        _mixer_sample_one(proj_ref.at[bi], cst_ref.at[bi], h0_ref.at[bi], s0_ref.at[bi], cw_ref, cb_ref, wg_ref,
                          bg_ref, lam_ref, lb_ref, hgn_ref, m_ref.at[bi], conv_ref.at[bi], h_ref.at[bi],
                          s_ref.at[bi], xbuf.at[bi], steps)


def _mixer_sample_one(proj_ref, cst_ref, h0_ref, s0_ref, cw_ref, cb_ref, wg_ref, bg_ref, lam_ref, lb_ref,
                      hgn_ref, m_ref, conv_ref, h_ref, s_ref, xbuf, steps):
    w = LRU_WIDTH
    rows = SUBLANES
    row = lax.broadcasted_iota(I32, (rows, w), 0)
    valid = row < steps

    xl = proj_ref[:, C_XL:C_XL + w]
    xbuf[0:CONV_HDR, :] = cst_ref[...]
    xbuf[CONV_HDR:CONV_HDR + rows, :] = xl
    xc = _conv4(xbuf, xl, cw_ref, cb_ref, rows)
    conv_ref[...] = xbuf[CONV_HDR + steps - 3:CONV_HDR + steps, :]
    a, mult, ig = _lru_gates(xc, wg_ref, bg_ref, lam_ref)
    u = mult * ig * xc
    h = h0_ref[...]
    hs = jnp.zeros((rows, w), F32)
    for t in range(steps):
        h = a[t:t + 1, :] * h + u[t:t + 1, :]
        hs = jnp.where(row == t, h, hs)
    h_ref[...] = h
    m_ref[:, 0:w] = (hs * _gelu_tanh(proj_ref[:, C_YL:C_YL + w])).astype(BF16)

    lb = lb_ref[...]
    f = lb + (1.0 - lb) * _sigmoid(proj_ref[:, C_F:C_F + w])
    k = 1.0 - f
    b = jnp.log(f)
    s = 1
    while s < rows:
        b = b + jnp.where(row >= s, pltpu.roll(b, s, 0), 0.0)
        s *= 2
    q = proj_ref[:, C_Q:C_Q + w]
    v = proj_ref[:, C_V:C_V + w]
    b_end = b[steps - 1:steps, :]
    q_in = jnp.where(valid, q * jnp.exp(b), 0.0)
    k_out = jnp.where(valid, k * jnp.exp(jnp.where(valid, b_end - b, 0.0)), 0.0)
    v_ok = jnp.where(valid, v, 0.0)
    hgn = hgn_ref[...]
    row_h = lax.broadcasted_iota(I32, (rows, HG_HEAD_DIM), 0)
    for hd in range(HG_HEADS):
        cs = slice(hd * HG_HEAD_DIM, (hd + 1) * HG_HEAD_DIM)
        s_t = s0_ref[hd].T
        o = lax.dot_general(q_in[:, cs].astype(BF16), s_t.astype(BF16), _NT, preferred_element_type=F32)
        for d in range(steps):
            ok = (row_h < steps) & (row_h >= d)
            k_d = k[:, cs] if d == 0 else pltpu.roll(k[:, cs], d, 0)
            v_d = v[:, cs] if d == 0 else pltpu.roll(v[:, cs], d, 0)
            b_d = b[:, cs] if d == 0 else pltpu.roll(b[:, cs], d, 0)
            decay = jnp.exp(jnp.where(ok, b[:, cs] - b_d, 0.0))
            wgt = jnp.sum(jnp.where(ok, q[:, cs] * k_d * decay, 0.0), axis=-1, keepdims=True)
            o = o + wgt * v_d
        s_new = s_t * jnp.exp(b_end[:, cs]) + lax.dot_general(
            v_ok[:, cs].astype(BF16), k_out[:, cs].astype(BF16), _TN, preferred_element_type=F32)
        s_ref[hd] = s_new.T
        gate = proj_ref[:, C_G + hd * HG_HEAD_DIM:C_G + (hd + 1) * HG_HEAD_DIM]
        m_ref[:, w + hd * HG_HEAD_DIM:w + (hd + 1) * HG_HEAD_DIM] = _head_norm_gate(o, gate, hgn).astype(BF16)


def _mixer_sample(proj_pad, cst_hdr, h0, s0, mw, steps):
    cw, cb, wg, bg, lam, lb, hgn = mw
    nb = proj_pad.shape[0]
    per_step = _pick_tile(nb, (4, 2, 1))
    const = lambda shape: pl.BlockSpec(shape, lambda b: tuple(0 for _ in shape))
    per_b = lambda shape: pl.BlockSpec((per_step,) + shape, lambda b: (b,) + tuple(0 for _ in shape))
    return pl.pallas_call(
        functools.partial(_mixer_sample_kernel, steps=steps),
        out_shape=(jax.ShapeDtypeStruct((nb, SUBLANES, D_MODEL), BF16),
                   jax.ShapeDtypeStruct((nb, CONV_WIDTH - 1, LRU_WIDTH), F32),
                   jax.ShapeDtypeStruct((nb, 1, LRU_WIDTH), F32),
                   jax.ShapeDtypeStruct((nb, HG_HEADS, HG_HEAD_DIM, HG_HEAD_DIM), F32)),
        grid=(nb // per_step,),
        in_specs=[per_b((SUBLANES, IN_COLS)), per_b((CONV_HDR, LRU_WIDTH)), per_b((1, LRU_WIDTH)),
                  per_b((HG_HEADS, HG_HEAD_DIM, HG_HEAD_DIM)),
                  const((CONV_WIDTH, LRU_WIDTH)), const((1, LRU_WIDTH)),
                  const((LRU_WIDTH, 2 * LRU_WIDTH)), const((1, 2 * LRU_WIDTH)),
                  const((1, LRU_WIDTH)), const((1, HG_WIDTH)), const((1, HG_HEAD_DIM))],
        out_specs=(per_b((SUBLANES, D_MODEL)), per_b((CONV_WIDTH - 1, LRU_WIDTH)), per_b((1, LRU_WIDTH)),
                   per_b((HG_HEADS, HG_HEAD_DIM, HG_HEAD_DIM))),
        scratch_shapes=[pltpu.VMEM((per_step, CONV_HDR + SUBLANES, LRU_WIDTH), F32)],
        compiler_params=_cparams(("arbitrary",)),
    )(proj_pad, cst_hdr, h0, s0, cw, cb, wg, bg, lam, lb, hgn)


def _to_row_tiles(ref, val, rows):
    for c in range(LANE_CHUNKS):
        ref[pl.ds(c, rows, stride=LANE_CHUNKS), :] = val[:, c * LANES:(c + 1) * LANES]


def _from_row_tiles(ref, base, rows):
    return jnp.concatenate(
        [ref[pl.ds(base + c, rows, stride=LANE_CHUNKS), :] for c in range(LANE_CHUNKS)], axis=-1)


def _post_kernel(mp_ref, ms_ref, xp_ref, xs_ref, gtmp_ref, scfp_ref, shfp_ref, gtms_ref, scfs_ref, shfs_ref,
                 npost_ref, npre_ref, wout_ref, rwt_ref, rb_ref,
                 x1_ref, hf_ref, idx_ref, gate_ref, rank_ref, cnt_out_ref, cnt, *, n_p):
    i = pl.program_id(0)
    tm = xp_ref.shape[0]

    @pl.when(i == 0)
    def _():
        cnt[...] = jnp.zeros_like(cnt)

    def run(m_ref, x_ref, gtm_ref, scf_ref, shf_ref):
        mix = jnp.dot(m_ref[...], wout_ref[...], preferred_element_type=F32)
        x1 = x_ref[...] + gtm_ref[...] * _rms(mix, npost_ref[...])
        x1_ref[...] = x1
        hf = _rms(x1, npre_ref[...]) * (1.0 + scf_ref[...]) + shf_ref[...]
        _to_row_tiles(hf_ref, hf, tm)
        logits = lax.dot_general(rwt_ref[...], hf, _NT, precision=lax.Precision.HIGHEST,
                                 preferred_element_type=F32) + rb_ref[...]
        e_iota = lax.broadcasted_iota(I32, logits.shape, 0)
        vals, idxs = [], []
        for _ in range(TOP_K):
            mx = jnp.max(logits, axis=0, keepdims=True)
            ix = jnp.min(jnp.where(logits == mx, e_iota, N_EXPERTS), axis=0, keepdims=True)
            vals.append(mx)
            idxs.append(ix)
            logits = jnp.where(e_iota == ix, -jnp.inf, logits)
        exps = [jnp.exp(vv - vals[0]) for vv in vals]
        den = exps[0] + exps[1] + exps[2] + exps[3]
        earlier = (lax.broadcasted_iota(I32, (tm, tm), 0) < lax.broadcasted_iota(I32, (tm, tm), 1))
        earlier = jnp.where(earlier, 1.0, 0.0).astype(BF16)
        base = cnt[...]
        for kk in range(TOP_K):
            idx_ref[kk:kk + 1, :] = idxs[kk]
            gate_ref[kk:kk + 1, :] = exps[kk] / den
            hit = e_iota == idxs[kk]
            onehot = jnp.where(hit, 1.0, 0.0)
            before = jnp.dot(onehot.astype(BF16), earlier, preferred_element_type=F32)
            rank = jnp.sum(jnp.where(hit, before + base, 0.0), axis=0, keepdims=True)
            rank_ref[kk:kk + 1, :] = rank.astype(I32)
            base = base + jnp.sum(onehot, axis=1, keepdims=True)
        cnt[...] = base

    @pl.when(i < n_p)
    def _():
        run(mp_ref, xp_ref, gtmp_ref, scfp_ref, shfp_ref)

    @pl.when(i >= n_p)
    def _():
        run(ms_ref, xs_ref, gtms_ref, scfs_ref, shfs_ref)

    @pl.when(i == pl.num_programs(0) - 1)
    def _():
        cnt_out_ref[...] = cnt[...]


def _post(m_p, m_s, x_p, x_s, mod_p, mod_s, npost, npre, wout_bf, rwt, rb, tm, tiles_per_seq):
    n_p, n_s = x_p.shape[0] // tm, x_s.shape[0] // tm
    t = (n_p + n_s) * tm
    first = lambda i: jnp.minimum(i, n_p - 1)
    second = lambda i: jnp.maximum(i - n_p, 0)
    row = lambda: pl.BlockSpec((1, D_MODEL), lambda i: (0, 0))
    tok_p = lambda: pl.BlockSpec((tm, D_MODEL), lambda i: (first(i), 0))
    tok_s = lambda: pl.BlockSpec((tm, D_MODEL), lambda i: (second(i), 0))
    seq_p = lambda: pl.BlockSpec((None, 1, D_MODEL), lambda i: (first(i) // tiles_per_seq, 0, 0))
    per_k = lambda: pl.BlockSpec((TOP_K, tm), lambda i: (0, i))
    return pl.pallas_call(
        functools.partial(_post_kernel, n_p=n_p),
        out_shape=(jax.ShapeDtypeStruct((t, D_MODEL), F32),
                   jax.ShapeDtypeStruct((t * LANE_CHUNKS, LANES), F32),
                   jax.ShapeDtypeStruct((TOP_K, t), I32), jax.ShapeDtypeStruct((TOP_K, t), F32),
                   jax.ShapeDtypeStruct((TOP_K, t), I32), jax.ShapeDtypeStruct((N_EXPERTS, 1), F32)),
        grid=(n_p + n_s,),
        in_specs=[tok_p(), tok_s(), tok_p(), tok_s(), seq_p(), seq_p(), seq_p(), tok_s(), tok_s(), tok_s(),
                  row(), row(),
                  pl.BlockSpec((D_MODEL, D_MODEL), lambda i: (0, 0)),
                  pl.BlockSpec((N_EXPERTS, D_MODEL), lambda i: (0, 0)),
                  pl.BlockSpec((N_EXPERTS, 1), lambda i: (0, 0))],
        out_specs=(pl.BlockSpec((tm, D_MODEL), lambda i: (i, 0)),
                   pl.BlockSpec((tm * LANE_CHUNKS, LANES), lambda i: (i, 0)),
                   per_k(), per_k(), per_k(), pl.BlockSpec((N_EXPERTS, 1), lambda i: (0, 0))),
        scratch_shapes=[pltpu.VMEM((N_EXPERTS, 1), F32)],
        compiler_params=_cparams(("arbitrary",)),
    )(m_p, m_s, x_p, x_s, *mod_p, *mod_s, npost, npre, wout_bf, rwt, rb)


def _invert_kernel(fill_lo_ref, fill_hi_ref, dest_ref, src_ref, *, tm):
    i = pl.program_id(0)

    @pl.when(i == 0)
    def _():
        def per_expert(e, carry):
            def per_slot(s, c):
                src_ref[s] = 0
                return c
            lax.fori_loop(fill_lo_ref[e], fill_hi_ref[e], per_slot, 0)
            return carry
        lax.fori_loop(0, N_EXPERTS, per_expert, 0)

    for n in range(TOP_K * tm):
        src_ref[dest_ref[0, n]] = (i * tm + n % tm) * TOP_K + n // tm


def _invert(fill_lo, fill_hi, dest_tiles, n_slots, tm):
    return pl.pallas_call(
        functools.partial(_invert_kernel, tm=tm),
        out_shape=jax.ShapeDtypeStruct((n_slots,), I32),
        grid_spec=pltpu.PrefetchScalarGridSpec(
            num_scalar_prefetch=2, grid=(dest_tiles.shape[0],),
            in_specs=[pl.BlockSpec((None, 1, TOP_K * tm), lambda i, lo, hi: (i, 0, 0), memory_space=pltpu.SMEM)],
            out_specs=pl.BlockSpec(memory_space=pltpu.SMEM)),
        compiler_params=_cparams(("arbitrary",)),
    )(fill_lo, fill_hi, dest_tiles)


def _expert_kernel(be_ref, nxt_ref, nu_ref, src0_ref, src1_ref, src2_ref, hf_hbm, wgu_hbm, bgu_ref, wd_hbm, bd_ref,
                   ys_ref, xbuf, wgu_f32, wd_f32, wgu_bf, wd_bf, w_sem, row_sem):
    i = pl.program_id(0)
    n_used = nu_ref[0]
    slot = lax.rem(i, GATHER_SLOTS)
    slot_len = MOE_BM * LANE_CHUNKS

    def slot_rows(s):
        return xbuf.at[pl.ds(pl.multiple_of(s * slot_len, slot_len), slot_len), :]

    def gather(src_ref, s):
        dst = slot_rows(s)
        for r in range(MOE_BM):
            token = lax.shift_right_logical(src_ref[0, r], 2)
            pltpu.make_async_copy(hf_hbm.at[token], dst.at[pl.ds(r * LANE_CHUNKS, LANE_CHUNKS), :],
                                  row_sem.at[s]).start(priority=r % 2)

    def wait_rows(s):
        pltpu.make_async_copy(slot_rows(s), slot_rows(s), row_sem.at[s]).wait()

    def weight_copies(e):
        return (pltpu.make_async_copy(wgu_hbm.at[e], wgu_f32, w_sem.at[0]),
                pltpu.make_async_copy(wd_hbm.at[e], wd_f32, w_sem.at[1]))

    @pl.when(i == 0)
    def _():
        for cp in weight_copies(be_ref[0]):
            cp.start(priority=1)
        gather(src0_ref, 0)
        gather(src1_ref, 1)

    @pl.when(i < n_used)
    def _():
        prev = be_ref[jnp.maximum(i - 1, 0)]

        @pl.when((i == 0) | (be_ref[i] != prev))
        def _():
            for cp in weight_copies(be_ref[i]):
                cp.wait()
            wgu_bf[...] = wgu_f32[...].astype(BF16)
            wd_bf[...] = wd_f32[...].astype(BF16)
            nxt = nxt_ref[be_ref[i]]

            @pl.when(nxt >= 0)
            def _():
                for cp in weight_copies(nxt):
                    cp.start(priority=1)

        wait_rows(slot)
        x = _from_row_tiles(xbuf, slot * slot_len, MOE_BM).astype(BF16)
        gather(src2_ref, lax.rem(i + 2, GATHER_SLOTS))
        hu = jnp.dot(x, wgu_bf[...], preferred_element_type=F32) + bgu_ref[...]
        glu = jnp.minimum(hu[:, :D_EXPERT], SWIGLU_LIMIT)
        lin = jnp.clip(hu[:, D_EXPERT:], -SWIGLU_LIMIT, SWIGLU_LIMIT)
        act = glu * jax.nn.sigmoid(SWIGLU_ALPHA * glu) * (lin + 1.0)
        y = jnp.dot(act.astype(BF16), wd_bf[...], preferred_element_type=F32) + bd_ref[...]
        _to_row_tiles(ys_ref, y, MOE_BM)

    @pl.when((i == n_used) | (i == n_used + 1))
    def _():
        wait_rows(slot)

    @pl.when(i >= n_used)
    def _():
        ys_ref[...] = jnp.zeros_like(ys_ref)


def _experts(block_e, next_e, n_used, src_tiles, hf_tiles, w_gu, b_gu, w_down, b_down):
    n_blocks = block_e.shape[0]
    rows = MOE_BM * LANE_CHUNKS
    smem_tile = lambda f: pl.BlockSpec((None, 1, MOE_BM), f, memory_space=pltpu.SMEM)
    return pl.pallas_call(
        _expert_kernel,
        out_shape=jax.ShapeDtypeStruct((n_blocks * rows, LANES), F32),
        grid_spec=pltpu.PrefetchScalarGridSpec(
            num_scalar_prefetch=3,
            grid=(n_blocks,),
            in_specs=[smem_tile(lambda i, be, nx, nu: (i, 0, 0)),
                      smem_tile(lambda i, be, nx, nu: (jnp.minimum(i + 1, n_blocks - 1), 0, 0)),
                      smem_tile(lambda i, be, nx, nu: (jnp.minimum(i + 2, n_blocks - 1), 0, 0)),
                      pl.BlockSpec(memory_space=pl.ANY),
                      pl.BlockSpec(memory_space=pl.ANY),
                      pl.BlockSpec((None, 1, 2 * D_EXPERT), lambda i, be, nx, nu: (be[i], 0, 0)),
                      pl.BlockSpec(memory_space=pl.ANY),
                      pl.BlockSpec((None, 1, D_MODEL), lambda i, be, nx, nu: (be[i], 0, 0))],
            out_specs=pl.BlockSpec((rows, LANES), lambda i, be, nx, nu: (i, 0)),
            scratch_shapes=[pltpu.VMEM((GATHER_SLOTS * rows, LANES), F32),
                            pltpu.VMEM((D_MODEL, 2 * D_EXPERT), F32),
                            pltpu.VMEM((D_EXPERT, D_MODEL), F32),
                            pltpu.VMEM((D_MODEL, 2 * D_EXPERT), BF16),
                            pltpu.VMEM((D_EXPERT, D_MODEL), BF16),
                            pltpu.SemaphoreType.DMA((2,)),
                            pltpu.SemaphoreType.DMA((GATHER_SLOTS,))]),
        compiler_params=_cparams(("arbitrary",)),
    )(block_e, next_e, n_used, src_tiles, src_tiles, src_tiles, hf_tiles, w_gu, b_gu, w_down, b_down)


def _combine_kernel(dcur_ref, dnext_ref, ys_hbm, g_ref, x1_ref, gt_ref, np_ref, o_ref, ybuf, row_sem, *, tm):
    i = pl.program_id(0)
    n_rows = TOP_K * tm
    slot = i % 2

    def slot_rows(s):
        return ybuf.at[pl.ds(pl.multiple_of(s * n_rows * LANE_CHUNKS, n_rows * LANE_CHUNKS), n_rows * LANE_CHUNKS), :]

    def issue(d_ref, s):
        dst = slot_rows(s)
        for n in range(n_rows):
            pltpu.make_async_copy(ys_hbm.at[d_ref[0, n]], dst.at[pl.ds(n * LANE_CHUNKS, LANE_CHUNKS), :],
                                  row_sem.at[s]).start(priority=n % 2)

    @pl.when(i == 0)
    def _():
        issue(dcur_ref, 0)

    @pl.when(i + 1 < pl.num_programs(0))
    def _():
        issue(dnext_ref, 1 - slot)

    pltpu.make_async_copy(slot_rows(slot), slot_rows(slot), row_sem.at[slot]).wait()
    g = g_ref[...]
    y = None
    for kk in range(TOP_K):
        rows = _from_row_tiles(ybuf, (slot * n_rows + kk * tm) * LANE_CHUNKS, tm)
        y = g[:, kk:kk + 1] * rows if y is None else y + g[:, kk:kk + 1] * rows
    o_ref[...] = x1_ref[...] + gt_ref[...] * _rms(y, np_ref[...])


def _combine(dest_tiles, ys_tiles, gates_t, x1, gt, npost, tm, tiles_per_seq, tile0):
    n_tiles = dest_tiles.shape[0]
    t = n_tiles * tm
    smem_tile = lambda f: pl.BlockSpec((None, 1, TOP_K * tm), f, memory_space=pltpu.SMEM)
    return pl.pallas_call(
        functools.partial(_combine_kernel, tm=tm),
        out_shape=jax.ShapeDtypeStruct((t, D_MODEL), F32),
        grid=(n_tiles,),
        in_specs=[smem_tile(lambda i: (i, 0, 0)),
                  smem_tile(lambda i: (jnp.minimum(i + 1, n_tiles - 1), 0, 0)),
                  pl.BlockSpec(memory_space=pl.ANY),
                  pl.BlockSpec((tm, TOP_K), lambda i: (i + tile0, 0)),
                  pl.BlockSpec((tm, D_MODEL), lambda i: (i + tile0, 0)),
                  _mod_spec(gt, tm, tiles_per_seq),
                  pl.BlockSpec((1, D_MODEL), lambda i: (0, 0))],
        out_specs=pl.BlockSpec((tm, D_MODEL), lambda i: (i, 0)),
        scratch_shapes=[pltpu.VMEM((2 * TOP_K * tm * LANE_CHUNKS, LANES), F32),
                        pltpu.SemaphoreType.DMA((2,))],
        compiler_params=_cparams(("arbitrary",)),
    )(dest_tiles, dest_tiles, ys_tiles, gates_t, x1, gt, npost)


def _expert_layout(counts, n_blocks):
    padded = (counts + MOE_BM - 1) // MOE_BM * MOE_BM
    pad_ends = jnp.cumsum(padded)
    pad_starts = pad_ends - padded
    block_start = jnp.arange(n_blocks, dtype=I32) * MOE_BM
    block_e = jnp.minimum(jnp.sum((pad_ends[None, :] <= block_start[:, None]).astype(I32), axis=1), N_EXPERTS - 1)
    n_used = (pad_ends[-1:] // MOE_BM).astype(I32)
    e_ids = jnp.arange(N_EXPERTS, dtype=I32)
    later = (counts > 0)[None, :] & (e_ids[None, :] > e_ids[:, None])
    next_e = jnp.min(jnp.where(later, e_ids[None, :], N_EXPERTS), axis=1)
    next_e = jnp.where(next_e < N_EXPERTS, next_e, -1).astype(I32)
    return pad_starts, pad_starts + counts, pad_ends, block_e, next_e, n_used


def _slots(idx, rank, pad_starts):
    onehot = idx[:, :, None] == jnp.arange(N_EXPERTS, dtype=I32)[None, None, :]
    return rank + jnp.sum(jnp.where(onehot, pad_starts[None, None, :], 0), axis=-1)


def _tile_major(a, tm):
    t = a.shape[1]
    return a.reshape(TOP_K, t // tm, tm).transpose(1, 0, 2).reshape(t // tm, 1, TOP_K * tm)


def _block_diag(wb):
    n, d, _ = wb.shape
    eye = jnp.eye(n, dtype=wb.dtype)
    return (wb[:, :, None, :] * eye[:, None, :, None]).reshape(n * d, n * d)


def _pick_tile(n, prefs):
    for p in prefs:
        if n % p == 0:
            return p
    raise ValueError(f"no tile for {n}")


def kernel(x_prompt, x_sample, state_rglru_conv, state_rglru_h, state_hgrn_S, c_prompt, c_sample, ada_w, ada_b, norm_pre_mix, norm_post_mix, norm_pre_ffn, norm_post_ffn, w_in, conv_w, conv_b, lru_wa, lru_ba, lru_wi, lru_bi, lru_lambda, hg_lb, hg_norm, w_out, router_w, router_b, w_gu, b_gu, w_down, b_down):
    assert ada_w.shape[0] == 1, "single-layer trunk"
    bp, lp, _ = x_prompt.shape
    bs, ls, _ = x_sample.shape
    tp, ts = bp * lp, bs * ls
    assert CONV_WIDTH - 1 <= ls <= SUBLANES

    w_in_bf = w_in[0].astype(BF16)
    w_out_bf = w_out[0].astype(BF16)
    wg = jnp.concatenate([_block_diag(lru_wa[0]), _block_diag(lru_wi[0])], axis=1).astype(BF16)
    bg = jnp.concatenate([lru_ba[0], lru_bi[0]])[None, :]
    lbs = jnp.cumsum(jax.nn.softmax(hg_lb.astype(F32), axis=0), axis=0)[0][None, :]
    mw = (conv_w[0], conv_b[0][None, :], wg, bg, lru_lambda[0][None, :], lbs, hg_norm[0][None, :])
    row = lambda p: p[0][None, :]

    n_c = bp + bs
    n_c_pad = -(-n_c // SUBLANES) * SUBLANES
    c_all = jnp.concatenate([c_prompt, c_sample, jnp.zeros((n_c_pad - n_c, D_MODEL), F32)], axis=0)
    mod = _ada(c_all, ada_w[0], ada_b[0][None, :])
    mod_p = [m[:, None, :] for m in jnp.split(mod[:bp], 6, axis=-1)]
    mod_s = [jnp.repeat(m, ls, axis=0) for m in jnp.split(mod[bp:n_c], 6, axis=-1)]

    xp = x_prompt.reshape(tp, D_MODEL)
    xs = x_sample.reshape(ts, D_MODEL)
    tm_p = _pick_tile(lp, (256, 128, 64))
    tm_s = _pick_tile(ts, (256, 128, 64, 32, 16, 8))

    proj_p = _inproj(xp, mod_p[1], mod_p[0], row(norm_pre_mix), w_in_bf, tm_p, lp // tm_p)
    proj_s = _inproj(xs, mod_s[1], mod_s[0], row(norm_pre_mix), w_in_bf, tm_s, 1)
    tc = _pick_tile(lp, (256, 128, 64))
    m_p, conv_p, h_p, s_p = _mixer_prompt(proj_p, bp, lp, mw, tc)

    proj_s_pad = jnp.pad(proj_s.reshape(bs, ls, IN_COLS), ((0, 0), (0, SUBLANES - ls), (0, 0)))
    cst_hdr = jnp.pad(state_rglru_conv[0], ((0, 0), (CONV_HDR - (CONV_WIDTH - 1), 0), (0, 0)))
    m_s_pad, conv_s, h_s, s_s = _mixer_sample(proj_s_pad, cst_hdr, state_rglru_h[0][:, None, :],
                                              state_hgrn_S[0], mw, ls)
    m_s = m_s_pad[:, :ls, :].reshape(ts, D_MODEL)

    tmq = _pick_tile(math.gcd(lp, ts), (512, 256, 128, 64, 32, 16, 8))
    x1, hf, idx, gate, rank, counts = _post(
        m_p, m_s, xp, xs, (mod_p[2], mod_p[4], mod_p[3]), (mod_s[2], mod_s[4], mod_s[3]),
        row(norm_post_mix), row(norm_pre_ffn), w_out_bf, router_w[0].T, router_b[0][:, None], tmq, lp // tmq)

    n_blocks = -(-(TOP_K * (tp + ts)) // MOE_BM) + N_EXPERTS + GATHER_SLOTS - 1
    n_slots = n_blocks * MOE_BM
    pad_starts, fill_lo, pad_ends, block_e, next_e, n_used = _expert_layout(counts[:, 0].astype(I32), n_blocks)
    fill_hi = pad_ends.at[N_EXPERTS - 1].set(n_slots)
    dest = _slots(idx, rank, pad_starts)
    tmd = _pick_tile(math.gcd(lp, ts), (128, 64, 32, 16, 8))
    dtiles = _tile_major(dest, tmd)
    src = _invert(fill_lo, fill_hi, dtiles, n_slots, tmd)
    hf_tiles = hf.reshape(-1, LANE_CHUNKS, LANES)
    ys = _experts(block_e, next_e, n_used, src.reshape(n_blocks, 1, MOE_BM), hf_tiles,
                  w_gu[0], b_gu[0][:, None, :], w_down[0], b_down[0][:, None, :])
    ys_tiles = ys.reshape(-1, LANE_CHUNKS, LANES)

    gates_t = gate.T
    y_p = _combine(dtiles[:tp // tmd], ys_tiles, gates_t, x1, mod_p[5], row(norm_post_ffn), tmd, lp // tmd, 0)
    y_s = _combine(dtiles[tp // tmd:], ys_tiles, gates_t, x1, mod_s[5], row(norm_post_ffn), tmd, 1, tp // tmd)

    return (y_p.reshape(bp, lp, D_MODEL), y_s.reshape(bs, ls, D_MODEL),
            conv_p[None], h_p.reshape(1, bp, LRU_WIDTH), s_p[None],
            conv_s[None], h_s.reshape(1, bs, LRU_WIDTH), s_s[None])
```

```python
import functools
import math

import jax
import jax.numpy as jnp
from jax import lax
from jax.experimental import pallas as pl
from jax.experimental.pallas import tpu as pltpu

F32 = jnp.float32
BF16 = jnp.bfloat16
I32 = jnp.int32

D_MODEL = 1024
LRU_WIDTH = 512
LRU_BLOCKS = 8
LRU_BLOCK = LRU_WIDTH // LRU_BLOCKS
CONV_WIDTH = 4
LRU_C = 8.0
HG_WIDTH = 512
HG_HEAD_DIM = 128
HG_HEADS = 4
IN_COLS = 2 * LRU_WIDTH + 4 * HG_WIDTH
N_EXPERTS = 32
TOP_K = 4
D_EXPERT = 1024
SWIGLU_LIMIT = 7.0
SWIGLU_ALPHA = 1.702
RMS_EPS = 1e-6

C_XL, C_YL, C_Q, C_F, C_V, C_G = 0, 512, 1024, 1536, 2048, 2560

SUBLANES = 8
LANES = 128
LANE_CHUNKS = D_MODEL // LANES
CONV_HDR = SUBLANES
HG_CHUNK = 64
MOE_BM = 256
ZERO_ROWS = 64
VMEM_LIMIT = 56 * 1024 * 1024

_NT = (((1,), (1,)), ((), ()))
_TN = (((0,), (0,)), ((), ()))


def _cparams(sem):
    return pltpu.CompilerParams(dimension_semantics=sem, vmem_limit_bytes=VMEM_LIMIT)


def _rms(x, g):
    return x * lax.rsqrt(jnp.mean(x * x, axis=-1, keepdims=True) + RMS_EPS) * g


def _gelu_tanh(x):
    c = math.sqrt(2.0 / math.pi)
    return 0.5 * x * (1.0 + jnp.tanh(c * (x + 0.044715 * (x * x * x))))


def _ada_kernel(c_ref, w_ref, b_ref, o_ref):
    c = c_ref[...]
    s = c * jax.nn.sigmoid(c)
    o_ref[...] = jnp.dot(s.astype(BF16), w_ref[...].astype(BF16), preferred_element_type=F32) + b_ref[...]


def _ada(c_all, ada_w, ada_b):
    n = c_all.shape[0]
    tn = 1024
    return pl.pallas_call(
        _ada_kernel,
        out_shape=jax.ShapeDtypeStruct((n, 6 * D_MODEL), F32),
        grid=(6 * D_MODEL // tn,),
        in_specs=[pl.BlockSpec((n, D_MODEL), lambda j: (0, 0)),
                  pl.BlockSpec((D_MODEL, tn), lambda j: (0, j)),
                  pl.BlockSpec((1, tn), lambda j: (0, j))],
        out_specs=pl.BlockSpec((n, tn), lambda j: (0, j)),
        compiler_params=_cparams(("arbitrary",)),
    )(c_all, ada_w, ada_b)


def _inproj_kernel(x_ref, sc_ref, sh_ref, g_ref, w_ref, o_ref):
    h = _rms(x_ref[...], g_ref[...]) * (1.0 + sc_ref[...]) + sh_ref[...]
    o_ref[...] = jnp.dot(h.astype(BF16), w_ref[...], preferred_element_type=F32)


def _mod_spec(mod, tm, tiles_per_seq):
    if mod.ndim == 3:
        return pl.BlockSpec((None, 1, D_MODEL), lambda i: (i // tiles_per_seq, 0, 0))
    return pl.BlockSpec((tm, D_MODEL), lambda i: (i, 0))


def _inproj(x, sc, sh, g, w_bf, tm, tiles_per_seq):
    t = x.shape[0]
    return pl.pallas_call(
        _inproj_kernel,
        out_shape=jax.ShapeDtypeStruct((t, IN_COLS), F32),
        grid=(t // tm,),
        in_specs=[pl.BlockSpec((tm, D_MODEL), lambda i: (i, 0)),
                  _mod_spec(sc, tm, tiles_per_seq), _mod_spec(sh, tm, tiles_per_seq),
                  pl.BlockSpec((1, D_MODEL), lambda i: (0, 0)),
                  pl.BlockSpec((D_MODEL, IN_COLS), lambda i: (0, 0))],
        out_specs=pl.BlockSpec((tm, IN_COLS), lambda i: (i, 0)),
        compiler_params=_cparams(("arbitrary",)),
    )(x, sc, sh, g, w_bf)


def _sigmoid(x):
    return 0.5 * jnp.tanh(0.5 * x) + 0.5


def _group_rows(x):
    rows, w = x.shape
    return x.reshape(rows // SUBLANES, SUBLANES, w)


def _scan_rows(a, u, h0):
    a3, u3 = _group_rows(a), _group_rows(u)
    r3 = lax.broadcasted_iota(I32, a3.shape, 1)
    s = 1
    while s < SUBLANES:
        keep = r3 >= s
        a_sh = jnp.where(keep, pltpu.roll(a3, s, 1), 1.0)
        u_sh = jnp.where(keep, pltpu.roll(u3, s, 1), 0.0)
        u3 = a3 * u_sh + u3
        a3 = a3 * a_sh
        s *= 2
    out, h = [], h0
    for i in range(a3.shape[0]):
        hg = u3[i] + a3[i] * h
        h = hg[SUBLANES - 1:SUBLANES, :]
        out.append(hg)
    return jnp.concatenate(out, axis=0), h


def _chunk_cumsum(x, chunk):
    x3 = _group_rows(x)
    r3 = lax.broadcasted_iota(I32, x3.shape, 1)
    s = 1
    while s < SUBLANES:
        x3 = x3 + jnp.where(r3 >= s, pltpu.roll(x3, s, 1), 0.0)
        s *= 2
    out, carry = [], None
    for i in range(x3.shape[0]):
        cur = x3[i] if i % (chunk // SUBLANES) == 0 else x3[i] + carry
        carry = cur[SUBLANES - 1:SUBLANES, :]
        out.append(cur)
    return jnp.concatenate(out, axis=0)


def _lru_gates(xc, wg_ref, bg_ref, lam_ref):
    gates = jnp.dot(xc.astype(BF16), wg_ref[...], preferred_element_type=F32) + bg_ref[...]
    r = _sigmoid(gates[:, :LRU_WIDTH])
    ig = _sigmoid(gates[:, LRU_WIDTH:])
    z = -lam_ref[...]
    softplus = jnp.maximum(z, 0.0) + jnp.log1p(jnp.exp(-jnp.abs(z)))
    log_a = -LRU_C * r * softplus
    a = jnp.exp(log_a)
    th = jnp.tanh(log_a)
    mult2 = -2.0 * th / (1.0 - th)
    return a, jnp.sqrt(mult2), ig


def _conv4(xbuf, xl, cw_ref, cb_ref, rows):
    cw = cw_ref[...]
    h = CONV_HDR
    return (cb_ref[...] + cw[0:1] * xbuf[h - 3:h - 3 + rows, :] + cw[1:2] * xbuf[h - 2:h - 2 + rows, :]
            + cw[2:3] * xbuf[h - 1:h - 1 + rows, :] + cw[3:4] * xl)


def _conv4_rolled(xl, prev, cw_ref, cb_ref):
    cw = cw_ref[...]
    row8 = lax.broadcasted_iota(I32, prev.shape, 0)
    acc = cb_ref[...] + cw[CONV_WIDTH - 1:CONV_WIDTH] * xl
    for s in range(1, CONV_WIDTH):
        sh = pltpu.roll(xl, s, 0)
        top = jnp.where(row8 < s, pltpu.roll(prev, s, 0), sh[0:SUBLANES, :])
        sh = jnp.concatenate([top, sh[SUBLANES:, :]], axis=0)
        acc = acc + cw[CONV_WIDTH - 1 - s:CONV_WIDTH - s] * sh
    return acc


def _head_norm_gate(o, gate, hgn):
    return _rms(o, hgn) * _sigmoid(gate)


def _mixer_prompt_kernel(x_ref, sc_ref, sh_ref, g_ref, win_ref, cw_ref, cb_ref, wg_ref, bg_ref, lam_ref, lb_ref,
                         hgn_ref, m_ref, conv_ref, h_ref, s_ref,
                         proj_ref, xprev, hcar, st, obuf, *, tc):
    j = pl.program_id(1)
    w = LRU_WIDTH

    @pl.when(j == 0)
    def _():
        xprev[...] = jnp.zeros_like(xprev)
        hcar[...] = jnp.zeros_like(hcar)
        st[...] = jnp.zeros_like(st)

    hn = _rms(x_ref[...], g_ref[...]) * (1.0 + sc_ref[...]) + sh_ref[...]
    proj_ref[...] = jnp.dot(hn.astype(BF16), win_ref[...], preferred_element_type=F32)

    xl = proj_ref[:, C_XL:C_XL + w]
    xc = _conv4_rolled(xl, xprev[...], cw_ref, cb_ref)
    xprev[...] = xl[tc - SUBLANES:tc, :]
    a, mult, ig = _lru_gates(xc, wg_ref, bg_ref, lam_ref)
    row = lax.broadcasted_iota(I32, (tc, w), 0)
    mult = jnp.where((row == 0) & (j == 0), 1.0, mult)
    h, h_last = _scan_rows(a, mult * ig * xc, hcar[0:1, :])
    hcar[0:1, :] = h_last
    m_ref[:, 0:w] = (h * _gelu_tanh(proj_ref[:, C_YL:C_YL + w])).astype(BF16)

    lb = lb_ref[...]
    f = lb + (1.0 - lb) * _sigmoid(proj_ref[:, C_F:C_F + w])
    k = 1.0 - f
    b = _chunk_cumsum(jnp.log(f), HG_CHUNK)
    q = proj_ref[:, C_Q:C_Q + w]
    v = proj_ref[:, C_V:C_V + w]
    tri = (lax.broadcasted_iota(I32, (HG_CHUNK, HG_CHUNK), 0)
           >= lax.broadcasted_iota(I32, (HG_CHUNK, HG_CHUNK), 1))
    mid = HG_CHUNK // 2
    for hd in range(HG_HEADS):
        cs = slice(hd * HG_HEAD_DIM, (hd + 1) * HG_HEAD_DIM)
        s_t = st[hd]
        for c in range(tc // HG_CHUNK):
            rs = slice(c * HG_CHUNK, (c + 1) * HG_CHUNK)
            bc, qc, kc = b[rs, cs], q[rs, cs], k[rs, cs]
            vb = v[rs, cs].astype(BF16)
            b_mid = bc[mid - 1:mid, :]
            b_end = bc[HG_CHUNK - 1:HG_CHUNK, :]
            qp = (qc * jnp.exp(bc - b_mid)).astype(BF16)
            kp = (kc * jnp.exp(b_mid - bc)).astype(BF16)
            att = lax.dot_general(qp, kp, _NT, preferred_element_type=F32)
            att = jnp.where(tri, att, 0.0)
            q_in = (qc * jnp.exp(bc)).astype(BF16)
            o = (jnp.dot(att.astype(BF16), vb, preferred_element_type=F32)
                 + lax.dot_general(q_in, s_t.astype(BF16), _NT, preferred_element_type=F32))
            k_out = (kc * jnp.exp(b_end - bc)).astype(BF16)
            s_t = s_t * jnp.exp(b_end) + lax.dot_general(vb, k_out, _TN, preferred_element_type=F32)
            obuf[rs, cs] = o
        st[hd] = s_t
    hgn = hgn_ref[...]
    for hd in range(HG_HEADS):
        cs = slice(hd * HG_HEAD_DIM, (hd + 1) * HG_HEAD_DIM)
        gate = proj_ref[:, C_G + hd * HG_HEAD_DIM:C_G + (hd + 1) * HG_HEAD_DIM]
        m_ref[:, w + hd * HG_HEAD_DIM:w + (hd + 1) * HG_HEAD_DIM] = (
            _head_norm_gate(obuf[:, cs], gate, hgn).astype(BF16))

    @pl.when(j == pl.num_programs(1) - 1)
    def _():
        conv_ref[...] = xprev[SUBLANES - (CONV_WIDTH - 1):SUBLANES, :]
        h_ref[...] = hcar[0:1, :]
        for hd in range(HG_HEADS):
            s_ref[hd] = st[hd].T


def _mixer_prompt(x, sc, sh, g, w_in_bf, nb, seq, mw, tc):
    cw, cb, wg, bg, lam, lb, hgn = mw
    nj = seq // tc
    const = lambda shape: pl.BlockSpec(shape, lambda b, j: tuple(0 for _ in shape))
    per_seq = lambda: pl.BlockSpec((None, 1, D_MODEL), lambda b, j: (b, 0, 0))
    return pl.pallas_call(
        functools.partial(_mixer_prompt_kernel, tc=tc),
        out_shape=(jax.ShapeDtypeStruct((nb * seq, D_MODEL), BF16),
                   jax.ShapeDtypeStruct((nb, CONV_WIDTH - 1, LRU_WIDTH), F32),
                   jax.ShapeDtypeStruct((nb, 1, LRU_WIDTH), F32),
                   jax.ShapeDtypeStruct((nb, HG_HEADS, HG_HEAD_DIM, HG_HEAD_DIM), F32)),
        grid=(nb, nj),
        in_specs=[pl.BlockSpec((tc, D_MODEL), lambda b, j: (b * nj + j, 0)),
                  per_seq(), per_seq(), const((1, D_MODEL)), const((D_MODEL, IN_COLS)),
                  const((CONV_WIDTH, LRU_WIDTH)), const((1, LRU_WIDTH)),
                  const((LRU_WIDTH, 2 * LRU_WIDTH)), const((1, 2 * LRU_WIDTH)),
                  const((1, LRU_WIDTH)), const((1, HG_WIDTH)), const((1, HG_HEAD_DIM))],
        out_specs=(pl.BlockSpec((tc, D_MODEL), lambda b, j: (b * nj + j, 0)),
                   pl.BlockSpec((None, CONV_WIDTH - 1, LRU_WIDTH), lambda b, j: (b, 0, 0)),
                   pl.BlockSpec((None, 1, LRU_WIDTH), lambda b, j: (b, 0, 0)),
                   pl.BlockSpec((None, HG_HEADS, HG_HEAD_DIM, HG_HEAD_DIM), lambda b, j: (b, 0, 0, 0))),
        scratch_shapes=[pltpu.VMEM((tc, IN_COLS), F32),
                        pltpu.VMEM((SUBLANES, LRU_WIDTH), F32),
                        pltpu.VMEM((SUBLANES, LRU_WIDTH), F32),
                        pltpu.VMEM((HG_HEADS, HG_HEAD_DIM, HG_HEAD_DIM), F32),
                        pltpu.VMEM((tc, HG_WIDTH), F32)],
        compiler_params=_cparams(("arbitrary", "arbitrary")),
    )(x, sc, sh, g, w_in_bf, cw, cb, wg, bg, lam, lb, hgn)


def _mixer_sample_kernel(proj_ref, cst_ref, h0_ref, s0_ref, cw_ref, cb_ref, wg_ref, bg_ref, lam_ref, lb_ref,
                         hgn_ref, m_ref, conv_ref, h_ref, s_ref, xbuf, *, steps):
    for bi in range(proj_ref.shape[0]):
        _mixer_sample_one(proj_ref.at[bi], cst_ref.at[bi], h0_ref.at[bi], s0_ref.at[bi], cw_ref, cb_ref, wg_ref,
                          bg_ref, lam_ref, lb_ref, hgn_ref, m_ref.at[bi], conv_ref.at[bi], h_ref.at[bi],
                          s_ref.at[bi], xbuf.at[bi], steps)


def _mixer_sample_one(proj_ref, cst_ref, h0_ref, s0_ref, cw_ref, cb_ref, wg_ref, bg_ref, lam_ref, lb_ref,
                      hgn_ref, m_ref, conv_ref, h_ref, s_ref, xbuf, steps):
    w = LRU_WIDTH
    rows = SUBLANES
    row = lax.broadcasted_iota(I32, (rows, w), 0)
    valid = row < steps

    xl = proj_ref[:, C_XL:C_XL + w]
    xbuf[0:CONV_HDR, :] = cst_ref[...]
    xbuf[CONV_HDR:CONV_HDR + rows, :] = xl
    xc = _conv4(xbuf, xl, cw_ref, cb_ref, rows)
    conv_ref[...] = xbuf[CONV_HDR + steps - 3:CONV_HDR + steps, :]
    a, mult, ig = _lru_gates(xc, wg_ref, bg_ref, lam_ref)
    u = mult * ig * xc
    h = h0_ref[...]
    hs = jnp.zeros((rows, w), F32)
    for t in range(steps):
        h = a[t:t + 1, :] * h + u[t:t + 1, :]
        hs = jnp.where(row == t, h, hs)
    h_ref[...] = h
    m_ref[:, 0:w] = (hs * _gelu_tanh(proj_ref[:, C_YL:C_YL + w])).astype(BF16)

    lb = lb_ref[...]
    f = lb + (1.0 - lb) * _sigmoid(proj_ref[:, C_F:C_F + w])
    k = 1.0 - f
    b = jnp.log(f)
    s = 1
    while s < rows:
        b = b + jnp.where(row >= s, pltpu.roll(b, s, 0), 0.0)
        s *= 2
    q = proj_ref[:, C_Q:C_Q + w]
    v = proj_ref[:, C_V:C_V + w]
    b_end = b[steps - 1:steps, :]
    q_in = jnp.where(valid, q * jnp.exp(b), 0.0)
    k_out = jnp.where(valid, k * jnp.exp(jnp.where(valid, b_end - b, 0.0)), 0.0)
    v_ok = jnp.where(valid, v, 0.0)
    hgn = hgn_ref[...]
    row_h = lax.broadcasted_iota(I32, (rows, HG_HEAD_DIM), 0)
    for hd in range(HG_HEADS):
        cs = slice(hd * HG_HEAD_DIM, (hd + 1) * HG_HEAD_DIM)
        s_t = s0_ref[hd].T
        o = lax.dot_general(q_in[:, cs].astype(BF16), s_t.astype(BF16), _NT, preferred_element_type=F32)
        for d in range(steps):
            ok = (row_h < steps) & (row_h >= d)
            k_d = k[:, cs] if d == 0 else pltpu.roll(k[:, cs], d, 0)
            v_d = v[:, cs] if d == 0 else pltpu.roll(v[:, cs], d, 0)
            b_d = b[:, cs] if d == 0 else pltpu.roll(b[:, cs], d, 0)
            decay = jnp.exp(jnp.where(ok, b[:, cs] - b_d, 0.0))
            wgt = jnp.sum(jnp.where(ok, q[:, cs] * k_d * decay, 0.0), axis=-1, keepdims=True)
            o = o + wgt * v_d
        s_new = s_t * jnp.exp(b_end[:, cs]) + lax.dot_general(
            v_ok[:, cs].astype(BF16), k_out[:, cs].astype(BF16), _TN, preferred_element_type=F32)
        s_ref[hd] = s_new.T
        gate = proj_ref[:, C_G + hd * HG_HEAD_DIM:C_G + (hd + 1) * HG_HEAD_DIM]
        m_ref[:, w + hd * HG_HEAD_DIM:w + (hd + 1) * HG_HEAD_DIM] = _head_norm_gate(o, gate, hgn).astype(BF16)


def _mixer_sample(proj_pad, cst_hdr, h0, s0, mw, steps):
    cw, cb, wg, bg, lam, lb, hgn = mw
    nb = proj_pad.shape[0]
    per_step = _pick_tile(nb, (4, 2, 1))
    const = lambda shape: pl.BlockSpec(shape, lambda b: tuple(0 for _ in shape))
    per_b = lambda shape: pl.BlockSpec((per_step,) + shape, lambda b: (b,) + tuple(0 for _ in shape))
    return pl.pallas_call(
        functools.partial(_mixer_sample_kernel, steps=steps),
        out_shape=(jax.ShapeDtypeStruct((nb, SUBLANES, D_MODEL), BF16),
                   jax.ShapeDtypeStruct((nb, CONV_WIDTH - 1, LRU_WIDTH), F32),
                   jax.ShapeDtypeStruct((nb, 1, LRU_WIDTH), F32),
                   jax.ShapeDtypeStruct((nb, HG_HEADS, HG_HEAD_DIM, HG_HEAD_DIM), F32)),
        grid=(nb // per_step,),
        in_specs=[per_b((SUBLANES, IN_COLS)), per_b((CONV_HDR, LRU_WIDTH)), per_b((1, LRU_WIDTH)),
                  per_b((HG_HEADS, HG_HEAD_DIM, HG_HEAD_DIM)),
                  const((CONV_WIDTH, LRU_WIDTH)), const((1, LRU_WIDTH)),
                  const((LRU_WIDTH, 2 * LRU_WIDTH)), const((1, 2 * LRU_WIDTH)),
                  const((1, LRU_WIDTH)), const((1, HG_WIDTH)), const((1, HG_HEAD_DIM))],
        out_specs=(per_b((SUBLANES, D_MODEL)), per_b((CONV_WIDTH - 1, LRU_WIDTH)), per_b((1, LRU_WIDTH)),
                   per_b((HG_HEADS, HG_HEAD_DIM, HG_HEAD_DIM))),
        scratch_shapes=[pltpu.VMEM((per_step, CONV_HDR + SUBLANES, LRU_WIDTH), F32)],
        compiler_params=_cparams(("arbitrary",)),
    )(proj_pad, cst_hdr, h0, s0, cw, cb, wg, bg, lam, lb, hgn)


def _to_row_tiles(ref, val, rows):
    for c in range(LANE_CHUNKS):
        ref[pl.ds(c, rows, stride=LANE_CHUNKS), :] = val[:, c * LANES:(c + 1) * LANES]


def _from_row_tiles(ref, base, rows):
    return jnp.concatenate(
        [ref[pl.ds(base + c, rows, stride=LANE_CHUNKS), :] for c in range(LANE_CHUNKS)], axis=-1)


def _post_kernel(mp_ref, ms_ref, xp_ref, xs_ref, gtmp_ref, scfp_ref, shfp_ref, gtms_ref, scfs_ref, shfs_ref,
                 npost_ref, npre_ref, wout_ref, rwt_ref, rb_ref,
                 x1_ref, hf_ref, idx_ref, gate_ref, rank_ref, cnt_out_ref, cnt, *, n_p):
    i = pl.program_id(0)
    tm = xp_ref.shape[0]

    @pl.when(i == 0)
    def _():
        cnt[...] = jnp.zeros_like(cnt)

    def run(m_ref, x_ref, gtm_ref, scf_ref, shf_ref):
        mix = jnp.dot(m_ref[...], wout_ref[...], preferred_element_type=F32)
        x1 = x_ref[...] + gtm_ref[...] * _rms(mix, npost_ref[...])
        x1_ref[...] = x1
        hf = _rms(x1, npre_ref[...]) * (1.0 + scf_ref[...]) + shf_ref[...]
        _to_row_tiles(hf_ref, hf, tm)
        logits = lax.dot_general(rwt_ref[...], hf, _NT, precision=lax.Precision.HIGHEST,
                                 preferred_element_type=F32) + rb_ref[...]
        e_iota = lax.broadcasted_iota(I32, logits.shape, 0)
        vals, idxs = [], []
        for _ in range(TOP_K):
            mx = jnp.max(logits, axis=0, keepdims=True)
            ix = jnp.min(jnp.where(logits == mx, e_iota, N_EXPERTS), axis=0, keepdims=True)
            vals.append(mx)
            idxs.append(ix)
            logits = jnp.where(e_iota == ix, -jnp.inf, logits)
        exps = [jnp.exp(vv - vals[0]) for vv in vals]
        den = exps[0] + exps[1] + exps[2] + exps[3]
        earlier = (lax.broadcasted_iota(I32, (tm, tm), 0) < lax.broadcasted_iota(I32, (tm, tm), 1))
        earlier = jnp.where(earlier, 1.0, 0.0).astype(BF16)
        base = cnt[...]
        for kk in range(TOP_K):
            idx_ref[kk:kk + 1, :] = idxs[kk]
            gate_ref[kk:kk + 1, :] = exps[kk] / den
            hit = e_iota == idxs[kk]
            onehot = jnp.where(hit, 1.0, 0.0)
            before = jnp.dot(onehot.astype(BF16), earlier, preferred_element_type=F32)
            rank = jnp.sum(jnp.where(hit, before + base, 0.0), axis=0, keepdims=True)
            rank_ref[kk:kk + 1, :] = rank.astype(I32)
            base = base + jnp.sum(onehot, axis=1, keepdims=True)
        cnt[...] = base

    @pl.when(i < n_p)
    def _():
        run(mp_ref, xp_ref, gtmp_ref, scfp_ref, shfp_ref)

    @pl.when(i >= n_p)
    def _():
        run(ms_ref, xs_ref, gtms_ref, scfs_ref, shfs_ref)

    @pl.when(i == pl.num_programs(0) - 1)
    def _():
        cnt_out_ref[...] = cnt[...]


def _post(m_p, m_s, x_p, x_s, mod_p, mod_s, npost, npre, wout_bf, rwt, rb, tm, tiles_per_seq):
    n_p, n_s = x_p.shape[0] // tm, x_s.shape[0] // tm
    t = (n_p + n_s) * tm
    first = lambda i: jnp.minimum(i, n_p - 1)
    second = lambda i: jnp.maximum(i - n_p, 0)
    row = lambda: pl.BlockSpec((1, D_MODEL), lambda i: (0, 0))
    tok_p = lambda: pl.BlockSpec((tm, D_MODEL), lambda i: (first(i), 0))
    tok_s = lambda: pl.BlockSpec((tm, D_MODEL), lambda i: (second(i), 0))
    seq_p = lambda: pl.BlockSpec((None, 1, D_MODEL), lambda i: (first(i) // tiles_per_seq, 0, 0))
    per_k = lambda: pl.BlockSpec((TOP_K, tm), lambda i: (0, i))
    return pl.pallas_call(
        functools.partial(_post_kernel, n_p=n_p),
        out_shape=(jax.ShapeDtypeStruct((t, D_MODEL), F32),
                   jax.ShapeDtypeStruct((t * LANE_CHUNKS, LANES), F32),
                   jax.ShapeDtypeStruct((TOP_K, t), I32), jax.ShapeDtypeStruct((TOP_K, t), F32),
                   jax.ShapeDtypeStruct((TOP_K, t), I32), jax.ShapeDtypeStruct((N_EXPERTS, 1), F32)),
        grid=(n_p + n_s,),
        in_specs=[tok_p(), tok_s(), tok_p(), tok_s(), seq_p(), seq_p(), seq_p(), tok_s(), tok_s(), tok_s(),
                  row(), row(),
                  pl.BlockSpec((D_MODEL, D_MODEL), lambda i: (0, 0)),
                  pl.BlockSpec((N_EXPERTS, D_MODEL), lambda i: (0, 0)),
                  pl.BlockSpec((N_EXPERTS, 1), lambda i: (0, 0))],
        out_specs=(pl.BlockSpec((tm, D_MODEL), lambda i: (i, 0)),
                   pl.BlockSpec((tm * LANE_CHUNKS, LANES), lambda i: (i, 0)),
                   per_k(), per_k(), per_k(), pl.BlockSpec((N_EXPERTS, 1), lambda i: (0, 0))),
        scratch_shapes=[pltpu.VMEM((N_EXPERTS, 1), F32)],
        compiler_params=_cparams(("arbitrary",)),
    )(m_p, m_s, x_p, x_s, *mod_p, *mod_s, npost, npre, wout_bf, rwt, rb)


def _dispatch_kernel(fill_lo_ref, fill_hi_ref, dest_ref, hf_ref, xs_hbm, zeros, row_sem, pad_sem, *, tm, n_slots):
    i = pl.program_id(0)
    n_rows = TOP_K * tm

    @pl.when(i == 0)
    def _():
        zeros[...] = jnp.zeros_like(zeros)

        def per_expert(e, total):
            lo, hi = fill_lo_ref[e], fill_hi_ref[e]

            def per_slot(s, carry):
                pltpu.make_async_copy(zeros.at[0], xs_hbm.at[s], pad_sem).start()
                return carry
            lax.fori_loop(lo, hi, per_slot, 0)
            return total + (hi - lo)
        n_fill = lax.fori_loop(0, N_EXPERTS, per_expert, 0)
        tail_lo = fill_hi_ref[N_EXPERTS - 1] // ZERO_ROWS
        n_tail = n_slots // ZERO_ROWS - tail_lo

        def per_tail(c, carry):
            pltpu.make_async_copy(zeros, xs_hbm.at[pl.ds((tail_lo + c) * ZERO_ROWS, ZERO_ROWS)], pad_sem).start()
            return carry
        lax.fori_loop(0, n_tail, per_tail, 0)

        def drain_slot(s, carry):
            pltpu.make_async_copy(zeros.at[0], xs_hbm.at[0], pad_sem).wait()
            return carry
        lax.fori_loop(0, n_fill, drain_slot, 0)

        def drain_tail(c, carry):
            pltpu.make_async_copy(zeros, xs_hbm.at[pl.ds(0, ZERO_ROWS)], pad_sem).wait()
            return carry
        lax.fori_loop(0, n_tail, drain_tail, 0)

    for n in range(n_rows):
        pltpu.make_async_copy(hf_ref.at[pl.ds((n % tm) * LANE_CHUNKS, LANE_CHUNKS), :],
                              xs_hbm.at[dest_ref[0, n]], row_sem).start(priority=n % 2)
    pltpu.make_async_copy(xs_hbm.at[pl.ds(0, n_rows)], xs_hbm.at[pl.ds(0, n_rows)], row_sem).wait()


def _dispatch(fill_lo, fill_hi, dest_tiles, hf_rows, n_slots, tm):
    return pl.pallas_call(
        functools.partial(_dispatch_kernel, tm=tm, n_slots=n_slots),
        out_shape=jax.ShapeDtypeStruct((n_slots, LANE_CHUNKS, LANES), F32),
        grid_spec=pltpu.PrefetchScalarGridSpec(
            num_scalar_prefetch=2, grid=(dest_tiles.shape[0],),
            in_specs=[pl.BlockSpec((None, 1, TOP_K * tm), lambda i, lo, hi: (i, 0, 0), memory_space=pltpu.SMEM),
                      pl.BlockSpec((tm * LANE_CHUNKS, LANES), lambda i, lo, hi: (i, 0))],
            out_specs=pl.BlockSpec(memory_space=pl.ANY),
            scratch_shapes=[pltpu.VMEM((ZERO_ROWS, LANE_CHUNKS, LANES), F32),
                            pltpu.SemaphoreType.DMA, pltpu.SemaphoreType.DMA]),
        compiler_params=_cparams(("arbitrary",)),
    )(fill_lo, fill_hi, dest_tiles, hf_rows)


def _expert_kernel(be_ref, nxt_ref, nu_ref, xs_ref, wgu_hbm, bgu_ref, wd_hbm, bd_ref, ys_ref,
                   wgu_f32, wd_f32, wgu_bf, wd_bf, w_sem):
    i = pl.program_id(0)
    n_used = nu_ref[0]

    def weight_copies(e):
        return (pltpu.make_async_copy(wgu_hbm.at[e], wgu_f32, w_sem.at[0]),
                pltpu.make_async_copy(wd_hbm.at[e], wd_f32, w_sem.at[1]))

    @pl.when(i == 0)
    def _():
        for cp in weight_copies(be_ref[0]):
            cp.start()

    @pl.when(i < n_used)
    def _():
        prev = be_ref[jnp.maximum(i - 1, 0)]

        @pl.when((i == 0) | (be_ref[i] != prev))
        def _():
            for cp in weight_copies(be_ref[i]):
                cp.wait()
            wgu_bf[...] = wgu_f32[...].astype(BF16)
            wd_bf[...] = wd_f32[...].astype(BF16)
            nxt = nxt_ref[be_ref[i]]

            @pl.when(nxt >= 0)
            def _():
                for cp in weight_copies(nxt):
                    cp.start()

        x = _from_row_tiles(xs_ref, 0, MOE_BM).astype(BF16)
        hu = jnp.dot(x, wgu_bf[...], preferred_element_type=F32) + bgu_ref[...]
        glu = jnp.minimum(hu[:, :D_EXPERT], SWIGLU_LIMIT)
        lin = jnp.clip(hu[:, D_EXPERT:], -SWIGLU_LIMIT, SWIGLU_LIMIT)
        act = glu * jax.nn.sigmoid(SWIGLU_ALPHA * glu) * (lin + 1.0)
        y = jnp.dot(act.astype(BF16), wd_bf[...], preferred_element_type=F32) + bd_ref[...]
        _to_row_tiles(ys_ref, y, MOE_BM)

    @pl.when(i >= n_used)
    def _():
        ys_ref[...] = jnp.zeros_like(ys_ref)


def _experts(block_e, next_e, n_used, xs2d, w_gu, b_gu, w_down, b_down):
    n_blocks = block_e.shape[0]
    rows = MOE_BM * LANE_CHUNKS
    live = lambda i, be, nx, nu: (jnp.minimum(i, nu[0] - 1), 0)
    return pl.pallas_call(
        _expert_kernel,
        out_shape=jax.ShapeDtypeStruct(xs2d.shape, F32),
        grid_spec=pltpu.PrefetchScalarGridSpec(
            num_scalar_prefetch=3,
            grid=(n_blocks,),
            in_specs=[pl.BlockSpec((rows, LANES), live),
                      pl.BlockSpec(memory_space=pl.ANY),
                      pl.BlockSpec((None, 1, 2 * D_EXPERT), lambda i, be, nx, nu: (be[i], 0, 0)),
                      pl.BlockSpec(memory_space=pl.ANY),
                      pl.BlockSpec((None, 1, D_MODEL), lambda i, be, nx, nu: (be[i], 0, 0))],
            out_specs=pl.BlockSpec((rows, LANES), lambda i, be, nx, nu: (i, 0)),
            scratch_shapes=[pltpu.VMEM((D_MODEL, 2 * D_EXPERT), F32),
                            pltpu.VMEM((D_EXPERT, D_MODEL), F32),
                            pltpu.VMEM((D_MODEL, 2 * D_EXPERT), BF16),
                            pltpu.VMEM((D_EXPERT, D_MODEL), BF16),
                            pltpu.SemaphoreType.DMA((2,))]),
        compiler_params=_cparams(("arbitrary",)),
    )(block_e, next_e, n_used, xs2d, w_gu, b_gu, w_down, b_down)


def _combine_kernel(dcur_ref, dnext_ref, ys_hbm, g_ref, x1_ref, gt_ref, np_ref, o_ref, ybuf, row_sem, *, tm):
    i = pl.program_id(0)
    n_rows = TOP_K * tm
    slot = i % 2

    def slot_rows(s):
        return ybuf.at[pl.ds(pl.multiple_of(s * n_rows * LANE_CHUNKS, n_rows * LANE_CHUNKS), n_rows * LANE_CHUNKS), :]

    def issue(d_ref, s):
        dst = slot_rows(s)
        for n in range(n_rows):
            pltpu.make_async_copy(ys_hbm.at[d_ref[0, n]], dst.at[pl.ds(n * LANE_CHUNKS, LANE_CHUNKS), :],
                                  row_sem.at[s]).start(priority=n % 2)

    @pl.when(i == 0)
    def _():
        issue(dcur_ref, 0)

    @pl.when(i + 1 < pl.num_programs(0))
    def _():
        issue(dnext_ref, 1 - slot)

    pltpu.make_async_copy(slot_rows(slot), slot_rows(slot), row_sem.at[slot]).wait()
    g = g_ref[...]
    y = None
    for kk in range(TOP_K):
        rows = _from_row_tiles(ybuf, (slot * n_rows + kk * tm) * LANE_CHUNKS, tm)
        y = g[:, kk:kk + 1] * rows if y is None else y + g[:, kk:kk + 1] * rows
    o_ref[...] = x1_ref[...] + gt_ref[...] * _rms(y, np_ref[...])


def _combine(dest_tiles, ys_tiles, gates_t, x1, gt, npost, tm, tiles_per_seq, tile0):
    n_tiles = dest_tiles.shape[0]
    t = n_tiles * tm
    smem_tile = lambda f: pl.BlockSpec((None, 1, TOP_K * tm), f, memory_space=pltpu.SMEM)
    return pl.pallas_call(
        functools.partial(_combine_kernel, tm=tm),
        out_shape=jax.ShapeDtypeStruct((t, D_MODEL), F32),
        grid=(n_tiles,),
        in_specs=[smem_tile(lambda i: (i, 0, 0)),
                  smem_tile(lambda i: (jnp.minimum(i + 1, n_tiles - 1), 0, 0)),
                  pl.BlockSpec(memory_space=pl.ANY),
                  pl.BlockSpec((tm, TOP_K), lambda i: (i + tile0, 0)),
                  pl.BlockSpec((tm, D_MODEL), lambda i: (i + tile0, 0)),
                  _mod_spec(gt, tm, tiles_per_seq),
                  pl.BlockSpec((1, D_MODEL), lambda i: (0, 0))],
        out_specs=pl.BlockSpec((tm, D_MODEL), lambda i: (i, 0)),
        scratch_shapes=[pltpu.VMEM((2 * TOP_K * tm * LANE_CHUNKS, LANES), F32),
                        pltpu.SemaphoreType.DMA((2,))],
        compiler_params=_cparams(("arbitrary",)),
    )(dest_tiles, dest_tiles, ys_tiles, gates_t, x1, gt, npost)


def _expert_layout(counts, n_blocks):
    padded = (counts + MOE_BM - 1) // MOE_BM * MOE_BM
    pad_ends = jnp.cumsum(padded)
    pad_starts = pad_ends - padded
    block_start = jnp.arange(n_blocks, dtype=I32) * MOE_BM
    block_e = jnp.minimum(jnp.sum((pad_ends[None, :] <= block_start[:, None]).astype(I32), axis=1), N_EXPERTS - 1)
    n_used = (pad_ends[-1:] // MOE_BM).astype(I32)
    e_ids = jnp.arange(N_EXPERTS, dtype=I32)
    later = (counts > 0)[None, :] & (e_ids[None, :] > e_ids[:, None])
    next_e = jnp.min(jnp.where(later, e_ids[None, :], N_EXPERTS), axis=1)
    next_e = jnp.where(next_e < N_EXPERTS, next_e, -1).astype(I32)
    return pad_starts, pad_starts + counts, pad_ends, block_e, next_e, n_used


def _slots(idx, rank, pad_starts):
    onehot = idx[:, :, None] == jnp.arange(N_EXPERTS, dtype=I32)[None, None, :]
    return rank + jnp.sum(jnp.where(onehot, pad_starts[None, None, :], 0), axis=-1)


def _tile_major(a, tm):
    t = a.shape[1]
    return a.reshape(TOP_K, t // tm, tm).transpose(1, 0, 2).reshape(t // tm, 1, TOP_K * tm)


def _block_diag(wb):
    n, d, _ = wb.shape
    eye = jnp.eye(n, dtype=wb.dtype)
    return (wb[:, :, None, :] * eye[:, None, :, None]).reshape(n * d, n * d)


def _pick_tile(n, prefs):
    for p in prefs:
        if n % p == 0:
            return p
    raise ValueError(f"no tile for {n}")


def kernel(x_prompt, x_sample, state_rglru_conv, state_rglru_h, state_hgrn_S, c_prompt, c_sample, ada_w, ada_b, norm_pre_mix, norm_post_mix, norm_pre_ffn, norm_post_ffn, w_in, conv_w, conv_b, lru_wa, lru_ba, lru_wi, lru_bi, lru_lambda, hg_lb, hg_norm, w_out, router_w, router_b, w_gu, b_gu, w_down, b_down):
    assert ada_w.shape[0] == 1, "single-layer trunk"
    bp, lp, _ = x_prompt.shape
    bs, ls, _ = x_sample.shape
    tp, ts = bp * lp, bs * ls
    assert CONV_WIDTH - 1 <= ls <= SUBLANES

    w_in_bf = w_in[0].astype(BF16)
    w_out_bf = w_out[0].astype(BF16)
    wg = jnp.concatenate([_block_diag(lru_wa[0]), _block_diag(lru_wi[0])], axis=1).astype(BF16)
    bg = jnp.concatenate([lru_ba[0], lru_bi[0]])[None, :]
    lbs = jnp.cumsum(jax.nn.softmax(hg_lb.astype(F32), axis=0), axis=0)[0][None, :]
    mw = (conv_w[0], conv_b[0][None, :], wg, bg, lru_lambda[0][None, :], lbs, hg_norm[0][None, :])
    row = lambda p: p[0][None, :]

    n_c = bp + bs
    n_c_pad = -(-n_c // SUBLANES) * SUBLANES
    c_all = jnp.concatenate([c_prompt, c_sample, jnp.zeros((n_c_pad - n_c, D_MODEL), F32)], axis=0)
    mod = _ada(c_all, ada_w[0], ada_b[0][None, :])
    mod_p = [m[:, None, :] for m in jnp.split(mod[:bp], 6, axis=-1)]
    mod_s = [jnp.repeat(m, ls, axis=0) for m in jnp.split(mod[bp:n_c], 6, axis=-1)]

    xp = x_prompt.reshape(tp, D_MODEL)
    xs = x_sample.reshape(ts, D_MODEL)
    tm_s = _pick_tile(ts, (256, 128, 64, 32, 16, 8))

    tc = _pick_tile(lp, (256, 128, 64))
    m_p, conv_p, h_p, s_p = _mixer_prompt(xp, mod_p[1], mod_p[0], row(norm_pre_mix), w_in_bf, bp, lp, mw, tc)
    proj_s = _inproj(xs, mod_s[1], mod_s[0], row(norm_pre_mix), w_in_bf, tm_s, 1)

    proj_s_pad = jnp.pad(proj_s.reshape(bs, ls, IN_COLS), ((0, 0), (0, SUBLANES - ls), (0, 0)))
    cst_hdr = jnp.pad(state_rglru_conv[0], ((0, 0), (CONV_HDR - (CONV_WIDTH - 1), 0), (0, 0)))
    m_s_pad, conv_s, h_s, s_s = _mixer_sample(proj_s_pad, cst_hdr, state_rglru_h[0][:, None, :],
                                              state_hgrn_S[0], mw, ls)
    m_s = m_s_pad[:, :ls, :].reshape(ts, D_MODEL)

    tmq = _pick_tile(math.gcd(lp, ts), (512, 256, 128, 64, 32, 16, 8))
    x1, hf, idx, gate, rank, counts = _post(
        m_p, m_s, xp, xs, (mod_p[2], mod_p[4], mod_p[3]), (mod_s[2], mod_s[4], mod_s[3]),
        row(norm_post_mix), row(norm_pre_ffn), w_out_bf, router_w[0].T, router_b[0][:, None], tmq, lp // tmq)

    n_blocks = -(-(TOP_K * (tp + ts)) // MOE_BM) + N_EXPERTS
    n_slots = n_blocks * MOE_BM
    pad_starts, fill_lo, fill_hi, block_e, next_e, n_used = _expert_layout(counts[:, 0].astype(I32), n_blocks)
    dest = _slots(idx, rank, pad_starts)
    tmd = _pick_tile(math.gcd(lp, ts), (128, 64, 32, 16, 8))
    dtiles = _tile_major(dest, tmd)
    xs_rows = _dispatch(fill_lo, fill_hi, dtiles, hf, n_slots, tmd)
    ys = _experts(block_e, next_e, n_used, xs_rows.reshape(n_slots * LANE_CHUNKS, LANES),
                  w_gu[0], b_gu[0][:, None, :], w_down[0], b_down[0][:, None, :])
    ys_tiles = ys.reshape(-1, LANE_CHUNKS, LANES)

    gates_t = gate.T
    y_p = _combine(dtiles[:tp // tmd], ys_tiles, gates_t, x1, mod_p[5], row(norm_post_ffn), tmd, lp // tmd, 0)
    y_s = _combine(dtiles[tp // tmd:], ys_tiles, gates_t, x1, mod_s[5], row(norm_post_ffn), tmd, 1, tp // tmd)

    return (y_p.reshape(bp, lp, D_MODEL), y_s.reshape(bs, ls, D_MODEL),
            conv_p[None], h_p.reshape(1, bp, LRU_WIDTH), s_p[None],
            conv_s[None], h_s.reshape(1, bs, LRU_WIDTH), s_s[None])
```

```python
import functools
import math

import jax
import jax.numpy as jnp
from jax import lax
from jax.experimental import pallas as pl
from jax.experimental.pallas import tpu as pltpu

F32 = jnp.float32
BF16 = jnp.bfloat16
I32 = jnp.int32

D_MODEL = 1024
LRU_WIDTH = 512
LRU_BLOCKS = 8
LRU_BLOCK = LRU_WIDTH // LRU_BLOCKS
CONV_WIDTH = 4
LRU_C = 8.0
HG_WIDTH = 512
HG_HEAD_DIM = 128
HG_HEADS = 4
IN_COLS = 2 * LRU_WIDTH + 4 * HG_WIDTH
N_EXPERTS = 32
TOP_K = 4
D_EXPERT = 1024
SWIGLU_LIMIT = 7.0
SWIGLU_ALPHA = 1.702
RMS_EPS = 1e-6

C_XL, C_YL, C_Q, C_F, C_V, C_G = 0, 512, 1024, 1536, 2048, 2560

SUBLANES = 8
LANES = 128
LANE_CHUNKS = D_MODEL // LANES
CONV_HDR = SUBLANES
HG_CHUNK = 64
HG_MAX_EXPONENT = 60.0
MOE_BM = 256
ZERO_ROWS = 64
VMEM_LIMIT = 56 * 1024 * 1024

_NT = (((1,), (1,)), ((), ()))
_TN = (((0,), (0,)), ((), ()))


def _cparams(sem):
    return pltpu.CompilerParams(dimension_semantics=sem, vmem_limit_bytes=VMEM_LIMIT)


def _rms(x, g):
    return x * lax.rsqrt(jnp.mean(x * x, axis=-1, keepdims=True) + RMS_EPS) * g


def _gelu_tanh(x):
    c = math.sqrt(2.0 / math.pi)
    return 0.5 * x * (1.0 + jnp.tanh(c * (x + 0.044715 * (x * x * x))))


def _ada_kernel(c_ref, w_ref, b_ref, o_ref):
    c = c_ref[...]
    s = c * jax.nn.sigmoid(c)
    o_ref[...] = jnp.dot(s.astype(BF16), w_ref[...].astype(BF16), preferred_element_type=F32) + b_ref[...]


def _ada(c_all, ada_w, ada_b):
    n = c_all.shape[0]
    tn = 1024
    return pl.pallas_call(
        _ada_kernel,
        out_shape=jax.ShapeDtypeStruct((n, 6 * D_MODEL), F32),
        grid=(6 * D_MODEL // tn,),
        in_specs=[pl.BlockSpec((n, D_MODEL), lambda j: (0, 0)),
                  pl.BlockSpec((D_MODEL, tn), lambda j: (0, j)),
                  pl.BlockSpec((1, tn), lambda j: (0, j))],
        out_specs=pl.BlockSpec((n, tn), lambda j: (0, j)),
        compiler_params=_cparams(("arbitrary",)),
    )(c_all, ada_w, ada_b)


def _inproj_kernel(x_ref, sc_ref, sh_ref, g_ref, w_ref, o_ref):
    h = _rms(x_ref[...], g_ref[...]) * (1.0 + sc_ref[...]) + sh_ref[...]
    o_ref[...] = jnp.dot(h.astype(BF16), w_ref[...], preferred_element_type=F32)


def _mod_spec(mod, tm, tiles_per_seq):
    if mod.ndim == 3:
        return pl.BlockSpec((None, 1, D_MODEL), lambda i: (i // tiles_per_seq, 0, 0))
    return pl.BlockSpec((tm, D_MODEL), lambda i: (i, 0))


def _inproj(x, sc, sh, g, w_bf, tm, tiles_per_seq):
    t = x.shape[0]
    return pl.pallas_call(
        _inproj_kernel,
        out_shape=jax.ShapeDtypeStruct((t, IN_COLS), F32),
        grid=(t // tm,),
        in_specs=[pl.BlockSpec((tm, D_MODEL), lambda i: (i, 0)),
                  _mod_spec(sc, tm, tiles_per_seq), _mod_spec(sh, tm, tiles_per_seq),
                  pl.BlockSpec((1, D_MODEL), lambda i: (0, 0)),
                  pl.BlockSpec((D_MODEL, IN_COLS), lambda i: (0, 0))],
        out_specs=pl.BlockSpec((tm, IN_COLS), lambda i: (i, 0)),
        compiler_params=_cparams(("arbitrary",)),
    )(x, sc, sh, g, w_bf)


def _sigmoid(x):
    return 0.5 * jnp.tanh(0.5 * x) + 0.5


def _group_rows(x):
    rows, w = x.shape
    return x.reshape(rows // SUBLANES, SUBLANES, w)


def _scan_rows(a, u, h0):
    a3, u3 = _group_rows(a), _group_rows(u)
    r3 = lax.broadcasted_iota(I32, a3.shape, 1)
    s = 1
    while s < SUBLANES:
        keep = r3 >= s
        a_sh = jnp.where(keep, pltpu.roll(a3, s, 1), 1.0)
        u_sh = jnp.where(keep, pltpu.roll(u3, s, 1), 0.0)
        u3 = a3 * u_sh + u3
        a3 = a3 * a_sh
        s *= 2
    out, h = [], h0
    for i in range(a3.shape[0]):
        hg = u3[i] + a3[i] * h
        h = hg[SUBLANES - 1:SUBLANES, :]
        out.append(hg)
    return jnp.concatenate(out, axis=0), h


def _chunk_cumsum(x, chunk):
    x3 = _group_rows(x)
    r3 = lax.broadcasted_iota(I32, x3.shape, 1)
    s = 1
    while s < SUBLANES:
        x3 = x3 + jnp.where(r3 >= s, pltpu.roll(x3, s, 1), 0.0)
        s *= 2
    out, carry = [], None
    for i in range(x3.shape[0]):
        cur = x3[i] if i % (chunk // SUBLANES) == 0 else x3[i] + carry
        carry = cur[SUBLANES - 1:SUBLANES, :]
        out.append(cur)
    return jnp.concatenate(out, axis=0)


def _lru_gates(xc, wg_ref, bg_ref, lam_ref):
    gates = jnp.dot(xc.astype(BF16), wg_ref[...], preferred_element_type=F32) + bg_ref[...]
    r = _sigmoid(gates[:, :LRU_WIDTH])
    ig = _sigmoid(gates[:, LRU_WIDTH:])
    z = -lam_ref[...]
    softplus = jnp.maximum(z, 0.0) + jnp.log1p(jnp.exp(-jnp.abs(z)))
    log_a = -LRU_C * r * softplus
    a = jnp.exp(log_a)
    th = jnp.tanh(log_a)
    mult2 = -2.0 * th / (1.0 - th)
    return a, jnp.sqrt(mult2), ig


def _conv4(xbuf, xl, cw_ref, cb_ref, rows):
    cw = cw_ref[...]
    h = CONV_HDR
    return (cb_ref[...] + cw[0:1] * xbuf[h - 3:h - 3 + rows, :] + cw[1:2] * xbuf[h - 2:h - 2 + rows, :]
            + cw[2:3] * xbuf[h - 1:h - 1 + rows, :] + cw[3:4] * xl)


def _conv4_rolled(xl, prev, cw_ref, cb_ref):
    cw = cw_ref[...]
    row8 = lax.broadcasted_iota(I32, prev.shape, 0)
    acc = cb_ref[...] + cw[CONV_WIDTH - 1:CONV_WIDTH] * xl
    for s in range(1, CONV_WIDTH):
        sh = pltpu.roll(xl, s, 0)
        top = jnp.where(row8 < s, pltpu.roll(prev, s, 0), sh[0:SUBLANES, :])
        sh = jnp.concatenate([top, sh[SUBLANES:, :]], axis=0)
        acc = acc + cw[CONV_WIDTH - 1 - s:CONV_WIDTH - s] * sh
    return acc


def _head_norm_gate(o, gate, hgn):
    return _rms(o, hgn) * _sigmoid(gate)


def _hgrn_window(q, k, v, b, s_t, steps):
    row = lax.broadcasted_iota(I32, q.shape, 0)
    valid = row < steps
    b_end = b[steps - 1:steps, :]
    q_in = jnp.where(valid, q * jnp.exp(b), 0.0)
    o = lax.dot_general(q_in.astype(BF16), s_t.astype(BF16), _NT, preferred_element_type=F32)
    for d in range(steps):
        ok = valid & (row >= d)
        k_d = k if d == 0 else pltpu.roll(k, d, 0)
        v_d = v if d == 0 else pltpu.roll(v, d, 0)
        b_d = b if d == 0 else pltpu.roll(b, d, 0)
        decay = jnp.exp(jnp.where(ok, b - b_d, 0.0))
        wgt = jnp.sum(jnp.where(ok, q * k_d * decay, 0.0), axis=-1, keepdims=True)
        o = o + wgt * v_d
    k_out = jnp.where(valid, k * jnp.exp(jnp.where(valid, b_end - b, 0.0)), 0.0)
    v_ok = jnp.where(valid, v, 0.0)
    s_new = s_t * jnp.exp(b_end) + lax.dot_general(v_ok.astype(BF16), k_out.astype(BF16), _TN,
                                                   preferred_element_type=F32)
    return o, s_new


def _tile_cumsum(x):
    row = lax.broadcasted_iota(I32, x.shape, 0)
    s = 1
    while s < SUBLANES:
        x = x + jnp.where(row >= s, pltpu.roll(x, s, 0), 0.0)
        s *= 2
    return x


def _mixer_prompt_kernel(x_ref, sc_ref, sh_ref, g_ref, win_ref, cw_ref, cb_ref, wg_ref, bg_ref, lam_ref, lb_ref,
                         hgn_ref, m_ref, conv_ref, h_ref, s_ref,
                         proj_ref, xprev, hcar, st, st0, obuf, *, tc):
    j = pl.program_id(1)
    w = LRU_WIDTH

    @pl.when(j == 0)
    def _():
        xprev[...] = jnp.zeros_like(xprev)
        hcar[...] = jnp.zeros_like(hcar)
        st[...] = jnp.zeros_like(st)

    hn = _rms(x_ref[...], g_ref[...]) * (1.0 + sc_ref[...]) + sh_ref[...]
    proj_ref[...] = jnp.dot(hn.astype(BF16), win_ref[...], preferred_element_type=F32)

    xl = proj_ref[:, C_XL:C_XL + w]
    xc = _conv4_rolled(xl, xprev[...], cw_ref, cb_ref)
    xprev[...] = xl[tc - SUBLANES:tc, :]
    a, mult, ig = _lru_gates(xc, wg_ref, bg_ref, lam_ref)
    row = lax.broadcasted_iota(I32, (tc, w), 0)
    mult = jnp.where((row == 0) & (j == 0), 1.0, mult)
    h, h_last = _scan_rows(a, mult * ig * xc, hcar[0:1, :])
    hcar[0:1, :] = h_last
    m_ref[:, 0:w] = (h * _gelu_tanh(proj_ref[:, C_YL:C_YL + w])).astype(BF16)

    lb = lb_ref[...]
    f = lb + (1.0 - lb) * _sigmoid(proj_ref[:, C_F:C_F + w])
    k = 1.0 - f
    b = _chunk_cumsum(jnp.log(f), HG_CHUNK)
    q = proj_ref[:, C_Q:C_Q + w]
    v = proj_ref[:, C_V:C_V + w]
    mid = HG_CHUNK // 2
    st0[...] = st[...]
    tri = (lax.broadcasted_iota(I32, (HG_CHUNK, HG_CHUNK), 0)
           >= lax.broadcasted_iota(I32, (HG_CHUNK, HG_CHUNK), 1))
    for hd in range(HG_HEADS):
        cs = slice(hd * HG_HEAD_DIM, (hd + 1) * HG_HEAD_DIM)
        s_t = st[hd]
        for c in range(tc // HG_CHUNK):
            rs = slice(c * HG_CHUNK, (c + 1) * HG_CHUNK)
            bc, qc, kc = b[rs, cs], q[rs, cs], k[rs, cs]
            vb = v[rs, cs].astype(BF16)
            b_mid = bc[mid - 1:mid, :]
            b_end = bc[HG_CHUNK - 1:HG_CHUNK, :]
            qp = (qc * jnp.exp(bc - b_mid)).astype(BF16)
            kp = (kc * jnp.exp(b_mid - bc)).astype(BF16)
            att = lax.dot_general(qp, kp, _NT, preferred_element_type=F32)
            att = jnp.where(tri, att, 0.0)
            q_in = (qc * jnp.exp(bc)).astype(BF16)
            o = (jnp.dot(att.astype(BF16), vb, preferred_element_type=F32)
                 + lax.dot_general(q_in, s_t.astype(BF16), _NT, preferred_element_type=F32))
            k_out = (kc * jnp.exp(b_end - bc)).astype(BF16)
            s_t = s_t * jnp.exp(b_end) + lax.dot_general(vb, k_out, _TN, preferred_element_type=F32)
            obuf[rs, cs] = o
        st[hd] = s_t

    half_decay = None
    for c in range(tc // HG_CHUNK):
        b_mid = b[c * HG_CHUNK + mid - 1:c * HG_CHUNK + mid, :]
        b_end = b[(c + 1) * HG_CHUNK - 1:(c + 1) * HG_CHUNK, :]
        worst = jnp.maximum(-b_mid, b_mid - b_end)
        half_decay = worst if half_decay is None else jnp.maximum(half_decay, worst)

    @pl.when(jnp.logical_not(jnp.max(half_decay) < HG_MAX_EXPONENT))
    def _():
        st[...] = st0[...]

        def window(gi, carry):
            rows = pl.ds(pl.multiple_of(gi * SUBLANES, SUBLANES), SUBLANES)
            f8 = lb + (1.0 - lb) * _sigmoid(proj_ref[rows, C_F:C_F + w])
            b8 = _tile_cumsum(jnp.log(f8))
            k8 = 1.0 - f8
            q8 = proj_ref[rows, C_Q:C_Q + w]
            v8 = proj_ref[rows, C_V:C_V + w]
            for hd in range(HG_HEADS):
                cs = slice(hd * HG_HEAD_DIM, (hd + 1) * HG_HEAD_DIM)
                o, s_new = _hgrn_window(q8[:, cs], k8[:, cs], v8[:, cs], b8[:, cs], st[hd], SUBLANES)
                st[hd] = s_new
                obuf[rows, cs] = o
            return carry
        lax.fori_loop(0, tc // SUBLANES, window, 0)

    hgn = hgn_ref[...]
    for hd in range(HG_HEADS):
        cs = slice(hd * HG_HEAD_DIM, (hd + 1) * HG_HEAD_DIM)
        gate = proj_ref[:, C_G + hd * HG_HEAD_DIM:C_G + (hd + 1) * HG_HEAD_DIM]
        m_ref[:, w + hd * HG_HEAD_DIM:w + (hd + 1) * HG_HEAD_DIM] = (
            _head_norm_gate(obuf[:, cs], gate, hgn).astype(BF16))

    @pl.when(j == pl.num_programs(1) - 1)
    def _():
        conv_ref[...] = xprev[SUBLANES - (CONV_WIDTH - 1):SUBLANES, :]
        h_ref[...] = hcar[0:1, :]
        for hd in range(HG_HEADS):
            s_ref[hd] = st[hd].T


def _mixer_prompt(x, sc, sh, g, w_in_bf, nb, seq, mw, tc):
    cw, cb, wg, bg, lam, lb, hgn = mw
    nj = seq // tc
    const = lambda shape: pl.BlockSpec(shape, lambda b, j: tuple(0 for _ in shape))
    per_seq = lambda: pl.BlockSpec((None, 1, D_MODEL), lambda b, j: (b, 0, 0))
    return pl.pallas_call(
        functools.partial(_mixer_prompt_kernel, tc=tc),
        out_shape=(jax.ShapeDtypeStruct((nb * seq, D_MODEL), BF16),
                   jax.ShapeDtypeStruct((nb, CONV_WIDTH - 1, LRU_WIDTH), F32),
                   jax.ShapeDtypeStruct((nb, 1, LRU_WIDTH), F32),
                   jax.ShapeDtypeStruct((nb, HG_HEADS, HG_HEAD_DIM, HG_HEAD_DIM), F32)),
        grid=(nb, nj),
        in_specs=[pl.BlockSpec((tc, D_MODEL), lambda b, j: (b * nj + j, 0)),
                  per_seq(), per_seq(), const((1, D_MODEL)), const((D_MODEL, IN_COLS)),
                  const((CONV_WIDTH, LRU_WIDTH)), const((1, LRU_WIDTH)),
                  const((LRU_WIDTH, 2 * LRU_WIDTH)), const((1, 2 * LRU_WIDTH)),
                  const((1, LRU_WIDTH)), const((1, HG_WIDTH)), const((1, HG_HEAD_DIM))],
        out_specs=(pl.BlockSpec((tc, D_MODEL), lambda b, j: (b * nj + j, 0)),
                   pl.BlockSpec((None, CONV_WIDTH - 1, LRU_WIDTH), lambda b, j: (b, 0, 0)),
                   pl.BlockSpec((None, 1, LRU_WIDTH), lambda b, j: (b, 0, 0)),
                   pl.BlockSpec((None, HG_HEADS, HG_HEAD_DIM, HG_HEAD_DIM), lambda b, j: (b, 0, 0, 0))),
        scratch_shapes=[pltpu.VMEM((tc, IN_COLS), F32),
                        pltpu.VMEM((SUBLANES, LRU_WIDTH), F32),
                        pltpu.VMEM((SUBLANES, LRU_WIDTH), F32),
                        pltpu.VMEM((HG_HEADS, HG_HEAD_DIM, HG_HEAD_DIM), F32),
                        pltpu.VMEM((HG_HEADS, HG_HEAD_DIM, HG_HEAD_DIM), F32),
                        pltpu.VMEM((tc, HG_WIDTH), F32)],
        compiler_params=_cparams(("arbitrary", "arbitrary")),
    )(x, sc, sh, g, w_in_bf, cw, cb, wg, bg, lam, lb, hgn)


def _mixer_sample_kernel(proj_ref, cst_ref, h0_ref, s0_ref, cw_ref, cb_ref, wg_ref, bg_ref, lam_ref, lb_ref,
                         hgn_ref, m_ref, conv_ref, h_ref, s_ref, xbuf, *, steps):
    for bi in range(proj_ref.shape[0]):
        _mixer_sample_one(proj_ref.at[bi], cst_ref.at[bi], h0_ref.at[bi], s0_ref.at[bi], cw_ref, cb_ref, wg_ref,
                          bg_ref, lam_ref, lb_ref, hgn_ref, m_ref.at[bi], conv_ref.at[bi], h_ref.at[bi],
                          s_ref.at[bi], xbuf.at[bi], steps)


def _mixer_sample_one(proj_ref, cst_ref, h0_ref, s0_ref, cw_ref, cb_ref, wg_ref, bg_ref, lam_ref, lb_ref,
                      hgn_ref, m_ref, conv_ref, h_ref, s_ref, xbuf, steps):
    w = LRU_WIDTH
    rows = SUBLANES
    row = lax.broadcasted_iota(I32, (rows, w), 0)
    valid = row < steps

    xl = proj_ref[:, C_XL:C_XL + w]
    xbuf[0:CONV_HDR, :] = cst_ref[...]
    xbuf[CONV_HDR:CONV_HDR + rows, :] = xl
    xc = _conv4(xbuf, xl, cw_ref, cb_ref, rows)
    conv_ref[...] = xbuf[CONV_HDR + steps - 3:CONV_HDR + steps, :]
    a, mult, ig = _lru_gates(xc, wg_ref, bg_ref, lam_ref)
    u = mult * ig * xc
    h = h0_ref[...]
    hs = jnp.zeros((rows, w), F32)
    for t in range(steps):
        h = a[t:t + 1, :] * h + u[t:t + 1, :]
        hs = jnp.where(row == t, h, hs)
    h_ref[...] = h
    m_ref[:, 0:w] = (hs * _gelu_tanh(proj_ref[:, C_YL:C_YL + w])).astype(BF16)

    lb = lb_ref[...]
    f = lb + (1.0 - lb) * _sigmoid(proj_ref[:, C_F:C_F + w])
    k = 1.0 - f
    b = _tile_cumsum(jnp.log(f))
    q = proj_ref[:, C_Q:C_Q + w]
    v = proj_ref[:, C_V:C_V + w]
    hgn = hgn_ref[...]
    for hd in range(HG_HEADS):
        cs = slice(hd * HG_HEAD_DIM, (hd + 1) * HG_HEAD_DIM)
        o, s_new = _hgrn_window(q[:, cs], k[:, cs], v[:, cs], b[:, cs], s0_ref[hd].T, steps)
        s_ref[hd] = s_new.T
        gate = proj_ref[:, C_G + hd * HG_HEAD_DIM:C_G + (hd + 1) * HG_HEAD_DIM]
        m_ref[:, w + hd * HG_HEAD_DIM:w + (hd + 1) * HG_HEAD_DIM] = _head_norm_gate(o, gate, hgn).astype(BF16)


def _mixer_sample(proj_pad, cst_hdr, h0, s0, mw, steps):
    cw, cb, wg, bg, lam, lb, hgn = mw
    nb = proj_pad.shape[0]
    per_step = _pick_tile(nb, (4, 2, 1))
    const = lambda shape: pl.BlockSpec(shape, lambda b: tuple(0 for _ in shape))
    per_b = lambda shape: pl.BlockSpec((per_step,) + shape, lambda b: (b,) + tuple(0 for _ in shape))
    return pl.pallas_call(
        functools.partial(_mixer_sample_kernel, steps=steps),
        out_shape=(jax.ShapeDtypeStruct((nb, SUBLANES, D_MODEL), BF16),
                   jax.ShapeDtypeStruct((nb, CONV_WIDTH - 1, LRU_WIDTH), F32),
                   jax.ShapeDtypeStruct((nb, 1, LRU_WIDTH), F32),
                   jax.ShapeDtypeStruct((nb, HG_HEADS, HG_HEAD_DIM, HG_HEAD_DIM), F32)),
        grid=(nb // per_step,),
        in_specs=[per_b((SUBLANES, IN_COLS)), per_b((CONV_HDR, LRU_WIDTH)), per_b((1, LRU_WIDTH)),
                  per_b((HG_HEADS, HG_HEAD_DIM, HG_HEAD_DIM)),
                  const((CONV_WIDTH, LRU_WIDTH)), const((1, LRU_WIDTH)),
                  const((LRU_WIDTH, 2 * LRU_WIDTH)), const((1, 2 * LRU_WIDTH)),
                  const((1, LRU_WIDTH)), const((1, HG_WIDTH)), const((1, HG_HEAD_DIM))],
        out_specs=(per_b((SUBLANES, D_MODEL)), per_b((CONV_WIDTH - 1, LRU_WIDTH)), per_b((1, LRU_WIDTH)),
                   per_b((HG_HEADS, HG_HEAD_DIM, HG_HEAD_DIM))),
        scratch_shapes=[pltpu.VMEM((per_step, CONV_HDR + SUBLANES, LRU_WIDTH), F32)],
        compiler_params=_cparams(("arbitrary",)),
    )(proj_pad, cst_hdr, h0, s0, cw, cb, wg, bg, lam, lb, hgn)


def _to_row_tiles(ref, val, rows):
    for c in range(LANE_CHUNKS):
        ref[pl.ds(c, rows, stride=LANE_CHUNKS), :] = val[:, c * LANES:(c + 1) * LANES]


def _from_row_tiles(ref, base, rows):
    return jnp.concatenate(
        [ref[pl.ds(base + c, rows, stride=LANE_CHUNKS), :] for c in range(LANE_CHUNKS)], axis=-1)


def _post_kernel(mp_ref, ms_ref, xp_ref, xs_ref, gtmp_ref, scfp_ref, shfp_ref, gtms_ref, scfs_ref, shfs_ref,
                 npost_ref, npre_ref, wout_ref, rwt_ref, rb_ref,
                 x1_ref, hf_ref, idx_ref, gate_ref, rank_ref, cnt_out_ref, cnt, *, n_p):
    i = pl.program_id(0)
    tm = xp_ref.shape[0]

    @pl.when(i == 0)
    def _():
        cnt[...] = jnp.zeros_like(cnt)

    def run(m_ref, x_ref, gtm_ref, scf_ref, shf_ref):
        mix = jnp.dot(m_ref[...], wout_ref[...], preferred_element_type=F32)
        x1 = x_ref[...] + gtm_ref[...] * _rms(mix, npost_ref[...])
        x1_ref[...] = x1
        hf = _rms(x1, npre_ref[...]) * (1.0 + scf_ref[...]) + shf_ref[...]
        _to_row_tiles(hf_ref, hf, tm)
        logits = lax.dot_general(rwt_ref[...], hf, _NT, precision=lax.Precision.HIGHEST,
                                 preferred_element_type=F32) + rb_ref[...]
        e_iota = lax.broadcasted_iota(I32, logits.shape, 0)
        vals, idxs = [], []
        for _ in range(TOP_K):
            mx = jnp.max(logits, axis=0, keepdims=True)
            ix = jnp.min(jnp.where(logits == mx, e_iota, N_EXPERTS), axis=0, keepdims=True)
            vals.append(mx)
            idxs.append(ix)
            logits = jnp.where(e_iota == ix, -jnp.inf, logits)
        exps = [jnp.exp(vv - vals[0]) for vv in vals]
        den = exps[0] + exps[1] + exps[2] + exps[3]
        earlier = (lax.broadcasted_iota(I32, (tm, tm), 0) < lax.broadcasted_iota(I32, (tm, tm), 1))
        earlier = jnp.where(earlier, 1.0, 0.0).astype(BF16)
        base = cnt[...]
        for kk in range(TOP_K):
            idx_ref[kk:kk + 1, :] = idxs[kk]
            gate_ref[kk:kk + 1, :] = exps[kk] / den
            hit = e_iota == idxs[kk]
            onehot = jnp.where(hit, 1.0, 0.0)
            before = jnp.dot(onehot.astype(BF16), earlier, preferred_element_type=F32)
            rank = jnp.sum(jnp.where(hit, before + base, 0.0), axis=0, keepdims=True)
            rank_ref[kk:kk + 1, :] = rank.astype(I32)
            base = base + jnp.sum(onehot, axis=1, keepdims=True)
        cnt[...] = base

    @pl.when(i < n_p)
    def _():
        run(mp_ref, xp_ref, gtmp_ref, scfp_ref, shfp_ref)

    @pl.when(i >= n_p)
    def _():
        run(ms_ref, xs_ref, gtms_ref, scfs_ref, shfs_ref)

    @pl.when(i == pl.num_programs(0) - 1)
    def _():
        cnt_out_ref[...] = cnt[...]


def _post(m_p, m_s, x_p, x_s, mod_p, mod_s, npost, npre, wout_bf, rwt, rb, tm, tiles_per_seq):
    n_p, n_s = x_p.shape[0] // tm, x_s.shape[0] // tm
    t = (n_p + n_s) * tm
    first = lambda i: jnp.minimum(i, n_p - 1)
    second = lambda i: jnp.maximum(i - n_p, 0)
    row = lambda: pl.BlockSpec((1, D_MODEL), lambda i: (0, 0))
    tok_p = lambda: pl.BlockSpec((tm, D_MODEL), lambda i: (first(i), 0))
    tok_s = lambda: pl.BlockSpec((tm, D_MODEL), lambda i: (second(i), 0))
    seq_p = lambda: pl.BlockSpec((None, 1, D_MODEL), lambda i: (first(i) // tiles_per_seq, 0, 0))
    per_k = lambda: pl.BlockSpec((TOP_K, tm), lambda i: (0, i))
    return pl.pallas_call(
        functools.partial(_post_kernel, n_p=n_p),
        out_shape=(jax.ShapeDtypeStruct((t, D_MODEL), F32),
                   jax.ShapeDtypeStruct((t * LANE_CHUNKS, LANES), F32),
                   jax.ShapeDtypeStruct((TOP_K, t), I32), jax.ShapeDtypeStruct((TOP_K, t), F32),
                   jax.ShapeDtypeStruct((TOP_K, t), I32), jax.ShapeDtypeStruct((N_EXPERTS, 1), F32)),
        grid=(n_p + n_s,),
        in_specs=[tok_p(), tok_s(), tok_p(), tok_s(), seq_p(), seq_p(), seq_p(), tok_s(), tok_s(), tok_s(),
                  row(), row(),
                  pl.BlockSpec((D_MODEL, D_MODEL), lambda i: (0, 0)),
                  pl.BlockSpec((N_EXPERTS, D_MODEL), lambda i: (0, 0)),
                  pl.BlockSpec((N_EXPERTS, 1), lambda i: (0, 0))],
        out_specs=(pl.BlockSpec((tm, D_MODEL), lambda i: (i, 0)),
                   pl.BlockSpec((tm * LANE_CHUNKS, LANES), lambda i: (i, 0)),
                   per_k(), per_k(), per_k(), pl.BlockSpec((N_EXPERTS, 1), lambda i: (0, 0))),
        scratch_shapes=[pltpu.VMEM((N_EXPERTS, 1), F32)],
        compiler_params=_cparams(("arbitrary",)),
    )(m_p, m_s, x_p, x_s, *mod_p, *mod_s, npost, npre, wout_bf, rwt, rb)


def _dispatch_kernel(fill_lo_ref, fill_hi_ref, dest_ref, hf_ref, xs_hbm, zeros, row_sem, pad_sem, *, tm, n_slots):
    i = pl.program_id(0)
    n_rows = TOP_K * tm

    @pl.when(i == 0)
    def _():
        zeros[...] = jnp.zeros_like(zeros)

        def per_expert(e, total):
            lo, hi = fill_lo_ref[e], fill_hi_ref[e]

            def per_slot(s, carry):
                pltpu.make_async_copy(zeros.at[0], xs_hbm.at[s], pad_sem).start()
                return carry
            lax.fori_loop(lo, hi, per_slot, 0)
            return total + (hi - lo)
        n_fill = lax.fori_loop(0, N_EXPERTS, per_expert, 0)
        tail_lo = fill_hi_ref[N_EXPERTS - 1] // ZERO_ROWS
        n_tail = n_slots // ZERO_ROWS - tail_lo

        def per_tail(c, carry):
            pltpu.make_async_copy(zeros, xs_hbm.at[pl.ds((tail_lo + c) * ZERO_ROWS, ZERO_ROWS)], pad_sem).start()
            return carry
        lax.fori_loop(0, n_tail, per_tail, 0)

        def drain_slot(s, carry):
            pltpu.make_async_copy(zeros.at[0], xs_hbm.at[0], pad_sem).wait()
            return carry
        lax.fori_loop(0, n_fill, drain_slot, 0)

        def drain_tail(c, carry):
            pltpu.make_async_copy(zeros, xs_hbm.at[pl.ds(0, ZERO_ROWS)], pad_sem).wait()
            return carry
        lax.fori_loop(0, n_tail, drain_tail, 0)

    for n in range(n_rows):
        pltpu.make_async_copy(hf_ref.at[pl.ds((n % tm) * LANE_CHUNKS, LANE_CHUNKS), :],
                              xs_hbm.at[dest_ref[0, n]], row_sem).start(priority=n % 2)
    pltpu.make_async_copy(xs_hbm.at[pl.ds(0, n_rows)], xs_hbm.at[pl.ds(0, n_rows)], row_sem).wait()


def _dispatch(fill_lo, fill_hi, dest_tiles, hf_rows, n_slots, tm):
    return pl.pallas_call(
        functools.partial(_dispatch_kernel, tm=tm, n_slots=n_slots),
        out_shape=jax.ShapeDtypeStruct((n_slots, LANE_CHUNKS, LANES), F32),
        grid_spec=pltpu.PrefetchScalarGridSpec(
            num_scalar_prefetch=2, grid=(dest_tiles.shape[0],),
            in_specs=[pl.BlockSpec((None, 1, TOP_K * tm), lambda i, lo, hi: (i, 0, 0), memory_space=pltpu.SMEM),
                      pl.BlockSpec((tm * LANE_CHUNKS, LANES), lambda i, lo, hi: (i, 0))],
            out_specs=pl.BlockSpec(memory_space=pl.ANY),
            scratch_shapes=[pltpu.VMEM((ZERO_ROWS, LANE_CHUNKS, LANES), F32),
                            pltpu.SemaphoreType.DMA, pltpu.SemaphoreType.DMA]),
        compiler_params=_cparams(("arbitrary",)),
    )(fill_lo, fill_hi, dest_tiles, hf_rows)


def _expert_kernel(be_ref, nxt_ref, nu_ref, xs_ref, wgu_hbm, bgu_ref, wd_hbm, bd_ref, ys_ref,
                   wgu_f32, wd_f32, wgu_bf, wd_bf, w_sem):
    i = pl.program_id(0)
    n_used = nu_ref[0]

    def weight_copies(e):
        return (pltpu.make_async_copy(wgu_hbm.at[e], wgu_f32, w_sem.at[0]),
                pltpu.make_async_copy(wd_hbm.at[e], wd_f32, w_sem.at[1]))

    @pl.when(i == 0)
    def _():
        for cp in weight_copies(be_ref[0]):
            cp.start()

    @pl.when(i < n_used)
    def _():
        prev = be_ref[jnp.maximum(i - 1, 0)]

        @pl.when((i == 0) | (be_ref[i] != prev))
        def _():
            for cp in weight_copies(be_ref[i]):
                cp.wait()
            wgu_bf[...] = wgu_f32[...].astype(BF16)
            wd_bf[...] = wd_f32[...].astype(BF16)
            nxt = nxt_ref[be_ref[i]]

            @pl.when(nxt >= 0)
            def _():
                for cp in weight_copies(nxt):
                    cp.start()

        x = _from_row_tiles(xs_ref, 0, MOE_BM).astype(BF16)
        hu = jnp.dot(x, wgu_bf[...], preferred_element_type=F32) + bgu_ref[...]
        glu = jnp.minimum(hu[:, :D_EXPERT], SWIGLU_LIMIT)
        lin = jnp.clip(hu[:, D_EXPERT:], -SWIGLU_LIMIT, SWIGLU_LIMIT)
        act = glu * jax.nn.sigmoid(SWIGLU_ALPHA * glu) * (lin + 1.0)
        y = jnp.dot(act.astype(BF16), wd_bf[...], preferred_element_type=F32) + bd_ref[...]
        _to_row_tiles(ys_ref, y, MOE_BM)

    @pl.when(i >= n_used)
    def _():
        ys_ref[...] = jnp.zeros_like(ys_ref)


def _experts(block_e, next_e, n_used, xs2d, w_gu, b_gu, w_down, b_down):
    n_blocks = block_e.shape[0]
    rows = MOE_BM * LANE_CHUNKS
    live = lambda i, be, nx, nu: (jnp.minimum(i, nu[0] - 1), 0)
    return pl.pallas_call(
        _expert_kernel,
        out_shape=jax.ShapeDtypeStruct(xs2d.shape, F32),
        grid_spec=pltpu.PrefetchScalarGridSpec(
            num_scalar_prefetch=3,
            grid=(n_blocks,),
            in_specs=[pl.BlockSpec((rows, LANES), live),
                      pl.BlockSpec(memory_space=pl.ANY),
                      pl.BlockSpec((None, 1, 2 * D_EXPERT), lambda i, be, nx, nu: (be[i], 0, 0)),
                      pl.BlockSpec(memory_space=pl.ANY),
                      pl.BlockSpec((None, 1, D_MODEL), lambda i, be, nx, nu: (be[i], 0, 0))],
            out_specs=pl.BlockSpec((rows, LANES), lambda i, be, nx, nu: (i, 0)),
            scratch_shapes=[pltpu.VMEM((D_MODEL, 2 * D_EXPERT), F32),
                            pltpu.VMEM((D_EXPERT, D_MODEL), F32),
                            pltpu.VMEM((D_MODEL, 2 * D_EXPERT), BF16),
                            pltpu.VMEM((D_EXPERT, D_MODEL), BF16),
                            pltpu.SemaphoreType.DMA((2,))]),
        compiler_params=_cparams(("arbitrary",)),
    )(block_e, next_e, n_used, xs2d, w_gu, b_gu, w_down, b_down)


def _combine_kernel(dcur_ref, dnext_ref, ys_hbm, g_ref, x1_ref, gt_ref, np_ref, o_ref, ybuf, row_sem, *, tm):
    i = pl.program_id(0)
    n_rows = TOP_K * tm
    slot = i % 2

    def slot_rows(s):
        return ybuf.at[pl.ds(pl.multiple_of(s * n_rows * LANE_CHUNKS, n_rows * LANE_CHUNKS), n_rows * LANE_CHUNKS), :]

    def issue(d_ref, s):
        dst = slot_rows(s)
        for n in range(n_rows):
            pltpu.make_async_copy(ys_hbm.at[d_ref[0, n]], dst.at[pl.ds(n * LANE_CHUNKS, LANE_CHUNKS), :],
                                  row_sem.at[s]).start(priority=n % 2)

    @pl.when(i == 0)
    def _():
        issue(dcur_ref, 0)

    @pl.when(i + 1 < pl.num_programs(0))
    def _():
        issue(dnext_ref, 1 - slot)

    pltpu.make_async_copy(slot_rows(slot), slot_rows(slot), row_sem.at[slot]).wait()
    g = g_ref[...]
    y = None
    for kk in range(TOP_K):
        rows = _from_row_tiles(ybuf, (slot * n_rows + kk * tm) * LANE_CHUNKS, tm)
        y = g[:, kk:kk + 1] * rows if y is None else y + g[:, kk:kk + 1] * rows
    o_ref[...] = x1_ref[...] + gt_ref[...] * _rms(y, np_ref[...])


def _combine(dest_tiles, ys_tiles, gates_t, x1, gt, npost, tm, tiles_per_seq, tile0):
    n_tiles = dest_tiles.shape[0]
    t = n_tiles * tm
    smem_tile = lambda f: pl.BlockSpec((None, 1, TOP_K * tm), f, memory_space=pltpu.SMEM)
    return pl.pallas_call(
        functools.partial(_combine_kernel, tm=tm),
        out_shape=jax.ShapeDtypeStruct((t, D_MODEL), F32),
        grid=(n_tiles,),
        in_specs=[smem_tile(lambda i: (i, 0, 0)),
                  smem_tile(lambda i: (jnp.minimum(i + 1, n_tiles - 1), 0, 0)),
                  pl.BlockSpec(memory_space=pl.ANY),
                  pl.BlockSpec((tm, TOP_K), lambda i: (i + tile0, 0)),
                  pl.BlockSpec((tm, D_MODEL), lambda i: (i + tile0, 0)),
                  _mod_spec(gt, tm, tiles_per_seq),
                  pl.BlockSpec((1, D_MODEL), lambda i: (0, 0))],
        out_specs=pl.BlockSpec((tm, D_MODEL), lambda i: (i, 0)),
        scratch_shapes=[pltpu.VMEM((2 * TOP_K * tm * LANE_CHUNKS, LANES), F32),
                        pltpu.SemaphoreType.DMA((2,))],
        compiler_params=_cparams(("arbitrary",)),
    )(dest_tiles, dest_tiles, ys_tiles, gates_t, x1, gt, npost)


def _expert_layout(counts, n_blocks):
    padded = (counts + MOE_BM - 1) // MOE_BM * MOE_BM
    pad_ends = jnp.cumsum(padded)
    pad_starts = pad_ends - padded
    block_start = jnp.arange(n_blocks, dtype=I32) * MOE_BM
    block_e = jnp.minimum(jnp.sum((pad_ends[None, :] <= block_start[:, None]).astype(I32), axis=1), N_EXPERTS - 1)
    n_used = (pad_ends[-1:] // MOE_BM).astype(I32)
    e_ids = jnp.arange(N_EXPERTS, dtype=I32)
    later = (counts > 0)[None, :] & (e_ids[None, :] > e_ids[:, None])
    next_e = jnp.min(jnp.where(later, e_ids[None, :], N_EXPERTS), axis=1)
    next_e = jnp.where(next_e < N_EXPERTS, next_e, -1).astype(I32)
    return pad_starts, pad_starts + counts, pad_ends, block_e, next_e, n_used


def _slots(idx, rank, pad_starts):
    onehot = idx[:, :, None] == jnp.arange(N_EXPERTS, dtype=I32)[None, None, :]
    return rank + jnp.sum(jnp.where(onehot, pad_starts[None, None, :], 0), axis=-1)


def _tile_major(a, tm):
    t = a.shape[1]
    return a.reshape(TOP_K, t // tm, tm).transpose(1, 0, 2).reshape(t // tm, 1, TOP_K * tm)


def _block_diag(wb):
    n, d, _ = wb.shape
    eye = jnp.eye(n, dtype=wb.dtype)
    return (wb[:, :, None, :] * eye[:, None, :, None]).reshape(n * d, n * d)


def _pick_tile(n, prefs):
    for p in prefs:
        if n % p == 0:
            return p
    raise ValueError(f"no tile for {n}")


def kernel(x_prompt, x_sample, state_rglru_conv, state_rglru_h, state_hgrn_S, c_prompt, c_sample, ada_w, ada_b, norm_pre_mix, norm_post_mix, norm_pre_ffn, norm_post_ffn, w_in, conv_w, conv_b, lru_wa, lru_ba, lru_wi, lru_bi, lru_lambda, hg_lb, hg_norm, w_out, router_w, router_b, w_gu, b_gu, w_down, b_down):
    assert ada_w.shape[0] == 1, "single-layer trunk"
    bp, lp, _ = x_prompt.shape
    bs, ls, _ = x_sample.shape
    tp, ts = bp * lp, bs * ls
    assert CONV_WIDTH - 1 <= ls <= SUBLANES

    w_in_bf = w_in[0].astype(BF16)
    w_out_bf = w_out[0].astype(BF16)
    wg = jnp.concatenate([_block_diag(lru_wa[0]), _block_diag(lru_wi[0])], axis=1).astype(BF16)
    bg = jnp.concatenate([lru_ba[0], lru_bi[0]])[None, :]
    lbs = jnp.cumsum(jax.nn.softmax(hg_lb.astype(F32), axis=0), axis=0)[0][None, :]
    mw = (conv_w[0], conv_b[0][None, :], wg, bg, lru_lambda[0][None, :], lbs, hg_norm[0][None, :])
    row = lambda p: p[0][None, :]

    n_c = bp + bs
    n_c_pad = -(-n_c // SUBLANES) * SUBLANES
    c_all = jnp.concatenate([c_prompt, c_sample, jnp.zeros((n_c_pad - n_c, D_MODEL), F32)], axis=0)
    mod = _ada(c_all, ada_w[0], ada_b[0][None, :])
    mod_p = [m[:, None, :] for m in jnp.split(mod[:bp], 6, axis=-1)]
    mod_s = [jnp.repeat(m, ls, axis=0) for m in jnp.split(mod[bp:n_c], 6, axis=-1)]

    xp = x_prompt.reshape(tp, D_MODEL)
    xs = x_sample.reshape(ts, D_MODEL)
    tm_s = _pick_tile(ts, (256, 128, 64, 32, 16, 8))

    tc = _pick_tile(lp, (256, 128, 64))
    m_p, conv_p, h_p, s_p = _mixer_prompt(xp, mod_p[1], mod_p[0], row(norm_pre_mix), w_in_bf, bp, lp, mw, tc)
    proj_s = _inproj(xs, mod_s[1], mod_s[0], row(norm_pre_mix), w_in_bf, tm_s, 1)

    proj_s_pad = jnp.pad(proj_s.reshape(bs, ls, IN_COLS), ((0, 0), (0, SUBLANES - ls), (0, 0)))
    cst_hdr = jnp.pad(state_rglru_conv[0], ((0, 0), (CONV_HDR - (CONV_WIDTH - 1), 0), (0, 0)))
    m_s_pad, conv_s, h_s, s_s = _mixer_sample(proj_s_pad, cst_hdr, state_rglru_h[0][:, None, :],
                                              state_hgrn_S[0], mw, ls)
    m_s = m_s_pad[:, :ls, :].reshape(ts, D_MODEL)

    tmq = _pick_tile(math.gcd(lp, ts), (512, 256, 128, 64, 32, 16, 8))
    x1, hf, idx, gate, rank, counts = _post(
        m_p, m_s, xp, xs, (mod_p[2], mod_p[4], mod_p[3]), (mod_s[2], mod_s[4], mod_s[3]),
        row(norm_post_mix), row(norm_pre_ffn), w_out_bf, router_w[0].T, router_b[0][:, None], tmq, lp // tmq)

    n_blocks = -(-(TOP_K * (tp + ts)) // MOE_BM) + N_EXPERTS
    n_slots = n_blocks * MOE_BM
    pad_starts, fill_lo, fill_hi, block_e, next_e, n_used = _expert_layout(counts[:, 0].astype(I32), n_blocks)
    dest = _slots(idx, rank, pad_starts)
    tmd = _pick_tile(math.gcd(lp, ts), (128, 64, 32, 16, 8))
    dtiles = _tile_major(dest, tmd)
    xs_rows = _dispatch(fill_lo, fill_hi, dtiles, hf, n_slots, tmd)
    ys = _experts(block_e, next_e, n_used, xs_rows.reshape(n_slots * LANE_CHUNKS, LANES),
                  w_gu[0], b_gu[0][:, None, :], w_down[0], b_down[0][:, None, :])
    ys_tiles = ys.reshape(-1, LANE_CHUNKS, LANES)

    gates_t = gate.T
    y_p = _combine(dtiles[:tp // tmd], ys_tiles, gates_t, x1, mod_p[5], row(norm_post_ffn), tmd, lp // tmd, 0)
    y_s = _combine(dtiles[tp // tmd:], ys_tiles, gates_t, x1, mod_s[5], row(norm_post_ffn), tmd, 1, tp // tmd)

    return (y_p.reshape(bp, lp, D_MODEL), y_s.reshape(bs, ls, D_MODEL),
            conv_p[None], h_p.reshape(1, bp, LRU_WIDTH), s_p[None],
            conv_s[None], h_s.reshape(1, bs, LRU_WIDTH), s_s[None])
```

```python
import functools
import math

import jax
import jax.numpy as jnp
from jax import lax
from jax.experimental import pallas as pl
from jax.experimental.pallas import tpu as pltpu

F32 = jnp.float32
BF16 = jnp.bfloat16
I32 = jnp.int32

D_MODEL = 1024
LRU_WIDTH = 512
LRU_BLOCKS = 8
LRU_BLOCK = LRU_WIDTH // LRU_BLOCKS
CONV_WIDTH = 4
LRU_C = 8.0
HG_WIDTH = 512
HG_HEAD_DIM = 128
HG_HEADS = 4
IN_COLS = 2 * LRU_WIDTH + 4 * HG_WIDTH
N_EXPERTS = 32
TOP_K = 4
D_EXPERT = 1024
SWIGLU_LIMIT = 7.0
SWIGLU_ALPHA = 1.702
RMS_EPS = 1e-6

C_XL, C_YL, C_Q, C_F, C_V, C_G = 0, 512, 1024, 1536, 2048, 2560

SUBLANES = 8
LANES = 128
LANE_CHUNKS = D_MODEL // LANES
CONV_HDR = SUBLANES
HG_CHUNK = 64
HG_MAX_EXPONENT = 60.0
MOE_BM = 512
ZERO_ROWS = 64
VMEM_LIMIT = 56 * 1024 * 1024

_NT = (((1,), (1,)), ((), ()))
_TN = (((0,), (0,)), ((), ()))


def _cparams(sem):
    return pltpu.CompilerParams(dimension_semantics=sem, vmem_limit_bytes=VMEM_LIMIT)


def _rms(x, g):
    return x * lax.rsqrt(jnp.mean(x * x, axis=-1, keepdims=True) + RMS_EPS) * g


def _gelu_tanh(x):
    c = math.sqrt(2.0 / math.pi)
    return 0.5 * x * (1.0 + jnp.tanh(c * (x + 0.044715 * (x * x * x))))


def _ada_kernel(c_ref, w_ref, b_ref, o_ref):
    c = c_ref[...]
    s = c * jax.nn.sigmoid(c)
    o_ref[...] = jnp.dot(s.astype(BF16), w_ref[...].astype(BF16), preferred_element_type=F32) + b_ref[...]


def _ada(c_all, ada_w, ada_b):
    n = c_all.shape[0]
    tn = 1024
    return pl.pallas_call(
        _ada_kernel,
        out_shape=jax.ShapeDtypeStruct((n, 6 * D_MODEL), F32),
        grid=(6 * D_MODEL // tn,),
        in_specs=[pl.BlockSpec((n, D_MODEL), lambda j: (0, 0)),
                  pl.BlockSpec((D_MODEL, tn), lambda j: (0, j)),
                  pl.BlockSpec((1, tn), lambda j: (0, j))],
        out_specs=pl.BlockSpec((n, tn), lambda j: (0, j)),
        compiler_params=_cparams(("arbitrary",)),
    )(c_all, ada_w, ada_b)


def _inproj_kernel(x_ref, sc_ref, sh_ref, g_ref, w_ref, o_ref):
    h = _rms(x_ref[...], g_ref[...]) * (1.0 + sc_ref[...]) + sh_ref[...]
    o_ref[...] = jnp.dot(h.astype(BF16), w_ref[...], preferred_element_type=F32)


def _mod_spec(mod, tm, tiles_per_seq):
    if mod.ndim == 3:
        return pl.BlockSpec((None, 1, D_MODEL), lambda i: (i // tiles_per_seq, 0, 0))
    return pl.BlockSpec((tm, D_MODEL), lambda i: (i, 0))


def _inproj(x, sc, sh, g, w_bf, tm, tiles_per_seq):
    t = x.shape[0]
    return pl.pallas_call(
        _inproj_kernel,
        out_shape=jax.ShapeDtypeStruct((t, IN_COLS), F32),
        grid=(t // tm,),
        in_specs=[pl.BlockSpec((tm, D_MODEL), lambda i: (i, 0)),
                  _mod_spec(sc, tm, tiles_per_seq), _mod_spec(sh, tm, tiles_per_seq),
                  pl.BlockSpec((1, D_MODEL), lambda i: (0, 0)),
                  pl.BlockSpec((D_MODEL, IN_COLS), lambda i: (0, 0))],
        out_specs=pl.BlockSpec((tm, IN_COLS), lambda i: (i, 0)),
        compiler_params=_cparams(("arbitrary",)),
    )(x, sc, sh, g, w_bf)


def _sigmoid(x):
    return 0.5 * jnp.tanh(0.5 * x) + 0.5


def _group_rows(x):
    rows, w = x.shape
    return x.reshape(rows // SUBLANES, SUBLANES, w)


def _scan_rows(a, u, h0):
    a3, u3 = _group_rows(a), _group_rows(u)
    r3 = lax.broadcasted_iota(I32, a3.shape, 1)
    s = 1
    while s < SUBLANES:
        keep = r3 >= s
        a_sh = jnp.where(keep, pltpu.roll(a3, s, 1), 1.0)
        u_sh = jnp.where(keep, pltpu.roll(u3, s, 1), 0.0)
        u3 = a3 * u_sh + u3
        a3 = a3 * a_sh
        s *= 2
    out, h = [], h0
    for i in range(a3.shape[0]):
        hg = u3[i] + a3[i] * h
        h = hg[SUBLANES - 1:SUBLANES, :]
        out.append(hg)
    return jnp.concatenate(out, axis=0), h


def _chunk_cumsum(x, chunk):
    x3 = _group_rows(x)
    r3 = lax.broadcasted_iota(I32, x3.shape, 1)
    s = 1
    while s < SUBLANES:
        x3 = x3 + jnp.where(r3 >= s, pltpu.roll(x3, s, 1), 0.0)
        s *= 2
    out, carry = [], None
    for i in range(x3.shape[0]):
        cur = x3[i] if i % (chunk // SUBLANES) == 0 else x3[i] + carry
        carry = cur[SUBLANES - 1:SUBLANES, :]
        out.append(cur)
    return jnp.concatenate(out, axis=0)


def _lru_gates(xc, wg_ref, bg_ref, lam_ref):
    gates = jnp.dot(xc.astype(BF16), wg_ref[...], preferred_element_type=F32) + bg_ref[...]
    r = _sigmoid(gates[:, :LRU_WIDTH])
    ig = _sigmoid(gates[:, LRU_WIDTH:])
    z = -lam_ref[...]
    softplus = jnp.maximum(z, 0.0) + jnp.log1p(jnp.exp(-jnp.abs(z)))
    log_a = -LRU_C * r * softplus
    a = jnp.exp(log_a)
    th = jnp.tanh(log_a)
    mult2 = -2.0 * th / (1.0 - th)
    return a, jnp.sqrt(mult2), ig


def _conv4(xbuf, xl, cw_ref, cb_ref, rows):
    cw = cw_ref[...]
    h = CONV_HDR
    return (cb_ref[...] + cw[0:1] * xbuf[h - 3:h - 3 + rows, :] + cw[1:2] * xbuf[h - 2:h - 2 + rows, :]
            + cw[2:3] * xbuf[h - 1:h - 1 + rows, :] + cw[3:4] * xl)


def _conv4_rolled(xl, prev, cw_ref, cb_ref):
    cw = cw_ref[...]
    row8 = lax.broadcasted_iota(I32, prev.shape, 0)
    acc = cb_ref[...] + cw[CONV_WIDTH - 1:CONV_WIDTH] * xl
    for s in range(1, CONV_WIDTH):
        sh = pltpu.roll(xl, s, 0)
        top = jnp.where(row8 < s, pltpu.roll(prev, s, 0), sh[0:SUBLANES, :])
        sh = jnp.concatenate([top, sh[SUBLANES:, :]], axis=0)
        acc = acc + cw[CONV_WIDTH - 1 - s:CONV_WIDTH - s] * sh
    return acc


def _head_norm_gate(o, gate, hgn):
    return _rms(o, hgn) * _sigmoid(gate)


def _hgrn_window(q, k, v, b, s_t, steps):
    row = lax.broadcasted_iota(I32, q.shape, 0)
    valid = row < steps
    b_end = b[steps - 1:steps, :]
    q_in = jnp.where(valid, q * jnp.exp(b), 0.0)
    o = lax.dot_general(q_in.astype(BF16), s_t.astype(BF16), _NT, preferred_element_type=F32)
    for d in range(steps):
        ok = valid & (row >= d)
        k_d = k if d == 0 else pltpu.roll(k, d, 0)
        v_d = v if d == 0 else pltpu.roll(v, d, 0)
        b_d = b if d == 0 else pltpu.roll(b, d, 0)
        decay = jnp.exp(jnp.where(ok, b - b_d, 0.0))
        wgt = jnp.sum(jnp.where(ok, q * k_d * decay, 0.0), axis=-1, keepdims=True)
        o = o + wgt * v_d
    k_out = jnp.where(valid, k * jnp.exp(jnp.where(valid, b_end - b, 0.0)), 0.0)
    v_ok = jnp.where(valid, v, 0.0)
    s_new = s_t * jnp.exp(b_end) + lax.dot_general(v_ok.astype(BF16), k_out.astype(BF16), _TN,
                                                   preferred_element_type=F32)
    return o, s_new


def _tile_cumsum(x):
    row = lax.broadcasted_iota(I32, x.shape, 0)
    s = 1
    while s < SUBLANES:
        x = x + jnp.where(row >= s, pltpu.roll(x, s, 0), 0.0)
        s *= 2
    return x


def _mixer_prompt_kernel(x_ref, sc_ref, sh_ref, g_ref, win_ref, cw_ref, cb_ref, wg_ref, bg_ref, lam_ref, lb_ref,
                         hgn_ref, m_ref, conv_ref, h_ref, s_ref,
                         proj_ref, xprev, hcar, st, st0, obuf, *, tc):
    j = pl.program_id(1)
    w = LRU_WIDTH

    @pl.when(j == 0)
    def _():
        xprev[...] = jnp.zeros_like(xprev)
        hcar[...] = jnp.zeros_like(hcar)
        st[...] = jnp.zeros_like(st)

    hn = _rms(x_ref[...], g_ref[...]) * (1.0 + sc_ref[...]) + sh_ref[...]
    proj_ref[...] = jnp.dot(hn.astype(BF16), win_ref[...], preferred_element_type=F32)

    xl = proj_ref[:, C_XL:C_XL + w]
    xc = _conv4_rolled(xl, xprev[...], cw_ref, cb_ref)
    xprev[...] = xl[tc - SUBLANES:tc, :]
    a, mult, ig = _lru_gates(xc, wg_ref, bg_ref, lam_ref)
    row = lax.broadcasted_iota(I32, (tc, w), 0)
    mult = jnp.where((row == 0) & (j == 0), 1.0, mult)
    h, h_last = _scan_rows(a, mult * ig * xc, hcar[0:1, :])
    hcar[0:1, :] = h_last
    m_ref[:, 0:w] = (h * _gelu_tanh(proj_ref[:, C_YL:C_YL + w])).astype(BF16)

    lb = lb_ref[...]
    f = lb + (1.0 - lb) * _sigmoid(proj_ref[:, C_F:C_F + w])
    k = 1.0 - f
    b = _chunk_cumsum(jnp.log(f), HG_CHUNK)
    q = proj_ref[:, C_Q:C_Q + w]
    v = proj_ref[:, C_V:C_V + w]
    mid = HG_CHUNK // 2
    st0[...] = st[...]
    tri = (lax.broadcasted_iota(I32, (HG_CHUNK, HG_CHUNK), 0)
           >= lax.broadcasted_iota(I32, (HG_CHUNK, HG_CHUNK), 1))
    for hd in range(HG_HEADS):
        cs = slice(hd * HG_HEAD_DIM, (hd + 1) * HG_HEAD_DIM)
        s_t = st[hd]
        for c in range(tc // HG_CHUNK):
            rs = slice(c * HG_CHUNK, (c + 1) * HG_CHUNK)
            bc, qc, kc = b[rs, cs], q[rs, cs], k[rs, cs]
            vb = v[rs, cs].astype(BF16)
            b_mid = bc[mid - 1:mid, :]
            b_end = bc[HG_CHUNK - 1:HG_CHUNK, :]
            qp = (qc * jnp.exp(bc - b_mid)).astype(BF16)
            kp = (kc * jnp.exp(b_mid - bc)).astype(BF16)
            att = lax.dot_general(qp, kp, _NT, preferred_element_type=F32)
            att = jnp.where(tri, att, 0.0)
            q_in = (qc * jnp.exp(bc)).astype(BF16)
            o = (jnp.dot(att.astype(BF16), vb, preferred_element_type=F32)
                 + lax.dot_general(q_in, s_t.astype(BF16), _NT, preferred_element_type=F32))
            k_out = (kc * jnp.exp(b_end - bc)).astype(BF16)
            s_t = s_t * jnp.exp(b_end) + lax.dot_general(vb, k_out, _TN, preferred_element_type=F32)
            obuf[rs, cs] = o
        st[hd] = s_t

    half_decay = None
    for c in range(tc // HG_CHUNK):
        b_mid = b[c * HG_CHUNK + mid - 1:c * HG_CHUNK + mid, :]
        b_end = b[(c + 1) * HG_CHUNK - 1:(c + 1) * HG_CHUNK, :]
        worst = jnp.maximum(-b_mid, b_mid - b_end)
        half_decay = worst if half_decay is None else jnp.maximum(half_decay, worst)

    @pl.when(jnp.logical_not(jnp.max(half_decay) < HG_MAX_EXPONENT))
    def _():
        st[...] = st0[...]

        def window(gi, carry):
            rows = pl.ds(pl.multiple_of(gi * SUBLANES, SUBLANES), SUBLANES)
            f8 = lb + (1.0 - lb) * _sigmoid(proj_ref[rows, C_F:C_F + w])
            b8 = _tile_cumsum(jnp.log(f8))
            k8 = 1.0 - f8
            q8 = proj_ref[rows, C_Q:C_Q + w]
            v8 = proj_ref[rows, C_V:C_V + w]
            for hd in range(HG_HEADS):
                cs = slice(hd * HG_HEAD_DIM, (hd + 1) * HG_HEAD_DIM)
                o, s_new = _hgrn_window(q8[:, cs], k8[:, cs], v8[:, cs], b8[:, cs], st[hd], SUBLANES)
                st[hd] = s_new
                obuf[rows, cs] = o
            return carry
        lax.fori_loop(0, tc // SUBLANES, window, 0)

    hgn = hgn_ref[...]
    for hd in range(HG_HEADS):
        cs = slice(hd * HG_HEAD_DIM, (hd + 1) * HG_HEAD_DIM)
        gate = proj_ref[:, C_G + hd * HG_HEAD_DIM:C_G + (hd + 1) * HG_HEAD_DIM]
        m_ref[:, w + hd * HG_HEAD_DIM:w + (hd + 1) * HG_HEAD_DIM] = (
            _head_norm_gate(obuf[:, cs], gate, hgn).astype(BF16))

    @pl.when(j == pl.num_programs(1) - 1)
    def _():
        conv_ref[...] = xprev[SUBLANES - (CONV_WIDTH - 1):SUBLANES, :]
        h_ref[...] = hcar[0:1, :]
        for hd in range(HG_HEADS):
            s_ref[hd] = st[hd].T


def _mixer_prompt(x, sc, sh, g, w_in_bf, nb, seq, mw, tc):
    cw, cb, wg, bg, lam, lb, hgn = mw
    nj = seq // tc
    const = lambda shape: pl.BlockSpec(shape, lambda b, j: tuple(0 for _ in shape))
    per_seq = lambda: pl.BlockSpec((None, 1, D_MODEL), lambda b, j: (b, 0, 0))
    return pl.pallas_call(
        functools.partial(_mixer_prompt_kernel, tc=tc),
        out_shape=(jax.ShapeDtypeStruct((nb * seq, D_MODEL), BF16),
                   jax.ShapeDtypeStruct((nb, CONV_WIDTH - 1, LRU_WIDTH), F32),
                   jax.ShapeDtypeStruct((nb, 1, LRU_WIDTH), F32),
                   jax.ShapeDtypeStruct((nb, HG_HEADS, HG_HEAD_DIM, HG_HEAD_DIM), F32)),
        grid=(nb, nj),
        in_specs=[pl.BlockSpec((tc, D_MODEL), lambda b, j: (b * nj + j, 0)),
                  per_seq(), per_seq(), const((1, D_MODEL)), const((D_MODEL, IN_COLS)),
                  const((CONV_WIDTH, LRU_WIDTH)), const((1, LRU_WIDTH)),
                  const((LRU_WIDTH, 2 * LRU_WIDTH)), const((1, 2 * LRU_WIDTH)),
                  const((1, LRU_WIDTH)), const((1, HG_WIDTH)), const((1, HG_HEAD_DIM))],
        out_specs=(pl.BlockSpec((tc, D_MODEL), lambda b, j: (b * nj + j, 0)),
                   pl.BlockSpec((None, CONV_WIDTH - 1, LRU_WIDTH), lambda b, j: (b, 0, 0)),
                   pl.BlockSpec((None, 1, LRU_WIDTH), lambda b, j: (b, 0, 0)),
                   pl.BlockSpec((None, HG_HEADS, HG_HEAD_DIM, HG_HEAD_DIM), lambda b, j: (b, 0, 0, 0))),
        scratch_shapes=[pltpu.VMEM((tc, IN_COLS), F32),
                        pltpu.VMEM((SUBLANES, LRU_WIDTH), F32),
                        pltpu.VMEM((SUBLANES, LRU_WIDTH), F32),
                        pltpu.VMEM((HG_HEADS, HG_HEAD_DIM, HG_HEAD_DIM), F32),
                        pltpu.VMEM((HG_HEADS, HG_HEAD_DIM, HG_HEAD_DIM), F32),
                        pltpu.VMEM((tc, HG_WIDTH), F32)],
        compiler_params=_cparams(("arbitrary", "arbitrary")),
    )(x, sc, sh, g, w_in_bf, cw, cb, wg, bg, lam, lb, hgn)


def _mixer_sample_kernel(proj_ref, cst_ref, h0_ref, s0_ref, cw_ref, cb_ref, wg_ref, bg_ref, lam_ref, lb_ref,
                         hgn_ref, m_ref, conv_ref, h_ref, s_ref, xbuf, *, steps):
    for bi in range(proj_ref.shape[0]):
        _mixer_sample_one(proj_ref.at[bi], cst_ref.at[bi], h0_ref.at[bi], s0_ref.at[bi], cw_ref, cb_ref, wg_ref,
                          bg_ref, lam_ref, lb_ref, hgn_ref, m_ref.at[bi], conv_ref.at[bi], h_ref.at[bi],
                          s_ref.at[bi], xbuf.at[bi], steps)


def _mixer_sample_one(proj_ref, cst_ref, h0_ref, s0_ref, cw_ref, cb_ref, wg_ref, bg_ref, lam_ref, lb_ref,
                      hgn_ref, m_ref, conv_ref, h_ref, s_ref, xbuf, steps):
    w = LRU_WIDTH
    rows = SUBLANES
    row = lax.broadcasted_iota(I32, (rows, w), 0)
    valid = row < steps

    xl = proj_ref[:, C_XL:C_XL + w]
    xbuf[0:CONV_HDR, :] = cst_ref[...]
    xbuf[CONV_HDR:CONV_HDR + rows, :] = xl
    xc = _conv4(xbuf, xl, cw_ref, cb_ref, rows)
    conv_ref[...] = xbuf[CONV_HDR + steps - 3:CONV_HDR + steps, :]
    a, mult, ig = _lru_gates(xc, wg_ref, bg_ref, lam_ref)
    u = mult * ig * xc
    h = h0_ref[...]
    hs = jnp.zeros((rows, w), F32)
    for t in range(steps):
        h = a[t:t + 1, :] * h + u[t:t + 1, :]
        hs = jnp.where(row == t, h, hs)
    h_ref[...] = h
    m_ref[:, 0:w] = (hs * _gelu_tanh(proj_ref[:, C_YL:C_YL + w])).astype(BF16)

    lb = lb_ref[...]
    f = lb + (1.0 - lb) * _sigmoid(proj_ref[:, C_F:C_F + w])
    k = 1.0 - f
    b = _tile_cumsum(jnp.log(f))
    q = proj_ref[:, C_Q:C_Q + w]
    v = proj_ref[:, C_V:C_V + w]
    hgn = hgn_ref[...]
    for hd in range(HG_HEADS):
        cs = slice(hd * HG_HEAD_DIM, (hd + 1) * HG_HEAD_DIM)
        o, s_new = _hgrn_window(q[:, cs], k[:, cs], v[:, cs], b[:, cs], s0_ref[hd].T, steps)
        s_ref[hd] = s_new.T
        gate = proj_ref[:, C_G + hd * HG_HEAD_DIM:C_G + (hd + 1) * HG_HEAD_DIM]
        m_ref[:, w + hd * HG_HEAD_DIM:w + (hd + 1) * HG_HEAD_DIM] = _head_norm_gate(o, gate, hgn).astype(BF16)


def _mixer_sample(proj_pad, cst_hdr, h0, s0, mw, steps):
    cw, cb, wg, bg, lam, lb, hgn = mw
    nb = proj_pad.shape[0]
    per_step = _pick_tile(nb, (4, 2, 1))
    const = lambda shape: pl.BlockSpec(shape, lambda b: tuple(0 for _ in shape))
    per_b = lambda shape: pl.BlockSpec((per_step,) + shape, lambda b: (b,) + tuple(0 for _ in shape))
    return pl.pallas_call(
        functools.partial(_mixer_sample_kernel, steps=steps),
        out_shape=(jax.ShapeDtypeStruct((nb, SUBLANES, D_MODEL), BF16),
                   jax.ShapeDtypeStruct((nb, CONV_WIDTH - 1, LRU_WIDTH), F32),
                   jax.ShapeDtypeStruct((nb, 1, LRU_WIDTH), F32),
                   jax.ShapeDtypeStruct((nb, HG_HEADS, HG_HEAD_DIM, HG_HEAD_DIM), F32)),
        grid=(nb // per_step,),
        in_specs=[per_b((SUBLANES, IN_COLS)), per_b((CONV_HDR, LRU_WIDTH)), per_b((1, LRU_WIDTH)),
                  per_b((HG_HEADS, HG_HEAD_DIM, HG_HEAD_DIM)),
                  const((CONV_WIDTH, LRU_WIDTH)), const((1, LRU_WIDTH)),
                  const((LRU_WIDTH, 2 * LRU_WIDTH)), const((1, 2 * LRU_WIDTH)),
                  const((1, LRU_WIDTH)), const((1, HG_WIDTH)), const((1, HG_HEAD_DIM))],
        out_specs=(per_b((SUBLANES, D_MODEL)), per_b((CONV_WIDTH - 1, LRU_WIDTH)), per_b((1, LRU_WIDTH)),
                   per_b((HG_HEADS, HG_HEAD_DIM, HG_HEAD_DIM))),
        scratch_shapes=[pltpu.VMEM((per_step, CONV_HDR + SUBLANES, LRU_WIDTH), F32)],
        compiler_params=_cparams(("arbitrary",)),
    )(proj_pad, cst_hdr, h0, s0, cw, cb, wg, bg, lam, lb, hgn)


def _to_row_tiles(ref, val, rows):
    for c in range(LANE_CHUNKS):
        ref[pl.ds(c, rows, stride=LANE_CHUNKS), :] = val[:, c * LANES:(c + 1) * LANES]


def _from_row_tiles(ref, base, rows):
    return jnp.concatenate(
        [ref[pl.ds(base + c, rows, stride=LANE_CHUNKS), :] for c in range(LANE_CHUNKS)], axis=-1)


def _post_kernel(mp_ref, ms_ref, xp_ref, xs_ref, gtmp_ref, scfp_ref, shfp_ref, gtms_ref, scfs_ref, shfs_ref,
                 npost_ref, npre_ref, wout_ref, rwt_ref, rb_ref,
                 x1_ref, hf_ref, idx_ref, gate_ref, rank_ref, cnt_out_ref, cnt, *, n_p):
    i = pl.program_id(0)
    tm = xp_ref.shape[0]

    @pl.when(i == 0)
    def _():
        cnt[...] = jnp.zeros_like(cnt)

    def run(m_ref, x_ref, gtm_ref, scf_ref, shf_ref):
        mix = jnp.dot(m_ref[...], wout_ref[...], preferred_element_type=F32)
        x1 = x_ref[...] + gtm_ref[...] * _rms(mix, npost_ref[...])
        x1_ref[...] = x1
        hf = _rms(x1, npre_ref[...]) * (1.0 + scf_ref[...]) + shf_ref[...]
        _to_row_tiles(hf_ref, hf, tm)
        logits = lax.dot_general(rwt_ref[...], hf, _NT, precision=lax.Precision.HIGHEST,
                                 preferred_element_type=F32) + rb_ref[...]
        e_iota = lax.broadcasted_iota(I32, logits.shape, 0)
        vals, idxs = [], []
        for _ in range(TOP_K):
            mx = jnp.max(logits, axis=0, keepdims=True)
            ix = jnp.min(jnp.where(logits == mx, e_iota, N_EXPERTS), axis=0, keepdims=True)
            vals.append(mx)
            idxs.append(ix)
            logits = jnp.where(e_iota == ix, -jnp.inf, logits)
        exps = [jnp.exp(vv - vals[0]) for vv in vals]
        den = exps[0] + exps[1] + exps[2] + exps[3]
        earlier = (lax.broadcasted_iota(I32, (tm, tm), 0) < lax.broadcasted_iota(I32, (tm, tm), 1))
        earlier = jnp.where(earlier, 1.0, 0.0).astype(BF16)
        base = cnt[...]
        for kk in range(TOP_K):
            idx_ref[kk:kk + 1, :] = idxs[kk]
            gate_ref[kk:kk + 1, :] = exps[kk] / den
            hit = e_iota == idxs[kk]
            onehot = jnp.where(hit, 1.0, 0.0)
            before = jnp.dot(onehot.astype(BF16), earlier, preferred_element_type=F32)
            rank = jnp.sum(jnp.where(hit, before + base, 0.0), axis=0, keepdims=True)
            rank_ref[kk:kk + 1, :] = rank.astype(I32)
            base = base + jnp.sum(onehot, axis=1, keepdims=True)
        cnt[...] = base

    @pl.when(i < n_p)
    def _():
        run(mp_ref, xp_ref, gtmp_ref, scfp_ref, shfp_ref)

    @pl.when(i >= n_p)
    def _():
        run(ms_ref, xs_ref, gtms_ref, scfs_ref, shfs_ref)

    @pl.when(i == pl.num_programs(0) - 1)
    def _():
        cnt_out_ref[...] = cnt[...]


def _post(m_p, m_s, x_p, x_s, mod_p, mod_s, npost, npre, wout_bf, rwt, rb, tm, tiles_per_seq):
    n_p, n_s = x_p.shape[0] // tm, x_s.shape[0] // tm
    t = (n_p + n_s) * tm
    first = lambda i: jnp.minimum(i, n_p - 1)
    second = lambda i: jnp.maximum(i - n_p, 0)
    row = lambda: pl.BlockSpec((1, D_MODEL), lambda i: (0, 0))
    tok_p = lambda: pl.BlockSpec((tm, D_MODEL), lambda i: (first(i), 0))
    tok_s = lambda: pl.BlockSpec((tm, D_MODEL), lambda i: (second(i), 0))
    seq_p = lambda: pl.BlockSpec((None, 1, D_MODEL), lambda i: (first(i) // tiles_per_seq, 0, 0))
    per_k = lambda: pl.BlockSpec((TOP_K, tm), lambda i: (0, i))
    return pl.pallas_call(
        functools.partial(_post_kernel, n_p=n_p),
        out_shape=(jax.ShapeDtypeStruct((t, D_MODEL), F32),
                   jax.ShapeDtypeStruct((t * LANE_CHUNKS, LANES), F32),
                   jax.ShapeDtypeStruct((TOP_K, t), I32), jax.ShapeDtypeStruct((TOP_K, t), F32),
                   jax.ShapeDtypeStruct((TOP_K, t), I32), jax.ShapeDtypeStruct((N_EXPERTS, 1), F32)),
        grid=(n_p + n_s,),
        in_specs=[tok_p(), tok_s(), tok_p(), tok_s(), seq_p(), seq_p(), seq_p(), tok_s(), tok_s(), tok_s(),
                  row(), row(),
                  pl.BlockSpec((D_MODEL, D_MODEL), lambda i: (0, 0)),
                  pl.BlockSpec((N_EXPERTS, D_MODEL), lambda i: (0, 0)),
                  pl.BlockSpec((N_EXPERTS, 1), lambda i: (0, 0))],
        out_specs=(pl.BlockSpec((tm, D_MODEL), lambda i: (i, 0)),
                   pl.BlockSpec((tm * LANE_CHUNKS, LANES), lambda i: (i, 0)),
                   per_k(), per_k(), per_k(), pl.BlockSpec((N_EXPERTS, 1), lambda i: (0, 0))),
        scratch_shapes=[pltpu.VMEM((N_EXPERTS, 1), F32)],
        compiler_params=_cparams(("arbitrary",)),
    )(m_p, m_s, x_p, x_s, *mod_p, *mod_s, npost, npre, wout_bf, rwt, rb)


def _dispatch_kernel(fill_lo_ref, fill_hi_ref, dest_ref, hf_ref, xs_hbm, zeros, row_sem, pad_sem, *, tm):
    i = pl.program_id(0)
    n_rows = TOP_K * tm

    @pl.when(i == 0)
    def _():
        zeros[...] = jnp.zeros_like(zeros)

        def per_expert(e, totals):
            lo, hi = fill_lo_ref[e], fill_hi_ref[e]
            mid = jnp.minimum((lo + ZERO_ROWS - 1) // ZERO_ROWS * ZERO_ROWS, hi)

            def per_row(s, carry):
                pltpu.make_async_copy(zeros.at[0], xs_hbm.at[s], pad_sem).start()
                return carry
            lax.fori_loop(lo, mid, per_row, 0)

            def per_chunk(c, carry):
                pltpu.make_async_copy(zeros, xs_hbm.at[pl.ds(pl.multiple_of(c * ZERO_ROWS, ZERO_ROWS), ZERO_ROWS)],
                                      pad_sem).start()
                return carry
            lax.fori_loop(mid // ZERO_ROWS, hi // ZERO_ROWS, per_chunk, 0)
            return totals[0] + (mid - lo), totals[1] + (hi - mid) // ZERO_ROWS
        n_rows_filled, n_chunks_filled = lax.fori_loop(0, N_EXPERTS, per_expert, (0, 0))

        def drain_row(s, carry):
            pltpu.make_async_copy(zeros.at[0], xs_hbm.at[0], pad_sem).wait()
            return carry
        lax.fori_loop(0, n_rows_filled, drain_row, 0)

        def drain_chunk(c, carry):
            pltpu.make_async_copy(zeros, xs_hbm.at[pl.ds(0, ZERO_ROWS)], pad_sem).wait()
            return carry
        lax.fori_loop(0, n_chunks_filled, drain_chunk, 0)

    for n in range(n_rows):
        pltpu.make_async_copy(hf_ref.at[pl.ds((n % tm) * LANE_CHUNKS, LANE_CHUNKS), :],
                              xs_hbm.at[dest_ref[0, n]], row_sem).start(priority=n % 2)
    pltpu.make_async_copy(xs_hbm.at[pl.ds(0, n_rows)], xs_hbm.at[pl.ds(0, n_rows)], row_sem).wait()


def _dispatch(fill_lo, fill_hi, dest_tiles, hf_rows, n_slots, tm):
    return pl.pallas_call(
        functools.partial(_dispatch_kernel, tm=tm),
        out_shape=jax.ShapeDtypeStruct((n_slots, LANE_CHUNKS, LANES), F32),
        grid_spec=pltpu.PrefetchScalarGridSpec(
            num_scalar_prefetch=2, grid=(dest_tiles.shape[0],),
            in_specs=[pl.BlockSpec((None, 1, TOP_K * tm), lambda i, lo, hi: (i, 0, 0), memory_space=pltpu.SMEM),
                      pl.BlockSpec((tm * LANE_CHUNKS, LANES), lambda i, lo, hi: (i, 0))],
            out_specs=pl.BlockSpec(memory_space=pl.ANY),
            scratch_shapes=[pltpu.VMEM((ZERO_ROWS, LANE_CHUNKS, LANES), F32),
                            pltpu.SemaphoreType.DMA, pltpu.SemaphoreType.DMA]),
        compiler_params=_cparams(("arbitrary",)),
    )(fill_lo, fill_hi, dest_tiles, hf_rows)


def _expert_kernel(be_ref, nxt_ref, nu_ref, xs_ref, wgu_hbm, bgu_ref, wd_hbm, bd_ref, ys_ref,
                   wgu_f32, wd_f32, wgu_bf, wd_bf, w_sem):
    i = pl.program_id(0)
    n_used = nu_ref[0]

    def weight_copies(e):
        return (pltpu.make_async_copy(wgu_hbm.at[e], wgu_f32, w_sem.at[0]),
                pltpu.make_async_copy(wd_hbm.at[e], wd_f32, w_sem.at[1]))

    @pl.when(i == 0)
    def _():
        for cp in weight_copies(be_ref[0]):
            cp.start()

    @pl.when(i < n_used)
    def _():
        prev = be_ref[jnp.maximum(i - 1, 0)]

        @pl.when((i == 0) | (be_ref[i] != prev))
        def _():
            for cp in weight_copies(be_ref[i]):
                cp.wait()
            wgu_bf[...] = wgu_f32[...].astype(BF16)
            wd_bf[...] = wd_f32[...].astype(BF16)
            nxt = nxt_ref[be_ref[i]]

            @pl.when(nxt >= 0)
            def _():
                for cp in weight_copies(nxt):
                    cp.start()

        x = _from_row_tiles(xs_ref, 0, MOE_BM).astype(BF16)
        hu = jnp.dot(x, wgu_bf[...], preferred_element_type=F32) + bgu_ref[...]
        glu = jnp.minimum(hu[:, :D_EXPERT], SWIGLU_LIMIT)
        lin = jnp.clip(hu[:, D_EXPERT:], -SWIGLU_LIMIT, SWIGLU_LIMIT)
        act = glu * jax.nn.sigmoid(SWIGLU_ALPHA * glu) * (lin + 1.0)
        y = jnp.dot(act.astype(BF16), wd_bf[...], preferred_element_type=F32) + bd_ref[...]
        _to_row_tiles(ys_ref, y, MOE_BM)

    @pl.when(i >= n_used)
    def _():
        ys_ref[...] = jnp.zeros_like(ys_ref)


def _experts(block_e, next_e, n_used, xs2d, w_gu, b_gu, w_down, b_down):
    n_blocks = block_e.shape[0]
    rows = MOE_BM * LANE_CHUNKS
    live = lambda i, be, nx, nu: (jnp.minimum(i, nu[0] - 1), 0)
    return pl.pallas_call(
        _expert_kernel,
        out_shape=jax.ShapeDtypeStruct(xs2d.shape, F32),
        grid_spec=pltpu.PrefetchScalarGridSpec(
            num_scalar_prefetch=3,
            grid=(n_blocks,),
            in_specs=[pl.BlockSpec((rows, LANES), live),
                      pl.BlockSpec(memory_space=pl.ANY),
                      pl.BlockSpec((None, 1, 2 * D_EXPERT), lambda i, be, nx, nu: (be[i], 0, 0)),
                      pl.BlockSpec(memory_space=pl.ANY),
                      pl.BlockSpec((None, 1, D_MODEL), lambda i, be, nx, nu: (be[i], 0, 0))],
            out_specs=pl.BlockSpec((rows, LANES), lambda i, be, nx, nu: (i, 0)),
            scratch_shapes=[pltpu.VMEM((D_MODEL, 2 * D_EXPERT), F32),
                            pltpu.VMEM((D_EXPERT, D_MODEL), F32),
                            pltpu.VMEM((D_MODEL, 2 * D_EXPERT), BF16),
                            pltpu.VMEM((D_EXPERT, D_MODEL), BF16),
                            pltpu.SemaphoreType.DMA((2,))]),
        compiler_params=_cparams(("arbitrary",)),
    )(block_e, next_e, n_used, xs2d, w_gu, b_gu, w_down, b_down)


def _combine_kernel(dcur_ref, dnext_ref, ys_hbm, g_ref, x1_ref, gt_ref, np_ref, o_ref, ybuf, row_sem, *, tm):
    i = pl.program_id(0)
    n_rows = TOP_K * tm
    slot = i % 2

    def slot_rows(s):
        return ybuf.at[pl.ds(pl.multiple_of(s * n_rows * LANE_CHUNKS, n_rows * LANE_CHUNKS), n_rows * LANE_CHUNKS), :]

    def issue(d_ref, s):
        dst = slot_rows(s)
        for n in range(n_rows):
            pltpu.make_async_copy(ys_hbm.at[d_ref[0, n]], dst.at[pl.ds(n * LANE_CHUNKS, LANE_CHUNKS), :],
                                  row_sem.at[s]).start(priority=n % 2)

    @pl.when(i == 0)
    def _():
        issue(dcur_ref, 0)

    @pl.when(i + 1 < pl.num_programs(0))
    def _():
        issue(dnext_ref, 1 - slot)

    pltpu.make_async_copy(slot_rows(slot), slot_rows(slot), row_sem.at[slot]).wait()
    g = g_ref[...]
    y = None
    for kk in range(TOP_K):
        rows = _from_row_tiles(ybuf, (slot * n_rows + kk * tm) * LANE_CHUNKS, tm)
        y = g[:, kk:kk + 1] * rows if y is None else y + g[:, kk:kk + 1] * rows
    o_ref[...] = x1_ref[...] + gt_ref[...] * _rms(y, np_ref[...])


def _combine(dest_tiles, ys_tiles, gates_t, x1, gt, npost, tm, tiles_per_seq, tile0):
    n_tiles = dest_tiles.shape[0]
    t = n_tiles * tm
    smem_tile = lambda f: pl.BlockSpec((None, 1, TOP_K * tm), f, memory_space=pltpu.SMEM)
    return pl.pallas_call(
        functools.partial(_combine_kernel, tm=tm),
        out_shape=jax.ShapeDtypeStruct((t, D_MODEL), F32),
        grid=(n_tiles,),
        in_specs=[smem_tile(lambda i: (i, 0, 0)),
                  smem_tile(lambda i: (jnp.minimum(i + 1, n_tiles - 1), 0, 0)),
                  pl.BlockSpec(memory_space=pl.ANY),
                  pl.BlockSpec((tm, TOP_K), lambda i: (i + tile0, 0)),
                  pl.BlockSpec((tm, D_MODEL), lambda i: (i + tile0, 0)),
                  _mod_spec(gt, tm, tiles_per_seq),
                  pl.BlockSpec((1, D_MODEL), lambda i: (0, 0))],
        out_specs=pl.BlockSpec((tm, D_MODEL), lambda i: (i, 0)),
        scratch_shapes=[pltpu.VMEM((2 * TOP_K * tm * LANE_CHUNKS, LANES), F32),
                        pltpu.SemaphoreType.DMA((2,))],
        compiler_params=_cparams(("arbitrary",)),
    )(dest_tiles, dest_tiles, ys_tiles, gates_t, x1, gt, npost)


def _expert_layout(counts, n_blocks):
    padded = (counts + MOE_BM - 1) // MOE_BM * MOE_BM
    pad_ends = jnp.cumsum(padded)
    pad_starts = pad_ends - padded
    block_start = jnp.arange(n_blocks, dtype=I32) * MOE_BM
    block_e = jnp.minimum(jnp.sum((pad_ends[None, :] <= block_start[:, None]).astype(I32), axis=1), N_EXPERTS - 1)
    n_used = (pad_ends[-1:] // MOE_BM).astype(I32)
    e_ids = jnp.arange(N_EXPERTS, dtype=I32)
    later = (counts > 0)[None, :] & (e_ids[None, :] > e_ids[:, None])
    next_e = jnp.min(jnp.where(later, e_ids[None, :], N_EXPERTS), axis=1)
    next_e = jnp.where(next_e < N_EXPERTS, next_e, -1).astype(I32)
    return pad_starts, pad_starts + counts, pad_ends, block_e, next_e, n_used


def _slots(idx, rank, pad_starts):
    onehot = idx[:, :, None] == jnp.arange(N_EXPERTS, dtype=I32)[None, None, :]
    return rank + jnp.sum(jnp.where(onehot, pad_starts[None, None, :], 0), axis=-1)


def _tile_major(a, tm):
    t = a.shape[1]
    return a.reshape(TOP_K, t // tm, tm).transpose(1, 0, 2).reshape(t // tm, 1, TOP_K * tm)


def _block_diag(wb):
    n, d, _ = wb.shape
    eye = jnp.eye(n, dtype=wb.dtype)
    return (wb[:, :, None, :] * eye[:, None, :, None]).reshape(n * d, n * d)


def _pick_tile(n, prefs):
    for p in prefs:
        if n % p == 0:
            return p
    raise ValueError(f"no tile for {n}")


def kernel(x_prompt, x_sample, state_rglru_conv, state_rglru_h, state_hgrn_S, c_prompt, c_sample, ada_w, ada_b, norm_pre_mix, norm_post_mix, norm_pre_ffn, norm_post_ffn, w_in, conv_w, conv_b, lru_wa, lru_ba, lru_wi, lru_bi, lru_lambda, hg_lb, hg_norm, w_out, router_w, router_b, w_gu, b_gu, w_down, b_down):
    assert ada_w.shape[0] == 1, "single-layer trunk"
    bp, lp, _ = x_prompt.shape
    bs, ls, _ = x_sample.shape
    tp, ts = bp * lp, bs * ls
    assert CONV_WIDTH - 1 <= ls <= SUBLANES

    w_in_bf = w_in[0].astype(BF16)
    w_out_bf = w_out[0].astype(BF16)
    wg = jnp.concatenate([_block_diag(lru_wa[0]), _block_diag(lru_wi[0])], axis=1).astype(BF16)
    bg = jnp.concatenate([lru_ba[0], lru_bi[0]])[None, :]
    lbs = jnp.cumsum(jax.nn.softmax(hg_lb.astype(F32), axis=0), axis=0)[0][None, :]
    mw = (conv_w[0], conv_b[0][None, :], wg, bg, lru_lambda[0][None, :], lbs, hg_norm[0][None, :])
    row = lambda p: p[0][None, :]

    n_c = bp + bs
    n_c_pad = -(-n_c // SUBLANES) * SUBLANES
    c_all = jnp.concatenate([c_prompt, c_sample, jnp.zeros((n_c_pad - n_c, D_MODEL), F32)], axis=0)
    mod = _ada(c_all, ada_w[0], ada_b[0][None, :])
    mod_p = [m[:, None, :] for m in jnp.split(mod[:bp], 6, axis=-1)]
    mod_s = [jnp.repeat(m, ls, axis=0) for m in jnp.split(mod[bp:n_c], 6, axis=-1)]

    xp = x_prompt.reshape(tp, D_MODEL)
    xs = x_sample.reshape(ts, D_MODEL)
    tm_s = _pick_tile(ts, (256, 128, 64, 32, 16, 8))

    tc = _pick_tile(lp, (256, 128, 64))
    m_p, conv_p, h_p, s_p = _mixer_prompt(xp, mod_p[1], mod_p[0], row(norm_pre_mix), w_in_bf, bp, lp, mw, tc)
    proj_s = _inproj(xs, mod_s[1], mod_s[0], row(norm_pre_mix), w_in_bf, tm_s, 1)

    proj_s_pad = jnp.pad(proj_s.reshape(bs, ls, IN_COLS), ((0, 0), (0, SUBLANES - ls), (0, 0)))
    cst_hdr = jnp.pad(state_rglru_conv[0], ((0, 0), (CONV_HDR - (CONV_WIDTH - 1), 0), (0, 0)))
    m_s_pad, conv_s, h_s, s_s = _mixer_sample(proj_s_pad, cst_hdr, state_rglru_h[0][:, None, :],
                                              state_hgrn_S[0], mw, ls)
    m_s = m_s_pad[:, :ls, :].reshape(ts, D_MODEL)

    tmq = _pick_tile(math.gcd(lp, ts), (512, 256, 128, 64, 32, 16, 8))
    x1, hf, idx, gate, rank, counts = _post(
        m_p, m_s, xp, xs, (mod_p[2], mod_p[4], mod_p[3]), (mod_s[2], mod_s[4], mod_s[3]),
        row(norm_post_mix), row(norm_pre_ffn), w_out_bf, router_w[0].T, router_b[0][:, None], tmq, lp // tmq)

    n_blocks = -(-(TOP_K * (tp + ts)) // MOE_BM) + N_EXPERTS
    n_slots = n_blocks * MOE_BM
    pad_starts, fill_lo, pad_ends, block_e, next_e, n_used = _expert_layout(counts[:, 0].astype(I32), n_blocks)
    fill_hi = pad_ends.at[N_EXPERTS - 1].set(n_slots)
    dest = _slots(idx, rank, pad_starts)
    tmd = _pick_tile(math.gcd(lp, ts), (128, 64, 32, 16, 8))
    dtiles = _tile_major(dest, tmd)
    xs_rows = _dispatch(fill_lo, fill_hi, dtiles, hf, n_slots, tmd)
    ys = _experts(block_e, next_e, n_used, xs_rows.reshape(n_slots * LANE_CHUNKS, LANES),
                  w_gu[0], b_gu[0][:, None, :], w_down[0], b_down[0][:, None, :])
    ys_tiles = ys.reshape(-1, LANE_CHUNKS, LANES)

    gates_t = gate.T
    y_p = _combine(dtiles[:tp // tmd], ys_tiles, gates_t, x1, mod_p[5], row(norm_post_ffn), tmd, lp // tmd, 0)
    y_s = _combine(dtiles[tp // tmd:], ys_tiles, gates_t, x1, mod_s[5], row(norm_post_ffn), tmd, 1, tp // tmd)

    return (y_p.reshape(bp, lp, D_MODEL), y_s.reshape(bs, ls, D_MODEL),
            conv_p[None], h_p.reshape(1, bp, LRU_WIDTH), s_p[None],
            conv_s[None], h_s.reshape(1, bs, LRU_WIDTH), s_s[None])
```

```python
import functools
import math

import jax
import jax.numpy as jnp
from jax import lax
from jax.experimental import pallas as pl
from jax.experimental.pallas import tpu as pltpu

F32 = jnp.float32
BF16 = jnp.bfloat16
I32 = jnp.int32

D_MODEL = 1024
LRU_WIDTH = 512
LRU_BLOCKS = 8
LRU_BLOCK = LRU_WIDTH // LRU_BLOCKS
CONV_WIDTH = 4
LRU_C = 8.0
HG_WIDTH = 512
HG_HEAD_DIM = 128
HG_HEADS = 4
IN_COLS = 2 * LRU_WIDTH + 4 * HG_WIDTH
N_EXPERTS = 32
TOP_K = 4
D_EXPERT = 1024
SWIGLU_LIMIT = 7.0
SWIGLU_ALPHA = 1.702
RMS_EPS = 1e-6

C_XL, C_YL, C_Q, C_F, C_V, C_G = 0, 512, 1024, 1536, 2048, 2560

SUBLANES = 8
LANES = 128
LANE_CHUNKS = D_MODEL // LANES
CONV_HDR = SUBLANES
HG_CHUNK = 64
HG_MAX_EXPONENT = 60.0
MOE_BM = 512
ZERO_ROWS = 64
VMEM_LIMIT = 56 * 1024 * 1024

_NT = (((1,), (1,)), ((), ()))
_TN = (((0,), (0,)), ((), ()))


def _cparams(sem):
    return pltpu.CompilerParams(dimension_semantics=sem, vmem_limit_bytes=VMEM_LIMIT)


def _rms(x, g):
    return x * lax.rsqrt(jnp.mean(x * x, axis=-1, keepdims=True) + RMS_EPS) * g


def _gelu_tanh(x):
    c = math.sqrt(2.0 / math.pi)
    return 0.5 * x * (1.0 + jnp.tanh(c * (x + 0.044715 * (x * x * x))))


def _ada_kernel(c_ref, w_ref, b_ref, o_ref):
    c = c_ref[...]
    s = c * jax.nn.sigmoid(c)
    o_ref[...] = jnp.dot(s.astype(BF16), w_ref[...].astype(BF16), preferred_element_type=F32) + b_ref[...]


def _ada(c_all, ada_w, ada_b):
    n = c_all.shape[0]
    tn = 1024
    return pl.pallas_call(
        _ada_kernel,
        out_shape=jax.ShapeDtypeStruct((n, 6 * D_MODEL), F32),
        grid=(6 * D_MODEL // tn,),
        in_specs=[pl.BlockSpec((n, D_MODEL), lambda j: (0, 0)),
                  pl.BlockSpec((D_MODEL, tn), lambda j: (0, j)),
                  pl.BlockSpec((1, tn), lambda j: (0, j))],
        out_specs=pl.BlockSpec((n, tn), lambda j: (0, j)),
        compiler_params=_cparams(("arbitrary",)),
    )(c_all, ada_w, ada_b)


def _inproj_kernel(x_ref, sc_ref, sh_ref, g_ref, w_ref, o_ref):
    h = _rms(x_ref[...], g_ref[...]) * (1.0 + sc_ref[...]) + sh_ref[...]
    o_ref[...] = jnp.dot(h.astype(BF16), w_ref[...], preferred_element_type=F32)


def _mod_spec(mod, tm, tiles_per_seq):
    if mod.ndim == 3:
        return pl.BlockSpec((None, 1, D_MODEL), lambda i: (i // tiles_per_seq, 0, 0))
    return pl.BlockSpec((tm, D_MODEL), lambda i: (i, 0))


def _inproj(x, sc, sh, g, w_bf, tm, tiles_per_seq):
    t = x.shape[0]
    return pl.pallas_call(
        _inproj_kernel,
        out_shape=jax.ShapeDtypeStruct((t, IN_COLS), F32),
        grid=(t // tm,),
        in_specs=[pl.BlockSpec((tm, D_MODEL), lambda i: (i, 0)),
                  _mod_spec(sc, tm, tiles_per_seq), _mod_spec(sh, tm, tiles_per_seq),
                  pl.BlockSpec((1, D_MODEL), lambda i: (0, 0)),
                  pl.BlockSpec((D_MODEL, IN_COLS), lambda i: (0, 0))],
        out_specs=pl.BlockSpec((tm, IN_COLS), lambda i: (i, 0)),
        compiler_params=_cparams(("arbitrary",)),
    )(x, sc, sh, g, w_bf)


def _sigmoid(x):
    return 0.5 * jnp.tanh(0.5 * x) + 0.5


def _group_rows(x):
    rows, w = x.shape
    return x.reshape(rows // SUBLANES, SUBLANES, w)


def _scan_rows(a, u, h0):
    a3, u3 = _group_rows(a), _group_rows(u)
    r3 = lax.broadcasted_iota(I32, a3.shape, 1)
    s = 1
    while s < SUBLANES:
        keep = r3 >= s
        a_sh = jnp.where(keep, pltpu.roll(a3, s, 1), 1.0)
        u_sh = jnp.where(keep, pltpu.roll(u3, s, 1), 0.0)
        u3 = a3 * u_sh + u3
        a3 = a3 * a_sh
        s *= 2
    out, h = [], h0
    for i in range(a3.shape[0]):
        hg = u3[i] + a3[i] * h
        h = hg[SUBLANES - 1:SUBLANES, :]
        out.append(hg)
    return jnp.concatenate(out, axis=0), h


def _chunk_cumsum(x, chunk):
    x3 = _group_rows(x)
    r3 = lax.broadcasted_iota(I32, x3.shape, 1)
    s = 1
    while s < SUBLANES:
        x3 = x3 + jnp.where(r3 >= s, pltpu.roll(x3, s, 1), 0.0)
        s *= 2
    out, carry = [], None
    for i in range(x3.shape[0]):
        cur = x3[i] if i % (chunk // SUBLANES) == 0 else x3[i] + carry
        carry = cur[SUBLANES - 1:SUBLANES, :]
        out.append(cur)
    return jnp.concatenate(out, axis=0)


def _lru_gates(xc, wg_ref, bg_ref, lam_ref):
    gates = jnp.dot(xc.astype(BF16), wg_ref[...], preferred_element_type=F32) + bg_ref[...]
    r = _sigmoid(gates[:, :LRU_WIDTH])
    ig = _sigmoid(gates[:, LRU_WIDTH:])
    z = -lam_ref[...]
    softplus = jnp.maximum(z, 0.0) + jnp.log1p(jnp.exp(-jnp.abs(z)))
    log_a = -LRU_C * r * softplus
    a = jnp.exp(log_a)
    th = jnp.tanh(log_a)
    mult2 = -2.0 * th / (1.0 - th)
    return a, jnp.sqrt(mult2), ig


def _conv4(xbuf, xl, cw_ref, cb_ref, rows):
    cw = cw_ref[...]
    h = CONV_HDR
    return (cb_ref[...] + cw[0:1] * xbuf[h - 3:h - 3 + rows, :] + cw[1:2] * xbuf[h - 2:h - 2 + rows, :]
            + cw[2:3] * xbuf[h - 1:h - 1 + rows, :] + cw[3:4] * xl)


def _conv4_rolled(xl, prev, cw_ref, cb_ref):
    cw = cw_ref[...]
    row8 = lax.broadcasted_iota(I32, prev.shape, 0)
    acc = cb_ref[...] + cw[CONV_WIDTH - 1:CONV_WIDTH] * xl
    for s in range(1, CONV_WIDTH):
        sh = pltpu.roll(xl, s, 0)
        top = jnp.where(row8 < s, pltpu.roll(prev, s, 0), sh[0:SUBLANES, :])
        sh = jnp.concatenate([top, sh[SUBLANES:, :]], axis=0)
        acc = acc + cw[CONV_WIDTH - 1 - s:CONV_WIDTH - s] * sh
    return acc


def _head_norm_gate(o, gate, hgn):
    return _rms(o, hgn) * _sigmoid(gate)


def _hgrn_window(q, k, v, b, s_t, steps):
    row = lax.broadcasted_iota(I32, q.shape, 0)
    valid = row < steps
    b_end = b[steps - 1:steps, :]
    q_in = jnp.where(valid, q * jnp.exp(b), 0.0)
    o = lax.dot_general(q_in.astype(BF16), s_t.astype(BF16), _NT, preferred_element_type=F32)
    for d in range(steps):
        ok = valid & (row >= d)
        k_d = k if d == 0 else pltpu.roll(k, d, 0)
        v_d = v if d == 0 else pltpu.roll(v, d, 0)
        b_d = b if d == 0 else pltpu.roll(b, d, 0)
        decay = jnp.exp(jnp.where(ok, b - b_d, 0.0))
        wgt = jnp.sum(jnp.where(ok, q * k_d * decay, 0.0), axis=-1, keepdims=True)
        o = o + wgt * v_d
    k_out = jnp.where(valid, k * jnp.exp(jnp.where(valid, b_end - b, 0.0)), 0.0)
    v_ok = jnp.where(valid, v, 0.0)
    s_new = s_t * jnp.exp(b_end) + lax.dot_general(v_ok.astype(BF16), k_out.astype(BF16), _TN,
                                                   preferred_element_type=F32)
    return o, s_new


def _tile_cumsum(x):
    row = lax.broadcasted_iota(I32, x.shape, 0)
    s = 1
    while s < SUBLANES:
        x = x + jnp.where(row >= s, pltpu.roll(x, s, 0), 0.0)
        s *= 2
    return x


def _mix_tile(proj_ref, cw_ref, cb_ref, wg_ref, bg_ref, lam_ref, lb_ref, hgn_ref, m_ref, xprev, hcar, st, st0, obuf,
              seq_start, tc, side_work):
    w = LRU_WIDTH
    side_work = iter(side_work)

    next(side_work)()
    xl = proj_ref[:, C_XL:C_XL + w]
    xc = _conv4_rolled(xl, xprev[...], cw_ref, cb_ref)
    xprev[...] = xl[tc - SUBLANES:tc, :]
    a, mult, ig = _lru_gates(xc, wg_ref, bg_ref, lam_ref)
    row = lax.broadcasted_iota(I32, (tc, w), 0)
    if seq_start is not None:
        mult = jnp.where((row == 0) & seq_start, 1.0, mult)
    h, h_last = _scan_rows(a, mult * ig * xc, hcar[0:1, :])
    hcar[0:1, :] = h_last
    m_ref[:, 0:w] = (h * _gelu_tanh(proj_ref[:, C_YL:C_YL + w])).astype(BF16)

    next(side_work)()
    lb = lb_ref[...]
    f = lb + (1.0 - lb) * _sigmoid(proj_ref[:, C_F:C_F + w])
    k = 1.0 - f
    b = _chunk_cumsum(jnp.log(f), HG_CHUNK)
    q = proj_ref[:, C_Q:C_Q + w]
    v = proj_ref[:, C_V:C_V + w]
    mid = HG_CHUNK // 2
    st0[...] = st[...]
    tri = (lax.broadcasted_iota(I32, (HG_CHUNK, HG_CHUNK), 0)
           >= lax.broadcasted_iota(I32, (HG_CHUNK, HG_CHUNK), 1))
    states = [st[hd] for hd in range(HG_HEADS)]
    for c in range(tc // HG_CHUNK):
        next(side_work)()
        rs = slice(c * HG_CHUNK, (c + 1) * HG_CHUNK)
        for hd in range(HG_HEADS):
            cs = slice(hd * HG_HEAD_DIM, (hd + 1) * HG_HEAD_DIM)
            s_t = states[hd]
            bc, qc, kc = b[rs, cs], q[rs, cs], k[rs, cs]
            vb = v[rs, cs].astype(BF16)
            b_mid = bc[mid - 1:mid, :]
            b_end = bc[HG_CHUNK - 1:HG_CHUNK, :]
            qp = (qc * jnp.exp(bc - b_mid)).astype(BF16)
            kp = (kc * jnp.exp(b_mid - bc)).astype(BF16)
            att = lax.dot_general(qp, kp, _NT, preferred_element_type=F32)
            att = jnp.where(tri, att, 0.0)
            q_in = (qc * jnp.exp(bc)).astype(BF16)
            o = (jnp.dot(att.astype(BF16), vb, preferred_element_type=F32)
                 + lax.dot_general(q_in, s_t.astype(BF16), _NT, preferred_element_type=F32))
            k_out = (kc * jnp.exp(b_end - bc)).astype(BF16)
            states[hd] = s_t * jnp.exp(b_end) + lax.dot_general(vb, k_out, _TN, preferred_element_type=F32)
            obuf[rs, cs] = o
    for hd in range(HG_HEADS):
        st[hd] = states[hd]

    half_decay = None
    for c in range(tc // HG_CHUNK):
        b_mid = b[c * HG_CHUNK + mid - 1:c * HG_CHUNK + mid, :]
        b_end = b[(c + 1) * HG_CHUNK - 1:(c + 1) * HG_CHUNK, :]
        worst = jnp.maximum(-b_mid, b_mid - b_end)
        half_decay = worst if half_decay is None else jnp.maximum(half_decay, worst)

    @pl.when(jnp.logical_not(jnp.max(half_decay) < HG_MAX_EXPONENT))
    def _():
        st[...] = st0[...]

        def window(gi, carry):
            rows = pl.ds(pl.multiple_of(gi * SUBLANES, SUBLANES), SUBLANES)
            f8 = lb + (1.0 - lb) * _sigmoid(proj_ref[rows, C_F:C_F + w])
            b8 = _tile_cumsum(jnp.log(f8))
            k8 = 1.0 - f8
            q8 = proj_ref[rows, C_Q:C_Q + w]
            v8 = proj_ref[rows, C_V:C_V + w]
            for hd in range(HG_HEADS):
                cs = slice(hd * HG_HEAD_DIM, (hd + 1) * HG_HEAD_DIM)
                o, s_new = _hgrn_window(q8[:, cs], k8[:, cs], v8[:, cs], b8[:, cs], st[hd], SUBLANES)
                st[hd] = s_new
                obuf[rows, cs] = o
            return carry
        lax.fori_loop(0, tc // SUBLANES, window, 0)

    hgn = hgn_ref[...]
    for hd in range(HG_HEADS):
        cs = slice(hd * HG_HEAD_DIM, (hd + 1) * HG_HEAD_DIM)
        gate = proj_ref[:, C_G + hd * HG_HEAD_DIM:C_G + (hd + 1) * HG_HEAD_DIM]
        m_ref[:, w + hd * HG_HEAD_DIM:w + (hd + 1) * HG_HEAD_DIM] = (
            _head_norm_gate(obuf[:, cs], gate, hgn).astype(BF16))


def _mixer_prompt_kernel(x_ref, xn_ref, sc_ref, sh_ref, scn_ref, shn_ref, g_ref, win_ref, cw_ref, cb_ref, wg_ref,
                         bg_ref, lam_ref, lb_ref, hgn_ref, m_ref, conv_ref, h_ref, s_ref,
                         proj_a, proj_b, hn_buf, xprev, hcar, st, st0, obuf, *, tc):
    j = pl.program_id(1)
    n_phases = 2 + tc // HG_CHUNK
    cols = IN_COLS // n_phases

    def project(x, sc, sh, out_ref):
        def chunk(c):
            if c == 0:
                hn_buf[...] = (_rms(x(), g_ref[...]) * (1.0 + sc[...]) + sh[...]).astype(BF16)
            out_ref[:, c * cols:(c + 1) * cols] = jnp.dot(hn_buf[...], win_ref[:, c * cols:(c + 1) * cols],
                                                          preferred_element_type=F32)
        return [functools.partial(chunk, c) for c in range(n_phases)]

    def mix(proj_ref, first_row, seq_start, side_work):
        _mix_tile(proj_ref, cw_ref, cb_ref, wg_ref, bg_ref, lam_ref, lb_ref, hgn_ref,
                  m_ref.at[pl.ds(first_row, tc), :], xprev, hcar, st, st0, obuf, seq_start, tc, side_work)

    @pl.when((pl.program_id(0) == 0) & (j == 0))
    def _():
        for thunk in project(lambda: x_ref[0:tc, :], sc_ref, sh_ref, proj_a):
            thunk()

    @pl.when(j == 0)
    def _():
        xprev[...] = jnp.zeros_like(xprev)
        hcar[...] = jnp.zeros_like(hcar)
        st[...] = jnp.zeros_like(st)

    mix(proj_a, 0, j == 0, project(lambda: x_ref[tc:2 * tc, :], sc_ref, sh_ref, proj_b))
    mix(proj_b, tc, None, project(lambda: xn_ref[...], scn_ref, shn_ref, proj_a))

    @pl.when(j == pl.num_programs(1) - 1)
    def _():
        conv_ref[...] = xprev[SUBLANES - (CONV_WIDTH - 1):SUBLANES, :]
        h_ref[...] = hcar[0:1, :]
        for hd in range(HG_HEADS):
            s_ref[hd] = st[hd].T


def _mixer_prompt(x, sc, sh, g, w_in_bf, nb, seq, mw, tc):
    cw, cb, wg, bg, lam, lb, hgn = mw
    nj = seq // (2 * tc)
    last_tile = nb * seq // tc - 1
    next_tile = lambda b, j: jnp.minimum(2 * (b * nj + j) + 2, last_tile)
    const = lambda shape: pl.BlockSpec(shape, lambda b, j: tuple(0 for _ in shape))
    per_seq = lambda: pl.BlockSpec((None, 1, D_MODEL), lambda b, j: (b, 0, 0))
    next_seq = lambda: pl.BlockSpec((None, 1, D_MODEL), lambda b, j: (next_tile(b, j) // (2 * nj), 0, 0))
    return pl.pallas_call(
        functools.partial(_mixer_prompt_kernel, tc=tc),
        out_shape=(jax.ShapeDtypeStruct((nb * seq, D_MODEL), BF16),
                   jax.ShapeDtypeStruct((nb, CONV_WIDTH - 1, LRU_WIDTH), F32),
                   jax.ShapeDtypeStruct((nb, 1, LRU_WIDTH), F32),
                   jax.ShapeDtypeStruct((nb, HG_HEADS, HG_HEAD_DIM, HG_HEAD_DIM), F32)),
        grid=(nb, nj),
        in_specs=[pl.BlockSpec((2 * tc, D_MODEL), lambda b, j: (b * nj + j, 0)),
                  pl.BlockSpec((tc, D_MODEL), lambda b, j: (next_tile(b, j), 0)),
                  per_seq(), per_seq(), next_seq(), next_seq(), const((1, D_MODEL)), const((D_MODEL, IN_COLS)),
                  const((CONV_WIDTH, LRU_WIDTH)), const((1, LRU_WIDTH)),
                  const((LRU_WIDTH, 2 * LRU_WIDTH)), const((1, 2 * LRU_WIDTH)),
                  const((1, LRU_WIDTH)), const((1, HG_WIDTH)), const((1, HG_HEAD_DIM))],
        out_specs=(pl.BlockSpec((2 * tc, D_MODEL), lambda b, j: (b * nj + j, 0)),
                   pl.BlockSpec((None, CONV_WIDTH - 1, LRU_WIDTH), lambda b, j: (b, 0, 0)),
                   pl.BlockSpec((None, 1, LRU_WIDTH), lambda b, j: (b, 0, 0)),
                   pl.BlockSpec((None, HG_HEADS, HG_HEAD_DIM, HG_HEAD_DIM), lambda b, j: (b, 0, 0, 0))),
        scratch_shapes=[pltpu.VMEM((tc, IN_COLS), F32),
                        pltpu.VMEM((tc, IN_COLS), F32),
                        pltpu.VMEM((tc, D_MODEL), BF16),
                        pltpu.VMEM((SUBLANES, LRU_WIDTH), F32),
                        pltpu.VMEM((SUBLANES, LRU_WIDTH), F32),
                        pltpu.VMEM((HG_HEADS, HG_HEAD_DIM, HG_HEAD_DIM), F32),
                        pltpu.VMEM((HG_HEADS, HG_HEAD_DIM, HG_HEAD_DIM), F32),
                        pltpu.VMEM((tc, HG_WIDTH), F32)],
        compiler_params=_cparams(("arbitrary", "arbitrary")),
    )(x, x, sc, sh, sc, sh, g, w_in_bf, cw, cb, wg, bg, lam, lb, hgn)


def _mixer_sample_kernel(proj_ref, cst_ref, h0_ref, s0_ref, cw_ref, cb_ref, wg_ref, bg_ref, lam_ref, lb_ref,
                         hgn_ref, m_ref, conv_ref, h_ref, s_ref, xbuf, *, steps):
    for bi in range(proj_ref.shape[0]):
        _mixer_sample_one(proj_ref.at[bi], cst_ref.at[bi], h0_ref.at[bi], s0_ref.at[bi], cw_ref, cb_ref, wg_ref,
                          bg_ref, lam_ref, lb_ref, hgn_ref, m_ref.at[bi], conv_ref.at[bi], h_ref.at[bi],
                          s_ref.at[bi], xbuf.at[bi], steps)


def _mixer_sample_one(proj_ref, cst_ref, h0_ref, s0_ref, cw_ref, cb_ref, wg_ref, bg_ref, lam_ref, lb_ref,
                      hgn_ref, m_ref, conv_ref, h_ref, s_ref, xbuf, steps):
    w = LRU_WIDTH
    rows = SUBLANES
    row = lax.broadcasted_iota(I32, (rows, w), 0)
    valid = row < steps

    xl = proj_ref[:, C_XL:C_XL + w]
    xbuf[0:CONV_HDR, :] = cst_ref[...]
    xbuf[CONV_HDR:CONV_HDR + rows, :] = xl
    xc = _conv4(xbuf, xl, cw_ref, cb_ref, rows)
    conv_ref[...] = xbuf[CONV_HDR + steps - 3:CONV_HDR + steps, :]
    a, mult, ig = _lru_gates(xc, wg_ref, bg_ref, lam_ref)
    u = mult * ig * xc
    h = h0_ref[...]
    hs = jnp.zeros((rows, w), F32)
    for t in range(steps):
        h = a[t:t + 1, :] * h + u[t:t + 1, :]
        hs = jnp.where(row == t, h, hs)
    h_ref[...] = h
    m_ref[:, 0:w] = (hs * _gelu_tanh(proj_ref[:, C_YL:C_YL + w])).astype(BF16)

    lb = lb_ref[...]
    f = lb + (1.0 - lb) * _sigmoid(proj_ref[:, C_F:C_F + w])
    k = 1.0 - f
    b = _tile_cumsum(jnp.log(f))
    q = proj_ref[:, C_Q:C_Q + w]
    v = proj_ref[:, C_V:C_V + w]
    hgn = hgn_ref[...]
    for hd in range(HG_HEADS):
        cs = slice(hd * HG_HEAD_DIM, (hd + 1) * HG_HEAD_DIM)
        o, s_new = _hgrn_window(q[:, cs], k[:, cs], v[:, cs], b[:, cs], s0_ref[hd].T, steps)
        s_ref[hd] = s_new.T
        gate = proj_ref[:, C_G + hd * HG_HEAD_DIM:C_G + (hd + 1) * HG_HEAD_DIM]
        m_ref[:, w + hd * HG_HEAD_DIM:w + (hd + 1) * HG_HEAD_DIM] = _head_norm_gate(o, gate, hgn).astype(BF16)


def _mixer_sample(proj_pad, cst_hdr, h0, s0, mw, steps):
    cw, cb, wg, bg, lam, lb, hgn = mw
    nb = proj_pad.shape[0]
    per_step = _pick_tile(nb, (4, 2, 1))
    const = lambda shape: pl.BlockSpec(shape, lambda b: tuple(0 for _ in shape))
    per_b = lambda shape: pl.BlockSpec((per_step,) + shape, lambda b: (b,) + tuple(0 for _ in shape))
    return pl.pallas_call(
        functools.partial(_mixer_sample_kernel, steps=steps),
        out_shape=(jax.ShapeDtypeStruct((nb, SUBLANES, D_MODEL), BF16),
                   jax.ShapeDtypeStruct((nb, CONV_WIDTH - 1, LRU_WIDTH), F32),
                   jax.ShapeDtypeStruct((nb, 1, LRU_WIDTH), F32),
                   jax.ShapeDtypeStruct((nb, HG_HEADS, HG_HEAD_DIM, HG_HEAD_DIM), F32)),
        grid=(nb // per_step,),
        in_specs=[per_b((SUBLANES, IN_COLS)), per_b((CONV_HDR, LRU_WIDTH)), per_b((1, LRU_WIDTH)),
                  per_b((HG_HEADS, HG_HEAD_DIM, HG_HEAD_DIM)),
                  const((CONV_WIDTH, LRU_WIDTH)), const((1, LRU_WIDTH)),
                  const((LRU_WIDTH, 2 * LRU_WIDTH)), const((1, 2 * LRU_WIDTH)),
                  const((1, LRU_WIDTH)), const((1, HG_WIDTH)), const((1, HG_HEAD_DIM))],
        out_specs=(per_b((SUBLANES, D_MODEL)), per_b((CONV_WIDTH - 1, LRU_WIDTH)), per_b((1, LRU_WIDTH)),
                   per_b((HG_HEADS, HG_HEAD_DIM, HG_HEAD_DIM))),
        scratch_shapes=[pltpu.VMEM((per_step, CONV_HDR + SUBLANES, LRU_WIDTH), F32)],
        compiler_params=_cparams(("arbitrary",)),
    )(proj_pad, cst_hdr, h0, s0, cw, cb, wg, bg, lam, lb, hgn)


def _to_row_tiles(ref, val, rows):
    for c in range(LANE_CHUNKS):
        ref[pl.ds(c, rows, stride=LANE_CHUNKS), :] = val[:, c * LANES:(c + 1) * LANES]


def _from_row_tiles(ref, base, rows):
    return jnp.concatenate(
        [ref[pl.ds(base + c, rows, stride=LANE_CHUNKS), :] for c in range(LANE_CHUNKS)], axis=-1)


def _post_kernel(mp_ref, ms_ref, xp_ref, xs_ref, gtmp_ref, scfp_ref, shfp_ref, gtms_ref, scfs_ref, shfs_ref,
                 npost_ref, npre_ref, wout_ref, rwt_ref, rb_ref,
                 x1_ref, hf_ref, idx_ref, gate_ref, rank_ref, cnt_out_ref, cnt, *, n_p):
    i = pl.program_id(0)
    tm = xp_ref.shape[0]

    @pl.when(i == 0)
    def _():
        cnt[...] = jnp.zeros_like(cnt)

    def run(m_ref, x_ref, gtm_ref, scf_ref, shf_ref):
        mix = jnp.dot(m_ref[...], wout_ref[...], preferred_element_type=F32)
        x1 = x_ref[...] + gtm_ref[...] * _rms(mix, npost_ref[...])
        x1_ref[...] = x1
        hf = _rms(x1, npre_ref[...]) * (1.0 + scf_ref[...]) + shf_ref[...]
        _to_row_tiles(hf_ref, hf, tm)
        logits = lax.dot_general(rwt_ref[...], hf, _NT, precision=lax.Precision.HIGHEST,
                                 preferred_element_type=F32) + rb_ref[...]
        e_iota = lax.broadcasted_iota(I32, logits.shape, 0)
        vals, idxs = [], []
        for _ in range(TOP_K):
            mx = jnp.max(logits, axis=0, keepdims=True)
            ix = jnp.min(jnp.where(logits == mx, e_iota, N_EXPERTS), axis=0, keepdims=True)
            vals.append(mx)
            idxs.append(ix)
            logits = jnp.where(e_iota == ix, -jnp.inf, logits)
        exps = [jnp.exp(vv - vals[0]) for vv in vals]
        den = exps[0] + exps[1] + exps[2] + exps[3]
        earlier = (lax.broadcasted_iota(I32, (tm, tm), 0) < lax.broadcasted_iota(I32, (tm, tm), 1))
        earlier = jnp.where(earlier, 1.0, 0.0).astype(BF16)
        base = cnt[...]
        for kk in range(TOP_K):
            idx_ref[kk:kk + 1, :] = idxs[kk]
            gate_ref[kk:kk + 1, :] = exps[kk] / den
            hit = e_iota == idxs[kk]
            onehot = jnp.where(hit, 1.0, 0.0)
            before = jnp.dot(onehot.astype(BF16), earlier, preferred_element_type=F32)
            rank = jnp.sum(jnp.where(hit, before + base, 0.0), axis=0, keepdims=True)
            rank_ref[kk:kk + 1, :] = rank.astype(I32)
            base = base + jnp.sum(onehot, axis=1, keepdims=True)
        cnt[...] = base

    @pl.when(i < n_p)
    def _():
        run(mp_ref, xp_ref, gtmp_ref, scfp_ref, shfp_ref)

    @pl.when(i >= n_p)
    def _():
        run(ms_ref, xs_ref, gtms_ref, scfs_ref, shfs_ref)

    @pl.when(i == pl.num_programs(0) - 1)
    def _():
        cnt_out_ref[...] = cnt[...]


def _post(m_p, m_s, x_p, x_s, mod_p, mod_s, npost, npre, wout_bf, rwt, rb, tm, tiles_per_seq):
    n_p, n_s = x_p.shape[0] // tm, x_s.shape[0] // tm
    t = (n_p + n_s) * tm
    first = lambda i: jnp.minimum(i, n_p - 1)
    second = lambda i: jnp.maximum(i - n_p, 0)
    row = lambda: pl.BlockSpec((1, D_MODEL), lambda i: (0, 0))
    tok_p = lambda: pl.BlockSpec((tm, D_MODEL), lambda i: (first(i), 0))
    tok_s = lambda: pl.BlockSpec((tm, D_MODEL), lambda i: (second(i), 0))
    seq_p = lambda: pl.BlockSpec((None, 1, D_MODEL), lambda i: (first(i) // tiles_per_seq, 0, 0))
    per_k = lambda: pl.BlockSpec((TOP_K, tm), lambda i: (0, i))
    return pl.pallas_call(
        functools.partial(_post_kernel, n_p=n_p),
        out_shape=(jax.ShapeDtypeStruct((t, D_MODEL), F32),
                   jax.ShapeDtypeStruct((t * LANE_CHUNKS, LANES), F32),
                   jax.ShapeDtypeStruct((TOP_K, t), I32), jax.ShapeDtypeStruct((TOP_K, t), F32),
                   jax.ShapeDtypeStruct((TOP_K, t), I32), jax.ShapeDtypeStruct((N_EXPERTS, 1), F32)),
        grid=(n_p + n_s,),
        in_specs=[tok_p(), tok_s(), tok_p(), tok_s(), seq_p(), seq_p(), seq_p(), tok_s(), tok_s(), tok_s(),
                  row(), row(),
                  pl.BlockSpec((D_MODEL, D_MODEL), lambda i: (0, 0)),
                  pl.BlockSpec((N_EXPERTS, D_MODEL), lambda i: (0, 0)),
                  pl.BlockSpec((N_EXPERTS, 1), lambda i: (0, 0))],
        out_specs=(pl.BlockSpec((tm, D_MODEL), lambda i: (i, 0)),
                   pl.BlockSpec((tm * LANE_CHUNKS, LANES), lambda i: (i, 0)),
                   per_k(), per_k(), per_k(), pl.BlockSpec((N_EXPERTS, 1), lambda i: (0, 0))),
        scratch_shapes=[pltpu.VMEM((N_EXPERTS, 1), F32)],
        compiler_params=_cparams(("arbitrary",)),
    )(m_p, m_s, x_p, x_s, *mod_p, *mod_s, npost, npre, wout_bf, rwt, rb)


def _dispatch_kernel(fill_lo_ref, fill_hi_ref, dest_ref, hf_ref, xs_hbm, zeros, row_sem, pad_sem, *, tm):
    i = pl.program_id(0)
    n_rows = TOP_K * tm

    @pl.when(i == 0)
    def _():
        zeros[...] = jnp.zeros_like(zeros)

        def per_expert(e, totals):
            lo, hi = fill_lo_ref[e], fill_hi_ref[e]
            mid = jnp.minimum((lo + ZERO_ROWS - 1) // ZERO_ROWS * ZERO_ROWS, hi)

            def per_row(s, carry):
                pltpu.make_async_copy(zeros.at[0], xs_hbm.at[s], pad_sem).start()
                return carry
            lax.fori_loop(lo, mid, per_row, 0)

            def per_chunk(c, carry):
                pltpu.make_async_copy(zeros, xs_hbm.at[pl.ds(pl.multiple_of(c * ZERO_ROWS, ZERO_ROWS), ZERO_ROWS)],
                                      pad_sem).start()
                return carry
            lax.fori_loop(mid // ZERO_ROWS, hi // ZERO_ROWS, per_chunk, 0)
            return totals[0] + (mid - lo), totals[1] + (hi - mid) // ZERO_ROWS
        n_rows_filled, n_chunks_filled = lax.fori_loop(0, N_EXPERTS, per_expert, (0, 0))

        def drain_row(s, carry):
            pltpu.make_async_copy(zeros.at[0], xs_hbm.at[0], pad_sem).wait()
            return carry
        lax.fori_loop(0, n_rows_filled, drain_row, 0)

        def drain_chunk(c, carry):
            pltpu.make_async_copy(zeros, xs_hbm.at[pl.ds(0, ZERO_ROWS)], pad_sem).wait()
            return carry
        lax.fori_loop(0, n_chunks_filled, drain_chunk, 0)

    for n in range(n_rows):
        pltpu.make_async_copy(hf_ref.at[pl.ds((n % tm) * LANE_CHUNKS, LANE_CHUNKS), :],
                              xs_hbm.at[dest_ref[0, n]], row_sem).start(priority=n % 2)
    pltpu.make_async_copy(xs_hbm.at[pl.ds(0, n_rows)], xs_hbm.at[pl.ds(0, n_rows)], row_sem).wait()


def _dispatch(fill_lo, fill_hi, dest_tiles, hf_rows, n_slots, tm):
    return pl.pallas_call(
        functools.partial(_dispatch_kernel, tm=tm),
        out_shape=jax.ShapeDtypeStruct((n_slots, LANE_CHUNKS, LANES), F32),
        grid_spec=pltpu.PrefetchScalarGridSpec(
            num_scalar_prefetch=2, grid=(dest_tiles.shape[0],),
            in_specs=[pl.BlockSpec((None, 1, TOP_K * tm), lambda i, lo, hi: (i, 0, 0), memory_space=pltpu.SMEM),
                      pl.BlockSpec((tm * LANE_CHUNKS, LANES), lambda i, lo, hi: (i, 0))],
            out_specs=pl.BlockSpec(memory_space=pl.ANY),
            scratch_shapes=[pltpu.VMEM((ZERO_ROWS, LANE_CHUNKS, LANES), F32),
                            pltpu.SemaphoreType.DMA, pltpu.SemaphoreType.DMA]),
        compiler_params=_cparams(("arbitrary",)),
    )(fill_lo, fill_hi, dest_tiles, hf_rows)


def _expert_kernel(be_ref, nxt_ref, nu_ref, xs_ref, wgu_hbm, bgu_ref, wd_hbm, bd_ref, ys_ref,
                   wgu_f32, wd_f32, wgu_bf, wd_bf, w_sem):
    i = pl.program_id(0)
    n_used = nu_ref[0]

    def weight_copies(e):
        return (pltpu.make_async_copy(wgu_hbm.at[e], wgu_f32, w_sem.at[0]),
                pltpu.make_async_copy(wd_hbm.at[e], wd_f32, w_sem.at[1]))

    @pl.when(i == 0)
    def _():
        for cp in weight_copies(be_ref[0]):
            cp.start()

    @pl.when(i < n_used)
    def _():
        prev = be_ref[jnp.maximum(i - 1, 0)]

        @pl.when((i == 0) | (be_ref[i] != prev))
        def _():
            for cp in weight_copies(be_ref[i]):
                cp.wait()
            wgu_bf[...] = wgu_f32[...].astype(BF16)
            wd_bf[...] = wd_f32[...].astype(BF16)
            nxt = nxt_ref[be_ref[i]]

            @pl.when(nxt >= 0)
            def _():
                for cp in weight_copies(nxt):
                    cp.start()

        x = _from_row_tiles(xs_ref, 0, MOE_BM).astype(BF16)
        hu = jnp.dot(x, wgu_bf[...], preferred_element_type=F32) + bgu_ref[...]
        glu = jnp.minimum(hu[:, :D_EXPERT], SWIGLU_LIMIT)
        lin = jnp.clip(hu[:, D_EXPERT:], -SWIGLU_LIMIT, SWIGLU_LIMIT)
        act = glu * jax.nn.sigmoid(SWIGLU_ALPHA * glu) * (lin + 1.0)
        y = jnp.dot(act.astype(BF16), wd_bf[...], preferred_element_type=F32) + bd_ref[...]
        _to_row_tiles(ys_ref, y, MOE_BM)

    @pl.when(i >= n_used)
    def _():
        ys_ref[...] = jnp.zeros_like(ys_ref)


def _experts(block_e, next_e, n_used, xs2d, w_gu, b_gu, w_down, b_down):
    n_blocks = block_e.shape[0]
    rows = MOE_BM * LANE_CHUNKS
    live = lambda i, be, nx, nu: (jnp.minimum(i, nu[0] - 1), 0)
    return pl.pallas_call(
        _expert_kernel,
        out_shape=jax.ShapeDtypeStruct(xs2d.shape, F32),
        grid_spec=pltpu.PrefetchScalarGridSpec(
            num_scalar_prefetch=3,
            grid=(n_blocks,),
            in_specs=[pl.BlockSpec((rows, LANES), live),
                      pl.BlockSpec(memory_space=pl.ANY),
                      pl.BlockSpec((None, 1, 2 * D_EXPERT), lambda i, be, nx, nu: (be[i], 0, 0)),
                      pl.BlockSpec(memory_space=pl.ANY),
                      pl.BlockSpec((None, 1, D_MODEL), lambda i, be, nx, nu: (be[i], 0, 0))],
            out_specs=pl.BlockSpec((rows, LANES), lambda i, be, nx, nu: (i, 0)),
            scratch_shapes=[pltpu.VMEM((D_MODEL, 2 * D_EXPERT), F32),
                            pltpu.VMEM((D_EXPERT, D_MODEL), F32),
                            pltpu.VMEM((D_MODEL, 2 * D_EXPERT), BF16),
                            pltpu.VMEM((D_EXPERT, D_MODEL), BF16),
                            pltpu.SemaphoreType.DMA((2,))]),
        compiler_params=_cparams(("arbitrary",)),
    )(block_e, next_e, n_used, xs2d, w_gu, b_gu, w_down, b_down)


def _combine_kernel(dcur_ref, dnext_ref, ys_hbm, g_ref, x1_ref, gt_ref, np_ref, o_ref, ybuf, row_sem, *, tm):
    i = pl.program_id(0)
    n_rows = TOP_K * tm
    slot = i % 2

    def slot_rows(s):
        return ybuf.at[pl.ds(pl.multiple_of(s * n_rows * LANE_CHUNKS, n_rows * LANE_CHUNKS), n_rows * LANE_CHUNKS), :]

    def issue(d_ref, s):
        dst = slot_rows(s)
        for n in range(n_rows):
            pltpu.make_async_copy(ys_hbm.at[d_ref[0, n]], dst.at[pl.ds(n * LANE_CHUNKS, LANE_CHUNKS), :],
                                  row_sem.at[s]).start(priority=n % 2)

    @pl.when(i == 0)
    def _():
        issue(dcur_ref, 0)

    @pl.when(i + 1 < pl.num_programs(0))
    def _():
        issue(dnext_ref, 1 - slot)

    pltpu.make_async_copy(slot_rows(slot), slot_rows(slot), row_sem.at[slot]).wait()
    g = g_ref[...]
    y = None
    for kk in range(TOP_K):
        rows = _from_row_tiles(ybuf, (slot * n_rows + kk * tm) * LANE_CHUNKS, tm)
        y = g[:, kk:kk + 1] * rows if y is None else y + g[:, kk:kk + 1] * rows
    o_ref[...] = x1_ref[...] + gt_ref[...] * _rms(y, np_ref[...])


def _combine(dest_tiles, ys_tiles, gates_t, x1, gt, npost, tm, tiles_per_seq, tile0):
    n_tiles = dest_tiles.shape[0]
    t = n_tiles * tm
    smem_tile = lambda f: pl.BlockSpec((None, 1, TOP_K * tm), f, memory_space=pltpu.SMEM)
    return pl.pallas_call(
        functools.partial(_combine_kernel, tm=tm),
        out_shape=jax.ShapeDtypeStruct((t, D_MODEL), F32),
        grid=(n_tiles,),
        in_specs=[smem_tile(lambda i: (i, 0, 0)),
                  smem_tile(lambda i: (jnp.minimum(i + 1, n_tiles - 1), 0, 0)),
                  pl.BlockSpec(memory_space=pl.ANY),
                  pl.BlockSpec((tm, TOP_K), lambda i: (i + tile0, 0)),
                  pl.BlockSpec((tm, D_MODEL), lambda i: (i + tile0, 0)),
                  _mod_spec(gt, tm, tiles_per_seq),
                  pl.BlockSpec((1, D_MODEL), lambda i: (0, 0))],
        out_specs=pl.BlockSpec((tm, D_MODEL), lambda i: (i, 0)),
        scratch_shapes=[pltpu.VMEM((2 * TOP_K * tm * LANE_CHUNKS, LANES), F32),
                        pltpu.SemaphoreType.DMA((2,))],
        compiler_params=_cparams(("arbitrary",)),
    )(dest_tiles, dest_tiles, ys_tiles, gates_t, x1, gt, npost)


def _expert_layout(counts, n_blocks):
    padded = (counts + MOE_BM - 1) // MOE_BM * MOE_BM
    pad_ends = jnp.cumsum(padded)
    pad_starts = pad_ends - padded
    block_start = jnp.arange(n_blocks, dtype=I32) * MOE_BM
    block_e = jnp.minimum(jnp.sum((pad_ends[None, :] <= block_start[:, None]).astype(I32), axis=1), N_EXPERTS - 1)
    n_used = (pad_ends[-1:] // MOE_BM).astype(I32)
    e_ids = jnp.arange(N_EXPERTS, dtype=I32)
    later = (counts > 0)[None, :] & (e_ids[None, :] > e_ids[:, None])
    next_e = jnp.min(jnp.where(later, e_ids[None, :], N_EXPERTS), axis=1)
    next_e = jnp.where(next_e < N_EXPERTS, next_e, -1).astype(I32)
    return pad_starts, pad_starts + counts, pad_ends, block_e, next_e, n_used


def _slots(idx, rank, pad_starts):
    onehot = idx[:, :, None] == jnp.arange(N_EXPERTS, dtype=I32)[None, None, :]
    return rank + jnp.sum(jnp.where(onehot, pad_starts[None, None, :], 0), axis=-1)


def _tile_major(a, tm):
    t = a.shape[1]
    return a.reshape(TOP_K, t // tm, tm).transpose(1, 0, 2).reshape(t // tm, 1, TOP_K * tm)


def _block_diag(wb):
    n, d, _ = wb.shape
    eye = jnp.eye(n, dtype=wb.dtype)
    return (wb[:, :, None, :] * eye[:, None, :, None]).reshape(n * d, n * d)


def _pick_tile(n, prefs):
    for p in prefs:
        if n % p == 0:
            return p
    raise ValueError(f"no tile for {n}")


def kernel(x_prompt, x_sample, state_rglru_conv, state_rglru_h, state_hgrn_S, c_prompt, c_sample, ada_w, ada_b, norm_pre_mix, norm_post_mix, norm_pre_ffn, norm_post_ffn, w_in, conv_w, conv_b, lru_wa, lru_ba, lru_wi, lru_bi, lru_lambda, hg_lb, hg_norm, w_out, router_w, router_b, w_gu, b_gu, w_down, b_down):
    assert ada_w.shape[0] == 1, "single-layer trunk"
    bp, lp, _ = x_prompt.shape
    bs, ls, _ = x_sample.shape
    tp, ts = bp * lp, bs * ls
    assert CONV_WIDTH - 1 <= ls <= SUBLANES

    w_in_bf = w_in[0].astype(BF16)
    w_out_bf = w_out[0].astype(BF16)
    wg = jnp.concatenate([_block_diag(lru_wa[0]), _block_diag(lru_wi[0])], axis=1).astype(BF16)
    bg = jnp.concatenate([lru_ba[0], lru_bi[0]])[None, :]
    lbs = jnp.cumsum(jax.nn.softmax(hg_lb.astype(F32), axis=0), axis=0)[0][None, :]
    mw = (conv_w[0], conv_b[0][None, :], wg, bg, lru_lambda[0][None, :], lbs, hg_norm[0][None, :])
    row = lambda p: p[0][None, :]

    n_c = bp + bs
    n_c_pad = -(-n_c // SUBLANES) * SUBLANES
    c_all = jnp.concatenate([c_prompt, c_sample, jnp.zeros((n_c_pad - n_c, D_MODEL), F32)], axis=0)
    mod = _ada(c_all, ada_w[0], ada_b[0][None, :])
    mod_p = [m[:, None, :] for m in jnp.split(mod[:bp], 6, axis=-1)]
    mod_s = [jnp.repeat(m, ls, axis=0) for m in jnp.split(mod[bp:n_c], 6, axis=-1)]

    xp = x_prompt.reshape(tp, D_MODEL)
    xs = x_sample.reshape(ts, D_MODEL)
    tm_s = _pick_tile(ts, (256, 128, 64, 32, 16, 8))

    tc = _pick_tile(lp // 2, (256, 128, 64))
    m_p, conv_p, h_p, s_p = _mixer_prompt(xp, mod_p[1], mod_p[0], row(norm_pre_mix), w_in_bf, bp, lp, mw, tc)
    proj_s = _inproj(xs, mod_s[1], mod_s[0], row(norm_pre_mix), w_in_bf, tm_s, 1)

    proj_s_pad = jnp.pad(proj_s.reshape(bs, ls, IN_COLS), ((0, 0), (0, SUBLANES - ls), (0, 0)))
    cst_hdr = jnp.pad(state_rglru_conv[0], ((0, 0), (CONV_HDR - (CONV_WIDTH - 1), 0), (0, 0)))
    m_s_pad, conv_s, h_s, s_s = _mixer_sample(proj_s_pad, cst_hdr, state_rglru_h[0][:, None, :],
                                              state_hgrn_S[0], mw, ls)
    m_s = m_s_pad[:, :ls, :].reshape(ts, D_MODEL)

    tmq = _pick_tile(math.gcd(lp, ts), (512, 256, 128, 64, 32, 16, 8))
    x1, hf, idx, gate, rank, counts = _post(
        m_p, m_s, xp, xs, (mod_p[2], mod_p[4], mod_p[3]), (mod_s[2], mod_s[4], mod_s[3]),
        row(norm_post_mix), row(norm_pre_ffn), w_out_bf, router_w[0].T, router_b[0][:, None], tmq, lp // tmq)

    n_blocks = -(-(TOP_K * (tp + ts)) // MOE_BM) + N_EXPERTS
    n_slots = n_blocks * MOE_BM
    pad_starts, fill_lo, pad_ends, block_e, next_e, n_used = _expert_layout(counts[:, 0].astype(I32), n_blocks)
    fill_hi = pad_ends.at[N_EXPERTS - 1].set(n_slots)
    dest = _slots(idx, rank, pad_starts)
    tmd = _pick_tile(math.gcd(lp, ts), (256, 128, 64, 32, 16, 8))
    dtiles = _tile_major(dest, tmd)
    xs_rows = _dispatch(fill_lo, fill_hi, dtiles, hf, n_slots, tmd)
    ys = _experts(block_e, next_e, n_used, xs_rows.reshape(n_slots * LANE_CHUNKS, LANES),
                  w_gu[0], b_gu[0][:, None, :], w_down[0], b_down[0][:, None, :])
    ys_tiles = ys.reshape(-1, LANE_CHUNKS, LANES)

    gates_t = gate.T
    y_p = _combine(dtiles[:tp // tmd], ys_tiles, gates_t, x1, mod_p[5], row(norm_post_ffn), tmd, lp // tmd, 0)
    y_s = _combine(dtiles[tp // tmd:], ys_tiles, gates_t, x1, mod_s[5], row(norm_post_ffn), tmd, 1, tp // tmd)

    return (y_p.reshape(bp, lp, D_MODEL), y_s.reshape(bs, ls, D_MODEL),
            conv_p[None], h_p.reshape(1, bp, LRU_WIDTH), s_p[None],
            conv_s[None], h_s.reshape(1, bs, LRU_WIDTH), s_s[None])
```

```python
import functools
import math

import jax
import jax.numpy as jnp
from jax import lax
from jax.experimental import pallas as pl
from jax.experimental.pallas import tpu as pltpu

F32 = jnp.float32
BF16 = jnp.bfloat16
I32 = jnp.int32

D_MODEL = 1024
LRU_WIDTH = 512
LRU_BLOCKS = 8
LRU_BLOCK = LRU_WIDTH // LRU_BLOCKS
CONV_WIDTH = 4
LRU_C = 8.0
HG_WIDTH = 512
HG_HEAD_DIM = 128
HG_HEADS = 4
IN_COLS = 2 * LRU_WIDTH + 4 * HG_WIDTH
N_EXPERTS = 32
TOP_K = 4
D_EXPERT = 1024
SWIGLU_LIMIT = 7.0
SWIGLU_ALPHA = 1.702
RMS_EPS = 1e-6

C_XL, C_YL, C_Q, C_F, C_V, C_G = 0, 512, 1024, 1536, 2048, 2560

SUBLANES = 8
LANES = 128
LANE_CHUNKS = D_MODEL // LANES
CONV_HDR = SUBLANES
HG_CHUNK = 64
HG_MAX_EXPONENT = 60.0
MOE_BM = 512
ZERO_ROWS = 64
VMEM_LIMIT = 56 * 1024 * 1024

_NT = (((1,), (1,)), ((), ()))
_TN = (((0,), (0,)), ((), ()))


def _cparams(sem):
    return pltpu.CompilerParams(dimension_semantics=sem, vmem_limit_bytes=VMEM_LIMIT)


def _rms(x, g):
    return x * lax.rsqrt(jnp.mean(x * x, axis=-1, keepdims=True) + RMS_EPS) * g


def _gelu_tanh(x):
    c = math.sqrt(2.0 / math.pi)
    return 0.5 * x * (1.0 + jnp.tanh(c * (x + 0.044715 * (x * x * x))))


def _ada_kernel(c_ref, w_ref, b_ref, o_ref):
    c = c_ref[...]
    s = c * jax.nn.sigmoid(c)
    o_ref[...] = jnp.dot(s.astype(BF16), w_ref[...].astype(BF16), preferred_element_type=F32) + b_ref[...]


def _ada(c_all, ada_w, ada_b):
    n = c_all.shape[0]
    tn = 1024
    return pl.pallas_call(
        _ada_kernel,
        out_shape=jax.ShapeDtypeStruct((n, 6 * D_MODEL), F32),
        grid=(6 * D_MODEL // tn,),
        in_specs=[pl.BlockSpec((n, D_MODEL), lambda j: (0, 0)),
                  pl.BlockSpec((D_MODEL, tn), lambda j: (0, j)),
                  pl.BlockSpec((1, tn), lambda j: (0, j))],
        out_specs=pl.BlockSpec((n, tn), lambda j: (0, j)),
        compiler_params=_cparams(("arbitrary",)),
    )(c_all, ada_w, ada_b)


def _inproj_kernel(x_ref, sc_ref, sh_ref, g_ref, w_ref, o_ref):
    h = _rms(x_ref[...], g_ref[...]) * (1.0 + sc_ref[...]) + sh_ref[...]
    o_ref[...] = jnp.dot(h.astype(BF16), w_ref[...], preferred_element_type=F32)


def _mod_spec(mod, tm, tiles_per_seq):
    if mod.ndim == 3:
        return pl.BlockSpec((None, 1, D_MODEL), lambda i: (i // tiles_per_seq, 0, 0))
    return pl.BlockSpec((tm, D_MODEL), lambda i: (i, 0))


def _inproj(x, sc, sh, g, w_bf, tm, tiles_per_seq):
    t = x.shape[0]
    return pl.pallas_call(
        _inproj_kernel,
        out_shape=jax.ShapeDtypeStruct((t, IN_COLS), F32),
        grid=(t // tm,),
        in_specs=[pl.BlockSpec((tm, D_MODEL), lambda i: (i, 0)),
                  _mod_spec(sc, tm, tiles_per_seq), _mod_spec(sh, tm, tiles_per_seq),
                  pl.BlockSpec((1, D_MODEL), lambda i: (0, 0)),
                  pl.BlockSpec((D_MODEL, IN_COLS), lambda i: (0, 0))],
        out_specs=pl.BlockSpec((tm, IN_COLS), lambda i: (i, 0)),
        compiler_params=_cparams(("arbitrary",)),
    )(x, sc, sh, g, w_bf)


def _sigmoid(x):
    return 0.5 * jnp.tanh(0.5 * x) + 0.5


def _group_rows(x):
    rows, w = x.shape
    return x.reshape(rows // SUBLANES, SUBLANES, w)


def _scan_rows(a, u, h0):
    a3, u3 = _group_rows(a), _group_rows(u)
    r3 = lax.broadcasted_iota(I32, a3.shape, 1)
    s = 1
    while s < SUBLANES:
        keep = r3 >= s
        a_sh = jnp.where(keep, pltpu.roll(a3, s, 1), 1.0)
        u_sh = jnp.where(keep, pltpu.roll(u3, s, 1), 0.0)
        u3 = a3 * u_sh + u3
        a3 = a3 * a_sh
        s *= 2
    out, h = [], h0
    for i in range(a3.shape[0]):
        hg = u3[i] + a3[i] * h
        h = hg[SUBLANES - 1:SUBLANES, :]
        out.append(hg)
    return jnp.concatenate(out, axis=0), h


def _chunk_cumsum(x, chunk):
    x3 = _group_rows(x)
    r3 = lax.broadcasted_iota(I32, x3.shape, 1)
    s = 1
    while s < SUBLANES:
        x3 = x3 + jnp.where(r3 >= s, pltpu.roll(x3, s, 1), 0.0)
        s *= 2
    out, carry = [], None
    for i in range(x3.shape[0]):
        cur = x3[i] if i % (chunk // SUBLANES) == 0 else x3[i] + carry
        carry = cur[SUBLANES - 1:SUBLANES, :]
        out.append(cur)
    return jnp.concatenate(out, axis=0)


def _lru_gates(xc, wg_ref, bg_ref, lam_ref):
    gates = jnp.dot(xc.astype(BF16), wg_ref[...], preferred_element_type=F32) + bg_ref[...]
    r = _sigmoid(gates[:, :LRU_WIDTH])
    ig = _sigmoid(gates[:, LRU_WIDTH:])
    z = -lam_ref[...]
    softplus = jnp.maximum(z, 0.0) + jnp.log1p(jnp.exp(-jnp.abs(z)))
    log_a = -LRU_C * r * softplus
    a = jnp.exp(log_a)
    th = jnp.tanh(log_a)
    mult2 = -2.0 * th / (1.0 - th)
    return a, jnp.sqrt(mult2), ig


def _conv4(xbuf, xl, cw_ref, cb_ref, rows):
    cw = cw_ref[...]
    h = CONV_HDR
    return (cb_ref[...] + cw[0:1] * xbuf[h - 3:h - 3 + rows, :] + cw[1:2] * xbuf[h - 2:h - 2 + rows, :]
            + cw[2:3] * xbuf[h - 1:h - 1 + rows, :] + cw[3:4] * xl)


def _conv4_rolled(xl, prev, cw_ref, cb_ref):
    cw = cw_ref[...]
    row8 = lax.broadcasted_iota(I32, prev.shape, 0)
    acc = cb_ref[...] + cw[CONV_WIDTH - 1:CONV_WIDTH] * xl
    for s in range(1, CONV_WIDTH):
        sh = pltpu.roll(xl, s, 0)
        top = jnp.where(row8 < s, pltpu.roll(prev, s, 0), sh[0:SUBLANES, :])
        sh = jnp.concatenate([top, sh[SUBLANES:, :]], axis=0)
        acc = acc + cw[CONV_WIDTH - 1 - s:CONV_WIDTH - s] * sh
    return acc


def _head_norm_gate(o, gate, hgn):
    return _rms(o, hgn) * _sigmoid(gate)


def _hgrn_window(q, k, v, b, state, steps, state_is_transposed):
    row = lax.broadcasted_iota(I32, q.shape, 0)
    valid = row < steps
    b_end = b[steps - 1:steps, :]
    q_in = jnp.where(valid, q * jnp.exp(b), 0.0).astype(BF16)
    if state_is_transposed:
        o = lax.dot_general(q_in, state.astype(BF16), _NT, preferred_element_type=F32)
    else:
        o = jnp.dot(q_in, state.astype(BF16), preferred_element_type=F32)
    for d in range(steps):
        ok = valid & (row >= d)
        k_d = k if d == 0 else pltpu.roll(k, d, 0)
        v_d = v if d == 0 else pltpu.roll(v, d, 0)
        b_d = b if d == 0 else pltpu.roll(b, d, 0)
        decay = jnp.exp(jnp.where(ok, b - b_d, 0.0))
        wgt = jnp.sum(jnp.where(ok, q * k_d * decay, 0.0), axis=-1, keepdims=True)
        o = o + wgt * v_d
    k_out = jnp.where(valid, k * jnp.exp(jnp.where(valid, b_end - b, 0.0)), 0.0).astype(BF16)
    v_ok = jnp.where(valid, v, 0.0).astype(BF16)
    keep = jnp.exp(b_end)
    if state_is_transposed:
        s_new = state * keep + lax.dot_general(v_ok, k_out, _TN, preferred_element_type=F32)
    else:
        keep_col = jnp.broadcast_to(keep, q.shape).T[:, 0:1]
        s_new = state * keep_col + lax.dot_general(k_out, v_ok, _TN, preferred_element_type=F32)
    return o, s_new


def _tile_cumsum(x):
    row = lax.broadcasted_iota(I32, x.shape, 0)
    s = 1
    while s < SUBLANES:
        x = x + jnp.where(row >= s, pltpu.roll(x, s, 0), 0.0)
        s *= 2
    return x


def _mix_tile(proj_ref, cw_ref, cb_ref, wg_ref, bg_ref, lam_ref, lb_ref, hgn_ref, m_ref, xprev, hcar, st, st0, obuf,
              seq_start, tc, side_work):
    w = LRU_WIDTH
    side_work = iter(side_work)

    next(side_work)()
    xl = proj_ref[:, C_XL:C_XL + w]
    xc = _conv4_rolled(xl, xprev[...], cw_ref, cb_ref)
    xprev[...] = xl[tc - SUBLANES:tc, :]
    a, mult, ig = _lru_gates(xc, wg_ref, bg_ref, lam_ref)
    row = lax.broadcasted_iota(I32, (tc, w), 0)
    if seq_start is not None:
        mult = jnp.where((row == 0) & seq_start, 1.0, mult)
    h, h_last = _scan_rows(a, mult * ig * xc, hcar[0:1, :])
    hcar[0:1, :] = h_last
    m_ref[:, 0:w] = (h * _gelu_tanh(proj_ref[:, C_YL:C_YL + w])).astype(BF16)

    next(side_work)()
    lb = lb_ref[...]
    f = lb + (1.0 - lb) * _sigmoid(proj_ref[:, C_F:C_F + w])
    k = 1.0 - f
    b = _chunk_cumsum(jnp.log(f), HG_CHUNK)
    q = proj_ref[:, C_Q:C_Q + w]
    v = proj_ref[:, C_V:C_V + w]
    mid = HG_CHUNK // 2
    st0[...] = st[...]
    tri = (lax.broadcasted_iota(I32, (HG_CHUNK, HG_CHUNK), 0)
           >= lax.broadcasted_iota(I32, (HG_CHUNK, HG_CHUNK), 1))
    states = [st[hd] for hd in range(HG_HEADS)]
    for c in range(tc // HG_CHUNK):
        next(side_work)()
        rs = slice(c * HG_CHUNK, (c + 1) * HG_CHUNK)
        for hd in range(HG_HEADS):
            cs = slice(hd * HG_HEAD_DIM, (hd + 1) * HG_HEAD_DIM)
            s_t = states[hd]
            bc, qc, kc = b[rs, cs], q[rs, cs], k[rs, cs]
            vb = v[rs, cs].astype(BF16)
            b_mid = bc[mid - 1:mid, :]
            b_end = bc[HG_CHUNK - 1:HG_CHUNK, :]
            qp = (qc * jnp.exp(bc - b_mid)).astype(BF16)
            kp = (kc * jnp.exp(b_mid - bc)).astype(BF16)
            att = lax.dot_general(qp, kp, _NT, preferred_element_type=F32)
            att = jnp.where(tri, att, 0.0)
            q_in = (qc * jnp.exp(bc)).astype(BF16)
            o = (jnp.dot(att.astype(BF16), vb, preferred_element_type=F32)
                 + lax.dot_general(q_in, s_t.astype(BF16), _NT, preferred_element_type=F32))
            k_out = (kc * jnp.exp(b_end - bc)).astype(BF16)
            states[hd] = s_t * jnp.exp(b_end) + lax.dot_general(vb, k_out, _TN, preferred_element_type=F32)
            obuf[rs, cs] = o
    for hd in range(HG_HEADS):
        st[hd] = states[hd]

    half_decay = None
    for c in range(tc // HG_CHUNK):
        b_mid = b[c * HG_CHUNK + mid - 1:c * HG_CHUNK + mid, :]
        b_end = b[(c + 1) * HG_CHUNK - 1:(c + 1) * HG_CHUNK, :]
        worst = jnp.maximum(-b_mid, b_mid - b_end)
        half_decay = worst if half_decay is None else jnp.maximum(half_decay, worst)

    @pl.when(jnp.logical_not(jnp.max(half_decay) < HG_MAX_EXPONENT))
    def _():
        st[...] = st0[...]

        def window(gi, carry):
            rows = pl.ds(pl.multiple_of(gi * SUBLANES, SUBLANES), SUBLANES)
            f8 = lb + (1.0 - lb) * _sigmoid(proj_ref[rows, C_F:C_F + w])
            b8 = _tile_cumsum(jnp.log(f8))
            k8 = 1.0 - f8
            q8 = proj_ref[rows, C_Q:C_Q + w]
            v8 = proj_ref[rows, C_V:C_V + w]
            for hd in range(HG_HEADS):
                cs = slice(hd * HG_HEAD_DIM, (hd + 1) * HG_HEAD_DIM)
                o, s_new = _hgrn_window(q8[:, cs], k8[:, cs], v8[:, cs], b8[:, cs], st[hd], SUBLANES, True)
                st[hd] = s_new
                obuf[rows, cs] = o
            return carry
        lax.fori_loop(0, tc // SUBLANES, window, 0)

    hgn = hgn_ref[...]
    for hd in range(HG_HEADS):
        cs = slice(hd * HG_HEAD_DIM, (hd + 1) * HG_HEAD_DIM)
        gate = proj_ref[:, C_G + hd * HG_HEAD_DIM:C_G + (hd + 1) * HG_HEAD_DIM]
        m_ref[:, w + hd * HG_HEAD_DIM:w + (hd + 1) * HG_HEAD_DIM] = (
            _head_norm_gate(obuf[:, cs], gate, hgn).astype(BF16))


def _mixer_prompt_kernel(x_ref, xn_ref, sc_ref, sh_ref, scn_ref, shn_ref, g_ref, win_ref, cw_ref, cb_ref, wg_ref,
                         bg_ref, lam_ref, lb_ref, hgn_ref, m_ref, conv_ref, h_ref, s_ref,
                         proj_a, proj_b, hn_buf, xprev, hcar, st, st0, obuf, *, tc):
    j = pl.program_id(1)
    n_phases = 2 + tc // HG_CHUNK
    cols = IN_COLS // n_phases

    def project(x, sc, sh, out_ref):
        def chunk(c):
            if c == 0:
                hn_buf[...] = (_rms(x(), g_ref[...]) * (1.0 + sc[...]) + sh[...]).astype(BF16)
            out_ref[:, c * cols:(c + 1) * cols] = jnp.dot(hn_buf[...], win_ref[:, c * cols:(c + 1) * cols],
                                                          preferred_element_type=F32)
        return [functools.partial(chunk, c) for c in range(n_phases)]

    def mix(proj_ref, first_row, seq_start, side_work):
        _mix_tile(proj_ref, cw_ref, cb_ref, wg_ref, bg_ref, lam_ref, lb_ref, hgn_ref,
                  m_ref.at[pl.ds(first_row, tc), :], xprev, hcar, st, st0, obuf, seq_start, tc, side_work)

    @pl.when((pl.program_id(0) == 0) & (j == 0))
    def _():
        for thunk in project(lambda: x_ref[0:tc, :], sc_ref, sh_ref, proj_a):
            thunk()

    @pl.when(j == 0)
    def _():
        xprev[...] = jnp.zeros_like(xprev)
        hcar[...] = jnp.zeros_like(hcar)
        st[...] = jnp.zeros_like(st)

    mix(proj_a, 0, j == 0, project(lambda: x_ref[tc:2 * tc, :], sc_ref, sh_ref, proj_b))
    mix(proj_b, tc, None, project(lambda: xn_ref[...], scn_ref, shn_ref, proj_a))

    @pl.when(j == pl.num_programs(1) - 1)
    def _():
        conv_ref[...] = xprev[SUBLANES - (CONV_WIDTH - 1):SUBLANES, :]
        h_ref[...] = hcar[0:1, :]
        for hd in range(HG_HEADS):
            s_ref[hd] = st[hd].T


def _mixer_prompt(x, sc, sh, g, w_in_bf, nb, seq, mw, tc):
    cw, cb, wg, bg, lam, lb, hgn = mw
    nj = seq // (2 * tc)
    last_tile = nb * seq // tc - 1
    next_tile = lambda b, j: jnp.minimum(2 * (b * nj + j) + 2, last_tile)
    const = lambda shape: pl.BlockSpec(shape, lambda b, j: tuple(0 for _ in shape))
    per_seq = lambda: pl.BlockSpec((None, 1, D_MODEL), lambda b, j: (b, 0, 0))
    next_seq = lambda: pl.BlockSpec((None, 1, D_MODEL), lambda b, j: (next_tile(b, j) // (2 * nj), 0, 0))
    return pl.pallas_call(
        functools.partial(_mixer_prompt_kernel, tc=tc),
        out_shape=(jax.ShapeDtypeStruct((nb * seq, D_MODEL), BF16),
                   jax.ShapeDtypeStruct((nb, CONV_WIDTH - 1, LRU_WIDTH), F32),
                   jax.ShapeDtypeStruct((nb, 1, LRU_WIDTH), F32),
                   jax.ShapeDtypeStruct((nb, HG_HEADS, HG_HEAD_DIM, HG_HEAD_DIM), F32)),
        grid=(nb, nj),
        in_specs=[pl.BlockSpec((2 * tc, D_MODEL), lambda b, j: (b * nj + j, 0)),
                  pl.BlockSpec((tc, D_MODEL), lambda b, j: (next_tile(b, j), 0)),
                  per_seq(), per_seq(), next_seq(), next_seq(), const((1, D_MODEL)), const((D_MODEL, IN_COLS)),
                  const((CONV_WIDTH, LRU_WIDTH)), const((1, LRU_WIDTH)),
                  const((LRU_WIDTH, 2 * LRU_WIDTH)), const((1, 2 * LRU_WIDTH)),
                  const((1, LRU_WIDTH)), const((1, HG_WIDTH)), const((1, HG_HEAD_DIM))],
        out_specs=(pl.BlockSpec((2 * tc, D_MODEL), lambda b, j: (b * nj + j, 0)),
                   pl.BlockSpec((None, CONV_WIDTH - 1, LRU_WIDTH), lambda b, j: (b, 0, 0)),
                   pl.BlockSpec((None, 1, LRU_WIDTH), lambda b, j: (b, 0, 0)),
                   pl.BlockSpec((None, HG_HEADS, HG_HEAD_DIM, HG_HEAD_DIM), lambda b, j: (b, 0, 0, 0))),
        scratch_shapes=[pltpu.VMEM((tc, IN_COLS), F32),
                        pltpu.VMEM((tc, IN_COLS), F32),
                        pltpu.VMEM((tc, D_MODEL), BF16),
                        pltpu.VMEM((SUBLANES, LRU_WIDTH), F32),
                        pltpu.VMEM((SUBLANES, LRU_WIDTH), F32),
                        pltpu.VMEM((HG_HEADS, HG_HEAD_DIM, HG_HEAD_DIM), F32),
                        pltpu.VMEM((HG_HEADS, HG_HEAD_DIM, HG_HEAD_DIM), F32),
                        pltpu.VMEM((tc, HG_WIDTH), F32)],
        compiler_params=_cparams(("arbitrary", "arbitrary")),
    )(x, x, sc, sh, sc, sh, g, w_in_bf, cw, cb, wg, bg, lam, lb, hgn)


def _mixer_sample_kernel(proj_ref, cst_ref, h0_ref, s0_ref, cw_ref, cb_ref, wg_ref, bg_ref, lam_ref, lb_ref,
                         hgn_ref, m_ref, conv_ref, h_ref, s_ref, xbuf, *, steps):
    for bi in range(proj_ref.shape[0]):
        _mixer_sample_one(proj_ref.at[bi], cst_ref.at[bi], h0_ref.at[bi], s0_ref.at[bi], cw_ref, cb_ref, wg_ref,
                          bg_ref, lam_ref, lb_ref, hgn_ref, m_ref.at[bi], conv_ref.at[bi], h_ref.at[bi],
                          s_ref.at[bi], xbuf.at[bi], steps)


def _mixer_sample_one(proj_ref, cst_ref, h0_ref, s0_ref, cw_ref, cb_ref, wg_ref, bg_ref, lam_ref, lb_ref,
                      hgn_ref, m_ref, conv_ref, h_ref, s_ref, xbuf, steps):
    w = LRU_WIDTH
    rows = SUBLANES
    row = lax.broadcasted_iota(I32, (rows, w), 0)
    valid = row < steps

    xl = proj_ref[:, C_XL:C_XL + w]
    xbuf[0:CONV_HDR, :] = cst_ref[...]
    xbuf[CONV_HDR:CONV_HDR + rows, :] = xl
    xc = _conv4(xbuf, xl, cw_ref, cb_ref, rows)
    conv_ref[...] = xbuf[CONV_HDR + steps - 3:CONV_HDR + steps, :]
    a, mult, ig = _lru_gates(xc, wg_ref, bg_ref, lam_ref)
    u = mult * ig * xc
    h = h0_ref[...]
    hs = jnp.zeros((rows, w), F32)
    for t in range(steps):
        h = a[t:t + 1, :] * h + u[t:t + 1, :]
        hs = jnp.where(row == t, h, hs)
    h_ref[...] = h
    m_ref[:, 0:w] = (hs * _gelu_tanh(proj_ref[:, C_YL:C_YL + w])).astype(BF16)

    lb = lb_ref[...]
    f = lb + (1.0 - lb) * _sigmoid(proj_ref[:, C_F:C_F + w])
    k = 1.0 - f
    b = _tile_cumsum(jnp.log(f))
    q = proj_ref[:, C_Q:C_Q + w]
    v = proj_ref[:, C_V:C_V + w]
    hgn = hgn_ref[...]
    for hd in range(HG_HEADS):
        cs = slice(hd * HG_HEAD_DIM, (hd + 1) * HG_HEAD_DIM)
        o, s_new = _hgrn_window(q[:, cs], k[:, cs], v[:, cs], b[:, cs], s0_ref[hd], steps, False)
        s_ref[hd] = s_new
        gate = proj_ref[:, C_G + hd * HG_HEAD_DIM:C_G + (hd + 1) * HG_HEAD_DIM]
        m_ref[:, w + hd * HG_HEAD_DIM:w + (hd + 1) * HG_HEAD_DIM] = _head_norm_gate(o, gate, hgn).astype(BF16)


def _mixer_sample(proj_pad, cst_hdr, h0, s0, mw, steps):
    cw, cb, wg, bg, lam, lb, hgn = mw
    nb = proj_pad.shape[0]
    per_step = _pick_tile(nb, (8, 4, 2, 1))
    const = lambda shape: pl.BlockSpec(shape, lambda b: tuple(0 for _ in shape))
    per_b = lambda shape: pl.BlockSpec((per_step,) + shape, lambda b: (b,) + tuple(0 for _ in shape))
    return pl.pallas_call(
        functools.partial(_mixer_sample_kernel, steps=steps),
        out_shape=(jax.ShapeDtypeStruct((nb, SUBLANES, D_MODEL), BF16),
                   jax.ShapeDtypeStruct((nb, CONV_WIDTH - 1, LRU_WIDTH), F32),
                   jax.ShapeDtypeStruct((nb, 1, LRU_WIDTH), F32),
                   jax.ShapeDtypeStruct((nb, HG_HEADS, HG_HEAD_DIM, HG_HEAD_DIM), F32)),
        grid=(nb // per_step,),
        in_specs=[per_b((SUBLANES, IN_COLS)), per_b((CONV_HDR, LRU_WIDTH)), per_b((1, LRU_WIDTH)),
                  per_b((HG_HEADS, HG_HEAD_DIM, HG_HEAD_DIM)),
                  const((CONV_WIDTH, LRU_WIDTH)), const((1, LRU_WIDTH)),
                  const((LRU_WIDTH, 2 * LRU_WIDTH)), const((1, 2 * LRU_WIDTH)),
                  const((1, LRU_WIDTH)), const((1, HG_WIDTH)), const((1, HG_HEAD_DIM))],
        out_specs=(per_b((SUBLANES, D_MODEL)), per_b((CONV_WIDTH - 1, LRU_WIDTH)), per_b((1, LRU_WIDTH)),
                   per_b((HG_HEADS, HG_HEAD_DIM, HG_HEAD_DIM))),
        scratch_shapes=[pltpu.VMEM((per_step, CONV_HDR + SUBLANES, LRU_WIDTH), F32)],
        compiler_params=_cparams(("arbitrary",)),
    )(proj_pad, cst_hdr, h0, s0, cw, cb, wg, bg, lam, lb, hgn)


def _to_row_tiles(ref, val, rows):
    for c in range(LANE_CHUNKS):
        ref[pl.ds(c, rows, stride=LANE_CHUNKS), :] = val[:, c * LANES:(c + 1) * LANES]


def _from_row_tiles(ref, base, rows):
    return jnp.concatenate(
        [ref[pl.ds(base + c, rows, stride=LANE_CHUNKS), :] for c in range(LANE_CHUNKS)], axis=-1)


def _post_kernel(mp_ref, ms_ref, xp_ref, xs_ref, gtmp_ref, scfp_ref, shfp_ref, gtms_ref, scfs_ref, shfs_ref,
                 npost_ref, npre_ref, wout_ref, rwt_ref, rb_ref,
                 x1_ref, hf_ref, idx_ref, gate_ref, rank_ref, cnt_out_ref, cnt, *, n_p):
    i = pl.program_id(0)
    tm = xp_ref.shape[0]

    @pl.when(i == 0)
    def _():
        cnt[...] = jnp.zeros_like(cnt)

    def run(m_ref, x_ref, gtm_ref, scf_ref, shf_ref):
        mix = jnp.dot(m_ref[...], wout_ref[...], preferred_element_type=F32)
        x1 = x_ref[...] + gtm_ref[...] * _rms(mix, npost_ref[...])
        x1_ref[...] = x1
        hf = _rms(x1, npre_ref[...]) * (1.0 + scf_ref[...]) + shf_ref[...]
        _to_row_tiles(hf_ref, hf, tm)
        logits = lax.dot_general(rwt_ref[...], hf, _NT, precision=lax.Precision.HIGHEST,
                                 preferred_element_type=F32) + rb_ref[...]
        e_iota = lax.broadcasted_iota(I32, logits.shape, 0)
        vals, idxs = [], []
        for _ in range(TOP_K):
            mx = jnp.max(logits, axis=0, keepdims=True)
            ix = jnp.min(jnp.where(logits == mx, e_iota, N_EXPERTS), axis=0, keepdims=True)
            vals.append(mx)
            idxs.append(ix)
            logits = jnp.where(e_iota == ix, -jnp.inf, logits)
        exps = [jnp.exp(vv - vals[0]) for vv in vals]
        den = exps[0] + exps[1] + exps[2] + exps[3]
        earlier = (lax.broadcasted_iota(I32, (tm, tm), 0) < lax.broadcasted_iota(I32, (tm, tm), 1))
        earlier = jnp.where(earlier, 1.0, 0.0).astype(BF16)
        base = cnt[...]
        for kk in range(TOP_K):
            idx_ref[kk:kk + 1, :] = idxs[kk]
            gate_ref[kk:kk + 1, :] = exps[kk] / den
            hit = e_iota == idxs[kk]
            onehot = jnp.where(hit, 1.0, 0.0)
            before = jnp.dot(onehot.astype(BF16), earlier, preferred_element_type=F32)
            rank = jnp.sum(jnp.where(hit, before + base, 0.0), axis=0, keepdims=True)
            rank_ref[kk:kk + 1, :] = rank.astype(I32)
            base = base + jnp.sum(onehot, axis=1, keepdims=True)
        cnt[...] = base

    @pl.when(i < n_p)
    def _():
        run(mp_ref, xp_ref, gtmp_ref, scfp_ref, shfp_ref)

    @pl.when(i >= n_p)
    def _():
        run(ms_ref, xs_ref, gtms_ref, scfs_ref, shfs_ref)

    @pl.when(i == pl.num_programs(0) - 1)
    def _():
        cnt_out_ref[...] = cnt[...]


def _post(m_p, m_s, x_p, x_s, mod_p, mod_s, npost, npre, wout_bf, rwt, rb, tm, tiles_per_seq):
    n_p, n_s = x_p.shape[0] // tm, x_s.shape[0] // tm
    t = (n_p + n_s) * tm
    first = lambda i: jnp.minimum(i, n_p - 1)
    second = lambda i: jnp.maximum(i - n_p, 0)
    row = lambda: pl.BlockSpec((1, D_MODEL), lambda i: (0, 0))
    tok_p = lambda: pl.BlockSpec((tm, D_MODEL), lambda i: (first(i), 0))
    tok_s = lambda: pl.BlockSpec((tm, D_MODEL), lambda i: (second(i), 0))
    seq_p = lambda: pl.BlockSpec((None, 1, D_MODEL), lambda i: (first(i) // tiles_per_seq, 0, 0))
    per_k = lambda: pl.BlockSpec((TOP_K, tm), lambda i: (0, i))
    return pl.pallas_call(
        functools.partial(_post_kernel, n_p=n_p),
        out_shape=(jax.ShapeDtypeStruct((t, D_MODEL), F32),
                   jax.ShapeDtypeStruct((t * LANE_CHUNKS, LANES), F32),
                   jax.ShapeDtypeStruct((TOP_K, t), I32), jax.ShapeDtypeStruct((TOP_K, t), F32),
                   jax.ShapeDtypeStruct((TOP_K, t), I32), jax.ShapeDtypeStruct((N_EXPERTS, 1), F32)),
        grid=(n_p + n_s,),
        in_specs=[tok_p(), tok_s(), tok_p(), tok_s(), seq_p(), seq_p(), seq_p(), tok_s(), tok_s(), tok_s(),
                  row(), row(),
                  pl.BlockSpec((D_MODEL, D_MODEL), lambda i: (0, 0)),
                  pl.BlockSpec((N_EXPERTS, D_MODEL), lambda i: (0, 0)),
                  pl.BlockSpec((N_EXPERTS, 1), lambda i: (0, 0))],
        out_specs=(pl.BlockSpec((tm, D_MODEL), lambda i: (i, 0)),
                   pl.BlockSpec((tm * LANE_CHUNKS, LANES), lambda i: (i, 0)),
                   per_k(), per_k(), per_k(), pl.BlockSpec((N_EXPERTS, 1), lambda i: (0, 0))),
        scratch_shapes=[pltpu.VMEM((N_EXPERTS, 1), F32)],
        compiler_params=_cparams(("arbitrary",)),
    )(m_p, m_s, x_p, x_s, *mod_p, *mod_s, npost, npre, wout_bf, rwt, rb)


def _dispatch_kernel(fill_lo_ref, fill_hi_ref, dest_ref, hf_ref, xs_hbm, zeros, row_sem, pad_sem, *, tm):
    i = pl.program_id(0)
    n_rows = TOP_K * tm

    @pl.when(i == 0)
    def _():
        zeros[...] = jnp.zeros_like(zeros)

        def per_expert(e, totals):
            lo, hi = fill_lo_ref[e], fill_hi_ref[e]
            mid = jnp.minimum((lo + ZERO_ROWS - 1) // ZERO_ROWS * ZERO_ROWS, hi)

            def per_row(s, carry):
                pltpu.make_async_copy(zeros.at[0], xs_hbm.at[s], pad_sem).start()
                return carry
            lax.fori_loop(lo, mid, per_row, 0)

            def per_chunk(c, carry):
                pltpu.make_async_copy(zeros, xs_hbm.at[pl.ds(pl.multiple_of(c * ZERO_ROWS, ZERO_ROWS), ZERO_ROWS)],
                                      pad_sem).start()
                return carry
            lax.fori_loop(mid // ZERO_ROWS, hi // ZERO_ROWS, per_chunk, 0)
            return totals[0] + (mid - lo), totals[1] + (hi - mid) // ZERO_ROWS
        n_rows_filled, n_chunks_filled = lax.fori_loop(0, N_EXPERTS, per_expert, (0, 0))

        def drain_row(s, carry):
            pltpu.make_async_copy(zeros.at[0], xs_hbm.at[0], pad_sem).wait()
            return carry
        lax.fori_loop(0, n_rows_filled, drain_row, 0)

        def drain_chunk(c, carry):
            pltpu.make_async_copy(zeros, xs_hbm.at[pl.ds(0, ZERO_ROWS)], pad_sem).wait()
            return carry
        lax.fori_loop(0, n_chunks_filled, drain_chunk, 0)

    for n in range(n_rows):
        pltpu.make_async_copy(hf_ref.at[pl.ds((n % tm) * LANE_CHUNKS, LANE_CHUNKS), :],
                              xs_hbm.at[dest_ref[0, n]], row_sem).start(priority=n % 2)
    pltpu.make_async_copy(xs_hbm.at[pl.ds(0, n_rows)], xs_hbm.at[pl.ds(0, n_rows)], row_sem).wait()


def _dispatch(fill_lo, fill_hi, dest_tiles, hf_rows, n_slots, tm):
    return pl.pallas_call(
        functools.partial(_dispatch_kernel, tm=tm),
        out_shape=jax.ShapeDtypeStruct((n_slots, LANE_CHUNKS, LANES), F32),
        grid_spec=pltpu.PrefetchScalarGridSpec(
            num_scalar_prefetch=2, grid=(dest_tiles.shape[0],),
            in_specs=[pl.BlockSpec((None, 1, TOP_K * tm), lambda i, lo, hi: (i, 0, 0), memory_space=pltpu.SMEM),
                      pl.BlockSpec((tm * LANE_CHUNKS, LANES), lambda i, lo, hi: (i, 0))],
            out_specs=pl.BlockSpec(memory_space=pl.ANY),
            scratch_shapes=[pltpu.VMEM((ZERO_ROWS, LANE_CHUNKS, LANES), F32),
                            pltpu.SemaphoreType.DMA, pltpu.SemaphoreType.DMA]),
        compiler_params=_cparams(("arbitrary",)),
    )(fill_lo, fill_hi, dest_tiles, hf_rows)


def _expert_kernel(be_ref, nxt_ref, nu_ref, xs_ref, wgu_hbm, bgu_ref, wd_hbm, bd_ref, ys_ref,
                   wgu_f32, wd_f32, wgu_bf, wd_bf, w_sem):
    i = pl.program_id(0)
    n_used = nu_ref[0]

    def weight_copies(e):
        return (pltpu.make_async_copy(wgu_hbm.at[e], wgu_f32, w_sem.at[0]),
                pltpu.make_async_copy(wd_hbm.at[e], wd_f32, w_sem.at[1]))

    @pl.when(i == 0)
    def _():
        for cp in weight_copies(be_ref[0]):
            cp.start()

    @pl.when(i < n_used)
    def _():
        prev = be_ref[jnp.maximum(i - 1, 0)]

        @pl.when((i == 0) | (be_ref[i] != prev))
        def _():
            for cp in weight_copies(be_ref[i]):
                cp.wait()
            wgu_bf[...] = wgu_f32[...].astype(BF16)
            wd_bf[...] = wd_f32[...].astype(BF16)
            nxt = nxt_ref[be_ref[i]]

            @pl.when(nxt >= 0)
            def _():
                for cp in weight_copies(nxt):
                    cp.start()

        x = _from_row_tiles(xs_ref, 0, MOE_BM).astype(BF16)
        hu = jnp.dot(x, wgu_bf[...], preferred_element_type=F32) + bgu_ref[...]
        glu = jnp.minimum(hu[:, :D_EXPERT], SWIGLU_LIMIT)
        lin = jnp.clip(hu[:, D_EXPERT:], -SWIGLU_LIMIT, SWIGLU_LIMIT)
        act = glu * jax.nn.sigmoid(SWIGLU_ALPHA * glu) * (lin + 1.0)
        y = jnp.dot(act.astype(BF16), wd_bf[...], preferred_element_type=F32) + bd_ref[...]
        _to_row_tiles(ys_ref, y, MOE_BM)

    @pl.when(i >= n_used)
    def _():
        ys_ref[...] = jnp.zeros_like(ys_ref)


def _experts(block_e, next_e, n_used, xs2d, w_gu, b_gu, w_down, b_down):
    n_blocks = block_e.shape[0]
    rows = MOE_BM * LANE_CHUNKS
    live = lambda i, be, nx, nu: (jnp.minimum(i, nu[0] - 1), 0)
    return pl.pallas_call(
        _expert_kernel,
        out_shape=jax.ShapeDtypeStruct(xs2d.shape, F32),
        grid_spec=pltpu.PrefetchScalarGridSpec(
            num_scalar_prefetch=3,
            grid=(n_blocks,),
            in_specs=[pl.BlockSpec((rows, LANES), live),
                      pl.BlockSpec(memory_space=pl.ANY),
                      pl.BlockSpec((None, 1, 2 * D_EXPERT), lambda i, be, nx, nu: (be[i], 0, 0)),
                      pl.BlockSpec(memory_space=pl.ANY),
                      pl.BlockSpec((None, 1, D_MODEL), lambda i, be, nx, nu: (be[i], 0, 0))],
            out_specs=pl.BlockSpec((rows, LANES), lambda i, be, nx, nu: (i, 0)),
            scratch_shapes=[pltpu.VMEM((D_MODEL, 2 * D_EXPERT), F32),
                            pltpu.VMEM((D_EXPERT, D_MODEL), F32),
                            pltpu.VMEM((D_MODEL, 2 * D_EXPERT), BF16),
                            pltpu.VMEM((D_EXPERT, D_MODEL), BF16),
                            pltpu.SemaphoreType.DMA((2,))]),
        compiler_params=_cparams(("arbitrary",)),
    )(block_e, next_e, n_used, xs2d, w_gu, b_gu, w_down, b_down)


def _combine_kernel(dcur_ref, dnext_ref, ys_hbm, g_ref, x1_ref, gt_ref, np_ref, o_ref, ybuf, row_sem, *, tm):
    i = pl.program_id(0)
    n_rows = TOP_K * tm
    slot = i % 2

    def slot_rows(s):
        return ybuf.at[pl.ds(pl.multiple_of(s * n_rows * LANE_CHUNKS, n_rows * LANE_CHUNKS), n_rows * LANE_CHUNKS), :]

    def issue(d_ref, s):
        dst = slot_rows(s)
        for n in range(n_rows):
            pltpu.make_async_copy(ys_hbm.at[d_ref[0, n]], dst.at[pl.ds(n * LANE_CHUNKS, LANE_CHUNKS), :],
                                  row_sem.at[s]).start(priority=n % 2)

    @pl.when(i == 0)
    def _():
        issue(dcur_ref, 0)

    @pl.when(i + 1 < pl.num_programs(0))
    def _():
        issue(dnext_ref, 1 - slot)

    pltpu.make_async_copy(slot_rows(slot), slot_rows(slot), row_sem.at[slot]).wait()
    g = g_ref[...]
    y = None
    for kk in range(TOP_K):
        rows = _from_row_tiles(ybuf, (slot * n_rows + kk * tm) * LANE_CHUNKS, tm)
        y = g[:, kk:kk + 1] * rows if y is None else y + g[:, kk:kk + 1] * rows
    o_ref[...] = x1_ref[...] + gt_ref[...] * _rms(y, np_ref[...])


def _combine(dest_tiles, ys_tiles, gates_t, x1, gt, npost, tm, tiles_per_seq, tile0):
    n_tiles = dest_tiles.shape[0]
    t = n_tiles * tm
    smem_tile = lambda f: pl.BlockSpec((None, 1, TOP_K * tm), f, memory_space=pltpu.SMEM)
    return pl.pallas_call(
        functools.partial(_combine_kernel, tm=tm),
        out_shape=jax.ShapeDtypeStruct((t, D_MODEL), F32),
        grid=(n_tiles,),
        in_specs=[smem_tile(lambda i: (i, 0, 0)),
                  smem_tile(lambda i: (jnp.minimum(i + 1, n_tiles - 1), 0, 0)),
                  pl.BlockSpec(memory_space=pl.ANY),
                  pl.BlockSpec((tm, TOP_K), lambda i: (i + tile0, 0)),
                  pl.BlockSpec((tm, D_MODEL), lambda i: (i + tile0, 0)),
                  _mod_spec(gt, tm, tiles_per_seq),
                  pl.BlockSpec((1, D_MODEL), lambda i: (0, 0))],
        out_specs=pl.BlockSpec((tm, D_MODEL), lambda i: (i, 0)),
        scratch_shapes=[pltpu.VMEM((2 * TOP_K * tm * LANE_CHUNKS, LANES), F32),
                        pltpu.SemaphoreType.DMA((2,))],
        compiler_params=_cparams(("arbitrary",)),
    )(dest_tiles, dest_tiles, ys_tiles, gates_t, x1, gt, npost)


def _expert_layout(counts, n_blocks):
    padded = (counts + MOE_BM - 1) // MOE_BM * MOE_BM
    pad_ends = jnp.cumsum(padded)
    pad_starts = pad_ends - padded
    block_start = jnp.arange(n_blocks, dtype=I32) * MOE_BM
    block_e = jnp.minimum(jnp.sum((pad_ends[None, :] <= block_start[:, None]).astype(I32), axis=1), N_EXPERTS - 1)
    n_used = (pad_ends[-1:] // MOE_BM).astype(I32)
    e_ids = jnp.arange(N_EXPERTS, dtype=I32)
    later = (counts > 0)[None, :] & (e_ids[None, :] > e_ids[:, None])
    next_e = jnp.min(jnp.where(later, e_ids[None, :], N_EXPERTS), axis=1)
    next_e = jnp.where(next_e < N_EXPERTS, next_e, -1).astype(I32)
    return pad_starts, pad_starts + counts, pad_ends, block_e, next_e, n_used


def _slots(idx, rank, pad_starts):
    onehot = idx[:, :, None] == jnp.arange(N_EXPERTS, dtype=I32)[None, None, :]
    return rank + jnp.sum(jnp.where(onehot, pad_starts[None, None, :], 0), axis=-1)


def _tile_major(a, tm):
    t = a.shape[1]
    return a.reshape(TOP_K, t // tm, tm).transpose(1, 0, 2).reshape(t // tm, 1, TOP_K * tm)


def _block_diag(wb):
    n, d, _ = wb.shape
    eye = jnp.eye(n, dtype=wb.dtype)
    return (wb[:, :, None, :] * eye[:, None, :, None]).reshape(n * d, n * d)


def _pick_tile(n, prefs):
    for p in prefs:
        if n % p == 0:
            return p
    raise ValueError(f"no tile for {n}")


def kernel(x_prompt, x_sample, state_rglru_conv, state_rglru_h, state_hgrn_S, c_prompt, c_sample, ada_w, ada_b, norm_pre_mix, norm_post_mix, norm_pre_ffn, norm_post_ffn, w_in, conv_w, conv_b, lru_wa, lru_ba, lru_wi, lru_bi, lru_lambda, hg_lb, hg_norm, w_out, router_w, router_b, w_gu, b_gu, w_down, b_down):
    assert ada_w.shape[0] == 1, "single-layer trunk"
    bp, lp, _ = x_prompt.shape
    bs, ls, _ = x_sample.shape
    tp, ts = bp * lp, bs * ls
    assert CONV_WIDTH - 1 <= ls <= SUBLANES

    w_in_bf = w_in[0].astype(BF16)
    w_out_bf = w_out[0].astype(BF16)
    wg = jnp.concatenate([_block_diag(lru_wa[0]), _block_diag(lru_wi[0])], axis=1).astype(BF16)
    bg = jnp.concatenate([lru_ba[0], lru_bi[0]])[None, :]
    lbs = jnp.cumsum(jax.nn.softmax(hg_lb.astype(F32), axis=0), axis=0)[0][None, :]
    mw = (conv_w[0], conv_b[0][None, :], wg, bg, lru_lambda[0][None, :], lbs, hg_norm[0][None, :])
    row = lambda p: p[0][None, :]

    n_c = bp + bs
    n_c_pad = -(-n_c // SUBLANES) * SUBLANES
    c_all = jnp.concatenate([c_prompt, c_sample, jnp.zeros((n_c_pad - n_c, D_MODEL), F32)], axis=0)
    mod = _ada(c_all, ada_w[0], ada_b[0][None, :])
    mod_p = [m[:, None, :] for m in jnp.split(mod[:bp], 6, axis=-1)]
    mod_s = [jnp.repeat(m, ls, axis=0) for m in jnp.split(mod[bp:n_c], 6, axis=-1)]

    xp = x_prompt.reshape(tp, D_MODEL)
    xs = x_sample.reshape(ts, D_MODEL)
    tm_s = _pick_tile(ts, (256, 128, 64, 32, 16, 8))

    tc = _pick_tile(lp // 2, (256, 128, 64))
    m_p, conv_p, h_p, s_p = _mixer_prompt(xp, mod_p[1], mod_p[0], row(norm_pre_mix), w_in_bf, bp, lp, mw, tc)
    proj_s = _inproj(xs, mod_s[1], mod_s[0], row(norm_pre_mix), w_in_bf, tm_s, 1)

    proj_s_pad = jnp.pad(proj_s.reshape(bs, ls, IN_COLS), ((0, 0), (0, SUBLANES - ls), (0, 0)))
    cst_hdr = jnp.pad(state_rglru_conv[0], ((0, 0), (CONV_HDR - (CONV_WIDTH - 1), 0), (0, 0)))
    m_s_pad, conv_s, h_s, s_s = _mixer_sample(proj_s_pad, cst_hdr, state_rglru_h[0][:, None, :],
                                              state_hgrn_S[0], mw, ls)
    m_s = m_s_pad[:, :ls, :].reshape(ts, D_MODEL)

    tmq = _pick_tile(math.gcd(lp, ts), (512, 256, 128, 64, 32, 16, 8))
    x1, hf, idx, gate, rank, counts = _post(
        m_p, m_s, xp, xs, (mod_p[2], mod_p[4], mod_p[3]), (mod_s[2], mod_s[4], mod_s[3]),
        row(norm_post_mix), row(norm_pre_ffn), w_out_bf, router_w[0].T, router_b[0][:, None], tmq, lp // tmq)

    n_blocks = -(-(TOP_K * (tp + ts)) // MOE_BM) + N_EXPERTS
    n_slots = n_blocks * MOE_BM
    pad_starts, fill_lo, pad_ends, block_e, next_e, n_used = _expert_layout(counts[:, 0].astype(I32), n_blocks)
    fill_hi = pad_ends.at[N_EXPERTS - 1].set(n_slots)
    dest = _slots(idx, rank, pad_starts)
    tmd = _pick_tile(math.gcd(lp, ts), (512, 256, 128, 64, 32, 16, 8))
    dtiles = _tile_major(dest, tmd)
    xs_rows = _dispatch(fill_lo, fill_hi, dtiles, hf, n_slots, tmd)
    ys = _experts(block_e, next_e, n_used, xs_rows.reshape(n_slots * LANE_CHUNKS, LANES),
                  w_gu[0], b_gu[0][:, None, :], w_down[0], b_down[0][:, None, :])
    ys_tiles = ys.reshape(-1, LANE_CHUNKS, LANES)

    gates_t = gate.T
    y_p = _combine(dtiles[:tp // tmd], ys_tiles, gates_t, x1, mod_p[5], row(norm_post_ffn), tmd, lp // tmd, 0)
    y_s = _combine(dtiles[tp // tmd:], ys_tiles, gates_t, x1, mod_s[5], row(norm_post_ffn), tmd, 1, tp // tmd)

    return (y_p.reshape(bp, lp, D_MODEL), y_s.reshape(bs, ls, D_MODEL),
            conv_p[None], h_p.reshape(1, bp, LRU_WIDTH), s_p[None],
            conv_s[None], h_s.reshape(1, bs, LRU_WIDTH), s_s[None])
```

```python
import functools
import math

import jax
import jax.numpy as jnp
from jax import lax
from jax.experimental import pallas as pl
from jax.experimental.pallas import tpu as pltpu

F32 = jnp.float32
BF16 = jnp.bfloat16
I32 = jnp.int32

D_MODEL = 1024
LRU_WIDTH = 512
LRU_BLOCKS = 8
LRU_BLOCK = LRU_WIDTH // LRU_BLOCKS
CONV_WIDTH = 4
LRU_C = 8.0
HG_WIDTH = 512
HG_HEAD_DIM = 128
HG_HEADS = 4
IN_COLS = 2 * LRU_WIDTH + 4 * HG_WIDTH
N_EXPERTS = 32
TOP_K = 4
D_EXPERT = 1024
SWIGLU_LIMIT = 7.0
SWIGLU_ALPHA = 1.702
RMS_EPS = 1e-6

C_XL, C_YL, C_Q, C_F, C_V, C_G = 0, 512, 1024, 1536, 2048, 2560

SUBLANES = 8
LANES = 128
LANE_CHUNKS = D_MODEL // LANES
CONV_HDR = SUBLANES
HG_CHUNK = 64
HG_MAX_EXPONENT = 60.0
MOE_BM = 512
ZERO_ROWS = 64
VMEM_LIMIT = 56 * 1024 * 1024

_NT = (((1,), (1,)), ((), ()))
_TN = (((0,), (0,)), ((), ()))


def _cparams(sem):
    return pltpu.CompilerParams(dimension_semantics=sem, vmem_limit_bytes=VMEM_LIMIT)


def _rms(x, g):
    return x * lax.rsqrt(jnp.mean(x * x, axis=-1, keepdims=True) + RMS_EPS) * g


def _gelu_tanh(x):
    c = math.sqrt(2.0 / math.pi)
    return 0.5 * x * (1.0 + jnp.tanh(c * (x + 0.044715 * (x * x * x))))


def _ada_kernel(c_ref, w_ref, b_ref, o_ref):
    c = c_ref[...]
    s = c * jax.nn.sigmoid(c)
    o_ref[...] = jnp.dot(s.astype(BF16), w_ref[...].astype(BF16), preferred_element_type=F32) + b_ref[...]


def _ada(c_all, ada_w, ada_b):
    n = c_all.shape[0]
    tn = 1024
    return pl.pallas_call(
        _ada_kernel,
        out_shape=jax.ShapeDtypeStruct((n, 6 * D_MODEL), F32),
        grid=(6 * D_MODEL // tn,),
        in_specs=[pl.BlockSpec((n, D_MODEL), lambda j: (0, 0)),
                  pl.BlockSpec((D_MODEL, tn), lambda j: (0, j)),
                  pl.BlockSpec((1, tn), lambda j: (0, j))],
        out_specs=pl.BlockSpec((n, tn), lambda j: (0, j)),
        compiler_params=_cparams(("arbitrary",)),
    )(c_all, ada_w, ada_b)


def _inproj_kernel(x_ref, sc_ref, sh_ref, g_ref, w_ref, o_ref):
    h = _rms(x_ref[...], g_ref[...]) * (1.0 + sc_ref[...]) + sh_ref[...]
    o_ref[...] = jnp.dot(h.astype(BF16), w_ref[...], preferred_element_type=F32)


def _mod_spec(mod, tm, tiles_per_seq):
    if mod.ndim == 3:
        return pl.BlockSpec((None, 1, D_MODEL), lambda i: (i // tiles_per_seq, 0, 0))
    return pl.BlockSpec((tm, D_MODEL), lambda i: (i, 0))


def _inproj(x, sc, sh, g, w_bf, tm, tiles_per_seq):
    t = x.shape[0]
    return pl.pallas_call(
        _inproj_kernel,
        out_shape=jax.ShapeDtypeStruct((t, IN_COLS), F32),
        grid=(t // tm,),
        in_specs=[pl.BlockSpec((tm, D_MODEL), lambda i: (i, 0)),
                  _mod_spec(sc, tm, tiles_per_seq), _mod_spec(sh, tm, tiles_per_seq),
                  pl.BlockSpec((1, D_MODEL), lambda i: (0, 0)),
                  pl.BlockSpec((D_MODEL, IN_COLS), lambda i: (0, 0))],
        out_specs=pl.BlockSpec((tm, IN_COLS), lambda i: (i, 0)),
        compiler_params=_cparams(("arbitrary",)),
    )(x, sc, sh, g, w_bf)


def _sigmoid(x):
    return 0.5 * jnp.tanh(0.5 * x) + 0.5


def _group_rows(x):
    rows, w = x.shape
    return x.reshape(rows // SUBLANES, SUBLANES, w)


def _scan_rows(a, u, h0):
    a3, u3 = _group_rows(a), _group_rows(u)
    r3 = lax.broadcasted_iota(I32, a3.shape, 1)
    s = 1
    while s < SUBLANES:
        keep = r3 >= s
        a_sh = jnp.where(keep, pltpu.roll(a3, s, 1), 1.0)
        u_sh = jnp.where(keep, pltpu.roll(u3, s, 1), 0.0)
        u3 = a3 * u_sh + u3
        a3 = a3 * a_sh
        s *= 2
    out, h = [], h0
    for i in range(a3.shape[0]):
        hg = u3[i] + a3[i] * h
        h = hg[SUBLANES - 1:SUBLANES, :]
        out.append(hg)
    return jnp.concatenate(out, axis=0), h


def _chunk_cumsum(x, chunk):
    x3 = _group_rows(x)
    r3 = lax.broadcasted_iota(I32, x3.shape, 1)
    s = 1
    while s < SUBLANES:
        x3 = x3 + jnp.where(r3 >= s, pltpu.roll(x3, s, 1), 0.0)
        s *= 2
    out, carry = [], None
    for i in range(x3.shape[0]):
        cur = x3[i] if i % (chunk // SUBLANES) == 0 else x3[i] + carry
        carry = cur[SUBLANES - 1:SUBLANES, :]
        out.append(cur)
    return jnp.concatenate(out, axis=0)


def _lru_gates(xc, wg_ref, bg_ref, lam_ref):
    gates = jnp.dot(xc.astype(BF16), wg_ref[...], preferred_element_type=F32) + bg_ref[...]
    r = _sigmoid(gates[:, :LRU_WIDTH])
    ig = _sigmoid(gates[:, LRU_WIDTH:])
    z = -lam_ref[...]
    softplus = jnp.maximum(z, 0.0) + jnp.log1p(jnp.exp(-jnp.abs(z)))
    log_a = -LRU_C * r * softplus
    a = jnp.exp(log_a)
    th = jnp.tanh(log_a)
    mult2 = -2.0 * th / (1.0 - th)
    return a, jnp.sqrt(mult2), ig


def _conv4(xbuf, xl, cw_ref, cb_ref, rows):
    cw = cw_ref[...]
    h = CONV_HDR
    return (cb_ref[...] + cw[0:1] * xbuf[h - 3:h - 3 + rows, :] + cw[1:2] * xbuf[h - 2:h - 2 + rows, :]
            + cw[2:3] * xbuf[h - 1:h - 1 + rows, :] + cw[3:4] * xl)


def _conv4_rolled(xl, prev, cw_ref, cb_ref):
    cw = cw_ref[...]
    row8 = lax.broadcasted_iota(I32, prev.shape, 0)
    acc = cb_ref[...] + cw[CONV_WIDTH - 1:CONV_WIDTH] * xl
    for s in range(1, CONV_WIDTH):
        sh = pltpu.roll(xl, s, 0)
        top = jnp.where(row8 < s, pltpu.roll(prev, s, 0), sh[0:SUBLANES, :])
        sh = jnp.concatenate([top, sh[SUBLANES:, :]], axis=0)
        acc = acc + cw[CONV_WIDTH - 1 - s:CONV_WIDTH - s] * sh
    return acc


def _head_norm_gate(o, gate, hgn):
    return _rms(o, hgn) * _sigmoid(gate)


def _hgrn_window(q, k, v, b, state, steps, state_is_transposed):
    row = lax.broadcasted_iota(I32, q.shape, 0)
    valid = row < steps
    b_end = b[steps - 1:steps, :]
    q_in = jnp.where(valid, q * jnp.exp(b), 0.0).astype(BF16)
    if state_is_transposed:
        o = lax.dot_general(q_in, state.astype(BF16), _NT, preferred_element_type=F32)
    else:
        o = jnp.dot(q_in, state.astype(BF16), preferred_element_type=F32)
    for d in range(steps):
        ok = valid & (row >= d)
        k_d = k if d == 0 else pltpu.roll(k, d, 0)
        v_d = v if d == 0 else pltpu.roll(v, d, 0)
        b_d = b if d == 0 else pltpu.roll(b, d, 0)
        decay = jnp.exp(jnp.where(ok, b - b_d, 0.0))
        wgt = jnp.sum(jnp.where(ok, q * k_d * decay, 0.0), axis=-1, keepdims=True)
        o = o + wgt * v_d
    k_out = jnp.where(valid, k * jnp.exp(jnp.where(valid, b_end - b, 0.0)), 0.0).astype(BF16)
    v_ok = jnp.where(valid, v, 0.0).astype(BF16)
    keep = jnp.exp(b_end)
    if state_is_transposed:
        s_new = state * keep + lax.dot_general(v_ok, k_out, _TN, preferred_element_type=F32)
    else:
        keep_col = jnp.broadcast_to(keep, q.shape).T[:, 0:1]
        s_new = state * keep_col + lax.dot_general(k_out, v_ok, _TN, preferred_element_type=F32)
    return o, s_new


def _tile_cumsum(x):
    row = lax.broadcasted_iota(I32, x.shape, 0)
    s = 1
    while s < SUBLANES:
        x = x + jnp.where(row >= s, pltpu.roll(x, s, 0), 0.0)
        s *= 2
    return x


def _mix_tile(proj_ref, cw_ref, cb_ref, wg_ref, bg_ref, lam_ref, lb_ref, hgn_ref, m_ref, xprev, hcar, st, st0, obuf,
              seq_start, tc, side_work):
    w = LRU_WIDTH
    side_work = iter(side_work)

    next(side_work)()
    xl = proj_ref[:, C_XL:C_XL + w]
    xc = _conv4_rolled(xl, xprev[...], cw_ref, cb_ref)
    xprev[...] = xl[tc - SUBLANES:tc, :]
    a, mult, ig = _lru_gates(xc, wg_ref, bg_ref, lam_ref)
    row = lax.broadcasted_iota(I32, (tc, w), 0)
    if seq_start is not None:
        mult = jnp.where((row == 0) & seq_start, 1.0, mult)
    h, h_last = _scan_rows(a, mult * ig * xc, hcar[0:1, :])
    hcar[0:1, :] = h_last
    m_ref[:, 0:w] = (h * _gelu_tanh(proj_ref[:, C_YL:C_YL + w])).astype(BF16)

    next(side_work)()
    lb = lb_ref[...]
    f = lb + (1.0 - lb) * _sigmoid(proj_ref[:, C_F:C_F + w])
    k = 1.0 - f
    b = _chunk_cumsum(jnp.log(f), HG_CHUNK)
    q = proj_ref[:, C_Q:C_Q + w]
    v = proj_ref[:, C_V:C_V + w]
    mid = HG_CHUNK // 2
    st0[...] = st[...]
    tri = (lax.broadcasted_iota(I32, (HG_CHUNK, HG_CHUNK), 0)
           >= lax.broadcasted_iota(I32, (HG_CHUNK, HG_CHUNK), 1))
    states = [st[hd] for hd in range(HG_HEADS)]
    for c in range(tc // HG_CHUNK):
        next(side_work)()
        rs = slice(c * HG_CHUNK, (c + 1) * HG_CHUNK)
        for hd in range(HG_HEADS):
            cs = slice(hd * HG_HEAD_DIM, (hd + 1) * HG_HEAD_DIM)
            s_t = states[hd]
            bc, qc, kc = b[rs, cs], q[rs, cs], k[rs, cs]
            vb = v[rs, cs].astype(BF16)
            b_mid = bc[mid - 1:mid, :]
            b_end = bc[HG_CHUNK - 1:HG_CHUNK, :]
            qp = (qc * jnp.exp(bc - b_mid)).astype(BF16)
            kp = (kc * jnp.exp(b_mid - bc)).astype(BF16)
            att = lax.dot_general(qp, kp, _NT, preferred_element_type=F32)
            att = jnp.where(tri, att, 0.0)
            q_in = (qc * jnp.exp(bc)).astype(BF16)
            o = (jnp.dot(att.astype(BF16), vb, preferred_element_type=F32)
                 + lax.dot_general(q_in, s_t.astype(BF16), _NT, preferred_element_type=F32))
            k_out = (kc * jnp.exp(b_end - bc)).astype(BF16)
            states[hd] = s_t * jnp.exp(b_end) + lax.dot_general(vb, k_out, _TN, preferred_element_type=F32)
            obuf[rs, cs] = o
    for hd in range(HG_HEADS):
        st[hd] = states[hd]

    half_decay = None
    for c in range(tc // HG_CHUNK):
        b_mid = b[c * HG_CHUNK + mid - 1:c * HG_CHUNK + mid, :]
        b_end = b[(c + 1) * HG_CHUNK - 1:(c + 1) * HG_CHUNK, :]
        worst = jnp.maximum(-b_mid, b_mid - b_end)
        half_decay = worst if half_decay is None else jnp.maximum(half_decay, worst)

    @pl.when(jnp.logical_not(jnp.max(half_decay) < HG_MAX_EXPONENT))
    def _():
        st[...] = st0[...]

        def window(gi, carry):
            rows = pl.ds(pl.multiple_of(gi * SUBLANES, SUBLANES), SUBLANES)
            f8 = lb + (1.0 - lb) * _sigmoid(proj_ref[rows, C_F:C_F + w])
            b8 = _tile_cumsum(jnp.log(f8))
            k8 = 1.0 - f8
            q8 = proj_ref[rows, C_Q:C_Q + w]
            v8 = proj_ref[rows, C_V:C_V + w]
            for hd in range(HG_HEADS):
                cs = slice(hd * HG_HEAD_DIM, (hd + 1) * HG_HEAD_DIM)
                o, s_new = _hgrn_window(q8[:, cs], k8[:, cs], v8[:, cs], b8[:, cs], st[hd], SUBLANES, True)
                st[hd] = s_new
                obuf[rows, cs] = o
            return carry
        lax.fori_loop(0, tc // SUBLANES, window, 0)

    hgn = hgn_ref[...]
    for hd in range(HG_HEADS):
        cs = slice(hd * HG_HEAD_DIM, (hd + 1) * HG_HEAD_DIM)
        gate = proj_ref[:, C_G + hd * HG_HEAD_DIM:C_G + (hd + 1) * HG_HEAD_DIM]
        m_ref[:, w + hd * HG_HEAD_DIM:w + (hd + 1) * HG_HEAD_DIM] = (
            _head_norm_gate(obuf[:, cs], gate, hgn).astype(BF16))


def _mixer_prompt_kernel(x_ref, xn_ref, sc_ref, sh_ref, scn_ref, shn_ref, g_ref, win_ref, cw_ref, cb_ref, wg_ref,
                         bg_ref, lam_ref, lb_ref, hgn_ref, m_ref, conv_ref, h_ref, s_ref,
                         proj_a, proj_b, hn_buf, xprev, hcar, st, st0, obuf, *, tc):
    j = pl.program_id(1)
    n_phases = 2 + tc // HG_CHUNK
    cols = IN_COLS // n_phases

    def project(x, sc, sh, out_ref):
        def chunk(c):
            if c == 0:
                hn_buf[...] = (_rms(x(), g_ref[...]) * (1.0 + sc[...]) + sh[...]).astype(BF16)
            out_ref[:, c * cols:(c + 1) * cols] = jnp.dot(hn_buf[...], win_ref[:, c * cols:(c + 1) * cols],
                                                          preferred_element_type=F32)
        return [functools.partial(chunk, c) for c in range(n_phases)]

    def mix(proj_ref, first_row, seq_start, side_work):
        _mix_tile(proj_ref, cw_ref, cb_ref, wg_ref, bg_ref, lam_ref, lb_ref, hgn_ref,
                  m_ref.at[pl.ds(first_row, tc), :], xprev, hcar, st, st0, obuf, seq_start, tc, side_work)

    @pl.when((pl.program_id(0) == 0) & (j == 0))
    def _():
        for thunk in project(lambda: x_ref[0:tc, :], sc_ref, sh_ref, proj_a):
            thunk()

    @pl.when(j == 0)
    def _():
        xprev[...] = jnp.zeros_like(xprev)
        hcar[...] = jnp.zeros_like(hcar)
        st[...] = jnp.zeros_like(st)

    mix(proj_a, 0, j == 0, project(lambda: x_ref[tc:2 * tc, :], sc_ref, sh_ref, proj_b))
    mix(proj_b, tc, None, project(lambda: xn_ref[...], scn_ref, shn_ref, proj_a))

    @pl.when(j == pl.num_programs(1) - 1)
    def _():
        conv_ref[...] = xprev[SUBLANES - (CONV_WIDTH - 1):SUBLANES, :]
        h_ref[...] = hcar[0:1, :]
        for hd in range(HG_HEADS):
            s_ref[hd] = st[hd].T


def _mixer_prompt(x, sc, sh, g, w_in_bf, nb, seq, mw, tc):
    cw, cb, wg, bg, lam, lb, hgn = mw
    nj = seq // (2 * tc)
    last_tile = nb * seq // tc - 1
    next_tile = lambda b, j: jnp.minimum(2 * (b * nj + j) + 2, last_tile)
    const = lambda shape: pl.BlockSpec(shape, lambda b, j: tuple(0 for _ in shape))
    per_seq = lambda: pl.BlockSpec((None, 1, D_MODEL), lambda b, j: (b, 0, 0))
    next_seq = lambda: pl.BlockSpec((None, 1, D_MODEL), lambda b, j: (next_tile(b, j) // (2 * nj), 0, 0))
    return pl.pallas_call(
        functools.partial(_mixer_prompt_kernel, tc=tc),
        out_shape=(jax.ShapeDtypeStruct((nb * seq, D_MODEL), BF16),
                   jax.ShapeDtypeStruct((nb, CONV_WIDTH - 1, LRU_WIDTH), F32),
                   jax.ShapeDtypeStruct((nb, 1, LRU_WIDTH), F32),
                   jax.ShapeDtypeStruct((nb, HG_HEADS, HG_HEAD_DIM, HG_HEAD_DIM), F32)),
        grid=(nb, nj),
        in_specs=[pl.BlockSpec((2 * tc, D_MODEL), lambda b, j: (b * nj + j, 0)),
                  pl.BlockSpec((tc, D_MODEL), lambda b, j: (next_tile(b, j), 0)),
                  per_seq(), per_seq(), next_seq(), next_seq(), const((1, D_MODEL)), const((D_MODEL, IN_COLS)),
                  const((CONV_WIDTH, LRU_WIDTH)), const((1, LRU_WIDTH)),
                  const((LRU_WIDTH, 2 * LRU_WIDTH)), const((1, 2 * LRU_WIDTH)),
                  const((1, LRU_WIDTH)), const((1, HG_WIDTH)), const((1, HG_HEAD_DIM))],
        out_specs=(pl.BlockSpec((2 * tc, D_MODEL), lambda b, j: (b * nj + j, 0)),
                   pl.BlockSpec((None, CONV_WIDTH - 1, LRU_WIDTH), lambda b, j: (b, 0, 0)),
                   pl.BlockSpec((None, 1, LRU_WIDTH), lambda b, j: (b, 0, 0)),
                   pl.BlockSpec((None, HG_HEADS, HG_HEAD_DIM, HG_HEAD_DIM), lambda b, j: (b, 0, 0, 0))),
        scratch_shapes=[pltpu.VMEM((tc, IN_COLS), F32),
                        pltpu.VMEM((tc, IN_COLS), F32),
                        pltpu.VMEM((tc, D_MODEL), BF16),
                        pltpu.VMEM((SUBLANES, LRU_WIDTH), F32),
                        pltpu.VMEM((SUBLANES, LRU_WIDTH), F32),
                        pltpu.VMEM((HG_HEADS, HG_HEAD_DIM, HG_HEAD_DIM), F32),
                        pltpu.VMEM((HG_HEADS, HG_HEAD_DIM, HG_HEAD_DIM), F32),
                        pltpu.VMEM((tc, HG_WIDTH), F32)],
        compiler_params=_cparams(("arbitrary", "arbitrary")),
    )(x, x, sc, sh, sc, sh, g, w_in_bf, cw, cb, wg, bg, lam, lb, hgn)


def _mixer_sample_kernel(proj_ref, cst_ref, h0_ref, s0_ref, cw_ref, cb_ref, wg_ref, bg_ref, lam_ref, lb_ref,
                         hgn_ref, m_ref, conv_ref, h_ref, s_ref, xbuf, *, steps):
    for bi in range(proj_ref.shape[0]):
        _mixer_sample_one(proj_ref.at[bi], cst_ref.at[bi], h0_ref.at[bi], s0_ref.at[bi], cw_ref, cb_ref, wg_ref,
                          bg_ref, lam_ref, lb_ref, hgn_ref, m_ref.at[bi], conv_ref.at[bi], h_ref.at[bi],
                          s_ref.at[bi], xbuf.at[bi], steps)


def _mixer_sample_one(proj_ref, cst_ref, h0_ref, s0_ref, cw_ref, cb_ref, wg_ref, bg_ref, lam_ref, lb_ref,
                      hgn_ref, m_ref, conv_ref, h_ref, s_ref, xbuf, steps):
    w = LRU_WIDTH
    rows = SUBLANES
    row = lax.broadcasted_iota(I32, (rows, w), 0)
    valid = row < steps

    xl = proj_ref[:, C_XL:C_XL + w]
    xbuf[0:CONV_HDR, :] = cst_ref[...]
    xbuf[CONV_HDR:CONV_HDR + rows, :] = xl
    xc = _conv4(xbuf, xl, cw_ref, cb_ref, rows)
    conv_ref[...] = xbuf[CONV_HDR + steps - 3:CONV_HDR + steps, :]
    a, mult, ig = _lru_gates(xc, wg_ref, bg_ref, lam_ref)
    u = mult * ig * xc
    h = h0_ref[...]
    hs = jnp.zeros((rows, w), F32)
    for t in range(steps):
        h = a[t:t + 1, :] * h + u[t:t + 1, :]
        hs = jnp.where(row == t, h, hs)
    h_ref[...] = h
    m_ref[:, 0:w] = (hs * _gelu_tanh(proj_ref[:, C_YL:C_YL + w])).astype(BF16)

    lb = lb_ref[...]
    f = lb + (1.0 - lb) * _sigmoid(proj_ref[:, C_F:C_F + w])
    k = 1.0 - f
    b = _tile_cumsum(jnp.log(f))
    q = proj_ref[:, C_Q:C_Q + w]
    v = proj_ref[:, C_V:C_V + w]
    hgn = hgn_ref[...]
    for hd in range(HG_HEADS):
        cs = slice(hd * HG_HEAD_DIM, (hd + 1) * HG_HEAD_DIM)
        o, s_new = _hgrn_window(q[:, cs], k[:, cs], v[:, cs], b[:, cs], s0_ref[hd], steps, False)
        s_ref[hd] = s_new
        gate = proj_ref[:, C_G + hd * HG_HEAD_DIM:C_G + (hd + 1) * HG_HEAD_DIM]
        m_ref[:, w + hd * HG_HEAD_DIM:w + (hd + 1) * HG_HEAD_DIM] = _head_norm_gate(o, gate, hgn).astype(BF16)


def _mixer_sample(proj_pad, cst_hdr, h0, s0, mw, steps):
    cw, cb, wg, bg, lam, lb, hgn = mw
    nb = proj_pad.shape[0]
    per_step = _pick_tile(nb, (8, 4, 2, 1))
    const = lambda shape: pl.BlockSpec(shape, lambda b: tuple(0 for _ in shape))
    per_b = lambda shape: pl.BlockSpec((per_step,) + shape, lambda b: (b,) + tuple(0 for _ in shape))
    return pl.pallas_call(
        functools.partial(_mixer_sample_kernel, steps=steps),
        out_shape=(jax.ShapeDtypeStruct((nb, SUBLANES, D_MODEL), BF16),
                   jax.ShapeDtypeStruct((nb, CONV_WIDTH - 1, LRU_WIDTH), F32),
                   jax.ShapeDtypeStruct((nb, 1, LRU_WIDTH), F32),
                   jax.ShapeDtypeStruct((nb, HG_HEADS, HG_HEAD_DIM, HG_HEAD_DIM), F32)),
        grid=(nb // per_step,),
        in_specs=[per_b((SUBLANES, IN_COLS)), per_b((CONV_HDR, LRU_WIDTH)), per_b((1, LRU_WIDTH)),
                  per_b((HG_HEADS, HG_HEAD_DIM, HG_HEAD_DIM)),
                  const((CONV_WIDTH, LRU_WIDTH)), const((1, LRU_WIDTH)),
                  const((LRU_WIDTH, 2 * LRU_WIDTH)), const((1, 2 * LRU_WIDTH)),
                  const((1, LRU_WIDTH)), const((1, HG_WIDTH)), const((1, HG_HEAD_DIM))],
        out_specs=(per_b((SUBLANES, D_MODEL)), per_b((CONV_WIDTH - 1, LRU_WIDTH)), per_b((1, LRU_WIDTH)),
                   per_b((HG_HEADS, HG_HEAD_DIM, HG_HEAD_DIM))),
        scratch_shapes=[pltpu.VMEM((per_step, CONV_HDR + SUBLANES, LRU_WIDTH), F32)],
        compiler_params=_cparams(("arbitrary",)),
    )(proj_pad, cst_hdr, h0, s0, cw, cb, wg, bg, lam, lb, hgn)


def _to_row_tiles(ref, val, rows):
    for c in range(LANE_CHUNKS):
        ref[pl.ds(c, rows, stride=LANE_CHUNKS), :] = val[:, c * LANES:(c + 1) * LANES]


def _from_row_tiles(ref, base, rows):
    return jnp.concatenate(
        [ref[pl.ds(base + c, rows, stride=LANE_CHUNKS), :] for c in range(LANE_CHUNKS)], axis=-1)


def _post_kernel(mp_ref, ms_ref, xp_ref, xs_ref, gtmp_ref, scfp_ref, shfp_ref, gtms_ref, scfs_ref, shfs_ref,
                 npost_ref, npre_ref, wout_ref, rwt_ref, rb_ref,
                 x1_ref, hf_ref, idx_ref, gate_ref, rank_ref, cnt_out_ref, cnt, *, n_p):
    i = pl.program_id(0)
    tm = xp_ref.shape[0]

    @pl.when(i == 0)
    def _():
        cnt[...] = jnp.zeros_like(cnt)

    def run(m_ref, x_ref, gtm_ref, scf_ref, shf_ref):
        mix = jnp.dot(m_ref[...], wout_ref[...], preferred_element_type=F32)
        x1 = x_ref[...] + gtm_ref[...] * _rms(mix, npost_ref[...])
        x1_ref[...] = x1
        hf = _rms(x1, npre_ref[...]) * (1.0 + scf_ref[...]) + shf_ref[...]
        _to_row_tiles(hf_ref, hf, tm)
        logits = lax.dot_general(rwt_ref[...], hf, _NT, precision=lax.Precision.HIGHEST,
                                 preferred_element_type=F32) + rb_ref[...]
        e_iota = lax.broadcasted_iota(I32, logits.shape, 0)
        vals, idxs = [], []
        for _ in range(TOP_K):
            mx = jnp.max(logits, axis=0, keepdims=True)
            ix = jnp.min(jnp.where(logits == mx, e_iota, N_EXPERTS), axis=0, keepdims=True)
            vals.append(mx)
            idxs.append(ix)
            logits = jnp.where(e_iota == ix, -jnp.inf, logits)
        exps = [jnp.exp(vv - vals[0]) for vv in vals]
        den = exps[0] + exps[1] + exps[2] + exps[3]
        earlier = (lax.broadcasted_iota(I32, (tm, tm), 0) < lax.broadcasted_iota(I32, (tm, tm), 1))
        earlier = jnp.where(earlier, 1.0, 0.0).astype(BF16)
        base = cnt[...]
        for kk in range(TOP_K):
            idx_ref[kk:kk + 1, :] = idxs[kk]
            gate_ref[kk:kk + 1, :] = exps[kk] / den
            hit = e_iota == idxs[kk]
            onehot = jnp.where(hit, 1.0, 0.0)
            before = jnp.dot(onehot.astype(BF16), earlier, preferred_element_type=F32)
            rank = jnp.sum(jnp.where(hit, before + base, 0.0), axis=0, keepdims=True)
            rank_ref[kk:kk + 1, :] = rank.astype(I32)
            base = base + jnp.sum(onehot, axis=1, keepdims=True)
        cnt[...] = base

    @pl.when(i < n_p)
    def _():
        run(mp_ref, xp_ref, gtmp_ref, scfp_ref, shfp_ref)

    @pl.when(i >= n_p)
    def _():
        run(ms_ref, xs_ref, gtms_ref, scfs_ref, shfs_ref)

    @pl.when(i == pl.num_programs(0) - 1)
    def _():
        cnt_out_ref[...] = cnt[...]


def _post(m_p, m_s, x_p, x_s, mod_p, mod_s, npost, npre, wout_bf, rwt, rb, tm, tiles_per_seq):
    n_p, n_s = x_p.shape[0] // tm, x_s.shape[0] // tm
    t = (n_p + n_s) * tm
    first = lambda i: jnp.minimum(i, n_p - 1)
    second = lambda i: jnp.maximum(i - n_p, 0)
    row = lambda: pl.BlockSpec((1, D_MODEL), lambda i: (0, 0))
    tok_p = lambda: pl.BlockSpec((tm, D_MODEL), lambda i: (first(i), 0))
    tok_s = lambda: pl.BlockSpec((tm, D_MODEL), lambda i: (second(i), 0))
    seq_p = lambda: pl.BlockSpec((None, 1, D_MODEL), lambda i: (first(i) // tiles_per_seq, 0, 0))
    per_k = lambda: pl.BlockSpec((TOP_K, tm), lambda i: (0, i))
    return pl.pallas_call(
        functools.partial(_post_kernel, n_p=n_p),
        out_shape=(jax.ShapeDtypeStruct((t, D_MODEL), F32),
                   jax.ShapeDtypeStruct((t * LANE_CHUNKS, LANES), F32),
                   jax.ShapeDtypeStruct((TOP_K, t), I32), jax.ShapeDtypeStruct((TOP_K, t), F32),
                   jax.ShapeDtypeStruct((TOP_K, t), I32), jax.ShapeDtypeStruct((N_EXPERTS, 1), F32)),
        grid=(n_p + n_s,),
        in_specs=[tok_p(), tok_s(), tok_p(), tok_s(), seq_p(), seq_p(), seq_p(), tok_s(), tok_s(), tok_s(),
                  row(), row(),
                  pl.BlockSpec((D_MODEL, D_MODEL), lambda i: (0, 0)),
                  pl.BlockSpec((N_EXPERTS, D_MODEL), lambda i: (0, 0)),
                  pl.BlockSpec((N_EXPERTS, 1), lambda i: (0, 0))],
        out_specs=(pl.BlockSpec((tm, D_MODEL), lambda i: (i, 0)),
                   pl.BlockSpec((tm * LANE_CHUNKS, LANES), lambda i: (i, 0)),
                   per_k(), per_k(), per_k(), pl.BlockSpec((N_EXPERTS, 1), lambda i: (0, 0))),
        scratch_shapes=[pltpu.VMEM((N_EXPERTS, 1), F32)],
        compiler_params=_cparams(("arbitrary",)),
    )(m_p, m_s, x_p, x_s, *mod_p, *mod_s, npost, npre, wout_bf, rwt, rb)


def _dispatch_kernel(fill_lo_ref, fill_hi_ref, dest_ref, hf_ref, xs_hbm, zeros, row_sem, pad_sem, *, tm):
    i = pl.program_id(0)
    n_rows = TOP_K * tm

    @pl.when(i == 0)
    def _():
        zeros[...] = jnp.zeros_like(zeros)

        def per_expert(e, totals):
            lo, hi = fill_lo_ref[e], fill_hi_ref[e]
            mid = jnp.minimum((lo + ZERO_ROWS - 1) // ZERO_ROWS * ZERO_ROWS, hi)

            def per_row(s, carry):
                pltpu.make_async_copy(zeros.at[0], xs_hbm.at[s], pad_sem).start()
                return carry
            lax.fori_loop(lo, mid, per_row, 0)

            def per_chunk(c, carry):
                pltpu.make_async_copy(zeros, xs_hbm.at[pl.ds(pl.multiple_of(c * ZERO_ROWS, ZERO_ROWS), ZERO_ROWS)],
                                      pad_sem).start()
                return carry
            lax.fori_loop(mid // ZERO_ROWS, hi // ZERO_ROWS, per_chunk, 0)
            return totals[0] + (mid - lo), totals[1] + (hi - mid) // ZERO_ROWS
        n_rows_filled, n_chunks_filled = lax.fori_loop(0, N_EXPERTS, per_expert, (0, 0))

        def drain_row(s, carry):
            pltpu.make_async_copy(zeros.at[0], xs_hbm.at[0], pad_sem).wait()
            return carry
        lax.fori_loop(0, n_rows_filled, drain_row, 0)

        def drain_chunk(c, carry):
            pltpu.make_async_copy(zeros, xs_hbm.at[pl.ds(0, ZERO_ROWS)], pad_sem).wait()
            return carry
        lax.fori_loop(0, n_chunks_filled, drain_chunk, 0)

    for n in range(n_rows):
        pltpu.make_async_copy(hf_ref.at[pl.ds((n % tm) * LANE_CHUNKS, LANE_CHUNKS), :],
                              xs_hbm.at[dest_ref[0, n]], row_sem).start(priority=n % 2)
    pltpu.make_async_copy(xs_hbm.at[pl.ds(0, n_rows)], xs_hbm.at[pl.ds(0, n_rows)], row_sem).wait()


def _dispatch(fill_lo, fill_hi, dest_tiles, hf_rows, n_slots, tm):
    return pl.pallas_call(
        functools.partial(_dispatch_kernel, tm=tm),
        out_shape=jax.ShapeDtypeStruct((n_slots, LANE_CHUNKS, LANES), F32),
        grid_spec=pltpu.PrefetchScalarGridSpec(
            num_scalar_prefetch=2, grid=(dest_tiles.shape[0],),
            in_specs=[pl.BlockSpec((None, 1, TOP_K * tm), lambda i, lo, hi: (i, 0, 0), memory_space=pltpu.SMEM),
                      pl.BlockSpec((tm * LANE_CHUNKS, LANES), lambda i, lo, hi: (i, 0))],
            out_specs=pl.BlockSpec(memory_space=pl.ANY),
            scratch_shapes=[pltpu.VMEM((ZERO_ROWS, LANE_CHUNKS, LANES), F32),
                            pltpu.SemaphoreType.DMA, pltpu.SemaphoreType.DMA]),
        compiler_params=_cparams(("arbitrary",)),
    )(fill_lo, fill_hi, dest_tiles, hf_rows)


def _expert_kernel(be_ref, half_ref, nxt_ref, nu_ref, xs_ref, wgu_hbm, bgu_ref, wd_hbm, bd_ref, ys_ref,
                   wgu_f32, wd_f32, wgu_bf, wd_bf, w_sem):
    i = pl.program_id(0)
    n_used = nu_ref[0]

    def weight_copies(e):
        return (pltpu.make_async_copy(wgu_hbm.at[e], wgu_f32, w_sem.at[0]),
                pltpu.make_async_copy(wd_hbm.at[e], wd_f32, w_sem.at[1]))

    @pl.when(i == 0)
    def _():
        for cp in weight_copies(be_ref[0]):
            cp.start()

    @pl.when(i < n_used)
    def _():
        prev = be_ref[jnp.maximum(i - 1, 0)]

        @pl.when((i == 0) | (be_ref[i] != prev))
        def _():
            for cp in weight_copies(be_ref[i]):
                cp.wait()
            wgu_bf[...] = wgu_f32[...].astype(BF16)
            wd_bf[...] = wd_f32[...].astype(BF16)
            nxt = nxt_ref[be_ref[i]]

            @pl.when(nxt >= 0)
            def _():
                for cp in weight_copies(nxt):
                    cp.start()

        def swiglu_rows(rows):
            x = _from_row_tiles(xs_ref, 0, rows).astype(BF16)
            hu = jnp.dot(x, wgu_bf[...], preferred_element_type=F32) + bgu_ref[...]
            glu = jnp.minimum(hu[:, :D_EXPERT], SWIGLU_LIMIT)
            lin = jnp.clip(hu[:, D_EXPERT:], -SWIGLU_LIMIT, SWIGLU_LIMIT)
            act = glu * jax.nn.sigmoid(SWIGLU_ALPHA * glu) * (lin + 1.0)
            y = jnp.dot(act.astype(BF16), wd_bf[...], preferred_element_type=F32) + bd_ref[...]
            _to_row_tiles(ys_ref, y, rows)

        @pl.when(half_ref[i] == 0)
        def _():
            swiglu_rows(MOE_BM)

        @pl.when(half_ref[i] != 0)
        def _():
            swiglu_rows(MOE_BM // 2)
            ys_ref[MOE_BM // 2 * LANE_CHUNKS:, :] = jnp.zeros((MOE_BM // 2 * LANE_CHUNKS, LANES), F32)

    @pl.when(i >= n_used)
    def _():
        ys_ref[...] = jnp.zeros_like(ys_ref)


def _experts(block_e, block_half, next_e, n_used, xs2d, w_gu, b_gu, w_down, b_down):
    n_blocks = block_e.shape[0]
    rows = MOE_BM * LANE_CHUNKS
    live = lambda i, be, hf, nx, nu: (jnp.minimum(i, nu[0] - 1), 0)
    return pl.pallas_call(
        _expert_kernel,
        out_shape=jax.ShapeDtypeStruct(xs2d.shape, F32),
        grid_spec=pltpu.PrefetchScalarGridSpec(
            num_scalar_prefetch=4,
            grid=(n_blocks,),
            in_specs=[pl.BlockSpec((rows, LANES), live),
                      pl.BlockSpec(memory_space=pl.ANY),
                      pl.BlockSpec((None, 1, 2 * D_EXPERT), lambda i, be, hf, nx, nu: (be[i], 0, 0)),
                      pl.BlockSpec(memory_space=pl.ANY),
                      pl.BlockSpec((None, 1, D_MODEL), lambda i, be, hf, nx, nu: (be[i], 0, 0))],
            out_specs=pl.BlockSpec((rows, LANES), lambda i, be, hf, nx, nu: (i, 0)),
            scratch_shapes=[pltpu.VMEM((D_MODEL, 2 * D_EXPERT), F32),
                            pltpu.VMEM((D_EXPERT, D_MODEL), F32),
                            pltpu.VMEM((D_MODEL, 2 * D_EXPERT), BF16),
                            pltpu.VMEM((D_EXPERT, D_MODEL), BF16),
                            pltpu.SemaphoreType.DMA((2,))]),
        compiler_params=_cparams(("arbitrary",)),
    )(block_e, block_half, next_e, n_used, xs2d, w_gu, b_gu, w_down, b_down)


def _combine_kernel(dcur_ref, dnext_ref, ys_hbm, g_ref, x1_ref, gt_ref, np_ref, o_ref, ybuf, row_sem, *, tm):
    i = pl.program_id(0)
    n_rows = TOP_K * tm
    slot = i % 2

    def slot_rows(s):
        return ybuf.at[pl.ds(pl.multiple_of(s * n_rows * LANE_CHUNKS, n_rows * LANE_CHUNKS), n_rows * LANE_CHUNKS), :]

    def issue(d_ref, s):
        dst = slot_rows(s)
        for n in range(n_rows):
            pltpu.make_async_copy(ys_hbm.at[d_ref[0, n]], dst.at[pl.ds(n * LANE_CHUNKS, LANE_CHUNKS), :],
                                  row_sem.at[s]).start(priority=n % 2)

    @pl.when(i == 0)
    def _():
        issue(dcur_ref, 0)

    @pl.when(i + 1 < pl.num_programs(0))
    def _():
        issue(dnext_ref, 1 - slot)

    pltpu.make_async_copy(slot_rows(slot), slot_rows(slot), row_sem.at[slot]).wait()
    g = g_ref[...]
    y = None
    for kk in range(TOP_K):
        rows = _from_row_tiles(ybuf, (slot * n_rows + kk * tm) * LANE_CHUNKS, tm)
        y = g[:, kk:kk + 1] * rows if y is None else y + g[:, kk:kk + 1] * rows
    o_ref[...] = x1_ref[...] + gt_ref[...] * _rms(y, np_ref[...])


def _combine(dest_tiles, ys_tiles, gates_t, x1, gt, npost, tm, tiles_per_seq, tile0):
    n_tiles = dest_tiles.shape[0]
    t = n_tiles * tm
    smem_tile = lambda f: pl.BlockSpec((None, 1, TOP_K * tm), f, memory_space=pltpu.SMEM)
    return pl.pallas_call(
        functools.partial(_combine_kernel, tm=tm),
        out_shape=jax.ShapeDtypeStruct((t, D_MODEL), F32),
        grid=(n_tiles,),
        in_specs=[smem_tile(lambda i: (i, 0, 0)),
                  smem_tile(lambda i: (jnp.minimum(i + 1, n_tiles - 1), 0, 0)),
                  pl.BlockSpec(memory_space=pl.ANY),
                  pl.BlockSpec((tm, TOP_K), lambda i: (i + tile0, 0)),
                  pl.BlockSpec((tm, D_MODEL), lambda i: (i + tile0, 0)),
                  _mod_spec(gt, tm, tiles_per_seq),
                  pl.BlockSpec((1, D_MODEL), lambda i: (0, 0))],
        out_specs=pl.BlockSpec((tm, D_MODEL), lambda i: (i, 0)),
        scratch_shapes=[pltpu.VMEM((2 * TOP_K * tm * LANE_CHUNKS, LANES), F32),
                        pltpu.SemaphoreType.DMA((2,))],
        compiler_params=_cparams(("arbitrary",)),
    )(dest_tiles, dest_tiles, ys_tiles, gates_t, x1, gt, npost)


def _expert_layout(counts, n_blocks):
    padded = (counts + MOE_BM - 1) // MOE_BM * MOE_BM
    pad_ends = jnp.cumsum(padded)
    pad_starts = pad_ends - padded
    block_start = jnp.arange(n_blocks, dtype=I32) * MOE_BM
    block_e = jnp.minimum(jnp.sum((pad_ends[None, :] <= block_start[:, None]).astype(I32), axis=1), N_EXPERTS - 1)
    n_used = (pad_ends[-1:] // MOE_BM).astype(I32)
    e_ids = jnp.arange(N_EXPERTS, dtype=I32)
    rows_end = jnp.sum(jnp.where(block_e[:, None] == e_ids[None, :], (pad_starts + counts)[None, :], 0), axis=1)
    block_half = (rows_end - block_start <= MOE_BM // 2).astype(I32)
    later = (counts > 0)[None, :] & (e_ids[None, :] > e_ids[:, None])
    next_e = jnp.min(jnp.where(later, e_ids[None, :], N_EXPERTS), axis=1)
    next_e = jnp.where(next_e < N_EXPERTS, next_e, -1).astype(I32)
    return pad_starts, pad_starts + counts, pad_ends, block_e, block_half, next_e, n_used


def _slots(idx, rank, pad_starts):
    onehot = idx[:, :, None] == jnp.arange(N_EXPERTS, dtype=I32)[None, None, :]
    return rank + jnp.sum(jnp.where(onehot, pad_starts[None, None, :], 0), axis=-1)


def _tile_major(a, tm):
    t = a.shape[1]
    return a.reshape(TOP_K, t // tm, tm).transpose(1, 0, 2).reshape(t // tm, 1, TOP_K * tm)


def _block_diag(wb):
    n, d, _ = wb.shape
    eye = jnp.eye(n, dtype=wb.dtype)
    return (wb[:, :, None, :] * eye[:, None, :, None]).reshape(n * d, n * d)


def _pick_tile(n, prefs):
    for p in prefs:
        if n % p == 0:
            return p
    raise ValueError(f"no tile for {n}")


def kernel(x_prompt, x_sample, state_rglru_conv, state_rglru_h, state_hgrn_S, c_prompt, c_sample, ada_w, ada_b, norm_pre_mix, norm_post_mix, norm_pre_ffn, norm_post_ffn, w_in, conv_w, conv_b, lru_wa, lru_ba, lru_wi, lru_bi, lru_lambda, hg_lb, hg_norm, w_out, router_w, router_b, w_gu, b_gu, w_down, b_down):
    assert ada_w.shape[0] == 1, "single-layer trunk"
    bp, lp, _ = x_prompt.shape
    bs, ls, _ = x_sample.shape
    tp, ts = bp * lp, bs * ls
    assert CONV_WIDTH - 1 <= ls <= SUBLANES

    w_in_bf = w_in[0].astype(BF16)
    w_out_bf = w_out[0].astype(BF16)
    wg = jnp.concatenate([_block_diag(lru_wa[0]), _block_diag(lru_wi[0])], axis=1).astype(BF16)
    bg = jnp.concatenate([lru_ba[0], lru_bi[0]])[None, :]
    lbs = jnp.cumsum(jax.nn.softmax(hg_lb.astype(F32), axis=0), axis=0)[0][None, :]
    mw = (conv_w[0], conv_b[0][None, :], wg, bg, lru_lambda[0][None, :], lbs, hg_norm[0][None, :])
    row = lambda p: p[0][None, :]

    n_c = bp + bs
    n_c_pad = -(-n_c // SUBLANES) * SUBLANES
    c_all = jnp.concatenate([c_prompt, c_sample, jnp.zeros((n_c_pad - n_c, D_MODEL), F32)], axis=0)
    mod = _ada(c_all, ada_w[0], ada_b[0][None, :])
    mod_p = [m[:, None, :] for m in jnp.split(mod[:bp], 6, axis=-1)]
    mod_s = [jnp.repeat(m, ls, axis=0) for m in jnp.split(mod[bp:n_c], 6, axis=-1)]

    xp = x_prompt.reshape(tp, D_MODEL)
    xs = x_sample.reshape(ts, D_MODEL)
    tm_s = _pick_tile(ts, (256, 128, 64, 32, 16, 8))

    tc = _pick_tile(lp // 2, (256, 128, 64))
    m_p, conv_p, h_p, s_p = _mixer_prompt(xp, mod_p[1], mod_p[0], row(norm_pre_mix), w_in_bf, bp, lp, mw, tc)
    proj_s = _inproj(xs, mod_s[1], mod_s[0], row(norm_pre_mix), w_in_bf, tm_s, 1)

    proj_s_pad = jnp.pad(proj_s.reshape(bs, ls, IN_COLS), ((0, 0), (0, SUBLANES - ls), (0, 0)))
    cst_hdr = jnp.pad(state_rglru_conv[0], ((0, 0), (CONV_HDR - (CONV_WIDTH - 1), 0), (0, 0)))
    m_s_pad, conv_s, h_s, s_s = _mixer_sample(proj_s_pad, cst_hdr, state_rglru_h[0][:, None, :],
                                              state_hgrn_S[0], mw, ls)
    m_s = m_s_pad[:, :ls, :].reshape(ts, D_MODEL)

    tmq = _pick_tile(math.gcd(lp, ts), (512, 256, 128, 64, 32, 16, 8))
    x1, hf, idx, gate, rank, counts = _post(
        m_p, m_s, xp, xs, (mod_p[2], mod_p[4], mod_p[3]), (mod_s[2], mod_s[4], mod_s[3]),
        row(norm_post_mix), row(norm_pre_ffn), w_out_bf, router_w[0].T, router_b[0][:, None], tmq, lp // tmq)

    n_blocks = -(-(TOP_K * (tp + ts)) // MOE_BM) + N_EXPERTS
    n_slots = n_blocks * MOE_BM
    pad_starts, fill_lo, pad_ends, block_e, block_half, next_e, n_used = _expert_layout(
        counts[:, 0].astype(I32), n_blocks)
    fill_hi = pad_ends.at[N_EXPERTS - 1].set(n_slots)
    dest = _slots(idx, rank, pad_starts)
    tmd = _pick_tile(math.gcd(lp, ts), (512, 256, 128, 64, 32, 16, 8))
    tmc = _pick_tile(math.gcd(lp, ts), (256, 128, 64, 32, 16, 8))
    xs_rows = _dispatch(fill_lo, fill_hi, _tile_major(dest, tmd), hf, n_slots, tmd)
    ys = _experts(block_e, block_half, next_e, n_used, xs_rows.reshape(n_slots * LANE_CHUNKS, LANES),
                  w_gu[0], b_gu[0][:, None, :], w_down[0], b_down[0][:, None, :])
    ys_tiles = ys.reshape(-1, LANE_CHUNKS, LANES)

    gates_t = gate.T
    ctiles = _tile_major(dest, tmc)
    y_p = _combine(ctiles[:tp // tmc], ys_tiles, gates_t, x1, mod_p[5], row(norm_post_ffn), tmc, lp // tmc, 0)
    y_s = _combine(ctiles[tp // tmc:], ys_tiles, gates_t, x1, mod_s[5], row(norm_post_ffn), tmc, 1, tp // tmc)

    return (y_p.reshape(bp, lp, D_MODEL), y_s.reshape(bs, ls, D_MODEL),
            conv_p[None], h_p.reshape(1, bp, LRU_WIDTH), s_p[None],
            conv_s[None], h_s.reshape(1, bs, LRU_WIDTH), s_s[None])
```

```python
import functools
import math

import jax
import jax.numpy as jnp
from jax import lax
from jax.experimental import pallas as pl
from jax.experimental.pallas import tpu as pltpu

F32 = jnp.float32
BF16 = jnp.bfloat16
I32 = jnp.int32

D_MODEL = 1024
LRU_WIDTH = 512
CONV_WIDTH = 4
LRU_C = 8.0
HG_WIDTH = 512
HG_HEAD_DIM = 128
HG_HEADS = 4
IN_COLS = 2 * LRU_WIDTH + 4 * HG_WIDTH
N_EXPERTS = 32
TOP_K = 4
D_EXPERT = 1024
SWIGLU_LIMIT = 7.0
SWIGLU_ALPHA = 1.702
RMS_EPS = 1e-6

C_XL, C_YL, C_Q, C_F, C_V, C_G = 0, 512, 1024, 1536, 2048, 2560

SUBLANES = 8
LANES = 128
LANE_CHUNKS = D_MODEL // LANES
CONV_HDR = SUBLANES
HG_CHUNK = 64
HG_MAX_EXPONENT = 60.0
MOE_BM = 512
ZERO_ROWS = 64
VMEM_LIMIT = 56 * 1024 * 1024

_NT = (((1,), (1,)), ((), ()))
_TN = (((0,), (0,)), ((), ()))


def _cparams(sem):
    return pltpu.CompilerParams(dimension_semantics=sem, vmem_limit_bytes=VMEM_LIMIT)


def _rms(x, g):
    return x * lax.rsqrt(jnp.mean(x * x, axis=-1, keepdims=True) + RMS_EPS) * g


def _gelu_tanh(x):
    c = math.sqrt(2.0 / math.pi)
    return 0.5 * x * (1.0 + jnp.tanh(c * (x + 0.044715 * (x * x * x))))


def _ada_kernel(c_ref, w_ref, b_ref, o_ref):
    c = c_ref[...]
    s = c * jax.nn.sigmoid(c)
    o_ref[...] = jnp.dot(s.astype(BF16), w_ref[...].astype(BF16), preferred_element_type=F32) + b_ref[...]


def _ada(c_all, ada_w, ada_b):
    n = c_all.shape[0]
    tn = 1024
    return pl.pallas_call(
        _ada_kernel,
        out_shape=jax.ShapeDtypeStruct((n, 6 * D_MODEL), F32),
        grid=(6 * D_MODEL // tn,),
        in_specs=[pl.BlockSpec((n, D_MODEL), lambda j: (0, 0)),
                  pl.BlockSpec((D_MODEL, tn), lambda j: (0, j)),
                  pl.BlockSpec((1, tn), lambda j: (0, j))],
        out_specs=pl.BlockSpec((n, tn), lambda j: (0, j)),
        compiler_params=_cparams(("arbitrary",)),
    )(c_all, ada_w, ada_b)


def _inproj_kernel(x_ref, sc_ref, sh_ref, g_ref, w_ref, o_ref):
    h = _rms(x_ref[...], g_ref[...]) * (1.0 + sc_ref[...]) + sh_ref[...]
    o_ref[...] = jnp.dot(h.astype(BF16), w_ref[...], preferred_element_type=F32)


def _mod_spec(mod, tm, tiles_per_seq):
    if mod.ndim == 3:
        return pl.BlockSpec((None, 1, D_MODEL), lambda i: (i // tiles_per_seq, 0, 0))
    return pl.BlockSpec((tm, D_MODEL), lambda i: (i, 0))


def _inproj(x, sc, sh, g, w_bf, tm, tiles_per_seq):
    t = x.shape[0]
    return pl.pallas_call(
        _inproj_kernel,
        out_shape=jax.ShapeDtypeStruct((t, IN_COLS), F32),
        grid=(t // tm,),
        in_specs=[pl.BlockSpec((tm, D_MODEL), lambda i: (i, 0)),
                  _mod_spec(sc, tm, tiles_per_seq), _mod_spec(sh, tm, tiles_per_seq),
                  pl.BlockSpec((1, D_MODEL), lambda i: (0, 0)),
                  pl.BlockSpec((D_MODEL, IN_COLS), lambda i: (0, 0))],
        out_specs=pl.BlockSpec((tm, IN_COLS), lambda i: (i, 0)),
        compiler_params=_cparams(("arbitrary",)),
    )(x, sc, sh, g, w_bf)


def _sigmoid(x):
    return 0.5 * jnp.tanh(0.5 * x) + 0.5


def _group_rows(x):
    rows, w = x.shape
    return x.reshape(rows // SUBLANES, SUBLANES, w)


def _scan_rows(a, u, h0):
    a3, u3 = _group_rows(a), _group_rows(u)
    r3 = lax.broadcasted_iota(I32, a3.shape, 1)
    s = 1
    while s < SUBLANES:
        keep = r3 >= s
        a_sh = jnp.where(keep, pltpu.roll(a3, s, 1), 1.0)
        u_sh = jnp.where(keep, pltpu.roll(u3, s, 1), 0.0)
        u3 = a3 * u_sh + u3
        a3 = a3 * a_sh
        s *= 2
    out, h = [], h0
    for i in range(a3.shape[0]):
        hg = u3[i] + a3[i] * h
        h = hg[SUBLANES - 1:SUBLANES, :]
        out.append(hg)
    return jnp.concatenate(out, axis=0), h


def _chunk_cumsum(x, chunk):
    x3 = _group_rows(x)
    r3 = lax.broadcasted_iota(I32, x3.shape, 1)
    s = 1
    while s < SUBLANES:
        x3 = x3 + jnp.where(r3 >= s, pltpu.roll(x3, s, 1), 0.0)
        s *= 2
    out, carry = [], None
    for i in range(x3.shape[0]):
        cur = x3[i] if i % (chunk // SUBLANES) == 0 else x3[i] + carry
        carry = cur[SUBLANES - 1:SUBLANES, :]
        out.append(cur)
    return jnp.concatenate(out, axis=0)


def _lru_gates(xc, wg_ref, bg_ref, lam_ref):
    gates = jnp.dot(xc.astype(BF16), wg_ref[...], preferred_element_type=F32) + bg_ref[...]
    r = _sigmoid(gates[:, :LRU_WIDTH])
    ig = _sigmoid(gates[:, LRU_WIDTH:])
    z = -lam_ref[...]
    softplus = jnp.maximum(z, 0.0) + jnp.log1p(jnp.exp(-jnp.abs(z)))
    log_a = -LRU_C * r * softplus
    a = jnp.exp(log_a)
    th = jnp.tanh(log_a)
    mult2 = -2.0 * th / (1.0 - th)
    return a, jnp.sqrt(mult2), ig


def _conv4(xbuf, xl, cw_ref, cb_ref, rows):
    cw = cw_ref[...]
    h = CONV_HDR
    return (cb_ref[...] + cw[0:1] * xbuf[h - 3:h - 3 + rows, :] + cw[1:2] * xbuf[h - 2:h - 2 + rows, :]
            + cw[2:3] * xbuf[h - 1:h - 1 + rows, :] + cw[3:4] * xl)


def _conv4_rolled(xl, prev, cw_ref, cb_ref):
    cw = cw_ref[...]
    row8 = lax.broadcasted_iota(I32, prev.shape, 0)
    acc = cb_ref[...] + cw[CONV_WIDTH - 1:CONV_WIDTH] * xl
    for s in range(1, CONV_WIDTH):
        sh = pltpu.roll(xl, s, 0)
        top = jnp.where(row8 < s, pltpu.roll(prev, s, 0), sh[0:SUBLANES, :])
        sh = jnp.concatenate([top, sh[SUBLANES:, :]], axis=0)
        acc = acc + cw[CONV_WIDTH - 1 - s:CONV_WIDTH - s] * sh
    return acc


def _head_norm_gate(o, gate, hgn):
    return _rms(o, hgn) * _sigmoid(gate)


def _hgrn_window(q, k, v, b, state, steps, state_is_transposed):
    row = lax.broadcasted_iota(I32, q.shape, 0)
    valid = row < steps
    b_end = b[steps - 1:steps, :]
    q_in = jnp.where(valid, q * jnp.exp(b), 0.0).astype(BF16)
    if state_is_transposed:
        o = lax.dot_general(q_in, state.astype(BF16), _NT, preferred_element_type=F32)
    else:
        o = jnp.dot(q_in, state.astype(BF16), preferred_element_type=F32)
    for d in range(steps):
        ok = valid & (row >= d)
        k_d = k if d == 0 else pltpu.roll(k, d, 0)
        v_d = v if d == 0 else pltpu.roll(v, d, 0)
        b_d = b if d == 0 else pltpu.roll(b, d, 0)
        decay = jnp.exp(jnp.where(ok, b - b_d, 0.0))
        wgt = jnp.sum(jnp.where(ok, q * k_d * decay, 0.0), axis=-1, keepdims=True)
        o = o + wgt * v_d
    k_out = jnp.where(valid, k * jnp.exp(jnp.where(valid, b_end - b, 0.0)), 0.0).astype(BF16)
    v_ok = jnp.where(valid, v, 0.0).astype(BF16)
    keep = jnp.exp(b_end)
    if state_is_transposed:
        s_new = state * keep + lax.dot_general(v_ok, k_out, _TN, preferred_element_type=F32)
    else:
        keep_col = jnp.broadcast_to(keep, q.shape).T[:, 0:1]
        s_new = state * keep_col + lax.dot_general(k_out, v_ok, _TN, preferred_element_type=F32)
    return o, s_new


def _tile_cumsum(x):
    row = lax.broadcasted_iota(I32, x.shape, 0)
    s = 1
    while s < SUBLANES:
        x = x + jnp.where(row >= s, pltpu.roll(x, s, 0), 0.0)
        s *= 2
    return x


def _mix_tile(proj_ref, cw_ref, cb_ref, wg_ref, bg_ref, lam_ref, lb_ref, hgn_ref, m_ref, xprev, hcar, st, st0, obuf,
              seq_start, tc, side_work):
    w = LRU_WIDTH
    side_work = iter(side_work)

    next(side_work)()
    xl = proj_ref[:, C_XL:C_XL + w]
    xc = _conv4_rolled(xl, xprev[...], cw_ref, cb_ref)
    xprev[...] = xl[tc - SUBLANES:tc, :]
    a, mult, ig = _lru_gates(xc, wg_ref, bg_ref, lam_ref)
    row = lax.broadcasted_iota(I32, (tc, w), 0)
    if seq_start is not None:
        mult = jnp.where((row == 0) & seq_start, 1.0, mult)
    h, h_last = _scan_rows(a, mult * ig * xc, hcar[0:1, :])
    hcar[0:1, :] = h_last
    m_ref[:, 0:w] = (h * _gelu_tanh(proj_ref[:, C_YL:C_YL + w])).astype(BF16)

    next(side_work)()
    lb = lb_ref[...]
    f = lb + (1.0 - lb) * _sigmoid(proj_ref[:, C_F:C_F + w])
    k = 1.0 - f
    b = _chunk_cumsum(jnp.log(f), HG_CHUNK)
    q = proj_ref[:, C_Q:C_Q + w]
    v = proj_ref[:, C_V:C_V + w]
    mid = HG_CHUNK // 2
    st0[...] = st[...]
    tri = (lax.broadcasted_iota(I32, (HG_CHUNK, HG_CHUNK), 0)
           >= lax.broadcasted_iota(I32, (HG_CHUNK, HG_CHUNK), 1))
    states = [st[hd] for hd in range(HG_HEADS)]
    for c in range(tc // HG_CHUNK):
        next(side_work)()
        rs = slice(c * HG_CHUNK, (c + 1) * HG_CHUNK)
        for hd in range(HG_HEADS):
            cs = slice(hd * HG_HEAD_DIM, (hd + 1) * HG_HEAD_DIM)
            s_t = states[hd]
            bc, qc, kc = b[rs, cs], q[rs, cs], k[rs, cs]
            vb = v[rs, cs].astype(BF16)
            b_mid = bc[mid - 1:mid, :]
            b_end = bc[HG_CHUNK - 1:HG_CHUNK, :]
            qp = (qc * jnp.exp(bc - b_mid)).astype(BF16)
            kp = (kc * jnp.exp(b_mid - bc)).astype(BF16)
            att = lax.dot_general(qp, kp, _NT, preferred_element_type=F32)
            att = jnp.where(tri, att, 0.0)
            q_in = (qc * jnp.exp(bc)).astype(BF16)
            o = (jnp.dot(att.astype(BF16), vb, preferred_element_type=F32)
                 + lax.dot_general(q_in, s_t.astype(BF16), _NT, preferred_element_type=F32))
            k_out = (kc * jnp.exp(b_end - bc)).astype(BF16)
            states[hd] = s_t * jnp.exp(b_end) + lax.dot_general(vb, k_out, _TN, preferred_element_type=F32)
            obuf[rs, cs] = o
    for hd in range(HG_HEADS):
        st[hd] = states[hd]

    half_decay = None
    for c in range(tc // HG_CHUNK):
        b_mid = b[c * HG_CHUNK + mid - 1:c * HG_CHUNK + mid, :]
        b_end = b[(c + 1) * HG_CHUNK - 1:(c + 1) * HG_CHUNK, :]
        worst = jnp.maximum(-b_mid, b_mid - b_end)
        half_decay = worst if half_decay is None else jnp.maximum(half_decay, worst)

    @pl.when(jnp.logical_not(jnp.max(half_decay) < HG_MAX_EXPONENT))
    def _():
        st[...] = st0[...]

        def window(gi, carry):
            rows = pl.ds(pl.multiple_of(gi * SUBLANES, SUBLANES), SUBLANES)
            f8 = lb + (1.0 - lb) * _sigmoid(proj_ref[rows, C_F:C_F + w])
            b8 = _tile_cumsum(jnp.log(f8))
            k8 = 1.0 - f8
            q8 = proj_ref[rows, C_Q:C_Q + w]
            v8 = proj_ref[rows, C_V:C_V + w]
            for hd in range(HG_HEADS):
                cs = slice(hd * HG_HEAD_DIM, (hd + 1) * HG_HEAD_DIM)
                o, s_new = _hgrn_window(q8[:, cs], k8[:, cs], v8[:, cs], b8[:, cs], st[hd], SUBLANES, True)
                st[hd] = s_new
                obuf[rows, cs] = o
            return carry
        lax.fori_loop(0, tc // SUBLANES, window, 0)

    hgn = hgn_ref[...]
    for hd in range(HG_HEADS):
        cs = slice(hd * HG_HEAD_DIM, (hd + 1) * HG_HEAD_DIM)
        gate = proj_ref[:, C_G + hd * HG_HEAD_DIM:C_G + (hd + 1) * HG_HEAD_DIM]
        m_ref[:, w + hd * HG_HEAD_DIM:w + (hd + 1) * HG_HEAD_DIM] = (
            _head_norm_gate(obuf[:, cs], gate, hgn).astype(BF16))


def _mixer_prompt_kernel(x_ref, xn_ref, sc_ref, sh_ref, scn_ref, shn_ref, g_ref, win_ref, cw_ref, cb_ref, wg_ref,
                         bg_ref, lam_ref, lb_ref, hgn_ref, m_ref, conv_ref, h_ref, s_ref,
                         proj_a, proj_b, hn_buf, xprev, hcar, st, st0, obuf, *, tc):
    j = pl.program_id(1)
    n_phases = 2 + tc // HG_CHUNK
    cols = IN_COLS // n_phases

    def project(x, sc, sh, out_ref):
        def chunk(c):
            if c == 0:
                hn_buf[...] = (_rms(x(), g_ref[...]) * (1.0 + sc[...]) + sh[...]).astype(BF16)
            out_ref[:, c * cols:(c + 1) * cols] = jnp.dot(hn_buf[...], win_ref[:, c * cols:(c + 1) * cols],
                                                          preferred_element_type=F32)
        return [functools.partial(chunk, c) for c in range(n_phases)]

    def mix(proj_ref, first_row, seq_start, side_work):
        _mix_tile(proj_ref, cw_ref, cb_ref, wg_ref, bg_ref, lam_ref, lb_ref, hgn_ref,
                  m_ref.at[pl.ds(first_row, tc), :], xprev, hcar, st, st0, obuf, seq_start, tc, side_work)

    @pl.when((pl.program_id(0) == 0) & (j == 0))
    def _():
        for thunk in project(lambda: x_ref[0:tc, :], sc_ref, sh_ref, proj_a):
            thunk()

    @pl.when(j == 0)
    def _():
        xprev[...] = jnp.zeros_like(xprev)
        hcar[...] = jnp.zeros_like(hcar)
        st[...] = jnp.zeros_like(st)

    mix(proj_a, 0, j == 0, project(lambda: x_ref[tc:2 * tc, :], sc_ref, sh_ref, proj_b))
    mix(proj_b, tc, None, project(lambda: xn_ref[...], scn_ref, shn_ref, proj_a))

    @pl.when(j == pl.num_programs(1) - 1)
    def _():
        conv_ref[...] = xprev[SUBLANES - (CONV_WIDTH - 1):SUBLANES, :]
        h_ref[...] = hcar[0:1, :]
        for hd in range(HG_HEADS):
            s_ref[hd] = st[hd].T


def _mixer_prompt(x, sc, sh, g, w_in_bf, nb, seq, mw, tc):
    cw, cb, wg, bg, lam, lb, hgn = mw
    nj = seq // (2 * tc)
    last_tile = nb * seq // tc - 1
    next_tile = lambda b, j: jnp.minimum(2 * (b * nj + j) + 2, last_tile)
    const = lambda shape: pl.BlockSpec(shape, lambda b, j: tuple(0 for _ in shape))
    per_seq = lambda: pl.BlockSpec((None, 1, D_MODEL), lambda b, j: (b, 0, 0))
    next_seq = lambda: pl.BlockSpec((None, 1, D_MODEL), lambda b, j: (next_tile(b, j) // (2 * nj), 0, 0))
    return pl.pallas_call(
        functools.partial(_mixer_prompt_kernel, tc=tc),
        out_shape=(jax.ShapeDtypeStruct((nb * seq, D_MODEL), BF16),
                   jax.ShapeDtypeStruct((nb, CONV_WIDTH - 1, LRU_WIDTH), F32),
                   jax.ShapeDtypeStruct((nb, 1, LRU_WIDTH), F32),
                   jax.ShapeDtypeStruct((nb, HG_HEADS, HG_HEAD_DIM, HG_HEAD_DIM), F32)),
        grid=(nb, nj),
        in_specs=[pl.BlockSpec((2 * tc, D_MODEL), lambda b, j: (b * nj + j, 0)),
                  pl.BlockSpec((tc, D_MODEL), lambda b, j: (next_tile(b, j), 0)),
                  per_seq(), per_seq(), next_seq(), next_seq(), const((1, D_MODEL)), const((D_MODEL, IN_COLS)),
                  const((CONV_WIDTH, LRU_WIDTH)), const((1, LRU_WIDTH)),
                  const((LRU_WIDTH, 2 * LRU_WIDTH)), const((1, 2 * LRU_WIDTH)),
                  const((1, LRU_WIDTH)), const((1, HG_WIDTH)), const((1, HG_HEAD_DIM))],
        out_specs=(pl.BlockSpec((2 * tc, D_MODEL), lambda b, j: (b * nj + j, 0)),
                   pl.BlockSpec((None, CONV_WIDTH - 1, LRU_WIDTH), lambda b, j: (b, 0, 0)),
                   pl.BlockSpec((None, 1, LRU_WIDTH), lambda b, j: (b, 0, 0)),
                   pl.BlockSpec((None, HG_HEADS, HG_HEAD_DIM, HG_HEAD_DIM), lambda b, j: (b, 0, 0, 0))),
        scratch_shapes=[pltpu.VMEM((tc, IN_COLS), F32),
                        pltpu.VMEM((tc, IN_COLS), F32),
                        pltpu.VMEM((tc, D_MODEL), BF16),
                        pltpu.VMEM((SUBLANES, LRU_WIDTH), F32),
                        pltpu.VMEM((SUBLANES, LRU_WIDTH), F32),
                        pltpu.VMEM((HG_HEADS, HG_HEAD_DIM, HG_HEAD_DIM), F32),
                        pltpu.VMEM((HG_HEADS, HG_HEAD_DIM, HG_HEAD_DIM), F32),
                        pltpu.VMEM((tc, HG_WIDTH), F32)],
        compiler_params=_cparams(("arbitrary", "arbitrary")),
    )(x, x, sc, sh, sc, sh, g, w_in_bf, cw, cb, wg, bg, lam, lb, hgn)


def _mixer_sample_kernel(proj_ref, cst_ref, h0_ref, s0_ref, cw_ref, cb_ref, wg_ref, bg_ref, lam_ref, lb_ref,
                         hgn_ref, m_ref, conv_ref, h_ref, s_ref, xbuf, *, steps):
    for bi in range(proj_ref.shape[0]):
        _mixer_sample_one(proj_ref.at[bi], cst_ref.at[bi], h0_ref.at[bi], s0_ref.at[bi], cw_ref, cb_ref, wg_ref,
                          bg_ref, lam_ref, lb_ref, hgn_ref, m_ref.at[bi], conv_ref.at[bi], h_ref.at[bi],
                          s_ref.at[bi], xbuf.at[bi], steps)


def _mixer_sample_one(proj_ref, cst_ref, h0_ref, s0_ref, cw_ref, cb_ref, wg_ref, bg_ref, lam_ref, lb_ref,
                      hgn_ref, m_ref, conv_ref, h_ref, s_ref, xbuf, steps):
    w = LRU_WIDTH
    rows = SUBLANES
    row = lax.broadcasted_iota(I32, (rows, w), 0)

    xl = proj_ref[:, C_XL:C_XL + w]
    xbuf[0:CONV_HDR, :] = cst_ref[...]
    xbuf[CONV_HDR:CONV_HDR + rows, :] = xl
    xc = _conv4(xbuf, xl, cw_ref, cb_ref, rows)
    conv_ref[...] = xbuf[CONV_HDR + steps - 3:CONV_HDR + steps, :]
    a, mult, ig = _lru_gates(xc, wg_ref, bg_ref, lam_ref)
    u = mult * ig * xc
    h = h0_ref[...]
    hs = jnp.zeros((rows, w), F32)
    for t in range(steps):
        h = a[t:t + 1, :] * h + u[t:t + 1, :]
        hs = jnp.where(row == t, h, hs)
    h_ref[...] = h
    m_ref[:, 0:w] = (hs * _gelu_tanh(proj_ref[:, C_YL:C_YL + w])).astype(BF16)

    lb = lb_ref[...]
    f = lb + (1.0 - lb) * _sigmoid(proj_ref[:, C_F:C_F + w])
    k = 1.0 - f
    b = _tile_cumsum(jnp.log(f))
    q = proj_ref[:, C_Q:C_Q + w]
    v = proj_ref[:, C_V:C_V + w]
    hgn = hgn_ref[...]
    for hd in range(HG_HEADS):
        cs = slice(hd * HG_HEAD_DIM, (hd + 1) * HG_HEAD_DIM)
        o, s_new = _hgrn_window(q[:, cs], k[:, cs], v[:, cs], b[:, cs], s0_ref[hd], steps, False)
        s_ref[hd] = s_new
        gate = proj_ref[:, C_G + hd * HG_HEAD_DIM:C_G + (hd + 1) * HG_HEAD_DIM]
        m_ref[:, w + hd * HG_HEAD_DIM:w + (hd + 1) * HG_HEAD_DIM] = _head_norm_gate(o, gate, hgn).astype(BF16)


def _mixer_sample(proj_pad, cst_hdr, h0, s0, mw, steps):
    cw, cb, wg, bg, lam, lb, hgn = mw
    nb = proj_pad.shape[0]
    per_step = _pick_tile(nb, (8, 4, 2, 1))
    const = lambda shape: pl.BlockSpec(shape, lambda b: tuple(0 for _ in shape))
    per_b = lambda shape: pl.BlockSpec((per_step,) + shape, lambda b: (b,) + tuple(0 for _ in shape))
    return pl.pallas_call(
        functools.partial(_mixer_sample_kernel, steps=steps),
        out_shape=(jax.ShapeDtypeStruct((nb, SUBLANES, D_MODEL), BF16),
                   jax.ShapeDtypeStruct((nb, CONV_WIDTH - 1, LRU_WIDTH), F32),
                   jax.ShapeDtypeStruct((nb, 1, LRU_WIDTH), F32),
                   jax.ShapeDtypeStruct((nb, HG_HEADS, HG_HEAD_DIM, HG_HEAD_DIM), F32)),
        grid=(nb // per_step,),
        in_specs=[per_b((SUBLANES, IN_COLS)), per_b((CONV_HDR, LRU_WIDTH)), per_b((1, LRU_WIDTH)),
                  per_b((HG_HEADS, HG_HEAD_DIM, HG_HEAD_DIM)),
                  const((CONV_WIDTH, LRU_WIDTH)), const((1, LRU_WIDTH)),
                  const((LRU_WIDTH, 2 * LRU_WIDTH)), const((1, 2 * LRU_WIDTH)),
                  const((1, LRU_WIDTH)), const((1, HG_WIDTH)), const((1, HG_HEAD_DIM))],
        out_specs=(per_b((SUBLANES, D_MODEL)), per_b((CONV_WIDTH - 1, LRU_WIDTH)), per_b((1, LRU_WIDTH)),
                   per_b((HG_HEADS, HG_HEAD_DIM, HG_HEAD_DIM))),
        scratch_shapes=[pltpu.VMEM((per_step, CONV_HDR + SUBLANES, LRU_WIDTH), F32)],
        compiler_params=_cparams(("arbitrary",)),
    )(proj_pad, cst_hdr, h0, s0, cw, cb, wg, bg, lam, lb, hgn)


def _to_row_tiles(ref, val, rows):
    for c in range(LANE_CHUNKS):
        ref[pl.ds(c, rows, stride=LANE_CHUNKS), :] = val[:, c * LANES:(c + 1) * LANES]


def _from_row_tiles(ref, base, rows):
    return jnp.concatenate(
        [ref[pl.ds(base + c, rows, stride=LANE_CHUNKS), :] for c in range(LANE_CHUNKS)], axis=-1)


def _post_kernel(mp_ref, ms_ref, xp_ref, xs_ref, gtmp_ref, scfp_ref, shfp_ref, gtms_ref, scfs_ref, shfs_ref,
                 npost_ref, npre_ref, wout_ref, rwt_ref, rb_ref, earlier_ref,
                 x1_ref, hf_ref, idx_ref, gate_ref, rank_ref, cnt_out_ref, cnt, *, n_p):
    i = pl.program_id(0)
    tm = xp_ref.shape[0]

    @pl.when(i == 0)
    def _():
        cnt[...] = jnp.zeros_like(cnt)

    def run(m_ref, x_ref, gtm_ref, scf_ref, shf_ref):
        mix = jnp.dot(m_ref[...], wout_ref[...], preferred_element_type=F32)
        x1 = x_ref[...] + gtm_ref[...] * _rms(mix, npost_ref[...])
        x1_ref[...] = x1
        hf = _rms(x1, npre_ref[...]) * (1.0 + scf_ref[...]) + shf_ref[...]
        _to_row_tiles(hf_ref, hf, tm)
        logits = lax.dot_general(rwt_ref[...], hf, _NT, precision=lax.Precision.HIGHEST,
                                 preferred_element_type=F32) + rb_ref[...]
        e_iota = lax.broadcasted_iota(I32, logits.shape, 0)
        vals, idxs = [], []
        for _ in range(TOP_K):
            mx = jnp.max(logits, axis=0, keepdims=True)
            ix = jnp.min(jnp.where(logits == mx, e_iota, N_EXPERTS), axis=0, keepdims=True)
            vals.append(mx)
            idxs.append(ix)
            logits = jnp.where(e_iota == ix, -jnp.inf, logits)
        exps = [jnp.exp(vv - vals[0]) for vv in vals]
        den = exps[0] + exps[1] + exps[2] + exps[3]
        earlier = earlier_ref[...]
        base = cnt[...]
        for kk in range(TOP_K):
            idx_ref[kk:kk + 1, :] = idxs[kk]
            gate_ref[kk:kk + 1, :] = exps[kk] / den
            hit = e_iota == idxs[kk]
            onehot = jnp.where(hit, 1.0, 0.0)
            before = jnp.dot(onehot.astype(BF16), earlier, preferred_element_type=F32)
            rank = jnp.sum(jnp.where(hit, before + base, 0.0), axis=0, keepdims=True)
            rank_ref[kk:kk + 1, :] = rank.astype(I32)
            base = base + jnp.sum(onehot, axis=1, keepdims=True)
        cnt[...] = base

    @pl.when(i < n_p)
    def _():
        run(mp_ref, xp_ref, gtmp_ref, scfp_ref, shfp_ref)

    @pl.when(i >= n_p)
    def _():
        run(ms_ref, xs_ref, gtms_ref, scfs_ref, shfs_ref)

    @pl.when(i == pl.num_programs(0) - 1)
    def _():
        cnt_out_ref[...] = cnt[...]


def _post(m_p, m_s, x_p, x_s, mod_p, mod_s, npost, npre, wout_bf, rwt, rb, tm, tiles_per_seq):
    n_p, n_s = x_p.shape[0] // tm, x_s.shape[0] // tm
    t = (n_p + n_s) * tm
    first = lambda i: jnp.minimum(i, n_p - 1)
    second = lambda i: jnp.maximum(i - n_p, 0)
    row = lambda: pl.BlockSpec((1, D_MODEL), lambda i: (0, 0))
    tok_p = lambda: pl.BlockSpec((tm, D_MODEL), lambda i: (first(i), 0))
    tok_s = lambda: pl.BlockSpec((tm, D_MODEL), lambda i: (second(i), 0))
    seq_p = lambda: pl.BlockSpec((None, 1, D_MODEL), lambda i: (first(i) // tiles_per_seq, 0, 0))
    per_k = lambda: pl.BlockSpec((TOP_K, tm), lambda i: (0, i))
    earlier = jnp.triu(jnp.ones((tm, tm), BF16), k=1)
    return pl.pallas_call(
        functools.partial(_post_kernel, n_p=n_p),
        out_shape=(jax.ShapeDtypeStruct((t, D_MODEL), F32),
                   jax.ShapeDtypeStruct((t * LANE_CHUNKS, LANES), F32),
                   jax.ShapeDtypeStruct((TOP_K, t), I32), jax.ShapeDtypeStruct((TOP_K, t), F32),
                   jax.ShapeDtypeStruct((TOP_K, t), I32), jax.ShapeDtypeStruct((N_EXPERTS, 1), F32)),
        grid=(n_p + n_s,),
        in_specs=[tok_p(), tok_s(), tok_p(), tok_s(), seq_p(), seq_p(), seq_p(), tok_s(), tok_s(), tok_s(),
                  row(), row(),
                  pl.BlockSpec((D_MODEL, D_MODEL), lambda i: (0, 0)),
                  pl.BlockSpec((N_EXPERTS, D_MODEL), lambda i: (0, 0)),
                  pl.BlockSpec((N_EXPERTS, 1), lambda i: (0, 0)),
                  pl.BlockSpec((tm, tm), lambda i: (0, 0))],
        out_specs=(pl.BlockSpec((tm, D_MODEL), lambda i: (i, 0)),
                   pl.BlockSpec((tm * LANE_CHUNKS, LANES), lambda i: (i, 0)),
                   per_k(), per_k(), per_k(), pl.BlockSpec((N_EXPERTS, 1), lambda i: (0, 0))),
        scratch_shapes=[pltpu.VMEM((N_EXPERTS, 1), F32)],
        compiler_params=_cparams(("arbitrary",)),
    )(m_p, m_s, x_p, x_s, *mod_p, *mod_s, npost, npre, wout_bf, rwt, rb, earlier)


def _dispatch_kernel(fill_lo_ref, fill_hi_ref, dest_ref, hf_ref, xs_hbm, zeros, row_sem, pad_sem, *, tm):
    i = pl.program_id(0)
    n_rows = TOP_K * tm

    @pl.when(i == 0)
    def _():
        zeros[...] = jnp.zeros_like(zeros)

        def per_expert(e, totals):
            lo, hi = fill_lo_ref[e], fill_hi_ref[e]
            mid = jnp.minimum((lo + ZERO_ROWS - 1) // ZERO_ROWS * ZERO_ROWS, hi)

            def per_row(s, carry):
                pltpu.make_async_copy(zeros.at[0], xs_hbm.at[s], pad_sem).start()
                return carry
            lax.fori_loop(lo, mid, per_row, 0)

            def per_chunk(c, carry):
                pltpu.make_async_copy(zeros, xs_hbm.at[pl.ds(pl.multiple_of(c * ZERO_ROWS, ZERO_ROWS), ZERO_ROWS)],
                                      pad_sem).start()
                return carry
            lax.fori_loop(mid // ZERO_ROWS, hi // ZERO_ROWS, per_chunk, 0)
            return totals[0] + (mid - lo), totals[1] + (hi - mid) // ZERO_ROWS
        n_rows_filled, n_chunks_filled = lax.fori_loop(0, N_EXPERTS, per_expert, (0, 0))

        def drain_row(s, carry):
            pltpu.make_async_copy(zeros.at[0], xs_hbm.at[0], pad_sem).wait()
            return carry
        lax.fori_loop(0, n_rows_filled, drain_row, 0)

        def drain_chunk(c, carry):
            pltpu.make_async_copy(zeros, xs_hbm.at[pl.ds(0, ZERO_ROWS)], pad_sem).wait()
            return carry
        lax.fori_loop(0, n_chunks_filled, drain_chunk, 0)

    for n in range(n_rows):
        pltpu.make_async_copy(hf_ref.at[pl.ds((n % tm) * LANE_CHUNKS, LANE_CHUNKS), :],
                              xs_hbm.at[dest_ref[0, n]], row_sem).start(priority=n % 2)
    pltpu.make_async_copy(xs_hbm.at[pl.ds(0, n_rows)], xs_hbm.at[pl.ds(0, n_rows)], row_sem).wait()


def _dispatch(fill_lo, fill_hi, dest_tiles, hf_rows, n_slots, tm):
    return pl.pallas_call(
        functools.partial(_dispatch_kernel, tm=tm),
        out_shape=jax.ShapeDtypeStruct((n_slots, LANE_CHUNKS, LANES), F32),
        grid_spec=pltpu.PrefetchScalarGridSpec(
            num_scalar_prefetch=2, grid=(dest_tiles.shape[0],),
            in_specs=[pl.BlockSpec((None, 1, TOP_K * tm), lambda i, lo, hi: (i, 0, 0), memory_space=pltpu.SMEM),
                      pl.BlockSpec((tm * LANE_CHUNKS, LANES), lambda i, lo, hi: (i, 0))],
            out_specs=pl.BlockSpec(memory_space=pl.ANY),
            scratch_shapes=[pltpu.VMEM((ZERO_ROWS, LANE_CHUNKS, LANES), F32),
                            pltpu.SemaphoreType.DMA, pltpu.SemaphoreType.DMA]),
        compiler_params=_cparams(("arbitrary",)),
    )(fill_lo, fill_hi, dest_tiles, hf_rows)


def _expert_kernel(be_ref, half_ref, nxt_ref, nu_ref, xs_ref, wgu_hbm, bgu_ref, wd_hbm, bd_ref, ys_ref,
                   wgu_f32, wd_f32, wgu_bf, wd_bf, w_sem):
    i = pl.program_id(0)
    n_used = nu_ref[0]

    def weight_copies(e):
        return (pltpu.make_async_copy(wgu_hbm.at[e], wgu_f32, w_sem.at[0]),
                pltpu.make_async_copy(wd_hbm.at[e], wd_f32, w_sem.at[1]))

    @pl.when(i == 0)
    def _():
        for cp in weight_copies(be_ref[0]):
            cp.start()

    @pl.when(i < n_used)
    def _():
        prev = be_ref[jnp.maximum(i - 1, 0)]

        @pl.when((i == 0) | (be_ref[i] != prev))
        def _():
            for cp in weight_copies(be_ref[i]):
                cp.wait()
            wgu_bf[...] = wgu_f32[...].astype(BF16)
            wd_bf[...] = wd_f32[...].astype(BF16)
            nxt = nxt_ref[be_ref[i]]

            @pl.when(nxt >= 0)
            def _():
                for cp in weight_copies(nxt):
                    cp.start()

        def swiglu_rows(rows):
            x = _from_row_tiles(xs_ref, 0, rows).astype(BF16)
            hu = jnp.dot(x, wgu_bf[...], preferred_element_type=F32) + bgu_ref[...]
            glu = jnp.minimum(hu[:, :D_EXPERT], SWIGLU_LIMIT)
            lin = jnp.clip(hu[:, D_EXPERT:], -SWIGLU_LIMIT, SWIGLU_LIMIT)
            act = glu * jax.nn.sigmoid(SWIGLU_ALPHA * glu) * (lin + 1.0)
            y = jnp.dot(act.astype(BF16), wd_bf[...], preferred_element_type=F32) + bd_ref[...]
            _to_row_tiles(ys_ref, y, rows)

        @pl.when(half_ref[i] == 0)
        def _():
            swiglu_rows(MOE_BM)

        @pl.when(half_ref[i] != 0)
        def _():
            swiglu_rows(MOE_BM // 2)
            ys_ref[MOE_BM // 2 * LANE_CHUNKS:, :] = jnp.zeros((MOE_BM // 2 * LANE_CHUNKS, LANES), F32)

    @pl.when(i >= n_used)
    def _():
        ys_ref[...] = jnp.zeros_like(ys_ref)


def _experts(block_e, block_half, next_e, n_used, xs2d, w_gu, b_gu, w_down, b_down):
    n_blocks = block_e.shape[0]
    rows = MOE_BM * LANE_CHUNKS
    live = lambda i, be, hf, nx, nu: (jnp.minimum(i, nu[0] - 1), 0)
    return pl.pallas_call(
        _expert_kernel,
        out_shape=jax.ShapeDtypeStruct(xs2d.shape, F32),
        grid_spec=pltpu.PrefetchScalarGridSpec(
            num_scalar_prefetch=4,
            grid=(n_blocks,),
            in_specs=[pl.BlockSpec((rows, LANES), live),
                      pl.BlockSpec(memory_space=pl.ANY),
                      pl.BlockSpec((None, 1, 2 * D_EXPERT), lambda i, be, hf, nx, nu: (be[i], 0, 0)),
                      pl.BlockSpec(memory_space=pl.ANY),
                      pl.BlockSpec((None, 1, D_MODEL), lambda i, be, hf, nx, nu: (be[i], 0, 0))],
            out_specs=pl.BlockSpec((rows, LANES), lambda i, be, hf, nx, nu: (i, 0)),
            scratch_shapes=[pltpu.VMEM((D_MODEL, 2 * D_EXPERT), F32),
                            pltpu.VMEM((D_EXPERT, D_MODEL), F32),
                            pltpu.VMEM((D_MODEL, 2 * D_EXPERT), BF16),
                            pltpu.VMEM((D_EXPERT, D_MODEL), BF16),
                            pltpu.SemaphoreType.DMA((2,))]),
        compiler_params=_cparams(("arbitrary",)),
    )(block_e, block_half, next_e, n_used, xs2d, w_gu, b_gu, w_down, b_down)


def _combine_kernel(dcur_ref, dnext_ref, ys_hbm, g_ref, x1_ref, gt_ref, np_ref, o_ref, ybuf, row_sem, *, tm):
    i = pl.program_id(0)
    n_rows = TOP_K * tm
    slot = i % 2

    def slot_rows(s):
        return ybuf.at[pl.ds(pl.multiple_of(s * n_rows * LANE_CHUNKS, n_rows * LANE_CHUNKS), n_rows * LANE_CHUNKS), :]

    def issue(d_ref, s):
        dst = slot_rows(s)
        for n in range(n_rows):
            pltpu.make_async_copy(ys_hbm.at[d_ref[0, n]], dst.at[pl.ds(n * LANE_CHUNKS, LANE_CHUNKS), :],
                                  row_sem.at[s]).start(priority=n % 2)

    @pl.when(i == 0)
    def _():
        issue(dcur_ref, 0)

    @pl.when(i + 1 < pl.num_programs(0))
    def _():
        issue(dnext_ref, 1 - slot)

    pltpu.make_async_copy(slot_rows(slot), slot_rows(slot), row_sem.at[slot]).wait()
    g = g_ref[...]
    y = None
    for kk in range(TOP_K):
        rows = _from_row_tiles(ybuf, (slot * n_rows + kk * tm) * LANE_CHUNKS, tm)
        y = g[:, kk:kk + 1] * rows if y is None else y + g[:, kk:kk + 1] * rows
    o_ref[...] = x1_ref[...] + gt_ref[...] * _rms(y, np_ref[...])


def _combine(dest_tiles, ys_tiles, gates_t, x1, gt, npost, tm, tiles_per_seq, tile0):
    n_tiles = dest_tiles.shape[0]
    t = n_tiles * tm
    smem_tile = lambda f: pl.BlockSpec((None, 1, TOP_K * tm), f, memory_space=pltpu.SMEM)
    return pl.pallas_call(
        functools.partial(_combine_kernel, tm=tm),
        out_shape=jax.ShapeDtypeStruct((t, D_MODEL), F32),
        grid=(n_tiles,),
        in_specs=[smem_tile(lambda i: (i, 0, 0)),
                  smem_tile(lambda i: (jnp.minimum(i + 1, n_tiles - 1), 0, 0)),
                  pl.BlockSpec(memory_space=pl.ANY),
                  pl.BlockSpec((tm, TOP_K), lambda i: (i + tile0, 0)),
                  pl.BlockSpec((tm, D_MODEL), lambda i: (i + tile0, 0)),
                  _mod_spec(gt, tm, tiles_per_seq),
                  pl.BlockSpec((1, D_MODEL), lambda i: (0, 0))],
        out_specs=pl.BlockSpec((tm, D_MODEL), lambda i: (i, 0)),
        scratch_shapes=[pltpu.VMEM((2 * TOP_K * tm * LANE_CHUNKS, LANES), F32),
                        pltpu.SemaphoreType.DMA((2,))],
        compiler_params=_cparams(("arbitrary",)),
    )(dest_tiles, dest_tiles, ys_tiles, gates_t, x1, gt, npost)


def _expert_layout(counts, n_blocks):
    padded = (counts + MOE_BM - 1) // MOE_BM * MOE_BM
    pad_ends = jnp.cumsum(padded)
    pad_starts = pad_ends - padded
    block_start = jnp.arange(n_blocks, dtype=I32) * MOE_BM
    block_e = jnp.minimum(jnp.sum((pad_ends[None, :] <= block_start[:, None]).astype(I32), axis=1), N_EXPERTS - 1)
    n_used = (pad_ends[-1:] // MOE_BM).astype(I32)
    e_ids = jnp.arange(N_EXPERTS, dtype=I32)
    rows_end = jnp.sum(jnp.where(block_e[:, None] == e_ids[None, :], (pad_starts + counts)[None, :], 0), axis=1)
    block_half = (rows_end - block_start <= MOE_BM // 2).astype(I32)
    later = (counts > 0)[None, :] & (e_ids[None, :] > e_ids[:, None])
    next_e = jnp.min(jnp.where(later, e_ids[None, :], N_EXPERTS), axis=1)
    next_e = jnp.where(next_e < N_EXPERTS, next_e, -1).astype(I32)
    return pad_starts, pad_starts + counts, pad_ends, block_e, block_half, next_e, n_used


def _slots(idx, rank, pad_starts):
    onehot = idx[:, :, None] == jnp.arange(N_EXPERTS, dtype=I32)[None, None, :]
    return rank + jnp.sum(jnp.where(onehot, pad_starts[None, None, :], 0), axis=-1)


def _tile_major(a, tm):
    t = a.shape[1]
    return a.reshape(TOP_K, t // tm, tm).transpose(1, 0, 2).reshape(t // tm, 1, TOP_K * tm)


def _block_diag(wb):
    n, d, _ = wb.shape
    eye = jnp.eye(n, dtype=wb.dtype)
    return (wb[:, :, None, :] * eye[:, None, :, None]).reshape(n * d, n * d)


def _pick_tile(n, prefs):
    for p in prefs:
        if n % p == 0:
            return p
    raise ValueError(f"no tile for {n}")


def kernel(x_prompt, x_sample, state_rglru_conv, state_rglru_h, state_hgrn_S, c_prompt, c_sample, ada_w, ada_b, norm_pre_mix, norm_post_mix, norm_pre_ffn, norm_post_ffn, w_in, conv_w, conv_b, lru_wa, lru_ba, lru_wi, lru_bi, lru_lambda, hg_lb, hg_norm, w_out, router_w, router_b, w_gu, b_gu, w_down, b_down):
    assert ada_w.shape[0] == 1, "single-layer trunk"
    bp, lp, _ = x_prompt.shape
    bs, ls, _ = x_sample.shape
    tp, ts = bp * lp, bs * ls
    assert CONV_WIDTH - 1 <= ls <= SUBLANES

    w_in_bf = w_in[0].astype(BF16)
    w_out_bf = w_out[0].astype(BF16)
    wg = jnp.concatenate([_block_diag(lru_wa[0]), _block_diag(lru_wi[0])], axis=1).astype(BF16)
    bg = jnp.concatenate([lru_ba[0], lru_bi[0]])[None, :]
    lbs = jnp.cumsum(jax.nn.softmax(hg_lb.astype(F32), axis=0), axis=0)[0][None, :]
    mw = (conv_w[0], conv_b[0][None, :], wg, bg, lru_lambda[0][None, :], lbs, hg_norm[0][None, :])
    row = lambda p: p[0][None, :]

    n_c = bp + bs
    n_c_pad = -(-n_c // SUBLANES) * SUBLANES
    c_all = jnp.concatenate([c_prompt, c_sample, jnp.zeros((n_c_pad - n_c, D_MODEL), F32)], axis=0)
    mod = _ada(c_all, ada_w[0], ada_b[0][None, :])
    mod_p = [m[:, None, :] for m in jnp.split(mod[:bp], 6, axis=-1)]
    mod_s = [jnp.repeat(m, ls, axis=0) for m in jnp.split(mod[bp:n_c], 6, axis=-1)]

    xp = x_prompt.reshape(tp, D_MODEL)
    xs = x_sample.reshape(ts, D_MODEL)
    tm_s = _pick_tile(ts, (256, 128, 64, 32, 16, 8))

    tc = _pick_tile(lp // 2, (256, 128, 64))
    m_p, conv_p, h_p, s_p = _mixer_prompt(xp, mod_p[1], mod_p[0], row(norm_pre_mix), w_in_bf, bp, lp, mw, tc)
    proj_s = _inproj(xs, mod_s[1], mod_s[0], row(norm_pre_mix), w_in_bf, tm_s, 1)

    proj_s_pad = jnp.pad(proj_s.reshape(bs, ls, IN_COLS), ((0, 0), (0, SUBLANES - ls), (0, 0)))
    cst_hdr = jnp.pad(state_rglru_conv[0], ((0, 0), (CONV_HDR - (CONV_WIDTH - 1), 0), (0, 0)))
    m_s_pad, conv_s, h_s, s_s = _mixer_sample(proj_s_pad, cst_hdr, state_rglru_h[0][:, None, :],
                                              state_hgrn_S[0], mw, ls)
    m_s = m_s_pad[:, :ls, :].reshape(ts, D_MODEL)

    tmq = _pick_tile(math.gcd(lp, ts), (512, 256, 128, 64, 32, 16, 8))
    x1, hf, idx, gate, rank, counts = _post(
        m_p, m_s, xp, xs, (mod_p[2], mod_p[4], mod_p[3]), (mod_s[2], mod_s[4], mod_s[3]),
        row(norm_post_mix), row(norm_pre_ffn), w_out_bf, router_w[0].T, router_b[0][:, None], tmq, lp // tmq)

    n_blocks = -(-(TOP_K * (tp + ts)) // MOE_BM) + N_EXPERTS
    n_slots = n_blocks * MOE_BM
    pad_starts, fill_lo, pad_ends, block_e, block_half, next_e, n_used = _expert_layout(
        counts[:, 0].astype(I32), n_blocks)
    fill_hi = pad_ends.at[N_EXPERTS - 1].set(n_slots)
    dest = _slots(idx, rank, pad_starts)
    tmd = _pick_tile(math.gcd(lp, ts), (512, 256, 128, 64, 32, 16, 8))
    tmc = _pick_tile(math.gcd(lp, ts), (256, 128, 64, 32, 16, 8))
    xs_rows = _dispatch(fill_lo, fill_hi, _tile_major(dest, tmd), hf, n_slots, tmd)
    ys = _experts(block_e, block_half, next_e, n_used, xs_rows.reshape(n_slots * LANE_CHUNKS, LANES),
                  w_gu[0], b_gu[0][:, None, :], w_down[0], b_down[0][:, None, :])
    ys_tiles = ys.reshape(-1, LANE_CHUNKS, LANES)

    gates_t = gate.T
    ctiles = _tile_major(dest, tmc)
    y_p = _combine(ctiles[:tp // tmc], ys_tiles, gates_t, x1, mod_p[5], row(norm_post_ffn), tmc, lp // tmc, 0)
    y_s = _combine(ctiles[tp // tmc:], ys_tiles, gates_t, x1, mod_s[5], row(norm_post_ffn), tmc, 1, tp // tmc)

    return (y_p.reshape(bp, lp, D_MODEL), y_s.reshape(bs, ls, D_MODEL),
            conv_p[None], h_p.reshape(1, bp, LRU_WIDTH), s_p[None],
            conv_s[None], h_s.reshape(1, bs, LRU_WIDTH), s_s[None])
```

```python
import functools
import math

import jax
import jax.numpy as jnp
from jax import lax
from jax.experimental import pallas as pl
from jax.experimental.pallas import tpu as pltpu

F32 = jnp.float32
BF16 = jnp.bfloat16
I32 = jnp.int32

D_MODEL = 1024
LRU_WIDTH = 512
CONV_WIDTH = 4
LRU_C = 8.0
HG_WIDTH = 512
HG_HEAD_DIM = 128
HG_HEADS = 4
IN_COLS = 2 * LRU_WIDTH + 4 * HG_WIDTH
N_EXPERTS = 32
TOP_K = 4
D_EXPERT = 1024
SWIGLU_LIMIT = 7.0
SWIGLU_ALPHA = 1.702
RMS_EPS = 1e-6

C_XL, C_YL, C_Q, C_F, C_V, C_G = 0, 512, 1024, 1536, 2048, 2560

SUBLANES = 8
LANES = 128
LANE_CHUNKS = D_MODEL // LANES
CONV_HDR = SUBLANES
HG_CHUNK = 64
HG_MAX_EXPONENT = 60.0
POST_PARTS = 2
MOE_BM = 512
ZERO_ROWS = 64
VMEM_LIMIT = 56 * 1024 * 1024

_NT = (((1,), (1,)), ((), ()))
_TN = (((0,), (0,)), ((), ()))


def _cparams(sem):
    return pltpu.CompilerParams(dimension_semantics=sem, vmem_limit_bytes=VMEM_LIMIT)


def _rms(x, g):
    return x * lax.rsqrt(jnp.mean(x * x, axis=-1, keepdims=True) + RMS_EPS) * g


def _gelu_tanh(x):
    c = math.sqrt(2.0 / math.pi)
    return 0.5 * x * (1.0 + jnp.tanh(c * (x + 0.044715 * (x * x * x))))


def _ada_kernel(c_ref, w_ref, b_ref, o_ref):
    c = c_ref[...]
    s = c * jax.nn.sigmoid(c)
    o_ref[...] = jnp.dot(s.astype(BF16), w_ref[...].astype(BF16), preferred_element_type=F32) + b_ref[...]


def _ada(c_all, ada_w, ada_b):
    n = c_all.shape[0]
    tn = 1024
    return pl.pallas_call(
        _ada_kernel,
        out_shape=jax.ShapeDtypeStruct((n, 6 * D_MODEL), F32),
        grid=(6 * D_MODEL // tn,),
        in_specs=[pl.BlockSpec((n, D_MODEL), lambda j: (0, 0)),
                  pl.BlockSpec((D_MODEL, tn), lambda j: (0, j)),
                  pl.BlockSpec((1, tn), lambda j: (0, j))],
        out_specs=pl.BlockSpec((n, tn), lambda j: (0, j)),
        compiler_params=_cparams(("arbitrary",)),
    )(c_all, ada_w, ada_b)


def _inproj_kernel(x_ref, sc_ref, sh_ref, g_ref, w_ref, o_ref):
    h = _rms(x_ref[...], g_ref[...]) * (1.0 + sc_ref[...]) + sh_ref[...]
    o_ref[...] = jnp.dot(h.astype(BF16), w_ref[...], preferred_element_type=F32)


def _mod_spec(mod, tm, tiles_per_seq):
    if mod.ndim == 3:
        return pl.BlockSpec((None, 1, D_MODEL), lambda i: (i // tiles_per_seq, 0, 0))
    return pl.BlockSpec((tm, D_MODEL), lambda i: (i, 0))


def _inproj(x, sc, sh, g, w_bf, tm, tiles_per_seq):
    t = x.shape[0]
    return pl.pallas_call(
        _inproj_kernel,
        out_shape=jax.ShapeDtypeStruct((t, IN_COLS), F32),
        grid=(t // tm,),
        in_specs=[pl.BlockSpec((tm, D_MODEL), lambda i: (i, 0)),
                  _mod_spec(sc, tm, tiles_per_seq), _mod_spec(sh, tm, tiles_per_seq),
                  pl.BlockSpec((1, D_MODEL), lambda i: (0, 0)),
                  pl.BlockSpec((D_MODEL, IN_COLS), lambda i: (0, 0))],
        out_specs=pl.BlockSpec((tm, IN_COLS), lambda i: (i, 0)),
        compiler_params=_cparams(("arbitrary",)),
    )(x, sc, sh, g, w_bf)


def _sigmoid(x):
    return 0.5 * jnp.tanh(0.5 * x) + 0.5


def _group_rows(x):
    rows, w = x.shape
    return x.reshape(rows // SUBLANES, SUBLANES, w)


def _scan_rows(a, u, h0):
    a3, u3 = _group_rows(a), _group_rows(u)
    r3 = lax.broadcasted_iota(I32, a3.shape, 1)
    s = 1
    while s < SUBLANES:
        keep = r3 >= s
        a_sh = jnp.where(keep, pltpu.roll(a3, s, 1), 1.0)
        u_sh = jnp.where(keep, pltpu.roll(u3, s, 1), 0.0)
        u3 = a3 * u_sh + u3
        a3 = a3 * a_sh
        s *= 2
    out, h = [], h0
    for i in range(a3.shape[0]):
        hg = u3[i] + a3[i] * h
        h = hg[SUBLANES - 1:SUBLANES, :]
        out.append(hg)
    return jnp.concatenate(out, axis=0), h


def _chunk_cumsum(x, chunk):
    x3 = _group_rows(x)
    r3 = lax.broadcasted_iota(I32, x3.shape, 1)
    s = 1
    while s < SUBLANES:
        x3 = x3 + jnp.where(r3 >= s, pltpu.roll(x3, s, 1), 0.0)
        s *= 2
    out, carry = [], None
    for i in range(x3.shape[0]):
        cur = x3[i] if i % (chunk // SUBLANES) == 0 else x3[i] + carry
        carry = cur[SUBLANES - 1:SUBLANES, :]
        out.append(cur)
    return jnp.concatenate(out, axis=0)


def _lru_gates(xc, wg_ref, bg_ref, lam_ref):
    gates = jnp.dot(xc.astype(BF16), wg_ref[...], preferred_element_type=F32) + bg_ref[...]
    r = _sigmoid(gates[:, :LRU_WIDTH])
    ig = _sigmoid(gates[:, LRU_WIDTH:])
    z = -lam_ref[...]
    softplus = jnp.maximum(z, 0.0) + jnp.log1p(jnp.exp(-jnp.abs(z)))
    log_a = -LRU_C * r * softplus
    a = jnp.exp(log_a)
    th = jnp.tanh(log_a)
    mult2 = -2.0 * th / (1.0 - th)
    return a, jnp.sqrt(mult2), ig


def _conv4(xbuf, xl, cw_ref, cb_ref, rows):
    cw = cw_ref[...]
    h = CONV_HDR
    return (cb_ref[...] + cw[0:1] * xbuf[h - 3:h - 3 + rows, :] + cw[1:2] * xbuf[h - 2:h - 2 + rows, :]
            + cw[2:3] * xbuf[h - 1:h - 1 + rows, :] + cw[3:4] * xl)


def _conv4_rolled(xl, prev, cw_ref, cb_ref):
    cw = cw_ref[...]
    row8 = lax.broadcasted_iota(I32, prev.shape, 0)
    acc = cb_ref[...] + cw[CONV_WIDTH - 1:CONV_WIDTH] * xl
    for s in range(1, CONV_WIDTH):
        sh = pltpu.roll(xl, s, 0)
        top = jnp.where(row8 < s, pltpu.roll(prev, s, 0), sh[0:SUBLANES, :])
        sh = jnp.concatenate([top, sh[SUBLANES:, :]], axis=0)
        acc = acc + cw[CONV_WIDTH - 1 - s:CONV_WIDTH - s] * sh
    return acc


def _head_norm_gate(o, gate, hgn):
    return _rms(o, hgn) * _sigmoid(gate)


def _hgrn_window(q, k, v, b, state, steps, state_is_transposed):
    row = lax.broadcasted_iota(I32, q.shape, 0)
    valid = row < steps
    b_end = b[steps - 1:steps, :]
    q_in = jnp.where(valid, q * jnp.exp(b), 0.0).astype(BF16)
    if state_is_transposed:
        o = lax.dot_general(q_in, state.astype(BF16), _NT, preferred_element_type=F32)
    else:
        o = jnp.dot(q_in, state.astype(BF16), preferred_element_type=F32)
    for d in range(steps):
        ok = valid & (row >= d)
        k_d = k if d == 0 else pltpu.roll(k, d, 0)
        v_d = v if d == 0 else pltpu.roll(v, d, 0)
        b_d = b if d == 0 else pltpu.roll(b, d, 0)
        decay = jnp.exp(jnp.where(ok, b - b_d, 0.0))
        wgt = jnp.sum(jnp.where(ok, q * k_d * decay, 0.0), axis=-1, keepdims=True)
        o = o + wgt * v_d
    k_out = jnp.where(valid, k * jnp.exp(jnp.where(valid, b_end - b, 0.0)), 0.0).astype(BF16)
    v_ok = jnp.where(valid, v, 0.0).astype(BF16)
    keep = jnp.exp(b_end)
    if state_is_transposed:
        s_new = state * keep + lax.dot_general(v_ok, k_out, _TN, preferred_element_type=F32)
    else:
        keep_col = jnp.broadcast_to(keep, q.shape).T[:, 0:1]
        s_new = state * keep_col + lax.dot_general(k_out, v_ok, _TN, preferred_element_type=F32)
    return o, s_new


def _tile_cumsum(x):
    row = lax.broadcasted_iota(I32, x.shape, 0)
    s = 1
    while s < SUBLANES:
        x = x + jnp.where(row >= s, pltpu.roll(x, s, 0), 0.0)
        s *= 2
    return x


def _mix_tile(proj_ref, cw_ref, cb_ref, wg_ref, bg_ref, lam_ref, lb_ref, hgn_ref, m_ref, xprev, hcar, st, st0, obuf,
              seq_start, tc, side_work):
    w = LRU_WIDTH
    side_work = iter(side_work)

    next(side_work)()
    xl = proj_ref[:, C_XL:C_XL + w]
    xc = _conv4_rolled(xl, xprev[...], cw_ref, cb_ref)
    xprev[...] = xl[tc - SUBLANES:tc, :]
    a, mult, ig = _lru_gates(xc, wg_ref, bg_ref, lam_ref)
    row = lax.broadcasted_iota(I32, (tc, w), 0)
    if seq_start is not None:
        mult = jnp.where((row == 0) & seq_start, 1.0, mult)
    h, h_last = _scan_rows(a, mult * ig * xc, hcar[0:1, :])
    hcar[0:1, :] = h_last
    m_ref[:, 0:w] = (h * _gelu_tanh(proj_ref[:, C_YL:C_YL + w])).astype(BF16)

    next(side_work)()
    lb = lb_ref[...]
    f = lb + (1.0 - lb) * _sigmoid(proj_ref[:, C_F:C_F + w])
    k = 1.0 - f
    b = _chunk_cumsum(jnp.log(f), HG_CHUNK)
    q = proj_ref[:, C_Q:C_Q + w]
    v = proj_ref[:, C_V:C_V + w]
    mid = HG_CHUNK // 2
    st0[...] = st[...]
    tri = (lax.broadcasted_iota(I32, (HG_CHUNK, HG_CHUNK), 0)
           >= lax.broadcasted_iota(I32, (HG_CHUNK, HG_CHUNK), 1))
    states = [st[hd] for hd in range(HG_HEADS)]
    for c in range(tc // HG_CHUNK):
        next(side_work)()
        rs = slice(c * HG_CHUNK, (c + 1) * HG_CHUNK)
        for hd in range(HG_HEADS):
            cs = slice(hd * HG_HEAD_DIM, (hd + 1) * HG_HEAD_DIM)
            s_t = states[hd]
            bc, qc, kc = b[rs, cs], q[rs, cs], k[rs, cs]
            vb = v[rs, cs].astype(BF16)
            b_mid = bc[mid - 1:mid, :]
            b_end = bc[HG_CHUNK - 1:HG_CHUNK, :]
            qp = (qc * jnp.exp(bc - b_mid)).astype(BF16)
            kp = (kc * jnp.exp(b_mid - bc)).astype(BF16)
            att = lax.dot_general(qp, kp, _NT, preferred_element_type=F32)
            att = jnp.where(tri, att, 0.0)
            q_in = (qc * jnp.exp(bc)).astype(BF16)
            o = (jnp.dot(att.astype(BF16), vb, preferred_element_type=F32)
                 + lax.dot_general(q_in, s_t.astype(BF16), _NT, preferred_element_type=F32))
            k_out = (kc * jnp.exp(b_end - bc)).astype(BF16)
            states[hd] = s_t * jnp.exp(b_end) + lax.dot_general(vb, k_out, _TN, preferred_element_type=F32)
            obuf[rs, cs] = o
    for hd in range(HG_HEADS):
        st[hd] = states[hd]

    half_decay = None
    for c in range(tc // HG_CHUNK):
        b_mid = b[c * HG_CHUNK + mid - 1:c * HG_CHUNK + mid, :]
        b_end = b[(c + 1) * HG_CHUNK - 1:(c + 1) * HG_CHUNK, :]
        worst = jnp.maximum(-b_mid, b_mid - b_end)
        half_decay = worst if half_decay is None else jnp.maximum(half_decay, worst)

    @pl.when(jnp.logical_not(jnp.max(half_decay) < HG_MAX_EXPONENT))
    def _():
        st[...] = st0[...]

        def window(gi, carry):
            rows = pl.ds(pl.multiple_of(gi * SUBLANES, SUBLANES), SUBLANES)
            f8 = lb + (1.0 - lb) * _sigmoid(proj_ref[rows, C_F:C_F + w])
            b8 = _tile_cumsum(jnp.log(f8))
            k8 = 1.0 - f8
            q8 = proj_ref[rows, C_Q:C_Q + w]
            v8 = proj_ref[rows, C_V:C_V + w]
            for hd in range(HG_HEADS):
                cs = slice(hd * HG_HEAD_DIM, (hd + 1) * HG_HEAD_DIM)
                o, s_new = _hgrn_window(q8[:, cs], k8[:, cs], v8[:, cs], b8[:, cs], st[hd], SUBLANES, True)
                st[hd] = s_new
                obuf[rows, cs] = o
            return carry
        lax.fori_loop(0, tc // SUBLANES, window, 0)

    hgn = hgn_ref[...]
    for hd in range(HG_HEADS):
        cs = slice(hd * HG_HEAD_DIM, (hd + 1) * HG_HEAD_DIM)
        gate = proj_ref[:, C_G + hd * HG_HEAD_DIM:C_G + (hd + 1) * HG_HEAD_DIM]
        m_ref[:, w + hd * HG_HEAD_DIM:w + (hd + 1) * HG_HEAD_DIM] = (
            _head_norm_gate(obuf[:, cs], gate, hgn).astype(BF16))


def _mixer_prompt_kernel(x_ref, xn_ref, sc_ref, sh_ref, scn_ref, shn_ref, g_ref, win_ref, cw_ref, cb_ref, wg_ref,
                         bg_ref, lam_ref, lb_ref, hgn_ref, m_ref, conv_ref, h_ref, s_ref,
                         proj_a, proj_b, hn_buf, xprev, hcar, st, st0, obuf, *, tc):
    j = pl.program_id(1)
    n_phases = 2 + tc // HG_CHUNK
    cols = IN_COLS // n_phases

    def project(x, sc, sh, out_ref):
        def chunk(c):
            if c == 0:
                hn_buf[...] = (_rms(x(), g_ref[...]) * (1.0 + sc[...]) + sh[...]).astype(BF16)
            out_ref[:, c * cols:(c + 1) * cols] = jnp.dot(hn_buf[...], win_ref[:, c * cols:(c + 1) * cols],
                                                          preferred_element_type=F32)
        return [functools.partial(chunk, c) for c in range(n_phases)]

    def mix(proj_ref, first_row, seq_start, side_work):
        _mix_tile(proj_ref, cw_ref, cb_ref, wg_ref, bg_ref, lam_ref, lb_ref, hgn_ref,
                  m_ref.at[pl.ds(first_row, tc), :], xprev, hcar, st, st0, obuf, seq_start, tc, side_work)

    @pl.when((pl.program_id(0) == 0) & (j == 0))
    def _():
        for thunk in project(lambda: x_ref[0:tc, :], sc_ref, sh_ref, proj_a):
            thunk()

    @pl.when(j == 0)
    def _():
        xprev[...] = jnp.zeros_like(xprev)
        hcar[...] = jnp.zeros_like(hcar)
        st[...] = jnp.zeros_like(st)

    mix(proj_a, 0, j == 0, project(lambda: x_ref[tc:2 * tc, :], sc_ref, sh_ref, proj_b))
    mix(proj_b, tc, None, project(lambda: xn_ref[...], scn_ref, shn_ref, proj_a))

    @pl.when(j == pl.num_programs(1) - 1)
    def _():
        conv_ref[...] = xprev[SUBLANES - (CONV_WIDTH - 1):SUBLANES, :]
        h_ref[...] = hcar[0:1, :]
        for hd in range(HG_HEADS):
            s_ref[hd] = st[hd].T


def _mixer_prompt(x, sc, sh, g, w_in_bf, nb, seq, mw, tc):
    cw, cb, wg, bg, lam, lb, hgn = mw
    nj = seq // (2 * tc)
    last_tile = nb * seq // tc - 1
    next_tile = lambda b, j: jnp.minimum(2 * (b * nj + j) + 2, last_tile)
    const = lambda shape: pl.BlockSpec(shape, lambda b, j: tuple(0 for _ in shape))
    per_seq = lambda: pl.BlockSpec((None, 1, D_MODEL), lambda b, j: (b, 0, 0))
    next_seq = lambda: pl.BlockSpec((None, 1, D_MODEL), lambda b, j: (next_tile(b, j) // (2 * nj), 0, 0))
    return pl.pallas_call(
        functools.partial(_mixer_prompt_kernel, tc=tc),
        out_shape=(jax.ShapeDtypeStruct((nb * seq, D_MODEL), BF16),
                   jax.ShapeDtypeStruct((nb, CONV_WIDTH - 1, LRU_WIDTH), F32),
                   jax.ShapeDtypeStruct((nb, 1, LRU_WIDTH), F32),
                   jax.ShapeDtypeStruct((nb, HG_HEADS, HG_HEAD_DIM, HG_HEAD_DIM), F32)),
        grid=(nb, nj),
        in_specs=[pl.BlockSpec((2 * tc, D_MODEL), lambda b, j: (b * nj + j, 0)),
                  pl.BlockSpec((tc, D_MODEL), lambda b, j: (next_tile(b, j), 0)),
                  per_seq(), per_seq(), next_seq(), next_seq(), const((1, D_MODEL)), const((D_MODEL, IN_COLS)),
                  const((CONV_WIDTH, LRU_WIDTH)), const((1, LRU_WIDTH)),
                  const((LRU_WIDTH, 2 * LRU_WIDTH)), const((1, 2 * LRU_WIDTH)),
                  const((1, LRU_WIDTH)), const((1, HG_WIDTH)), const((1, HG_HEAD_DIM))],
        out_specs=(pl.BlockSpec((2 * tc, D_MODEL), lambda b, j: (b * nj + j, 0)),
                   pl.BlockSpec((None, CONV_WIDTH - 1, LRU_WIDTH), lambda b, j: (b, 0, 0)),
                   pl.BlockSpec((None, 1, LRU_WIDTH), lambda b, j: (b, 0, 0)),
                   pl.BlockSpec((None, HG_HEADS, HG_HEAD_DIM, HG_HEAD_DIM), lambda b, j: (b, 0, 0, 0))),
        scratch_shapes=[pltpu.VMEM((tc, IN_COLS), F32),
                        pltpu.VMEM((tc, IN_COLS), F32),
                        pltpu.VMEM((tc, D_MODEL), BF16),
                        pltpu.VMEM((SUBLANES, LRU_WIDTH), F32),
                        pltpu.VMEM((SUBLANES, LRU_WIDTH), F32),
                        pltpu.VMEM((HG_HEADS, HG_HEAD_DIM, HG_HEAD_DIM), F32),
                        pltpu.VMEM((HG_HEADS, HG_HEAD_DIM, HG_HEAD_DIM), F32),
                        pltpu.VMEM((tc, HG_WIDTH), F32)],
        compiler_params=_cparams(("arbitrary", "arbitrary")),
    )(x, x, sc, sh, sc, sh, g, w_in_bf, cw, cb, wg, bg, lam, lb, hgn)


def _mixer_sample_kernel(proj_ref, cst_ref, h0_ref, s0_ref, cw_ref, cb_ref, wg_ref, bg_ref, lam_ref, lb_ref,
                         hgn_ref, m_ref, conv_ref, h_ref, s_ref, xbuf, *, steps):
    for bi in range(proj_ref.shape[0]):
        _mixer_sample_one(proj_ref.at[bi], cst_ref.at[bi], h0_ref.at[bi], s0_ref.at[bi], cw_ref, cb_ref, wg_ref,
                          bg_ref, lam_ref, lb_ref, hgn_ref, m_ref.at[bi], conv_ref.at[bi], h_ref.at[bi],
                          s_ref.at[bi], xbuf.at[bi], steps)


def _mixer_sample_one(proj_ref, cst_ref, h0_ref, s0_ref, cw_ref, cb_ref, wg_ref, bg_ref, lam_ref, lb_ref,
                      hgn_ref, m_ref, conv_ref, h_ref, s_ref, xbuf, steps):
    w = LRU_WIDTH
    rows = SUBLANES
    row = lax.broadcasted_iota(I32, (rows, w), 0)

    xl = proj_ref[:, C_XL:C_XL + w]
    xbuf[0:CONV_HDR, :] = cst_ref[...]
    xbuf[CONV_HDR:CONV_HDR + rows, :] = xl
    xc = _conv4(xbuf, xl, cw_ref, cb_ref, rows)
    conv_ref[...] = xbuf[CONV_HDR + steps - 3:CONV_HDR + steps, :]
    a, mult, ig = _lru_gates(xc, wg_ref, bg_ref, lam_ref)
    u = mult * ig * xc
    h = h0_ref[...]
    hs = jnp.zeros((rows, w), F32)
    for t in range(steps):
        h = a[t:t + 1, :] * h + u[t:t + 1, :]
        hs = jnp.where(row == t, h, hs)
    h_ref[...] = h
    m_ref[:, 0:w] = (hs * _gelu_tanh(proj_ref[:, C_YL:C_YL + w])).astype(BF16)

    lb = lb_ref[...]
    f = lb + (1.0 - lb) * _sigmoid(proj_ref[:, C_F:C_F + w])
    k = 1.0 - f
    b = _tile_cumsum(jnp.log(f))
    q = proj_ref[:, C_Q:C_Q + w]
    v = proj_ref[:, C_V:C_V + w]
    hgn = hgn_ref[...]
    for hd in range(HG_HEADS):
        cs = slice(hd * HG_HEAD_DIM, (hd + 1) * HG_HEAD_DIM)
        o, s_new = _hgrn_window(q[:, cs], k[:, cs], v[:, cs], b[:, cs], s0_ref[hd], steps, False)
        s_ref[hd] = s_new
        gate = proj_ref[:, C_G + hd * HG_HEAD_DIM:C_G + (hd + 1) * HG_HEAD_DIM]
        m_ref[:, w + hd * HG_HEAD_DIM:w + (hd + 1) * HG_HEAD_DIM] = _head_norm_gate(o, gate, hgn).astype(BF16)


def _mixer_sample(proj_pad, cst_hdr, h0, s0, mw, steps):
    cw, cb, wg, bg, lam, lb, hgn = mw
    nb = proj_pad.shape[0]
    per_step = _pick_tile(nb, (8, 4, 2, 1))
    const = lambda shape: pl.BlockSpec(shape, lambda b: tuple(0 for _ in shape))
    per_b = lambda shape: pl.BlockSpec((per_step,) + shape, lambda b: (b,) + tuple(0 for _ in shape))
    return pl.pallas_call(
        functools.partial(_mixer_sample_kernel, steps=steps),
        out_shape=(jax.ShapeDtypeStruct((nb, SUBLANES, D_MODEL), BF16),
                   jax.ShapeDtypeStruct((nb, CONV_WIDTH - 1, LRU_WIDTH), F32),
                   jax.ShapeDtypeStruct((nb, 1, LRU_WIDTH), F32),
                   jax.ShapeDtypeStruct((nb, HG_HEADS, HG_HEAD_DIM, HG_HEAD_DIM), F32)),
        grid=(nb // per_step,),
        in_specs=[per_b((SUBLANES, IN_COLS)), per_b((CONV_HDR, LRU_WIDTH)), per_b((1, LRU_WIDTH)),
                  per_b((HG_HEADS, HG_HEAD_DIM, HG_HEAD_DIM)),
                  const((CONV_WIDTH, LRU_WIDTH)), const((1, LRU_WIDTH)),
                  const((LRU_WIDTH, 2 * LRU_WIDTH)), const((1, 2 * LRU_WIDTH)),
                  const((1, LRU_WIDTH)), const((1, HG_WIDTH)), const((1, HG_HEAD_DIM))],
        out_specs=(per_b((SUBLANES, D_MODEL)), per_b((CONV_WIDTH - 1, LRU_WIDTH)), per_b((1, LRU_WIDTH)),
                   per_b((HG_HEADS, HG_HEAD_DIM, HG_HEAD_DIM))),
        scratch_shapes=[pltpu.VMEM((per_step, CONV_HDR + SUBLANES, LRU_WIDTH), F32)],
        compiler_params=_cparams(("arbitrary",)),
    )(proj_pad, cst_hdr, h0, s0, cw, cb, wg, bg, lam, lb, hgn)


def _to_row_tiles(ref, val, rows, base=0):
    for c in range(LANE_CHUNKS):
        ref[pl.ds(base + c, rows, stride=LANE_CHUNKS), :] = val[:, c * LANES:(c + 1) * LANES]


def _from_row_tiles(ref, base, rows):
    return jnp.concatenate(
        [ref[pl.ds(base + c, rows, stride=LANE_CHUNKS), :] for c in range(LANE_CHUNKS)], axis=-1)


def _post_kernel(mp_ref, ms_ref, xp_ref, xs_ref, gtmp_ref, scfp_ref, shfp_ref, gtms_ref, scfs_ref, shfs_ref,
                 npost_ref, npre_ref, wout_ref, rwt_ref, rb_ref,
                 x1_ref, hf_ref, idx_ref, gate_ref, rank_ref, cnt_out_ref, cnt, *, n_p):
    i = pl.program_id(0)
    tm = xp_ref.shape[0]

    @pl.when(i == 0)
    def _():
        cnt[...] = jnp.zeros_like(cnt)

    def run(m_ref, x_ref, gtm_ref, scf_ref, shf_ref):
        n_parts = POST_PARTS if (tm // POST_PARTS) % LANES == 0 else 1
        th = tm // n_parts
        mod_rows = lambda ref, rs: ref[rs, :] if ref.shape[0] == tm else ref[...]
        earlier = (lax.broadcasted_iota(I32, (th, th), 0) < lax.broadcasted_iota(I32, (th, th), 1))
        earlier = jnp.where(earlier, 1.0, 0.0).astype(BF16)
        pending = []
        for part in range(n_parts):
            rs = slice(part * th, (part + 1) * th)
            mix = jnp.dot(m_ref[rs, :], wout_ref[...], preferred_element_type=F32)
            x1 = x_ref[rs, :] + mod_rows(gtm_ref, rs) * _rms(mix, npost_ref[...])
            x1_ref[rs, :] = x1
            hf = _rms(x1, npre_ref[...]) * (1.0 + mod_rows(scf_ref, rs)) + mod_rows(shf_ref, rs)
            _to_row_tiles(hf_ref, hf, th, part * th * LANE_CHUNKS)
            logits = lax.dot_general(rwt_ref[...], hf, _NT, precision=lax.Precision.HIGHEST,
                                     preferred_element_type=F32) + rb_ref[...]
            e_iota = lax.broadcasted_iota(I32, logits.shape, 0)
            vals, idxs = [], []
            for _ in range(TOP_K):
                mx = jnp.max(logits, axis=0, keepdims=True)
                ix = jnp.min(jnp.where(logits == mx, e_iota, N_EXPERTS), axis=0, keepdims=True)
                vals.append(mx)
                idxs.append(ix)
                logits = jnp.where(e_iota == ix, -jnp.inf, logits)
            exps = [jnp.exp(vv - vals[0]) for vv in vals]
            den = exps[0] + exps[1] + exps[2] + exps[3]
            for kk in range(TOP_K):
                idx_ref[kk:kk + 1, rs] = idxs[kk]
                gate_ref[kk:kk + 1, rs] = exps[kk] / den
                hit = e_iota == idxs[kk]
                onehot = jnp.where(hit, 1.0, 0.0)
                before = jnp.dot(onehot.astype(BF16), earlier, preferred_element_type=F32)
                pending.append((kk, rs, hit, before, jnp.sum(onehot, axis=1, keepdims=True)))
        base = cnt[...]
        for kk, rs, hit, before, count in pending:
            rank = jnp.sum(jnp.where(hit, before + base, 0.0), axis=0, keepdims=True)
            rank_ref[kk:kk + 1, rs] = rank.astype(I32)
            base = base + count
        cnt[...] = base

    @pl.when(i < n_p)
    def _():
        run(mp_ref, xp_ref, gtmp_ref, scfp_ref, shfp_ref)

    @pl.when(i >= n_p)
    def _():
        run(ms_ref, xs_ref, gtms_ref, scfs_ref, shfs_ref)

    @pl.when(i == pl.num_programs(0) - 1)
    def _():
        cnt_out_ref[...] = cnt[...]


def _post(m_p, m_s, x_p, x_s, mod_p, mod_s, npost, npre, wout_bf, rwt, rb, tm, tiles_per_seq):
    n_p, n_s = x_p.shape[0] // tm, x_s.shape[0] // tm
    t = (n_p + n_s) * tm
    first = lambda i: jnp.minimum(i, n_p - 1)
    second = lambda i: jnp.maximum(i - n_p, 0)
    row = lambda: pl.BlockSpec((1, D_MODEL), lambda i: (0, 0))
    tok_p = lambda: pl.BlockSpec((tm, D_MODEL), lambda i: (first(i), 0))
    tok_s = lambda: pl.BlockSpec((tm, D_MODEL), lambda i: (second(i), 0))
    seq_p = lambda: pl.BlockSpec((None, 1, D_MODEL), lambda i: (first(i) // tiles_per_seq, 0, 0))
    per_k = lambda: pl.BlockSpec((TOP_K, tm), lambda i: (0, i))
    return pl.pallas_call(
        functools.partial(_post_kernel, n_p=n_p),
        out_shape=(jax.ShapeDtypeStruct((t, D_MODEL), F32),
                   jax.ShapeDtypeStruct((t * LANE_CHUNKS, LANES), F32),
                   jax.ShapeDtypeStruct((TOP_K, t), I32), jax.ShapeDtypeStruct((TOP_K, t), F32),
                   jax.ShapeDtypeStruct((TOP_K, t), I32), jax.ShapeDtypeStruct((N_EXPERTS, 1), F32)),
        grid=(n_p + n_s,),
        in_specs=[tok_p(), tok_s(), tok_p(), tok_s(), seq_p(), seq_p(), seq_p(), tok_s(), tok_s(), tok_s(),
                  row(), row(),
                  pl.BlockSpec((D_MODEL, D_MODEL), lambda i: (0, 0)),
                  pl.BlockSpec((N_EXPERTS, D_MODEL), lambda i: (0, 0)),
                  pl.BlockSpec((N_EXPERTS, 1), lambda i: (0, 0))],
        out_specs=(pl.BlockSpec((tm, D_MODEL), lambda i: (i, 0)),
                   pl.BlockSpec((tm * LANE_CHUNKS, LANES), lambda i: (i, 0)),
                   per_k(), per_k(), per_k(), pl.BlockSpec((N_EXPERTS, 1), lambda i: (0, 0))),
        scratch_shapes=[pltpu.VMEM((N_EXPERTS, 1), F32)],
        compiler_params=_cparams(("arbitrary",)),
    )(m_p, m_s, x_p, x_s, *mod_p, *mod_s, npost, npre, wout_bf, rwt, rb)


def _dispatch_kernel(fill_lo_ref, fill_hi_ref, dest_ref, hf_ref, xs_hbm, zeros, row_sem, pad_sem, *, tm):
    i = pl.program_id(0)
    n_rows = TOP_K * tm

    @pl.when(i == 0)
    def _():
        zeros[...] = jnp.zeros_like(zeros)

        def per_expert(e, totals):
            lo, hi = fill_lo_ref[e], fill_hi_ref[e]
            mid = jnp.minimum((lo + ZERO_ROWS - 1) // ZERO_ROWS * ZERO_ROWS, hi)

            def per_row(s, carry):
                pltpu.make_async_copy(zeros.at[0], xs_hbm.at[s], pad_sem).start()
                return carry
            lax.fori_loop(lo, mid, per_row, 0)

            def per_chunk(c, carry):
                pltpu.make_async_copy(zeros, xs_hbm.at[pl.ds(pl.multiple_of(c * ZERO_ROWS, ZERO_ROWS), ZERO_ROWS)],
                                      pad_sem).start()
                return carry
            lax.fori_loop(mid // ZERO_ROWS, hi // ZERO_ROWS, per_chunk, 0)
            return totals[0] + (mid - lo), totals[1] + (hi - mid) // ZERO_ROWS
        n_rows_filled, n_chunks_filled = lax.fori_loop(0, N_EXPERTS, per_expert, (0, 0))

        def drain_row(s, carry):
            pltpu.make_async_copy(zeros.at[0], xs_hbm.at[0], pad_sem).wait()
            return carry
        lax.fori_loop(0, n_rows_filled, drain_row, 0)

        def drain_chunk(c, carry):
            pltpu.make_async_copy(zeros, xs_hbm.at[pl.ds(0, ZERO_ROWS)], pad_sem).wait()
            return carry
        lax.fori_loop(0, n_chunks_filled, drain_chunk, 0)

    for n in range(n_rows):
        pltpu.make_async_copy(hf_ref.at[pl.ds((n % tm) * LANE_CHUNKS, LANE_CHUNKS), :],
                              xs_hbm.at[dest_ref[0, n]], row_sem).start(priority=n % 2)
    pltpu.make_async_copy(xs_hbm.at[pl.ds(0, n_rows)], xs_hbm.at[pl.ds(0, n_rows)], row_sem).wait()


def _dispatch(fill_lo, fill_hi, dest_tiles, hf_rows, n_slots, tm):
    return pl.pallas_call(
        functools.partial(_dispatch_kernel, tm=tm),
        out_shape=jax.ShapeDtypeStruct((n_slots, LANE_CHUNKS, LANES), F32),
        grid_spec=pltpu.PrefetchScalarGridSpec(
            num_scalar_prefetch=2, grid=(dest_tiles.shape[0],),
            in_specs=[pl.BlockSpec((None, 1, TOP_K * tm), lambda i, lo, hi: (i, 0, 0), memory_space=pltpu.SMEM),
                      pl.BlockSpec((tm * LANE_CHUNKS, LANES), lambda i, lo, hi: (i, 0))],
            out_specs=pl.BlockSpec(memory_space=pl.ANY),
            scratch_shapes=[pltpu.VMEM((ZERO_ROWS, LANE_CHUNKS, LANES), F32),
                            pltpu.SemaphoreType.DMA, pltpu.SemaphoreType.DMA]),
        compiler_params=_cparams(("arbitrary",)),
    )(fill_lo, fill_hi, dest_tiles, hf_rows)


def _expert_kernel(be_ref, half_ref, nxt_ref, nu_ref, xs_ref, wgu_hbm, bgu_ref, wd_hbm, bd_ref, ys_ref,
                   wgu_f32, wd_f32, wgu_bf, wd_bf, w_sem):
    i = pl.program_id(0)
    n_used = nu_ref[0]

    def weight_copies(e):
        return (pltpu.make_async_copy(wgu_hbm.at[e], wgu_f32, w_sem.at[0]),
                pltpu.make_async_copy(wd_hbm.at[e], wd_f32, w_sem.at[1]))

    @pl.when(i == 0)
    def _():
        for cp in weight_copies(be_ref[0]):
            cp.start()

    @pl.when(i < n_used)
    def _():
        prev = be_ref[jnp.maximum(i - 1, 0)]

        @pl.when((i == 0) | (be_ref[i] != prev))
        def _():
            for cp in weight_copies(be_ref[i]):
                cp.wait()
            wgu_bf[...] = wgu_f32[...].astype(BF16)
            wd_bf[...] = wd_f32[...].astype(BF16)
            nxt = nxt_ref[be_ref[i]]

            @pl.when(nxt >= 0)
            def _():
                for cp in weight_copies(nxt):
                    cp.start()

        def swiglu_rows(rows):
            x = _from_row_tiles(xs_ref, 0, rows).astype(BF16)
            hu = jnp.dot(x, wgu_bf[...], preferred_element_type=F32) + bgu_ref[...]
            glu = jnp.minimum(hu[:, :D_EXPERT], SWIGLU_LIMIT)
            lin = jnp.clip(hu[:, D_EXPERT:], -SWIGLU_LIMIT, SWIGLU_LIMIT)
            act = glu * jax.nn.sigmoid(SWIGLU_ALPHA * glu) * (lin + 1.0)
            y = jnp.dot(act.astype(BF16), wd_bf[...], preferred_element_type=F32) + bd_ref[...]
            _to_row_tiles(ys_ref, y, rows)

        @pl.when(half_ref[i] == 0)
        def _():
            swiglu_rows(MOE_BM)

        @pl.when(half_ref[i] != 0)
        def _():
            swiglu_rows(MOE_BM // 2)
            ys_ref[MOE_BM // 2 * LANE_CHUNKS:, :] = jnp.zeros((MOE_BM // 2 * LANE_CHUNKS, LANES), F32)

    @pl.when(i >= n_used)
    def _():
        ys_ref[...] = jnp.zeros_like(ys_ref)


def _experts(block_e, block_half, next_e, n_used, xs2d, w_gu, b_gu, w_down, b_down):
    n_blocks = block_e.shape[0]
    rows = MOE_BM * LANE_CHUNKS
    live = lambda i, be, hf, nx, nu: (jnp.minimum(i, nu[0] - 1), 0)
    return pl.pallas_call(
        _expert_kernel,
        out_shape=jax.ShapeDtypeStruct(xs2d.shape, F32),
        grid_spec=pltpu.PrefetchScalarGridSpec(
            num_scalar_prefetch=4,
            grid=(n_blocks,),
            in_specs=[pl.BlockSpec((rows, LANES), live),
                      pl.BlockSpec(memory_space=pl.ANY),
                      pl.BlockSpec((None, 1, 2 * D_EXPERT), lambda i, be, hf, nx, nu: (be[i], 0, 0)),
                      pl.BlockSpec(memory_space=pl.ANY),
                      pl.BlockSpec((None, 1, D_MODEL), lambda i, be, hf, nx, nu: (be[i], 0, 0))],
            out_specs=pl.BlockSpec((rows, LANES), lambda i, be, hf, nx, nu: (i, 0)),
            scratch_shapes=[pltpu.VMEM((D_MODEL, 2 * D_EXPERT), F32),
                            pltpu.VMEM((D_EXPERT, D_MODEL), F32),
                            pltpu.VMEM((D_MODEL, 2 * D_EXPERT), BF16),
                            pltpu.VMEM((D_EXPERT, D_MODEL), BF16),
                            pltpu.SemaphoreType.DMA((2,))]),
        compiler_params=_cparams(("arbitrary",)),
    )(block_e, block_half, next_e, n_used, xs2d, w_gu, b_gu, w_down, b_down)


def _combine_kernel(dcur_ref, dnext_ref, ys_hbm, g_ref, x1_ref, gt_ref, np_ref, o_ref, ybuf, row_sem, *, tm):
    i = pl.program_id(0)
    n_rows = TOP_K * tm
    slot = i % 2

    def slot_rows(s):
        return ybuf.at[pl.ds(pl.multiple_of(s * n_rows * LANE_CHUNKS, n_rows * LANE_CHUNKS), n_rows * LANE_CHUNKS), :]

    def issue(d_ref, s):
        dst = slot_rows(s)
        for n in range(n_rows):
            pltpu.make_async_copy(ys_hbm.at[d_ref[0, n]], dst.at[pl.ds(n * LANE_CHUNKS, LANE_CHUNKS), :],
                                  row_sem.at[s]).start(priority=n % 2)

    @pl.when(i == 0)
    def _():
        issue(dcur_ref, 0)

    @pl.when(i + 1 < pl.num_programs(0))
    def _():
        issue(dnext_ref, 1 - slot)

    pltpu.make_async_copy(slot_rows(slot), slot_rows(slot), row_sem.at[slot]).wait()
    g = g_ref[...]
    y = None
    for kk in range(TOP_K):
        rows = _from_row_tiles(ybuf, (slot * n_rows + kk * tm) * LANE_CHUNKS, tm)
        y = g[:, kk:kk + 1] * rows if y is None else y + g[:, kk:kk + 1] * rows
    o_ref[...] = x1_ref[...] + gt_ref[...] * _rms(y, np_ref[...])


def _combine(dest_tiles, ys_tiles, gates_t, x1, gt, npost, tm, tiles_per_seq, tile0):
    n_tiles = dest_tiles.shape[0]
    t = n_tiles * tm
    smem_tile = lambda f: pl.BlockSpec((None, 1, TOP_K * tm), f, memory_space=pltpu.SMEM)
    return pl.pallas_call(
        functools.partial(_combine_kernel, tm=tm),
        out_shape=jax.ShapeDtypeStruct((t, D_MODEL), F32),
        grid=(n_tiles,),
        in_specs=[smem_tile(lambda i: (i, 0, 0)),
                  smem_tile(lambda i: (jnp.minimum(i + 1, n_tiles - 1), 0, 0)),
                  pl.BlockSpec(memory_space=pl.ANY),
                  pl.BlockSpec((tm, TOP_K), lambda i: (i + tile0, 0)),
                  pl.BlockSpec((tm, D_MODEL), lambda i: (i + tile0, 0)),
                  _mod_spec(gt, tm, tiles_per_seq),
                  pl.BlockSpec((1, D_MODEL), lambda i: (0, 0))],
        out_specs=pl.BlockSpec((tm, D_MODEL), lambda i: (i, 0)),
        scratch_shapes=[pltpu.VMEM((2 * TOP_K * tm * LANE_CHUNKS, LANES), F32),
                        pltpu.SemaphoreType.DMA((2,))],
        compiler_params=_cparams(("arbitrary",)),
    )(dest_tiles, dest_tiles, ys_tiles, gates_t, x1, gt, npost)


def _expert_layout(counts, n_blocks):
    padded = (counts + MOE_BM - 1) // MOE_BM * MOE_BM
    pad_ends = jnp.cumsum(padded)
    pad_starts = pad_ends - padded
    block_start = jnp.arange(n_blocks, dtype=I32) * MOE_BM
    block_e = jnp.minimum(jnp.sum((pad_ends[None, :] <= block_start[:, None]).astype(I32), axis=1), N_EXPERTS - 1)
    n_used = (pad_ends[-1:] // MOE_BM).astype(I32)
    e_ids = jnp.arange(N_EXPERTS, dtype=I32)
    rows_end = jnp.sum(jnp.where(block_e[:, None] == e_ids[None, :], (pad_starts + counts)[None, :], 0), axis=1)
    block_half = (rows_end - block_start <= MOE_BM // 2).astype(I32)
    later = (counts > 0)[None, :] & (e_ids[None, :] > e_ids[:, None])
    next_e = jnp.min(jnp.where(later, e_ids[None, :], N_EXPERTS), axis=1)
    next_e = jnp.where(next_e < N_EXPERTS, next_e, -1).astype(I32)
    return pad_starts, pad_starts + counts, pad_ends, block_e, block_half, next_e, n_used


def _slots(idx, rank, pad_starts):
    onehot = idx[:, :, None] == jnp.arange(N_EXPERTS, dtype=I32)[None, None, :]
    return rank + jnp.sum(jnp.where(onehot, pad_starts[None, None, :], 0), axis=-1)


def _tile_major(a, tm):
    t = a.shape[1]
    return a.reshape(TOP_K, t // tm, tm).transpose(1, 0, 2).reshape(t // tm, 1, TOP_K * tm)


def _block_diag(wb):
    n, d, _ = wb.shape
    eye = jnp.eye(n, dtype=wb.dtype)
    return (wb[:, :, None, :] * eye[:, None, :, None]).reshape(n * d, n * d)


def _pick_tile(n, prefs):
    for p in prefs:
        if n % p == 0:
            return p
    raise ValueError(f"no tile for {n}")


def kernel(x_prompt, x_sample, state_rglru_conv, state_rglru_h, state_hgrn_S, c_prompt, c_sample, ada_w, ada_b, norm_pre_mix, norm_post_mix, norm_pre_ffn, norm_post_ffn, w_in, conv_w, conv_b, lru_wa, lru_ba, lru_wi, lru_bi, lru_lambda, hg_lb, hg_norm, w_out, router_w, router_b, w_gu, b_gu, w_down, b_down):
    assert ada_w.shape[0] == 1, "single-layer trunk"
    bp, lp, _ = x_prompt.shape
    bs, ls, _ = x_sample.shape
    tp, ts = bp * lp, bs * ls
    assert CONV_WIDTH - 1 <= ls <= SUBLANES

    w_in_bf = w_in[0].astype(BF16)
    w_out_bf = w_out[0].astype(BF16)
    wg = jnp.concatenate([_block_diag(lru_wa[0]), _block_diag(lru_wi[0])], axis=1).astype(BF16)
    bg = jnp.concatenate([lru_ba[0], lru_bi[0]])[None, :]
    lbs = jnp.cumsum(jax.nn.softmax(hg_lb.astype(F32), axis=0), axis=0)[0][None, :]
    mw = (conv_w[0], conv_b[0][None, :], wg, bg, lru_lambda[0][None, :], lbs, hg_norm[0][None, :])
    row = lambda p: p[0][None, :]

    n_c = bp + bs
    n_c_pad = -(-n_c // SUBLANES) * SUBLANES
    c_all = jnp.concatenate([c_prompt, c_sample, jnp.zeros((n_c_pad - n_c, D_MODEL), F32)], axis=0)
    mod = _ada(c_all, ada_w[0], ada_b[0][None, :])
    mod_p = [m[:, None, :] for m in jnp.split(mod[:bp], 6, axis=-1)]
    mod_s = [jnp.repeat(m, ls, axis=0) for m in jnp.split(mod[bp:n_c], 6, axis=-1)]

    xp = x_prompt.reshape(tp, D_MODEL)
    xs = x_sample.reshape(ts, D_MODEL)
    tm_s = _pick_tile(ts, (256, 128, 64, 32, 16, 8))

    tc = _pick_tile(lp // 2, (256, 128, 64))
    m_p, conv_p, h_p, s_p = _mixer_prompt(xp, mod_p[1], mod_p[0], row(norm_pre_mix), w_in_bf, bp, lp, mw, tc)
    proj_s = _inproj(xs, mod_s[1], mod_s[0], row(norm_pre_mix), w_in_bf, tm_s, 1)

    proj_s_pad = jnp.pad(proj_s.reshape(bs, ls, IN_COLS), ((0, 0), (0, SUBLANES - ls), (0, 0)))
    cst_hdr = jnp.pad(state_rglru_conv[0], ((0, 0), (CONV_HDR - (CONV_WIDTH - 1), 0), (0, 0)))
    m_s_pad, conv_s, h_s, s_s = _mixer_sample(proj_s_pad, cst_hdr, state_rglru_h[0][:, None, :],
                                              state_hgrn_S[0], mw, ls)
    m_s = m_s_pad[:, :ls, :].reshape(ts, D_MODEL)

    tmq = _pick_tile(math.gcd(lp, ts), (512, 256, 128, 64, 32, 16, 8))
    x1, hf, idx, gate, rank, counts = _post(
        m_p, m_s, xp, xs, (mod_p[2], mod_p[4], mod_p[3]), (mod_s[2], mod_s[4], mod_s[3]),
        row(norm_post_mix), row(norm_pre_ffn), w_out_bf, router_w[0].T, router_b[0][:, None], tmq, lp // tmq)

    n_blocks = -(-(TOP_K * (tp + ts)) // MOE_BM) + N_EXPERTS
    n_slots = n_blocks * MOE_BM
    pad_starts, fill_lo, pad_ends, block_e, block_half, next_e, n_used = _expert_layout(
        counts[:, 0].astype(I32), n_blocks)
    fill_hi = pad_ends.at[N_EXPERTS - 1].set(n_slots)
    dest = _slots(idx, rank, pad_starts)
    tmd = _pick_tile(math.gcd(lp, ts), (512, 256, 128, 64, 32, 16, 8))
    tmc = _pick_tile(math.gcd(lp, ts), (256, 128, 64, 32, 16, 8))
    xs_rows = _dispatch(fill_lo, fill_hi, _tile_major(dest, tmd), hf, n_slots, tmd)
    ys = _experts(block_e, block_half, next_e, n_used, xs_rows.reshape(n_slots * LANE_CHUNKS, LANES),
                  w_gu[0], b_gu[0][:, None, :], w_down[0], b_down[0][:, None, :])
    ys_tiles = ys.reshape(-1, LANE_CHUNKS, LANES)

    gates_t = gate.T
    ctiles = _tile_major(dest, tmc)
    y_p = _combine(ctiles[:tp // tmc], ys_tiles, gates_t, x1, mod_p[5], row(norm_post_ffn), tmc, lp // tmc, 0)
    y_s = _combine(ctiles[tp // tmc:], ys_tiles, gates_t, x1, mod_s[5], row(norm_post_ffn), tmc, 1, tp // tmc)

    return (y_p.reshape(bp, lp, D_MODEL), y_s.reshape(bs, ls, D_MODEL),
            conv_p[None], h_p.reshape(1, bp, LRU_WIDTH), s_p[None],
            conv_s[None], h_s.reshape(1, bs, LRU_WIDTH), s_s[None])
```

```python
import functools
import math

import jax
import jax.numpy as jnp
from jax import lax
from jax.experimental import pallas as pl
from jax.experimental.pallas import tpu as pltpu

F32 = jnp.float32
BF16 = jnp.bfloat16
I32 = jnp.int32

D_MODEL = 1024
LRU_WIDTH = 512
CONV_WIDTH = 4
LRU_C = 8.0
HG_WIDTH = 512
HG_HEAD_DIM = 128
HG_HEADS = 4
IN_COLS = 2 * LRU_WIDTH + 4 * HG_WIDTH
N_EXPERTS = 32
TOP_K = 4
D_EXPERT = 1024
SWIGLU_LIMIT = 7.0
SWIGLU_ALPHA = 1.702
RMS_EPS = 1e-6

C_XL, C_YL, C_Q, C_F, C_V, C_G = 0, 512, 1024, 1536, 2048, 2560

SUBLANES = 8
LANES = 128
LANE_CHUNKS = D_MODEL // LANES
CONV_HDR = SUBLANES
HG_CHUNK = 64
HG_MAX_EXPONENT = 60.0
MOE_BM = 512
ZERO_ROWS = 64
VMEM_LIMIT = 56 * 1024 * 1024

_NT = (((1,), (1,)), ((), ()))
_TN = (((0,), (0,)), ((), ()))


def _cparams(sem):
    return pltpu.CompilerParams(dimension_semantics=sem, vmem_limit_bytes=VMEM_LIMIT)


def _rms(x, g):
    return x * lax.rsqrt(jnp.mean(x * x, axis=-1, keepdims=True) + RMS_EPS) * g


def _gelu_tanh(x):
    c = math.sqrt(2.0 / math.pi)
    return 0.5 * x * (1.0 + jnp.tanh(c * (x + 0.044715 * (x * x * x))))


def _ada_kernel(c_ref, w_ref, b_ref, o_ref):
    c = c_ref[...]
    s = c * jax.nn.sigmoid(c)
    o_ref[...] = jnp.dot(s.astype(BF16), w_ref[...].astype(BF16), preferred_element_type=F32) + b_ref[...]


def _ada(c_all, ada_w, ada_b):
    n = c_all.shape[0]
    tn = 1024
    return pl.pallas_call(
        _ada_kernel,
        out_shape=jax.ShapeDtypeStruct((n, 6 * D_MODEL), F32),
        grid=(6 * D_MODEL // tn,),
        in_specs=[pl.BlockSpec((n, D_MODEL), lambda j: (0, 0)),
                  pl.BlockSpec((D_MODEL, tn), lambda j: (0, j)),
                  pl.BlockSpec((1, tn), lambda j: (0, j))],
        out_specs=pl.BlockSpec((n, tn), lambda j: (0, j)),
        compiler_params=_cparams(("arbitrary",)),
    )(c_all, ada_w, ada_b)


def _inproj_kernel(x_ref, sc_ref, sh_ref, g_ref, w_ref, o_ref):
    h = _rms(x_ref[...], g_ref[...]) * (1.0 + sc_ref[...]) + sh_ref[...]
    o_ref[...] = jnp.dot(h.astype(BF16), w_ref[...], preferred_element_type=F32)


def _mod_spec(mod, tm, tiles_per_seq):
    if mod.ndim == 3:
        return pl.BlockSpec((None, 1, D_MODEL), lambda i: (i // tiles_per_seq, 0, 0))
    return pl.BlockSpec((tm, D_MODEL), lambda i: (i, 0))


def _inproj(x, sc, sh, g, w_bf, tm, tiles_per_seq):
    t = x.shape[0]
    return pl.pallas_call(
        _inproj_kernel,
        out_shape=jax.ShapeDtypeStruct((t, IN_COLS), F32),
        grid=(t // tm,),
        in_specs=[pl.BlockSpec((tm, D_MODEL), lambda i: (i, 0)),
                  _mod_spec(sc, tm, tiles_per_seq), _mod_spec(sh, tm, tiles_per_seq),
                  pl.BlockSpec((1, D_MODEL), lambda i: (0, 0)),
                  pl.BlockSpec((D_MODEL, IN_COLS), lambda i: (0, 0))],
        out_specs=pl.BlockSpec((tm, IN_COLS), lambda i: (i, 0)),
        compiler_params=_cparams(("arbitrary",)),
    )(x, sc, sh, g, w_bf)


def _sigmoid(x):
    return 0.5 * jnp.tanh(0.5 * x) + 0.5


def _group_rows(x):
    rows, w = x.shape
    return x.reshape(rows // SUBLANES, SUBLANES, w)


def _scan_rows(a, u, h0):
    a3, u3 = _group_rows(a), _group_rows(u)
    r3 = lax.broadcasted_iota(I32, a3.shape, 1)
    s = 1
    while s < SUBLANES:
        keep = r3 >= s
        a_sh = jnp.where(keep, pltpu.roll(a3, s, 1), 1.0)
        u_sh = jnp.where(keep, pltpu.roll(u3, s, 1), 0.0)
        u3 = a3 * u_sh + u3
        a3 = a3 * a_sh
        s *= 2
    out, h = [], h0
    for i in range(a3.shape[0]):
        hg = u3[i] + a3[i] * h
        h = hg[SUBLANES - 1:SUBLANES, :]
        out.append(hg)
    return jnp.concatenate(out, axis=0), h


def _chunk_cumsum(x, chunk):
    x3 = _group_rows(x)
    r3 = lax.broadcasted_iota(I32, x3.shape, 1)
    s = 1
    while s < SUBLANES:
        x3 = x3 + jnp.where(r3 >= s, pltpu.roll(x3, s, 1), 0.0)
        s *= 2
    out, carry = [], None
    for i in range(x3.shape[0]):
        cur = x3[i] if i % (chunk // SUBLANES) == 0 else x3[i] + carry
        carry = cur[SUBLANES - 1:SUBLANES, :]
        out.append(cur)
    return jnp.concatenate(out, axis=0)


def _lru_gates(xc, wg_ref, bg_ref, lam_ref):
    gates = jnp.dot(xc.astype(BF16), wg_ref[...], preferred_element_type=F32) + bg_ref[...]
    r = _sigmoid(gates[:, :LRU_WIDTH])
    ig = _sigmoid(gates[:, LRU_WIDTH:])
    z = -lam_ref[...]
    softplus = jnp.maximum(z, 0.0) + jnp.log1p(jnp.exp(-jnp.abs(z)))
    log_a = -LRU_C * r * softplus
    a = jnp.exp(log_a)
    th = jnp.tanh(log_a)
    mult2 = -2.0 * th / (1.0 - th)
    return a, jnp.sqrt(mult2), ig


def _conv4(xbuf, xl, cw_ref, cb_ref, rows):
    cw = cw_ref[...]
    h = CONV_HDR
    return (cb_ref[...] + cw[0:1] * xbuf[h - 3:h - 3 + rows, :] + cw[1:2] * xbuf[h - 2:h - 2 + rows, :]
            + cw[2:3] * xbuf[h - 1:h - 1 + rows, :] + cw[3:4] * xl)


def _conv4_rolled(xl, prev, cw_ref, cb_ref):
    cw = cw_ref[...]
    row8 = lax.broadcasted_iota(I32, prev.shape, 0)
    acc = cb_ref[...] + cw[CONV_WIDTH - 1:CONV_WIDTH] * xl
    for s in range(1, CONV_WIDTH):
        sh = pltpu.roll(xl, s, 0)
        top = jnp.where(row8 < s, pltpu.roll(prev, s, 0), sh[0:SUBLANES, :])
        sh = jnp.concatenate([top, sh[SUBLANES:, :]], axis=0)
        acc = acc + cw[CONV_WIDTH - 1 - s:CONV_WIDTH - s] * sh
    return acc


def _head_norm_gate(o, gate, hgn):
    return _rms(o, hgn) * _sigmoid(gate)


def _hgrn_window(q, k, v, b, state, steps, state_is_transposed):
    row = lax.broadcasted_iota(I32, q.shape, 0)
    valid = row < steps
    b_end = b[steps - 1:steps, :]
    q_in = jnp.where(valid, q * jnp.exp(b), 0.0).astype(BF16)
    if state_is_transposed:
        o = lax.dot_general(q_in, state.astype(BF16), _NT, preferred_element_type=F32)
    else:
        o = jnp.dot(q_in, state.astype(BF16), preferred_element_type=F32)
    for d in range(steps):
        ok = valid & (row >= d)
        k_d = k if d == 0 else pltpu.roll(k, d, 0)
        v_d = v if d == 0 else pltpu.roll(v, d, 0)
        b_d = b if d == 0 else pltpu.roll(b, d, 0)
        decay = jnp.exp(jnp.where(ok, b - b_d, 0.0))
        wgt = jnp.sum(jnp.where(ok, q * k_d * decay, 0.0), axis=-1, keepdims=True)
        o = o + wgt * v_d
    k_out = jnp.where(valid, k * jnp.exp(jnp.where(valid, b_end - b, 0.0)), 0.0).astype(BF16)
    v_ok = jnp.where(valid, v, 0.0).astype(BF16)
    keep = jnp.exp(b_end)
    if state_is_transposed:
        s_new = state * keep + lax.dot_general(v_ok, k_out, _TN, preferred_element_type=F32)
    else:
        keep_col = jnp.broadcast_to(keep, q.shape).T[:, 0:1]
        s_new = state * keep_col + lax.dot_general(k_out, v_ok, _TN, preferred_element_type=F32)
    return o, s_new


def _tile_cumsum(x):
    row = lax.broadcasted_iota(I32, x.shape, 0)
    s = 1
    while s < SUBLANES:
        x = x + jnp.where(row >= s, pltpu.roll(x, s, 0), 0.0)
        s *= 2
    return x


def _mix_tile(proj_ref, cw_ref, cb_ref, wg_ref, bg_ref, lam_ref, lb_ref, hgn_ref, m_ref, xprev, hcar, st, st0, obuf,
              seq_start, tc, side_work):
    w = LRU_WIDTH
    side_work = iter(side_work)

    next(side_work)()
    xl = proj_ref[:, C_XL:C_XL + w]
    xc = _conv4_rolled(xl, xprev[...], cw_ref, cb_ref)
    xprev[...] = xl[tc - SUBLANES:tc, :]
    a, mult, ig = _lru_gates(xc, wg_ref, bg_ref, lam_ref)
    row = lax.broadcasted_iota(I32, (tc, w), 0)
    if seq_start is not None:
        mult = jnp.where((row == 0) & seq_start, 1.0, mult)
    h, h_last = _scan_rows(a, mult * ig * xc, hcar[0:1, :])
    hcar[0:1, :] = h_last
    m_ref[:, 0:w] = (h * _gelu_tanh(proj_ref[:, C_YL:C_YL + w])).astype(BF16)

    next(side_work)()
    lb = lb_ref[...]
    f = lb + (1.0 - lb) * _sigmoid(proj_ref[:, C_F:C_F + w])
    k = 1.0 - f
    b = _chunk_cumsum(jnp.log(f), HG_CHUNK)
    q = proj_ref[:, C_Q:C_Q + w]
    v = proj_ref[:, C_V:C_V + w]
    mid = HG_CHUNK // 2
    st0[...] = st[...]
    tri = (lax.broadcasted_iota(I32, (HG_CHUNK, HG_CHUNK), 0)
           >= lax.broadcasted_iota(I32, (HG_CHUNK, HG_CHUNK), 1))
    states = [st[hd] for hd in range(HG_HEADS)]
    for c in range(tc // HG_CHUNK):
        next(side_work)()
        rs = slice(c * HG_CHUNK, (c + 1) * HG_CHUNK)
        for hd in range(HG_HEADS):
            cs = slice(hd * HG_HEAD_DIM, (hd + 1) * HG_HEAD_DIM)
            s_t = states[hd]
            bc, qc, kc = b[rs, cs], q[rs, cs], k[rs, cs]
            vb = v[rs, cs].astype(BF16)
            b_mid = bc[mid - 1:mid, :]
            b_end = bc[HG_CHUNK - 1:HG_CHUNK, :]
            qp = (qc * jnp.exp(bc - b_mid)).astype(BF16)
            kp = (kc * jnp.exp(b_mid - bc)).astype(BF16)
            att = lax.dot_general(qp, kp, _NT, preferred_element_type=F32)
            att = jnp.where(tri, att, 0.0)
            q_in = (qc * jnp.exp(bc)).astype(BF16)
            o = (jnp.dot(att.astype(BF16), vb, preferred_element_type=F32)
                 + lax.dot_general(q_in, s_t.astype(BF16), _NT, preferred_element_type=F32))
            k_out = (kc * jnp.exp(b_end - bc)).astype(BF16)
            states[hd] = s_t * jnp.exp(b_end) + lax.dot_general(vb, k_out, _TN, preferred_element_type=F32)
            obuf[rs, cs] = o
    for hd in range(HG_HEADS):
        st[hd] = states[hd]

    half_decay = None
    for c in range(tc // HG_CHUNK):
        b_mid = b[c * HG_CHUNK + mid - 1:c * HG_CHUNK + mid, :]
        b_end = b[(c + 1) * HG_CHUNK - 1:(c + 1) * HG_CHUNK, :]
        worst = jnp.maximum(-b_mid, b_mid - b_end)
        half_decay = worst if half_decay is None else jnp.maximum(half_decay, worst)

    @pl.when(jnp.logical_not(jnp.max(half_decay) < HG_MAX_EXPONENT))
    def _():
        st[...] = st0[...]

        def window(gi, carry):
            rows = pl.ds(pl.multiple_of(gi * SUBLANES, SUBLANES), SUBLANES)
            f8 = lb + (1.0 - lb) * _sigmoid(proj_ref[rows, C_F:C_F + w])
            b8 = _tile_cumsum(jnp.log(f8))
            k8 = 1.0 - f8
            q8 = proj_ref[rows, C_Q:C_Q + w]
            v8 = proj_ref[rows, C_V:C_V + w]
            for hd in range(HG_HEADS):
                cs = slice(hd * HG_HEAD_DIM, (hd + 1) * HG_HEAD_DIM)
                o, s_new = _hgrn_window(q8[:, cs], k8[:, cs], v8[:, cs], b8[:, cs], st[hd], SUBLANES, True)
                st[hd] = s_new
                obuf[rows, cs] = o
            return carry
        lax.fori_loop(0, tc // SUBLANES, window, 0)

    hgn = hgn_ref[...]
    for hd in range(HG_HEADS):
        cs = slice(hd * HG_HEAD_DIM, (hd + 1) * HG_HEAD_DIM)
        gate = proj_ref[:, C_G + hd * HG_HEAD_DIM:C_G + (hd + 1) * HG_HEAD_DIM]
        m_ref[:, w + hd * HG_HEAD_DIM:w + (hd + 1) * HG_HEAD_DIM] = (
            _head_norm_gate(obuf[:, cs], gate, hgn).astype(BF16))


def _mixer_prompt_kernel(x_ref, xn_ref, sc_ref, sh_ref, scn_ref, shn_ref, g_ref, win_ref, cw_ref, cb_ref, wg_ref,
                         bg_ref, lam_ref, lb_ref, hgn_ref, m_ref, conv_ref, h_ref, s_ref,
                         proj_a, proj_b, hn_buf, xprev, hcar, st, st0, obuf, *, tc):
    j = pl.program_id(1)
    n_phases = 2 + tc // HG_CHUNK
    cols = IN_COLS // n_phases

    def project(x, sc, sh, out_ref):
        def chunk(c):
            if c == 0:
                hn_buf[...] = (_rms(x(), g_ref[...]) * (1.0 + sc[...]) + sh[...]).astype(BF16)
            out_ref[:, c * cols:(c + 1) * cols] = jnp.dot(hn_buf[...], win_ref[:, c * cols:(c + 1) * cols],
                                                          preferred_element_type=F32)
        return [functools.partial(chunk, c) for c in range(n_phases)]

    def mix(proj_ref, first_row, seq_start, side_work):
        _mix_tile(proj_ref, cw_ref, cb_ref, wg_ref, bg_ref, lam_ref, lb_ref, hgn_ref,
                  m_ref.at[pl.ds(first_row, tc), :], xprev, hcar, st, st0, obuf, seq_start, tc, side_work)

    @pl.when((pl.program_id(0) == 0) & (j == 0))
    def _():
        for thunk in project(lambda: x_ref[0:tc, :], sc_ref, sh_ref, proj_a):
            thunk()

    @pl.when(j == 0)
    def _():
        xprev[...] = jnp.zeros_like(xprev)
        hcar[...] = jnp.zeros_like(hcar)
        st[...] = jnp.zeros_like(st)

    mix(proj_a, 0, j == 0, project(lambda: x_ref[tc:2 * tc, :], sc_ref, sh_ref, proj_b))
    mix(proj_b, tc, None, project(lambda: xn_ref[...], scn_ref, shn_ref, proj_a))

    @pl.when(j == pl.num_programs(1) - 1)
    def _():
        conv_ref[...] = xprev[SUBLANES - (CONV_WIDTH - 1):SUBLANES, :]
        h_ref[...] = hcar[0:1, :]
        for hd in range(HG_HEADS):
            s_ref[hd] = st[hd].T


def _mixer_prompt(x, sc, sh, g, w_in_bf, nb, seq, mw, tc):
    cw, cb, wg, bg, lam, lb, hgn = mw
    nj = seq // (2 * tc)
    last_tile = nb * seq // tc - 1
    next_tile = lambda b, j: jnp.minimum(2 * (b * nj + j) + 2, last_tile)
    const = lambda shape: pl.BlockSpec(shape, lambda b, j: tuple(0 for _ in shape))
    per_seq = lambda: pl.BlockSpec((None, 1, D_MODEL), lambda b, j: (b, 0, 0))
    next_seq = lambda: pl.BlockSpec((None, 1, D_MODEL), lambda b, j: (next_tile(b, j) // (2 * nj), 0, 0))
    return pl.pallas_call(
        functools.partial(_mixer_prompt_kernel, tc=tc),
        out_shape=(jax.ShapeDtypeStruct((nb * seq, D_MODEL), BF16),
                   jax.ShapeDtypeStruct((nb, CONV_WIDTH - 1, LRU_WIDTH), F32),
                   jax.ShapeDtypeStruct((nb, 1, LRU_WIDTH), F32),
                   jax.ShapeDtypeStruct((nb, HG_HEADS, HG_HEAD_DIM, HG_HEAD_DIM), F32)),
        grid=(nb, nj),
        in_specs=[pl.BlockSpec((2 * tc, D_MODEL), lambda b, j: (b * nj + j, 0)),
                  pl.BlockSpec((tc, D_MODEL), lambda b, j: (next_tile(b, j), 0)),
                  per_seq(), per_seq(), next_seq(), next_seq(), const((1, D_MODEL)), const((D_MODEL, IN_COLS)),
                  const((CONV_WIDTH, LRU_WIDTH)), const((1, LRU_WIDTH)),
                  const((LRU_WIDTH, 2 * LRU_WIDTH)), const((1, 2 * LRU_WIDTH)),
                  const((1, LRU_WIDTH)), const((1, HG_WIDTH)), const((1, HG_HEAD_DIM))],
        out_specs=(pl.BlockSpec((2 * tc, D_MODEL), lambda b, j: (b * nj + j, 0)),
                   pl.BlockSpec((None, CONV_WIDTH - 1, LRU_WIDTH), lambda b, j: (b, 0, 0)),
                   pl.BlockSpec((None, 1, LRU_WIDTH), lambda b, j: (b, 0, 0)),
                   pl.BlockSpec((None, HG_HEADS, HG_HEAD_DIM, HG_HEAD_DIM), lambda b, j: (b, 0, 0, 0))),
        scratch_shapes=[pltpu.VMEM((tc, IN_COLS), F32),
                        pltpu.VMEM((tc, IN_COLS), F32),
                        pltpu.VMEM((tc, D_MODEL), BF16),
                        pltpu.VMEM((SUBLANES, LRU_WIDTH), F32),
                        pltpu.VMEM((SUBLANES, LRU_WIDTH), F32),
                        pltpu.VMEM((HG_HEADS, HG_HEAD_DIM, HG_HEAD_DIM), F32),
                        pltpu.VMEM((HG_HEADS, HG_HEAD_DIM, HG_HEAD_DIM), F32),
                        pltpu.VMEM((tc, HG_WIDTH), F32)],
        compiler_params=_cparams(("arbitrary", "arbitrary")),
    )(x, x, sc, sh, sc, sh, g, w_in_bf, cw, cb, wg, bg, lam, lb, hgn)


def _mixer_sample_kernel(proj_ref, cst_ref, h0_ref, s0_ref, cw_ref, cb_ref, wg_ref, bg_ref, lam_ref, lb_ref,
                         hgn_ref, m_ref, conv_ref, h_ref, s_ref, xbuf, *, steps):
    for bi in range(proj_ref.shape[0]):
        _mixer_sample_one(proj_ref.at[bi], cst_ref.at[bi], h0_ref.at[bi], s0_ref.at[bi], cw_ref, cb_ref, wg_ref,
                          bg_ref, lam_ref, lb_ref, hgn_ref, m_ref.at[bi], conv_ref.at[bi], h_ref.at[bi],
                          s_ref.at[bi], xbuf.at[bi], steps)


def _mixer_sample_one(proj_ref, cst_ref, h0_ref, s0_ref, cw_ref, cb_ref, wg_ref, bg_ref, lam_ref, lb_ref,
                      hgn_ref, m_ref, conv_ref, h_ref, s_ref, xbuf, steps):
    w = LRU_WIDTH
    rows = SUBLANES
    row = lax.broadcasted_iota(I32, (rows, w), 0)

    xl = proj_ref[:, C_XL:C_XL + w]
    xbuf[0:CONV_HDR, :] = cst_ref[...]
    xbuf[CONV_HDR:CONV_HDR + rows, :] = xl
    xc = _conv4(xbuf, xl, cw_ref, cb_ref, rows)
    conv_ref[...] = xbuf[CONV_HDR + steps - 3:CONV_HDR + steps, :]
    a, mult, ig = _lru_gates(xc, wg_ref, bg_ref, lam_ref)
    u = mult * ig * xc
    h = h0_ref[...]
    hs = jnp.zeros((rows, w), F32)
    for t in range(steps):
        h = a[t:t + 1, :] * h + u[t:t + 1, :]
        hs = jnp.where(row == t, h, hs)
    h_ref[...] = h
    m_ref[:, 0:w] = (hs * _gelu_tanh(proj_ref[:, C_YL:C_YL + w])).astype(BF16)

    lb = lb_ref[...]
    f = lb + (1.0 - lb) * _sigmoid(proj_ref[:, C_F:C_F + w])
    k = 1.0 - f
    b = _tile_cumsum(jnp.log(f))
    q = proj_ref[:, C_Q:C_Q + w]
    v = proj_ref[:, C_V:C_V + w]
    hgn = hgn_ref[...]
    for hd in range(HG_HEADS):
        cs = slice(hd * HG_HEAD_DIM, (hd + 1) * HG_HEAD_DIM)
        o, s_new = _hgrn_window(q[:, cs], k[:, cs], v[:, cs], b[:, cs], s0_ref[hd], steps, False)
        s_ref[hd] = s_new
        gate = proj_ref[:, C_G + hd * HG_HEAD_DIM:C_G + (hd + 1) * HG_HEAD_DIM]
        m_ref[:, w + hd * HG_HEAD_DIM:w + (hd + 1) * HG_HEAD_DIM] = _head_norm_gate(o, gate, hgn).astype(BF16)


def _mixer_sample(proj_pad, cst_hdr, h0, s0, mw, steps):
    cw, cb, wg, bg, lam, lb, hgn = mw
    nb = proj_pad.shape[0]
    per_step = _pick_tile(nb, (8, 4, 2, 1))
    const = lambda shape: pl.BlockSpec(shape, lambda b: tuple(0 for _ in shape))
    per_b = lambda shape: pl.BlockSpec((per_step,) + shape, lambda b: (b,) + tuple(0 for _ in shape))
    return pl.pallas_call(
        functools.partial(_mixer_sample_kernel, steps=steps),
        out_shape=(jax.ShapeDtypeStruct((nb, SUBLANES, D_MODEL), BF16),
                   jax.ShapeDtypeStruct((nb, CONV_WIDTH - 1, LRU_WIDTH), F32),
                   jax.ShapeDtypeStruct((nb, 1, LRU_WIDTH), F32),
                   jax.ShapeDtypeStruct((nb, HG_HEADS, HG_HEAD_DIM, HG_HEAD_DIM), F32)),
        grid=(nb // per_step,),
        in_specs=[per_b((SUBLANES, IN_COLS)), per_b((CONV_HDR, LRU_WIDTH)), per_b((1, LRU_WIDTH)),
                  per_b((HG_HEADS, HG_HEAD_DIM, HG_HEAD_DIM)),
                  const((CONV_WIDTH, LRU_WIDTH)), const((1, LRU_WIDTH)),
                  const((LRU_WIDTH, 2 * LRU_WIDTH)), const((1, 2 * LRU_WIDTH)),
                  const((1, LRU_WIDTH)), const((1, HG_WIDTH)), const((1, HG_HEAD_DIM))],
        out_specs=(per_b((SUBLANES, D_MODEL)), per_b((CONV_WIDTH - 1, LRU_WIDTH)), per_b((1, LRU_WIDTH)),
                   per_b((HG_HEADS, HG_HEAD_DIM, HG_HEAD_DIM))),
        scratch_shapes=[pltpu.VMEM((per_step, CONV_HDR + SUBLANES, LRU_WIDTH), F32)],
        compiler_params=_cparams(("arbitrary",)),
    )(proj_pad, cst_hdr, h0, s0, cw, cb, wg, bg, lam, lb, hgn)


def _to_row_tiles(ref, val, rows):
    for c in range(LANE_CHUNKS):
        ref[pl.ds(c, rows, stride=LANE_CHUNKS), :] = val[:, c * LANES:(c + 1) * LANES]


def _from_row_tiles(ref, base, rows):
    return jnp.concatenate(
        [ref[pl.ds(base + c, rows, stride=LANE_CHUNKS), :] for c in range(LANE_CHUNKS)], axis=-1)


def _post_kernel(mp_ref, ms_ref, xp_ref, xs_ref, gtmp_ref, scfp_ref, shfp_ref, gtms_ref, scfs_ref, shfs_ref,
                 npost_ref, npre_ref, wout_ref, rwt_ref, rb_ref,
                 x1_ref, hf_ref, idx_ref, gate_ref, rank_ref, cnt_out_ref, cnt, *, n_p):
    i = pl.program_id(0)
    tm = xp_ref.shape[0]

    @pl.when(i == 0)
    def _():
        cnt[...] = jnp.zeros_like(cnt)

    def run(m_ref, x_ref, gtm_ref, scf_ref, shf_ref):
        mix = jnp.dot(m_ref[...], wout_ref[...], preferred_element_type=F32)
        x1 = x_ref[...] + gtm_ref[...] * _rms(mix, npost_ref[...])
        x1_ref[...] = x1
        hf = _rms(x1, npre_ref[...]) * (1.0 + scf_ref[...]) + shf_ref[...]
        _to_row_tiles(hf_ref, hf, tm)
        logits = lax.dot_general(rwt_ref[...], hf, _NT, precision=lax.Precision.HIGHEST,
                                 preferred_element_type=F32) + rb_ref[...]
        e_iota = lax.broadcasted_iota(I32, logits.shape, 0)
        vals, idxs = [], []
        for _ in range(TOP_K):
            mx = jnp.max(logits, axis=0, keepdims=True)
            ix = jnp.min(jnp.where(logits == mx, e_iota, N_EXPERTS), axis=0, keepdims=True)
            vals.append(mx)
            idxs.append(ix)
            logits = jnp.where(e_iota == ix, -jnp.inf, logits)
        exps = [jnp.exp(vv - vals[0]) for vv in vals]
        den = exps[0] + exps[1] + exps[2] + exps[3]
        earlier = (lax.broadcasted_iota(I32, (tm, tm), 0) < lax.broadcasted_iota(I32, (tm, tm), 1))
        earlier = jnp.where(earlier, 1.0, 0.0).astype(BF16)
        base = cnt[...]
        for kk in range(TOP_K):
            idx_ref[kk:kk + 1, :] = idxs[kk]
            gate_ref[kk:kk + 1, :] = exps[kk] / den
            hit = e_iota == idxs[kk]
            onehot = jnp.where(hit, 1.0, 0.0)
            before = jnp.dot(onehot.astype(BF16), earlier, preferred_element_type=F32)
            rank = jnp.sum(jnp.where(hit, before + base, 0.0), axis=0, keepdims=True)
            rank_ref[kk:kk + 1, :] = rank.astype(I32)
            base = base + jnp.sum(onehot, axis=1, keepdims=True)
        cnt[...] = base

    @pl.when(i < n_p)
    def _():
        run(mp_ref, xp_ref, gtmp_ref, scfp_ref, shfp_ref)

    @pl.when(i >= n_p)
    def _():
        run(ms_ref, xs_ref, gtms_ref, scfs_ref, shfs_ref)

    @pl.when(i == pl.num_programs(0) - 1)
    def _():
        cnt_out_ref[...] = cnt[...]


def _post(m_p, m_s, x_p, x_s, mod_p, mod_s, npost, npre, wout_bf, rwt, rb, tm, tiles_per_seq):
    n_p, n_s = x_p.shape[0] // tm, x_s.shape[0] // tm
    t = (n_p + n_s) * tm
    first = lambda i: jnp.minimum(i, n_p - 1)
    second = lambda i: jnp.maximum(i - n_p, 0)
    row = lambda: pl.BlockSpec((1, D_MODEL), lambda i: (0, 0))
    tok_p = lambda: pl.BlockSpec((tm, D_MODEL), lambda i: (first(i), 0))
    tok_s = lambda: pl.BlockSpec((tm, D_MODEL), lambda i: (second(i), 0))
    seq_p = lambda: pl.BlockSpec((None, 1, D_MODEL), lambda i: (first(i) // tiles_per_seq, 0, 0))
    per_k = lambda: pl.BlockSpec((TOP_K, tm), lambda i: (0, i))
    return pl.pallas_call(
        functools.partial(_post_kernel, n_p=n_p),
        out_shape=(jax.ShapeDtypeStruct((t, D_MODEL), F32),
                   jax.ShapeDtypeStruct((t * LANE_CHUNKS, LANES), F32),
                   jax.ShapeDtypeStruct((TOP_K, t), I32), jax.ShapeDtypeStruct((TOP_K, t), F32),
                   jax.ShapeDtypeStruct((TOP_K, t), I32), jax.ShapeDtypeStruct((N_EXPERTS, 1), F32)),
        grid=(n_p + n_s,),
        in_specs=[tok_p(), tok_s(), tok_p(), tok_s(), seq_p(), seq_p(), seq_p(), tok_s(), tok_s(), tok_s(),
                  row(), row(),
                  pl.BlockSpec((D_MODEL, D_MODEL), lambda i: (0, 0)),
                  pl.BlockSpec((N_EXPERTS, D_MODEL), lambda i: (0, 0)),
                  pl.BlockSpec((N_EXPERTS, 1), lambda i: (0, 0))],
        out_specs=(pl.BlockSpec((tm, D_MODEL), lambda i: (i, 0)),
                   pl.BlockSpec((tm * LANE_CHUNKS, LANES), lambda i: (i, 0)),
                   per_k(), per_k(), per_k(), pl.BlockSpec((N_EXPERTS, 1), lambda i: (0, 0))),
        scratch_shapes=[pltpu.VMEM((N_EXPERTS, 1), F32)],
        compiler_params=_cparams(("arbitrary",)),
    )(m_p, m_s, x_p, x_s, *mod_p, *mod_s, npost, npre, wout_bf, rwt, rb)


def _dispatch_kernel(fill_lo_ref, fill_hi_ref, dest_ref, hf_ref, xs_hbm, zeros, row_sem, pad_sem, *, tm):
    i = pl.program_id(0)
    n_rows = TOP_K * tm

    @pl.when(i == 0)
    def _():
        zeros[...] = jnp.zeros_like(zeros)

        def per_expert(e, totals):
            lo, hi = fill_lo_ref[e], fill_hi_ref[e]
            mid = jnp.minimum((lo + ZERO_ROWS - 1) // ZERO_ROWS * ZERO_ROWS, hi)

            def per_row(s, carry):
                pltpu.make_async_copy(zeros.at[0], xs_hbm.at[s], pad_sem).start()
                return carry
            lax.fori_loop(lo, mid, per_row, 0)

            def per_chunk(c, carry):
                pltpu.make_async_copy(zeros, xs_hbm.at[pl.ds(pl.multiple_of(c * ZERO_ROWS, ZERO_ROWS), ZERO_ROWS)],
                                      pad_sem).start()
                return carry
            lax.fori_loop(mid // ZERO_ROWS, hi // ZERO_ROWS, per_chunk, 0)
            return totals[0] + (mid - lo), totals[1] + (hi - mid) // ZERO_ROWS
        n_rows_filled, n_chunks_filled = lax.fori_loop(0, N_EXPERTS, per_expert, (0, 0))

        def drain_row(s, carry):
            pltpu.make_async_copy(zeros.at[0], xs_hbm.at[0], pad_sem).wait()
            return carry
        lax.fori_loop(0, n_rows_filled, drain_row, 0)

        def drain_chunk(c, carry):
            pltpu.make_async_copy(zeros, xs_hbm.at[pl.ds(0, ZERO_ROWS)], pad_sem).wait()
            return carry
        lax.fori_loop(0, n_chunks_filled, drain_chunk, 0)

    for n in range(n_rows):
        pltpu.make_async_copy(hf_ref.at[pl.ds((n % tm) * LANE_CHUNKS, LANE_CHUNKS), :],
                              xs_hbm.at[dest_ref[0, n]], row_sem).start(priority=n % 2)
    pltpu.make_async_copy(xs_hbm.at[pl.ds(0, n_rows)], xs_hbm.at[pl.ds(0, n_rows)], row_sem).wait()


def _dispatch(fill_lo, fill_hi, dest_tiles, hf_rows, n_slots, tm):
    return pl.pallas_call(
        functools.partial(_dispatch_kernel, tm=tm),
        out_shape=jax.ShapeDtypeStruct((n_slots, LANE_CHUNKS, LANES), F32),
        grid_spec=pltpu.PrefetchScalarGridSpec(
            num_scalar_prefetch=2, grid=(dest_tiles.shape[0],),
            in_specs=[pl.BlockSpec((None, 1, TOP_K * tm), lambda i, lo, hi: (i, 0, 0), memory_space=pltpu.SMEM),
                      pl.BlockSpec((tm * LANE_CHUNKS, LANES), lambda i, lo, hi: (i, 0))],
            out_specs=pl.BlockSpec(memory_space=pl.ANY),
            scratch_shapes=[pltpu.VMEM((ZERO_ROWS, LANE_CHUNKS, LANES), F32),
                            pltpu.SemaphoreType.DMA, pltpu.SemaphoreType.DMA]),
        compiler_params=_cparams(("arbitrary",)),
    )(fill_lo, fill_hi, dest_tiles, hf_rows)


def _expert_kernel(be_ref, half_ref, nxt_ref, nu_ref, xs_ref, wgu_hbm, bgu_ref, wd_hbm, bd_ref, ys_ref,
                   wgu_f32, wd_f32, wgu_bf, wd_bf, w_sem):
    i = pl.program_id(0)
    n_used = nu_ref[0]

    def weight_copies(e):
        return (pltpu.make_async_copy(wgu_hbm.at[e], wgu_f32, w_sem.at[0]),
                pltpu.make_async_copy(wd_hbm.at[e], wd_f32, w_sem.at[1]))

    @pl.when(i == 0)
    def _():
        for cp in weight_copies(be_ref[0]):
            cp.start()

    @pl.when(i < n_used)
    def _():
        prev = be_ref[jnp.maximum(i - 1, 0)]

        @pl.when((i == 0) | (be_ref[i] != prev))
        def _():
            for cp in weight_copies(be_ref[i]):
                cp.wait()
            wgu_bf[...] = wgu_f32[...].astype(BF16)
            wd_bf[...] = wd_f32[...].astype(BF16)
            nxt = nxt_ref[be_ref[i]]

            @pl.when(nxt >= 0)
            def _():
                for cp in weight_copies(nxt):
                    cp.start()

        def swiglu_rows(rows):
            x = _from_row_tiles(xs_ref, 0, rows).astype(BF16)
            hu = jnp.dot(x, wgu_bf[...], preferred_element_type=F32) + bgu_ref[...]
            glu = jnp.minimum(hu[:, :D_EXPERT], SWIGLU_LIMIT)
            lin = jnp.clip(hu[:, D_EXPERT:], -SWIGLU_LIMIT, SWIGLU_LIMIT)
            act = glu * _sigmoid(SWIGLU_ALPHA * glu) * (lin + 1.0)
            y = jnp.dot(act.astype(BF16), wd_bf[...], preferred_element_type=F32) + bd_ref[...]
            _to_row_tiles(ys_ref, y, rows)

        @pl.when(half_ref[i] == 0)
        def _():
            swiglu_rows(MOE_BM)

        @pl.when(half_ref[i] != 0)
        def _():
            swiglu_rows(MOE_BM // 2)
            ys_ref[MOE_BM // 2 * LANE_CHUNKS:, :] = jnp.zeros((MOE_BM // 2 * LANE_CHUNKS, LANES), F32)

    @pl.when(i >= n_used)
    def _():
        ys_ref[...] = jnp.zeros_like(ys_ref)


def _experts(block_e, block_half, next_e, n_used, xs2d, w_gu, b_gu, w_down, b_down):
    n_blocks = block_e.shape[0]
    rows = MOE_BM * LANE_CHUNKS
    live = lambda i, be, hf, nx, nu: (jnp.minimum(i, nu[0] - 1), 0)
    return pl.pallas_call(
        _expert_kernel,
        out_shape=jax.ShapeDtypeStruct(xs2d.shape, F32),
        grid_spec=pltpu.PrefetchScalarGridSpec(
            num_scalar_prefetch=4,
            grid=(n_blocks,),
            in_specs=[pl.BlockSpec((rows, LANES), live),
                      pl.BlockSpec(memory_space=pl.ANY),
                      pl.BlockSpec((None, 1, 2 * D_EXPERT), lambda i, be, hf, nx, nu: (be[i], 0, 0)),
                      pl.BlockSpec(memory_space=pl.ANY),
                      pl.BlockSpec((None, 1, D_MODEL), lambda i, be, hf, nx, nu: (be[i], 0, 0))],
            out_specs=pl.BlockSpec((rows, LANES), lambda i, be, hf, nx, nu: (i, 0)),
            scratch_shapes=[pltpu.VMEM((D_MODEL, 2 * D_EXPERT), F32),
                            pltpu.VMEM((D_EXPERT, D_MODEL), F32),
                            pltpu.VMEM((D_MODEL, 2 * D_EXPERT), BF16),
                            pltpu.VMEM((D_EXPERT, D_MODEL), BF16),
                            pltpu.SemaphoreType.DMA((2,))]),
        compiler_params=_cparams(("arbitrary",)),
    )(block_e, block_half, next_e, n_used, xs2d, w_gu, b_gu, w_down, b_down)


def _combine_kernel(dcur_ref, dnext_ref, ys_hbm, g_ref, x1_ref, gt_ref, np_ref, o_ref, ybuf, row_sem, *, tm):
    i = pl.program_id(0)
    n_rows = TOP_K * tm
    slot = i % 2

    def slot_rows(s):
        return ybuf.at[pl.ds(pl.multiple_of(s * n_rows * LANE_CHUNKS, n_rows * LANE_CHUNKS), n_rows * LANE_CHUNKS), :]

    def issue(d_ref, s):
        dst = slot_rows(s)
        for n in range(n_rows):
            pltpu.make_async_copy(ys_hbm.at[d_ref[0, n]], dst.at[pl.ds(n * LANE_CHUNKS, LANE_CHUNKS), :],
                                  row_sem.at[s]).start(priority=n % 2)

    @pl.when(i == 0)
    def _():
        issue(dcur_ref, 0)

    @pl.when(i + 1 < pl.num_programs(0))
    def _():
        issue(dnext_ref, 1 - slot)

    pltpu.make_async_copy(slot_rows(slot), slot_rows(slot), row_sem.at[slot]).wait()
    g = g_ref[...]
    y = None
    for kk in range(TOP_K):
        rows = _from_row_tiles(ybuf, (slot * n_rows + kk * tm) * LANE_CHUNKS, tm)
        y = g[:, kk:kk + 1] * rows if y is None else y + g[:, kk:kk + 1] * rows
    o_ref[...] = x1_ref[...] + gt_ref[...] * _rms(y, np_ref[...])


def _combine(dest_tiles, ys_tiles, gates_t, x1, gt, npost, tm, tiles_per_seq, tile0):
    n_tiles = dest_tiles.shape[0]
    t = n_tiles * tm
    smem_tile = lambda f: pl.BlockSpec((None, 1, TOP_K * tm), f, memory_space=pltpu.SMEM)
    return pl.pallas_call(
        functools.partial(_combine_kernel, tm=tm),
        out_shape=jax.ShapeDtypeStruct((t, D_MODEL), F32),
        grid=(n_tiles,),
        in_specs=[smem_tile(lambda i: (i, 0, 0)),
                  smem_tile(lambda i: (jnp.minimum(i + 1, n_tiles - 1), 0, 0)),
                  pl.BlockSpec(memory_space=pl.ANY),
                  pl.BlockSpec((tm, TOP_K), lambda i: (i + tile0, 0)),
                  pl.BlockSpec((tm, D_MODEL), lambda i: (i + tile0, 0)),
                  _mod_spec(gt, tm, tiles_per_seq),
                  pl.BlockSpec((1, D_MODEL), lambda i: (0, 0))],
        out_specs=pl.BlockSpec((tm, D_MODEL), lambda i: (i, 0)),
        scratch_shapes=[pltpu.VMEM((2 * TOP_K * tm * LANE_CHUNKS, LANES), F32),
                        pltpu.SemaphoreType.DMA((2,))],
        compiler_params=_cparams(("arbitrary",)),
    )(dest_tiles, dest_tiles, ys_tiles, gates_t, x1, gt, npost)


def _expert_layout(counts, n_blocks):
    padded = (counts + MOE_BM - 1) // MOE_BM * MOE_BM
    pad_ends = jnp.cumsum(padded)
    pad_starts = pad_ends - padded
    block_start = jnp.arange(n_blocks, dtype=I32) * MOE_BM
    block_e = jnp.minimum(jnp.sum((pad_ends[None, :] <= block_start[:, None]).astype(I32), axis=1), N_EXPERTS - 1)
    n_used = (pad_ends[-1:] // MOE_BM).astype(I32)
    e_ids = jnp.arange(N_EXPERTS, dtype=I32)
    rows_end = jnp.sum(jnp.where(block_e[:, None] == e_ids[None, :], (pad_starts + counts)[None, :], 0), axis=1)
    block_half = (rows_end - block_start <= MOE_BM // 2).astype(I32)
    later = (counts > 0)[None, :] & (e_ids[None, :] > e_ids[:, None])
    next_e = jnp.min(jnp.where(later, e_ids[None, :], N_EXPERTS), axis=1)
    next_e = jnp.where(next_e < N_EXPERTS, next_e, -1).astype(I32)
    return pad_starts, pad_starts + counts, pad_ends, block_e, block_half, next_e, n_used


def _slots(idx, rank, pad_starts):
    onehot = idx[:, :, None] == jnp.arange(N_EXPERTS, dtype=I32)[None, None, :]
    return rank + jnp.sum(jnp.where(onehot, pad_starts[None, None, :], 0), axis=-1)


def _tile_major(a, tm):
    t = a.shape[1]
    return a.reshape(TOP_K, t // tm, tm).transpose(1, 0, 2).reshape(t // tm, 1, TOP_K * tm)


def _block_diag(wb):
    n, d, _ = wb.shape
    eye = jnp.eye(n, dtype=wb.dtype)
    return (wb[:, :, None, :] * eye[:, None, :, None]).reshape(n * d, n * d)


def _pick_tile(n, prefs):
    for p in prefs:
        if n % p == 0:
            return p
    raise ValueError(f"no tile for {n}")


def kernel(x_prompt, x_sample, state_rglru_conv, state_rglru_h, state_hgrn_S, c_prompt, c_sample, ada_w, ada_b, norm_pre_mix, norm_post_mix, norm_pre_ffn, norm_post_ffn, w_in, conv_w, conv_b, lru_wa, lru_ba, lru_wi, lru_bi, lru_lambda, hg_lb, hg_norm, w_out, router_w, router_b, w_gu, b_gu, w_down, b_down):
    assert ada_w.shape[0] == 1, "single-layer trunk"
    bp, lp, _ = x_prompt.shape
    bs, ls, _ = x_sample.shape
    tp, ts = bp * lp, bs * ls
    assert CONV_WIDTH - 1 <= ls <= SUBLANES

    w_in_bf = w_in[0].astype(BF16)
    w_out_bf = w_out[0].astype(BF16)
    wg = jnp.concatenate([_block_diag(lru_wa[0]), _block_diag(lru_wi[0])], axis=1).astype(BF16)
    bg = jnp.concatenate([lru_ba[0], lru_bi[0]])[None, :]
    lbs = jnp.cumsum(jax.nn.softmax(hg_lb.astype(F32), axis=0), axis=0)[0][None, :]
    mw = (conv_w[0], conv_b[0][None, :], wg, bg, lru_lambda[0][None, :], lbs, hg_norm[0][None, :])
    row = lambda p: p[0][None, :]

    n_c = bp + bs
    n_c_pad = -(-n_c // SUBLANES) * SUBLANES
    c_all = jnp.concatenate([c_prompt, c_sample, jnp.zeros((n_c_pad - n_c, D_MODEL), F32)], axis=0)
    mod = _ada(c_all, ada_w[0], ada_b[0][None, :])
    mod_p = [m[:, None, :] for m in jnp.split(mod[:bp], 6, axis=-1)]
    mod_s = [jnp.repeat(m, ls, axis=0) for m in jnp.split(mod[bp:n_c], 6, axis=-1)]

    xp = x_prompt.reshape(tp, D_MODEL)
    xs = x_sample.reshape(ts, D_MODEL)
    tm_s = _pick_tile(ts, (256, 128, 64, 32, 16, 8))

    tc = _pick_tile(lp // 2, (256, 128, 64))
    m_p, conv_p, h_p, s_p = _mixer_prompt(xp, mod_p[1], mod_p[0], row(norm_pre_mix), w_in_bf, bp, lp, mw, tc)
    proj_s = _inproj(xs, mod_s[1], mod_s[0], row(norm_pre_mix), w_in_bf, tm_s, 1)

    proj_s_pad = jnp.pad(proj_s.reshape(bs, ls, IN_COLS), ((0, 0), (0, SUBLANES - ls), (0, 0)))
    cst_hdr = jnp.pad(state_rglru_conv[0], ((0, 0), (CONV_HDR - (CONV_WIDTH - 1), 0), (0, 0)))
    m_s_pad, conv_s, h_s, s_s = _mixer_sample(proj_s_pad, cst_hdr, state_rglru_h[0][:, None, :],
                                              state_hgrn_S[0], mw, ls)
    m_s = m_s_pad[:, :ls, :].reshape(ts, D_MODEL)

    tmq = _pick_tile(math.gcd(lp, ts), (512, 256, 128, 64, 32, 16, 8))
    x1, hf, idx, gate, rank, counts = _post(
        m_p, m_s, xp, xs, (mod_p[2], mod_p[4], mod_p[3]), (mod_s[2], mod_s[4], mod_s[3]),
        row(norm_post_mix), row(norm_pre_ffn), w_out_bf, router_w[0].T, router_b[0][:, None], tmq, lp // tmq)

    n_blocks = -(-(TOP_K * (tp + ts)) // MOE_BM) + N_EXPERTS
    n_slots = n_blocks * MOE_BM
    pad_starts, fill_lo, pad_ends, block_e, block_half, next_e, n_used = _expert_layout(
        counts[:, 0].astype(I32), n_blocks)
    fill_hi = pad_ends.at[N_EXPERTS - 1].set(n_slots)
    dest = _slots(idx, rank, pad_starts)
    tmd = _pick_tile(math.gcd(lp, ts), (512, 256, 128, 64, 32, 16, 8))
    tmc = _pick_tile(math.gcd(lp, ts), (256, 128, 64, 32, 16, 8))
    xs_rows = _dispatch(fill_lo, fill_hi, _tile_major(dest, tmd), hf, n_slots, tmd)
    ys = _experts(block_e, block_half, next_e, n_used, xs_rows.reshape(n_slots * LANE_CHUNKS, LANES),
                  w_gu[0], b_gu[0][:, None, :], w_down[0], b_down[0][:, None, :])
    ys_tiles = ys.reshape(-1, LANE_CHUNKS, LANES)

    gates_t = gate.T
    ctiles = _tile_major(dest, tmc)
    y_p = _combine(ctiles[:tp // tmc], ys_tiles, gates_t, x1, mod_p[5], row(norm_post_ffn), tmc, lp // tmc, 0)
    y_s = _combine(ctiles[tp // tmc:], ys_tiles, gates_t, x1, mod_s[5], row(norm_post_ffn), tmc, 1, tp // tmc)

    return (y_p.reshape(bp, lp, D_MODEL), y_s.reshape(bs, ls, D_MODEL),
            conv_p[None], h_p.reshape(1, bp, LRU_WIDTH), s_p[None],
            conv_s[None], h_s.reshape(1, bs, LRU_WIDTH), s_s[None])
```

```python
import functools
import math

import jax
import jax.numpy as jnp
from jax import lax
from jax.experimental import pallas as pl
from jax.experimental.pallas import tpu as pltpu

F32 = jnp.float32
BF16 = jnp.bfloat16
I32 = jnp.int32

D_MODEL = 1024
LRU_WIDTH = 512
CONV_WIDTH = 4
LRU_C = 8.0
HG_WIDTH = 512
HG_HEAD_DIM = 128
HG_HEADS = 4
IN_COLS = 2 * LRU_WIDTH + 4 * HG_WIDTH
N_EXPERTS = 32
TOP_K = 4
D_EXPERT = 1024
SWIGLU_LIMIT = 7.0
SWIGLU_ALPHA = 1.702
RMS_EPS = 1e-6

C_XL, C_YL, C_Q, C_F, C_V, C_G = 0, 512, 1024, 1536, 2048, 2560

SUBLANES = 8
LANES = 128
LANE_CHUNKS = D_MODEL // LANES
CONV_HDR = SUBLANES
HG_CHUNK = 64
HG_MAX_EXPONENT = 60.0
MOE_BM = 512
ZERO_ROWS = 64
VMEM_LIMIT = 56 * 1024 * 1024

_NT = (((1,), (1,)), ((), ()))
_TN = (((0,), (0,)), ((), ()))


def _cparams(sem):
    return pltpu.CompilerParams(dimension_semantics=sem, vmem_limit_bytes=VMEM_LIMIT)


def _rms(x, g):
    return x * lax.rsqrt(jnp.mean(x * x, axis=-1, keepdims=True) + RMS_EPS) * g


def _gelu_tanh(x):
    c = math.sqrt(2.0 / math.pi)
    return 0.5 * x * (1.0 + jnp.tanh(c * (x + 0.044715 * (x * x * x))))


def _ada_kernel(c_ref, w_ref, b_ref, o_ref):
    c = c_ref[...]
    s = c * jax.nn.sigmoid(c)
    o_ref[...] = jnp.dot(s.astype(BF16), w_ref[...].astype(BF16), preferred_element_type=F32) + b_ref[...]


def _ada(c_all, ada_w, ada_b):
    n = c_all.shape[0]
    tn = 1024
    return pl.pallas_call(
        _ada_kernel,
        out_shape=jax.ShapeDtypeStruct((n, 6 * D_MODEL), F32),
        grid=(6 * D_MODEL // tn,),
        in_specs=[pl.BlockSpec((n, D_MODEL), lambda j: (0, 0)),
                  pl.BlockSpec((D_MODEL, tn), lambda j: (0, j)),
                  pl.BlockSpec((1, tn), lambda j: (0, j))],
        out_specs=pl.BlockSpec((n, tn), lambda j: (0, j)),
        compiler_params=_cparams(("arbitrary",)),
    )(c_all, ada_w, ada_b)


def _inproj_kernel(x_ref, sc_ref, sh_ref, g_ref, w_ref, o_ref):
    h = _rms(x_ref[...], g_ref[...]) * (1.0 + sc_ref[...]) + sh_ref[...]
    o_ref[...] = jnp.dot(h.astype(BF16), w_ref[...], preferred_element_type=F32)


def _mod_spec(mod, tm, tiles_per_seq):
    if mod.ndim == 3:
        return pl.BlockSpec((None, 1, D_MODEL), lambda i: (i // tiles_per_seq, 0, 0))
    return pl.BlockSpec((tm, D_MODEL), lambda i: (i, 0))


def _inproj(x, sc, sh, g, w_bf, tm, tiles_per_seq):
    t = x.shape[0]
    return pl.pallas_call(
        _inproj_kernel,
        out_shape=jax.ShapeDtypeStruct((t, IN_COLS), F32),
        grid=(t // tm,),
        in_specs=[pl.BlockSpec((tm, D_MODEL), lambda i: (i, 0)),
                  _mod_spec(sc, tm, tiles_per_seq), _mod_spec(sh, tm, tiles_per_seq),
                  pl.BlockSpec((1, D_MODEL), lambda i: (0, 0)),
                  pl.BlockSpec((D_MODEL, IN_COLS), lambda i: (0, 0))],
        out_specs=pl.BlockSpec((tm, IN_COLS), lambda i: (i, 0)),
        compiler_params=_cparams(("arbitrary",)),
    )(x, sc, sh, g, w_bf)


def _sigmoid(x):
    return 0.5 * jnp.tanh(0.5 * x) + 0.5


def _group_rows(x):
    rows, w = x.shape
    return x.reshape(rows // SUBLANES, SUBLANES, w)


def _scan_rows(a, u, h0):
    a3, u3 = _group_rows(a), _group_rows(u)
    r3 = lax.broadcasted_iota(I32, a3.shape, 1)
    s = 1
    while s < SUBLANES:
        keep = r3 >= s
        a_sh = jnp.where(keep, pltpu.roll(a3, s, 1), 1.0)
        u_sh = jnp.where(keep, pltpu.roll(u3, s, 1), 0.0)
        u3 = a3 * u_sh + u3
        a3 = a3 * a_sh
        s *= 2
    out, h = [], h0
    for i in range(a3.shape[0]):
        hg = u3[i] + a3[i] * h
        h = hg[SUBLANES - 1:SUBLANES, :]
        out.append(hg)
    return jnp.concatenate(out, axis=0), h


def _chunk_cumsum(x, chunk):
    x3 = _group_rows(x)
    r3 = lax.broadcasted_iota(I32, x3.shape, 1)
    s = 1
    while s < SUBLANES:
        x3 = x3 + jnp.where(r3 >= s, pltpu.roll(x3, s, 1), 0.0)
        s *= 2
    out, carry = [], None
    for i in range(x3.shape[0]):
        cur = x3[i] if i % (chunk // SUBLANES) == 0 else x3[i] + carry
        carry = cur[SUBLANES - 1:SUBLANES, :]
        out.append(cur)
    return jnp.concatenate(out, axis=0)


def _lru_gates(xc, wg_ref, bg_ref, lam_ref):
    gates = jnp.dot(xc.astype(BF16), wg_ref[...], preferred_element_type=F32) + bg_ref[...]
    r = _sigmoid(gates[:, :LRU_WIDTH])
    ig = _sigmoid(gates[:, LRU_WIDTH:])
    z = -lam_ref[...]
    softplus = jnp.maximum(z, 0.0) + jnp.log1p(jnp.exp(-jnp.abs(z)))
    log_a = -LRU_C * r * softplus
    a = jnp.exp(log_a)
    th = jnp.tanh(log_a)
    mult2 = -2.0 * th / (1.0 - th)
    return a, jnp.sqrt(mult2), ig


def _conv4(xbuf, xl, cw_ref, cb_ref, rows):
    cw = cw_ref[...]
    h = CONV_HDR
    return (cb_ref[...] + cw[0:1] * xbuf[h - 3:h - 3 + rows, :] + cw[1:2] * xbuf[h - 2:h - 2 + rows, :]
            + cw[2:3] * xbuf[h - 1:h - 1 + rows, :] + cw[3:4] * xl)


def _conv4_rolled(xl, prev, cw_ref, cb_ref):
    cw = cw_ref[...]
    row8 = lax.broadcasted_iota(I32, prev.shape, 0)
    acc = cb_ref[...] + cw[CONV_WIDTH - 1:CONV_WIDTH] * xl
    for s in range(1, CONV_WIDTH):
        sh = pltpu.roll(xl, s, 0)
        top = jnp.where(row8 < s, pltpu.roll(prev, s, 0), sh[0:SUBLANES, :])
        sh = jnp.concatenate([top, sh[SUBLANES:, :]], axis=0)
        acc = acc + cw[CONV_WIDTH - 1 - s:CONV_WIDTH - s] * sh
    return acc


def _head_norm_gate(o, gate, hgn):
    return _rms(o, hgn) * _sigmoid(gate)


def _hgrn_window(q, k, v, b, state, steps, state_is_transposed):
    row = lax.broadcasted_iota(I32, q.shape, 0)
    valid = row < steps
    b_end = b[steps - 1:steps, :]
    q_in = jnp.where(valid, q * jnp.exp(b), 0.0).astype(BF16)
    if state_is_transposed:
        o = lax.dot_general(q_in, state.astype(BF16), _NT, preferred_element_type=F32)
    else:
        o = jnp.dot(q_in, state.astype(BF16), preferred_element_type=F32)
    for d in range(steps):
        ok = valid & (row >= d)
        k_d = k if d == 0 else pltpu.roll(k, d, 0)
        v_d = v if d == 0 else pltpu.roll(v, d, 0)
        b_d = b if d == 0 else pltpu.roll(b, d, 0)
        decay = jnp.exp(jnp.where(ok, b - b_d, 0.0))
        wgt = jnp.sum(jnp.where(ok, q * k_d * decay, 0.0), axis=-1, keepdims=True)
        o = o + wgt * v_d
    k_out = jnp.where(valid, k * jnp.exp(jnp.where(valid, b_end - b, 0.0)), 0.0).astype(BF16)
    v_ok = jnp.where(valid, v, 0.0).astype(BF16)
    keep = jnp.exp(b_end)
    if state_is_transposed:
        s_new = state * keep + lax.dot_general(v_ok, k_out, _TN, preferred_element_type=F32)
    else:
        keep_col = jnp.broadcast_to(keep, q.shape).T[:, 0:1]
        s_new = state * keep_col + lax.dot_general(k_out, v_ok, _TN, preferred_element_type=F32)
    return o, s_new


def _tile_cumsum(x):
    row = lax.broadcasted_iota(I32, x.shape, 0)
    s = 1
    while s < SUBLANES:
        x = x + jnp.where(row >= s, pltpu.roll(x, s, 0), 0.0)
        s *= 2
    return x


def _mix_tile(proj_ref, cw_ref, cb_ref, wg_ref, bg_ref, lam_ref, lb_ref, hgn_ref, m_ref, xprev, hcar, st, st0, obuf,
              seq_start, tc, side_work):
    w = LRU_WIDTH
    side_work = iter(side_work)

    next(side_work)()
    xl = proj_ref[:, C_XL:C_XL + w]
    xc = _conv4_rolled(xl, xprev[...], cw_ref, cb_ref)
    xprev[...] = xl[tc - SUBLANES:tc, :]
    a, mult, ig = _lru_gates(xc, wg_ref, bg_ref, lam_ref)
    row = lax.broadcasted_iota(I32, (tc, w), 0)
    if seq_start is not None:
        mult = jnp.where((row == 0) & seq_start, 1.0, mult)
    h, h_last = _scan_rows(a, mult * ig * xc, hcar[0:1, :])
    hcar[0:1, :] = h_last
    m_ref[:, 0:w] = (h * _gelu_tanh(proj_ref[:, C_YL:C_YL + w])).astype(BF16)

    next(side_work)()
    lb = lb_ref[...]
    f = lb + (1.0 - lb) * _sigmoid(proj_ref[:, C_F:C_F + w])
    k = 1.0 - f
    b = _chunk_cumsum(jnp.log(f), HG_CHUNK)
    q = proj_ref[:, C_Q:C_Q + w]
    v = proj_ref[:, C_V:C_V + w]
    mid = HG_CHUNK // 2
    st0[...] = st[...]
    tri = (lax.broadcasted_iota(I32, (HG_CHUNK, HG_CHUNK), 0)
           >= lax.broadcasted_iota(I32, (HG_CHUNK, HG_CHUNK), 1))
    states = [st[hd] for hd in range(HG_HEADS)]
    for c in range(tc // HG_CHUNK):
        next(side_work)()
        rs = slice(c * HG_CHUNK, (c + 1) * HG_CHUNK)
        for hd in range(HG_HEADS):
            cs = slice(hd * HG_HEAD_DIM, (hd + 1) * HG_HEAD_DIM)
            s_t = states[hd]
            bc, qc, kc = b[rs, cs], q[rs, cs], k[rs, cs]
            vb = v[rs, cs].astype(BF16)
            b_mid = bc[mid - 1:mid, :]
            b_end = bc[HG_CHUNK - 1:HG_CHUNK, :]
            qp = (qc * jnp.exp(bc - b_mid)).astype(BF16)
            kp = (kc * jnp.exp(b_mid - bc)).astype(BF16)
            att = lax.dot_general(qp, kp, _NT, preferred_element_type=F32)
            att = jnp.where(tri, att, 0.0)
            q_in = (qc * jnp.exp(bc)).astype(BF16)
            o = (jnp.dot(att.astype(BF16), vb, preferred_element_type=F32)
                 + lax.dot_general(q_in, s_t.astype(BF16), _NT, preferred_element_type=F32))
            k_out = (kc * jnp.exp(b_end - bc)).astype(BF16)
            states[hd] = s_t * jnp.exp(b_end) + lax.dot_general(vb, k_out, _TN, preferred_element_type=F32)
            obuf[rs, cs] = o
    for hd in range(HG_HEADS):
        st[hd] = states[hd]

    half_decay = None
    for c in range(tc // HG_CHUNK):
        b_mid = b[c * HG_CHUNK + mid - 1:c * HG_CHUNK + mid, :]
        b_end = b[(c + 1) * HG_CHUNK - 1:(c + 1) * HG_CHUNK, :]
        worst = jnp.maximum(-b_mid, b_mid - b_end)
        half_decay = worst if half_decay is None else jnp.maximum(half_decay, worst)

    @pl.when(jnp.logical_not(jnp.max(half_decay) < HG_MAX_EXPONENT))
    def _():
        st[...] = st0[...]

        def window(gi, carry):
            rows = pl.ds(pl.multiple_of(gi * SUBLANES, SUBLANES), SUBLANES)
            f8 = lb + (1.0 - lb) * _sigmoid(proj_ref[rows, C_F:C_F + w])
            b8 = _tile_cumsum(jnp.log(f8))
            k8 = 1.0 - f8
            q8 = proj_ref[rows, C_Q:C_Q + w]
            v8 = proj_ref[rows, C_V:C_V + w]
            for hd in range(HG_HEADS):
                cs = slice(hd * HG_HEAD_DIM, (hd + 1) * HG_HEAD_DIM)
                o, s_new = _hgrn_window(q8[:, cs], k8[:, cs], v8[:, cs], b8[:, cs], st[hd], SUBLANES, True)
                st[hd] = s_new
                obuf[rows, cs] = o
            return carry
        lax.fori_loop(0, tc // SUBLANES, window, 0)

    hgn = hgn_ref[...]
    for hd in range(HG_HEADS):
        cs = slice(hd * HG_HEAD_DIM, (hd + 1) * HG_HEAD_DIM)
        gate = proj_ref[:, C_G + hd * HG_HEAD_DIM:C_G + (hd + 1) * HG_HEAD_DIM]
        m_ref[:, w + hd * HG_HEAD_DIM:w + (hd + 1) * HG_HEAD_DIM] = (
            _head_norm_gate(obuf[:, cs], gate, hgn).astype(BF16))


def _mixer_prompt_kernel(x_ref, xn_ref, sc_ref, sh_ref, scn_ref, shn_ref, g_ref, win_ref, cw_ref, cb_ref, wg_ref,
                         bg_ref, lam_ref, lb_ref, hgn_ref, m_ref, conv_ref, h_ref, s_ref,
                         proj_a, proj_b, hn_buf, xprev, hcar, st, st0, obuf, *, tc):
    j = pl.program_id(1)
    n_phases = 2 + tc // HG_CHUNK
    cols = IN_COLS // n_phases

    def project(x, sc, sh, out_ref):
        def chunk(c):
            if c == 0:
                hn_buf[...] = (_rms(x(), g_ref[...]) * (1.0 + sc[...]) + sh[...]).astype(BF16)
            out_ref[:, c * cols:(c + 1) * cols] = jnp.dot(hn_buf[...], win_ref[:, c * cols:(c + 1) * cols],
                                                          preferred_element_type=F32)
        return [functools.partial(chunk, c) for c in range(n_phases)]

    def mix(proj_ref, first_row, seq_start, side_work):
        _mix_tile(proj_ref, cw_ref, cb_ref, wg_ref, bg_ref, lam_ref, lb_ref, hgn_ref,
                  m_ref.at[pl.ds(first_row, tc), :], xprev, hcar, st, st0, obuf, seq_start, tc, side_work)

    @pl.when((pl.program_id(0) == 0) & (j == 0))
    def _():
        for thunk in project(lambda: x_ref[0:tc, :], sc_ref, sh_ref, proj_a):
            thunk()

    @pl.when(j == 0)
    def _():
        xprev[...] = jnp.zeros_like(xprev)
        hcar[...] = jnp.zeros_like(hcar)
        st[...] = jnp.zeros_like(st)

    mix(proj_a, 0, j == 0, project(lambda: x_ref[tc:2 * tc, :], sc_ref, sh_ref, proj_b))
    mix(proj_b, tc, None, project(lambda: xn_ref[...], scn_ref, shn_ref, proj_a))

    @pl.when(j == pl.num_programs(1) - 1)
    def _():
        conv_ref[...] = xprev[SUBLANES - (CONV_WIDTH - 1):SUBLANES, :]
        h_ref[...] = hcar[0:1, :]
        for hd in range(HG_HEADS):
            s_ref[hd] = st[hd].T


def _mixer_prompt(x, sc, sh, g, w_in_bf, nb, seq, mw, tc):
    cw, cb, wg, bg, lam, lb, hgn = mw
    nj = seq // (2 * tc)
    last_tile = nb * seq // tc - 1
    next_tile = lambda b, j: jnp.minimum(2 * (b * nj + j) + 2, last_tile)
    const = lambda shape: pl.BlockSpec(shape, lambda b, j: tuple(0 for _ in shape))
    per_seq = lambda: pl.BlockSpec((None, 1, D_MODEL), lambda b, j: (b, 0, 0))
    next_seq = lambda: pl.BlockSpec((None, 1, D_MODEL), lambda b, j: (next_tile(b, j) // (2 * nj), 0, 0))
    return pl.pallas_call(
        functools.partial(_mixer_prompt_kernel, tc=tc),
        out_shape=(jax.ShapeDtypeStruct((nb * seq, D_MODEL), BF16),
                   jax.ShapeDtypeStruct((nb, CONV_WIDTH - 1, LRU_WIDTH), F32),
                   jax.ShapeDtypeStruct((nb, 1, LRU_WIDTH), F32),
                   jax.ShapeDtypeStruct((nb, HG_HEADS, HG_HEAD_DIM, HG_HEAD_DIM), F32)),
        grid=(nb, nj),
        in_specs=[pl.BlockSpec((2 * tc, D_MODEL), lambda b, j: (b * nj + j, 0)),
                  pl.BlockSpec((tc, D_MODEL), lambda b, j: (next_tile(b, j), 0)),
                  per_seq(), per_seq(), next_seq(), next_seq(), const((1, D_MODEL)), const((D_MODEL, IN_COLS)),
                  const((CONV_WIDTH, LRU_WIDTH)), const((1, LRU_WIDTH)),
                  const((LRU_WIDTH, 2 * LRU_WIDTH)), const((1, 2 * LRU_WIDTH)),
                  const((1, LRU_WIDTH)), const((1, HG_WIDTH)), const((1, HG_HEAD_DIM))],
        out_specs=(pl.BlockSpec((2 * tc, D_MODEL), lambda b, j: (b * nj + j, 0)),
                   pl.BlockSpec((None, CONV_WIDTH - 1, LRU_WIDTH), lambda b, j: (b, 0, 0)),
                   pl.BlockSpec((None, 1, LRU_WIDTH), lambda b, j: (b, 0, 0)),
                   pl.BlockSpec((None, HG_HEADS, HG_HEAD_DIM, HG_HEAD_DIM), lambda b, j: (b, 0, 0, 0))),
        scratch_shapes=[pltpu.VMEM((tc, IN_COLS), F32),
                        pltpu.VMEM((tc, IN_COLS), F32),
                        pltpu.VMEM((tc, D_MODEL), BF16),
                        pltpu.VMEM((SUBLANES, LRU_WIDTH), F32),
                        pltpu.VMEM((SUBLANES, LRU_WIDTH), F32),
                        pltpu.VMEM((HG_HEADS, HG_HEAD_DIM, HG_HEAD_DIM), F32),
                        pltpu.VMEM((HG_HEADS, HG_HEAD_DIM, HG_HEAD_DIM), F32),
                        pltpu.VMEM((tc, HG_WIDTH), F32)],
        compiler_params=_cparams(("arbitrary", "arbitrary")),
    )(x, x, sc, sh, sc, sh, g, w_in_bf, cw, cb, wg, bg, lam, lb, hgn)


def _mixer_sample_kernel(proj_ref, cst_ref, h0_ref, s0_ref, cw_ref, cb_ref, wg_ref, bg_ref, lam_ref, lb_ref,
                         hgn_ref, m_ref, conv_ref, h_ref, s_ref, xbuf, *, steps):
    for bi in range(proj_ref.shape[0]):
        _mixer_sample_one(proj_ref.at[bi], cst_ref.at[bi], h0_ref.at[bi], s0_ref.at[bi], cw_ref, cb_ref, wg_ref,
                          bg_ref, lam_ref, lb_ref, hgn_ref, m_ref, slice(bi * steps, (bi + 1) * steps), conv_ref.at[bi],
                          h_ref.at[bi],
                          s_ref.at[bi], xbuf.at[bi], steps)


def _mixer_sample_one(proj_ref, cst_ref, h0_ref, s0_ref, cw_ref, cb_ref, wg_ref, bg_ref, lam_ref, lb_ref,
                      hgn_ref, m_ref, m_rows, conv_ref, h_ref, s_ref, xbuf, steps):
    w = LRU_WIDTH
    rows = SUBLANES
    row = lax.broadcasted_iota(I32, (rows, w), 0)

    xl = proj_ref[:, C_XL:C_XL + w]
    xbuf[0:CONV_HDR, :] = cst_ref[...]
    xbuf[CONV_HDR:CONV_HDR + rows, :] = xl
    xc = _conv4(xbuf, xl, cw_ref, cb_ref, rows)
    conv_ref[...] = xbuf[CONV_HDR + steps - 3:CONV_HDR + steps, :]
    a, mult, ig = _lru_gates(xc, wg_ref, bg_ref, lam_ref)
    u = mult * ig * xc
    h = h0_ref[...]
    hs = jnp.zeros((rows, w), F32)
    for t in range(steps):
        h = a[t:t + 1, :] * h + u[t:t + 1, :]
        hs = jnp.where(row == t, h, hs)
    h_ref[...] = h
    m_ref[m_rows, 0:w] = (hs * _gelu_tanh(proj_ref[:, C_YL:C_YL + w]))[0:steps, :]

    lb = lb_ref[...]
    f = lb + (1.0 - lb) * _sigmoid(proj_ref[:, C_F:C_F + w])
    k = 1.0 - f
    b = _tile_cumsum(jnp.log(f))
    q = proj_ref[:, C_Q:C_Q + w]
    v = proj_ref[:, C_V:C_V + w]
    hgn = hgn_ref[...]
    for hd in range(HG_HEADS):
        cs = slice(hd * HG_HEAD_DIM, (hd + 1) * HG_HEAD_DIM)
        o, s_new = _hgrn_window(q[:, cs], k[:, cs], v[:, cs], b[:, cs], s0_ref[hd], steps, False)
        s_ref[hd] = s_new
        gate = proj_ref[:, C_G + hd * HG_HEAD_DIM:C_G + (hd + 1) * HG_HEAD_DIM]
        m_ref[m_rows, w + hd * HG_HEAD_DIM:w + (hd + 1) * HG_HEAD_DIM] = _head_norm_gate(o, gate, hgn)[0:steps, :]


def _mixer_sample(proj_pad, cst_hdr, h0, s0, mw, steps):
    cw, cb, wg, bg, lam, lb, hgn = mw
    nb = proj_pad.shape[0]
    per_step = _pick_tile(nb, (8, 4, 2, 1))
    const = lambda shape: pl.BlockSpec(shape, lambda b: tuple(0 for _ in shape))
    per_b = lambda shape: pl.BlockSpec((per_step,) + shape, lambda b: (b,) + tuple(0 for _ in shape))
    return pl.pallas_call(
        functools.partial(_mixer_sample_kernel, steps=steps),
        out_shape=(jax.ShapeDtypeStruct((nb * steps, D_MODEL), F32),
                   jax.ShapeDtypeStruct((nb, CONV_WIDTH - 1, LRU_WIDTH), F32),
                   jax.ShapeDtypeStruct((nb, 1, LRU_WIDTH), F32),
                   jax.ShapeDtypeStruct((nb, HG_HEADS, HG_HEAD_DIM, HG_HEAD_DIM), F32)),
        grid=(nb // per_step,),
        in_specs=[per_b((SUBLANES, IN_COLS)), per_b((CONV_HDR, LRU_WIDTH)), per_b((1, LRU_WIDTH)),
                  per_b((HG_HEADS, HG_HEAD_DIM, HG_HEAD_DIM)),
                  const((CONV_WIDTH, LRU_WIDTH)), const((1, LRU_WIDTH)),
                  const((LRU_WIDTH, 2 * LRU_WIDTH)), const((1, 2 * LRU_WIDTH)),
                  const((1, LRU_WIDTH)), const((1, HG_WIDTH)), const((1, HG_HEAD_DIM))],
        out_specs=(pl.BlockSpec((per_step * steps, D_MODEL), lambda b: (b, 0)),
                   per_b((CONV_WIDTH - 1, LRU_WIDTH)), per_b((1, LRU_WIDTH)),
                   per_b((HG_HEADS, HG_HEAD_DIM, HG_HEAD_DIM))),
        scratch_shapes=[pltpu.VMEM((per_step, CONV_HDR + SUBLANES, LRU_WIDTH), F32)],
        compiler_params=_cparams(("arbitrary",)),
    )(proj_pad, cst_hdr, h0, s0, cw, cb, wg, bg, lam, lb, hgn)


def _to_row_tiles(ref, val, rows):
    for c in range(LANE_CHUNKS):
        ref[pl.ds(c, rows, stride=LANE_CHUNKS), :] = val[:, c * LANES:(c + 1) * LANES]


def _from_row_tiles(ref, base, rows):
    return jnp.concatenate(
        [ref[pl.ds(base + c, rows, stride=LANE_CHUNKS), :] for c in range(LANE_CHUNKS)], axis=-1)


def _post_kernel(mp_ref, ms_ref, xp_ref, xs_ref, gtmp_ref, scfp_ref, shfp_ref, gtms_ref, scfs_ref, shfs_ref,
                 npost_ref, npre_ref, wout_ref, rwt_ref, rb_ref,
                 x1_ref, hf_ref, idx_ref, gate_ref, rank_ref, cnt_out_ref, cnt, *, n_p):
    i = pl.program_id(0)
    tm = xp_ref.shape[0]

    @pl.when(i == 0)
    def _():
        cnt[...] = jnp.zeros_like(cnt)

    def run(m_ref, x_ref, gtm_ref, scf_ref, shf_ref):
        mix = jnp.dot(m_ref[...].astype(BF16), wout_ref[...], preferred_element_type=F32)
        x1 = x_ref[...] + gtm_ref[...] * _rms(mix, npost_ref[...])
        x1_ref[...] = x1
        hf = _rms(x1, npre_ref[...]) * (1.0 + scf_ref[...]) + shf_ref[...]
        _to_row_tiles(hf_ref, hf, tm)
        logits = lax.dot_general(rwt_ref[...], hf, _NT, precision=lax.Precision.HIGHEST,
                                 preferred_element_type=F32) + rb_ref[...]
        e_iota = lax.broadcasted_iota(I32, logits.shape, 0)
        vals, idxs = [], []
        for _ in range(TOP_K):
            mx = jnp.max(logits, axis=0, keepdims=True)
            ix = jnp.min(jnp.where(logits == mx, e_iota, N_EXPERTS), axis=0, keepdims=True)
            vals.append(mx)
            idxs.append(ix)
            logits = jnp.where(e_iota == ix, -jnp.inf, logits)
        exps = [jnp.exp(vv - vals[0]) for vv in vals]
        den = exps[0] + exps[1] + exps[2] + exps[3]
        earlier = (lax.broadcasted_iota(I32, (tm, tm), 0) < lax.broadcasted_iota(I32, (tm, tm), 1))
        earlier = jnp.where(earlier, 1.0, 0.0).astype(BF16)
        base = cnt[...]
        for kk in range(TOP_K):
            idx_ref[kk:kk + 1, :] = idxs[kk]
            gate_ref[kk:kk + 1, :] = exps[kk] / den
            hit = e_iota == idxs[kk]
            onehot = jnp.where(hit, 1.0, 0.0)
            before = jnp.dot(onehot.astype(BF16), earlier, preferred_element_type=F32)
            rank = jnp.sum(jnp.where(hit, before + base, 0.0), axis=0, keepdims=True)
            rank_ref[kk:kk + 1, :] = rank.astype(I32)
            base = base + jnp.sum(onehot, axis=1, keepdims=True)
        cnt[...] = base

    @pl.when(i < n_p)
    def _():
        run(mp_ref, xp_ref, gtmp_ref, scfp_ref, shfp_ref)

    @pl.when(i >= n_p)
    def _():
        run(ms_ref, xs_ref, gtms_ref, scfs_ref, shfs_ref)

    @pl.when(i == pl.num_programs(0) - 1)
    def _():
        cnt_out_ref[...] = cnt[...]


def _post(m_p, m_s, x_p, x_s, mod_p, mod_s, npost, npre, wout_bf, rwt, rb, tm, tiles_per_seq):
    n_p, n_s = x_p.shape[0] // tm, x_s.shape[0] // tm
    t = (n_p + n_s) * tm
    first = lambda i: jnp.minimum(i, n_p - 1)
    second = lambda i: jnp.maximum(i - n_p, 0)
    row = lambda: pl.BlockSpec((1, D_MODEL), lambda i: (0, 0))
    tok_p = lambda: pl.BlockSpec((tm, D_MODEL), lambda i: (first(i), 0))
    tok_s = lambda: pl.BlockSpec((tm, D_MODEL), lambda i: (second(i), 0))
    seq_p = lambda: pl.BlockSpec((None, 1, D_MODEL), lambda i: (first(i) // tiles_per_seq, 0, 0))
    per_k = lambda: pl.BlockSpec((TOP_K, tm), lambda i: (0, i))
    return pl.pallas_call(
        functools.partial(_post_kernel, n_p=n_p),
        out_shape=(jax.ShapeDtypeStruct((t, D_MODEL), F32),
                   jax.ShapeDtypeStruct((t * LANE_CHUNKS, LANES), F32),
                   jax.ShapeDtypeStruct((TOP_K, t), I32), jax.ShapeDtypeStruct((TOP_K, t), F32),
                   jax.ShapeDtypeStruct((TOP_K, t), I32), jax.ShapeDtypeStruct((N_EXPERTS, 1), F32)),
        grid=(n_p + n_s,),
        in_specs=[tok_p(), tok_s(), tok_p(), tok_s(), seq_p(), seq_p(), seq_p(), tok_s(), tok_s(), tok_s(),
                  row(), row(),
                  pl.BlockSpec((D_MODEL, D_MODEL), lambda i: (0, 0)),
                  pl.BlockSpec((N_EXPERTS, D_MODEL), lambda i: (0, 0)),
                  pl.BlockSpec((N_EXPERTS, 1), lambda i: (0, 0))],
        out_specs=(pl.BlockSpec((tm, D_MODEL), lambda i: (i, 0)),
                   pl.BlockSpec((tm * LANE_CHUNKS, LANES), lambda i: (i, 0)),
                   per_k(), per_k(), per_k(), pl.BlockSpec((N_EXPERTS, 1), lambda i: (0, 0))),
        scratch_shapes=[pltpu.VMEM((N_EXPERTS, 1), F32)],
        compiler_params=_cparams(("arbitrary",)),
    )(m_p, m_s, x_p, x_s, *mod_p, *mod_s, npost, npre, wout_bf, rwt, rb)


def _dispatch_kernel(fill_lo_ref, fill_hi_ref, dest_ref, hf_ref, xs_hbm, zeros, row_sem, pad_sem, *, tm):
    i = pl.program_id(0)
    n_rows = TOP_K * tm

    @pl.when(i == 0)
    def _():
        zeros[...] = jnp.zeros_like(zeros)

        def per_expert(e, totals):
            lo, hi = fill_lo_ref[e], fill_hi_ref[e]
            mid = jnp.minimum((lo + ZERO_ROWS - 1) // ZERO_ROWS * ZERO_ROWS, hi)

            def per_row(s, carry):
                pltpu.make_async_copy(zeros.at[0], xs_hbm.at[s], pad_sem).start()
                return carry
            lax.fori_loop(lo, mid, per_row, 0)

            def per_chunk(c, carry):
                pltpu.make_async_copy(zeros, xs_hbm.at[pl.ds(pl.multiple_of(c * ZERO_ROWS, ZERO_ROWS), ZERO_ROWS)],
                                      pad_sem).start()
                return carry
            lax.fori_loop(mid // ZERO_ROWS, hi // ZERO_ROWS, per_chunk, 0)
            return totals[0] + (mid - lo), totals[1] + (hi - mid) // ZERO_ROWS
        n_rows_filled, n_chunks_filled = lax.fori_loop(0, N_EXPERTS, per_expert, (0, 0))

        def drain_row(s, carry):
            pltpu.make_async_copy(zeros.at[0], xs_hbm.at[0], pad_sem).wait()
            return carry
        lax.fori_loop(0, n_rows_filled, drain_row, 0)

        def drain_chunk(c, carry):
            pltpu.make_async_copy(zeros, xs_hbm.at[pl.ds(0, ZERO_ROWS)], pad_sem).wait()
            return carry
        lax.fori_loop(0, n_chunks_filled, drain_chunk, 0)

    for n in range(n_rows):
        pltpu.make_async_copy(hf_ref.at[pl.ds((n % tm) * LANE_CHUNKS, LANE_CHUNKS), :],
                              xs_hbm.at[dest_ref[0, n]], row_sem).start(priority=n % 2)
    pltpu.make_async_copy(xs_hbm.at[pl.ds(0, n_rows)], xs_hbm.at[pl.ds(0, n_rows)], row_sem).wait()


def _dispatch(fill_lo, fill_hi, dest_tiles, hf_rows, n_slots, tm):
    return pl.pallas_call(
        functools.partial(_dispatch_kernel, tm=tm),
        out_shape=jax.ShapeDtypeStruct((n_slots, LANE_CHUNKS, LANES), F32),
        grid_spec=pltpu.PrefetchScalarGridSpec(
            num_scalar_prefetch=2, grid=(dest_tiles.shape[0],),
            in_specs=[pl.BlockSpec((None, 1, TOP_K * tm), lambda i, lo, hi: (i, 0, 0), memory_space=pltpu.SMEM),
                      pl.BlockSpec((tm * LANE_CHUNKS, LANES), lambda i, lo, hi: (i, 0))],
            out_specs=pl.BlockSpec(memory_space=pl.ANY),
            scratch_shapes=[pltpu.VMEM((ZERO_ROWS, LANE_CHUNKS, LANES), F32),
                            pltpu.SemaphoreType.DMA, pltpu.SemaphoreType.DMA]),
        compiler_params=_cparams(("arbitrary",)),
    )(fill_lo, fill_hi, dest_tiles, hf_rows)


def _expert_kernel(be_ref, half_ref, nxt_ref, nu_ref, xs_ref, wgu_hbm, bgu_ref, wd_hbm, bd_ref, ys_ref,
                   wgu_f32, wd_f32, wgu_bf, wd_bf, w_sem):
    i = pl.program_id(0)
    n_used = nu_ref[0]

    def weight_copies(e):
        return (pltpu.make_async_copy(wgu_hbm.at[e], wgu_f32, w_sem.at[0]),
                pltpu.make_async_copy(wd_hbm.at[e], wd_f32, w_sem.at[1]))

    @pl.when(i == 0)
    def _():
        for cp in weight_copies(be_ref[0]):
            cp.start()

    @pl.when(i < n_used)
    def _():
        prev = be_ref[jnp.maximum(i - 1, 0)]

        @pl.when((i == 0) | (be_ref[i] != prev))
        def _():
            for cp in weight_copies(be_ref[i]):
                cp.wait()
            wgu_bf[...] = wgu_f32[...].astype(BF16)
            wd_bf[...] = wd_f32[...].astype(BF16)
            nxt = nxt_ref[be_ref[i]]

            @pl.when(nxt >= 0)
            def _():
                for cp in weight_copies(nxt):
                    cp.start()

        def swiglu_rows(rows):
            x = _from_row_tiles(xs_ref, 0, rows).astype(BF16)
            hu = jnp.dot(x, wgu_bf[...], preferred_element_type=F32) + bgu_ref[...]
            glu = jnp.minimum(hu[:, :D_EXPERT], SWIGLU_LIMIT)
            lin = jnp.clip(hu[:, D_EXPERT:], -SWIGLU_LIMIT, SWIGLU_LIMIT)
            act = glu * _sigmoid(SWIGLU_ALPHA * glu) * (lin + 1.0)
            y = jnp.dot(act.astype(BF16), wd_bf[...], preferred_element_type=F32) + bd_ref[...]
            _to_row_tiles(ys_ref, y, rows)

        @pl.when(half_ref[i] == 0)
        def _():
            swiglu_rows(MOE_BM)

        @pl.when(half_ref[i] != 0)
        def _():
            swiglu_rows(MOE_BM // 2)
            ys_ref[MOE_BM // 2 * LANE_CHUNKS:, :] = jnp.zeros((MOE_BM // 2 * LANE_CHUNKS, LANES), F32)

    @pl.when(i >= n_used)
    def _():
        ys_ref[...] = jnp.zeros_like(ys_ref)


def _experts(block_e, block_half, next_e, n_used, xs2d, w_gu, b_gu, w_down, b_down):
    n_blocks = block_e.shape[0]
    rows = MOE_BM * LANE_CHUNKS
    live = lambda i, be, hf, nx, nu: (jnp.minimum(i, nu[0] - 1), 0)
    return pl.pallas_call(
        _expert_kernel,
        out_shape=jax.ShapeDtypeStruct(xs2d.shape, F32),
        grid_spec=pltpu.PrefetchScalarGridSpec(
            num_scalar_prefetch=4,
            grid=(n_blocks,),
            in_specs=[pl.BlockSpec((rows, LANES), live),
                      pl.BlockSpec(memory_space=pl.ANY),
                      pl.BlockSpec((None, 1, 2 * D_EXPERT), lambda i, be, hf, nx, nu: (be[i], 0, 0)),
                      pl.BlockSpec(memory_space=pl.ANY),
                      pl.BlockSpec((None, 1, D_MODEL), lambda i, be, hf, nx, nu: (be[i], 0, 0))],
            out_specs=pl.BlockSpec((rows, LANES), lambda i, be, hf, nx, nu: (i, 0)),
            scratch_shapes=[pltpu.VMEM((D_MODEL, 2 * D_EXPERT), F32),
                            pltpu.VMEM((D_EXPERT, D_MODEL), F32),
                            pltpu.VMEM((D_MODEL, 2 * D_EXPERT), BF16),
                            pltpu.VMEM((D_EXPERT, D_MODEL), BF16),
                            pltpu.SemaphoreType.DMA((2,))]),
        compiler_params=_cparams(("arbitrary",)),
    )(block_e, block_half, next_e, n_used, xs2d, w_gu, b_gu, w_down, b_down)


def _combine_kernel(dcur_ref, dnext_ref, ys_hbm, g_ref, x1_ref, gt_ref, np_ref, o_ref, ybuf, row_sem, *, tm):
    i = pl.program_id(0)
    n_rows = TOP_K * tm
    slot = i % 2

    def slot_rows(s):
        return ybuf.at[pl.ds(pl.multiple_of(s * n_rows * LANE_CHUNKS, n_rows * LANE_CHUNKS), n_rows * LANE_CHUNKS), :]

    def issue(d_ref, s):
        dst = slot_rows(s)
        for n in range(n_rows):
            pltpu.make_async_copy(ys_hbm.at[d_ref[0, n]], dst.at[pl.ds(n * LANE_CHUNKS, LANE_CHUNKS), :],
                                  row_sem.at[s]).start(priority=n % 2)

    @pl.when(i == 0)
    def _():
        issue(dcur_ref, 0)

    @pl.when(i + 1 < pl.num_programs(0))
    def _():
        issue(dnext_ref, 1 - slot)

    pltpu.make_async_copy(slot_rows(slot), slot_rows(slot), row_sem.at[slot]).wait()
    g = g_ref[...]
    y = None
    for kk in range(TOP_K):
        rows = _from_row_tiles(ybuf, (slot * n_rows + kk * tm) * LANE_CHUNKS, tm)
        y = g[:, kk:kk + 1] * rows if y is None else y + g[:, kk:kk + 1] * rows
    o_ref[...] = x1_ref[...] + gt_ref[...] * _rms(y, np_ref[...])


def _combine(dest_tiles, ys_tiles, gates_t, x1, gt, npost, tm, tiles_per_seq, tile0):
    n_tiles = dest_tiles.shape[0]
    t = n_tiles * tm
    smem_tile = lambda f: pl.BlockSpec((None, 1, TOP_K * tm), f, memory_space=pltpu.SMEM)
    return pl.pallas_call(
        functools.partial(_combine_kernel, tm=tm),
        out_shape=jax.ShapeDtypeStruct((t, D_MODEL), F32),
        grid=(n_tiles,),
        in_specs=[smem_tile(lambda i: (i, 0, 0)),
                  smem_tile(lambda i: (jnp.minimum(i + 1, n_tiles - 1), 0, 0)),
                  pl.BlockSpec(memory_space=pl.ANY),
                  pl.BlockSpec((tm, TOP_K), lambda i: (i + tile0, 0)),
                  pl.BlockSpec((tm, D_MODEL), lambda i: (i + tile0, 0)),
                  _mod_spec(gt, tm, tiles_per_seq),
                  pl.BlockSpec((1, D_MODEL), lambda i: (0, 0))],
        out_specs=pl.BlockSpec((tm, D_MODEL), lambda i: (i, 0)),
        scratch_shapes=[pltpu.VMEM((2 * TOP_K * tm * LANE_CHUNKS, LANES), F32),
                        pltpu.SemaphoreType.DMA((2,))],
        compiler_params=_cparams(("arbitrary",)),
    )(dest_tiles, dest_tiles, ys_tiles, gates_t, x1, gt, npost)


def _expert_layout(counts, n_blocks):
    padded = (counts + MOE_BM - 1) // MOE_BM * MOE_BM
    pad_ends = jnp.cumsum(padded)
    pad_starts = pad_ends - padded
    block_start = jnp.arange(n_blocks, dtype=I32) * MOE_BM
    block_e = jnp.minimum(jnp.sum((pad_ends[None, :] <= block_start[:, None]).astype(I32), axis=1), N_EXPERTS - 1)
    n_used = (pad_ends[-1:] // MOE_BM).astype(I32)
    e_ids = jnp.arange(N_EXPERTS, dtype=I32)
    rows_end = jnp.sum(jnp.where(block_e[:, None] == e_ids[None, :], (pad_starts + counts)[None, :], 0), axis=1)
    block_half = (rows_end - block_start <= MOE_BM // 2).astype(I32)
    later = (counts > 0)[None, :] & (e_ids[None, :] > e_ids[:, None])
    next_e = jnp.min(jnp.where(later, e_ids[None, :], N_EXPERTS), axis=1)
    next_e = jnp.where(next_e < N_EXPERTS, next_e, -1).astype(I32)
    return pad_starts, pad_starts + counts, pad_ends, block_e, block_half, next_e, n_used


def _slots(idx, rank, pad_starts):
    onehot = idx[:, :, None] == jnp.arange(N_EXPERTS, dtype=I32)[None, None, :]
    return rank + jnp.sum(jnp.where(onehot, pad_starts[None, None, :], 0), axis=-1)


def _tile_major(a, tm):
    t = a.shape[1]
    return a.reshape(TOP_K, t // tm, tm).transpose(1, 0, 2).reshape(t // tm, 1, TOP_K * tm)


def _block_diag(wb):
    n, d, _ = wb.shape
    eye = jnp.eye(n, dtype=wb.dtype)
    return (wb[:, :, None, :] * eye[:, None, :, None]).reshape(n * d, n * d)


def _pick_tile(n, prefs):
    for p in prefs:
        if n % p == 0:
            return p
    raise ValueError(f"no tile for {n}")


def kernel(x_prompt, x_sample, state_rglru_conv, state_rglru_h, state_hgrn_S, c_prompt, c_sample, ada_w, ada_b, norm_pre_mix, norm_post_mix, norm_pre_ffn, norm_post_ffn, w_in, conv_w, conv_b, lru_wa, lru_ba, lru_wi, lru_bi, lru_lambda, hg_lb, hg_norm, w_out, router_w, router_b, w_gu, b_gu, w_down, b_down):
    assert ada_w.shape[0] == 1, "single-layer trunk"
    bp, lp, _ = x_prompt.shape
    bs, ls, _ = x_sample.shape
    tp, ts = bp * lp, bs * ls
    assert CONV_WIDTH - 1 <= ls <= SUBLANES

    w_in_bf = w_in[0].astype(BF16)
    w_out_bf = w_out[0].astype(BF16)
    wg = jnp.concatenate([_block_diag(lru_wa[0]), _block_diag(lru_wi[0])], axis=1).astype(BF16)
    bg = jnp.concatenate([lru_ba[0], lru_bi[0]])[None, :]
    lbs = jnp.cumsum(jax.nn.softmax(hg_lb.astype(F32), axis=0), axis=0)[0][None, :]
    mw = (conv_w[0], conv_b[0][None, :], wg, bg, lru_lambda[0][None, :], lbs, hg_norm[0][None, :])
    row = lambda p: p[0][None, :]

    n_c = bp + bs
    n_c_pad = -(-n_c // SUBLANES) * SUBLANES
    c_all = jnp.concatenate([c_prompt, c_sample, jnp.zeros((n_c_pad - n_c, D_MODEL), F32)], axis=0)
    mod = _ada(c_all, ada_w[0], ada_b[0][None, :])
    mod_p = [m[:, None, :] for m in jnp.split(mod[:bp], 6, axis=-1)]
    mod_s = [jnp.repeat(m, ls, axis=0) for m in jnp.split(mod[bp:n_c], 6, axis=-1)]

    xp = x_prompt.reshape(tp, D_MODEL)
    xs = x_sample.reshape(ts, D_MODEL)
    tm_s = _pick_tile(ts, (256, 128, 64, 32, 16, 8))

    tc = _pick_tile(lp // 2, (256, 128, 64))
    m_p, conv_p, h_p, s_p = _mixer_prompt(xp, mod_p[1], mod_p[0], row(norm_pre_mix), w_in_bf, bp, lp, mw, tc)
    proj_s = _inproj(xs, mod_s[1], mod_s[0], row(norm_pre_mix), w_in_bf, tm_s, 1)

    proj_s_pad = jnp.pad(proj_s.reshape(bs, ls, IN_COLS), ((0, 0), (0, SUBLANES - ls), (0, 0)))
    cst_hdr = jnp.pad(state_rglru_conv[0], ((0, 0), (CONV_HDR - (CONV_WIDTH - 1), 0), (0, 0)))
    m_s, conv_s, h_s, s_s = _mixer_sample(proj_s_pad, cst_hdr, state_rglru_h[0][:, None, :],
                                          state_hgrn_S[0], mw, ls)

    tmq = _pick_tile(math.gcd(lp, ts), (512, 256, 128, 64, 32, 16, 8))
    x1, hf, idx, gate, rank, counts = _post(
        m_p, m_s, xp, xs, (mod_p[2], mod_p[4], mod_p[3]), (mod_s[2], mod_s[4], mod_s[3]),
        row(norm_post_mix), row(norm_pre_ffn), w_out_bf, router_w[0].T, router_b[0][:, None], tmq, lp // tmq)

    n_blocks = -(-(TOP_K * (tp + ts)) // MOE_BM) + N_EXPERTS
    n_slots = n_blocks * MOE_BM
    pad_starts, fill_lo, pad_ends, block_e, block_half, next_e, n_used = _expert_layout(
        counts[:, 0].astype(I32), n_blocks)
    fill_hi = pad_ends.at[N_EXPERTS - 1].set(n_slots)
    dest = _slots(idx, rank, pad_starts)
    tmd = _pick_tile(math.gcd(lp, ts), (512, 256, 128, 64, 32, 16, 8))
    tmc = _pick_tile(math.gcd(lp, ts), (256, 128, 64, 32, 16, 8))
    xs_rows = _dispatch(fill_lo, fill_hi, _tile_major(dest, tmd), hf, n_slots, tmd)
    ys = _experts(block_e, block_half, next_e, n_used, xs_rows.reshape(n_slots * LANE_CHUNKS, LANES),
                  w_gu[0], b_gu[0][:, None, :], w_down[0], b_down[0][:, None, :])
    ys_tiles = ys.reshape(-1, LANE_CHUNKS, LANES)

    gates_t = gate.T
    ctiles = _tile_major(dest, tmc)
    y_p = _combine(ctiles[:tp // tmc], ys_tiles, gates_t, x1, mod_p[5], row(norm_post_ffn), tmc, lp // tmc, 0)
    y_s = _combine(ctiles[tp // tmc:], ys_tiles, gates_t, x1, mod_s[5], row(norm_post_ffn), tmc, 1, tp // tmc)

    return (y_p.reshape(bp, lp, D_MODEL), y_s.reshape(bs, ls, D_MODEL),
            conv_p[None], h_p.reshape(1, bp, LRU_WIDTH), s_p[None],
            conv_s[None], h_s.reshape(1, bs, LRU_WIDTH), s_s[None])
```

```python
import functools
import math

import jax
import jax.numpy as jnp
from jax import lax
from jax.experimental import pallas as pl
from jax.experimental.pallas import tpu as pltpu

F32 = jnp.float32
BF16 = jnp.bfloat16
I32 = jnp.int32

D_MODEL = 1024
LRU_WIDTH = 512
CONV_WIDTH = 4
LRU_C = 8.0
HG_WIDTH = 512
HG_HEAD_DIM = 128
HG_HEADS = 4
IN_COLS = 2 * LRU_WIDTH + 4 * HG_WIDTH
N_EXPERTS = 32
TOP_K = 4
D_EXPERT = 1024
SWIGLU_LIMIT = 7.0
SWIGLU_ALPHA = 1.702
RMS_EPS = 1e-6

C_XL, C_YL, C_Q, C_F, C_V, C_G = 0, 512, 1024, 1536, 2048, 2560

SUBLANES = 8
LANES = 128
LANE_CHUNKS = D_MODEL // LANES
CONV_HDR = SUBLANES
HG_CHUNK = 64
HG_MAX_EXPONENT = 60.0
MOE_BM = 512
ZERO_ROWS = 64
VMEM_LIMIT = 56 * 1024 * 1024

_NT = (((1,), (1,)), ((), ()))
_TN = (((0,), (0,)), ((), ()))


def _cparams(sem):
    return pltpu.CompilerParams(dimension_semantics=sem, vmem_limit_bytes=VMEM_LIMIT)


def _rms(x, g):
    return x * lax.rsqrt(jnp.mean(x * x, axis=-1, keepdims=True) + RMS_EPS) * g


def _gelu_tanh(x):
    c = math.sqrt(2.0 / math.pi)
    return 0.5 * x * (1.0 + jnp.tanh(c * (x + 0.044715 * (x * x * x))))


def _ada_kernel(c_ref, w_ref, b_ref, o_ref):
    c = c_ref[...]
    s = c * jax.nn.sigmoid(c)
    o_ref[...] = jnp.dot(s.astype(BF16), w_ref[...].astype(BF16), preferred_element_type=F32) + b_ref[...]


def _ada(c_all, ada_w, ada_b):
    n = c_all.shape[0]
    tn = 1024
    return pl.pallas_call(
        _ada_kernel,
        out_shape=jax.ShapeDtypeStruct((n, 6 * D_MODEL), F32),
        grid=(6 * D_MODEL // tn,),
        in_specs=[pl.BlockSpec((n, D_MODEL), lambda j: (0, 0)),
                  pl.BlockSpec((D_MODEL, tn), lambda j: (0, j)),
                  pl.BlockSpec((1, tn), lambda j: (0, j))],
        out_specs=pl.BlockSpec((n, tn), lambda j: (0, j)),
        compiler_params=_cparams(("arbitrary",)),
    )(c_all, ada_w, ada_b)


def _inproj_kernel(x_ref, sc_ref, sh_ref, g_ref, w_ref, o_ref):
    h = _rms(x_ref[...], g_ref[...]) * (1.0 + sc_ref[...]) + sh_ref[...]
    o_ref[...] = jnp.dot(h.astype(BF16), w_ref[...], preferred_element_type=F32)


def _mod_spec(mod, tm, tiles_per_seq):
    if mod.ndim == 3:
        return pl.BlockSpec((None, 1, D_MODEL), lambda i: (i // tiles_per_seq, 0, 0))
    return pl.BlockSpec((tm, D_MODEL), lambda i: (i, 0))


def _inproj(x, sc, sh, g, w_bf, tm, tiles_per_seq):
    t = x.shape[0]
    return pl.pallas_call(
        _inproj_kernel,
        out_shape=jax.ShapeDtypeStruct((t, IN_COLS), F32),
        grid=(t // tm,),
        in_specs=[pl.BlockSpec((tm, D_MODEL), lambda i: (i, 0)),
                  _mod_spec(sc, tm, tiles_per_seq), _mod_spec(sh, tm, tiles_per_seq),
                  pl.BlockSpec((1, D_MODEL), lambda i: (0, 0)),
                  pl.BlockSpec((D_MODEL, IN_COLS), lambda i: (0, 0))],
        out_specs=pl.BlockSpec((tm, IN_COLS), lambda i: (i, 0)),
        compiler_params=_cparams(("arbitrary",)),
    )(x, sc, sh, g, w_bf)


def _sigmoid(x):
    return 0.5 * jnp.tanh(0.5 * x) + 0.5


def _group_rows(x):
    rows, w = x.shape
    return x.reshape(rows // SUBLANES, SUBLANES, w)


def _scan_rows(a, u, h0):
    a3, u3 = _group_rows(a), _group_rows(u)
    r3 = lax.broadcasted_iota(I32, a3.shape, 1)
    s = 1
    while s < SUBLANES:
        keep = r3 >= s
        a_sh = jnp.where(keep, pltpu.roll(a3, s, 1), 1.0)
        u_sh = jnp.where(keep, pltpu.roll(u3, s, 1), 0.0)
        u3 = a3 * u_sh + u3
        a3 = a3 * a_sh
        s *= 2
    out, h = [], h0
    for i in range(a3.shape[0]):
        hg = u3[i] + a3[i] * h
        h = hg[SUBLANES - 1:SUBLANES, :]
        out.append(hg)
    return jnp.concatenate(out, axis=0), h


def _chunk_cumsum(x, chunk):
    x3 = _group_rows(x)
    r3 = lax.broadcasted_iota(I32, x3.shape, 1)
    s = 1
    while s < SUBLANES:
        x3 = x3 + jnp.where(r3 >= s, pltpu.roll(x3, s, 1), 0.0)
        s *= 2
    out, carry = [], None
    for i in range(x3.shape[0]):
        cur = x3[i] if i % (chunk // SUBLANES) == 0 else x3[i] + carry
        carry = cur[SUBLANES - 1:SUBLANES, :]
        out.append(cur)
    return jnp.concatenate(out, axis=0)


def _lru_gates(xc, wg_ref, bg_ref, lam_ref):
    gates = jnp.dot(xc.astype(BF16), wg_ref[...], preferred_element_type=F32) + bg_ref[...]
    r = _sigmoid(gates[:, :LRU_WIDTH])
    ig = _sigmoid(gates[:, LRU_WIDTH:])
    z = -lam_ref[...]
    softplus = jnp.maximum(z, 0.0) + jnp.log1p(jnp.exp(-jnp.abs(z)))
    log_a = -LRU_C * r * softplus
    a = jnp.exp(log_a)
    th = jnp.tanh(log_a)
    mult2 = -2.0 * th / (1.0 - th)
    return a, jnp.sqrt(mult2), ig


def _conv4(xbuf, xl, cw_ref, cb_ref, rows):
    cw = cw_ref[...]
    h = CONV_HDR
    return (cb_ref[...] + cw[0:1] * xbuf[h - 3:h - 3 + rows, :] + cw[1:2] * xbuf[h - 2:h - 2 + rows, :]
            + cw[2:3] * xbuf[h - 1:h - 1 + rows, :] + cw[3:4] * xl)


def _conv4_rolled(xl, prev, cw_ref, cb_ref):
    cw = cw_ref[...]
    row8 = lax.broadcasted_iota(I32, prev.shape, 0)
    acc = cb_ref[...] + cw[CONV_WIDTH - 1:CONV_WIDTH] * xl
    for s in range(1, CONV_WIDTH):
        sh = pltpu.roll(xl, s, 0)
        top = jnp.where(row8 < s, pltpu.roll(prev, s, 0), sh[0:SUBLANES, :])
        sh = jnp.concatenate([top, sh[SUBLANES:, :]], axis=0)
        acc = acc + cw[CONV_WIDTH - 1 - s:CONV_WIDTH - s] * sh
    return acc


def _head_norm_gate(o, gate, hgn):
    return _rms(o, hgn) * _sigmoid(gate)


def _hgrn_window(q, k, v, b, state, steps, state_is_transposed):
    row = lax.broadcasted_iota(I32, q.shape, 0)
    valid = row < steps
    b_end = b[steps - 1:steps, :]
    q_in = jnp.where(valid, q * jnp.exp(b), 0.0).astype(BF16)
    if state_is_transposed:
        o = lax.dot_general(q_in, state.astype(BF16), _NT, preferred_element_type=F32)
    else:
        o = jnp.dot(q_in, state.astype(BF16), preferred_element_type=F32)
    for d in range(steps):
        ok = valid & (row >= d)
        k_d = k if d == 0 else pltpu.roll(k, d, 0)
        v_d = v if d == 0 else pltpu.roll(v, d, 0)
        b_d = b if d == 0 else pltpu.roll(b, d, 0)
        decay = jnp.exp(jnp.where(ok, b - b_d, 0.0))
        wgt = jnp.sum(jnp.where(ok, q * k_d * decay, 0.0), axis=-1, keepdims=True)
        o = o + wgt * v_d
    k_out = jnp.where(valid, k * jnp.exp(jnp.where(valid, b_end - b, 0.0)), 0.0).astype(BF16)
    v_ok = jnp.where(valid, v, 0.0).astype(BF16)
    keep = jnp.exp(b_end)
    if state_is_transposed:
        s_new = state * keep + lax.dot_general(v_ok, k_out, _TN, preferred_element_type=F32)
    else:
        keep_col = jnp.broadcast_to(keep, q.shape).T[:, 0:1]
        s_new = state * keep_col + lax.dot_general(k_out, v_ok, _TN, preferred_element_type=F32)
    return o, s_new


def _tile_cumsum(x):
    row = lax.broadcasted_iota(I32, x.shape, 0)
    s = 1
    while s < SUBLANES:
        x = x + jnp.where(row >= s, pltpu.roll(x, s, 0), 0.0)
        s *= 2
    return x


def _mix_tile(proj_ref, cw_ref, cb_ref, wg_ref, bg_ref, lam_ref, lb_ref, hgn_ref, m_ref, xprev, hcar, st, st0, obuf,
              seq_start, tc, side_work):
    w = LRU_WIDTH
    side_work = iter(side_work)

    next(side_work)()
    xl = proj_ref[:, C_XL:C_XL + w]
    xc = _conv4_rolled(xl, xprev[...], cw_ref, cb_ref)
    xprev[...] = xl[tc - SUBLANES:tc, :]
    a, mult, ig = _lru_gates(xc, wg_ref, bg_ref, lam_ref)
    row = lax.broadcasted_iota(I32, (tc, w), 0)
    if seq_start is not None:
        mult = jnp.where((row == 0) & seq_start, 1.0, mult)
    h, h_last = _scan_rows(a, mult * ig * xc, hcar[0:1, :])
    hcar[0:1, :] = h_last
    m_ref[:, 0:w] = (h * _gelu_tanh(proj_ref[:, C_YL:C_YL + w])).astype(BF16)

    next(side_work)()
    lb = lb_ref[...]
    f = lb + (1.0 - lb) * _sigmoid(proj_ref[:, C_F:C_F + w])
    k = 1.0 - f
    b = _chunk_cumsum(jnp.log(f), HG_CHUNK)
    q = proj_ref[:, C_Q:C_Q + w]
    v = proj_ref[:, C_V:C_V + w]
    mid = HG_CHUNK // 2
    st0[...] = st[...]
    tri = (lax.broadcasted_iota(I32, (HG_CHUNK, HG_CHUNK), 0)
           >= lax.broadcasted_iota(I32, (HG_CHUNK, HG_CHUNK), 1))
    states = [st[hd] for hd in range(HG_HEADS)]
    for c in range(tc // HG_CHUNK):
        next(side_work)()
        rs = slice(c * HG_CHUNK, (c + 1) * HG_CHUNK)
        for hd in range(HG_HEADS):
            cs = slice(hd * HG_HEAD_DIM, (hd + 1) * HG_HEAD_DIM)
            s_t = states[hd]
            bc, qc, kc = b[rs, cs], q[rs, cs], k[rs, cs]
            vb = v[rs, cs].astype(BF16)
            b_mid = bc[mid - 1:mid, :]
            b_end = bc[HG_CHUNK - 1:HG_CHUNK, :]
            qp = (qc * jnp.exp(bc - b_mid)).astype(BF16)
            kp = (kc * jnp.exp(b_mid - bc)).astype(BF16)
            att = lax.dot_general(qp, kp, _NT, preferred_element_type=F32)
            att = jnp.where(tri, att, 0.0)
            q_in = (qc * jnp.exp(bc)).astype(BF16)
            o = (jnp.dot(att.astype(BF16), vb, preferred_element_type=F32)
                 + lax.dot_general(q_in, s_t.astype(BF16), _NT, preferred_element_type=F32))
            k_out = (kc * jnp.exp(b_end - bc)).astype(BF16)
            states[hd] = s_t * jnp.exp(b_end) + lax.dot_general(vb, k_out, _TN, preferred_element_type=F32)
            obuf[rs, cs] = o
    for hd in range(HG_HEADS):
        st[hd] = states[hd]

    half_decay = None
    for c in range(tc // HG_CHUNK):
        b_mid = b[c * HG_CHUNK + mid - 1:c * HG_CHUNK + mid, :]
        b_end = b[(c + 1) * HG_CHUNK - 1:(c + 1) * HG_CHUNK, :]
        worst = jnp.maximum(-b_mid, b_mid - b_end)
        half_decay = worst if half_decay is None else jnp.maximum(half_decay, worst)

    @pl.when(jnp.logical_not(jnp.max(half_decay) < HG_MAX_EXPONENT))
    def _():
        st[...] = st0[...]

        def window(gi, carry):
            rows = pl.ds(pl.multiple_of(gi * SUBLANES, SUBLANES), SUBLANES)
            f8 = lb + (1.0 - lb) * _sigmoid(proj_ref[rows, C_F:C_F + w])
            b8 = _tile_cumsum(jnp.log(f8))
            k8 = 1.0 - f8
            q8 = proj_ref[rows, C_Q:C_Q + w]
            v8 = proj_ref[rows, C_V:C_V + w]
            for hd in range(HG_HEADS):
                cs = slice(hd * HG_HEAD_DIM, (hd + 1) * HG_HEAD_DIM)
                o, s_new = _hgrn_window(q8[:, cs], k8[:, cs], v8[:, cs], b8[:, cs], st[hd], SUBLANES, True)
                st[hd] = s_new
                obuf[rows, cs] = o
            return carry
        lax.fori_loop(0, tc // SUBLANES, window, 0)

    hgn = hgn_ref[...]
    for hd in range(HG_HEADS):
        cs = slice(hd * HG_HEAD_DIM, (hd + 1) * HG_HEAD_DIM)
        gate = proj_ref[:, C_G + hd * HG_HEAD_DIM:C_G + (hd + 1) * HG_HEAD_DIM]
        m_ref[:, w + hd * HG_HEAD_DIM:w + (hd + 1) * HG_HEAD_DIM] = (
            _head_norm_gate(obuf[:, cs], gate, hgn).astype(BF16))


def _mixer_prompt_kernel(x_ref, xn_ref, sc_ref, sh_ref, scn_ref, shn_ref, g_ref, win_ref, cw_ref, cb_ref, wg_ref,
                         bg_ref, lam_ref, lb_ref, hgn_ref, m_ref, conv_ref, h_ref, s_ref,
                         proj_a, proj_b, hn_buf, xprev, hcar, st, st0, obuf, *, tc):
    j = pl.program_id(1)
    n_phases = 2 + tc // HG_CHUNK
    cols = IN_COLS // n_phases

    def project(x, sc, sh, out_ref):
        def chunk(c):
            if c == 0:
                hn_buf[...] = (_rms(x(), g_ref[...]) * (1.0 + sc[...]) + sh[...]).astype(BF16)
            out_ref[:, c * cols:(c + 1) * cols] = jnp.dot(hn_buf[...], win_ref[:, c * cols:(c + 1) * cols],
                                                          preferred_element_type=F32)
        return [functools.partial(chunk, c) for c in range(n_phases)]

    def mix(proj_ref, first_row, seq_start, side_work):
        _mix_tile(proj_ref, cw_ref, cb_ref, wg_ref, bg_ref, lam_ref, lb_ref, hgn_ref,
                  m_ref.at[pl.ds(first_row, tc), :], xprev, hcar, st, st0, obuf, seq_start, tc, side_work)

    @pl.when((pl.program_id(0) == 0) & (j == 0))
    def _():
        for thunk in project(lambda: x_ref[0:tc, :], sc_ref, sh_ref, proj_a):
            thunk()

    @pl.when(j == 0)
    def _():
        xprev[...] = jnp.zeros_like(xprev)
        hcar[...] = jnp.zeros_like(hcar)
        st[...] = jnp.zeros_like(st)

    mix(proj_a, 0, j == 0, project(lambda: x_ref[tc:2 * tc, :], sc_ref, sh_ref, proj_b))
    mix(proj_b, tc, None, project(lambda: xn_ref[...], scn_ref, shn_ref, proj_a))

    @pl.when(j == pl.num_programs(1) - 1)
    def _():
        conv_ref[...] = xprev[SUBLANES - (CONV_WIDTH - 1):SUBLANES, :]
        h_ref[...] = hcar[0:1, :]
        for hd in range(HG_HEADS):
            s_ref[hd] = st[hd].T


def _mixer_prompt(x, sc, sh, g, w_in_bf, nb, seq, mw, tc):
    cw, cb, wg, bg, lam, lb, hgn = mw
    nj = seq // (2 * tc)
    last_tile = nb * seq // tc - 1
    next_tile = lambda b, j: jnp.minimum(2 * (b * nj + j) + 2, last_tile)
    const = lambda shape: pl.BlockSpec(shape, lambda b, j: tuple(0 for _ in shape))
    per_seq = lambda: pl.BlockSpec((None, 1, D_MODEL), lambda b, j: (b, 0, 0))
    next_seq = lambda: pl.BlockSpec((None, 1, D_MODEL), lambda b, j: (next_tile(b, j) // (2 * nj), 0, 0))
    return pl.pallas_call(
        functools.partial(_mixer_prompt_kernel, tc=tc),
        out_shape=(jax.ShapeDtypeStruct((nb * seq, D_MODEL), BF16),
                   jax.ShapeDtypeStruct((nb, CONV_WIDTH - 1, LRU_WIDTH), F32),
                   jax.ShapeDtypeStruct((nb, 1, LRU_WIDTH), F32),
                   jax.ShapeDtypeStruct((nb, HG_HEADS, HG_HEAD_DIM, HG_HEAD_DIM), F32)),
        grid=(nb, nj),
        in_specs=[pl.BlockSpec((2 * tc, D_MODEL), lambda b, j: (b * nj + j, 0)),
                  pl.BlockSpec((tc, D_MODEL), lambda b, j: (next_tile(b, j), 0)),
                  per_seq(), per_seq(), next_seq(), next_seq(), const((1, D_MODEL)), const((D_MODEL, IN_COLS)),
                  const((CONV_WIDTH, LRU_WIDTH)), const((1, LRU_WIDTH)),
                  const((LRU_WIDTH, 2 * LRU_WIDTH)), const((1, 2 * LRU_WIDTH)),
                  const((1, LRU_WIDTH)), const((1, HG_WIDTH)), const((1, HG_HEAD_DIM))],
        out_specs=(pl.BlockSpec((2 * tc, D_MODEL), lambda b, j: (b * nj + j, 0)),
                   pl.BlockSpec((None, CONV_WIDTH - 1, LRU_WIDTH), lambda b, j: (b, 0, 0)),
                   pl.BlockSpec((None, 1, LRU_WIDTH), lambda b, j: (b, 0, 0)),
                   pl.BlockSpec((None, HG_HEADS, HG_HEAD_DIM, HG_HEAD_DIM), lambda b, j: (b, 0, 0, 0))),
        scratch_shapes=[pltpu.VMEM((tc, IN_COLS), F32),
                        pltpu.VMEM((tc, IN_COLS), F32),
                        pltpu.VMEM((tc, D_MODEL), BF16),
                        pltpu.VMEM((SUBLANES, LRU_WIDTH), F32),
                        pltpu.VMEM((SUBLANES, LRU_WIDTH), F32),
                        pltpu.VMEM((HG_HEADS, HG_HEAD_DIM, HG_HEAD_DIM), F32),
                        pltpu.VMEM((HG_HEADS, HG_HEAD_DIM, HG_HEAD_DIM), F32),
                        pltpu.VMEM((tc, HG_WIDTH), F32)],
        compiler_params=_cparams(("arbitrary", "arbitrary")),
    )(x, x, sc, sh, sc, sh, g, w_in_bf, cw, cb, wg, bg, lam, lb, hgn)


def _mixer_sample_kernel(proj_ref, cst_ref, h0_ref, s0_ref, cw_ref, cb_ref, wg_ref, bg_ref, lam_ref, lb_ref,
                         hgn_ref, m_ref, conv_ref, h_ref, s_ref, pbuf, xbuf, *, steps):
    for bi in range(pbuf.shape[0]):
        seq_rows = slice(bi * steps, (bi + 1) * steps)
        pbuf[bi, 0:steps, :] = proj_ref[seq_rows, :]
        pbuf[bi, steps:SUBLANES, :] = jnp.zeros((SUBLANES - steps, IN_COLS), F32)
        _mixer_sample_one(pbuf.at[bi], cst_ref.at[bi], h0_ref.at[bi], s0_ref.at[bi], cw_ref, cb_ref, wg_ref,
                          bg_ref, lam_ref, lb_ref, hgn_ref, m_ref, seq_rows, conv_ref.at[bi], h_ref.at[bi],
                          s_ref.at[bi], xbuf.at[bi], steps)


def _mixer_sample_one(proj_ref, cst_ref, h0_ref, s0_ref, cw_ref, cb_ref, wg_ref, bg_ref, lam_ref, lb_ref,
                      hgn_ref, m_ref, m_rows, conv_ref, h_ref, s_ref, xbuf, steps):
    w = LRU_WIDTH
    rows = SUBLANES
    row = lax.broadcasted_iota(I32, (rows, w), 0)

    xl = proj_ref[:, C_XL:C_XL + w]
    xbuf[0:CONV_HDR, :] = cst_ref[...]
    xbuf[CONV_HDR:CONV_HDR + rows, :] = xl
    xc = _conv4(xbuf, xl, cw_ref, cb_ref, rows)
    conv_ref[...] = xbuf[CONV_HDR + steps - 3:CONV_HDR + steps, :]
    a, mult, ig = _lru_gates(xc, wg_ref, bg_ref, lam_ref)
    u = mult * ig * xc
    h = h0_ref[...]
    hs = jnp.zeros((rows, w), F32)
    for t in range(steps):
        h = a[t:t + 1, :] * h + u[t:t + 1, :]
        hs = jnp.where(row == t, h, hs)
    h_ref[...] = h
    m_ref[m_rows, 0:w] = (hs * _gelu_tanh(proj_ref[:, C_YL:C_YL + w]))[0:steps, :]

    lb = lb_ref[...]
    f = lb + (1.0 - lb) * _sigmoid(proj_ref[:, C_F:C_F + w])
    k = 1.0 - f
    b = _tile_cumsum(jnp.log(f))
    q = proj_ref[:, C_Q:C_Q + w]
    v = proj_ref[:, C_V:C_V + w]
    hgn = hgn_ref[...]
    for hd in range(HG_HEADS):
        cs = slice(hd * HG_HEAD_DIM, (hd + 1) * HG_HEAD_DIM)
        o, s_new = _hgrn_window(q[:, cs], k[:, cs], v[:, cs], b[:, cs], s0_ref[hd], steps, False)
        s_ref[hd] = s_new
        gate = proj_ref[:, C_G + hd * HG_HEAD_DIM:C_G + (hd + 1) * HG_HEAD_DIM]
        m_ref[m_rows, w + hd * HG_HEAD_DIM:w + (hd + 1) * HG_HEAD_DIM] = _head_norm_gate(o, gate, hgn)[0:steps, :]


def _mixer_sample(proj, cst_hdr, h0, s0, mw, steps):
    cw, cb, wg, bg, lam, lb, hgn = mw
    nb = proj.shape[0] // steps
    per_step = _pick_tile(nb, (8, 4, 2, 1))
    const = lambda shape: pl.BlockSpec(shape, lambda b: tuple(0 for _ in shape))
    per_b = lambda shape: pl.BlockSpec((per_step,) + shape, lambda b: (b,) + tuple(0 for _ in shape))
    return pl.pallas_call(
        functools.partial(_mixer_sample_kernel, steps=steps),
        out_shape=(jax.ShapeDtypeStruct((nb * steps, D_MODEL), F32),
                   jax.ShapeDtypeStruct((nb, CONV_WIDTH - 1, LRU_WIDTH), F32),
                   jax.ShapeDtypeStruct((nb, 1, LRU_WIDTH), F32),
                   jax.ShapeDtypeStruct((nb, HG_HEADS, HG_HEAD_DIM, HG_HEAD_DIM), F32)),
        grid=(nb // per_step,),
        in_specs=[pl.BlockSpec((per_step * steps, IN_COLS), lambda b: (b, 0)),
                  per_b((CONV_HDR, LRU_WIDTH)), per_b((1, LRU_WIDTH)),
                  per_b((HG_HEADS, HG_HEAD_DIM, HG_HEAD_DIM)),
                  const((CONV_WIDTH, LRU_WIDTH)), const((1, LRU_WIDTH)),
                  const((LRU_WIDTH, 2 * LRU_WIDTH)), const((1, 2 * LRU_WIDTH)),
                  const((1, LRU_WIDTH)), const((1, HG_WIDTH)), const((1, HG_HEAD_DIM))],
        out_specs=(pl.BlockSpec((per_step * steps, D_MODEL), lambda b: (b, 0)),
                   per_b((CONV_WIDTH - 1, LRU_WIDTH)), per_b((1, LRU_WIDTH)),
                   per_b((HG_HEADS, HG_HEAD_DIM, HG_HEAD_DIM))),
        scratch_shapes=[pltpu.VMEM((per_step, SUBLANES, IN_COLS), F32),
                        pltpu.VMEM((per_step, CONV_HDR + SUBLANES, LRU_WIDTH), F32)],
        compiler_params=_cparams(("arbitrary",)),
    )(proj, cst_hdr, h0, s0, cw, cb, wg, bg, lam, lb, hgn)


def _to_row_tiles(ref, val, rows):
    for c in range(LANE_CHUNKS):
        ref[pl.ds(c, rows, stride=LANE_CHUNKS), :] = val[:, c * LANES:(c + 1) * LANES]


def _from_row_tiles(ref, base, rows):
    return jnp.concatenate(
        [ref[pl.ds(base + c, rows, stride=LANE_CHUNKS), :] for c in range(LANE_CHUNKS)], axis=-1)


def _post_kernel(mp_ref, ms_ref, xp_ref, xs_ref, gtmp_ref, scfp_ref, shfp_ref, gtms_ref, scfs_ref, shfs_ref,
                 npost_ref, npre_ref, wout_ref, rwt_ref, rb_ref,
                 x1_ref, hf_ref, idx_ref, gate_ref, rank_ref, cnt_out_ref, cnt, *, n_p):
    i = pl.program_id(0)
    tm = xp_ref.shape[0]

    @pl.when(i == 0)
    def _():
        cnt[...] = jnp.zeros_like(cnt)

    def run(m_ref, x_ref, gtm_ref, scf_ref, shf_ref):
        mix = jnp.dot(m_ref[...].astype(BF16), wout_ref[...], preferred_element_type=F32)
        x1 = x_ref[...] + gtm_ref[...] * _rms(mix, npost_ref[...])
        x1_ref[...] = x1
        hf = _rms(x1, npre_ref[...]) * (1.0 + scf_ref[...]) + shf_ref[...]
        _to_row_tiles(hf_ref, hf, tm)
        logits = lax.dot_general(rwt_ref[...], hf, _NT, precision=lax.Precision.HIGHEST,
                                 preferred_element_type=F32) + rb_ref[...]
        e_iota = lax.broadcasted_iota(I32, logits.shape, 0)
        vals, idxs = [], []
        for _ in range(TOP_K):
            mx = jnp.max(logits, axis=0, keepdims=True)
            ix = jnp.min(jnp.where(logits == mx, e_iota, N_EXPERTS), axis=0, keepdims=True)
            vals.append(mx)
            idxs.append(ix)
            logits = jnp.where(e_iota == ix, -jnp.inf, logits)
        exps = [jnp.exp(vv - vals[0]) for vv in vals]
        den = exps[0] + exps[1] + exps[2] + exps[3]
        earlier = (lax.broadcasted_iota(I32, (tm, tm), 0) < lax.broadcasted_iota(I32, (tm, tm), 1))
        earlier = jnp.where(earlier, 1.0, 0.0).astype(BF16)
        base = cnt[...]
        for kk in range(TOP_K):
            idx_ref[kk:kk + 1, :] = idxs[kk]
            gate_ref[kk:kk + 1, :] = exps[kk] / den
            hit = e_iota == idxs[kk]
            onehot = jnp.where(hit, 1.0, 0.0)
            before = jnp.dot(onehot.astype(BF16), earlier, preferred_element_type=F32)
            rank = jnp.sum(jnp.where(hit, before + base, 0.0), axis=0, keepdims=True)
            rank_ref[kk:kk + 1, :] = rank.astype(I32)
            base = base + jnp.sum(onehot, axis=1, keepdims=True)
        cnt[...] = base

    @pl.when(i < n_p)
    def _():
        run(mp_ref, xp_ref, gtmp_ref, scfp_ref, shfp_ref)

    @pl.when(i >= n_p)
    def _():
        run(ms_ref, xs_ref, gtms_ref, scfs_ref, shfs_ref)

    @pl.when(i == pl.num_programs(0) - 1)
    def _():
        cnt_out_ref[...] = cnt[...]


def _post(m_p, m_s, x_p, x_s, mod_p, mod_s, npost, npre, wout_bf, rwt, rb, tm, tiles_per_seq):
    n_p, n_s = x_p.shape[0] // tm, x_s.shape[0] // tm
    t = (n_p + n_s) * tm
    first = lambda i: jnp.minimum(i, n_p - 1)
    second = lambda i: jnp.maximum(i - n_p, 0)
    row = lambda: pl.BlockSpec((1, D_MODEL), lambda i: (0, 0))
    tok_p = lambda: pl.BlockSpec((tm, D_MODEL), lambda i: (first(i), 0))
    tok_s = lambda: pl.BlockSpec((tm, D_MODEL), lambda i: (second(i), 0))
    seq_p = lambda: pl.BlockSpec((None, 1, D_MODEL), lambda i: (first(i) // tiles_per_seq, 0, 0))
    per_k = lambda: pl.BlockSpec((TOP_K, tm), lambda i: (0, i))
    return pl.pallas_call(
        functools.partial(_post_kernel, n_p=n_p),
        out_shape=(jax.ShapeDtypeStruct((t, D_MODEL), F32),
                   jax.ShapeDtypeStruct((t * LANE_CHUNKS, LANES), F32),
                   jax.ShapeDtypeStruct((TOP_K, t), I32), jax.ShapeDtypeStruct((TOP_K, t), F32),
                   jax.ShapeDtypeStruct((TOP_K, t), I32), jax.ShapeDtypeStruct((N_EXPERTS, 1), F32)),
        grid=(n_p + n_s,),
        in_specs=[tok_p(), tok_s(), tok_p(), tok_s(), seq_p(), seq_p(), seq_p(), tok_s(), tok_s(), tok_s(),
                  row(), row(),
                  pl.BlockSpec((D_MODEL, D_MODEL), lambda i: (0, 0)),
                  pl.BlockSpec((N_EXPERTS, D_MODEL), lambda i: (0, 0)),
                  pl.BlockSpec((N_EXPERTS, 1), lambda i: (0, 0))],
        out_specs=(pl.BlockSpec((tm, D_MODEL), lambda i: (i, 0)),
                   pl.BlockSpec((tm * LANE_CHUNKS, LANES), lambda i: (i, 0)),
                   per_k(), per_k(), per_k(), pl.BlockSpec((N_EXPERTS, 1), lambda i: (0, 0))),
        scratch_shapes=[pltpu.VMEM((N_EXPERTS, 1), F32)],
        compiler_params=_cparams(("arbitrary",)),
    )(m_p, m_s, x_p, x_s, *mod_p, *mod_s, npost, npre, wout_bf, rwt, rb)


def _dispatch_kernel(fill_lo_ref, fill_hi_ref, dest_ref, hf_ref, xs_hbm, zeros, row_sem, pad_sem, *, tm):
    i = pl.program_id(0)
    n_rows = TOP_K * tm

    @pl.when(i == 0)
    def _():
        zeros[...] = jnp.zeros_like(zeros)

        def per_expert(e, totals):
            lo, hi = fill_lo_ref[e], fill_hi_ref[e]
            mid = jnp.minimum((lo + ZERO_ROWS - 1) // ZERO_ROWS * ZERO_ROWS, hi)

            def per_row(s, carry):
                pltpu.make_async_copy(zeros.at[0], xs_hbm.at[s], pad_sem).start()
                return carry
            lax.fori_loop(lo, mid, per_row, 0)

            def per_chunk(c, carry):
                pltpu.make_async_copy(zeros, xs_hbm.at[pl.ds(pl.multiple_of(c * ZERO_ROWS, ZERO_ROWS), ZERO_ROWS)],
                                      pad_sem).start()
                return carry
            lax.fori_loop(mid // ZERO_ROWS, hi // ZERO_ROWS, per_chunk, 0)
            return totals[0] + (mid - lo), totals[1] + (hi - mid) // ZERO_ROWS
        n_rows_filled, n_chunks_filled = lax.fori_loop(0, N_EXPERTS, per_expert, (0, 0))

        def drain_row(s, carry):
            pltpu.make_async_copy(zeros.at[0], xs_hbm.at[0], pad_sem).wait()
            return carry
        lax.fori_loop(0, n_rows_filled, drain_row, 0)

        def drain_chunk(c, carry):
            pltpu.make_async_copy(zeros, xs_hbm.at[pl.ds(0, ZERO_ROWS)], pad_sem).wait()
            return carry
        lax.fori_loop(0, n_chunks_filled, drain_chunk, 0)

    for n in range(n_rows):
        pltpu.make_async_copy(hf_ref.at[pl.ds((n % tm) * LANE_CHUNKS, LANE_CHUNKS), :],
                              xs_hbm.at[dest_ref[0, n]], row_sem).start(priority=n % 2)
    pltpu.make_async_copy(xs_hbm.at[pl.ds(0, n_rows)], xs_hbm.at[pl.ds(0, n_rows)], row_sem).wait()


def _dispatch(fill_lo, fill_hi, dest_tiles, hf_rows, n_slots, tm):
    return pl.pallas_call(
        functools.partial(_dispatch_kernel, tm=tm),
        out_shape=jax.ShapeDtypeStruct((n_slots, LANE_CHUNKS, LANES), F32),
        grid_spec=pltpu.PrefetchScalarGridSpec(
            num_scalar_prefetch=2, grid=(dest_tiles.shape[0],),
            in_specs=[pl.BlockSpec((None, 1, TOP_K * tm), lambda i, lo, hi: (i, 0, 0), memory_space=pltpu.SMEM),
                      pl.BlockSpec((tm * LANE_CHUNKS, LANES), lambda i, lo, hi: (i, 0))],
            out_specs=pl.BlockSpec(memory_space=pl.ANY),
            scratch_shapes=[pltpu.VMEM((ZERO_ROWS, LANE_CHUNKS, LANES), F32),
                            pltpu.SemaphoreType.DMA, pltpu.SemaphoreType.DMA]),
        compiler_params=_cparams(("arbitrary",)),
    )(fill_lo, fill_hi, dest_tiles, hf_rows)


def _expert_kernel(be_ref, half_ref, nxt_ref, nu_ref, xs_ref, wgu_hbm, bgu_ref, wd_hbm, bd_ref, ys_ref,
                   wgu_f32, wd_f32, wgu_bf, wd_bf, w_sem):
    i = pl.program_id(0)
    n_used = nu_ref[0]

    def weight_copies(e):
        return (pltpu.make_async_copy(wgu_hbm.at[e], wgu_f32, w_sem.at[0]),
                pltpu.make_async_copy(wd_hbm.at[e], wd_f32, w_sem.at[1]))

    @pl.when(i == 0)
    def _():
        for cp in weight_copies(be_ref[0]):
            cp.start()

    @pl.when(i < n_used)
    def _():
        prev = be_ref[jnp.maximum(i - 1, 0)]

        @pl.when((i == 0) | (be_ref[i] != prev))
        def _():
            for cp in weight_copies(be_ref[i]):
                cp.wait()
            wgu_bf[...] = wgu_f32[...].astype(BF16)
            wd_bf[...] = wd_f32[...].astype(BF16)
            nxt = nxt_ref[be_ref[i]]

            @pl.when(nxt >= 0)
            def _():
                for cp in weight_copies(nxt):
                    cp.start()

        def swiglu_rows(rows):
            x = _from_row_tiles(xs_ref, 0, rows).astype(BF16)
            hu = jnp.dot(x, wgu_bf[...], preferred_element_type=F32) + bgu_ref[...]
            glu = jnp.minimum(hu[:, :D_EXPERT], SWIGLU_LIMIT)
            lin = jnp.clip(hu[:, D_EXPERT:], -SWIGLU_LIMIT, SWIGLU_LIMIT)
            act = glu * _sigmoid(SWIGLU_ALPHA * glu) * (lin + 1.0)
            y = jnp.dot(act.astype(BF16), wd_bf[...], preferred_element_type=F32) + bd_ref[...]
            _to_row_tiles(ys_ref, y, rows)

        @pl.when(half_ref[i] == 0)
        def _():
            swiglu_rows(MOE_BM)

        @pl.when(half_ref[i] != 0)
        def _():
            swiglu_rows(MOE_BM // 2)
            ys_ref[MOE_BM // 2 * LANE_CHUNKS:, :] = jnp.zeros((MOE_BM // 2 * LANE_CHUNKS, LANES), F32)

    @pl.when(i >= n_used)
    def _():
        ys_ref[...] = jnp.zeros_like(ys_ref)


def _experts(block_e, block_half, next_e, n_used, xs2d, w_gu, b_gu, w_down, b_down):
    n_blocks = block_e.shape[0]
    rows = MOE_BM * LANE_CHUNKS
    live = lambda i, be, hf, nx, nu: (jnp.minimum(i, nu[0] - 1), 0)
    return pl.pallas_call(
        _expert_kernel,
        out_shape=jax.ShapeDtypeStruct(xs2d.shape, F32),
        grid_spec=pltpu.PrefetchScalarGridSpec(
            num_scalar_prefetch=4,
            grid=(n_blocks,),
            in_specs=[pl.BlockSpec((rows, LANES), live),
                      pl.BlockSpec(memory_space=pl.ANY),
                      pl.BlockSpec((None, 1, 2 * D_EXPERT), lambda i, be, hf, nx, nu: (be[i], 0, 0)),
                      pl.BlockSpec(memory_space=pl.ANY),
                      pl.BlockSpec((None, 1, D_MODEL), lambda i, be, hf, nx, nu: (be[i], 0, 0))],
            out_specs=pl.BlockSpec((rows, LANES), lambda i, be, hf, nx, nu: (i, 0)),
            scratch_shapes=[pltpu.VMEM((D_MODEL, 2 * D_EXPERT), F32),
                            pltpu.VMEM((D_EXPERT, D_MODEL), F32),
                            pltpu.VMEM((D_MODEL, 2 * D_EXPERT), BF16),
                            pltpu.VMEM((D_EXPERT, D_MODEL), BF16),
                            pltpu.SemaphoreType.DMA((2,))]),
        compiler_params=_cparams(("arbitrary",)),
    )(block_e, block_half, next_e, n_used, xs2d, w_gu, b_gu, w_down, b_down)


def _combine_kernel(dcur_ref, dnext_ref, ys_hbm, g_ref, x1_ref, gt_ref, np_ref, o_ref, ybuf, row_sem, *, tm):
    i = pl.program_id(0)
    n_rows = TOP_K * tm
    slot = i % 2

    def slot_rows(s):
        return ybuf.at[pl.ds(pl.multiple_of(s * n_rows * LANE_CHUNKS, n_rows * LANE_CHUNKS), n_rows * LANE_CHUNKS), :]

    def issue(d_ref, s):
        dst = slot_rows(s)
        for n in range(n_rows):
            pltpu.make_async_copy(ys_hbm.at[d_ref[0, n]], dst.at[pl.ds(n * LANE_CHUNKS, LANE_CHUNKS), :],
                                  row_sem.at[s]).start(priority=n % 2)

    @pl.when(i == 0)
    def _():
        issue(dcur_ref, 0)

    @pl.when(i + 1 < pl.num_programs(0))
    def _():
        issue(dnext_ref, 1 - slot)

    pltpu.make_async_copy(slot_rows(slot), slot_rows(slot), row_sem.at[slot]).wait()
    g = g_ref[...]
    y = None
    for kk in range(TOP_K):
        rows = _from_row_tiles(ybuf, (slot * n_rows + kk * tm) * LANE_CHUNKS, tm)
        y = g[:, kk:kk + 1] * rows if y is None else y + g[:, kk:kk + 1] * rows
    o_ref[...] = x1_ref[...] + gt_ref[...] * _rms(y, np_ref[...])


def _combine(dest_tiles, ys_tiles, gates_t, x1, gt, npost, tm, tiles_per_seq, tile0):
    n_tiles = dest_tiles.shape[0]
    t = n_tiles * tm
    smem_tile = lambda f: pl.BlockSpec((None, 1, TOP_K * tm), f, memory_space=pltpu.SMEM)
    return pl.pallas_call(
        functools.partial(_combine_kernel, tm=tm),
        out_shape=jax.ShapeDtypeStruct((t, D_MODEL), F32),
        grid=(n_tiles,),
        in_specs=[smem_tile(lambda i: (i, 0, 0)),
                  smem_tile(lambda i: (jnp.minimum(i + 1, n_tiles - 1), 0, 0)),
                  pl.BlockSpec(memory_space=pl.ANY),
                  pl.BlockSpec((tm, TOP_K), lambda i: (i + tile0, 0)),
                  pl.BlockSpec((tm, D_MODEL), lambda i: (i + tile0, 0)),
                  _mod_spec(gt, tm, tiles_per_seq),
                  pl.BlockSpec((1, D_MODEL), lambda i: (0, 0))],
        out_specs=pl.BlockSpec((tm, D_MODEL), lambda i: (i, 0)),
        scratch_shapes=[pltpu.VMEM((2 * TOP_K * tm * LANE_CHUNKS, LANES), F32),
                        pltpu.SemaphoreType.DMA((2,))],
        compiler_params=_cparams(("arbitrary",)),
    )(dest_tiles, dest_tiles, ys_tiles, gates_t, x1, gt, npost)


def _expert_layout(counts, n_blocks):
    padded = (counts + MOE_BM - 1) // MOE_BM * MOE_BM
    pad_ends = jnp.cumsum(padded)
    pad_starts = pad_ends - padded
    block_start = jnp.arange(n_blocks, dtype=I32) * MOE_BM
    block_e = jnp.minimum(jnp.sum((pad_ends[None, :] <= block_start[:, None]).astype(I32), axis=1), N_EXPERTS - 1)
    n_used = (pad_ends[-1:] // MOE_BM).astype(I32)
    e_ids = jnp.arange(N_EXPERTS, dtype=I32)
    rows_end = jnp.sum(jnp.where(block_e[:, None] == e_ids[None, :], (pad_starts + counts)[None, :], 0), axis=1)
    block_half = (rows_end - block_start <= MOE_BM // 2).astype(I32)
    later = (counts > 0)[None, :] & (e_ids[None, :] > e_ids[:, None])
    next_e = jnp.min(jnp.where(later, e_ids[None, :], N_EXPERTS), axis=1)
    next_e = jnp.where(next_e < N_EXPERTS, next_e, -1).astype(I32)
    return pad_starts, pad_starts + counts, pad_ends, block_e, block_half, next_e, n_used


def _slots(idx, rank, pad_starts):
    onehot = idx[:, :, None] == jnp.arange(N_EXPERTS, dtype=I32)[None, None, :]
    return rank + jnp.sum(jnp.where(onehot, pad_starts[None, None, :], 0), axis=-1)


def _tile_major(a, tm):
    t = a.shape[1]
    return a.reshape(TOP_K, t // tm, tm).transpose(1, 0, 2).reshape(t // tm, 1, TOP_K * tm)


def _block_diag(wb):
    n, d, _ = wb.shape
    eye = jnp.eye(n, dtype=wb.dtype)
    return (wb[:, :, None, :] * eye[:, None, :, None]).reshape(n * d, n * d)


def _pick_tile(n, prefs):
    for p in prefs:
        if n % p == 0:
            return p
    raise ValueError(f"no tile for {n}")


def kernel(x_prompt, x_sample, state_rglru_conv, state_rglru_h, state_hgrn_S, c_prompt, c_sample, ada_w, ada_b, norm_pre_mix, norm_post_mix, norm_pre_ffn, norm_post_ffn, w_in, conv_w, conv_b, lru_wa, lru_ba, lru_wi, lru_bi, lru_lambda, hg_lb, hg_norm, w_out, router_w, router_b, w_gu, b_gu, w_down, b_down):
    assert ada_w.shape[0] == 1, "single-layer trunk"
    bp, lp, _ = x_prompt.shape
    bs, ls, _ = x_sample.shape
    tp, ts = bp * lp, bs * ls
    assert CONV_WIDTH - 1 <= ls <= SUBLANES

    w_in_bf = w_in[0].astype(BF16)
    w_out_bf = w_out[0].astype(BF16)
    wg = jnp.concatenate([_block_diag(lru_wa[0]), _block_diag(lru_wi[0])], axis=1).astype(BF16)
    bg = jnp.concatenate([lru_ba[0], lru_bi[0]])[None, :]
    lbs = jnp.cumsum(jax.nn.softmax(hg_lb.astype(F32), axis=0), axis=0)[0][None, :]
    mw = (conv_w[0], conv_b[0][None, :], wg, bg, lru_lambda[0][None, :], lbs, hg_norm[0][None, :])
    row = lambda p: p[0][None, :]

    n_c = bp + bs
    n_c_pad = -(-n_c // SUBLANES) * SUBLANES
    c_all = jnp.concatenate([c_prompt, c_sample, jnp.zeros((n_c_pad - n_c, D_MODEL), F32)], axis=0)
    mod = _ada(c_all, ada_w[0], ada_b[0][None, :])
    mod_p = [m[:, None, :] for m in jnp.split(mod[:bp], 6, axis=-1)]
    mod_s = [jnp.repeat(m, ls, axis=0) for m in jnp.split(mod[bp:n_c], 6, axis=-1)]

    xp = x_prompt.reshape(tp, D_MODEL)
    xs = x_sample.reshape(ts, D_MODEL)
    tm_s = _pick_tile(ts, (256, 128, 64, 32, 16, 8))

    tc = _pick_tile(lp // 2, (256, 128, 64))
    m_p, conv_p, h_p, s_p = _mixer_prompt(xp, mod_p[1], mod_p[0], row(norm_pre_mix), w_in_bf, bp, lp, mw, tc)
    proj_s = _inproj(xs, mod_s[1], mod_s[0], row(norm_pre_mix), w_in_bf, tm_s, 1)

    cst_hdr = jnp.pad(state_rglru_conv[0], ((0, 0), (CONV_HDR - (CONV_WIDTH - 1), 0), (0, 0)))
    m_s, conv_s, h_s, s_s = _mixer_sample(proj_s, cst_hdr, state_rglru_h[0][:, None, :], state_hgrn_S[0], mw, ls)

    tmq = _pick_tile(math.gcd(lp, ts), (512, 256, 128, 64, 32, 16, 8))
    x1, hf, idx, gate, rank, counts = _post(
        m_p, m_s, xp, xs, (mod_p[2], mod_p[4], mod_p[3]), (mod_s[2], mod_s[4], mod_s[3]),
        row(norm_post_mix), row(norm_pre_ffn), w_out_bf, router_w[0].T, router_b[0][:, None], tmq, lp // tmq)

    n_blocks = -(-(TOP_K * (tp + ts)) // MOE_BM) + N_EXPERTS
    n_slots = n_blocks * MOE_BM
    pad_starts, fill_lo, pad_ends, block_e, block_half, next_e, n_used = _expert_layout(
        counts[:, 0].astype(I32), n_blocks)
    fill_hi = pad_ends.at[N_EXPERTS - 1].set(n_slots)
    dest = _slots(idx, rank, pad_starts)
    tmd = _pick_tile(math.gcd(lp, ts), (512, 256, 128, 64, 32, 16, 8))
    tmc = _pick_tile(math.gcd(lp, ts), (256, 128, 64, 32, 16, 8))
    xs_rows = _dispatch(fill_lo, fill_hi, _tile_major(dest, tmd), hf, n_slots, tmd)
    ys = _experts(block_e, block_half, next_e, n_used, xs_rows.reshape(n_slots * LANE_CHUNKS, LANES),
                  w_gu[0], b_gu[0][:, None, :], w_down[0], b_down[0][:, None, :])
    ys_tiles = ys.reshape(-1, LANE_CHUNKS, LANES)

    gates_t = gate.T
    ctiles = _tile_major(dest, tmc)
    y_p = _combine(ctiles[:tp // tmc], ys_tiles, gates_t, x1, mod_p[5], row(norm_post_ffn), tmc, lp // tmc, 0)
    y_s = _combine(ctiles[tp // tmc:], ys_tiles, gates_t, x1, mod_s[5], row(norm_post_ffn), tmc, 1, tp // tmc)

    return (y_p.reshape(bp, lp, D_MODEL), y_s.reshape(bs, ls, D_MODEL),
            conv_p[None], h_p.reshape(1, bp, LRU_WIDTH), s_p[None],
            conv_s[None], h_s.reshape(1, bs, LRU_WIDTH), s_s[None])
```

```python
import functools
import math

import jax
import jax.numpy as jnp
from jax import lax
from jax.experimental import pallas as pl
from jax.experimental.pallas import tpu as pltpu

F32 = jnp.float32
BF16 = jnp.bfloat16
I32 = jnp.int32

D_MODEL = 1024
LRU_WIDTH = 512
CONV_WIDTH = 4
LRU_C = 8.0
HG_WIDTH = 512
HG_HEAD_DIM = 128
HG_HEADS = 4
IN_COLS = 2 * LRU_WIDTH + 4 * HG_WIDTH
N_EXPERTS = 32
TOP_K = 4
D_EXPERT = 1024
SWIGLU_LIMIT = 7.0
SWIGLU_ALPHA = 1.702
RMS_EPS = 1e-6

C_XL, C_YL, C_Q, C_F, C_V, C_G = 0, 512, 1024, 1536, 2048, 2560

SUBLANES = 8
LANES = 128
LANE_CHUNKS = D_MODEL // LANES
CONV_HDR = SUBLANES
HG_CHUNK = 64
HG_MAX_EXPONENT = 60.0
MOE_BM = 512
ZERO_ROWS = 64
VMEM_LIMIT = 56 * 1024 * 1024

_NT = (((1,), (1,)), ((), ()))
_TN = (((0,), (0,)), ((), ()))


def _cparams(sem):
    return pltpu.CompilerParams(dimension_semantics=sem, vmem_limit_bytes=VMEM_LIMIT)


def _rms(x, g):
    return x * lax.rsqrt(jnp.mean(x * x, axis=-1, keepdims=True) + RMS_EPS) * g


def _gelu_tanh(x):
    c = math.sqrt(2.0 / math.pi)
    return 0.5 * x * (1.0 + jnp.tanh(c * (x + 0.044715 * (x * x * x))))


def _ada_kernel(c_ref, w_ref, b_ref, o_ref):
    c = c_ref[...]
    s = c * jax.nn.sigmoid(c)
    o_ref[...] = jnp.dot(s.astype(BF16), w_ref[...].astype(BF16), preferred_element_type=F32) + b_ref[...]


def _ada(c_all, ada_w, ada_b):
    n = c_all.shape[0]
    tn = 1024
    return pl.pallas_call(
        _ada_kernel,
        out_shape=jax.ShapeDtypeStruct((n, 6 * D_MODEL), F32),
        grid=(6 * D_MODEL // tn,),
        in_specs=[pl.BlockSpec((n, D_MODEL), lambda j: (0, 0)),
                  pl.BlockSpec((D_MODEL, tn), lambda j: (0, j)),
                  pl.BlockSpec((1, tn), lambda j: (0, j))],
        out_specs=pl.BlockSpec((n, tn), lambda j: (0, j)),
        compiler_params=_cparams(("arbitrary",)),
    )(c_all, ada_w, ada_b)


def _inproj_kernel(x_ref, sc_ref, sh_ref, g_ref, w_ref, o_ref):
    h = _rms(x_ref[...], g_ref[...]) * (1.0 + sc_ref[...]) + sh_ref[...]
    o_ref[...] = jnp.dot(h.astype(BF16), w_ref[...], preferred_element_type=F32)


def _mod_spec(mod, tm, tiles_per_seq):
    if isinstance(mod, tuple):
        return pl.BlockSpec((tm, D_MODEL), lambda i: (i, mod[1]))
    return pl.BlockSpec((None, 1, D_MODEL), lambda i: (i // tiles_per_seq, 0, 0))


def _mod_array(mod):
    return mod[0] if isinstance(mod, tuple) else mod


def _inproj(x, sc, sh, g, w_bf, tm, tiles_per_seq):
    t = x.shape[0]
    return pl.pallas_call(
        _inproj_kernel,
        out_shape=jax.ShapeDtypeStruct((t, IN_COLS), F32),
        grid=(t // tm,),
        in_specs=[pl.BlockSpec((tm, D_MODEL), lambda i: (i, 0)),
                  _mod_spec(sc, tm, tiles_per_seq), _mod_spec(sh, tm, tiles_per_seq),
                  pl.BlockSpec((1, D_MODEL), lambda i: (0, 0)),
                  pl.BlockSpec((D_MODEL, IN_COLS), lambda i: (0, 0))],
        out_specs=pl.BlockSpec((tm, IN_COLS), lambda i: (i, 0)),
        compiler_params=_cparams(("arbitrary",)),
    )(x, _mod_array(sc), _mod_array(sh), g, w_bf)


def _sigmoid(x):
    return 0.5 * jnp.tanh(0.5 * x) + 0.5


def _group_rows(x):
    rows, w = x.shape
    return x.reshape(rows // SUBLANES, SUBLANES, w)


def _scan_rows(a, u, h0):
    a3, u3 = _group_rows(a), _group_rows(u)
    r3 = lax.broadcasted_iota(I32, a3.shape, 1)
    s = 1
    while s < SUBLANES:
        keep = r3 >= s
        a_sh = jnp.where(keep, pltpu.roll(a3, s, 1), 1.0)
        u_sh = jnp.where(keep, pltpu.roll(u3, s, 1), 0.0)
        u3 = a3 * u_sh + u3
        a3 = a3 * a_sh
        s *= 2
    out, h = [], h0
    for i in range(a3.shape[0]):
        hg = u3[i] + a3[i] * h
        h = hg[SUBLANES - 1:SUBLANES, :]
        out.append(hg)
    return jnp.concatenate(out, axis=0), h


def _chunk_cumsum(x, chunk):
    x3 = _group_rows(x)
    r3 = lax.broadcasted_iota(I32, x3.shape, 1)
    s = 1
    while s < SUBLANES:
        x3 = x3 + jnp.where(r3 >= s, pltpu.roll(x3, s, 1), 0.0)
        s *= 2
    out, carry = [], None
    for i in range(x3.shape[0]):
        cur = x3[i] if i % (chunk // SUBLANES) == 0 else x3[i] + carry
        carry = cur[SUBLANES - 1:SUBLANES, :]
        out.append(cur)
    return jnp.concatenate(out, axis=0)


def _lru_gates(xc, wg_ref, bg_ref, lam_ref):
    gates = jnp.dot(xc.astype(BF16), wg_ref[...], preferred_element_type=F32) + bg_ref[...]
    r = _sigmoid(gates[:, :LRU_WIDTH])
    ig = _sigmoid(gates[:, LRU_WIDTH:])
    z = -lam_ref[...]
    softplus = jnp.maximum(z, 0.0) + jnp.log1p(jnp.exp(-jnp.abs(z)))
    log_a = -LRU_C * r * softplus
    a = jnp.exp(log_a)
    th = jnp.tanh(log_a)
    mult2 = -2.0 * th / (1.0 - th)
    return a, jnp.sqrt(mult2), ig


def _conv4(xbuf, xl, cw_ref, cb_ref, rows):
    cw = cw_ref[...]
    h = CONV_HDR
    return (cb_ref[...] + cw[0:1] * xbuf[h - 3:h - 3 + rows, :] + cw[1:2] * xbuf[h - 2:h - 2 + rows, :]
            + cw[2:3] * xbuf[h - 1:h - 1 + rows, :] + cw[3:4] * xl)


def _conv4_rolled(xl, prev, cw_ref, cb_ref):
    cw = cw_ref[...]
    row8 = lax.broadcasted_iota(I32, prev.shape, 0)
    acc = cb_ref[...] + cw[CONV_WIDTH - 1:CONV_WIDTH] * xl
    for s in range(1, CONV_WIDTH):
        sh = pltpu.roll(xl, s, 0)
        top = jnp.where(row8 < s, pltpu.roll(prev, s, 0), sh[0:SUBLANES, :])
        sh = jnp.concatenate([top, sh[SUBLANES:, :]], axis=0)
        acc = acc + cw[CONV_WIDTH - 1 - s:CONV_WIDTH - s] * sh
    return acc


def _head_norm_gate(o, gate, hgn):
    return _rms(o, hgn) * _sigmoid(gate)


def _hgrn_window(q, k, v, b, state, steps, state_is_transposed):
    row = lax.broadcasted_iota(I32, q.shape, 0)
    valid = row < steps
    b_end = b[steps - 1:steps, :]
    q_in = jnp.where(valid, q * jnp.exp(b), 0.0).astype(BF16)
    if state_is_transposed:
        o = lax.dot_general(q_in, state.astype(BF16), _NT, preferred_element_type=F32)
    else:
        o = jnp.dot(q_in, state.astype(BF16), preferred_element_type=F32)
    for d in range(steps):
        ok = valid & (row >= d)
        k_d = k if d == 0 else pltpu.roll(k, d, 0)
        v_d = v if d == 0 else pltpu.roll(v, d, 0)
        b_d = b if d == 0 else pltpu.roll(b, d, 0)
        decay = jnp.exp(jnp.where(ok, b - b_d, 0.0))
        wgt = jnp.sum(jnp.where(ok, q * k_d * decay, 0.0), axis=-1, keepdims=True)
        o = o + wgt * v_d
    k_out = jnp.where(valid, k * jnp.exp(jnp.where(valid, b_end - b, 0.0)), 0.0).astype(BF16)
    v_ok = jnp.where(valid, v, 0.0).astype(BF16)
    keep = jnp.exp(b_end)
    if state_is_transposed:
        s_new = state * keep + lax.dot_general(v_ok, k_out, _TN, preferred_element_type=F32)
    else:
        keep_col = jnp.broadcast_to(keep, q.shape).T[:, 0:1]
        s_new = state * keep_col + lax.dot_general(k_out, v_ok, _TN, preferred_element_type=F32)
    return o, s_new


def _tile_cumsum(x):
    row = lax.broadcasted_iota(I32, x.shape, 0)
    s = 1
    while s < SUBLANES:
        x = x + jnp.where(row >= s, pltpu.roll(x, s, 0), 0.0)
        s *= 2
    return x


def _mix_tile(proj_ref, cw_ref, cb_ref, wg_ref, bg_ref, lam_ref, lb_ref, hgn_ref, m_ref, xprev, hcar, st, st0, obuf,
              seq_start, tc, side_work):
    w = LRU_WIDTH
    side_work = iter(side_work)

    next(side_work)()
    xl = proj_ref[:, C_XL:C_XL + w]
    xc = _conv4_rolled(xl, xprev[...], cw_ref, cb_ref)
    xprev[...] = xl[tc - SUBLANES:tc, :]
    a, mult, ig = _lru_gates(xc, wg_ref, bg_ref, lam_ref)
    row = lax.broadcasted_iota(I32, (tc, w), 0)
    if seq_start is not None:
        mult = jnp.where((row == 0) & seq_start, 1.0, mult)
    h, h_last = _scan_rows(a, mult * ig * xc, hcar[0:1, :])
    hcar[0:1, :] = h_last
    m_ref[:, 0:w] = (h * _gelu_tanh(proj_ref[:, C_YL:C_YL + w])).astype(BF16)

    next(side_work)()
    lb = lb_ref[...]
    f = lb + (1.0 - lb) * _sigmoid(proj_ref[:, C_F:C_F + w])
    k = 1.0 - f
    b = _chunk_cumsum(jnp.log(f), HG_CHUNK)
    q = proj_ref[:, C_Q:C_Q + w]
    v = proj_ref[:, C_V:C_V + w]
    mid = HG_CHUNK // 2
    st0[...] = st[...]
    tri = (lax.broadcasted_iota(I32, (HG_CHUNK, HG_CHUNK), 0)
           >= lax.broadcasted_iota(I32, (HG_CHUNK, HG_CHUNK), 1))
    states = [st[hd] for hd in range(HG_HEADS)]
    for c in range(tc // HG_CHUNK):
        next(side_work)()
        rs = slice(c * HG_CHUNK, (c + 1) * HG_CHUNK)
        for hd in range(HG_HEADS):
            cs = slice(hd * HG_HEAD_DIM, (hd + 1) * HG_HEAD_DIM)
            s_t = states[hd]
            bc, qc, kc = b[rs, cs], q[rs, cs], k[rs, cs]
            vb = v[rs, cs].astype(BF16)
            b_mid = bc[mid - 1:mid, :]
            b_end = bc[HG_CHUNK - 1:HG_CHUNK, :]
            qp = (qc * jnp.exp(bc - b_mid)).astype(BF16)
            kp = (kc * jnp.exp(b_mid - bc)).astype(BF16)
            att = lax.dot_general(qp, kp, _NT, preferred_element_type=F32)
            att = jnp.where(tri, att, 0.0)
            q_in = (qc * jnp.exp(bc)).astype(BF16)
            o = (jnp.dot(att.astype(BF16), vb, preferred_element_type=F32)
                 + lax.dot_general(q_in, s_t.astype(BF16), _NT, preferred_element_type=F32))
            k_out = (kc * jnp.exp(b_end - bc)).astype(BF16)
            states[hd] = s_t * jnp.exp(b_end) + lax.dot_general(vb, k_out, _TN, preferred_element_type=F32)
            obuf[rs, cs] = o
    for hd in range(HG_HEADS):
        st[hd] = states[hd]

    half_decay = None
    for c in range(tc // HG_CHUNK):
        b_mid = b[c * HG_CHUNK + mid - 1:c * HG_CHUNK + mid, :]
        b_end = b[(c + 1) * HG_CHUNK - 1:(c + 1) * HG_CHUNK, :]
        worst = jnp.maximum(-b_mid, b_mid - b_end)
        half_decay = worst if half_decay is None else jnp.maximum(half_decay, worst)

    @pl.when(jnp.logical_not(jnp.max(half_decay) < HG_MAX_EXPONENT))
    def _():
        st[...] = st0[...]

        def window(gi, carry):
            rows = pl.ds(pl.multiple_of(gi * SUBLANES, SUBLANES), SUBLANES)
            f8 = lb + (1.0 - lb) * _sigmoid(proj_ref[rows, C_F:C_F + w])
            b8 = _tile_cumsum(jnp.log(f8))
            k8 = 1.0 - f8
            q8 = proj_ref[rows, C_Q:C_Q + w]
            v8 = proj_ref[rows, C_V:C_V + w]
            for hd in range(HG_HEADS):
                cs = slice(hd * HG_HEAD_DIM, (hd + 1) * HG_HEAD_DIM)
                o, s_new = _hgrn_window(q8[:, cs], k8[:, cs], v8[:, cs], b8[:, cs], st[hd], SUBLANES, True)
                st[hd] = s_new
                obuf[rows, cs] = o
            return carry
        lax.fori_loop(0, tc // SUBLANES, window, 0)

    hgn = hgn_ref[...]
    for hd in range(HG_HEADS):
        cs = slice(hd * HG_HEAD_DIM, (hd + 1) * HG_HEAD_DIM)
        gate = proj_ref[:, C_G + hd * HG_HEAD_DIM:C_G + (hd + 1) * HG_HEAD_DIM]
        m_ref[:, w + hd * HG_HEAD_DIM:w + (hd + 1) * HG_HEAD_DIM] = (
            _head_norm_gate(obuf[:, cs], gate, hgn).astype(BF16))


def _mixer_prompt_kernel(x_ref, xn_ref, sc_ref, sh_ref, scn_ref, shn_ref, g_ref, win_ref, cw_ref, cb_ref, wg_ref,
                         bg_ref, lam_ref, lb_ref, hgn_ref, m_ref, conv_ref, h_ref, s_ref,
                         proj_a, proj_b, hn_buf, xprev, hcar, st, st0, obuf, *, tc):
    j = pl.program_id(1)
    n_phases = 2 + tc // HG_CHUNK
    cols = IN_COLS // n_phases

    def project(x, sc, sh, out_ref):
        def chunk(c):
            if c == 0:
                hn_buf[...] = (_rms(x(), g_ref[...]) * (1.0 + sc[...]) + sh[...]).astype(BF16)
            out_ref[:, c * cols:(c + 1) * cols] = jnp.dot(hn_buf[...], win_ref[:, c * cols:(c + 1) * cols],
                                                          preferred_element_type=F32)
        return [functools.partial(chunk, c) for c in range(n_phases)]

    def mix(proj_ref, first_row, seq_start, side_work):
        _mix_tile(proj_ref, cw_ref, cb_ref, wg_ref, bg_ref, lam_ref, lb_ref, hgn_ref,
                  m_ref.at[pl.ds(first_row, tc), :], xprev, hcar, st, st0, obuf, seq_start, tc, side_work)

    @pl.when((pl.program_id(0) == 0) & (j == 0))
    def _():
        for thunk in project(lambda: x_ref[0:tc, :], sc_ref, sh_ref, proj_a):
            thunk()

    @pl.when(j == 0)
    def _():
        xprev[...] = jnp.zeros_like(xprev)
        hcar[...] = jnp.zeros_like(hcar)
        st[...] = jnp.zeros_like(st)

    mix(proj_a, 0, j == 0, project(lambda: x_ref[tc:2 * tc, :], sc_ref, sh_ref, proj_b))
    mix(proj_b, tc, None, project(lambda: xn_ref[...], scn_ref, shn_ref, proj_a))

    @pl.when(j == pl.num_programs(1) - 1)
    def _():
        conv_ref[...] = xprev[SUBLANES - (CONV_WIDTH - 1):SUBLANES, :]
        h_ref[...] = hcar[0:1, :]
        for hd in range(HG_HEADS):
            s_ref[hd] = st[hd].T


def _mixer_prompt(x, sc, sh, g, w_in_bf, nb, seq, mw, tc):
    cw, cb, wg, bg, lam, lb, hgn = mw
    nj = seq // (2 * tc)
    last_tile = nb * seq // tc - 1
    next_tile = lambda b, j: jnp.minimum(2 * (b * nj + j) + 2, last_tile)
    const = lambda shape: pl.BlockSpec(shape, lambda b, j: tuple(0 for _ in shape))
    per_seq = lambda: pl.BlockSpec((None, 1, D_MODEL), lambda b, j: (b, 0, 0))
    next_seq = lambda: pl.BlockSpec((None, 1, D_MODEL), lambda b, j: (next_tile(b, j) // (2 * nj), 0, 0))
    return pl.pallas_call(
        functools.partial(_mixer_prompt_kernel, tc=tc),
        out_shape=(jax.ShapeDtypeStruct((nb * seq, D_MODEL), BF16),
                   jax.ShapeDtypeStruct((nb, CONV_WIDTH - 1, LRU_WIDTH), F32),
                   jax.ShapeDtypeStruct((nb, 1, LRU_WIDTH), F32),
                   jax.ShapeDtypeStruct((nb, HG_HEADS, HG_HEAD_DIM, HG_HEAD_DIM), F32)),
        grid=(nb, nj),
        in_specs=[pl.BlockSpec((2 * tc, D_MODEL), lambda b, j: (b * nj + j, 0)),
                  pl.BlockSpec((tc, D_MODEL), lambda b, j: (next_tile(b, j), 0)),
                  per_seq(), per_seq(), next_seq(), next_seq(), const((1, D_MODEL)), const((D_MODEL, IN_COLS)),
                  const((CONV_WIDTH, LRU_WIDTH)), const((1, LRU_WIDTH)),
                  const((LRU_WIDTH, 2 * LRU_WIDTH)), const((1, 2 * LRU_WIDTH)),
                  const((1, LRU_WIDTH)), const((1, HG_WIDTH)), const((1, HG_HEAD_DIM))],
        out_specs=(pl.BlockSpec((2 * tc, D_MODEL), lambda b, j: (b * nj + j, 0)),
                   pl.BlockSpec((None, CONV_WIDTH - 1, LRU_WIDTH), lambda b, j: (b, 0, 0)),
                   pl.BlockSpec((None, 1, LRU_WIDTH), lambda b, j: (b, 0, 0)),
                   pl.BlockSpec((None, HG_HEADS, HG_HEAD_DIM, HG_HEAD_DIM), lambda b, j: (b, 0, 0, 0))),
        scratch_shapes=[pltpu.VMEM((tc, IN_COLS), F32),
                        pltpu.VMEM((tc, IN_COLS), F32),
                        pltpu.VMEM((tc, D_MODEL), BF16),
                        pltpu.VMEM((SUBLANES, LRU_WIDTH), F32),
                        pltpu.VMEM((SUBLANES, LRU_WIDTH), F32),
                        pltpu.VMEM((HG_HEADS, HG_HEAD_DIM, HG_HEAD_DIM), F32),
                        pltpu.VMEM((HG_HEADS, HG_HEAD_DIM, HG_HEAD_DIM), F32),
                        pltpu.VMEM((tc, HG_WIDTH), F32)],
        compiler_params=_cparams(("arbitrary", "arbitrary")),
    )(x, x, sc, sh, sc, sh, g, w_in_bf, cw, cb, wg, bg, lam, lb, hgn)


def _mixer_sample_kernel(proj_ref, cst_ref, h0_ref, s0_ref, cw_ref, cb_ref, wg_ref, bg_ref, lam_ref, lb_ref,
                         hgn_ref, m_ref, conv_ref, h_ref, s_ref, pbuf, xbuf, *, steps):
    for bi in range(pbuf.shape[0]):
        seq_rows = slice(bi * steps, (bi + 1) * steps)
        pbuf[bi, 0:steps, :] = proj_ref[seq_rows, :]
        pbuf[bi, steps:SUBLANES, :] = jnp.zeros((SUBLANES - steps, IN_COLS), F32)
        _mixer_sample_one(pbuf.at[bi], cst_ref.at[bi], h0_ref.at[bi], s0_ref.at[bi], cw_ref, cb_ref, wg_ref,
                          bg_ref, lam_ref, lb_ref, hgn_ref, m_ref, seq_rows, conv_ref.at[bi], h_ref.at[bi],
                          s_ref.at[bi], xbuf.at[bi], steps)


def _mixer_sample_one(proj_ref, cst_ref, h0_ref, s0_ref, cw_ref, cb_ref, wg_ref, bg_ref, lam_ref, lb_ref,
                      hgn_ref, m_ref, m_rows, conv_ref, h_ref, s_ref, xbuf, steps):
    w = LRU_WIDTH
    rows = SUBLANES
    row = lax.broadcasted_iota(I32, (rows, w), 0)

    xl = proj_ref[:, C_XL:C_XL + w]
    xbuf[0:CONV_HDR, :] = cst_ref[...]
    xbuf[CONV_HDR:CONV_HDR + rows, :] = xl
    xc = _conv4(xbuf, xl, cw_ref, cb_ref, rows)
    conv_ref[...] = xbuf[CONV_HDR + steps - 3:CONV_HDR + steps, :]
    a, mult, ig = _lru_gates(xc, wg_ref, bg_ref, lam_ref)
    u = mult * ig * xc
    h = h0_ref[...]
    hs = jnp.zeros((rows, w), F32)
    for t in range(steps):
        h = a[t:t + 1, :] * h + u[t:t + 1, :]
        hs = jnp.where(row == t, h, hs)
    h_ref[...] = h
    m_ref[m_rows, 0:w] = (hs * _gelu_tanh(proj_ref[:, C_YL:C_YL + w]))[0:steps, :]

    lb = lb_ref[...]
    f = lb + (1.0 - lb) * _sigmoid(proj_ref[:, C_F:C_F + w])
    k = 1.0 - f
    b = _tile_cumsum(jnp.log(f))
    q = proj_ref[:, C_Q:C_Q + w]
    v = proj_ref[:, C_V:C_V + w]
    hgn = hgn_ref[...]
    for hd in range(HG_HEADS):
        cs = slice(hd * HG_HEAD_DIM, (hd + 1) * HG_HEAD_DIM)
        o, s_new = _hgrn_window(q[:, cs], k[:, cs], v[:, cs], b[:, cs], s0_ref[hd], steps, False)
        s_ref[hd] = s_new
        gate = proj_ref[:, C_G + hd * HG_HEAD_DIM:C_G + (hd + 1) * HG_HEAD_DIM]
        m_ref[m_rows, w + hd * HG_HEAD_DIM:w + (hd + 1) * HG_HEAD_DIM] = _head_norm_gate(o, gate, hgn)[0:steps, :]


def _mixer_sample(proj, cst_hdr, h0, s0, mw, steps):
    cw, cb, wg, bg, lam, lb, hgn = mw
    nb = proj.shape[0] // steps
    per_step = _pick_tile(nb, (8, 4, 2, 1))
    const = lambda shape: pl.BlockSpec(shape, lambda b: tuple(0 for _ in shape))
    per_b = lambda shape: pl.BlockSpec((per_step,) + shape, lambda b: (b,) + tuple(0 for _ in shape))
    return pl.pallas_call(
        functools.partial(_mixer_sample_kernel, steps=steps),
        out_shape=(jax.ShapeDtypeStruct((nb * steps, D_MODEL), F32),
                   jax.ShapeDtypeStruct((nb, CONV_WIDTH - 1, LRU_WIDTH), F32),
                   jax.ShapeDtypeStruct((nb, 1, LRU_WIDTH), F32),
                   jax.ShapeDtypeStruct((nb, HG_HEADS, HG_HEAD_DIM, HG_HEAD_DIM), F32)),
        grid=(nb // per_step,),
        in_specs=[pl.BlockSpec((per_step * steps, IN_COLS), lambda b: (b, 0)),
                  per_b((CONV_HDR, LRU_WIDTH)), per_b((1, LRU_WIDTH)),
                  per_b((HG_HEADS, HG_HEAD_DIM, HG_HEAD_DIM)),
                  const((CONV_WIDTH, LRU_WIDTH)), const((1, LRU_WIDTH)),
                  const((LRU_WIDTH, 2 * LRU_WIDTH)), const((1, 2 * LRU_WIDTH)),
                  const((1, LRU_WIDTH)), const((1, HG_WIDTH)), const((1, HG_HEAD_DIM))],
        out_specs=(pl.BlockSpec((per_step * steps, D_MODEL), lambda b: (b, 0)),
                   per_b((CONV_WIDTH - 1, LRU_WIDTH)), per_b((1, LRU_WIDTH)),
                   per_b((HG_HEADS, HG_HEAD_DIM, HG_HEAD_DIM))),
        scratch_shapes=[pltpu.VMEM((per_step, SUBLANES, IN_COLS), F32),
                        pltpu.VMEM((per_step, CONV_HDR + SUBLANES, LRU_WIDTH), F32)],
        compiler_params=_cparams(("arbitrary",)),
    )(proj, cst_hdr, h0, s0, cw, cb, wg, bg, lam, lb, hgn)


def _to_row_tiles(ref, val, rows):
    for c in range(LANE_CHUNKS):
        ref[pl.ds(c, rows, stride=LANE_CHUNKS), :] = val[:, c * LANES:(c + 1) * LANES]


def _from_row_tiles(ref, base, rows):
    return jnp.concatenate(
        [ref[pl.ds(base + c, rows, stride=LANE_CHUNKS), :] for c in range(LANE_CHUNKS)], axis=-1)


def _post_kernel(mp_ref, ms_ref, xp_ref, xs_ref, gtmp_ref, scfp_ref, shfp_ref, gtms_ref, scfs_ref, shfs_ref,
                 npost_ref, npre_ref, wout_ref, rwt_ref, rb_ref,
                 x1_ref, hf_ref, idx_ref, gate_ref, rank_ref, cnt_out_ref, cnt, *, n_p):
    i = pl.program_id(0)
    tm = xp_ref.shape[0]

    @pl.when(i == 0)
    def _():
        cnt[...] = jnp.zeros_like(cnt)

    def run(m_ref, x_ref, gtm_ref, scf_ref, shf_ref):
        mix = jnp.dot(m_ref[...].astype(BF16), wout_ref[...], preferred_element_type=F32)
        x1 = x_ref[...] + gtm_ref[...] * _rms(mix, npost_ref[...])
        x1_ref[...] = x1
        hf = _rms(x1, npre_ref[...]) * (1.0 + scf_ref[...]) + shf_ref[...]
        _to_row_tiles(hf_ref, hf, tm)
        logits = lax.dot_general(rwt_ref[...], hf, _NT, precision=lax.Precision.HIGHEST,
                                 preferred_element_type=F32) + rb_ref[...]
        e_iota = lax.broadcasted_iota(I32, logits.shape, 0)
        vals, idxs = [], []
        for _ in range(TOP_K):
            mx = jnp.max(logits, axis=0, keepdims=True)
            ix = jnp.min(jnp.where(logits == mx, e_iota, N_EXPERTS), axis=0, keepdims=True)
            vals.append(mx)
            idxs.append(ix)
            logits = jnp.where(e_iota == ix, -jnp.inf, logits)
        exps = [jnp.exp(vv - vals[0]) for vv in vals]
        den = exps[0] + exps[1] + exps[2] + exps[3]
        earlier = (lax.broadcasted_iota(I32, (tm, tm), 0) < lax.broadcasted_iota(I32, (tm, tm), 1))
        earlier = jnp.where(earlier, 1.0, 0.0).astype(BF16)
        base = cnt[...]
        for kk in range(TOP_K):
            idx_ref[kk:kk + 1, :] = idxs[kk]
            gate_ref[kk:kk + 1, :] = exps[kk] / den
            hit = e_iota == idxs[kk]
            onehot = jnp.where(hit, 1.0, 0.0)
            before = jnp.dot(onehot.astype(BF16), earlier, preferred_element_type=F32)
            rank = jnp.sum(jnp.where(hit, before + base, 0.0), axis=0, keepdims=True)
            rank_ref[kk:kk + 1, :] = rank.astype(I32)
            base = base + jnp.sum(onehot, axis=1, keepdims=True)
        cnt[...] = base

    @pl.when(i < n_p)
    def _():
        run(mp_ref, xp_ref, gtmp_ref, scfp_ref, shfp_ref)

    @pl.when(i >= n_p)
    def _():
        run(ms_ref, xs_ref, gtms_ref, scfs_ref, shfs_ref)

    @pl.when(i == pl.num_programs(0) - 1)
    def _():
        cnt_out_ref[...] = cnt[...]


def _post(m_p, m_s, x_p, x_s, mod_p, mod_s, npost, npre, wout_bf, rwt, rb, tm, tiles_per_seq):
    n_p, n_s = x_p.shape[0] // tm, x_s.shape[0] // tm
    t = (n_p + n_s) * tm
    first = lambda i: jnp.minimum(i, n_p - 1)
    second = lambda i: jnp.maximum(i - n_p, 0)
    row = lambda: pl.BlockSpec((1, D_MODEL), lambda i: (0, 0))
    tok_p = lambda: pl.BlockSpec((tm, D_MODEL), lambda i: (first(i), 0))
    tok_s = lambda: pl.BlockSpec((tm, D_MODEL), lambda i: (second(i), 0))
    seq_p = lambda: pl.BlockSpec((None, 1, D_MODEL), lambda i: (first(i) // tiles_per_seq, 0, 0))
    per_k = lambda: pl.BlockSpec((TOP_K, tm), lambda i: (0, i))
    return pl.pallas_call(
        functools.partial(_post_kernel, n_p=n_p),
        out_shape=(jax.ShapeDtypeStruct((t, D_MODEL), F32),
                   jax.ShapeDtypeStruct((t * LANE_CHUNKS, LANES), F32),
                   jax.ShapeDtypeStruct((TOP_K, t), I32), jax.ShapeDtypeStruct((TOP_K, t), F32),
                   jax.ShapeDtypeStruct((TOP_K, t), I32), jax.ShapeDtypeStruct((N_EXPERTS, 1), F32)),
        grid=(n_p + n_s,),
        in_specs=[tok_p(), tok_s(), tok_p(), tok_s(), seq_p(), seq_p(), seq_p(),
                  *[pl.BlockSpec((tm, D_MODEL), lambda i, col=col: (second(i), col)) for _, col in mod_s],
                  row(), row(),
                  pl.BlockSpec((D_MODEL, D_MODEL), lambda i: (0, 0)),
                  pl.BlockSpec((N_EXPERTS, D_MODEL), lambda i: (0, 0)),
                  pl.BlockSpec((N_EXPERTS, 1), lambda i: (0, 0))],
        out_specs=(pl.BlockSpec((tm, D_MODEL), lambda i: (i, 0)),
                   pl.BlockSpec((tm * LANE_CHUNKS, LANES), lambda i: (i, 0)),
                   per_k(), per_k(), per_k(), pl.BlockSpec((N_EXPERTS, 1), lambda i: (0, 0))),
        scratch_shapes=[pltpu.VMEM((N_EXPERTS, 1), F32)],
        compiler_params=_cparams(("arbitrary",)),
    )(m_p, m_s, x_p, x_s, *mod_p, *[arr for arr, _ in mod_s], npost, npre, wout_bf, rwt, rb)


def _dispatch_kernel(fill_lo_ref, fill_hi_ref, dest_ref, hf_ref, xs_hbm, zeros, row_sem, pad_sem, *, tm):
    i = pl.program_id(0)
    n_rows = TOP_K * tm

    @pl.when(i == 0)
    def _():
        zeros[...] = jnp.zeros_like(zeros)

        def per_expert(e, totals):
            lo, hi = fill_lo_ref[e], fill_hi_ref[e]
            mid = jnp.minimum((lo + ZERO_ROWS - 1) // ZERO_ROWS * ZERO_ROWS, hi)

            def per_row(s, carry):
                pltpu.make_async_copy(zeros.at[0], xs_hbm.at[s], pad_sem).start()
                return carry
            lax.fori_loop(lo, mid, per_row, 0)

            def per_chunk(c, carry):
                pltpu.make_async_copy(zeros, xs_hbm.at[pl.ds(pl.multiple_of(c * ZERO_ROWS, ZERO_ROWS), ZERO_ROWS)],
                                      pad_sem).start()
                return carry
            lax.fori_loop(mid // ZERO_ROWS, hi // ZERO_ROWS, per_chunk, 0)
            return totals[0] + (mid - lo), totals[1] + (hi - mid) // ZERO_ROWS
        n_rows_filled, n_chunks_filled = lax.fori_loop(0, N_EXPERTS, per_expert, (0, 0))

        def drain_row(s, carry):
            pltpu.make_async_copy(zeros.at[0], xs_hbm.at[0], pad_sem).wait()
            return carry
        lax.fori_loop(0, n_rows_filled, drain_row, 0)

        def drain_chunk(c, carry):
            pltpu.make_async_copy(zeros, xs_hbm.at[pl.ds(0, ZERO_ROWS)], pad_sem).wait()
            return carry
        lax.fori_loop(0, n_chunks_filled, drain_chunk, 0)

    for n in range(n_rows):
        pltpu.make_async_copy(hf_ref.at[pl.ds((n % tm) * LANE_CHUNKS, LANE_CHUNKS), :],
                              xs_hbm.at[dest_ref[0, n]], row_sem).start(priority=n % 2)
    pltpu.make_async_copy(xs_hbm.at[pl.ds(0, n_rows)], xs_hbm.at[pl.ds(0, n_rows)], row_sem).wait()


def _dispatch(fill_lo, fill_hi, dest_tiles, hf_rows, n_slots, tm):
    return pl.pallas_call(
        functools.partial(_dispatch_kernel, tm=tm),
        out_shape=jax.ShapeDtypeStruct((n_slots, LANE_CHUNKS, LANES), F32),
        grid_spec=pltpu.PrefetchScalarGridSpec(
            num_scalar_prefetch=2, grid=(dest_tiles.shape[0],),
            in_specs=[pl.BlockSpec((None, 1, TOP_K * tm), lambda i, lo, hi: (i, 0, 0), memory_space=pltpu.SMEM),
                      pl.BlockSpec((tm * LANE_CHUNKS, LANES), lambda i, lo, hi: (i, 0))],
            out_specs=pl.BlockSpec(memory_space=pl.ANY),
            scratch_shapes=[pltpu.VMEM((ZERO_ROWS, LANE_CHUNKS, LANES), F32),
                            pltpu.SemaphoreType.DMA, pltpu.SemaphoreType.DMA]),
        compiler_params=_cparams(("arbitrary",)),
    )(fill_lo, fill_hi, dest_tiles, hf_rows)


def _expert_kernel(be_ref, half_ref, nxt_ref, nu_ref, xs_ref, wgu_hbm, bgu_ref, wd_hbm, bd_ref, ys_ref,
                   wgu_f32, wd_f32, wgu_bf, wd_bf, w_sem):
    i = pl.program_id(0)
    n_used = nu_ref[0]

    def weight_copies(e):
        return (pltpu.make_async_copy(wgu_hbm.at[e], wgu_f32, w_sem.at[0]),
                pltpu.make_async_copy(wd_hbm.at[e], wd_f32, w_sem.at[1]))

    @pl.when(i == 0)
    def _():
        for cp in weight_copies(be_ref[0]):
            cp.start()

    @pl.when(i < n_used)
    def _():
        prev = be_ref[jnp.maximum(i - 1, 0)]

        @pl.when((i == 0) | (be_ref[i] != prev))
        def _():
            for cp in weight_copies(be_ref[i]):
                cp.wait()
            wgu_bf[...] = wgu_f32[...].astype(BF16)
            wd_bf[...] = wd_f32[...].astype(BF16)
            nxt = nxt_ref[be_ref[i]]

            @pl.when(nxt >= 0)
            def _():
                for cp in weight_copies(nxt):
                    cp.start()

        def swiglu_rows(rows):
            x = _from_row_tiles(xs_ref, 0, rows).astype(BF16)
            hu = jnp.dot(x, wgu_bf[...], preferred_element_type=F32) + bgu_ref[...]
            glu = jnp.minimum(hu[:, :D_EXPERT], SWIGLU_LIMIT)
            lin = jnp.clip(hu[:, D_EXPERT:], -SWIGLU_LIMIT, SWIGLU_LIMIT)
            act = glu * _sigmoid(SWIGLU_ALPHA * glu) * (lin + 1.0)
            y = jnp.dot(act.astype(BF16), wd_bf[...], preferred_element_type=F32) + bd_ref[...]
            _to_row_tiles(ys_ref, y, rows)

        @pl.when(half_ref[i] == 0)
        def _():
            swiglu_rows(MOE_BM)

        @pl.when(half_ref[i] != 0)
        def _():
            swiglu_rows(MOE_BM // 2)
            ys_ref[MOE_BM // 2 * LANE_CHUNKS:, :] = jnp.zeros((MOE_BM // 2 * LANE_CHUNKS, LANES), F32)

    @pl.when(i >= n_used)
    def _():
        ys_ref[...] = jnp.zeros_like(ys_ref)


def _experts(block_e, block_half, next_e, n_used, xs2d, w_gu, b_gu, w_down, b_down):
    n_blocks = block_e.shape[0]
    rows = MOE_BM * LANE_CHUNKS
    live = lambda i, be, hf, nx, nu: (jnp.minimum(i, nu[0] - 1), 0)
    return pl.pallas_call(
        _expert_kernel,
        out_shape=jax.ShapeDtypeStruct(xs2d.shape, F32),
        grid_spec=pltpu.PrefetchScalarGridSpec(
            num_scalar_prefetch=4,
            grid=(n_blocks,),
            in_specs=[pl.BlockSpec((rows, LANES), live),
                      pl.BlockSpec(memory_space=pl.ANY),
                      pl.BlockSpec((None, 1, 2 * D_EXPERT), lambda i, be, hf, nx, nu: (be[i], 0, 0)),
                      pl.BlockSpec(memory_space=pl.ANY),
                      pl.BlockSpec((None, 1, D_MODEL), lambda i, be, hf, nx, nu: (be[i], 0, 0))],
            out_specs=pl.BlockSpec((rows, LANES), lambda i, be, hf, nx, nu: (i, 0)),
            scratch_shapes=[pltpu.VMEM((D_MODEL, 2 * D_EXPERT), F32),
                            pltpu.VMEM((D_EXPERT, D_MODEL), F32),
                            pltpu.VMEM((D_MODEL, 2 * D_EXPERT), BF16),
                            pltpu.VMEM((D_EXPERT, D_MODEL), BF16),
                            pltpu.SemaphoreType.DMA((2,))]),
        compiler_params=_cparams(("arbitrary",)),
    )(block_e, block_half, next_e, n_used, xs2d, w_gu, b_gu, w_down, b_down)


def _combine_kernel(dcur_ref, dnext_ref, ys_hbm, g_ref, x1_ref, gt_ref, np_ref, o_ref, ybuf, row_sem, *, tm):
    i = pl.program_id(0)
    n_rows = TOP_K * tm
    slot = i % 2

    def slot_rows(s):
        return ybuf.at[pl.ds(pl.multiple_of(s * n_rows * LANE_CHUNKS, n_rows * LANE_CHUNKS), n_rows * LANE_CHUNKS), :]

    def issue(d_ref, s):
        dst = slot_rows(s)
        for n in range(n_rows):
            pltpu.make_async_copy(ys_hbm.at[d_ref[0, n]], dst.at[pl.ds(n * LANE_CHUNKS, LANE_CHUNKS), :],
                                  row_sem.at[s]).start(priority=n % 2)

    @pl.when(i == 0)
    def _():
        issue(dcur_ref, 0)

    @pl.when(i + 1 < pl.num_programs(0))
    def _():
        issue(dnext_ref, 1 - slot)

    pltpu.make_async_copy(slot_rows(slot), slot_rows(slot), row_sem.at[slot]).wait()
    g = g_ref[...]
    y = None
    for kk in range(TOP_K):
        rows = _from_row_tiles(ybuf, (slot * n_rows + kk * tm) * LANE_CHUNKS, tm)
        y = g[:, kk:kk + 1] * rows if y is None else y + g[:, kk:kk + 1] * rows
    o_ref[...] = x1_ref[...] + gt_ref[...] * _rms(y, np_ref[...])


def _combine(dest_tiles, ys_tiles, gates_t, x1, gt, npost, tm, tiles_per_seq, tile0):
    n_tiles = dest_tiles.shape[0]
    t = n_tiles * tm
    smem_tile = lambda f: pl.BlockSpec((None, 1, TOP_K * tm), f, memory_space=pltpu.SMEM)
    return pl.pallas_call(
        functools.partial(_combine_kernel, tm=tm),
        out_shape=jax.ShapeDtypeStruct((t, D_MODEL), F32),
        grid=(n_tiles,),
        in_specs=[smem_tile(lambda i: (i, 0, 0)),
                  smem_tile(lambda i: (jnp.minimum(i + 1, n_tiles - 1), 0, 0)),
                  pl.BlockSpec(memory_space=pl.ANY),
                  pl.BlockSpec((tm, TOP_K), lambda i: (i + tile0, 0)),
                  pl.BlockSpec((tm, D_MODEL), lambda i: (i + tile0, 0)),
                  _mod_spec(gt, tm, tiles_per_seq),
                  pl.BlockSpec((1, D_MODEL), lambda i: (0, 0))],
        out_specs=pl.BlockSpec((tm, D_MODEL), lambda i: (i, 0)),
        scratch_shapes=[pltpu.VMEM((2 * TOP_K * tm * LANE_CHUNKS, LANES), F32),
                        pltpu.SemaphoreType.DMA((2,))],
        compiler_params=_cparams(("arbitrary",)),
    )(dest_tiles, dest_tiles, ys_tiles, gates_t, x1, _mod_array(gt), npost)


def _expert_layout(counts, n_blocks):
    padded = (counts + MOE_BM - 1) // MOE_BM * MOE_BM
    pad_ends = jnp.cumsum(padded)
    pad_starts = pad_ends - padded
    block_start = jnp.arange(n_blocks, dtype=I32) * MOE_BM
    block_e = jnp.minimum(jnp.sum((pad_ends[None, :] <= block_start[:, None]).astype(I32), axis=1), N_EXPERTS - 1)
    n_used = (pad_ends[-1:] // MOE_BM).astype(I32)
    e_ids = jnp.arange(N_EXPERTS, dtype=I32)
    rows_end = jnp.sum(jnp.where(block_e[:, None] == e_ids[None, :], (pad_starts + counts)[None, :], 0), axis=1)
    block_half = (rows_end - block_start <= MOE_BM // 2).astype(I32)
    later = (counts > 0)[None, :] & (e_ids[None, :] > e_ids[:, None])
    next_e = jnp.min(jnp.where(later, e_ids[None, :], N_EXPERTS), axis=1)
    next_e = jnp.where(next_e < N_EXPERTS, next_e, -1).astype(I32)
    return pad_starts, pad_starts + counts, pad_ends, block_e, block_half, next_e, n_used


def _slots(idx, rank, pad_starts):
    onehot = idx[:, :, None] == jnp.arange(N_EXPERTS, dtype=I32)[None, None, :]
    return rank + jnp.sum(jnp.where(onehot, pad_starts[None, None, :], 0), axis=-1)


def _tile_major(a, tm):
    t = a.shape[1]
    return a.reshape(TOP_K, t // tm, tm).transpose(1, 0, 2).reshape(t // tm, 1, TOP_K * tm)


def _block_diag(wb):
    n, d, _ = wb.shape
    eye = jnp.eye(n, dtype=wb.dtype)
    return (wb[:, :, None, :] * eye[:, None, :, None]).reshape(n * d, n * d)


def _pick_tile(n, prefs):
    for p in prefs:
        if n % p == 0:
            return p
    raise ValueError(f"no tile for {n}")


def kernel(x_prompt, x_sample, state_rglru_conv, state_rglru_h, state_hgrn_S, c_prompt, c_sample, ada_w, ada_b, norm_pre_mix, norm_post_mix, norm_pre_ffn, norm_post_ffn, w_in, conv_w, conv_b, lru_wa, lru_ba, lru_wi, lru_bi, lru_lambda, hg_lb, hg_norm, w_out, router_w, router_b, w_gu, b_gu, w_down, b_down):
    assert ada_w.shape[0] == 1, "single-layer trunk"
    bp, lp, _ = x_prompt.shape
    bs, ls, _ = x_sample.shape
    tp, ts = bp * lp, bs * ls
    assert CONV_WIDTH - 1 <= ls <= SUBLANES

    w_in_bf = w_in[0].astype(BF16)
    w_out_bf = w_out[0].astype(BF16)
    wg = jnp.concatenate([_block_diag(lru_wa[0]), _block_diag(lru_wi[0])], axis=1).astype(BF16)
    bg = jnp.concatenate([lru_ba[0], lru_bi[0]])[None, :]
    lbs = jnp.cumsum(jax.nn.softmax(hg_lb.astype(F32), axis=0), axis=0)[0][None, :]
    mw = (conv_w[0], conv_b[0][None, :], wg, bg, lru_lambda[0][None, :], lbs, hg_norm[0][None, :])
    row = lambda p: p[0][None, :]

    n_c = bp + bs
    n_c_pad = -(-n_c // SUBLANES) * SUBLANES
    c_all = jnp.concatenate([c_prompt, c_sample, jnp.zeros((n_c_pad - n_c, D_MODEL), F32)], axis=0)
    mod = _ada(c_all, ada_w[0], ada_b[0][None, :])
    mod_p = [m[:, None, :] for m in jnp.split(mod[:bp], 6, axis=-1)]
    mod_s_rows = jnp.repeat(mod[bp:n_c], ls, axis=0)
    mod_s = [(mod_s_rows, k) for k in range(6)]

    xp = x_prompt.reshape(tp, D_MODEL)
    xs = x_sample.reshape(ts, D_MODEL)
    tm_s = _pick_tile(ts, (256, 128, 64, 32, 16, 8))

    tc = _pick_tile(lp // 2, (256, 128, 64))
    m_p, conv_p, h_p, s_p = _mixer_prompt(xp, mod_p[1], mod_p[0], row(norm_pre_mix), w_in_bf, bp, lp, mw, tc)
    proj_s = _inproj(xs, mod_s[1], mod_s[0], row(norm_pre_mix), w_in_bf, tm_s, 1)

    cst_hdr = jnp.pad(state_rglru_conv[0], ((0, 0), (CONV_HDR - (CONV_WIDTH - 1), 0), (0, 0)))
    m_s, conv_s, h_s, s_s = _mixer_sample(proj_s, cst_hdr, state_rglru_h[0][:, None, :], state_hgrn_S[0], mw, ls)

    tmq = _pick_tile(math.gcd(lp, ts), (512, 256, 128, 64, 32, 16, 8))
    x1, hf, idx, gate, rank, counts = _post(
        m_p, m_s, xp, xs, (mod_p[2], mod_p[4], mod_p[3]), (mod_s[2], mod_s[4], mod_s[3]),
        row(norm_post_mix), row(norm_pre_ffn), w_out_bf, router_w[0].T, router_b[0][:, None], tmq, lp // tmq)

    n_blocks = -(-(TOP_K * (tp + ts)) // MOE_BM) + N_EXPERTS
    n_slots = n_blocks * MOE_BM
    pad_starts, fill_lo, pad_ends, block_e, block_half, next_e, n_used = _expert_layout(
        counts[:, 0].astype(I32), n_blocks)
    fill_hi = pad_ends.at[N_EXPERTS - 1].set(n_slots)
    dest = _slots(idx, rank, pad_starts)
    tmd = _pick_tile(math.gcd(lp, ts), (512, 256, 128, 64, 32, 16, 8))
    tmc = _pick_tile(math.gcd(lp, ts), (256, 128, 64, 32, 16, 8))
    xs_rows = _dispatch(fill_lo, fill_hi, _tile_major(dest, tmd), hf, n_slots, tmd)
    ys = _experts(block_e, block_half, next_e, n_used, xs_rows.reshape(n_slots * LANE_CHUNKS, LANES),
                  w_gu[0], b_gu[0][:, None, :], w_down[0], b_down[0][:, None, :])
    ys_tiles = ys.reshape(-1, LANE_CHUNKS, LANES)

    gates_t = gate.T
    ctiles = _tile_major(dest, tmc)
    y_p = _combine(ctiles[:tp // tmc], ys_tiles, gates_t, x1, mod_p[5], row(norm_post_ffn), tmc, lp // tmc, 0)
    y_s = _combine(ctiles[tp // tmc:], ys_tiles, gates_t, x1, mod_s[5], row(norm_post_ffn), tmc, 1, tp // tmc)

    return (y_p.reshape(bp, lp, D_MODEL), y_s.reshape(bs, ls, D_MODEL),
            conv_p[None], h_p.reshape(1, bp, LRU_WIDTH), s_p[None],
            conv_s[None], h_s.reshape(1, bs, LRU_WIDTH), s_s[None])
```

```python
import functools
import math

import jax
import jax.numpy as jnp
from jax import lax
from jax.experimental import pallas as pl
from jax.experimental.pallas import tpu as pltpu

F32 = jnp.float32
BF16 = jnp.bfloat16
I32 = jnp.int32

D_MODEL = 1024
LRU_WIDTH = 512
CONV_WIDTH = 4
LRU_C = 8.0
HG_WIDTH = 512
HG_HEAD_DIM = 128
HG_HEADS = 4
IN_COLS = 2 * LRU_WIDTH + 4 * HG_WIDTH
N_EXPERTS = 32
TOP_K = 4
D_EXPERT = 1024
SWIGLU_LIMIT = 7.0
SWIGLU_ALPHA = 1.702
RMS_EPS = 1e-6

C_XL, C_YL, C_Q, C_F, C_V, C_G = 0, 512, 1024, 1536, 2048, 2560

SUBLANES = 8
LANES = 128
LANE_CHUNKS = D_MODEL // LANES
CONV_HDR = SUBLANES
HG_CHUNK = 64
HG_MAX_EXPONENT = 60.0
MOE_BM = 512
DISPATCH_BUFS = 3
ZERO_ROWS = 64
VMEM_LIMIT = 56 * 1024 * 1024

_NT = (((1,), (1,)), ((), ()))
_TN = (((0,), (0,)), ((), ()))


def _cparams(sem):
    return pltpu.CompilerParams(dimension_semantics=sem, vmem_limit_bytes=VMEM_LIMIT)


def _rms(x, g):
    return x * lax.rsqrt(jnp.mean(x * x, axis=-1, keepdims=True) + RMS_EPS) * g


def _gelu_tanh(x):
    c = math.sqrt(2.0 / math.pi)
    return 0.5 * x * (1.0 + jnp.tanh(c * (x + 0.044715 * (x * x * x))))


def _ada_kernel(c_ref, w_ref, b_ref, o_ref):
    c = c_ref[...]
    s = c * jax.nn.sigmoid(c)
    o_ref[...] = jnp.dot(s.astype(BF16), w_ref[...].astype(BF16), preferred_element_type=F32) + b_ref[...]


def _ada(c_all, ada_w, ada_b):
    n = c_all.shape[0]
    tn = 1024
    return pl.pallas_call(
        _ada_kernel,
        out_shape=jax.ShapeDtypeStruct((n, 6 * D_MODEL), F32),
        grid=(6 * D_MODEL // tn,),
        in_specs=[pl.BlockSpec((n, D_MODEL), lambda j: (0, 0)),
                  pl.BlockSpec((D_MODEL, tn), lambda j: (0, j)),
                  pl.BlockSpec((1, tn), lambda j: (0, j))],
        out_specs=pl.BlockSpec((n, tn), lambda j: (0, j)),
        compiler_params=_cparams(("arbitrary",)),
    )(c_all, ada_w, ada_b)


def _inproj_kernel(x_ref, sc_ref, sh_ref, g_ref, w_ref, o_ref):
    h = _rms(x_ref[...], g_ref[...]) * (1.0 + sc_ref[...]) + sh_ref[...]
    o_ref[...] = jnp.dot(h.astype(BF16), w_ref[...], preferred_element_type=F32)


def _mod_spec(mod, tm, tiles_per_seq):
    if mod.ndim == 3:
        return pl.BlockSpec((None, 1, D_MODEL), lambda i: (i // tiles_per_seq, 0, 0))
    return pl.BlockSpec((tm, D_MODEL), lambda i: (i, 0))


def _inproj(x, sc, sh, g, w_bf, tm, tiles_per_seq):
    t = x.shape[0]
    return pl.pallas_call(
        _inproj_kernel,
        out_shape=jax.ShapeDtypeStruct((t, IN_COLS), F32),
        grid=(t // tm,),
        in_specs=[pl.BlockSpec((tm, D_MODEL), lambda i: (i, 0)),
                  _mod_spec(sc, tm, tiles_per_seq), _mod_spec(sh, tm, tiles_per_seq),
                  pl.BlockSpec((1, D_MODEL), lambda i: (0, 0)),
                  pl.BlockSpec((D_MODEL, IN_COLS), lambda i: (0, 0))],
        out_specs=pl.BlockSpec((tm, IN_COLS), lambda i: (i, 0)),
        compiler_params=_cparams(("arbitrary",)),
    )(x, sc, sh, g, w_bf)


def _sigmoid(x):
    return 0.5 * jnp.tanh(0.5 * x) + 0.5


def _group_rows(x):
    rows, w = x.shape
    return x.reshape(rows // SUBLANES, SUBLANES, w)


def _scan_rows(a, u, h0):
    a3, u3 = _group_rows(a), _group_rows(u)
    r3 = lax.broadcasted_iota(I32, a3.shape, 1)
    s = 1
    while s < SUBLANES:
        keep = r3 >= s
        a_sh = jnp.where(keep, pltpu.roll(a3, s, 1), 1.0)
        u_sh = jnp.where(keep, pltpu.roll(u3, s, 1), 0.0)
        u3 = a3 * u_sh + u3
        a3 = a3 * a_sh
        s *= 2
    out, h = [], h0
    for i in range(a3.shape[0]):
        hg = u3[i] + a3[i] * h
        h = hg[SUBLANES - 1:SUBLANES, :]
        out.append(hg)
    return jnp.concatenate(out, axis=0), h


def _chunk_cumsum(x, chunk):
    x3 = _group_rows(x)
    r3 = lax.broadcasted_iota(I32, x3.shape, 1)
    s = 1
    while s < SUBLANES:
        x3 = x3 + jnp.where(r3 >= s, pltpu.roll(x3, s, 1), 0.0)
        s *= 2
    out, carry = [], None
    for i in range(x3.shape[0]):
        cur = x3[i] if i % (chunk // SUBLANES) == 0 else x3[i] + carry
        carry = cur[SUBLANES - 1:SUBLANES, :]
        out.append(cur)
    return jnp.concatenate(out, axis=0)


def _lru_gates(xc, wg_ref, bg_ref, lam_ref):
    gates = jnp.dot(xc.astype(BF16), wg_ref[...], preferred_element_type=F32) + bg_ref[...]
    r = _sigmoid(gates[:, :LRU_WIDTH])
    ig = _sigmoid(gates[:, LRU_WIDTH:])
    z = -lam_ref[...]
    softplus = jnp.maximum(z, 0.0) + jnp.log1p(jnp.exp(-jnp.abs(z)))
    log_a = -LRU_C * r * softplus
    a = jnp.exp(log_a)
    th = jnp.tanh(log_a)
    mult2 = -2.0 * th / (1.0 - th)
    return a, jnp.sqrt(mult2), ig


def _conv4(xbuf, xl, cw_ref, cb_ref, rows):
    cw = cw_ref[...]
    h = CONV_HDR
    return (cb_ref[...] + cw[0:1] * xbuf[h - 3:h - 3 + rows, :] + cw[1:2] * xbuf[h - 2:h - 2 + rows, :]
            + cw[2:3] * xbuf[h - 1:h - 1 + rows, :] + cw[3:4] * xl)


def _conv4_rolled(xl, prev, cw_ref, cb_ref):
    cw = cw_ref[...]
    row8 = lax.broadcasted_iota(I32, prev.shape, 0)
    acc = cb_ref[...] + cw[CONV_WIDTH - 1:CONV_WIDTH] * xl
    for s in range(1, CONV_WIDTH):
        sh = pltpu.roll(xl, s, 0)
        top = jnp.where(row8 < s, pltpu.roll(prev, s, 0), sh[0:SUBLANES, :])
        sh = jnp.concatenate([top, sh[SUBLANES:, :]], axis=0)
        acc = acc + cw[CONV_WIDTH - 1 - s:CONV_WIDTH - s] * sh
    return acc


def _head_norm_gate(o, gate, hgn):
    return _rms(o, hgn) * _sigmoid(gate)


def _hgrn_window(q, k, v, b, state, steps, state_is_transposed):
    row = lax.broadcasted_iota(I32, q.shape, 0)
    valid = row < steps
    b_end = b[steps - 1:steps, :]
    q_in = jnp.where(valid, q * jnp.exp(b), 0.0).astype(BF16)
    if state_is_transposed:
        o = lax.dot_general(q_in, state.astype(BF16), _NT, preferred_element_type=F32)
    else:
        o = jnp.dot(q_in, state.astype(BF16), preferred_element_type=F32)
    for d in range(steps):
        ok = valid & (row >= d)
        k_d = k if d == 0 else pltpu.roll(k, d, 0)
        v_d = v if d == 0 else pltpu.roll(v, d, 0)
        b_d = b if d == 0 else pltpu.roll(b, d, 0)
        decay = jnp.exp(jnp.where(ok, b - b_d, 0.0))
        wgt = jnp.sum(jnp.where(ok, q * k_d * decay, 0.0), axis=-1, keepdims=True)
        o = o + wgt * v_d
    k_out = jnp.where(valid, k * jnp.exp(jnp.where(valid, b_end - b, 0.0)), 0.0).astype(BF16)
    v_ok = jnp.where(valid, v, 0.0).astype(BF16)
    keep = jnp.exp(b_end)
    if state_is_transposed:
        s_new = state * keep + lax.dot_general(v_ok, k_out, _TN, preferred_element_type=F32)
    else:
        keep_col = jnp.broadcast_to(keep, q.shape).T[:, 0:1]
        s_new = state * keep_col + lax.dot_general(k_out, v_ok, _TN, preferred_element_type=F32)
    return o, s_new


def _tile_cumsum(x):
    row = lax.broadcasted_iota(I32, x.shape, 0)
    s = 1
    while s < SUBLANES:
        x = x + jnp.where(row >= s, pltpu.roll(x, s, 0), 0.0)
        s *= 2
    return x


def _mix_tile(proj_ref, cw_ref, cb_ref, wg_ref, bg_ref, lam_ref, lb_ref, hgn_ref, m_ref, xprev, hcar, st, st0, obuf,
              seq_start, tc, side_work):
    w = LRU_WIDTH
    side_work = iter(side_work)

    next(side_work)()
    xl = proj_ref[:, C_XL:C_XL + w]
    xc = _conv4_rolled(xl, xprev[...], cw_ref, cb_ref)
    xprev[...] = xl[tc - SUBLANES:tc, :]
    a, mult, ig = _lru_gates(xc, wg_ref, bg_ref, lam_ref)
    row = lax.broadcasted_iota(I32, (tc, w), 0)
    if seq_start is not None:
        mult = jnp.where((row == 0) & seq_start, 1.0, mult)
    h, h_last = _scan_rows(a, mult * ig * xc, hcar[0:1, :])
    hcar[0:1, :] = h_last
    m_ref[:, 0:w] = (h * _gelu_tanh(proj_ref[:, C_YL:C_YL + w])).astype(BF16)

    next(side_work)()
    lb = lb_ref[...]
    f = lb + (1.0 - lb) * _sigmoid(proj_ref[:, C_F:C_F + w])
    k = 1.0 - f
    b = _chunk_cumsum(jnp.log(f), HG_CHUNK)
    q = proj_ref[:, C_Q:C_Q + w]
    v = proj_ref[:, C_V:C_V + w]
    mid = HG_CHUNK // 2
    st0[...] = st[...]
    tri = (lax.broadcasted_iota(I32, (HG_CHUNK, HG_CHUNK), 0)
           >= lax.broadcasted_iota(I32, (HG_CHUNK, HG_CHUNK), 1))
    states = [st[hd] for hd in range(HG_HEADS)]
    for c in range(tc // HG_CHUNK):
        next(side_work)()
        rs = slice(c * HG_CHUNK, (c + 1) * HG_CHUNK)
        for hd in range(HG_HEADS):
            cs = slice(hd * HG_HEAD_DIM, (hd + 1) * HG_HEAD_DIM)
            s_t = states[hd]
            bc, qc, kc = b[rs, cs], q[rs, cs], k[rs, cs]
            vb = v[rs, cs].astype(BF16)
            b_mid = bc[mid - 1:mid, :]
            b_end = bc[HG_CHUNK - 1:HG_CHUNK, :]
            qp = (qc * jnp.exp(bc - b_mid)).astype(BF16)
            kp = (kc * jnp.exp(b_mid - bc)).astype(BF16)
            att = lax.dot_general(qp, kp, _NT, preferred_element_type=F32)
            att = jnp.where(tri, att, 0.0)
            q_in = (qc * jnp.exp(bc)).astype(BF16)
            o = (jnp.dot(att.astype(BF16), vb, preferred_element_type=F32)
                 + lax.dot_general(q_in, s_t.astype(BF16), _NT, preferred_element_type=F32))
            k_out = (kc * jnp.exp(b_end - bc)).astype(BF16)
            states[hd] = s_t * jnp.exp(b_end) + lax.dot_general(vb, k_out, _TN, preferred_element_type=F32)
            obuf[rs, cs] = o
    for hd in range(HG_HEADS):
        st[hd] = states[hd]

    half_decay = None
    for c in range(tc // HG_CHUNK):
        b_mid = b[c * HG_CHUNK + mid - 1:c * HG_CHUNK + mid, :]
        b_end = b[(c + 1) * HG_CHUNK - 1:(c + 1) * HG_CHUNK, :]
        worst = jnp.maximum(-b_mid, b_mid - b_end)
        half_decay = worst if half_decay is None else jnp.maximum(half_decay, worst)

    @pl.when(jnp.logical_not(jnp.max(half_decay) < HG_MAX_EXPONENT))
    def _():
        st[...] = st0[...]

        def window(gi, carry):
            rows = pl.ds(pl.multiple_of(gi * SUBLANES, SUBLANES), SUBLANES)
            f8 = lb + (1.0 - lb) * _sigmoid(proj_ref[rows, C_F:C_F + w])
            b8 = _tile_cumsum(jnp.log(f8))
            k8 = 1.0 - f8
            q8 = proj_ref[rows, C_Q:C_Q + w]
            v8 = proj_ref[rows, C_V:C_V + w]
            for hd in range(HG_HEADS):
                cs = slice(hd * HG_HEAD_DIM, (hd + 1) * HG_HEAD_DIM)
                o, s_new = _hgrn_window(q8[:, cs], k8[:, cs], v8[:, cs], b8[:, cs], st[hd], SUBLANES, True)
                st[hd] = s_new
                obuf[rows, cs] = o
            return carry
        lax.fori_loop(0, tc // SUBLANES, window, 0)

    hgn = hgn_ref[...]
    for hd in range(HG_HEADS):
        cs = slice(hd * HG_HEAD_DIM, (hd + 1) * HG_HEAD_DIM)
        gate = proj_ref[:, C_G + hd * HG_HEAD_DIM:C_G + (hd + 1) * HG_HEAD_DIM]
        m_ref[:, w + hd * HG_HEAD_DIM:w + (hd + 1) * HG_HEAD_DIM] = (
            _head_norm_gate(obuf[:, cs], gate, hgn).astype(BF16))


def _mixer_prompt_kernel(x_ref, xn_ref, sc_ref, sh_ref, scn_ref, shn_ref, g_ref, win_ref, cw_ref, cb_ref, wg_ref,
                         bg_ref, lam_ref, lb_ref, hgn_ref, m_ref, conv_ref, h_ref, s_ref,
                         proj_a, proj_b, hn_buf, xprev, hcar, st, st0, obuf, *, tc):
    j = pl.program_id(1)
    n_phases = 2 + tc // HG_CHUNK
    cols = IN_COLS // n_phases

    def project(x, sc, sh, out_ref):
        def chunk(c):
            if c == 0:
                hn_buf[...] = (_rms(x(), g_ref[...]) * (1.0 + sc[...]) + sh[...]).astype(BF16)
            out_ref[:, c * cols:(c + 1) * cols] = jnp.dot(hn_buf[...], win_ref[:, c * cols:(c + 1) * cols],
                                                          preferred_element_type=F32)
        return [functools.partial(chunk, c) for c in range(n_phases)]

    def mix(proj_ref, first_row, seq_start, side_work):
        _mix_tile(proj_ref, cw_ref, cb_ref, wg_ref, bg_ref, lam_ref, lb_ref, hgn_ref,
                  m_ref.at[pl.ds(first_row, tc), :], xprev, hcar, st, st0, obuf, seq_start, tc, side_work)

    @pl.when((pl.program_id(0) == 0) & (j == 0))
    def _():
        for thunk in project(lambda: x_ref[0:tc, :], sc_ref, sh_ref, proj_a):
            thunk()

    @pl.when(j == 0)
    def _():
        xprev[...] = jnp.zeros_like(xprev)
        hcar[...] = jnp.zeros_like(hcar)
        st[...] = jnp.zeros_like(st)

    mix(proj_a, 0, j == 0, project(lambda: x_ref[tc:2 * tc, :], sc_ref, sh_ref, proj_b))
    mix(proj_b, tc, None, project(lambda: xn_ref[...], scn_ref, shn_ref, proj_a))

    @pl.when(j == pl.num_programs(1) - 1)
    def _():
        conv_ref[...] = xprev[SUBLANES - (CONV_WIDTH - 1):SUBLANES, :]
        h_ref[...] = hcar[0:1, :]
        for hd in range(HG_HEADS):
            s_ref[hd] = st[hd].T


def _mixer_prompt(x, sc, sh, g, w_in_bf, nb, seq, mw, tc):
    cw, cb, wg, bg, lam, lb, hgn = mw
    nj = seq // (2 * tc)
    last_tile = nb * seq // tc - 1
    next_tile = lambda b, j: jnp.minimum(2 * (b * nj + j) + 2, last_tile)
    const = lambda shape: pl.BlockSpec(shape, lambda b, j: tuple(0 for _ in shape))
    per_seq = lambda: pl.BlockSpec((None, 1, D_MODEL), lambda b, j: (b, 0, 0))
    next_seq = lambda: pl.BlockSpec((None, 1, D_MODEL), lambda b, j: (next_tile(b, j) // (2 * nj), 0, 0))
    return pl.pallas_call(
        functools.partial(_mixer_prompt_kernel, tc=tc),
        out_shape=(jax.ShapeDtypeStruct((nb * seq, D_MODEL), BF16),
                   jax.ShapeDtypeStruct((nb, CONV_WIDTH - 1, LRU_WIDTH), F32),
                   jax.ShapeDtypeStruct((nb, 1, LRU_WIDTH), F32),
                   jax.ShapeDtypeStruct((nb, HG_HEADS, HG_HEAD_DIM, HG_HEAD_DIM), F32)),
        grid=(nb, nj),
        in_specs=[pl.BlockSpec((2 * tc, D_MODEL), lambda b, j: (b * nj + j, 0)),
                  pl.BlockSpec((tc, D_MODEL), lambda b, j: (next_tile(b, j), 0)),
                  per_seq(), per_seq(), next_seq(), next_seq(), const((1, D_MODEL)), const((D_MODEL, IN_COLS)),
                  const((CONV_WIDTH, LRU_WIDTH)), const((1, LRU_WIDTH)),
                  const((LRU_WIDTH, 2 * LRU_WIDTH)), const((1, 2 * LRU_WIDTH)),
                  const((1, LRU_WIDTH)), const((1, HG_WIDTH)), const((1, HG_HEAD_DIM))],
        out_specs=(pl.BlockSpec((2 * tc, D_MODEL), lambda b, j: (b * nj + j, 0)),
                   pl.BlockSpec((None, CONV_WIDTH - 1, LRU_WIDTH), lambda b, j: (b, 0, 0)),
                   pl.BlockSpec((None, 1, LRU_WIDTH), lambda b, j: (b, 0, 0)),
                   pl.BlockSpec((None, HG_HEADS, HG_HEAD_DIM, HG_HEAD_DIM), lambda b, j: (b, 0, 0, 0))),
        scratch_shapes=[pltpu.VMEM((tc, IN_COLS), F32),
                        pltpu.VMEM((tc, IN_COLS), F32),
                        pltpu.VMEM((tc, D_MODEL), BF16),
                        pltpu.VMEM((SUBLANES, LRU_WIDTH), F32),
                        pltpu.VMEM((SUBLANES, LRU_WIDTH), F32),
                        pltpu.VMEM((HG_HEADS, HG_HEAD_DIM, HG_HEAD_DIM), F32),
                        pltpu.VMEM((HG_HEADS, HG_HEAD_DIM, HG_HEAD_DIM), F32),
                        pltpu.VMEM((tc, HG_WIDTH), F32)],
        compiler_params=_cparams(("arbitrary", "arbitrary")),
    )(x, x, sc, sh, sc, sh, g, w_in_bf, cw, cb, wg, bg, lam, lb, hgn)


def _mixer_sample_kernel(proj_ref, cst_ref, h0_ref, s0_ref, cw_ref, cb_ref, wg_ref, bg_ref, lam_ref, lb_ref,
                         hgn_ref, m_ref, conv_ref, h_ref, s_ref, pbuf, xbuf, *, steps):
    for bi in range(pbuf.shape[0]):
        seq_rows = slice(bi * steps, (bi + 1) * steps)
        pbuf[bi, 0:steps, :] = proj_ref[seq_rows, :]
        pbuf[bi, steps:SUBLANES, :] = jnp.zeros((SUBLANES - steps, IN_COLS), F32)
        _mixer_sample_one(pbuf.at[bi], cst_ref.at[bi], h0_ref.at[bi], s0_ref.at[bi], cw_ref, cb_ref, wg_ref,
                          bg_ref, lam_ref, lb_ref, hgn_ref, m_ref, seq_rows, conv_ref.at[bi], h_ref.at[bi],
                          s_ref.at[bi], xbuf.at[bi], steps)


def _mixer_sample_one(proj_ref, cst_ref, h0_ref, s0_ref, cw_ref, cb_ref, wg_ref, bg_ref, lam_ref, lb_ref,
                      hgn_ref, m_ref, m_rows, conv_ref, h_ref, s_ref, xbuf, steps):
    w = LRU_WIDTH
    rows = SUBLANES
    row = lax.broadcasted_iota(I32, (rows, w), 0)

    xl = proj_ref[:, C_XL:C_XL + w]
    xbuf[0:CONV_HDR, :] = cst_ref[...]
    xbuf[CONV_HDR:CONV_HDR + rows, :] = xl
    xc = _conv4(xbuf, xl, cw_ref, cb_ref, rows)
    conv_ref[...] = xbuf[CONV_HDR + steps - 3:CONV_HDR + steps, :]
    a, mult, ig = _lru_gates(xc, wg_ref, bg_ref, lam_ref)
    u = mult * ig * xc
    h = h0_ref[...]
    hs = jnp.zeros((rows, w), F32)
    for t in range(steps):
        h = a[t:t + 1, :] * h + u[t:t + 1, :]
        hs = jnp.where(row == t, h, hs)
    h_ref[...] = h
    m_ref[m_rows, 0:w] = (hs * _gelu_tanh(proj_ref[:, C_YL:C_YL + w]))[0:steps, :]

    lb = lb_ref[...]
    f = lb + (1.0 - lb) * _sigmoid(proj_ref[:, C_F:C_F + w])
    k = 1.0 - f
    b = _tile_cumsum(jnp.log(f))
    q = proj_ref[:, C_Q:C_Q + w]
    v = proj_ref[:, C_V:C_V + w]
    hgn = hgn_ref[...]
    for hd in range(HG_HEADS):
        cs = slice(hd * HG_HEAD_DIM, (hd + 1) * HG_HEAD_DIM)
        o, s_new = _hgrn_window(q[:, cs], k[:, cs], v[:, cs], b[:, cs], s0_ref[hd], steps, False)
        s_ref[hd] = s_new
        gate = proj_ref[:, C_G + hd * HG_HEAD_DIM:C_G + (hd + 1) * HG_HEAD_DIM]
        m_ref[m_rows, w + hd * HG_HEAD_DIM:w + (hd + 1) * HG_HEAD_DIM] = _head_norm_gate(o, gate, hgn)[0:steps, :]


def _mixer_sample(proj, cst_hdr, h0, s0, mw, steps):
    cw, cb, wg, bg, lam, lb, hgn = mw
    nb = proj.shape[0] // steps
    per_step = _pick_tile(nb, (8, 4, 2, 1))
    const = lambda shape: pl.BlockSpec(shape, lambda b: tuple(0 for _ in shape))
    per_b = lambda shape: pl.BlockSpec((per_step,) + shape, lambda b: (b,) + tuple(0 for _ in shape))
    return pl.pallas_call(
        functools.partial(_mixer_sample_kernel, steps=steps),
        out_shape=(jax.ShapeDtypeStruct((nb * steps, D_MODEL), F32),
                   jax.ShapeDtypeStruct((nb, CONV_WIDTH - 1, LRU_WIDTH), F32),
                   jax.ShapeDtypeStruct((nb, 1, LRU_WIDTH), F32),
                   jax.ShapeDtypeStruct((nb, HG_HEADS, HG_HEAD_DIM, HG_HEAD_DIM), F32)),
        grid=(nb // per_step,),
        in_specs=[pl.BlockSpec((per_step * steps, IN_COLS), lambda b: (b, 0)),
                  per_b((CONV_HDR, LRU_WIDTH)), per_b((1, LRU_WIDTH)),
                  per_b((HG_HEADS, HG_HEAD_DIM, HG_HEAD_DIM)),
                  const((CONV_WIDTH, LRU_WIDTH)), const((1, LRU_WIDTH)),
                  const((LRU_WIDTH, 2 * LRU_WIDTH)), const((1, 2 * LRU_WIDTH)),
                  const((1, LRU_WIDTH)), const((1, HG_WIDTH)), const((1, HG_HEAD_DIM))],
        out_specs=(pl.BlockSpec((per_step * steps, D_MODEL), lambda b: (b, 0)),
                   per_b((CONV_WIDTH - 1, LRU_WIDTH)), per_b((1, LRU_WIDTH)),
                   per_b((HG_HEADS, HG_HEAD_DIM, HG_HEAD_DIM))),
        scratch_shapes=[pltpu.VMEM((per_step, SUBLANES, IN_COLS), F32),
                        pltpu.VMEM((per_step, CONV_HDR + SUBLANES, LRU_WIDTH), F32)],
        compiler_params=_cparams(("arbitrary",)),
    )(proj, cst_hdr, h0, s0, cw, cb, wg, bg, lam, lb, hgn)


def _to_row_tiles(ref, val, rows):
    for c in range(LANE_CHUNKS):
        ref[pl.ds(c, rows, stride=LANE_CHUNKS), :] = val[:, c * LANES:(c + 1) * LANES]


def _from_row_tiles(ref, base, rows):
    return jnp.concatenate(
        [ref[pl.ds(base + c, rows, stride=LANE_CHUNKS), :] for c in range(LANE_CHUNKS)], axis=-1)


def _post_kernel(mp_ref, ms_ref, xp_ref, xs_ref, gtmp_ref, scfp_ref, shfp_ref, gtms_ref, scfs_ref, shfs_ref,
                 npost_ref, npre_ref, wout_ref, rwt_ref, rb_ref,
                 x1_ref, hf_ref, idx_ref, gate_ref, rank_ref, cnt_out_ref, cnt, *, n_p):
    i = pl.program_id(0)
    tm = xp_ref.shape[0]

    @pl.when(i == 0)
    def _():
        cnt[...] = jnp.zeros_like(cnt)

    def run(m_ref, x_ref, gtm_ref, scf_ref, shf_ref):
        mix = jnp.dot(m_ref[...].astype(BF16), wout_ref[...], preferred_element_type=F32)
        x1 = x_ref[...] + gtm_ref[...] * _rms(mix, npost_ref[...])
        x1_ref[...] = x1
        hf = _rms(x1, npre_ref[...]) * (1.0 + scf_ref[...]) + shf_ref[...]
        _to_row_tiles(hf_ref, hf, tm)
        logits = lax.dot_general(rwt_ref[...], hf, _NT, precision=lax.Precision.HIGHEST,
                                 preferred_element_type=F32) + rb_ref[...]
        e_iota = lax.broadcasted_iota(I32, logits.shape, 0)
        vals, idxs = [], []
        for _ in range(TOP_K):
            mx = jnp.max(logits, axis=0, keepdims=True)
            ix = jnp.min(jnp.where(logits == mx, e_iota, N_EXPERTS), axis=0, keepdims=True)
            vals.append(mx)
            idxs.append(ix)
            logits = jnp.where(e_iota == ix, -jnp.inf, logits)
        exps = [jnp.exp(vv - vals[0]) for vv in vals]
        den = exps[0] + exps[1] + exps[2] + exps[3]
        earlier = (lax.broadcasted_iota(I32, (tm, tm), 0) < lax.broadcasted_iota(I32, (tm, tm), 1))
        earlier = jnp.where(earlier, 1.0, 0.0).astype(BF16)
        base = cnt[...]
        for kk in range(TOP_K):
            idx_ref[kk:kk + 1, :] = idxs[kk]
            gate_ref[kk:kk + 1, :] = exps[kk] / den
            hit = e_iota == idxs[kk]
            onehot = jnp.where(hit, 1.0, 0.0)
            before = jnp.dot(onehot.astype(BF16), earlier, preferred_element_type=F32)
            rank = jnp.sum(jnp.where(hit, before + base, 0.0), axis=0, keepdims=True)
            rank_ref[kk:kk + 1, :] = rank.astype(I32)
            base = base + jnp.sum(onehot, axis=1, keepdims=True)
        cnt[...] = base

    @pl.when(i < n_p)
    def _():
        run(mp_ref, xp_ref, gtmp_ref, scfp_ref, shfp_ref)

    @pl.when(i >= n_p)
    def _():
        run(ms_ref, xs_ref, gtms_ref, scfs_ref, shfs_ref)

    @pl.when(i == pl.num_programs(0) - 1)
    def _():
        cnt_out_ref[...] = cnt[...]


def _post(m_p, m_s, x_p, x_s, mod_p, mod_s, npost, npre, wout_bf, rwt, rb, tm, tiles_per_seq):
    n_p, n_s = x_p.shape[0] // tm, x_s.shape[0] // tm
    t = (n_p + n_s) * tm
    first = lambda i: jnp.minimum(i, n_p - 1)
    second = lambda i: jnp.maximum(i - n_p, 0)
    row = lambda: pl.BlockSpec((1, D_MODEL), lambda i: (0, 0))
    tok_p = lambda: pl.BlockSpec((tm, D_MODEL), lambda i: (first(i), 0))
    tok_s = lambda: pl.BlockSpec((tm, D_MODEL), lambda i: (second(i), 0))
    seq_p = lambda: pl.BlockSpec((None, 1, D_MODEL), lambda i: (first(i) // tiles_per_seq, 0, 0))
    per_k = lambda: pl.BlockSpec((TOP_K, tm), lambda i: (0, i))
    return pl.pallas_call(
        functools.partial(_post_kernel, n_p=n_p),
        out_shape=(jax.ShapeDtypeStruct((t, D_MODEL), F32),
                   jax.ShapeDtypeStruct((t * LANE_CHUNKS, LANES), F32),
                   jax.ShapeDtypeStruct((TOP_K, t), I32), jax.ShapeDtypeStruct((TOP_K, t), F32),
                   jax.ShapeDtypeStruct((TOP_K, t), I32), jax.ShapeDtypeStruct((N_EXPERTS, 1), F32)),
        grid=(n_p + n_s,),
        in_specs=[tok_p(), tok_s(), tok_p(), tok_s(), seq_p(), seq_p(), seq_p(), tok_s(), tok_s(), tok_s(),
                  row(), row(),
                  pl.BlockSpec((D_MODEL, D_MODEL), lambda i: (0, 0)),
                  pl.BlockSpec((N_EXPERTS, D_MODEL), lambda i: (0, 0)),
                  pl.BlockSpec((N_EXPERTS, 1), lambda i: (0, 0))],
        out_specs=(pl.BlockSpec((tm, D_MODEL), lambda i: (i, 0)),
                   pl.BlockSpec((tm * LANE_CHUNKS, LANES), lambda i: (i, 0)),
                   per_k(), per_k(), per_k(), pl.BlockSpec((N_EXPERTS, 1), lambda i: (0, 0))),
        scratch_shapes=[pltpu.VMEM((N_EXPERTS, 1), F32)],
        compiler_params=_cparams(("arbitrary",)),
    )(m_p, m_s, x_p, x_s, *mod_p, *mod_s, npost, npre, wout_bf, rwt, rb)


def _dispatch_kernel(fill_lo_ref, fill_hi_ref, dest_ref, hf_hbm, xs_hbm, zeros, src, tile_sem, row_sem, pad_sem,
                     *, tm):
    i = pl.program_id(0)
    n_rows = TOP_K * tm

    @pl.when(i == 0)
    def _():
        zeros[...] = jnp.zeros_like(zeros)

        def per_expert(e, totals):
            lo, hi = fill_lo_ref[e], fill_hi_ref[e]
            mid = jnp.minimum((lo + ZERO_ROWS - 1) // ZERO_ROWS * ZERO_ROWS, hi)

            def per_row(s, carry):
                pltpu.make_async_copy(zeros.at[0], xs_hbm.at[s], pad_sem).start()
                return carry
            lax.fori_loop(lo, mid, per_row, 0)

            def per_chunk(c, carry):
                pltpu.make_async_copy(zeros, xs_hbm.at[pl.ds(pl.multiple_of(c * ZERO_ROWS, ZERO_ROWS), ZERO_ROWS)],
                                      pad_sem).start()
                return carry
            lax.fori_loop(mid // ZERO_ROWS, hi // ZERO_ROWS, per_chunk, 0)
            return totals[0] + (mid - lo), totals[1] + (hi - mid) // ZERO_ROWS
        n_rows_filled, n_chunks_filled = lax.fori_loop(0, N_EXPERTS, per_expert, (0, 0))

        def drain_row(s, carry):
            pltpu.make_async_copy(zeros.at[0], xs_hbm.at[0], pad_sem).wait()
            return carry
        lax.fori_loop(0, n_rows_filled, drain_row, 0)

        def drain_chunk(c, carry):
            pltpu.make_async_copy(zeros, xs_hbm.at[pl.ds(0, ZERO_ROWS)], pad_sem).wait()
            return carry
        lax.fori_loop(0, n_chunks_filled, drain_chunk, 0)

    n_tiles = pl.num_programs(0)
    tile_len = tm * LANE_CHUNKS

    def tile_copy(t):
        s = lax.rem(t, DISPATCH_BUFS)
        return pltpu.make_async_copy(hf_hbm.at[pl.ds(pl.multiple_of(t * tile_len, tile_len), tile_len), :],
                                     src.at[s], tile_sem.at[s])

    def wait_rows(t):
        pltpu.make_async_copy(xs_hbm.at[pl.ds(0, n_rows)], xs_hbm.at[pl.ds(0, n_rows)],
                              row_sem.at[lax.rem(t, DISPATCH_BUFS)]).wait()

    @pl.when(i == 0)
    def _():
        tile_copy(0).start()

    @pl.when(i + 1 < n_tiles)
    def _():
        tile_copy(i + 1).start()

    tile_copy(i).wait()
    slot = lax.rem(i, DISPATCH_BUFS)
    tile = src.at[slot]
    for n in range(n_rows):
        pltpu.make_async_copy(tile.at[pl.ds((n % tm) * LANE_CHUNKS, LANE_CHUNKS), :],
                              xs_hbm.at[dest_ref[0, n]], row_sem.at[slot]).start(priority=n % 2)

    @pl.when(i > 0)
    def _():
        wait_rows(i - 1)

    @pl.when(i == n_tiles - 1)
    def _():
        wait_rows(i)


def _dispatch(fill_lo, fill_hi, dest_tiles, hf_rows, n_slots, tm):
    return pl.pallas_call(
        functools.partial(_dispatch_kernel, tm=tm),
        out_shape=jax.ShapeDtypeStruct((n_slots, LANE_CHUNKS, LANES), F32),
        grid_spec=pltpu.PrefetchScalarGridSpec(
            num_scalar_prefetch=2, grid=(dest_tiles.shape[0],),
            in_specs=[pl.BlockSpec((None, 1, TOP_K * tm), lambda i, lo, hi: (i, 0, 0), memory_space=pltpu.SMEM),
                      pl.BlockSpec(memory_space=pl.ANY)],
            out_specs=pl.BlockSpec(memory_space=pl.ANY),
            scratch_shapes=[pltpu.VMEM((ZERO_ROWS, LANE_CHUNKS, LANES), F32),
                            pltpu.VMEM((DISPATCH_BUFS, tm * LANE_CHUNKS, LANES), F32),
                            pltpu.SemaphoreType.DMA((DISPATCH_BUFS,)),
                            pltpu.SemaphoreType.DMA((DISPATCH_BUFS,)),
                            pltpu.SemaphoreType.DMA]),
        compiler_params=_cparams(("arbitrary",)),
    )(fill_lo, fill_hi, dest_tiles, hf_rows)


def _expert_kernel(be_ref, half_ref, nxt_ref, nu_ref, xs_ref, wgu_hbm, bgu_ref, wd_hbm, bd_ref, ys_ref,
                   wgu_f32, wd_f32, wgu_bf, wd_bf, w_sem):
    i = pl.program_id(0)
    n_used = nu_ref[0]

    def weight_copies(e):
        return (pltpu.make_async_copy(wgu_hbm.at[e], wgu_f32, w_sem.at[0]),
                pltpu.make_async_copy(wd_hbm.at[e], wd_f32, w_sem.at[1]))

    @pl.when(i == 0)
    def _():
        for cp in weight_copies(be_ref[0]):
            cp.start()

    @pl.when(i < n_used)
    def _():
        prev = be_ref[jnp.maximum(i - 1, 0)]

        @pl.when((i == 0) | (be_ref[i] != prev))
        def _():
            for cp in weight_copies(be_ref[i]):
                cp.wait()
            wgu_bf[...] = wgu_f32[...].astype(BF16)
            wd_bf[...] = wd_f32[...].astype(BF16)
            nxt = nxt_ref[be_ref[i]]

            @pl.when(nxt >= 0)
            def _():
                for cp in weight_copies(nxt):
                    cp.start()

        def swiglu_rows(rows):
            x = _from_row_tiles(xs_ref, 0, rows).astype(BF16)
            hu = jnp.dot(x, wgu_bf[...], preferred_element_type=F32) + bgu_ref[...]
            glu = jnp.minimum(hu[:, :D_EXPERT], SWIGLU_LIMIT)
            lin = jnp.clip(hu[:, D_EXPERT:], -SWIGLU_LIMIT, SWIGLU_LIMIT)
            act = glu * _sigmoid(SWIGLU_ALPHA * glu) * (lin + 1.0)
            y = jnp.dot(act.astype(BF16), wd_bf[...], preferred_element_type=F32) + bd_ref[...]
            _to_row_tiles(ys_ref, y, rows)

        @pl.when(half_ref[i] == 0)
        def _():
            swiglu_rows(MOE_BM)

        @pl.when(half_ref[i] != 0)
        def _():
            swiglu_rows(MOE_BM // 2)
            ys_ref[MOE_BM // 2 * LANE_CHUNKS:, :] = jnp.zeros((MOE_BM // 2 * LANE_CHUNKS, LANES), F32)

    @pl.when(i >= n_used)
    def _():
        ys_ref[...] = jnp.zeros_like(ys_ref)


def _experts(block_e, block_half, next_e, n_used, xs2d, w_gu, b_gu, w_down, b_down):
    n_blocks = block_e.shape[0]
    rows = MOE_BM * LANE_CHUNKS
    live = lambda i, be, hf, nx, nu: (jnp.minimum(i, nu[0] - 1), 0)
    return pl.pallas_call(
        _expert_kernel,
        out_shape=jax.ShapeDtypeStruct(xs2d.shape, F32),
        grid_spec=pltpu.PrefetchScalarGridSpec(
            num_scalar_prefetch=4,
            grid=(n_blocks,),
            in_specs=[pl.BlockSpec((rows, LANES), live),
                      pl.BlockSpec(memory_space=pl.ANY),
                      pl.BlockSpec((None, 1, 2 * D_EXPERT), lambda i, be, hf, nx, nu: (be[i], 0, 0)),
                      pl.BlockSpec(memory_space=pl.ANY),
                      pl.BlockSpec((None, 1, D_MODEL), lambda i, be, hf, nx, nu: (be[i], 0, 0))],
            out_specs=pl.BlockSpec((rows, LANES), lambda i, be, hf, nx, nu: (i, 0)),
            scratch_shapes=[pltpu.VMEM((D_MODEL, 2 * D_EXPERT), F32),
                            pltpu.VMEM((D_EXPERT, D_MODEL), F32),
                            pltpu.VMEM((D_MODEL, 2 * D_EXPERT), BF16),
                            pltpu.VMEM((D_EXPERT, D_MODEL), BF16),
                            pltpu.SemaphoreType.DMA((2,))]),
        compiler_params=_cparams(("arbitrary",)),
    )(block_e, block_half, next_e, n_used, xs2d, w_gu, b_gu, w_down, b_down)


def _combine_kernel(dcur_ref, dnext_ref, ys_hbm, g_ref, x1_ref, gt_ref, np_ref, o_ref, ybuf, row_sem, *, tm):
    i = pl.program_id(0)
    n_rows = TOP_K * tm
    slot = i % 2

    def slot_rows(s):
        return ybuf.at[pl.ds(pl.multiple_of(s * n_rows * LANE_CHUNKS, n_rows * LANE_CHUNKS), n_rows * LANE_CHUNKS), :]

    def issue(d_ref, s):
        dst = slot_rows(s)
        for n in range(n_rows):
            pltpu.make_async_copy(ys_hbm.at[d_ref[0, n]], dst.at[pl.ds(n * LANE_CHUNKS, LANE_CHUNKS), :],
                                  row_sem.at[s]).start(priority=n % 2)

    @pl.when(i == 0)
    def _():
        issue(dcur_ref, 0)

    @pl.when(i + 1 < pl.num_programs(0))
    def _():
        issue(dnext_ref, 1 - slot)

    pltpu.make_async_copy(slot_rows(slot), slot_rows(slot), row_sem.at[slot]).wait()
    g = g_ref[...]
    y = None
    for kk in range(TOP_K):
        rows = _from_row_tiles(ybuf, (slot * n_rows + kk * tm) * LANE_CHUNKS, tm)
        y = g[:, kk:kk + 1] * rows if y is None else y + g[:, kk:kk + 1] * rows
    o_ref[...] = x1_ref[...] + gt_ref[...] * _rms(y, np_ref[...])


def _combine(dest_tiles, ys_tiles, gates_t, x1, gt, npost, tm, tiles_per_seq, tile0):
    n_tiles = dest_tiles.shape[0]
    t = n_tiles * tm
    smem_tile = lambda f: pl.BlockSpec((None, 1, TOP_K * tm), f, memory_space=pltpu.SMEM)
    return pl.pallas_call(
        functools.partial(_combine_kernel, tm=tm),
        out_shape=jax.ShapeDtypeStruct((t, D_MODEL), F32),
        grid=(n_tiles,),
        in_specs=[smem_tile(lambda i: (i, 0, 0)),
                  smem_tile(lambda i: (jnp.minimum(i + 1, n_tiles - 1), 0, 0)),
                  pl.BlockSpec(memory_space=pl.ANY),
                  pl.BlockSpec((tm, TOP_K), lambda i: (i + tile0, 0)),
                  pl.BlockSpec((tm, D_MODEL), lambda i: (i + tile0, 0)),
                  _mod_spec(gt, tm, tiles_per_seq),
                  pl.BlockSpec((1, D_MODEL), lambda i: (0, 0))],
        out_specs=pl.BlockSpec((tm, D_MODEL), lambda i: (i, 0)),
        scratch_shapes=[pltpu.VMEM((2 * TOP_K * tm * LANE_CHUNKS, LANES), F32),
                        pltpu.SemaphoreType.DMA((2,))],
        compiler_params=_cparams(("arbitrary",)),
    )(dest_tiles, dest_tiles, ys_tiles, gates_t, x1, gt, npost)


def _expert_layout(counts, n_blocks):
    padded = (counts + MOE_BM - 1) // MOE_BM * MOE_BM
    pad_ends = jnp.cumsum(padded)
    pad_starts = pad_ends - padded
    block_start = jnp.arange(n_blocks, dtype=I32) * MOE_BM
    block_e = jnp.minimum(jnp.sum((pad_ends[None, :] <= block_start[:, None]).astype(I32), axis=1), N_EXPERTS - 1)
    n_used = (pad_ends[-1:] // MOE_BM).astype(I32)
    e_ids = jnp.arange(N_EXPERTS, dtype=I32)
    rows_end = jnp.sum(jnp.where(block_e[:, None] == e_ids[None, :], (pad_starts + counts)[None, :], 0), axis=1)
    block_half = (rows_end - block_start <= MOE_BM // 2).astype(I32)
    later = (counts > 0)[None, :] & (e_ids[None, :] > e_ids[:, None])
    next_e = jnp.min(jnp.where(later, e_ids[None, :], N_EXPERTS), axis=1)
    next_e = jnp.where(next_e < N_EXPERTS, next_e, -1).astype(I32)
    return pad_starts, pad_starts + counts, pad_ends, block_e, block_half, next_e, n_used


def _slots(idx, rank, pad_starts):
    onehot = idx[:, :, None] == jnp.arange(N_EXPERTS, dtype=I32)[None, None, :]
    return rank + jnp.sum(jnp.where(onehot, pad_starts[None, None, :], 0), axis=-1)


def _tile_major(a, tm):
    t = a.shape[1]
    return a.reshape(TOP_K, t // tm, tm).transpose(1, 0, 2).reshape(t // tm, 1, TOP_K * tm)


def _block_diag(wb):
    n, d, _ = wb.shape
    eye = jnp.eye(n, dtype=wb.dtype)
    return (wb[:, :, None, :] * eye[:, None, :, None]).reshape(n * d, n * d)


def _pick_tile(n, prefs):
    for p in prefs:
        if n % p == 0:
            return p
    raise ValueError(f"no tile for {n}")


def kernel(x_prompt, x_sample, state_rglru_conv, state_rglru_h, state_hgrn_S, c_prompt, c_sample, ada_w, ada_b, norm_pre_mix, norm_post_mix, norm_pre_ffn, norm_post_ffn, w_in, conv_w, conv_b, lru_wa, lru_ba, lru_wi, lru_bi, lru_lambda, hg_lb, hg_norm, w_out, router_w, router_b, w_gu, b_gu, w_down, b_down):
    assert ada_w.shape[0] == 1, "single-layer trunk"
    bp, lp, _ = x_prompt.shape
    bs, ls, _ = x_sample.shape
    tp, ts = bp * lp, bs * ls
    assert CONV_WIDTH - 1 <= ls <= SUBLANES

    w_in_bf = w_in[0].astype(BF16)
    w_out_bf = w_out[0].astype(BF16)
    wg = jnp.concatenate([_block_diag(lru_wa[0]), _block_diag(lru_wi[0])], axis=1).astype(BF16)
    bg = jnp.concatenate([lru_ba[0], lru_bi[0]])[None, :]
    lbs = jnp.cumsum(jax.nn.softmax(hg_lb.astype(F32), axis=0), axis=0)[0][None, :]
    mw = (conv_w[0], conv_b[0][None, :], wg, bg, lru_lambda[0][None, :], lbs, hg_norm[0][None, :])
    row = lambda p: p[0][None, :]

    n_c = bp + bs
    n_c_pad = -(-n_c // SUBLANES) * SUBLANES
    c_all = jnp.concatenate([c_prompt, c_sample, jnp.zeros((n_c_pad - n_c, D_MODEL), F32)], axis=0)
    mod = _ada(c_all, ada_w[0], ada_b[0][None, :])
    mod_p = [m[:, None, :] for m in jnp.split(mod[:bp], 6, axis=-1)]
    mod_s = [jnp.repeat(m, ls, axis=0) for m in jnp.split(mod[bp:n_c], 6, axis=-1)]

    xp = x_prompt.reshape(tp, D_MODEL)
    xs = x_sample.reshape(ts, D_MODEL)
    tm_s = _pick_tile(ts, (256, 128, 64, 32, 16, 8))

    tc = _pick_tile(lp // 2, (256, 128, 64))
    m_p, conv_p, h_p, s_p = _mixer_prompt(xp, mod_p[1], mod_p[0], row(norm_pre_mix), w_in_bf, bp, lp, mw, tc)
    proj_s = _inproj(xs, mod_s[1], mod_s[0], row(norm_pre_mix), w_in_bf, tm_s, 1)

    cst_hdr = jnp.pad(state_rglru_conv[0], ((0, 0), (CONV_HDR - (CONV_WIDTH - 1), 0), (0, 0)))
    m_s, conv_s, h_s, s_s = _mixer_sample(proj_s, cst_hdr, state_rglru_h[0][:, None, :], state_hgrn_S[0], mw, ls)

    tmq = _pick_tile(math.gcd(lp, ts), (512, 256, 128, 64, 32, 16, 8))
    x1, hf, idx, gate, rank, counts = _post(
        m_p, m_s, xp, xs, (mod_p[2], mod_p[4], mod_p[3]), (mod_s[2], mod_s[4], mod_s[3]),
        row(norm_post_mix), row(norm_pre_ffn), w_out_bf, router_w[0].T, router_b[0][:, None], tmq, lp // tmq)

    n_blocks = -(-(TOP_K * (tp + ts)) // MOE_BM) + N_EXPERTS
    n_slots = n_blocks * MOE_BM
    pad_starts, fill_lo, pad_ends, block_e, block_half, next_e, n_used = _expert_layout(
        counts[:, 0].astype(I32), n_blocks)
    fill_hi = pad_ends.at[N_EXPERTS - 1].set(n_slots)
    dest = _slots(idx, rank, pad_starts)
    tmd = _pick_tile(math.gcd(lp, ts), (512, 256, 128, 64, 32, 16, 8))
    tmc = _pick_tile(math.gcd(lp, ts), (256, 128, 64, 32, 16, 8))
    xs_rows = _dispatch(fill_lo, fill_hi, _tile_major(dest, tmd), hf, n_slots, tmd)
    ys = _experts(block_e, block_half, next_e, n_used, xs_rows.reshape(n_slots * LANE_CHUNKS, LANES),
                  w_gu[0], b_gu[0][:, None, :], w_down[0], b_down[0][:, None, :])
    ys_tiles = ys.reshape(-1, LANE_CHUNKS, LANES)

    gates_t = gate.T
    ctiles = _tile_major(dest, tmc)
    y_p = _combine(ctiles[:tp // tmc], ys_tiles, gates_t, x1, mod_p[5], row(norm_post_ffn), tmc, lp // tmc, 0)
    y_s = _combine(ctiles[tp // tmc:], ys_tiles, gates_t, x1, mod_s[5], row(norm_post_ffn), tmc, 1, tp // tmc)

    return (y_p.reshape(bp, lp, D_MODEL), y_s.reshape(bs, ls, D_MODEL),
            conv_p[None], h_p.reshape(1, bp, LRU_WIDTH), s_p[None],
            conv_s[None], h_s.reshape(1, bs, LRU_WIDTH), s_s[None])
```

```python
import functools
import math

import jax
import jax.numpy as jnp
from jax import lax
from jax.experimental import pallas as pl
from jax.experimental.pallas import tpu as pltpu

F32 = jnp.float32
BF16 = jnp.bfloat16
I32 = jnp.int32

D_MODEL = 1024
LRU_WIDTH = 512
CONV_WIDTH = 4
LRU_C = 8.0
HG_WIDTH = 512
HG_HEAD_DIM = 128
HG_HEADS = 4
IN_COLS = 2 * LRU_WIDTH + 4 * HG_WIDTH
N_EXPERTS = 32
TOP_K = 4
D_EXPERT = 1024
SWIGLU_LIMIT = 7.0
SWIGLU_ALPHA = 1.702
RMS_EPS = 1e-6

C_XL, C_YL, C_Q, C_F, C_V, C_G = 0, 512, 1024, 1536, 2048, 2560

SUBLANES = 8
LANES = 128
LANE_CHUNKS = D_MODEL // LANES
CONV_HDR = SUBLANES
HG_CHUNK = 64
HG_MAX_EXPONENT = 60.0
MOE_BM = 512
DISPATCH_BUFS = 3
ZERO_ROWS = 64
VMEM_LIMIT = 56 * 1024 * 1024

_NT = (((1,), (1,)), ((), ()))
_TN = (((0,), (0,)), ((), ()))


def _cparams(sem):
    return pltpu.CompilerParams(dimension_semantics=sem, vmem_limit_bytes=VMEM_LIMIT)


def _rms(x, g):
    return x * lax.rsqrt(jnp.mean(x * x, axis=-1, keepdims=True) + RMS_EPS) * g


def _gelu_tanh(x):
    c = math.sqrt(2.0 / math.pi)
    return 0.5 * x * (1.0 + jnp.tanh(c * (x + 0.044715 * (x * x * x))))


def _ada_kernel(c_ref, w_ref, b_ref, o_ref):
    c = c_ref[...]
    s = c * jax.nn.sigmoid(c)
    o_ref[...] = jnp.dot(s.astype(BF16), w_ref[...].astype(BF16), preferred_element_type=F32) + b_ref[...]


def _ada(c_all, ada_w, ada_b):
    n = c_all.shape[0]
    tn = 1024
    return pl.pallas_call(
        _ada_kernel,
        out_shape=jax.ShapeDtypeStruct((n, 6 * D_MODEL), F32),
        grid=(6 * D_MODEL // tn,),
        in_specs=[pl.BlockSpec((n, D_MODEL), lambda j: (0, 0)),
                  pl.BlockSpec((D_MODEL, tn), lambda j: (0, j)),
                  pl.BlockSpec((1, tn), lambda j: (0, j))],
        out_specs=pl.BlockSpec((n, tn), lambda j: (0, j)),
        compiler_params=_cparams(("arbitrary",)),
    )(c_all, ada_w, ada_b)


def _inproj_kernel(x_ref, sc_ref, sh_ref, g_ref, w_ref, o_ref):
    h = _rms(x_ref[...], g_ref[...]) * (1.0 + sc_ref[...]) + sh_ref[...]
    o_ref[...] = jnp.dot(h.astype(BF16), w_ref[...], preferred_element_type=F32)


def _mod_spec(mod, tm, tiles_per_seq):
    if mod.ndim == 3:
        return pl.BlockSpec((None, 1, D_MODEL), lambda i: (i // tiles_per_seq, 0, 0))
    return pl.BlockSpec((tm, D_MODEL), lambda i: (i, 0))


def _inproj(x, sc, sh, g, w_bf, tm, tiles_per_seq):
    t = x.shape[0]
    return pl.pallas_call(
        _inproj_kernel,
        out_shape=jax.ShapeDtypeStruct((t, IN_COLS), F32),
        grid=(t // tm,),
        in_specs=[pl.BlockSpec((tm, D_MODEL), lambda i: (i, 0)),
                  _mod_spec(sc, tm, tiles_per_seq), _mod_spec(sh, tm, tiles_per_seq),
                  pl.BlockSpec((1, D_MODEL), lambda i: (0, 0)),
                  pl.BlockSpec((D_MODEL, IN_COLS), lambda i: (0, 0))],
        out_specs=pl.BlockSpec((tm, IN_COLS), lambda i: (i, 0)),
        compiler_params=_cparams(("arbitrary",)),
    )(x, sc, sh, g, w_bf)


def _sigmoid(x):
    return 0.5 * jnp.tanh(0.5 * x) + 0.5


def _group_rows(x):
    rows, w = x.shape
    return x.reshape(rows // SUBLANES, SUBLANES, w)


def _scan_rows(a, u, h0):
    a3, u3 = _group_rows(a), _group_rows(u)
    r3 = lax.broadcasted_iota(I32, a3.shape, 1)
    s = 1
    while s < SUBLANES:
        keep = r3 >= s
        a_sh = jnp.where(keep, pltpu.roll(a3, s, 1), 1.0)
        u_sh = jnp.where(keep, pltpu.roll(u3, s, 1), 0.0)
        u3 = a3 * u_sh + u3
        a3 = a3 * a_sh
        s *= 2
    out, h = [], h0
    for i in range(a3.shape[0]):
        hg = u3[i] + a3[i] * h
        h = hg[SUBLANES - 1:SUBLANES, :]
        out.append(hg)
    return jnp.concatenate(out, axis=0), h


def _chunk_cumsum(x, chunk):
    x3 = _group_rows(x)
    r3 = lax.broadcasted_iota(I32, x3.shape, 1)
    s = 1
    while s < SUBLANES:
        x3 = x3 + jnp.where(r3 >= s, pltpu.roll(x3, s, 1), 0.0)
        s *= 2
    out, carry = [], None
    for i in range(x3.shape[0]):
        cur = x3[i] if i % (chunk // SUBLANES) == 0 else x3[i] + carry
        carry = cur[SUBLANES - 1:SUBLANES, :]
        out.append(cur)
    return jnp.concatenate(out, axis=0)


def _lru_gates(xc, wg_ref, bg_ref, lam_ref):
    gates = jnp.dot(xc.astype(BF16), wg_ref[...], preferred_element_type=F32) + bg_ref[...]
    r = _sigmoid(gates[:, :LRU_WIDTH])
    ig = _sigmoid(gates[:, LRU_WIDTH:])
    z = -lam_ref[...]
    softplus = jnp.maximum(z, 0.0) + jnp.log1p(jnp.exp(-jnp.abs(z)))
    log_a = -LRU_C * r * softplus
    a = jnp.exp(log_a)
    th = jnp.tanh(log_a)
    mult2 = -2.0 * th / (1.0 - th)
    return a, jnp.sqrt(mult2), ig


def _conv4(xbuf, xl, cw_ref, cb_ref, rows):
    cw = cw_ref[...]
    h = CONV_HDR
    return (cb_ref[...] + cw[0:1] * xbuf[h - 3:h - 3 + rows, :] + cw[1:2] * xbuf[h - 2:h - 2 + rows, :]
            + cw[2:3] * xbuf[h - 1:h - 1 + rows, :] + cw[3:4] * xl)


def _conv4_rolled(xl, prev, cw_ref, cb_ref):
    cw = cw_ref[...]
    row8 = lax.broadcasted_iota(I32, prev.shape, 0)
    acc = cb_ref[...] + cw[CONV_WIDTH - 1:CONV_WIDTH] * xl
    for s in range(1, CONV_WIDTH):
        sh = pltpu.roll(xl, s, 0)
        top = jnp.where(row8 < s, pltpu.roll(prev, s, 0), sh[0:SUBLANES, :])
        sh = jnp.concatenate([top, sh[SUBLANES:, :]], axis=0)
        acc = acc + cw[CONV_WIDTH - 1 - s:CONV_WIDTH - s] * sh
    return acc


def _head_norm_gate(o, gate, hgn):
    return _rms(o, hgn) * _sigmoid(gate)


def _hgrn_window(q, k, v, b, state, steps, state_is_transposed):
    row = lax.broadcasted_iota(I32, q.shape, 0)
    valid = row < steps
    b_end = b[steps - 1:steps, :]
    q_in = jnp.where(valid, q * jnp.exp(b), 0.0).astype(BF16)
    if state_is_transposed:
        o = lax.dot_general(q_in, state.astype(BF16), _NT, preferred_element_type=F32)
    else:
        o = jnp.dot(q_in, state.astype(BF16), preferred_element_type=F32)
    for d in range(steps):
        ok = valid & (row >= d)
        k_d = k if d == 0 else pltpu.roll(k, d, 0)
        v_d = v if d == 0 else pltpu.roll(v, d, 0)
        b_d = b if d == 0 else pltpu.roll(b, d, 0)
        decay = jnp.exp(jnp.where(ok, b - b_d, 0.0))
        wgt = jnp.sum(jnp.where(ok, q * k_d * decay, 0.0), axis=-1, keepdims=True)
        o = o + wgt * v_d
    k_out = jnp.where(valid, k * jnp.exp(jnp.where(valid, b_end - b, 0.0)), 0.0).astype(BF16)
    v_ok = jnp.where(valid, v, 0.0).astype(BF16)
    keep = jnp.exp(b_end)
    if state_is_transposed:
        s_new = state * keep + lax.dot_general(v_ok, k_out, _TN, preferred_element_type=F32)
    else:
        keep_col = jnp.broadcast_to(keep, q.shape).T[:, 0:1]
        s_new = state * keep_col + lax.dot_general(k_out, v_ok, _TN, preferred_element_type=F32)
    return o, s_new


def _tile_cumsum(x):
    row = lax.broadcasted_iota(I32, x.shape, 0)
    s = 1
    while s < SUBLANES:
        x = x + jnp.where(row >= s, pltpu.roll(x, s, 0), 0.0)
        s *= 2
    return x


def _mix_tile(proj_ref, cw_ref, cb_ref, wg_ref, bg_ref, lam_ref, lb_ref, hgn_ref, m_ref, xprev, hcar, st, st0, obuf,
              seq_start, tc, side_work):
    w = LRU_WIDTH
    side_work = iter(side_work)

    next(side_work)()
    xl = proj_ref[:, C_XL:C_XL + w]
    xc = _conv4_rolled(xl, xprev[...], cw_ref, cb_ref)
    xprev[...] = xl[tc - SUBLANES:tc, :]
    a, mult, ig = _lru_gates(xc, wg_ref, bg_ref, lam_ref)
    row = lax.broadcasted_iota(I32, (tc, w), 0)
    if seq_start is not None:
        mult = jnp.where((row == 0) & seq_start, 1.0, mult)
    h, h_last = _scan_rows(a, mult * ig * xc, hcar[0:1, :])
    hcar[0:1, :] = h_last
    m_ref[:, 0:w] = (h * _gelu_tanh(proj_ref[:, C_YL:C_YL + w])).astype(BF16)

    next(side_work)()
    lb = lb_ref[...]
    f = lb + (1.0 - lb) * _sigmoid(proj_ref[:, C_F:C_F + w])
    k = 1.0 - f
    b = _chunk_cumsum(jnp.log(f), HG_CHUNK)
    q = proj_ref[:, C_Q:C_Q + w]
    v = proj_ref[:, C_V:C_V + w]
    mid = HG_CHUNK // 2
    st0[...] = st[...]
    tri = (lax.broadcasted_iota(I32, (HG_CHUNK, HG_CHUNK), 0)
           >= lax.broadcasted_iota(I32, (HG_CHUNK, HG_CHUNK), 1))
    states = [st[hd] for hd in range(HG_HEADS)]
    for c in range(tc // HG_CHUNK):
        next(side_work)()
        rs = slice(c * HG_CHUNK, (c + 1) * HG_CHUNK)
        for hd in range(HG_HEADS):
            cs = slice(hd * HG_HEAD_DIM, (hd + 1) * HG_HEAD_DIM)
            s_t = states[hd]
            bc, qc, kc = b[rs, cs], q[rs, cs], k[rs, cs]
            vb = v[rs, cs].astype(BF16)
            b_mid = bc[mid - 1:mid, :]
            b_end = bc[HG_CHUNK - 1:HG_CHUNK, :]
            qp = (qc * jnp.exp(bc - b_mid)).astype(BF16)
            kp = (kc * jnp.exp(b_mid - bc)).astype(BF16)
            att = lax.dot_general(qp, kp, _NT, preferred_element_type=F32)
            att = jnp.where(tri, att, 0.0)
            q_in = (qc * jnp.exp(bc)).astype(BF16)
            o = (jnp.dot(att.astype(BF16), vb, preferred_element_type=F32)
                 + lax.dot_general(q_in, s_t.astype(BF16), _NT, preferred_element_type=F32))
            k_out = (kc * jnp.exp(b_end - bc)).astype(BF16)
            states[hd] = s_t * jnp.exp(b_end) + lax.dot_general(vb, k_out, _TN, preferred_element_type=F32)
            obuf[rs, cs] = o
    for hd in range(HG_HEADS):
        st[hd] = states[hd]

    half_decay = None
    for c in range(tc // HG_CHUNK):
        b_mid = b[c * HG_CHUNK + mid - 1:c * HG_CHUNK + mid, :]
        b_end = b[(c + 1) * HG_CHUNK - 1:(c + 1) * HG_CHUNK, :]
        worst = jnp.maximum(-b_mid, b_mid - b_end)
        half_decay = worst if half_decay is None else jnp.maximum(half_decay, worst)

    @pl.when(jnp.logical_not(jnp.max(half_decay) < HG_MAX_EXPONENT))
    def _():
        st[...] = st0[...]

        def window(gi, carry):
            rows = pl.ds(pl.multiple_of(gi * SUBLANES, SUBLANES), SUBLANES)
            f8 = lb + (1.0 - lb) * _sigmoid(proj_ref[rows, C_F:C_F + w])
            b8 = _tile_cumsum(jnp.log(f8))
            k8 = 1.0 - f8
            q8 = proj_ref[rows, C_Q:C_Q + w]
            v8 = proj_ref[rows, C_V:C_V + w]
            for hd in range(HG_HEADS):
                cs = slice(hd * HG_HEAD_DIM, (hd + 1) * HG_HEAD_DIM)
                o, s_new = _hgrn_window(q8[:, cs], k8[:, cs], v8[:, cs], b8[:, cs], st[hd], SUBLANES, True)
                st[hd] = s_new
                obuf[rows, cs] = o
            return carry
        lax.fori_loop(0, tc // SUBLANES, window, 0)

    hgn = hgn_ref[...]
    for hd in range(HG_HEADS):
        cs = slice(hd * HG_HEAD_DIM, (hd + 1) * HG_HEAD_DIM)
        gate = proj_ref[:, C_G + hd * HG_HEAD_DIM:C_G + (hd + 1) * HG_HEAD_DIM]
        m_ref[:, w + hd * HG_HEAD_DIM:w + (hd + 1) * HG_HEAD_DIM] = (
            _head_norm_gate(obuf[:, cs], gate, hgn).astype(BF16))


def _mixer_prompt_kernel(x_ref, xn_ref, sc_ref, sh_ref, scn_ref, shn_ref, g_ref, win_ref, cw_ref, cb_ref, wg_ref,
                         bg_ref, lam_ref, lb_ref, hgn_ref, m_ref, conv_ref, h_ref, s_ref,
                         proj_a, proj_b, hn_buf, xprev, hcar, st, st0, obuf, *, tc):
    j = pl.program_id(1)
    n_phases = 2 + tc // HG_CHUNK
    cols = IN_COLS // n_phases

    def project(x, sc, sh, out_ref):
        def chunk(c):
            if c == 0:
                hn_buf[...] = (_rms(x(), g_ref[...]) * (1.0 + sc[...]) + sh[...]).astype(BF16)
            out_ref[:, c * cols:(c + 1) * cols] = jnp.dot(hn_buf[...], win_ref[:, c * cols:(c + 1) * cols],
                                                          preferred_element_type=F32)
        return [functools.partial(chunk, c) for c in range(n_phases)]

    def mix(proj_ref, first_row, seq_start, side_work):
        _mix_tile(proj_ref, cw_ref, cb_ref, wg_ref, bg_ref, lam_ref, lb_ref, hgn_ref,
                  m_ref.at[pl.ds(first_row, tc), :], xprev, hcar, st, st0, obuf, seq_start, tc, side_work)

    @pl.when((pl.program_id(0) == 0) & (j == 0))
    def _():
        for thunk in project(lambda: x_ref[0:tc, :], sc_ref, sh_ref, proj_a):
            thunk()

    @pl.when(j == 0)
    def _():
        xprev[...] = jnp.zeros_like(xprev)
        hcar[...] = jnp.zeros_like(hcar)
        st[...] = jnp.zeros_like(st)

    mix(proj_a, 0, j == 0, project(lambda: x_ref[tc:2 * tc, :], sc_ref, sh_ref, proj_b))
    mix(proj_b, tc, None, project(lambda: xn_ref[...], scn_ref, shn_ref, proj_a))

    @pl.when(j == pl.num_programs(1) - 1)
    def _():
        conv_ref[...] = xprev[SUBLANES - (CONV_WIDTH - 1):SUBLANES, :]
        h_ref[...] = hcar[0:1, :]
        for hd in range(HG_HEADS):
            s_ref[hd] = st[hd].T


def _mixer_prompt(x, sc, sh, g, w_in_bf, nb, seq, mw, tc):
    cw, cb, wg, bg, lam, lb, hgn = mw
    nj = seq // (2 * tc)
    last_tile = nb * seq // tc - 1
    next_tile = lambda b, j: jnp.minimum(2 * (b * nj + j) + 2, last_tile)
    const = lambda shape: pl.BlockSpec(shape, lambda b, j: tuple(0 for _ in shape))
    per_seq = lambda: pl.BlockSpec((None, 1, D_MODEL), lambda b, j: (b, 0, 0))
    next_seq = lambda: pl.BlockSpec((None, 1, D_MODEL), lambda b, j: (next_tile(b, j) // (2 * nj), 0, 0))
    return pl.pallas_call(
        functools.partial(_mixer_prompt_kernel, tc=tc),
        out_shape=(jax.ShapeDtypeStruct((nb * seq, D_MODEL), BF16),
                   jax.ShapeDtypeStruct((nb, CONV_WIDTH - 1, LRU_WIDTH), F32),
                   jax.ShapeDtypeStruct((nb, 1, LRU_WIDTH), F32),
                   jax.ShapeDtypeStruct((nb, HG_HEADS, HG_HEAD_DIM, HG_HEAD_DIM), F32)),
        grid=(nb, nj),
        in_specs=[pl.BlockSpec((2 * tc, D_MODEL), lambda b, j: (b * nj + j, 0)),
                  pl.BlockSpec((tc, D_MODEL), lambda b, j: (next_tile(b, j), 0)),
                  per_seq(), per_seq(), next_seq(), next_seq(), const((1, D_MODEL)), const((D_MODEL, IN_COLS)),
                  const((CONV_WIDTH, LRU_WIDTH)), const((1, LRU_WIDTH)),
                  const((LRU_WIDTH, 2 * LRU_WIDTH)), const((1, 2 * LRU_WIDTH)),
                  const((1, LRU_WIDTH)), const((1, HG_WIDTH)), const((1, HG_HEAD_DIM))],
        out_specs=(pl.BlockSpec((2 * tc, D_MODEL), lambda b, j: (b * nj + j, 0)),
                   pl.BlockSpec((None, CONV_WIDTH - 1, LRU_WIDTH), lambda b, j: (b, 0, 0)),
                   pl.BlockSpec((None, 1, LRU_WIDTH), lambda b, j: (b, 0, 0)),
                   pl.BlockSpec((None, HG_HEADS, HG_HEAD_DIM, HG_HEAD_DIM), lambda b, j: (b, 0, 0, 0))),
        scratch_shapes=[pltpu.VMEM((tc, IN_COLS), F32),
                        pltpu.VMEM((tc, IN_COLS), F32),
                        pltpu.VMEM((tc, D_MODEL), BF16),
                        pltpu.VMEM((SUBLANES, LRU_WIDTH), F32),
                        pltpu.VMEM((SUBLANES, LRU_WIDTH), F32),
                        pltpu.VMEM((HG_HEADS, HG_HEAD_DIM, HG_HEAD_DIM), F32),
                        pltpu.VMEM((HG_HEADS, HG_HEAD_DIM, HG_HEAD_DIM), F32),
                        pltpu.VMEM((tc, HG_WIDTH), F32)],
        compiler_params=_cparams(("arbitrary", "arbitrary")),
    )(x, x, sc, sh, sc, sh, g, w_in_bf, cw, cb, wg, bg, lam, lb, hgn)


def _mixer_sample_kernel(proj_ref, cst_ref, h0_ref, s0_ref, cw_ref, cb_ref, wg_ref, bg_ref, lam_ref, lb_ref,
                         hgn_ref, m_ref, conv_ref, h_ref, s_ref, pbuf, xbuf, *, steps):
    for bi in range(pbuf.shape[0]):
        seq_rows = slice(bi * steps, (bi + 1) * steps)
        pbuf[bi, 0:steps, :] = proj_ref[seq_rows, :]
        pbuf[bi, steps:SUBLANES, :] = jnp.zeros((SUBLANES - steps, IN_COLS), F32)
        _mixer_sample_one(pbuf.at[bi], cst_ref.at[bi], h0_ref.at[bi], s0_ref.at[bi], cw_ref, cb_ref, wg_ref,
                          bg_ref, lam_ref, lb_ref, hgn_ref, m_ref, seq_rows, conv_ref.at[bi], h_ref.at[bi],
                          s_ref.at[bi], xbuf.at[bi], steps)


def _mixer_sample_one(proj_ref, cst_ref, h0_ref, s0_ref, cw_ref, cb_ref, wg_ref, bg_ref, lam_ref, lb_ref,
                      hgn_ref, m_ref, m_rows, conv_ref, h_ref, s_ref, xbuf, steps):
    w = LRU_WIDTH
    rows = SUBLANES
    row = lax.broadcasted_iota(I32, (rows, w), 0)

    xl = proj_ref[:, C_XL:C_XL + w]
    xbuf[0:CONV_HDR, :] = cst_ref[...]
    xbuf[CONV_HDR:CONV_HDR + rows, :] = xl
    xc = _conv4(xbuf, xl, cw_ref, cb_ref, rows)
    conv_ref[...] = xbuf[CONV_HDR + steps - 3:CONV_HDR + steps, :]
    a, mult, ig = _lru_gates(xc, wg_ref, bg_ref, lam_ref)
    u = mult * ig * xc
    h = h0_ref[...]
    hs = jnp.zeros((rows, w), F32)
    for t in range(steps):
        h = a[t:t + 1, :] * h + u[t:t + 1, :]
        hs = jnp.where(row == t, h, hs)
    h_ref[...] = h
    m_ref[m_rows, 0:w] = (hs * _gelu_tanh(proj_ref[:, C_YL:C_YL + w]))[0:steps, :]

    lb = lb_ref[...]
    f = lb + (1.0 - lb) * _sigmoid(proj_ref[:, C_F:C_F + w])
    k = 1.0 - f
    b = _tile_cumsum(jnp.log(f))
    q = proj_ref[:, C_Q:C_Q + w]
    v = proj_ref[:, C_V:C_V + w]
    hgn = hgn_ref[...]
    for hd in range(HG_HEADS):
        cs = slice(hd * HG_HEAD_DIM, (hd + 1) * HG_HEAD_DIM)
        o, s_new = _hgrn_window(q[:, cs], k[:, cs], v[:, cs], b[:, cs], s0_ref[hd], steps, False)
        s_ref[hd] = s_new
        gate = proj_ref[:, C_G + hd * HG_HEAD_DIM:C_G + (hd + 1) * HG_HEAD_DIM]
        m_ref[m_rows, w + hd * HG_HEAD_DIM:w + (hd + 1) * HG_HEAD_DIM] = _head_norm_gate(o, gate, hgn)[0:steps, :]


def _mixer_sample(proj, cst_hdr, h0, s0, mw, steps):
    cw, cb, wg, bg, lam, lb, hgn = mw
    nb = proj.shape[0] // steps
    per_step = _pick_tile(nb, (8, 4, 2, 1))
    const = lambda shape: pl.BlockSpec(shape, lambda b: tuple(0 for _ in shape))
    per_b = lambda shape: pl.BlockSpec((per_step,) + shape, lambda b: (b,) + tuple(0 for _ in shape))
    return pl.pallas_call(
        functools.partial(_mixer_sample_kernel, steps=steps),
        out_shape=(jax.ShapeDtypeStruct((nb * steps, D_MODEL), F32),
                   jax.ShapeDtypeStruct((nb, CONV_WIDTH - 1, LRU_WIDTH), F32),
                   jax.ShapeDtypeStruct((nb, 1, LRU_WIDTH), F32),
                   jax.ShapeDtypeStruct((nb, HG_HEADS, HG_HEAD_DIM, HG_HEAD_DIM), F32)),
        grid=(nb // per_step,),
        in_specs=[pl.BlockSpec((per_step * steps, IN_COLS), lambda b: (b, 0)),
                  per_b((CONV_HDR, LRU_WIDTH)), per_b((1, LRU_WIDTH)),
                  per_b((HG_HEADS, HG_HEAD_DIM, HG_HEAD_DIM)),
                  const((CONV_WIDTH, LRU_WIDTH)), const((1, LRU_WIDTH)),
                  const((LRU_WIDTH, 2 * LRU_WIDTH)), const((1, 2 * LRU_WIDTH)),
                  const((1, LRU_WIDTH)), const((1, HG_WIDTH)), const((1, HG_HEAD_DIM))],
        out_specs=(pl.BlockSpec((per_step * steps, D_MODEL), lambda b: (b, 0)),
                   per_b((CONV_WIDTH - 1, LRU_WIDTH)), per_b((1, LRU_WIDTH)),
                   per_b((HG_HEADS, HG_HEAD_DIM, HG_HEAD_DIM))),
        scratch_shapes=[pltpu.VMEM((per_step, SUBLANES, IN_COLS), F32),
                        pltpu.VMEM((per_step, CONV_HDR + SUBLANES, LRU_WIDTH), F32)],
        compiler_params=_cparams(("arbitrary",)),
    )(proj, cst_hdr, h0, s0, cw, cb, wg, bg, lam, lb, hgn)


def _to_row_tiles(ref, val, rows):
    for c in range(LANE_CHUNKS):
        ref[pl.ds(c, rows, stride=LANE_CHUNKS), :] = val[:, c * LANES:(c + 1) * LANES]


def _from_row_tiles(ref, base, rows):
    return jnp.concatenate(
        [ref[pl.ds(base + c, rows, stride=LANE_CHUNKS), :] for c in range(LANE_CHUNKS)], axis=-1)


def _post_kernel(mp_ref, ms_ref, xp_ref, xs_ref, gtmp_ref, scfp_ref, shfp_ref, gtms_ref, scfs_ref, shfs_ref,
                 npost_ref, npre_ref, wout_ref, rwt_ref, rb_ref,
                 x1_ref, hf_ref, idx_ref, gate_ref, rank_ref, cnt_out_ref, cnt, *, n_p):
    i = pl.program_id(0)
    tm = xp_ref.shape[0]

    @pl.when(i == 0)
    def _():
        cnt[...] = jnp.zeros_like(cnt)

    def run(m_ref, x_ref, gtm_ref, scf_ref, shf_ref):
        mix = jnp.dot(m_ref[...].astype(BF16), wout_ref[...], preferred_element_type=F32)
        x1 = x_ref[...] + gtm_ref[...] * _rms(mix, npost_ref[...])
        x1_ref[...] = x1
        hf = _rms(x1, npre_ref[...]) * (1.0 + scf_ref[...]) + shf_ref[...]
        _to_row_tiles(hf_ref, hf, tm)
        logits = lax.dot_general(rwt_ref[...], hf, _NT, precision=lax.Precision.HIGHEST,
                                 preferred_element_type=F32) + rb_ref[...]
        e_iota = lax.broadcasted_iota(I32, logits.shape, 0)
        vals, idxs = [], []
        for _ in range(TOP_K):
            mx = jnp.max(logits, axis=0, keepdims=True)
            ix = jnp.min(jnp.where(logits == mx, e_iota, N_EXPERTS), axis=0, keepdims=True)
            vals.append(mx)
            idxs.append(ix)
            logits = jnp.where(e_iota == ix, -jnp.inf, logits)
        exps = [jnp.exp(vv - vals[0]) for vv in vals]
        den = exps[0] + exps[1] + exps[2] + exps[3]
        earlier = (lax.broadcasted_iota(I32, (tm, tm), 0) < lax.broadcasted_iota(I32, (tm, tm), 1))
        earlier = jnp.where(earlier, 1.0, 0.0).astype(BF16)
        base = cnt[...]
        for kk in range(TOP_K):
            idx_ref[kk:kk + 1, :] = idxs[kk]
            gate_ref[kk:kk + 1, :] = exps[kk] / den
            hit = e_iota == idxs[kk]
            onehot = jnp.where(hit, 1.0, 0.0)
            before = jnp.dot(onehot.astype(BF16), earlier, preferred_element_type=F32)
            rank = jnp.sum(jnp.where(hit, before + base, 0.0), axis=0, keepdims=True)
            rank_ref[kk:kk + 1, :] = rank.astype(I32)
            base = base + jnp.sum(onehot, axis=1, keepdims=True)
        cnt[...] = base

    @pl.when(i < n_p)
    def _():
        run(mp_ref, xp_ref, gtmp_ref, scfp_ref, shfp_ref)

    @pl.when(i >= n_p)
    def _():
        run(ms_ref, xs_ref, gtms_ref, scfs_ref, shfs_ref)

    @pl.when(i == pl.num_programs(0) - 1)
    def _():
        cnt_out_ref[...] = cnt[...]


def _post(m_p, m_s, x_p, x_s, mod_p, mod_s, npost, npre, wout_bf, rwt, rb, tm, tiles_per_seq):
    n_p, n_s = x_p.shape[0] // tm, x_s.shape[0] // tm
    t = (n_p + n_s) * tm
    first = lambda i: jnp.minimum(i, n_p - 1)
    second = lambda i: jnp.maximum(i - n_p, 0)
    row = lambda: pl.BlockSpec((1, D_MODEL), lambda i: (0, 0))
    tok_p = lambda: pl.BlockSpec((tm, D_MODEL), lambda i: (first(i), 0))
    tok_s = lambda: pl.BlockSpec((tm, D_MODEL), lambda i: (second(i), 0))
    seq_p = lambda: pl.BlockSpec((None, 1, D_MODEL), lambda i: (first(i) // tiles_per_seq, 0, 0))
    per_k = lambda: pl.BlockSpec((TOP_K, tm), lambda i: (0, i))
    return pl.pallas_call(
        functools.partial(_post_kernel, n_p=n_p),
        out_shape=(jax.ShapeDtypeStruct((t, D_MODEL), F32),
                   jax.ShapeDtypeStruct((t * LANE_CHUNKS, LANES), F32),
                   jax.ShapeDtypeStruct((TOP_K, t), I32), jax.ShapeDtypeStruct((TOP_K, t), F32),
                   jax.ShapeDtypeStruct((TOP_K, t), I32), jax.ShapeDtypeStruct((N_EXPERTS, 1), F32)),
        grid=(n_p + n_s,),
        in_specs=[tok_p(), tok_s(), tok_p(), tok_s(), seq_p(), seq_p(), seq_p(), tok_s(), tok_s(), tok_s(),
                  row(), row(),
                  pl.BlockSpec((D_MODEL, D_MODEL), lambda i: (0, 0)),
                  pl.BlockSpec((N_EXPERTS, D_MODEL), lambda i: (0, 0)),
                  pl.BlockSpec((N_EXPERTS, 1), lambda i: (0, 0))],
        out_specs=(pl.BlockSpec((tm, D_MODEL), lambda i: (i, 0)),
                   pl.BlockSpec((tm * LANE_CHUNKS, LANES), lambda i: (i, 0)),
                   per_k(), per_k(), per_k(), pl.BlockSpec((N_EXPERTS, 1), lambda i: (0, 0))),
        scratch_shapes=[pltpu.VMEM((N_EXPERTS, 1), F32)],
        compiler_params=_cparams(("arbitrary",)),
    )(m_p, m_s, x_p, x_s, *mod_p, *mod_s, npost, npre, wout_bf, rwt, rb)


def _dispatch_kernel(fill_lo_ref, fill_hi_ref, dest_ref, hf_hbm, xs_hbm, zeros, src, fill_counts, tile_sem, row_sem,
                     pad_sem, *, tm):
    i = pl.program_id(0)
    n_rows = TOP_K * tm

    @pl.when(i == 0)
    def _():
        zeros[...] = jnp.zeros_like(zeros)

        def per_expert(e, totals):
            lo, hi = fill_lo_ref[e], fill_hi_ref[e]
            mid = jnp.minimum((lo + ZERO_ROWS - 1) // ZERO_ROWS * ZERO_ROWS, hi)

            def per_row(s, carry):
                pltpu.make_async_copy(zeros.at[0], xs_hbm.at[s], pad_sem).start()
                return carry
            lax.fori_loop(lo, mid, per_row, 0)

            def per_chunk(c, carry):
                pltpu.make_async_copy(zeros, xs_hbm.at[pl.ds(pl.multiple_of(c * ZERO_ROWS, ZERO_ROWS), ZERO_ROWS)],
                                      pad_sem).start()
                return carry
            lax.fori_loop(mid // ZERO_ROWS, hi // ZERO_ROWS, per_chunk, 0)
            return totals[0] + (mid - lo), totals[1] + (hi - mid) // ZERO_ROWS
        n_rows_filled, n_chunks_filled = lax.fori_loop(0, N_EXPERTS, per_expert, (0, 0))
        fill_counts[0] = n_rows_filled
        fill_counts[1] = n_chunks_filled

    @pl.when(i == pl.num_programs(0) - 1)
    def _():
        def drain_row(s, carry):
            pltpu.make_async_copy(zeros.at[0], xs_hbm.at[0], pad_sem).wait()
            return carry
        lax.fori_loop(0, fill_counts[0], drain_row, 0)

        def drain_chunk(c, carry):
            pltpu.make_async_copy(zeros, xs_hbm.at[pl.ds(0, ZERO_ROWS)], pad_sem).wait()
            return carry
        lax.fori_loop(0, fill_counts[1], drain_chunk, 0)

    n_tiles = pl.num_programs(0)
    tile_len = tm * LANE_CHUNKS

    def tile_copy(t):
        s = lax.rem(t, DISPATCH_BUFS)
        return pltpu.make_async_copy(hf_hbm.at[pl.ds(pl.multiple_of(t * tile_len, tile_len), tile_len), :],
                                     src.at[s], tile_sem.at[s])

    def wait_rows(t):
        pltpu.make_async_copy(xs_hbm.at[pl.ds(0, n_rows)], xs_hbm.at[pl.ds(0, n_rows)],
                              row_sem.at[lax.rem(t, DISPATCH_BUFS)]).wait()

    @pl.when(i == 0)
    def _():
        tile_copy(0).start()

    @pl.when(i + 1 < n_tiles)
    def _():
        tile_copy(i + 1).start()

    tile_copy(i).wait()
    slot = lax.rem(i, DISPATCH_BUFS)
    tile = src.at[slot]
    for n in range(n_rows):
        pltpu.make_async_copy(tile.at[pl.ds((n % tm) * LANE_CHUNKS, LANE_CHUNKS), :],
                              xs_hbm.at[dest_ref[0, n]], row_sem.at[slot]).start(priority=n % 2)

    @pl.when(i > 0)
    def _():
        wait_rows(i - 1)

    @pl.when(i == n_tiles - 1)
    def _():
        wait_rows(i)


def _dispatch(fill_lo, fill_hi, dest_tiles, hf_rows, n_slots, tm):
    return pl.pallas_call(
        functools.partial(_dispatch_kernel, tm=tm),
        out_shape=jax.ShapeDtypeStruct((n_slots, LANE_CHUNKS, LANES), F32),
        grid_spec=pltpu.PrefetchScalarGridSpec(
            num_scalar_prefetch=2, grid=(dest_tiles.shape[0],),
            in_specs=[pl.BlockSpec((None, 1, TOP_K * tm), lambda i, lo, hi: (i, 0, 0), memory_space=pltpu.SMEM),
                      pl.BlockSpec(memory_space=pl.ANY)],
            out_specs=pl.BlockSpec(memory_space=pl.ANY),
            scratch_shapes=[pltpu.VMEM((ZERO_ROWS, LANE_CHUNKS, LANES), F32),
                            pltpu.VMEM((DISPATCH_BUFS, tm * LANE_CHUNKS, LANES), F32),
                            pltpu.SMEM((2,), I32),
                            pltpu.SemaphoreType.DMA((DISPATCH_BUFS,)),
                            pltpu.SemaphoreType.DMA((DISPATCH_BUFS,)),
                            pltpu.SemaphoreType.DMA]),
        compiler_params=_cparams(("arbitrary",)),
    )(fill_lo, fill_hi, dest_tiles, hf_rows)


def _expert_kernel(be_ref, half_ref, nxt_ref, nu_ref, xs_ref, wgu_hbm, bgu_ref, wd_hbm, bd_ref, ys_ref,
                   wgu_f32, wd_f32, wgu_bf, wd_bf, w_sem):
    i = pl.program_id(0)
    n_used = nu_ref[0]

    def weight_copies(e):
        return (pltpu.make_async_copy(wgu_hbm.at[e], wgu_f32, w_sem.at[0]),
                pltpu.make_async_copy(wd_hbm.at[e], wd_f32, w_sem.at[1]))

    @pl.when(i == 0)
    def _():
        for cp in weight_copies(be_ref[0]):
            cp.start()

    @pl.when(i < n_used)
    def _():
        prev = be_ref[jnp.maximum(i - 1, 0)]

        @pl.when((i == 0) | (be_ref[i] != prev))
        def _():
            for cp in weight_copies(be_ref[i]):
                cp.wait()
            wgu_bf[...] = wgu_f32[...].astype(BF16)
            wd_bf[...] = wd_f32[...].astype(BF16)
            nxt = nxt_ref[be_ref[i]]

            @pl.when(nxt >= 0)
            def _():
                for cp in weight_copies(nxt):
                    cp.start()

        def swiglu_rows(rows):
            x = _from_row_tiles(xs_ref, 0, rows).astype(BF16)
            hu = jnp.dot(x, wgu_bf[...], preferred_element_type=F32) + bgu_ref[...]
            glu = jnp.minimum(hu[:, :D_EXPERT], SWIGLU_LIMIT)
            lin = jnp.clip(hu[:, D_EXPERT:], -SWIGLU_LIMIT, SWIGLU_LIMIT)
            act = glu * _sigmoid(SWIGLU_ALPHA * glu) * (lin + 1.0)
            y = jnp.dot(act.astype(BF16), wd_bf[...], preferred_element_type=F32) + bd_ref[...]
            _to_row_tiles(ys_ref, y, rows)

        @pl.when(half_ref[i] == 0)
        def _():
            swiglu_rows(MOE_BM)

        @pl.when(half_ref[i] != 0)
        def _():
            swiglu_rows(MOE_BM // 2)
            ys_ref[MOE_BM // 2 * LANE_CHUNKS:, :] = jnp.zeros((MOE_BM // 2 * LANE_CHUNKS, LANES), F32)

    @pl.when(i >= n_used)
    def _():
        ys_ref[...] = jnp.zeros_like(ys_ref)


def _experts(block_e, block_half, next_e, n_used, xs2d, w_gu, b_gu, w_down, b_down):
    n_blocks = block_e.shape[0]
    rows = MOE_BM * LANE_CHUNKS
    live = lambda i, be, hf, nx, nu: (jnp.minimum(i, nu[0] - 1), 0)
    return pl.pallas_call(
        _expert_kernel,
        out_shape=jax.ShapeDtypeStruct(xs2d.shape, F32),
        grid_spec=pltpu.PrefetchScalarGridSpec(
            num_scalar_prefetch=4,
            grid=(n_blocks,),
            in_specs=[pl.BlockSpec((rows, LANES), live),
                      pl.BlockSpec(memory_space=pl.ANY),
                      pl.BlockSpec((None, 1, 2 * D_EXPERT), lambda i, be, hf, nx, nu: (be[i], 0, 0)),
                      pl.BlockSpec(memory_space=pl.ANY),
                      pl.BlockSpec((None, 1, D_MODEL), lambda i, be, hf, nx, nu: (be[i], 0, 0))],
            out_specs=pl.BlockSpec((rows, LANES), lambda i, be, hf, nx, nu: (i, 0)),
            scratch_shapes=[pltpu.VMEM((D_MODEL, 2 * D_EXPERT), F32),
                            pltpu.VMEM((D_EXPERT, D_MODEL), F32),
                            pltpu.VMEM((D_MODEL, 2 * D_EXPERT), BF16),
                            pltpu.VMEM((D_EXPERT, D_MODEL), BF16),
                            pltpu.SemaphoreType.DMA((2,))]),
        compiler_params=_cparams(("arbitrary",)),
    )(block_e, block_half, next_e, n_used, xs2d, w_gu, b_gu, w_down, b_down)


def _combine_kernel(dcur_ref, dnext_ref, ys_hbm, g_ref, x1_ref, gt_ref, np_ref, o_ref, ybuf, row_sem, *, tm):
    i = pl.program_id(0)
    n_rows = TOP_K * tm
    slot = i % 2

    def slot_rows(s):
        return ybuf.at[pl.ds(pl.multiple_of(s * n_rows * LANE_CHUNKS, n_rows * LANE_CHUNKS), n_rows * LANE_CHUNKS), :]

    def issue(d_ref, s):
        dst = slot_rows(s)
        for n in range(n_rows):
            pltpu.make_async_copy(ys_hbm.at[d_ref[0, n]], dst.at[pl.ds(n * LANE_CHUNKS, LANE_CHUNKS), :],
                                  row_sem.at[s]).start(priority=n % 2)

    @pl.when(i == 0)
    def _():
        issue(dcur_ref, 0)

    @pl.when(i + 1 < pl.num_programs(0))
    def _():
        issue(dnext_ref, 1 - slot)

    pltpu.make_async_copy(slot_rows(slot), slot_rows(slot), row_sem.at[slot]).wait()
    g = g_ref[...]
    y = None
    for kk in range(TOP_K):
        rows = _from_row_tiles(ybuf, (slot * n_rows + kk * tm) * LANE_CHUNKS, tm)
        y = g[:, kk:kk + 1] * rows if y is None else y + g[:, kk:kk + 1] * rows
    o_ref[...] = x1_ref[...] + gt_ref[...] * _rms(y, np_ref[...])


def _combine(dest_tiles, ys_tiles, gates_t, x1, gt, npost, tm, tiles_per_seq, tile0):
    n_tiles = dest_tiles.shape[0]
    t = n_tiles * tm
    smem_tile = lambda f: pl.BlockSpec((None, 1, TOP_K * tm), f, memory_space=pltpu.SMEM)
    return pl.pallas_call(
        functools.partial(_combine_kernel, tm=tm),
        out_shape=jax.ShapeDtypeStruct((t, D_MODEL), F32),
        grid=(n_tiles,),
        in_specs=[smem_tile(lambda i: (i, 0, 0)),
                  smem_tile(lambda i: (jnp.minimum(i + 1, n_tiles - 1), 0, 0)),
                  pl.BlockSpec(memory_space=pl.ANY),
                  pl.BlockSpec((tm, TOP_K), lambda i: (i + tile0, 0)),
                  pl.BlockSpec((tm, D_MODEL), lambda i: (i + tile0, 0)),
                  _mod_spec(gt, tm, tiles_per_seq),
                  pl.BlockSpec((1, D_MODEL), lambda i: (0, 0))],
        out_specs=pl.BlockSpec((tm, D_MODEL), lambda i: (i, 0)),
        scratch_shapes=[pltpu.VMEM((2 * TOP_K * tm * LANE_CHUNKS, LANES), F32),
                        pltpu.SemaphoreType.DMA((2,))],
        compiler_params=_cparams(("arbitrary",)),
    )(dest_tiles, dest_tiles, ys_tiles, gates_t, x1, gt, npost)


def _expert_layout(counts, n_blocks):
    padded = (counts + MOE_BM - 1) // MOE_BM * MOE_BM
    pad_ends = jnp.cumsum(padded)
    pad_starts = pad_ends - padded
    block_start = jnp.arange(n_blocks, dtype=I32) * MOE_BM
    block_e = jnp.minimum(jnp.sum((pad_ends[None, :] <= block_start[:, None]).astype(I32), axis=1), N_EXPERTS - 1)
    n_used = (pad_ends[-1:] // MOE_BM).astype(I32)
    e_ids = jnp.arange(N_EXPERTS, dtype=I32)
    rows_end = jnp.sum(jnp.where(block_e[:, None] == e_ids[None, :], (pad_starts + counts)[None, :], 0), axis=1)
    block_half = (rows_end - block_start <= MOE_BM // 2).astype(I32)
    later = (counts > 0)[None, :] & (e_ids[None, :] > e_ids[:, None])
    next_e = jnp.min(jnp.where(later, e_ids[None, :], N_EXPERTS), axis=1)
    next_e = jnp.where(next_e < N_EXPERTS, next_e, -1).astype(I32)
    return pad_starts, pad_starts + counts, pad_ends, block_e, block_half, next_e, n_used


def _slots(idx, rank, pad_starts):
    onehot = idx[:, :, None] == jnp.arange(N_EXPERTS, dtype=I32)[None, None, :]
    return rank + jnp.sum(jnp.where(onehot, pad_starts[None, None, :], 0), axis=-1)


def _tile_major(a, tm):
    t = a.shape[1]
    return a.reshape(TOP_K, t // tm, tm).transpose(1, 0, 2).reshape(t // tm, 1, TOP_K * tm)


def _block_diag(wb):
    n, d, _ = wb.shape
    eye = jnp.eye(n, dtype=wb.dtype)
    return (wb[:, :, None, :] * eye[:, None, :, None]).reshape(n * d, n * d)


def _pick_tile(n, prefs):
    for p in prefs:
        if n % p == 0:
            return p
    raise ValueError(f"no tile for {n}")


def kernel(x_prompt, x_sample, state_rglru_conv, state_rglru_h, state_hgrn_S, c_prompt, c_sample, ada_w, ada_b, norm_pre_mix, norm_post_mix, norm_pre_ffn, norm_post_ffn, w_in, conv_w, conv_b, lru_wa, lru_ba, lru_wi, lru_bi, lru_lambda, hg_lb, hg_norm, w_out, router_w, router_b, w_gu, b_gu, w_down, b_down):
    assert ada_w.shape[0] == 1, "single-layer trunk"
    bp, lp, _ = x_prompt.shape
    bs, ls, _ = x_sample.shape
    tp, ts = bp * lp, bs * ls
    assert CONV_WIDTH - 1 <= ls <= SUBLANES

    w_in_bf = w_in[0].astype(BF16)
    w_out_bf = w_out[0].astype(BF16)
    wg = jnp.concatenate([_block_diag(lru_wa[0]), _block_diag(lru_wi[0])], axis=1).astype(BF16)
    bg = jnp.concatenate([lru_ba[0], lru_bi[0]])[None, :]
    lbs = jnp.cumsum(jax.nn.softmax(hg_lb.astype(F32), axis=0), axis=0)[0][None, :]
    mw = (conv_w[0], conv_b[0][None, :], wg, bg, lru_lambda[0][None, :], lbs, hg_norm[0][None, :])
    row = lambda p: p[0][None, :]

    n_c = bp + bs
    n_c_pad = -(-n_c // SUBLANES) * SUBLANES
    c_all = jnp.concatenate([c_prompt, c_sample, jnp.zeros((n_c_pad - n_c, D_MODEL), F32)], axis=0)
    mod = _ada(c_all, ada_w[0], ada_b[0][None, :])
    mod_p = [m[:, None, :] for m in jnp.split(mod[:bp], 6, axis=-1)]
    mod_s = [jnp.repeat(m, ls, axis=0) for m in jnp.split(mod[bp:n_c], 6, axis=-1)]

    xp = x_prompt.reshape(tp, D_MODEL)
    xs = x_sample.reshape(ts, D_MODEL)
    tm_s = _pick_tile(ts, (256, 128, 64, 32, 16, 8))

    tc = _pick_tile(lp // 2, (256, 128, 64))
    m_p, conv_p, h_p, s_p = _mixer_prompt(xp, mod_p[1], mod_p[0], row(norm_pre_mix), w_in_bf, bp, lp, mw, tc)
    proj_s = _inproj(xs, mod_s[1], mod_s[0], row(norm_pre_mix), w_in_bf, tm_s, 1)

    cst_hdr = jnp.pad(state_rglru_conv[0], ((0, 0), (CONV_HDR - (CONV_WIDTH - 1), 0), (0, 0)))
    m_s, conv_s, h_s, s_s = _mixer_sample(proj_s, cst_hdr, state_rglru_h[0][:, None, :], state_hgrn_S[0], mw, ls)

    tmq = _pick_tile(math.gcd(lp, ts), (512, 256, 128, 64, 32, 16, 8))
    x1, hf, idx, gate, rank, counts = _post(
        m_p, m_s, xp, xs, (mod_p[2], mod_p[4], mod_p[3]), (mod_s[2], mod_s[4], mod_s[3]),
        row(norm_post_mix), row(norm_pre_ffn), w_out_bf, router_w[0].T, router_b[0][:, None], tmq, lp // tmq)

    n_blocks = -(-(TOP_K * (tp + ts)) // MOE_BM) + N_EXPERTS
    n_slots = n_blocks * MOE_BM
    pad_starts, fill_lo, pad_ends, block_e, block_half, next_e, n_used = _expert_layout(
        counts[:, 0].astype(I32), n_blocks)
    fill_hi = pad_ends.at[N_EXPERTS - 1].set(n_slots)
    dest = _slots(idx, rank, pad_starts)
    tmd = _pick_tile(math.gcd(lp, ts), (512, 256, 128, 64, 32, 16, 8))
    tmc = _pick_tile(math.gcd(lp, ts), (256, 128, 64, 32, 16, 8))
    xs_rows = _dispatch(fill_lo, fill_hi, _tile_major(dest, tmd), hf, n_slots, tmd)
    ys = _experts(block_e, block_half, next_e, n_used, xs_rows.reshape(n_slots * LANE_CHUNKS, LANES),
                  w_gu[0], b_gu[0][:, None, :], w_down[0], b_down[0][:, None, :])
    ys_tiles = ys.reshape(-1, LANE_CHUNKS, LANES)

    gates_t = gate.T
    ctiles = _tile_major(dest, tmc)
    y_p = _combine(ctiles[:tp // tmc], ys_tiles, gates_t, x1, mod_p[5], row(norm_post_ffn), tmc, lp // tmc, 0)
    y_s = _combine(ctiles[tp // tmc:], ys_tiles, gates_t, x1, mod_s[5], row(norm_post_ffn), tmc, 1, tp // tmc)

    return (y_p.reshape(bp, lp, D_MODEL), y_s.reshape(bs, ls, D_MODEL),
            conv_p[None], h_p.reshape(1, bp, LRU_WIDTH), s_p[None],
            conv_s[None], h_s.reshape(1, bs, LRU_WIDTH), s_s[None])
```
